```python
import math
import jax
import jax.numpy as jnp
from jax import lax
import numpy as np

D_MODEL = 1024
BATCH = 16
SEQ = 4096
DEPTH = 1

CHUNK = 64
Q_BLOCK = 128
ATT_HEADS = 8
HEAD_DIM = 64
ATT_WIDTH = ATT_HEADS * HEAD_DIM
LRU_WIDTH = D_MODEL - ATT_WIDTH
LRU_BLOCKS = 8
LRU_BLOCK_DIM = LRU_WIDTH // LRU_BLOCKS
CONV_WIDTH = 4
LRU_C = 8.0
D_FF = -(-8 * D_MODEL // (3 * 256)) * 256
IN_WIDTH = 3 * ATT_WIDTH + ATT_HEADS + 2 * LRU_WIDTH
NORM_EPS = 1e-6

kernel_name = "hymba_fox_rglru_swiglu_layer"


def rmsnorm(x, g):
    xf = x.astype(jnp.float32)
    y = xf * lax.rsqrt(jnp.mean(xf * xf, axis=-1, keepdims=True) + NORM_EPS)
    return (y * g.astype(jnp.float32)).astype(x.dtype)


def forgetting_attention(q, k, v, cum_logf):
    seq = q.shape[2]
    scale = 1.0 / math.sqrt(HEAD_DIM)
    outs = []
    for start in range(0, seq, Q_BLOCK):
        end = start + Q_BLOCK
        qb = q[:, :, start:end]
        kb = k[:, :, :end]
        vb = v[:, :, :end]
        s = jnp.einsum('bhqd,bhkd->bhqk', qb, kb).astype(jnp.float32) * scale
        s = s + (cum_logf[:, :, start:end, None] - cum_logf[:, :, None, :end])
        mask = jnp.arange(start, end)[:, None] >= jnp.arange(end)[None, :]
        s = jnp.where(mask[None, None], s, -jnp.inf)
        p = jax.nn.softmax(s, axis=-1).astype(vb.dtype)
        outs.append(jnp.einsum('bhqk,bhkd->bhqd', p, vb))
    return jnp.concatenate(outs, axis=2)


def causal_depthwise_conv(x, w, b):
    seq = x.shape[1]
    xp = jnp.pad(x, ((0, 0), (CONV_WIDTH - 1, 0), (0, 0)))
    y = b
    for j in range(CONV_WIDTH):
        y = y + xp[:, j:j + seq] * w[j]
    return y


def _lin_combine(c1, c2):
    a1, b1 = c1
    a2, b2 = c2
    return a1 * a2, a2 * b1 + b2


def rg_lru(x, w_a, b_a, w_x, b_x, lam):
    bsz, seq, ch = x.shape
    xb = x.reshape(bsz, seq, LRU_BLOCKS, LRU_BLOCK_DIM)
    gate_a = jnp.einsum('bsnd,nde->bsne', xb, w_a).reshape(bsz, seq, ch) + b_a
    gate_x = jnp.einsum('bsnd,nde->bsne', xb, w_x).reshape(bsz, seq, ch) + b_x
    r = jax.nn.sigmoid(gate_a.astype(jnp.float32))
    i = jax.nn.sigmoid(gate_x.astype(jnp.float32))
    log_a = -LRU_C * r * jax.nn.softplus(-lam.astype(jnp.float32))
    a = jnp.exp(log_a)
    u = jnp.sqrt(-jnp.expm1(2.0 * log_a)) * (i * x.astype(jnp.float32))
    _, h = lax.associative_scan(_lin_combine, (a, u), axis=1)
    return h.astype(x.dtype)


def _fwd_setup_inputs(seed: int = 0) -> dict:
    key = jax.random.key(seed)
    ks = jax.random.split(key, 20)
    f32 = jnp.float32
    nrm = lambda k, shape, s: jax.random.normal(k, shape, f32) * s
    x = jax.random.normal(ks[0], (BATCH, SEQ, D_MODEL), f32)
    norm1_g = 1.0 + nrm(ks[1], (DEPTH, D_MODEL), 0.02)
    w_in = nrm(ks[2], (DEPTH, D_MODEL, IN_WIDTH), D_MODEL ** -0.5)
    q_norm_g = 1.0 + nrm(ks[3], (DEPTH, HEAD_DIM), 0.02)
    k_norm_g = 1.0 + nrm(ks[4], (DEPTH, HEAD_DIM), 0.02)
    b_f = 2.0 + nrm(ks[5], (DEPTH, ATT_HEADS), 0.5)
    conv_w = nrm(ks[6], (DEPTH, CONV_WIDTH, LRU_WIDTH), CONV_WIDTH ** -0.5)
    conv_b = nrm(ks[7], (DEPTH, LRU_WIDTH), 0.01)
    w_a = nrm(ks[8], (DEPTH, LRU_BLOCKS, LRU_BLOCK_DIM, LRU_BLOCK_DIM), LRU_BLOCK_DIM ** -0.5)
    b_a = nrm(ks[9], (DEPTH, LRU_WIDTH), 0.01)
    w_x = nrm(ks[10], (DEPTH, LRU_BLOCKS, LRU_BLOCK_DIM, LRU_BLOCK_DIM), LRU_BLOCK_DIM ** -0.5)
    b_x = nrm(ks[11], (DEPTH, LRU_WIDTH), 0.01)
    ac = jax.random.uniform(ks[12], (DEPTH, LRU_WIDTH), f32, 0.9, 0.999)
    a0 = ac ** (1.0 / LRU_C)
    lam = jnp.log(a0) - jnp.log1p(-a0)
    attn_out_g = 1.0 + nrm(ks[13], (DEPTH, ATT_WIDTH), 0.02)
    lru_out_g = 1.0 + nrm(ks[14], (DEPTH, LRU_WIDTH), 0.02)
    w_out = nrm(ks[15], (DEPTH, D_MODEL, D_MODEL), D_MODEL ** -0.5)
    norm2_g = 1.0 + nrm(ks[16], (DEPTH, D_MODEL), 0.02)
    w_gate = nrm(ks[17], (DEPTH, D_MODEL, D_FF), D_MODEL ** -0.5)
    w_up = nrm(ks[18], (DEPTH, D_MODEL, D_FF), D_MODEL ** -0.5)
    w_down = nrm(ks[19], (DEPTH, D_FF, D_MODEL), D_FF ** -0.5)
    return {"x": x, "norm1_g": norm1_g, "w_in": w_in, "q_norm_g": q_norm_g,
            "k_norm_g": k_norm_g, "b_f": b_f, "conv_w": conv_w, "conv_b": conv_b,
            "w_a": w_a, "b_a": b_a, "w_x": w_x, "b_x": b_x, "lam": lam,
            "attn_out_g": attn_out_g, "lru_out_g": lru_out_g, "w_out": w_out,
            "norm2_g": norm2_g, "w_gate": w_gate, "w_up": w_up, "w_down": w_down}


def _fwd_reference(x, norm1_g, w_in, q_norm_g, k_norm_g, b_f, conv_w, conv_b, w_a, b_a,
              w_x, b_x, lam, attn_out_g, lru_out_g, w_out, norm2_g, w_gate, w_up,
              w_down):
    bsz, seq, _ = x.shape
    split_at = [ATT_WIDTH, 2 * ATT_WIDTH, 3 * ATT_WIDTH, 3 * ATT_WIDTH + ATT_HEADS,
                3 * ATT_WIDTH + ATT_HEADS + LRU_WIDTH]
    for l in range(DEPTH):
        h = rmsnorm(x, norm1_g[l])
        proj = h @ w_in[l]
        q, k, v, f_logit, lru_x, lru_gate = jnp.split(proj, split_at, axis=-1)

        q = rmsnorm(q.reshape(bsz, seq, ATT_HEADS, HEAD_DIM), q_norm_g[l])
        k = rmsnorm(k.reshape(bsz, seq, ATT_HEADS, HEAD_DIM), k_norm_g[l])
        v = v.reshape(bsz, seq, ATT_HEADS, HEAD_DIM)
        q, k, v = (t.transpose(0, 2, 1, 3) for t in (q, k, v))
        log_f = jax.nn.log_sigmoid(f_logit.astype(jnp.float32) + b_f[l].astype(jnp.float32))
        cum_logf = jnp.cumsum(log_f, axis=1).transpose(0, 2, 1)
        att = forgetting_attention(q, k, v, cum_logf)
        att = att.transpose(0, 2, 1, 3).reshape(bsz, seq, ATT_WIDTH)

        xc = causal_depthwise_conv(lru_x, conv_w[l], conv_b[l])
        hr = rg_lru(xc, w_a[l], b_a[l], w_x[l], b_x[l], lam[l])
        rec = hr * jax.nn.gelu(lru_gate)

        mixed = jnp.concatenate([rmsnorm(att, attn_out_g[l]), rmsnorm(rec, lru_out_g[l])], axis=-1)
        x = x + mixed @ w_out[l]

        h2 = rmsnorm(x, norm2_g[l])
        x = x + (jax.nn.silu(h2 @ w_gate[l]) * (h2 @ w_up[l])) @ w_down[l]
    return x


import jax as _jax
import jax.numpy as _jnp

TWIN_FORMAT = 'train_step'
FWD_PARAMS = ['x', 'norm1_g', 'w_in', 'q_norm_g', 'k_norm_g', 'b_f', 'conv_w', 'conv_b', 'w_a', 'b_a', 'w_x', 'b_x', 'lam', 'attn_out_g', 'lru_out_g', 'w_out', 'norm2_g', 'w_gate', 'w_up', 'w_down']
TWIN_WEIGHTS = ['norm1_g', 'w_in', 'q_norm_g', 'k_norm_g', 'b_f', 'conv_w', 'conv_b', 'w_a', 'b_a', 'w_x', 'b_x', 'lam', 'attn_out_g', 'lru_out_g', 'w_out', 'norm2_g', 'w_gate', 'w_up', 'w_down']
TWIN_DIFF_INPUT = 'x'
TWIN_INPUTS = ['x', 'norm1_g', 'w_in', 'q_norm_g', 'k_norm_g', 'b_f', 'conv_w', 'conv_b', 'w_a', 'b_a', 'w_x', 'b_x', 'lam', 'attn_out_g', 'lru_out_g', 'w_out', 'norm2_g', 'w_gate', 'w_up', 'w_down', 'loss_target', 'm_norm1_g', 'm_w_in', 'm_q_norm_g', 'm_k_norm_g', 'm_b_f', 'm_conv_w', 'm_conv_b', 'm_w_a', 'm_b_a', 'm_w_x', 'm_b_x', 'm_lam', 'm_attn_out_g', 'm_lru_out_g', 'm_w_out', 'm_norm2_g', 'm_w_gate', 'm_w_up', 'm_w_down', 'v_norm1_g', 'v_w_in', 'v_q_norm_g', 'v_k_norm_g', 'v_b_f', 'v_conv_w', 'v_conv_b', 'v_w_a', 'v_b_a', 'v_w_x', 'v_b_x', 'v_lam', 'v_attn_out_g', 'v_lru_out_g', 'v_w_out', 'v_norm2_g', 'v_w_gate', 'v_w_up', 'v_w_down']
TWIN_OUTPUTS = ['loss', 'grad_x', 'grad_norm1_g', 'grad_w_in', 'grad_q_norm_g', 'grad_k_norm_g', 'grad_b_f', 'grad_conv_w', 'grad_conv_b', 'grad_w_a', 'grad_b_a', 'grad_w_x', 'grad_b_x', 'grad_lam', 'grad_attn_out_g', 'grad_lru_out_g', 'grad_w_out', 'grad_norm2_g', 'grad_w_gate', 'grad_w_up', 'grad_w_down', 'delta_norm1_g', 'delta_w_in', 'delta_q_norm_g', 'delta_k_norm_g', 'delta_b_f', 'delta_conv_w', 'delta_conv_b', 'delta_w_a', 'delta_b_a', 'delta_w_x', 'delta_b_x', 'delta_lam', 'delta_attn_out_g', 'delta_lru_out_g', 'delta_w_out', 'delta_norm2_g', 'delta_w_gate', 'delta_w_up', 'delta_w_down', 'new_m_norm1_g', 'new_m_w_in', 'new_m_q_norm_g', 'new_m_k_norm_g', 'new_m_b_f', 'new_m_conv_w', 'new_m_conv_b', 'new_m_w_a', 'new_m_b_a', 'new_m_w_x', 'new_m_b_x', 'new_m_lam', 'new_m_attn_out_g', 'new_m_lru_out_g', 'new_m_w_out', 'new_m_norm2_g', 'new_m_w_gate', 'new_m_w_up', 'new_m_w_down', 'new_v_norm1_g', 'new_v_w_in', 'new_v_q_norm_g', 'new_v_k_norm_g', 'new_v_b_f', 'new_v_conv_w', 'new_v_conv_b', 'new_v_w_a', 'new_v_b_a', 'new_v_w_x', 'new_v_b_x', 'new_v_lam', 'new_v_attn_out_g', 'new_v_lru_out_g', 'new_v_w_out', 'new_v_norm2_g', 'new_v_w_gate', 'new_v_w_up', 'new_v_w_down']
TWIN_LEAF_KINDS = {'loss': 'loss', 'grad_x': 'grad_x', 'grad_norm1_g': 'grad_w', 'grad_w_in': 'grad_w', 'grad_q_norm_g': 'grad_w', 'grad_k_norm_g': 'grad_w', 'grad_b_f': 'grad_w', 'grad_conv_w': 'grad_w', 'grad_conv_b': 'grad_w', 'grad_w_a': 'grad_w', 'grad_b_a': 'grad_w', 'grad_w_x': 'grad_w', 'grad_b_x': 'grad_w', 'grad_lam': 'grad_w', 'grad_attn_out_g': 'grad_w', 'grad_lru_out_g': 'grad_w', 'grad_w_out': 'grad_w', 'grad_norm2_g': 'grad_w', 'grad_w_gate': 'grad_w', 'grad_w_up': 'grad_w', 'grad_w_down': 'grad_w', 'delta_norm1_g': 'delta_w', 'delta_w_in': 'delta_w', 'delta_q_norm_g': 'delta_w', 'delta_k_norm_g': 'delta_w', 'delta_b_f': 'delta_w', 'delta_conv_w': 'delta_w', 'delta_conv_b': 'delta_w', 'delta_w_a': 'delta_w', 'delta_b_a': 'delta_w', 'delta_w_x': 'delta_w', 'delta_b_x': 'delta_w', 'delta_lam': 'delta_w', 'delta_attn_out_g': 'delta_w', 'delta_lru_out_g': 'delta_w', 'delta_w_out': 'delta_w', 'delta_norm2_g': 'delta_w', 'delta_w_gate': 'delta_w', 'delta_w_up': 'delta_w', 'delta_w_down': 'delta_w', 'new_m_norm1_g': 'new_m', 'new_m_w_in': 'new_m', 'new_m_q_norm_g': 'new_m', 'new_m_k_norm_g': 'new_m', 'new_m_b_f': 'new_m', 'new_m_conv_w': 'new_m', 'new_m_conv_b': 'new_m', 'new_m_w_a': 'new_m', 'new_m_b_a': 'new_m', 'new_m_w_x': 'new_m', 'new_m_b_x': 'new_m', 'new_m_lam': 'new_m', 'new_m_attn_out_g': 'new_m', 'new_m_lru_out_g': 'new_m', 'new_m_w_out': 'new_m', 'new_m_norm2_g': 'new_m', 'new_m_w_gate': 'new_m', 'new_m_w_up': 'new_m', 'new_m_w_down': 'new_m', 'new_v_norm1_g': 'new_v', 'new_v_w_in': 'new_v', 'new_v_q_norm_g': 'new_v', 'new_v_k_norm_g': 'new_v', 'new_v_b_f': 'new_v', 'new_v_conv_w': 'new_v', 'new_v_conv_b': 'new_v', 'new_v_w_a': 'new_v', 'new_v_b_a': 'new_v', 'new_v_w_x': 'new_v', 'new_v_b_x': 'new_v', 'new_v_lam': 'new_v', 'new_v_attn_out_g': 'new_v', 'new_v_lru_out_g': 'new_v', 'new_v_w_out': 'new_v', 'new_v_norm2_g': 'new_v', 'new_v_w_gate': 'new_v', 'new_v_w_up': 'new_v', 'new_v_w_down': 'new_v'}


def _forward(args):
    return _fwd_reference(*[args[k] for k in FWD_PARAMS])


def _output_shape():
    out = _jax.eval_shape(lambda: _forward(_fwd_setup_inputs(0)))
    return out.shape, out.dtype

N_MICROBATCH = 1
ADAM_LR = 0.001
ADAM_B1 = 0.9
ADAM_B2 = 0.999
ADAM_EPS = 1e-08
ADAM_WD = 0.01
ADAM_STEP = 10
PER_EXAMPLE_BATCH_AXIS = {'x': 0, 'loss_target': 0}
SHARED_INPUTS = []
_WEIGHT_DTYPES = {'norm1_g': _jnp.float32, 'w_in': _jnp.float32, 'q_norm_g': _jnp.float32, 'k_norm_g': _jnp.float32, 'b_f': _jnp.float32, 'conv_w': _jnp.float32, 'conv_b': _jnp.float32, 'w_a': _jnp.float32, 'b_a': _jnp.float32, 'w_x': _jnp.float32, 'b_x': _jnp.float32, 'lam': _jnp.float32, 'attn_out_g': _jnp.float32, 'lru_out_g': _jnp.float32, 'w_out': _jnp.float32, 'norm2_g': _jnp.float32, 'w_gate': _jnp.float32, 'w_up': _jnp.float32, 'w_down': _jnp.float32}
MOMENT_SCALE = {'norm1_g': 1.143552e+00, 'w_in': 6.386373e-01, 'q_norm_g': 1.234729e+00, 'k_norm_g': 1.232337e+00, 'b_f': 6.171864e+00, 'conv_w': 2.808858e+00, 'conv_b': 5.567047e+01, 'w_a': 2.104804e+00, 'b_a': 1.318245e+00, 'w_x': 3.764820e+00, 'b_x': 1.033717e+00, 'lam': 1.275180e+00, 'attn_out_g': 6.663232e+01, 'lru_out_g': 9.708869e+01, 'w_out': 2.762373e+00, 'norm2_g': 4.914766e+01, 'w_gate': 5.958304e-01, 'w_up': 3.782582e-01, 'w_down': 5.821723e-01}


def _to_microbatches(a, axis):
    t = _jnp.moveaxis(a, axis, 0)
    t = t.reshape((N_MICROBATCH, t.shape[0] // N_MICROBATCH) + t.shape[1:])
    return _jnp.moveaxis(t, 1, axis + 1)


def setup_inputs(seed: int = 0) -> dict:
    inp = _fwd_setup_inputs(seed)
    key = _jax.random.fold_in(_jax.random.key(seed), 7919)
    shape, _ = _output_shape()
    out = dict(inp)
    out["loss_target"] = _jax.random.normal(_jax.random.fold_in(key, 0), shape, _jnp.float32)
    for i, name in enumerate(TWIN_WEIGHTS):
        w = inp[name].astype(_jnp.float32)
        if MOMENT_SCALE is None:
            s = _jnp.sqrt(_jnp.mean(_jnp.square(w)) + 1e-30)
        else:
            s = MOMENT_SCALE[name]
        km, kv = _jax.random.split(_jax.random.fold_in(key, i + 1))
        out[name] = w
        out["m_" + name] = s * _jax.random.normal(km, w.shape, _jnp.float32)
        out["v_" + name] = (s * s) * _jax.random.uniform(kv, w.shape, _jnp.float32, 0.5, 1.5)
    if N_MICROBATCH > 1:
        for name, axis in PER_EXAMPLE_BATCH_AXIS.items():
            out[name] = _to_microbatches(out[name], axis)
    return {'x': out['x'], 'norm1_g': out['norm1_g'], 'w_in': out['w_in'], 'q_norm_g': out['q_norm_g'], 'k_norm_g': out['k_norm_g'], 'b_f': out['b_f'], 'conv_w': out['conv_w'], 'conv_b': out['conv_b'], 'w_a': out['w_a'], 'b_a': out['b_a'], 'w_x': out['w_x'], 'b_x': out['b_x'], 'lam': out['lam'], 'attn_out_g': out['attn_out_g'], 'lru_out_g': out['lru_out_g'], 'w_out': out['w_out'], 'norm2_g': out['norm2_g'], 'w_gate': out['w_gate'], 'w_up': out['w_up'], 'w_down': out['w_down'], 'loss_target': out['loss_target'], 'm_norm1_g': out['m_norm1_g'], 'm_w_in': out['m_w_in'], 'm_q_norm_g': out['m_q_norm_g'], 'm_k_norm_g': out['m_k_norm_g'], 'm_b_f': out['m_b_f'], 'm_conv_w': out['m_conv_w'], 'm_conv_b': out['m_conv_b'], 'm_w_a': out['m_w_a'], 'm_b_a': out['m_b_a'], 'm_w_x': out['m_w_x'], 'm_b_x': out['m_b_x'], 'm_lam': out['m_lam'], 'm_attn_out_g': out['m_attn_out_g'], 'm_lru_out_g': out['m_lru_out_g'], 'm_w_out': out['m_w_out'], 'm_norm2_g': out['m_norm2_g'], 'm_w_gate': out['m_w_gate'], 'm_w_up': out['m_w_up'], 'm_w_down': out['m_w_down'], 'v_norm1_g': out['v_norm1_g'], 'v_w_in': out['v_w_in'], 'v_q_norm_g': out['v_q_norm_g'], 'v_k_norm_g': out['v_k_norm_g'], 'v_b_f': out['v_b_f'], 'v_conv_w': out['v_conv_w'], 'v_conv_b': out['v_conv_b'], 'v_w_a': out['v_w_a'], 'v_b_a': out['v_b_a'], 'v_w_x': out['v_w_x'], 'v_b_x': out['v_b_x'], 'v_lam': out['v_lam'], 'v_attn_out_g': out['v_attn_out_g'], 'v_lru_out_g': out['v_lru_out_g'], 'v_w_out': out['v_w_out'], 'v_norm2_g': out['v_norm2_g'], 'v_w_gate': out['v_w_gate'], 'v_w_up': out['v_w_up'], 'v_w_down': out['v_w_down']}


def _loss(weights, diff, rest, loss_target):
    with _jax.named_scope("forward"):
        args = {**rest, TWIN_DIFF_INPUT: diff, **{k: w.astype(_WEIGHT_DTYPES[k]) for k, w in weights.items()}}
        y = _forward(args)
    with _jax.named_scope("loss_head"):
        err = _jnp.square(y.astype(_jnp.float32) - loss_target)
        return 0.5 * _jnp.sum(_jnp.mean(err, axis=-1)) if err.ndim else 0.5 * err


def _adamw(w, g, m, v):
    m = ADAM_B1 * m + (1.0 - ADAM_B1) * g
    v = ADAM_B2 * v + (1.0 - ADAM_B2) * _jnp.square(g)
    m_hat = m / (1.0 - ADAM_B1 ** ADAM_STEP)
    v_hat = v / (1.0 - ADAM_B2 ** ADAM_STEP)
    delta = -ADAM_LR * (m_hat / (_jnp.sqrt(v_hat) + ADAM_EPS) + ADAM_WD * w)
    return delta, m, v


def reference(x, norm1_g, w_in, q_norm_g, k_norm_g, b_f, conv_w, conv_b, w_a, b_a, w_x, b_x, lam, attn_out_g, lru_out_g, w_out, norm2_g, w_gate, w_up, w_down, loss_target, m_norm1_g, m_w_in, m_q_norm_g, m_k_norm_g, m_b_f, m_conv_w, m_conv_b, m_w_a, m_b_a, m_w_x, m_b_x, m_lam, m_attn_out_g, m_lru_out_g, m_w_out, m_norm2_g, m_w_gate, m_w_up, m_w_down, v_norm1_g, v_w_in, v_q_norm_g, v_k_norm_g, v_b_f, v_conv_w, v_conv_b, v_w_a, v_b_a, v_w_x, v_b_x, v_lam, v_attn_out_g, v_lru_out_g, v_w_out, v_norm2_g, v_w_gate, v_w_up, v_w_down):
    given = dict(x=x, norm1_g=norm1_g, w_in=w_in, q_norm_g=q_norm_g, k_norm_g=k_norm_g, b_f=b_f, conv_w=conv_w, conv_b=conv_b, w_a=w_a, b_a=b_a, w_x=w_x, b_x=b_x, lam=lam, attn_out_g=attn_out_g, lru_out_g=lru_out_g, w_out=w_out, norm2_g=norm2_g, w_gate=w_gate, w_up=w_up, w_down=w_down, loss_target=loss_target, m_norm1_g=m_norm1_g, m_w_in=m_w_in, m_q_norm_g=m_q_norm_g, m_k_norm_g=m_k_norm_g, m_b_f=m_b_f, m_conv_w=m_conv_w, m_conv_b=m_conv_b, m_w_a=m_w_a, m_b_a=m_b_a, m_w_x=m_w_x, m_b_x=m_b_x, m_lam=m_lam, m_attn_out_g=m_attn_out_g, m_lru_out_g=m_lru_out_g, m_w_out=m_w_out, m_norm2_g=m_norm2_g, m_w_gate=m_w_gate, m_w_up=m_w_up, m_w_down=m_w_down, v_norm1_g=v_norm1_g, v_w_in=v_w_in, v_q_norm_g=v_q_norm_g, v_k_norm_g=v_k_norm_g, v_b_f=v_b_f, v_conv_w=v_conv_w, v_conv_b=v_conv_b, v_w_a=v_w_a, v_b_a=v_b_a, v_w_x=v_w_x, v_b_x=v_b_x, v_lam=v_lam, v_attn_out_g=v_attn_out_g, v_lru_out_g=v_lru_out_g, v_w_out=v_w_out, v_norm2_g=v_norm2_g, v_w_gate=v_w_gate, v_w_up=v_w_up, v_w_down=v_w_down)
    weights = {n: given[n] for n in TWIN_WEIGHTS}
    shared = {n: given[n] for n in SHARED_INPUTS}
    per_example = {n: given[n] for n in ['x']}
    grad_fn = _jax.value_and_grad(_loss, argnums=(0, 1))

    def one_microbatch(ex, loss_target):
        ex = dict(ex)
        diff = ex.pop(TWIN_DIFF_INPUT)
        return grad_fn(weights, diff, {**shared, **ex}, loss_target)

    if N_MICROBATCH == 1:
        loss, (grad_w, grad_x) = one_microbatch(per_example, given["loss_target"])
    else:
        def body(carry, xs):
            loss_sum, grad_sum = carry
            l_k, (gw_k, gx_k) = one_microbatch(xs[0], xs[1])
            with _jax.named_scope("update"):
                return (loss_sum + l_k, _jax.tree.map(_jnp.add, grad_sum, gw_k)), gx_k

        init = (_jnp.zeros((), _jnp.float32), _jax.tree.map(_jnp.zeros_like, weights))
        (loss, grad_w), grad_x = _jax.lax.scan(body, init, (per_example, given["loss_target"]))
    with _jax.named_scope("update"):
        delta_w, new_m, new_v = {}, {}, {}
        for n in TWIN_WEIGHTS:
            delta_w[n], new_m[n], new_v[n] = _adamw(weights[n], grad_w[n], given["m_" + n], given["v_" + n])
    return (loss, grad_x, *[grad_w[n] for n in TWIN_WEIGHTS], *[delta_w[n] for n in TWIN_WEIGHTS],
            *[new_m[n] for n in TWIN_WEIGHTS], *[new_v[n] for n in TWIN_WEIGHTS])
```

```python
import functools
import math

import jax
import jax.numpy as jnp
from jax import lax
from jax.experimental import pallas as pl
from jax.experimental.pallas import tpu as pltpu

F32 = jnp.float32
MXU_DTYPE = jnp.bfloat16
MESH = pl.DeviceIdType.MESH

D_MODEL = 1024
ATT_WIDTH = 512
LRU_WIDTH = 512
HEADS = 8
HEAD_DIM = 64
PAIR = 2 * HEAD_DIM
N_PAIR = HEADS // 2
LRU_BLOCKS = 8
CONV_WIDTH = 4
LRU_C = 8.0
NORM_EPS = 1e-6
QK_SCALE = 1.0 / math.sqrt(HEAD_DIM)
F_PAD = 128
N_CAT = 3 * ATT_WIDTH + F_PAD + 2 * LRU_WIDTH
NEG = -1e30

ADAM_LR, ADAM_B1, ADAM_B2, ADAM_EPS, ADAM_WD, ADAM_STEP = 0.001, 0.9, 0.999, 1e-08, 0.01, 10

TOKEN_TILE = 256
ATT_TILE = 512
LRU_TILE = 256
VMEM_SMALL = 32 * 1024 * 1024
VMEM_LARGE = 56 * 1024 * 1024


def _params(sem, vmem=VMEM_SMALL):
    return pltpu.CompilerParams(dimension_semantics=sem, vmem_limit_bytes=vmem)


def _const(shape):
    nd = len(shape)
    return pl.BlockSpec(shape, lambda *_: (0,) * nd)


def _sigmoid(x):
    return 1.0 / (1.0 + jnp.exp(-x))


def _half_sums(t, lo):
    s_lo = jnp.sum(jnp.where(lo, t, 0.0), axis=-1, keepdims=True)
    s_hi = jnp.sum(jnp.where(lo, 0.0, t), axis=-1, keepdims=True)
    return jnp.where(lo, s_lo, s_hi)


def _lo_mask():
    return lax.broadcasted_iota(jnp.int32, (1, PAIR), 1) < HEAD_DIM


def _other_chips(x, y):
    return [(1 - x, y), (x, 1 - y), (1 - x, 1 - y)]


def _gather_over_chips(shards):
    n = len(shards)

    def body(*refs):
        ins, outs = refs[:n], refs[n:2 * n]
        send_sems, recv_sems, loc_sems = refs[2 * n:]
        x, y, c = lax.axis_index("x"), lax.axis_index("y"), lax.axis_index("c")
        me = 2 * x + y
        copies = []
        for w in range(n):
            loc = pltpu.make_async_copy(ins[w], outs[w].at[me], loc_sems.at[w])
            loc.start()
            copies.append(loc)
            for k, (cx, cy) in enumerate(_other_chips(x, y)):
                cp = pltpu.make_async_remote_copy(
                    src_ref=ins[w], dst_ref=outs[w].at[me],
                    send_sem=send_sems.at[3 * w + k], recv_sem=recv_sems.at[3 * w + k],
                    device_id=(cx, cy, c), device_id_type=MESH)
                cp.start()
                copies.append(cp)
        for cp in copies:
            cp.wait()

    return pl.pallas_call(
        body, name="gather_weights",
        out_shape=[jax.ShapeDtypeStruct((4,) + s.shape, s.dtype) for s in shards],
        in_specs=[pl.BlockSpec(memory_space=pl.ANY)] * n,
        out_specs=[pl.BlockSpec(memory_space=pl.ANY)] * n,
        scratch_shapes=[pltpu.SemaphoreType.DMA((3 * n,)), pltpu.SemaphoreType.DMA((3 * n,)),
                        pltpu.SemaphoreType.DMA((n,))],
    )(*shards)


def _exchange_grads(slabs, pack):
    n = len(slabs)

    def body(*refs):
        ins, pack_in = refs[:n], refs[n]
        outs, pack_out = refs[n + 1:2 * n + 1], refs[2 * n + 1]
        send_sems, recv_sems, loc_sems, psend, precv = refs[2 * n + 2:]
        x, y, c = lax.axis_index("x"), lax.axis_index("y"), lax.axis_index("c")
        me = 2 * x + y
        dev = 4 * x + 2 * y + c
        copies = []
        for w in range(n):
            loc = pltpu.make_async_copy(ins[w].at[me], outs[w].at[me], loc_sems.at[w])
            loc.start()
            copies.append(loc)
            for k, (cx, cy) in enumerate(_other_chips(x, y)):
                cp = pltpu.make_async_remote_copy(
                    src_ref=ins[w].at[2 * cx + cy], dst_ref=outs[w].at[me],
                    send_sem=send_sems.at[3 * w + k], recv_sem=recv_sems.at[3 * w + k],
                    device_id=(cx, cy, c), device_id_type=MESH)
                cp.start()
                copies.append(cp)
        loc = pltpu.make_async_copy(pack_in, pack_out.at[dev], loc_sems.at[n])
        loc.start()
        copies.append(loc)
        for k in range(1, 8):
            fx, fy, fc = (k >> 2) & 1, (k >> 1) & 1, k & 1
            tx = (1 - x) if fx else x
            ty = (1 - y) if fy else y
            tc = (1 - c) if fc else c
            cp = pltpu.make_async_remote_copy(
                src_ref=pack_in, dst_ref=pack_out.at[dev],
                send_sem=psend.at[k - 1], recv_sem=precv.at[k - 1],
                device_id=(tx, ty, tc), device_id_type=MESH)
            cp.start()
            copies.append(cp)
        for cp in copies:
            cp.wait()

    return pl.pallas_call(
        body, name="exchange_grads",
        out_shape=[jax.ShapeDtypeStruct(s.shape, s.dtype) for s in slabs]
        + [jax.ShapeDtypeStruct((8,) + pack.shape, pack.dtype)],
        in_specs=[pl.BlockSpec(memory_space=pl.ANY)] * (n + 1),
        out_specs=[pl.BlockSpec(memory_space=pl.ANY)] * (n + 1),
        scratch_shapes=[pltpu.SemaphoreType.DMA((3 * n,)), pltpu.SemaphoreType.DMA((3 * n,)),
                        pltpu.SemaphoreType.DMA((n + 1,)),
                        pltpu.SemaphoreType.DMA((7,)), pltpu.SemaphoreType.DMA((7,))],
    )(*slabs, pack)


def _swap_with_sibling(arrs):
    n = len(arrs)

    def body(*refs):
        ins, outs = refs[:n], refs[n:2 * n]
        send_sems, recv_sems = refs[2 * n:]
        x, y, c = lax.axis_index("x"), lax.axis_index("y"), lax.axis_index("c")
        copies = []
        for w in range(n):
            cp = pltpu.make_async_remote_copy(
                src_ref=ins[w], dst_ref=outs[w], send_sem=send_sems.at[w], recv_sem=recv_sems.at[w],
                device_id=(x, y, 1 - c), device_id_type=MESH)
            cp.start()
            copies.append(cp)
        for cp in copies:
            cp.wait()

    return pl.pallas_call(
        body, name="swap_sibling",
        out_shape=[jax.ShapeDtypeStruct(a.shape, a.dtype) for a in arrs],
        in_specs=[pl.BlockSpec(memory_space=pl.ANY)] * n,
        out_specs=[pl.BlockSpec(memory_space=pl.ANY)] * n,
        scratch_shapes=[pltpu.SemaphoreType.DMA((n,)), pltpu.SemaphoreType.DMA((n,))],
    )(*arrs)


def _head_norm(t, g2, lo):
    rr = lax.rsqrt(_half_sums(t * t, lo) * (1.0 / HEAD_DIM) + NORM_EPS)
    return t * rr * g2


def _inproj(x2d, g1, wcat, gq2, gk2):
    T = x2d.shape[0]
    tm = TOKEN_TILE

    def body(x_ref, g1_ref, w_ref, gq_ref, gk_ref, qkv_ref, qn_ref, kn_ref, vb_ref, f_ref, lx_ref, lg_ref):
        x = x_ref[...]
        r = lax.rsqrt(jnp.mean(x * x, axis=-1, keepdims=True) + NORM_EPS)
        h = (x * r * g1_ref[...]).astype(MXU_DTYPE)
        proj = jnp.dot(h, w_ref[...], preferred_element_type=F32)
        qkv_ref[...] = proj[:, :3 * ATT_WIDTH]
        lo = _lo_mask()
        for p in range(N_PAIR):
            cols = slice(PAIR * p, PAIR * (p + 1))
            q = proj[:, PAIR * p:PAIR * (p + 1)]
            k = proj[:, ATT_WIDTH + PAIR * p:ATT_WIDTH + PAIR * (p + 1)]
            qn_ref[:, cols] = (_head_norm(q, gq_ref[...], lo) * QK_SCALE).astype(MXU_DTYPE)
            kn_ref[:, cols] = _head_norm(k, gk_ref[...], lo).astype(MXU_DTYPE)
        vb_ref[...] = proj[:, 2 * ATT_WIDTH:3 * ATT_WIDTH].astype(MXU_DTYPE)
        f0 = 3 * ATT_WIDTH
        f_ref[...] = proj[:, f0:f0 + F_PAD]
        lx_ref[...] = proj[:, f0 + F_PAD:f0 + F_PAD + LRU_WIDTH]
        lg_ref[...] = proj[:, f0 + F_PAD + LRU_WIDTH:]

    row = lambda w: pl.BlockSpec((tm, w), lambda i: (i, 0))
    return pl.pallas_call(
        body, name="inproj", grid=(T // tm,),
        in_specs=[row(D_MODEL), _const((1, D_MODEL)), _const((D_MODEL, N_CAT)), _const((1, PAIR)), _const((1, PAIR))],
        out_specs=[row(3 * ATT_WIDTH), row(ATT_WIDTH), row(ATT_WIDTH), row(ATT_WIDTH), row(F_PAD),
                   row(LRU_WIDTH), row(LRU_WIDTH)],
        out_shape=[jax.ShapeDtypeStruct((T, 3 * ATT_WIDTH), F32),
                   jax.ShapeDtypeStruct((T, ATT_WIDTH), MXU_DTYPE), jax.ShapeDtypeStruct((T, ATT_WIDTH), MXU_DTYPE),
                   jax.ShapeDtypeStruct((T, ATT_WIDTH), MXU_DTYPE), jax.ShapeDtypeStruct((T, F_PAD), F32),
                   jax.ShapeDtypeStruct((T, LRU_WIDTH), F32), jax.ShapeDtypeStruct((T, LRU_WIDTH), F32)],
        compiler_params=_params(("parallel",), VMEM_LARGE),
    )(x2d, g1, wcat, gq2, gk2)


def _forget_cumsum(f2d, bf, bl, seq):
    def body(z_ref, b_ref, o_ref):
        z = z_ref[...] + b_ref[...]
        lf = jnp.minimum(z, 0.0) - jnp.log(1.0 + jnp.exp(-jnp.abs(z)))
        row = lax.broadcasted_iota(jnp.int32, (seq, F_PAD), 0)
        k = 1
        while k < seq:
            lf = lf + jnp.where(row >= k, pltpu.roll(lf, k, 0), 0.0)
            k *= 2
        o_ref[...] = lf

    return pl.pallas_call(
        body, name="forget_cumsum", grid=(bl,),
        in_specs=[pl.BlockSpec((seq, F_PAD), lambda b: (b, 0)), _const((1, F_PAD))],
        out_specs=pl.BlockSpec((seq, F_PAD), lambda b: (b, 0)),
        out_shape=jax.ShapeDtypeStruct(f2d.shape, F32),
        compiler_params=_params(("parallel",)),
    )(f2d, bf)


def _attn_fwd(qn, kn, vb, frow, fstart, bl, seq):
    tq = min(ATT_TILE, seq)
    nq = seq // tq
    T = bl * seq

    def body(fs_ref, q_ref, k_ref, v_ref, fr_ref, o_ref, ox_ref, lse_ref):
        b, p, i = pl.program_id(0), pl.program_id(1), pl.program_id(2)
        lane = lax.broadcasted_iota(jnp.int32, (1, PAIR), 1)
        rows = lax.broadcasted_iota(jnp.int32, (tq, tq), 0)
        cols = lax.broadcasted_iota(jnp.int32, (tq, tq), 1)
        causal = cols <= rows
        q = q_ref[...]
        out = jnp.zeros((tq, PAIR), F32)
        out_x = jnp.zeros((tq, PAIR), F32)
        lse = jnp.zeros((tq, PAIR), F32)
        for hh in range(2):
            hm = (lane >= HEAD_DIM * hh) & (lane < HEAD_DIM * (hh + 1))
            qh = jnp.where(hm, q, jnp.zeros_like(q))
            shift = fs_ref[((b * N_PAIR + p) * 2 + hh) * nq + i]

            def block(j, carry, masked, qh=qh, hm=hm, shift=shift, hh=hh):
                m, l, acc, acc_lo = carry
                start = pl.multiple_of(j * tq, tq)
                k = k_ref[pl.ds(start, tq), :]
                v = v_ref[pl.ds(start, tq), :]
                s = lax.dot_general(qh, k, (((1,), (1,)), ((), ())), preferred_element_type=F32)
                s = s - (fr_ref[0, 0, hh:hh + 1, pl.ds(start, tq)] - shift)
                if masked:
                    s = jnp.where(causal, s, NEG)
                m_new = jnp.maximum(m, jnp.max(s, axis=-1, keepdims=True))
                alpha = jnp.exp(m - m_new)
                pe = jnp.exp(s - m_new)
                l = alpha * l + jnp.sum(pe, axis=-1, keepdims=True)
                vh = jnp.where(hm, v, jnp.zeros_like(v))
                pb = pe.astype(MXU_DTYPE)
                p_lo = (pe - pb.astype(F32)).astype(MXU_DTYPE)
                acc = alpha * acc + jnp.dot(pb, vh, preferred_element_type=F32)
                acc_lo = alpha * acc_lo + jnp.dot(p_lo, vh, preferred_element_type=F32)
                return m_new, l, acc, acc_lo

            carry = (jnp.full((tq, 1), NEG, F32), jnp.zeros((tq, 1), F32), jnp.zeros((tq, PAIR), F32),
                     jnp.zeros((tq, PAIR), F32))
            carry = lax.fori_loop(0, i, functools.partial(block, masked=False), carry)
            m, l, acc, acc_lo = block(i, carry, True)
            out = out + acc / l
            out_x = out_x + (acc + acc_lo) / l
            lse = jnp.where(hm, m + jnp.log(l), lse)
        o_ref[...] = out
        ox_ref[...] = out_x
        lse_ref[...] = lse

    blk = pl.BlockSpec((tq, PAIR), lambda b, p, i: (b * nq + i, p))
    full = pl.BlockSpec((seq, PAIR), lambda b, p, i: (b, p))
    return pl.pallas_call(
        body, name="attn_fwd", grid=(bl, N_PAIR, nq),
        in_specs=[pl.BlockSpec(memory_space=pltpu.SMEM), blk, full, full,
                  pl.BlockSpec((1, 1, 2, seq), lambda b, p, i: (b, p, 0, 0))],
        out_specs=[blk, blk, blk],
        out_shape=[jax.ShapeDtypeStruct((T, ATT_WIDTH), F32)] * 3,
        compiler_params=_params(("parallel", "parallel", "arbitrary")),
    )(fstart, qn, kn, vb, frow)


def _conv_taps(lx, prev8, cw, cb):
    xs = jnp.concatenate([prev8, lx], axis=0)
    shifted = [lx] + [pltpu.roll(xs, k, 0)[8:] for k in range(1, CONV_WIDTH)]
    xc = cb + cw[CONV_WIDTH - 1:CONV_WIDTH] * lx
    for k in range(1, CONV_WIDTH):
        xc = xc + cw[CONV_WIDTH - 1 - k:CONV_WIDTH - k] * shifted[k]
    return xc, shifted


def _lru_gates(xc, wa, ba, wx, bx, lam):
    xb = xc.astype(MXU_DTYPE)
    r = _sigmoid(jnp.dot(xb, wa, preferred_element_type=F32) + ba)
    ig = _sigmoid(jnp.dot(xb, wx, preferred_element_type=F32) + bx)
    sp = jnp.maximum(-lam, 0.0) + jnp.log(1.0 + jnp.exp(-jnp.abs(lam)))
    log_a = -LRU_C * r * sp
    a = jnp.exp(log_a)
    th = jnp.tanh(log_a)
    mult = jnp.sqrt(-2.0 * th / (1.0 - th))
    return r, ig, sp, a, mult


def _gelu_parts(x):
    c0 = math.sqrt(2.0 / math.pi)
    t = jnp.tanh(c0 * (x + 0.044715 * x * x * x))
    g = 0.5 * x * (1.0 + t)
    dg = 0.5 * (1.0 + t) + 0.5 * x * (1.0 - t * t) * c0 * (1.0 + 3.0 * 0.044715 * x * x)
    return g, dg


def _lru_fwd(lx, lg, cw, cb, wa, ba, wx, bx, lam, bl, seq):
    tc = min(LRU_TILE, seq)
    nc = seq // tc
    T = bl * seq

    def body(lx_ref, lxp_ref, lg_ref, cw_ref, cb_ref, wa_ref, ba_ref, wx_ref, bx_ref, lam_ref,
             h_ref, rec_ref, hc_ref):
        i = pl.program_id(1)

        @pl.when(i == 0)
        def _():
            hc_ref[...] = jnp.zeros_like(hc_ref)

        lxv = lx_ref[...]
        prev8 = jnp.where(i > 0, lxp_ref[...], 0.0)
        xc, _ = _conv_taps(lxv, prev8, cw_ref[...], cb_ref[...])
        _, ig, _, a, mult = _lru_gates(xc, wa_ref[...], ba_ref[...], wx_ref[...], bx_ref[...], lam_ref[...])
        u = mult * (ig * xc)
        row = lax.broadcasted_iota(jnp.int32, (tc, LRU_WIDTH), 0)
        A, B = a, u
        k = 1
        while k < tc:
            a_s = jnp.where(row >= k, pltpu.roll(A, k, 0), 1.0)
            b_s = jnp.where(row >= k, pltpu.roll(B, k, 0), 0.0)
            B = A * b_s + B
            A = A * a_s
            k *= 2
        h = A * hc_ref[0:1, :] + B
        hc_ref[0:1, :] = h[tc - 1:tc, :]
        h_ref[...] = h
        g, _ = _gelu_parts(lg_ref[...])
        rec_ref[...] = h * g

    tile = pl.BlockSpec((tc, LRU_WIDTH), lambda b, i: (b * nc + i, 0))
    prev = pl.BlockSpec((8, LRU_WIDTH), lambda b, i: (jnp.maximum((b * seq + i * tc) // 8 - 1, 0), 0))
    vec = _const((1, LRU_WIDTH))
    mat = _const((LRU_WIDTH, LRU_WIDTH))
    return pl.pallas_call(
        body, name="lru_fwd", grid=(bl, nc),
        in_specs=[tile, prev, tile, _const((CONV_WIDTH, LRU_WIDTH)), vec, mat, vec, mat, vec, vec],
        out_specs=[tile, tile],
        out_shape=[jax.ShapeDtypeStruct((T, LRU_WIDTH), F32), jax.ShapeDtypeStruct((T, LRU_WIDTH), F32)],
        scratch_shapes=[pltpu.VMEM((8, LRU_WIDTH), F32)],
        compiler_params=_params(("arbitrary", "arbitrary")),
    )(lx, lx, lg, cw, cb, wa, ba, wx, bx, lam)


def _outproj(x2d, att, rec, ga, gr, wout):
    T = x2d.shape[0]
    tm = TOKEN_TILE

    def body(x_ref, a_ref, r_ref, ga_ref, gr_ref, w_ref, o_ref):
        a = a_ref[...]
        rc = r_ref[...]
        na = a * lax.rsqrt(jnp.mean(a * a, axis=-1, keepdims=True) + NORM_EPS) * ga_ref[...]
        nr = rc * lax.rsqrt(jnp.mean(rc * rc, axis=-1, keepdims=True) + NORM_EPS) * gr_ref[...]
        o_ref[...] = (x_ref[...]
                      + jnp.dot(na.astype(MXU_DTYPE), w_ref[:ATT_WIDTH, :], preferred_element_type=F32)
                      + jnp.dot(nr.astype(MXU_DTYPE), w_ref[ATT_WIDTH:, :], preferred_element_type=F32))

    row = lambda w: pl.BlockSpec((tm, w), lambda i: (i, 0))
    return pl.pallas_call(
        body, name="outproj", grid=(T // tm,),
        in_specs=[row(D_MODEL), row(ATT_WIDTH), row(LRU_WIDTH), _const((1, ATT_WIDTH)), _const((1, LRU_WIDTH)),
                  _const((D_MODEL, D_MODEL))],
        out_specs=row(D_MODEL),
        out_shape=jax.ShapeDtypeStruct((T, D_MODEL), F32),
        compiler_params=_params(("parallel",)),
    )(x2d, att, rec, ga, gr, wout)


def _mlp_fwd(x2, g2, wg, wu, wd, target):
    T = x2.shape[0]
    tm = TOKEN_TILE
    dff = wg.shape[1]

    def body(x_ref, g_ref, wg_ref, wu_ref, wd_ref, t_ref, gt_ref, up_ref, dy_ref, loss_ref):
        @pl.when(pl.program_id(0) == 0)
        def _():
            loss_ref[...] = jnp.zeros_like(loss_ref)

        x = x_ref[...]
        r = lax.rsqrt(jnp.mean(x * x, axis=-1, keepdims=True) + NORM_EPS)
        h = (x * r * g_ref[...]).astype(MXU_DTYPE)
        gt = jnp.dot(h, wg_ref[...], preferred_element_type=F32)
        up = jnp.dot(h, wu_ref[...], preferred_element_type=F32)
        gt_ref[...] = gt
        up_ref[...] = up
        act = (gt * _sigmoid(gt) * up).astype(MXU_DTYPE)
        y = x + jnp.dot(act, wd_ref[...], preferred_element_type=F32)
        e = y - t_ref[...]
        dy_ref[...] = e * (1.0 / D_MODEL)
        loss_ref[...] += jnp.sum(e * e)

    row = lambda w: pl.BlockSpec((tm, w), lambda i: (i, 0))
    return pl.pallas_call(
        body, name="mlp_fwd", grid=(T // tm,),
        in_specs=[row(D_MODEL), _const((1, D_MODEL)), _const((D_MODEL, dff)), _const((D_MODEL, dff)),
                  _const((dff, D_MODEL)), row(D_MODEL)],
        out_specs=[row(dff), row(dff), row(D_MODEL), _const((8, 128))],
        out_shape=[jax.ShapeDtypeStruct((T, dff), F32), jax.ShapeDtypeStruct((T, dff), F32),
                   jax.ShapeDtypeStruct((T, D_MODEL), F32), jax.ShapeDtypeStruct((8, 128), F32)],
        compiler_params=_params(("arbitrary",), VMEM_LARGE),
    )(x2, g2, wg, wu, wd, target)


def _nt(a, b):
    return lax.dot_general(a, b, (((1,), (1,)), ((), ())), preferred_element_type=F32)


def _tn(a, b):
    return lax.dot_general(a, b, (((0,), (0,)), ((), ())), preferred_element_type=F32)


def _mlp_bwd(dy, x2, gt, up, g2, wg, wu, wd):
    T = x2.shape[0]
    tm = TOKEN_TILE
    dff = wg.shape[1]

    def body(dy_ref, x_ref, gt_ref, up_ref, g_ref, wg_ref, wu_ref, wd_ref,
             dx_ref, dxb_ref, dgt_ref, dup_ref, act_ref, h_ref, dyb_ref, dg_ref):
        @pl.when(pl.program_id(0) == 0)
        def _():
            dg_ref[...] = jnp.zeros_like(dg_ref)

        dy_v = dy_ref[...]
        dyb = dy_v.astype(MXU_DTYPE)
        dyb_ref[...] = dyb
        x = x_ref[...]
        r = lax.rsqrt(jnp.mean(x * x, axis=-1, keepdims=True) + NORM_EPS)
        xh = x * r
        h_ref[...] = (xh * g_ref[...]).astype(MXU_DTYPE)
        gt_v = gt_ref[...]
        up_v = up_ref[...]
        sg = _sigmoid(gt_v)
        silu = gt_v * sg
        act_ref[...] = (silu * up_v).astype(MXU_DTYPE)
        dact = _nt(dyb, wd_ref[...])
        dup = (dact * silu).astype(MXU_DTYPE)
        dgt = (dact * up_v * (sg * (1.0 + gt_v * (1.0 - sg)))).astype(MXU_DTYPE)
        dup_ref[...] = dup
        dgt_ref[...] = dgt
        dh = _nt(dgt, wg_ref[...]) + _nt(dup, wu_ref[...])
        dg_ref[...] += jnp.sum(dh * xh, axis=0, keepdims=True)
        dxh = dh * g_ref[...]
        dx = dy_v + r * (dxh - xh * jnp.mean(dxh * xh, axis=-1, keepdims=True))
        dx_ref[...] = dx
        dxb_ref[...] = dx.astype(MXU_DTYPE)

    row = lambda w: pl.BlockSpec((tm, w), lambda i: (i, 0))
    return pl.pallas_call(
        body, name="mlp_bwd", grid=(T // tm,),
        in_specs=[row(D_MODEL), row(D_MODEL), row(dff), row(dff), _const((1, D_MODEL)),
                  _const((D_MODEL, dff)), _const((D_MODEL, dff)), _const((dff, D_MODEL))],
        out_specs=[row(D_MODEL), row(D_MODEL), row(dff), row(dff), row(dff), row(D_MODEL), row(D_MODEL),
                   _const((1, D_MODEL))],
        out_shape=[jax.ShapeDtypeStruct((T, D_MODEL), F32), jax.ShapeDtypeStruct((T, D_MODEL), MXU_DTYPE),
                   jax.ShapeDtypeStruct((T, dff), MXU_DTYPE), jax.ShapeDtypeStruct((T, dff), MXU_DTYPE),
                   jax.ShapeDtypeStruct((T, dff), MXU_DTYPE), jax.ShapeDtypeStruct((T, D_MODEL), MXU_DTYPE),
                   jax.ShapeDtypeStruct((T, D_MODEL), MXU_DTYPE), jax.ShapeDtypeStruct((1, D_MODEL), F32)],
        compiler_params=_params(("arbitrary",), VMEM_LARGE),
    )(dy, x2, gt, up, g2, wg, wu, wd)


def _matmul_tn(a, b, tn, name):
    T, K = a.shape
    N = b.shape[1]
    tt = min(512, T)

    def body(a_ref, b_ref, o_ref):
        @pl.when(pl.program_id(1) == 0)
        def _():
            o_ref[...] = jnp.zeros_like(o_ref)

        o_ref[...] += _tn(a_ref[...], b_ref[...])

    return pl.pallas_call(
        body, name=name, grid=(N // tn, T // tt),
        in_specs=[pl.BlockSpec((tt, K), lambda n, t: (t, 0)), pl.BlockSpec((tt, tn), lambda n, t: (t, n))],
        out_specs=pl.BlockSpec((K, tn), lambda n, t: (0, n)),
        out_shape=jax.ShapeDtypeStruct((K, N), F32),
        compiler_params=_params(("parallel", "arbitrary"), VMEM_LARGE),
    )(a, b)


def _outproj_bwd(dx2b, att, att_x, rec, ga, gr, wout):
    T = att.shape[0]
    tm = TOKEN_TILE

    def body(dx_ref, a_ref, ax_ref, r_ref, ga_ref, gr_ref, w_ref, datt_ref, delta_ref, drec_ref, mix_ref, dga_ref, dgr_ref):
        @pl.when(pl.program_id(0) == 0)
        def _():
            dga_ref[...] = jnp.zeros_like(dga_ref)
            dgr_ref[...] = jnp.zeros_like(dgr_ref)

        dmix = _nt(dx_ref[...], w_ref[...])

        def norm_bwd(v, g, dn):
            rr = lax.rsqrt(jnp.mean(v * v, axis=-1, keepdims=True) + NORM_EPS)
            vh = v * rr
            dvh = dn * g
            dv = rr * (dvh - vh * jnp.mean(dvh * vh, axis=-1, keepdims=True))
            return vh, dv, jnp.sum(dn * vh, axis=0, keepdims=True)

        a = a_ref[...]
        ah, datt, dga = norm_bwd(a, ga_ref[...], dmix[:, :ATT_WIDTH])
        rh, drec, dgr = norm_bwd(r_ref[...], gr_ref[...], dmix[:, ATT_WIDTH:])
        dga_ref[...] += dga
        dgr_ref[...] += dgr
        mix_ref[:, :ATT_WIDTH] = (ah * ga_ref[...]).astype(MXU_DTYPE)
        mix_ref[:, ATT_WIDTH:] = (rh * gr_ref[...]).astype(MXU_DTYPE)
        dattb = datt.astype(MXU_DTYPE)
        datt_ref[...] = dattb
        drec_ref[...] = drec
        lo = _lo_mask()
        prod = dattb.astype(F32) * ax_ref[...]
        for p in range(N_PAIR):
            delta_ref[:, PAIR * p:PAIR * (p + 1)] = _half_sums(prod[:, PAIR * p:PAIR * (p + 1)], lo)

    row = lambda w: pl.BlockSpec((tm, w), lambda i: (i, 0))
    return pl.pallas_call(
        body, name="outproj_bwd", grid=(T // tm,),
        in_specs=[row(D_MODEL), row(ATT_WIDTH), row(ATT_WIDTH), row(LRU_WIDTH), _const((1, ATT_WIDTH)), _const((1, LRU_WIDTH)),
                  _const((D_MODEL, D_MODEL))],
        out_specs=[row(ATT_WIDTH), row(ATT_WIDTH), row(LRU_WIDTH), row(D_MODEL),
                   _const((1, ATT_WIDTH)), _const((1, LRU_WIDTH))],
        out_shape=[jax.ShapeDtypeStruct((T, ATT_WIDTH), MXU_DTYPE), jax.ShapeDtypeStruct((T, ATT_WIDTH), F32),
                   jax.ShapeDtypeStruct((T, LRU_WIDTH), F32), jax.ShapeDtypeStruct((T, D_MODEL), MXU_DTYPE),
                   jax.ShapeDtypeStruct((1, ATT_WIDTH), F32), jax.ShapeDtypeStruct((1, LRU_WIDTH), F32)],
        compiler_params=_params(("arbitrary",)),
    )(dx2b, att, att_x, rec, ga, gr, wout)


def _lru_bwd(drec, lg, h, lx, cw, cb, wa, ba, wx, bx, lam, bl, seq):
    tc = min(LRU_TILE, seq)
    nc = seq // tc
    T = bl * seq
    n = tc

    def body(dr_ref, lg_ref, h_ref, hp_ref, lx_ref, lxp_ref, cw_ref, cb_ref, wa_ref, ba_ref, wx_ref, bx_ref, lam_ref,
             dlx_ref, dlg_ref, dwa_ref, dwx_ref, small_ref, gc_ref, dxn_ref):
        b, i = pl.program_id(0), pl.program_id(1)
        ir = nc - 1 - i

        @pl.when((b == 0) & (i == 0))
        def _():
            dwa_ref[...] = jnp.zeros_like(dwa_ref)
            dwx_ref[...] = jnp.zeros_like(dwx_ref)
            small_ref[...] = jnp.zeros_like(small_ref)

        @pl.when(i == 0)
        def _():
            gc_ref[...] = jnp.zeros_like(gc_ref)
            dxn_ref[...] = jnp.zeros_like(dxn_ref)

        cw = cw_ref[...]
        lam_v = lam_ref[...]
        lxv = lx_ref[...]
        prev8 = jnp.where(ir > 0, lxp_ref[...], 0.0)
        xc, shifted = _conv_taps(lxv, prev8, cw, cb_ref[...])
        r, ig, sp, a, mult = _lru_gates(xc, wa_ref[...], ba_ref[...], wx_ref[...], bx_ref[...], lam_v)
        hv = h_ref[...]
        drv = dr_ref[...]
        g, dg = _gelu_parts(lg_ref[...])
        dlg_ref[...] = drv * hv * dg
        dh = drv * g

        row = lax.broadcasted_iota(jnp.int32, (n, LRU_WIDTH), 0)
        A = jnp.where(row < n - 1, pltpu.roll(a, n - 1, 0), 0.0)
        B = dh + jnp.where(row == n - 1, gc_ref[0:1, :], 0.0)
        k = 1
        while k < n:
            a_s = jnp.where(row < n - k, pltpu.roll(A, n - k, 0), 1.0)
            b_s = jnp.where(row < n - k, pltpu.roll(B, n - k, 0), 0.0)
            B = B + A * b_s
            A = A * a_s
            k *= 2
        gs = B
        gc_ref[0:1, :] = a[0:1, :] * gs[0:1, :]

        hprev8 = jnp.where(ir > 0, hp_ref[...], 0.0)
        h_prev = pltpu.roll(jnp.concatenate([hprev8, hv], axis=0), 1, 0)[8:]
        da = gs * h_prev
        ix = ig * xc
        dmult = gs * ix
        dig = gs * mult * xc
        dxc = gs * mult * ig
        dlog_a = da * a - dmult * (a * a) / mult
        dr_gate = dlog_a * (-LRU_C * sp)
        dza = dr_gate * r * (1.0 - r)
        dzx = dig * ig * (1.0 - ig)
        dzab = dza.astype(MXU_DTYPE)
        dzxb = dzx.astype(MXU_DTYPE)
        xcb = xc.astype(MXU_DTYPE)
        dwa_ref[...] += _tn(xcb, dzab)
        dwx_ref[...] += _tn(xcb, dzxb)
        dxc = dxc + _nt(dzab, wa_ref[...]) + _nt(dzxb, wx_ref[...])

        ds = jnp.concatenate([dxc, dxn_ref[...]], axis=0)
        dlx = cw[CONV_WIDTH - 1:CONV_WIDTH] * dxc
        for k in range(1, CONV_WIDTH):
            dlx = dlx + cw[CONV_WIDTH - 1 - k:CONV_WIDTH - k] * pltpu.roll(ds, n + 8 - k, 0)[:n]
        dlx_ref[...] = dlx
        dxn_ref[...] = dxc[0:8, :]

        colsum = lambda v: jnp.sum(v, axis=0, keepdims=True)
        small_ref[0:1, :] += colsum(dza)
        small_ref[1:2, :] += colsum(dzx)
        small_ref[2:3, :] += colsum(dlog_a * r) * (LRU_C * _sigmoid(-lam_v))
        small_ref[3:4, :] += colsum(dxc)
        for k in range(CONV_WIDTH):
            j = CONV_WIDTH - 1 - k
            small_ref[4 + j:5 + j, :] += colsum(dxc * shifted[k])

    tile = pl.BlockSpec((tc, LRU_WIDTH), lambda b, i: (b * nc + (nc - 1 - i), 0))
    prev = pl.BlockSpec((8, LRU_WIDTH), lambda b, i: (jnp.maximum((b * seq + (nc - 1 - i) * tc) // 8 - 1, 0), 0))
    vec = _const((1, LRU_WIDTH))
    mat = _const((LRU_WIDTH, LRU_WIDTH))
    return pl.pallas_call(
        body, name="lru_bwd", grid=(bl, nc),
        in_specs=[tile, tile, tile, prev, tile, prev, _const((CONV_WIDTH, LRU_WIDTH)), vec, mat, vec, mat, vec, vec],
        out_specs=[tile, tile, mat, mat, _const((8, LRU_WIDTH))],
        out_shape=[jax.ShapeDtypeStruct((T, LRU_WIDTH), F32), jax.ShapeDtypeStruct((T, LRU_WIDTH), F32),
                   jax.ShapeDtypeStruct((LRU_WIDTH, LRU_WIDTH), F32), jax.ShapeDtypeStruct((LRU_WIDTH, LRU_WIDTH), F32),
                   jax.ShapeDtypeStruct((8, LRU_WIDTH), F32)],
        scratch_shapes=[pltpu.VMEM((8, LRU_WIDTH), F32), pltpu.VMEM((8, LRU_WIDTH), F32)],
        compiler_params=_params(("arbitrary", "arbitrary")),
    )(drec, lg, h, h, lx, lx, cw, cb, wa, ba, wx, bx, lam)


def _attn_bwd(qn, kn, vb, dob, lse, delta, frow, fstart, bl, seq):
    tq = min(ATT_TILE, seq)
    nq = seq // tq
    T = bl * seq

    def body(fs_ref, q_ref, k_ref, v_ref, do_ref, lse_ref, dl_ref, fr_ref, dq_ref, dk_ref, dv_ref, df_ref):
        b, p, j = pl.program_id(0), pl.program_id(1), pl.program_id(2)

        @pl.when(j == 0)
        def _():
            dq_ref[...] = jnp.zeros_like(dq_ref)

        lane = lax.broadcasted_iota(jnp.int32, (1, PAIR), 1)
        rows = lax.broadcasted_iota(jnp.int32, (tq, tq), 0)
        cols = lax.broadcasted_iota(jnp.int32, (tq, tq), 1)
        causal = cols <= rows
        kv = k_ref[...]
        vv = v_ref[...]
        dk = jnp.zeros((tq, PAIR), F32)
        dv = jnp.zeros((tq, PAIR), F32)
        for hh in range(2):
            hm = (lane >= HEAD_DIM * hh) & (lane < HEAD_DIM * (hh + 1))
            kh = jnp.where(hm, kv, jnp.zeros_like(kv))
            fk = fr_ref[0, 0, hh:hh + 1, :]
            base = ((b * N_PAIR + p) * 2 + hh) * nq
            c0 = HEAD_DIM * hh

            def block(i, carry, masked, hm=hm, kh=kh, fk=fk, base=base, c0=c0):
                dk, dv, df = carry
                start = pl.multiple_of(i * tq, tq)
                qi = q_ref[pl.ds(start, tq), :]
                doi = do_ref[pl.ds(start, tq), :]
                qh = jnp.where(hm, qi, jnp.zeros_like(qi))
                doh = jnp.where(hm, doi, jnp.zeros_like(doi))
                s = _nt(qh, kv) - (fk - fs_ref[base + i])
                if masked:
                    s = jnp.where(causal, s, NEG)
                pr = jnp.exp(s - lse_ref[pl.ds(start, tq), c0:c0 + 1])
                dp = _nt(doh, vv)
                ds = pr * (dp - dl_ref[pl.ds(start, tq), c0:c0 + 1])
                dsb = ds.astype(MXU_DTYPE)
                dv = dv + _tn(pr.astype(MXU_DTYPE), doh)
                dk = dk + _tn(dsb, qh)
                dq_ref[pl.ds(start, tq), :] += jnp.dot(dsb, kh, preferred_element_type=F32)
                df = df - jnp.sum(ds, axis=0, keepdims=True)
                return dk, dv, df

            carry = block(j, (dk, dv, jnp.zeros((1, tq), F32)), True)
            dk, dv, df = lax.fori_loop(j + 1, nq, functools.partial(block, masked=False), carry)
            df_ref[0, 0, hh:hh + 1, :] = df
        dk_ref[...] = dk
        dv_ref[...] = dv

    blk = pl.BlockSpec((tq, PAIR), lambda b, p, j: (b * nq + j, p))
    full = pl.BlockSpec((seq, PAIR), lambda b, p, j: (b, p))
    fblk = pl.BlockSpec((1, 1, 2, tq), lambda b, p, j: (b, p, 0, j))
    return pl.pallas_call(
        body, name="attn_bwd", grid=(bl, N_PAIR, nq),
        in_specs=[pl.BlockSpec(memory_space=pltpu.SMEM), full, blk, blk, full, full, full, fblk],
        out_specs=[full, blk, blk, fblk],
        out_shape=[jax.ShapeDtypeStruct((T, ATT_WIDTH), F32), jax.ShapeDtypeStruct((T, ATT_WIDTH), F32),
                   jax.ShapeDtypeStruct((T, ATT_WIDTH), F32), jax.ShapeDtypeStruct((bl, N_PAIR, 2, seq), F32)],
        compiler_params=_params(("parallel", "parallel", "arbitrary"), VMEM_LARGE),
    )(fstart, qn, kn, vb, dob, lse, delta, frow)


def _forget_bwd(dfcol, f2d, bf, bl, seq):
    def body(d_ref, z_ref, b_ref, o_ref, db_ref):
        @pl.when(pl.program_id(0) == 0)
        def _():
            db_ref[...] = jnp.zeros_like(db_ref)

        d = d_ref[...]
        row = lax.broadcasted_iota(jnp.int32, (seq, F_PAD), 0)
        k = 1
        while k < seq:
            d = d + jnp.where(row < seq - k, pltpu.roll(d, seq - k, 0), 0.0)
            k *= 2
        dz = d * _sigmoid(-(z_ref[...] + b_ref[...]))
        o_ref[...] = dz
        db_ref[...] += jnp.sum(dz, axis=0, keepdims=True)

    blk = pl.BlockSpec((seq, F_PAD), lambda b: (b, 0))
    return pl.pallas_call(
        body, name="forget_bwd", grid=(bl,),
        in_specs=[blk, blk, _const((1, F_PAD))],
        out_specs=[blk, _const((1, F_PAD))],
        out_shape=[jax.ShapeDtypeStruct(f2d.shape, F32), jax.ShapeDtypeStruct((1, F_PAD), F32)],
        compiler_params=_params(("arbitrary",)),
    )(dfcol, f2d, bf)


def _inproj_bwd(dq, dk, dv, qkv, df, dlx, dlg, x2d, dx2, g1, gq2, gk2, wcat):
    T = x2d.shape[0]
    tm = TOKEN_TILE

    def body(dq_ref, dk_ref, dv_ref, qkv_ref, df_ref, dlx_ref, dlg_ref, x_ref, dx2_ref, g1_ref, gq_ref, gk_ref, w_ref,
             gx_ref, dp_ref, h_ref, dg1_ref, dgq_ref, dgk_ref):
        @pl.when(pl.program_id(0) == 0)
        def _():
            dg1_ref[...] = jnp.zeros_like(dg1_ref)
            dgq_ref[...] = jnp.zeros_like(dgq_ref)
            dgk_ref[...] = jnp.zeros_like(dgk_ref)

        lo = _lo_mask()

        def head_norm_bwd(t, g2, dy):
            rr = lax.rsqrt(_half_sums(t * t, lo) * (1.0 / HEAD_DIM) + NORM_EPS)
            th = t * rr
            dth = dy * g2
            mm = _half_sums(dth * th, lo) * (1.0 / HEAD_DIM)
            return rr * (dth - th * mm), jnp.sum(dy * th, axis=0, keepdims=True)

        dgq = jnp.zeros((1, PAIR), F32)
        dgk = jnp.zeros((1, PAIR), F32)
        for p in range(N_PAIR):
            cq = slice(PAIR * p, PAIR * (p + 1))
            ck = slice(ATT_WIDTH + PAIR * p, ATT_WIDTH + PAIR * (p + 1))
            dqp, g_ = head_norm_bwd(qkv_ref[:, cq], gq_ref[...], dq_ref[:, cq] * QK_SCALE)
            dgq = dgq + g_
            dp_ref[:, cq] = dqp.astype(MXU_DTYPE)
            dkp, g_ = head_norm_bwd(qkv_ref[:, ck], gk_ref[...], dk_ref[:, cq])
            dgk = dgk + g_
            dp_ref[:, ck] = dkp.astype(MXU_DTYPE)
        dgq_ref[...] += dgq
        dgk_ref[...] += dgk
        f0 = 3 * ATT_WIDTH
        dp_ref[:, 2 * ATT_WIDTH:f0] = dv_ref[...].astype(MXU_DTYPE)
        dp_ref[:, f0:f0 + F_PAD] = df_ref[...].astype(MXU_DTYPE)
        dp_ref[:, f0 + F_PAD:f0 + F_PAD + LRU_WIDTH] = dlx_ref[...].astype(MXU_DTYPE)
        dp_ref[:, f0 + F_PAD + LRU_WIDTH:] = dlg_ref[...].astype(MXU_DTYPE)
        dh = _nt(dp_ref[...], w_ref[...])
        x = x_ref[...]
        r = lax.rsqrt(jnp.mean(x * x, axis=-1, keepdims=True) + NORM_EPS)
        xh = x * r
        h_ref[...] = (xh * g1_ref[...]).astype(MXU_DTYPE)
        dg1_ref[...] += jnp.sum(dh * xh, axis=0, keepdims=True)
        dxh = dh * g1_ref[...]
        gx_ref[...] = dx2_ref[...] + r * (dxh - xh * jnp.mean(dxh * xh, axis=-1, keepdims=True))

    row = lambda w: pl.BlockSpec((tm, w), lambda i: (i, 0))
    return pl.pallas_call(
        body, name="inproj_bwd", grid=(T // tm,),
        in_specs=[row(ATT_WIDTH), row(ATT_WIDTH), row(ATT_WIDTH), row(3 * ATT_WIDTH), row(F_PAD), row(LRU_WIDTH),
                  row(LRU_WIDTH), row(D_MODEL), row(D_MODEL), _const((1, D_MODEL)), _const((1, PAIR)), _const((1, PAIR)),
                  _const((D_MODEL, N_CAT))],
        out_specs=[row(D_MODEL), row(N_CAT), row(D_MODEL), _const((1, D_MODEL)), _const((1, PAIR)), _const((1, PAIR))],
        out_shape=[jax.ShapeDtypeStruct((T, D_MODEL), F32), jax.ShapeDtypeStruct((T, N_CAT), MXU_DTYPE),
                   jax.ShapeDtypeStruct((T, D_MODEL), MXU_DTYPE), jax.ShapeDtypeStruct((1, D_MODEL), F32),
                   jax.ShapeDtypeStruct((1, PAIR), F32), jax.ShapeDtypeStruct((1, PAIR), F32)],
        compiler_params=_params(("arbitrary",), VMEM_LARGE),
    )(dq, dk, dv, qkv, df, dlx, dlg, x2d, dx2, g1, gq2, gk2, wcat)


def _row_block(rows):
    for rb in (256, 176, 128, 64, 32, 16, 8):
        if rows % rb == 0:
            return rb
    return rows


def _sum_slabs(recv, name):
    _, rows, cols = recv.shape
    rb = _row_block(rows)

    def body(r_ref, o_ref):
        o_ref[...] = ((r_ref[0] + r_ref[1]) + r_ref[2]) + r_ref[3]

    return pl.pallas_call(
        body, name=name, grid=(rows // rb,),
        in_specs=[pl.BlockSpec((4, rb, cols), lambda i: (0, i, 0))],
        out_specs=pl.BlockSpec((rb, cols), lambda i: (i, 0)),
        out_shape=jax.ShapeDtypeStruct((rows, cols), F32),
        compiler_params=_params(("parallel",)),
    )(recv)


def _adamw_math(w, g, m, v):
    m = ADAM_B1 * m + (1.0 - ADAM_B1) * g
    v = ADAM_B2 * v + (1.0 - ADAM_B2) * (g * g)
    m_hat = m / (1.0 - ADAM_B1 ** ADAM_STEP)
    v_hat = v / (1.0 - ADAM_B2 ** ADAM_STEP)
    delta = -ADAM_LR * (m_hat / (jnp.sqrt(v_hat) + ADAM_EPS) + ADAM_WD * w)
    return delta, m, v


def _adamw_pair(mine, theirs, w, m, v, name):
    rows, cols = w.shape
    rb = _row_block(rows)

    def body(a_ref, b_ref, w_ref, m_ref, v_ref, g_ref, d_ref, nm_ref, nv_ref):
        g = a_ref[...] + b_ref[...]
        g_ref[...] = g
        d_ref[...], nm_ref[...], nv_ref[...] = _adamw_math(w_ref[...], g, m_ref[...], v_ref[...])

    blk = pl.BlockSpec((rb, cols), lambda i: (i, 0))
    return pl.pallas_call(
        body, name=name, grid=(rows // rb,),
        in_specs=[blk] * 5, out_specs=[blk] * 4,
        out_shape=[jax.ShapeDtypeStruct((rows, cols), F32)] * 4,
        compiler_params=_params(("parallel",)),
    )(mine, theirs, w, m, v)


def _sum_packs(recv):
    _, rows, cols = recv.shape

    def body(r_ref, o_ref):
        acc = r_ref[0]
        for d in range(1, 8):
            acc = acc + r_ref[d]
        o_ref[...] = acc

    return pl.pallas_call(
        body, name="sum_packs", grid=(1,),
        in_specs=[_const(recv.shape)], out_specs=_const((rows, cols)),
        out_shape=jax.ShapeDtypeStruct((rows, cols), F32),
        compiler_params=_params(("arbitrary",)),
    )(recv)


def _adamw_small(w, g, m, v):
    rows, cols = w.shape

    def body(w_ref, g_ref, m_ref, v_ref, d_ref, nm_ref, nv_ref):
        d_ref[...], nm_ref[...], nv_ref[...] = _adamw_math(w_ref[...], g_ref[...], m_ref[...], v_ref[...])

    blk = _const((rows, cols))
    return pl.pallas_call(
        body, name="adamw_small", grid=(1,),
        in_specs=[blk] * 4, out_specs=[blk] * 3,
        out_shape=[jax.ShapeDtypeStruct((rows, cols), F32)] * 3,
        compiler_params=_params(("arbitrary",)),
    )(w, g, m, v)


SMALL = ["norm1_g", "q_norm_g", "k_norm_g", "b_f", "conv_b", "w_a", "b_a", "w_x", "b_x", "lam",
         "attn_out_g", "lru_out_g", "norm2_g"]


def _pack(parts, rows=None):
    flat = jnp.concatenate([p.reshape(-1) for p in parts])
    n = flat.shape[0]
    total = -(-n // 1024) * 1024 if rows is None else rows * 128
    return jnp.pad(flat, (0, total - n)).reshape(-1, 128)


def _unpack(pack, shapes):
    flat = pack.reshape(-1)
    out, off = [], 0
    for s in shapes:
        n = math.prod(s)
        out.append(flat[off:off + n].reshape(s))
        off += n
    return out


def _block_diag(w):
    eye = jnp.eye(LRU_BLOCKS, dtype=w.dtype)
    return (w[:, :, None, :] * eye[:, None, :, None]).reshape(LRU_WIDTH, LRU_WIDTH)


def _diag_blocks(m):
    m4 = m.reshape(LRU_BLOCKS, LRU_WIDTH // LRU_BLOCKS, LRU_BLOCKS, LRU_WIDTH // LRU_BLOCKS)
    return jnp.stack([m4[n, :, n, :] for n in range(LRU_BLOCKS)])


def kernel(x, norm1_g, w_in, q_norm_g, k_norm_g, b_f, conv_w, conv_b, w_a, b_a, w_x, b_x, lam, attn_out_g, lru_out_g, w_out, norm2_g, w_gate, w_up, w_down, loss_target, m_norm1_g, m_w_in, m_q_norm_g, m_k_norm_g, m_b_f, m_conv_w, m_conv_b, m_w_a, m_b_a, m_w_x, m_b_x, m_lam, m_attn_out_g, m_lru_out_g, m_w_out, m_norm2_g, m_w_gate, m_w_up, m_w_down, v_norm1_g, v_w_in, v_q_norm_g, v_k_norm_g, v_b_f, v_conv_w, v_conv_b, v_w_a, v_b_a, v_w_x, v_b_x, v_lam, v_attn_out_g, v_lru_out_g, v_w_out, v_norm2_g, v_w_gate, v_w_up, v_w_down):
    args = dict(locals())
    bl, seq, _ = x.shape
    T = bl * seq
    tq = min(ATT_TILE, seq)
    nq = seq // tq
    dff = w_gate.shape[2] * 4
    my_chip = 2 * lax.axis_index("x") + lax.axis_index("y")

    g_in, g_out, g_gate, g_up, g_down, g_cw = _gather_over_chips(
        [w_in[0].astype(MXU_DTYPE), w_out[0].astype(MXU_DTYPE), w_gate[0].astype(MXU_DTYPE),
         w_up[0].astype(MXU_DTYPE), w_down[0].astype(MXU_DTYPE), conv_w[0]])
    col_cat = lambda g: jnp.transpose(g, (1, 0, 2)).reshape(g.shape[1], -1)
    w_in_full = col_cat(g_in)
    f0 = 3 * ATT_WIDTH
    wcat = jnp.concatenate([w_in_full[:, :f0], jnp.pad(w_in_full[:, f0:f0 + HEADS], ((0, 0), (0, F_PAD - HEADS))),
                            w_in_full[:, f0 + HEADS:]], axis=1)
    wout_full = g_out.reshape(D_MODEL, D_MODEL)
    wg_full, wu_full = col_cat(g_gate), col_cat(g_up)
    wd_full = g_down.reshape(dff, D_MODEL)
    cw_full = col_cat(g_cw)
    wa_bd = _block_diag(w_a[0]).astype(MXU_DTYPE)
    wx_bd = _block_diag(w_x[0]).astype(MXU_DTYPE)
    gq2 = jnp.tile(q_norm_g, (1, 2))
    gk2 = jnp.tile(k_norm_g, (1, 2))
    bf_pad = jnp.pad(b_f, ((0, 0), (0, F_PAD - HEADS)))

    x2d = x.reshape(T, D_MODEL)
    target2d = loss_target.reshape(T, D_MODEL)

    qkv, qn, kn, vb, f2d, lx, lg = _inproj(x2d, norm1_g, wcat, gq2, gk2)
    fcol = _forget_cumsum(f2d, bf_pad, bl, seq)
    frow = jnp.transpose(fcol.reshape(bl, seq, F_PAD)[:, :, :HEADS], (0, 2, 1)).reshape(bl, N_PAIR, 2, seq)
    fstart = frow[:, :, :, ::tq].reshape(-1)
    att, att_x, lse = _attn_fwd(qn, kn, vb, frow, fstart, bl, seq)
    h, rec = _lru_fwd(lx, lg, cw_full, conv_b, wa_bd, b_a, wx_bd, b_x, lam, bl, seq)
    x2 = _outproj(x2d, att, rec, attn_out_g, lru_out_g, wout_full)
    gt, up, dy, sq_err = _mlp_fwd(x2, norm2_g, wg_full, wu_full, wd_full, target2d)
    loss = lax.psum(sq_err[0, 0] * (0.5 / D_MODEL), ("x", "y", "c"))

    dx2, dx2b, dgtb, dupb, actb, h2b, dyb, dg2 = _mlp_bwd(dy, x2, gt, up, norm2_g, wg_full, wu_full, wd_full)
    dw_down = _matmul_tn(actb, dyb, 512, "dw_down")
    dw_gate = _matmul_tn(h2b, dgtb, dff // 2, "dw_gate")
    dw_up = _matmul_tn(h2b, dupb, dff // 2, "dw_up")
    dattb, delta, drec, mixb, dga, dgr = _outproj_bwd(dx2b, att, att_x, rec, attn_out_g, lru_out_g, wout_full)
    dw_out = _matmul_tn(mixb, dx2b, D_MODEL, "dw_out")
    dlx, dlg, dwa_bd, dwx_bd, lru_small = _lru_bwd(drec, lg, h, lx, cw_full, conv_b, wa_bd, b_a, wx_bd, b_x, lam, bl, seq)
    dq, dk, dv, dfrow = _attn_bwd(qn, kn, vb, dattb, lse, delta, frow, fstart, bl, seq)
    dfcol = jnp.pad(jnp.transpose(dfrow.reshape(bl, HEADS, seq), (0, 2, 1)), ((0, 0), (0, 0), (0, F_PAD - HEADS)))
    df, dbf = _forget_bwd(dfcol.reshape(T, F_PAD), f2d, bf_pad, bl, seq)
    grad_x, dprojb, h1b, dg1, dgq, dgk = _inproj_bwd(dq, dk, dv, qkv, df, dlx, dlg, x2d, dx2, norm1_g, gq2, gk2, wcat)
    dwcat = _matmul_tn(h1b, dprojb, N_CAT // 3, "dw_in")
    dw_in = jnp.concatenate([dwcat[:, :f0 + HEADS], dwcat[:, f0 + F_PAD:]], axis=1)

    col_split = lambda g: jnp.transpose(g.reshape(g.shape[0], 4, -1), (1, 0, 2))
    slabs = [col_split(dw_in), dw_out.reshape(4, D_MODEL // 4, D_MODEL), col_split(dw_gate), col_split(dw_up),
             dw_down.reshape(4, dff // 4, D_MODEL)]
    small_grads = {
        "norm1_g": dg1, "q_norm_g": dgq[:, :HEAD_DIM] + dgq[:, HEAD_DIM:], "k_norm_g": dgk[:, :HEAD_DIM] + dgk[:, HEAD_DIM:],
        "b_f": dbf[:, :HEADS], "conv_b": lru_small[3:4], "w_a": _diag_blocks(dwa_bd)[None], "b_a": lru_small[0:1],
        "w_x": _diag_blocks(dwx_bd)[None], "b_x": lru_small[1:2], "lam": lru_small[2:3],
        "attn_out_g": dga, "lru_out_g": dgr, "norm2_g": dg2}
    pack = _pack([small_grads[n] for n in SMALL] + [lru_small[4:8]])
    *recv, recv_pack = _exchange_grads(slabs, pack)
    big = ["w_in", "w_out", "w_gate", "w_up", "w_down"]
    part = [_sum_slabs(r, "sum_" + n) for r, n in zip(recv, big)]
    theirs = _swap_with_sibling(part)
    out = {}
    for n, a, b_ in zip(big, part, theirs):
        g, d, nm, nv = _adamw_pair(a, b_, args[n][0], args["m_" + n][0], args["v_" + n][0], "adamw_" + n)
        out[n] = (g[None], d[None], nm[None], nv[None])

    small_shapes = [args[n].shape for n in SMALL]
    red = _sum_packs(recv_pack)
    g_small = _unpack(red, small_shapes + [(CONV_WIDTH, LRU_WIDTH)])
    g_cw_mine = lax.dynamic_slice_in_dim(g_small[-1], my_chip * (LRU_WIDTH // 4), LRU_WIDTH // 4, axis=1)[None]
    g_list = g_small[:-1] + [g_cw_mine]
    names = SMALL + ["conv_w"]
    rows = pack.shape[0]
    d_p, m_p, v_p = _adamw_small(_pack([args[n] for n in names], rows), _pack(g_list, rows),
                                 _pack([args["m_" + n] for n in names], rows), _pack([args["v_" + n] for n in names], rows))
    shapes = [args[n].shape for n in names]
    for n, g, d, nm, nv in zip(names, g_list, _unpack(d_p, shapes), _unpack(m_p, shapes), _unpack(v_p, shapes)):
        out[n] = (g, d, nm, nv)

    order = ["norm1_g", "w_in", "q_norm_g", "k_norm_g", "b_f", "conv_w", "conv_b", "w_a", "b_a", "w_x", "b_x", "lam",
             "attn_out_g", "lru_out_g", "w_out", "norm2_g", "w_gate", "w_up", "w_down"]
    return (loss, grad_x.reshape(bl, seq, D_MODEL), *[out[n][0] for n in order], *[out[n][1] for n in order],
            *[out[n][2] for n in order], *[out[n][3] for n in order])
```

```python
import functools
import math

import jax
import jax.numpy as jnp
from jax import lax
from jax.experimental import pallas as pl
from jax.experimental.pallas import tpu as pltpu

F32 = jnp.float32
MXU_DTYPE = jnp.bfloat16
MESH = pl.DeviceIdType.MESH

D_MODEL = 1024
ATT_WIDTH = 512
LRU_WIDTH = 512
HEADS = 8
HEAD_DIM = 64
PAIR = 2 * HEAD_DIM
N_PAIR = HEADS // 2
LRU_BLOCKS = 8
CONV_WIDTH = 4
LRU_C = 8.0
NORM_EPS = 1e-6
QK_SCALE = 1.0 / math.sqrt(HEAD_DIM)
F_PAD = 128
N_CAT = 3 * ATT_WIDTH + F_PAD + 2 * LRU_WIDTH
NEG = -1e30

ADAM_LR, ADAM_B1, ADAM_B2, ADAM_EPS, ADAM_WD, ADAM_STEP = 0.001, 0.9, 0.999, 1e-08, 0.01, 10

TOKEN_TILE = 256
ATT_TILE = 512
LRU_TILE = 256
VMEM_SMALL = 32 * 1024 * 1024
VMEM_LARGE = 56 * 1024 * 1024


def _params(sem, vmem=VMEM_SMALL):
    return pltpu.CompilerParams(dimension_semantics=sem, vmem_limit_bytes=vmem)


def _const(shape):
    nd = len(shape)
    return pl.BlockSpec(shape, lambda *_: (0,) * nd)


def _sigmoid(x):
    return 1.0 / (1.0 + jnp.exp(-x))


def _half_sums(t, lo):
    s_lo = jnp.sum(jnp.where(lo, t, 0.0), axis=-1, keepdims=True)
    s_hi = jnp.sum(jnp.where(lo, 0.0, t), axis=-1, keepdims=True)
    return jnp.where(lo, s_lo, s_hi)


def _lo_mask():
    return lax.broadcasted_iota(jnp.int32, (1, PAIR), 1) < HEAD_DIM


def _other_chips(x, y):
    return [(1 - x, y), (x, 1 - y), (1 - x, 1 - y)]


def _chip_copies(ins, outs, send_sems, recv_sems, loc_sems, scatter):
    x, y, c = lax.axis_index("x"), lax.axis_index("y"), lax.axis_index("c")
    me = 2 * x + y
    copies = []
    for w in range(len(ins)):
        copies.append(pltpu.make_async_copy(ins[w].at[me] if scatter else ins[w], outs[w].at[me], loc_sems.at[w]))
        for k, (cx, cy) in enumerate(_other_chips(x, y)):
            copies.append(pltpu.make_async_remote_copy(
                src_ref=ins[w].at[2 * cx + cy] if scatter else ins[w], dst_ref=outs[w].at[me],
                send_sem=send_sems.at[3 * w + k], recv_sem=recv_sems.at[3 * w + k],
                device_id=(cx, cy, c), device_id_type=MESH))
    return copies


def _chip_sems(n):
    return [pltpu.SemaphoreType.DMA((3 * n,)), pltpu.SemaphoreType.DMA((3 * n,)), pltpu.SemaphoreType.DMA((n,))]


def _gather_over_chips(shards):
    n = len(shards)

    def body(*refs):
        copies = _chip_copies(refs[:n], refs[n:2 * n], *refs[2 * n:], scatter=False)
        for cp in copies:
            cp.start()
        for cp in copies:
            cp.wait()

    return pl.pallas_call(
        body, name="gather_weights",
        out_shape=[jax.ShapeDtypeStruct((4,) + s.shape, s.dtype) for s in shards],
        in_specs=[pl.BlockSpec(memory_space=pl.ANY)] * n,
        out_specs=[pl.BlockSpec(memory_space=pl.ANY)] * n,
        scratch_shapes=_chip_sems(n),
    )(*shards)


def _exchange_grads(slabs, pack):
    n = len(slabs)

    def body(*refs):
        ins, pack_in = refs[:n], refs[n]
        outs, pack_out = refs[n + 1:2 * n + 1], refs[2 * n + 1]
        send_sems, recv_sems, loc_sems, psend, precv = refs[2 * n + 2:]
        x, y, c = lax.axis_index("x"), lax.axis_index("y"), lax.axis_index("c")
        dev = 4 * x + 2 * y + c
        copies = _chip_copies(ins, outs, send_sems, recv_sems, loc_sems, scatter=True)
        copies.append(pltpu.make_async_copy(pack_in, pack_out.at[dev], loc_sems.at[n]))
        for k in range(1, 8):
            fx, fy, fc = (k >> 2) & 1, (k >> 1) & 1, k & 1
            tx = (1 - x) if fx else x
            ty = (1 - y) if fy else y
            tc = (1 - c) if fc else c
            copies.append(pltpu.make_async_remote_copy(
                src_ref=pack_in, dst_ref=pack_out.at[dev],
                send_sem=psend.at[k - 1], recv_sem=precv.at[k - 1],
                device_id=(tx, ty, tc), device_id_type=MESH))
        for cp in copies:
            cp.start()
        for cp in copies:
            cp.wait()

    return pl.pallas_call(
        body, name="exchange_grads",
        out_shape=[jax.ShapeDtypeStruct(s.shape, s.dtype) for s in slabs]
        + [jax.ShapeDtypeStruct((8,) + pack.shape, pack.dtype)],
        in_specs=[pl.BlockSpec(memory_space=pl.ANY)] * (n + 1),
        out_specs=[pl.BlockSpec(memory_space=pl.ANY)] * (n + 1),
        scratch_shapes=[pltpu.SemaphoreType.DMA((3 * n,)), pltpu.SemaphoreType.DMA((3 * n,)),
                        pltpu.SemaphoreType.DMA((n + 1,)),
                        pltpu.SemaphoreType.DMA((7,)), pltpu.SemaphoreType.DMA((7,))],
    )(*slabs, pack)


def _swap_with_sibling(arrs):
    n = len(arrs)

    def body(*refs):
        ins, outs = refs[:n], refs[n:2 * n]
        send_sems, recv_sems = refs[2 * n:]
        x, y, c = lax.axis_index("x"), lax.axis_index("y"), lax.axis_index("c")
        copies = []
        for w in range(n):
            cp = pltpu.make_async_remote_copy(
                src_ref=ins[w], dst_ref=outs[w], send_sem=send_sems.at[w], recv_sem=recv_sems.at[w],
                device_id=(x, y, 1 - c), device_id_type=MESH)
            cp.start()
            copies.append(cp)
        for cp in copies:
            cp.wait()

    return pl.pallas_call(
        body, name="swap_sibling",
        out_shape=[jax.ShapeDtypeStruct(a.shape, a.dtype) for a in arrs],
        in_specs=[pl.BlockSpec(memory_space=pl.ANY)] * n,
        out_specs=[pl.BlockSpec(memory_space=pl.ANY)] * n,
        scratch_shapes=[pltpu.SemaphoreType.DMA((n,)), pltpu.SemaphoreType.DMA((n,))],
    )(*arrs)


def _head_norm(t, g2, lo):
    rr = lax.rsqrt(_half_sums(t * t, lo) * (1.0 / HEAD_DIM) + NORM_EPS)
    return t * rr * g2


def _inproj(x2d, g1, wcat, gq2, gk2):
    T = x2d.shape[0]
    tm = TOKEN_TILE

    def body(x_ref, g1_ref, w_ref, gq_ref, gk_ref, qkv_ref, qn_ref, kn_ref, vb_ref, f_ref, lx_ref, lg_ref):
        x = x_ref[...]
        r = lax.rsqrt(jnp.mean(x * x, axis=-1, keepdims=True) + NORM_EPS)
        h = (x * r * g1_ref[...]).astype(MXU_DTYPE)
        proj = jnp.dot(h, w_ref[...], preferred_element_type=F32)
        qkv_ref[...] = proj[:, :3 * ATT_WIDTH]
        lo = _lo_mask()
        for p in range(N_PAIR):
            cols = slice(PAIR * p, PAIR * (p + 1))
            q = proj[:, PAIR * p:PAIR * (p + 1)]
            k = proj[:, ATT_WIDTH + PAIR * p:ATT_WIDTH + PAIR * (p + 1)]
            qn_ref[:, cols] = (_head_norm(q, gq_ref[...], lo) * QK_SCALE).astype(MXU_DTYPE)
            kn_ref[:, cols] = _head_norm(k, gk_ref[...], lo).astype(MXU_DTYPE)
        vb_ref[...] = proj[:, 2 * ATT_WIDTH:3 * ATT_WIDTH].astype(MXU_DTYPE)
        f0 = 3 * ATT_WIDTH
        f_ref[...] = proj[:, f0:f0 + F_PAD]
        lx_ref[...] = proj[:, f0 + F_PAD:f0 + F_PAD + LRU_WIDTH]
        lg_ref[...] = proj[:, f0 + F_PAD + LRU_WIDTH:]

    row = lambda w: pl.BlockSpec((tm, w), lambda i: (i, 0))
    return pl.pallas_call(
        body, name="inproj", grid=(T // tm,),
        in_specs=[row(D_MODEL), _const((1, D_MODEL)), _const((D_MODEL, N_CAT)), _const((1, PAIR)), _const((1, PAIR))],
        out_specs=[row(3 * ATT_WIDTH), row(ATT_WIDTH), row(ATT_WIDTH), row(ATT_WIDTH), row(F_PAD),
                   row(LRU_WIDTH), row(LRU_WIDTH)],
        out_shape=[jax.ShapeDtypeStruct((T, 3 * ATT_WIDTH), F32),
                   jax.ShapeDtypeStruct((T, ATT_WIDTH), MXU_DTYPE), jax.ShapeDtypeStruct((T, ATT_WIDTH), MXU_DTYPE),
                   jax.ShapeDtypeStruct((T, ATT_WIDTH), MXU_DTYPE), jax.ShapeDtypeStruct((T, F_PAD), F32),
                   jax.ShapeDtypeStruct((T, LRU_WIDTH), F32), jax.ShapeDtypeStruct((T, LRU_WIDTH), F32)],
        compiler_params=_params(("parallel",), VMEM_LARGE),
    )(x2d, g1, wcat, gq2, gk2)


def _forget_cumsum(f2d, bf, bl, seq):
    def body(z_ref, b_ref, o_ref):
        z = z_ref[...] + b_ref[...]
        lf = jnp.minimum(z, 0.0) - jnp.log(1.0 + jnp.exp(-jnp.abs(z)))
        row = lax.broadcasted_iota(jnp.int32, (seq, F_PAD), 0)
        k = 1
        while k < seq:
            lf = lf + jnp.where(row >= k, pltpu.roll(lf, k, 0), 0.0)
            k *= 2
        o_ref[...] = lf

    return pl.pallas_call(
        body, name="forget_cumsum", grid=(bl,),
        in_specs=[pl.BlockSpec((seq, F_PAD), lambda b: (b, 0)), _const((1, F_PAD))],
        out_specs=pl.BlockSpec((seq, F_PAD), lambda b: (b, 0)),
        out_shape=jax.ShapeDtypeStruct(f2d.shape, F32),
        compiler_params=_params(("parallel",)),
    )(f2d, bf)


def _attn_fwd(qn, kn, vb, frow, fstart, bl, seq, shards):
    tq = min(ATT_TILE, seq)
    nq = seq // tq
    T = bl * seq
    n = len(shards)

    def body(fs_ref, q_ref, k_ref, v_ref, fr_ref, *rest):
        g_in, (o_ref, ox_ref, lse_ref), g_out, sems = rest[:n], rest[n:n + 3], rest[n + 3:2 * n + 3], rest[2 * n + 3:]
        b, p, i = pl.program_id(0), pl.program_id(1), pl.program_id(2)
        copies = _chip_copies(g_in, g_out, *sems, scatter=False)

        @pl.when((b == 0) & (p == 0) & (i == 0))
        def _():
            for cp in copies:
                cp.start()

        lane = lax.broadcasted_iota(jnp.int32, (1, PAIR), 1)
        rows = lax.broadcasted_iota(jnp.int32, (tq, tq), 0)
        cols = lax.broadcasted_iota(jnp.int32, (tq, tq), 1)
        causal = cols <= rows
        q = q_ref[...]
        out = jnp.zeros((tq, PAIR), F32)
        out_x = jnp.zeros((tq, PAIR), F32)
        lse = jnp.zeros((tq, PAIR), F32)
        for hh in range(2):
            hm = (lane >= HEAD_DIM * hh) & (lane < HEAD_DIM * (hh + 1))
            qh = jnp.where(hm, q, jnp.zeros_like(q))
            shift = fs_ref[((b * N_PAIR + p) * 2 + hh) * nq + i]

            def block(j, carry, masked, qh=qh, hm=hm, shift=shift, hh=hh):
                m, l, acc, acc_lo = carry
                start = pl.multiple_of(j * tq, tq)
                k = k_ref[pl.ds(start, tq), :]
                v = v_ref[pl.ds(start, tq), :]
                s = lax.dot_general(qh, k, (((1,), (1,)), ((), ())), preferred_element_type=F32)
                s = s - (fr_ref[0, 0, hh:hh + 1, pl.ds(start, tq)] - shift)
                if masked:
                    s = jnp.where(causal, s, NEG)
                m_new = jnp.maximum(m, jnp.max(s, axis=-1, keepdims=True))
                alpha = jnp.exp(m - m_new)
                pe = jnp.exp(s - m_new)
                l = alpha * l + jnp.sum(pe, axis=-1, keepdims=True)
                vh = jnp.where(hm, v, jnp.zeros_like(v))
                pb = pe.astype(MXU_DTYPE)
                p_lo = (pe - pb.astype(F32)).astype(MXU_DTYPE)
                acc = alpha * acc + jnp.dot(pb, vh, preferred_element_type=F32)
                acc_lo = alpha * acc_lo + jnp.dot(p_lo, vh, preferred_element_type=F32)
                return m_new, l, acc, acc_lo

            carry = (jnp.full((tq, 1), NEG, F32), jnp.zeros((tq, 1), F32), jnp.zeros((tq, PAIR), F32),
                     jnp.zeros((tq, PAIR), F32))
            carry = lax.fori_loop(0, i, functools.partial(block, masked=False), carry)
            m, l, acc, acc_lo = block(i, carry, True)
            out = out + acc / l
            out_x = out_x + (acc + acc_lo) / l
            lse = jnp.where(hm, m + jnp.log(l), lse)
        o_ref[...] = out
        ox_ref[...] = out_x
        lse_ref[...] = lse

        @pl.when((b == bl - 1) & (p == N_PAIR - 1) & (i == nq - 1))
        def _():
            for cp in copies:
                cp.wait()

    blk = pl.BlockSpec((tq, PAIR), lambda b, p, i: (b * nq + i, p))
    full = pl.BlockSpec((seq, PAIR), lambda b, p, i: (b, p))
    return pl.pallas_call(
        body, name="attn_fwd", grid=(bl, N_PAIR, nq),
        in_specs=[pl.BlockSpec(memory_space=pltpu.SMEM), blk, full, full,
                  pl.BlockSpec((1, 1, 2, seq), lambda b, p, i: (b, p, 0, 0))] + [pl.BlockSpec(memory_space=pl.ANY)] * n,
        out_specs=[blk, blk, blk] + [pl.BlockSpec(memory_space=pl.ANY)] * n,
        out_shape=[jax.ShapeDtypeStruct((T, ATT_WIDTH), F32)] * 3
        + [jax.ShapeDtypeStruct((4,) + s.shape, s.dtype) for s in shards],
        scratch_shapes=_chip_sems(n),
        compiler_params=_params(("arbitrary", "arbitrary", "arbitrary")),
    )(fstart, qn, kn, vb, frow, *shards)


def _conv_taps(lx, prev8, cw, cb):
    xs = jnp.concatenate([prev8, lx], axis=0)
    shifted = [lx] + [pltpu.roll(xs, k, 0)[8:] for k in range(1, CONV_WIDTH)]
    xc = cb + cw[CONV_WIDTH - 1:CONV_WIDTH] * lx
    for k in range(1, CONV_WIDTH):
        xc = xc + cw[CONV_WIDTH - 1 - k:CONV_WIDTH - k] * shifted[k]
    return xc, shifted


def _lru_gates(xc, wa, ba, wx, bx, lam):
    xb = xc.astype(MXU_DTYPE)
    r = _sigmoid(jnp.dot(xb, wa, preferred_element_type=F32) + ba)
    ig = _sigmoid(jnp.dot(xb, wx, preferred_element_type=F32) + bx)
    sp = jnp.maximum(-lam, 0.0) + jnp.log(1.0 + jnp.exp(-jnp.abs(lam)))
    log_a = -LRU_C * r * sp
    a = jnp.exp(log_a)
    th = jnp.tanh(log_a)
    mult = jnp.sqrt(-2.0 * th / (1.0 - th))
    return r, ig, sp, a, mult


def _gelu_parts(x):
    c0 = math.sqrt(2.0 / math.pi)
    t = jnp.tanh(c0 * (x + 0.044715 * x * x * x))
    g = 0.5 * x * (1.0 + t)
    dg = 0.5 * (1.0 + t) + 0.5 * x * (1.0 - t * t) * c0 * (1.0 + 3.0 * 0.044715 * x * x)
    return g, dg


def _lru_fwd(lx, lg, cw, cb, wa, ba, wx, bx, lam, bl, seq):
    tc = min(LRU_TILE, seq)
    nc = seq // tc
    T = bl * seq

    def body(lx_ref, lxp_ref, lg_ref, cw_ref, cb_ref, wa_ref, ba_ref, wx_ref, bx_ref, lam_ref,
             h_ref, rec_ref, hc_ref):
        i = pl.program_id(1)

        @pl.when(i == 0)
        def _():
            hc_ref[...] = jnp.zeros_like(hc_ref)

        lxv = lx_ref[...]
        prev8 = jnp.where(i > 0, lxp_ref[...], 0.0)
        xc, _ = _conv_taps(lxv, prev8, cw_ref[...], cb_ref[...])
        _, ig, _, a, mult = _lru_gates(xc, wa_ref[...], ba_ref[...], wx_ref[...], bx_ref[...], lam_ref[...])
        u = mult * (ig * xc)
        row = lax.broadcasted_iota(jnp.int32, (tc, LRU_WIDTH), 0)
        A, B = a, u
        k = 1
        while k < tc:
            a_s = jnp.where(row >= k, pltpu.roll(A, k, 0), 1.0)
            b_s = jnp.where(row >= k, pltpu.roll(B, k, 0), 0.0)
            B = A * b_s + B
            A = A * a_s
            k *= 2
        h = A * hc_ref[0:1, :] + B
        hc_ref[0:1, :] = h[tc - 1:tc, :]
        h_ref[...] = h
        g, _ = _gelu_parts(lg_ref[...])
        rec_ref[...] = h * g

    tile = pl.BlockSpec((tc, LRU_WIDTH), lambda b, i: (b * nc + i, 0))
    prev = pl.BlockSpec((8, LRU_WIDTH), lambda b, i: (jnp.maximum((b * seq + i * tc) // 8 - 1, 0), 0))
    vec = _const((1, LRU_WIDTH))
    mat = _const((LRU_WIDTH, LRU_WIDTH))
    return pl.pallas_call(
        body, name="lru_fwd", grid=(bl, nc),
        in_specs=[tile, prev, tile, _const((CONV_WIDTH, LRU_WIDTH)), vec, mat, vec, mat, vec, vec],
        out_specs=[tile, tile],
        out_shape=[jax.ShapeDtypeStruct((T, LRU_WIDTH), F32), jax.ShapeDtypeStruct((T, LRU_WIDTH), F32)],
        scratch_shapes=[pltpu.VMEM((8, LRU_WIDTH), F32)],
        compiler_params=_params(("arbitrary", "arbitrary")),
    )(lx, lx, lg, cw, cb, wa, ba, wx, bx, lam)


def _outproj(x2d, att, rec, ga, gr, wout):
    T = x2d.shape[0]
    tm = TOKEN_TILE

    def body(x_ref, a_ref, r_ref, ga_ref, gr_ref, w_ref, o_ref):
        a = a_ref[...]
        rc = r_ref[...]
        na = a * lax.rsqrt(jnp.mean(a * a, axis=-1, keepdims=True) + NORM_EPS) * ga_ref[...]
        nr = rc * lax.rsqrt(jnp.mean(rc * rc, axis=-1, keepdims=True) + NORM_EPS) * gr_ref[...]
        o_ref[...] = (x_ref[...]
                      + jnp.dot(na.astype(MXU_DTYPE), w_ref[:ATT_WIDTH, :], preferred_element_type=F32)
                      + jnp.dot(nr.astype(MXU_DTYPE), w_ref[ATT_WIDTH:, :], preferred_element_type=F32))

    row = lambda w: pl.BlockSpec((tm, w), lambda i: (i, 0))
    return pl.pallas_call(
        body, name="outproj", grid=(T // tm,),
        in_specs=[row(D_MODEL), row(ATT_WIDTH), row(LRU_WIDTH), _const((1, ATT_WIDTH)), _const((1, LRU_WIDTH)),
                  _const((D_MODEL, D_MODEL))],
        out_specs=row(D_MODEL),
        out_shape=jax.ShapeDtypeStruct((T, D_MODEL), F32),
        compiler_params=_params(("parallel",)),
    )(x2d, att, rec, ga, gr, wout)


def _mlp_fwd(x2, g2, wg, wu, wd, target):
    T = x2.shape[0]
    tm = TOKEN_TILE
    dff = wg.shape[1]

    def body(x_ref, g_ref, wg_ref, wu_ref, wd_ref, t_ref, gt_ref, up_ref, dy_ref, loss_ref):
        @pl.when(pl.program_id(0) == 0)
        def _():
            loss_ref[...] = jnp.zeros_like(loss_ref)

        x = x_ref[...]
        r = lax.rsqrt(jnp.mean(x * x, axis=-1, keepdims=True) + NORM_EPS)
        h = (x * r * g_ref[...]).astype(MXU_DTYPE)
        gt = jnp.dot(h, wg_ref[...], preferred_element_type=F32)
        up = jnp.dot(h, wu_ref[...], preferred_element_type=F32)
        gt_ref[...] = gt
        up_ref[...] = up
        act = (gt * _sigmoid(gt) * up).astype(MXU_DTYPE)
        y = x + jnp.dot(act, wd_ref[...], preferred_element_type=F32)
        e = y - t_ref[...]
        dy_ref[...] = e * (1.0 / D_MODEL)
        loss_ref[...] += jnp.sum(e * e)

    row = lambda w: pl.BlockSpec((tm, w), lambda i: (i, 0))
    return pl.pallas_call(
        body, name="mlp_fwd", grid=(T // tm,),
        in_specs=[row(D_MODEL), _const((1, D_MODEL)), _const((D_MODEL, dff)), _const((D_MODEL, dff)),
                  _const((dff, D_MODEL)), row(D_MODEL)],
        out_specs=[row(dff), row(dff), row(D_MODEL), _const((8, 128))],
        out_shape=[jax.ShapeDtypeStruct((T, dff), F32), jax.ShapeDtypeStruct((T, dff), F32),
                   jax.ShapeDtypeStruct((T, D_MODEL), F32), jax.ShapeDtypeStruct((8, 128), F32)],
        compiler_params=_params(("arbitrary",), VMEM_LARGE),
    )(x2, g2, wg, wu, wd, target)


def _nt(a, b):
    return lax.dot_general(a, b, (((1,), (1,)), ((), ())), preferred_element_type=F32)


def _tn(a, b):
    return lax.dot_general(a, b, (((0,), (0,)), ((), ())), preferred_element_type=F32)


def _mlp_bwd(dy, x2, gt, up, g2, wg, wu, wd):
    T = x2.shape[0]
    tm = TOKEN_TILE
    dff = wg.shape[1]

    def body(dy_ref, x_ref, gt_ref, up_ref, g_ref, wg_ref, wu_ref, wd_ref,
             dx_ref, dxb_ref, dgt_ref, dup_ref, act_ref, h_ref, dyb_ref, dg_ref):
        @pl.when(pl.program_id(0) == 0)
        def _():
            dg_ref[...] = jnp.zeros_like(dg_ref)

        dy_v = dy_ref[...]
        dyb = dy_v.astype(MXU_DTYPE)
        dyb_ref[...] = dyb
        x = x_ref[...]
        r = lax.rsqrt(jnp.mean(x * x, axis=-1, keepdims=True) + NORM_EPS)
        xh = x * r
        h_ref[...] = (xh * g_ref[...]).astype(MXU_DTYPE)
        gt_v = gt_ref[...]
        up_v = up_ref[...]
        sg = _sigmoid(gt_v)
        silu = gt_v * sg
        act_ref[...] = (silu * up_v).astype(MXU_DTYPE)
        dact = _nt(dyb, wd_ref[...])
        dup = (dact * silu).astype(MXU_DTYPE)
        dgt = (dact * up_v * (sg * (1.0 + gt_v * (1.0 - sg)))).astype(MXU_DTYPE)
        dup_ref[...] = dup
        dgt_ref[...] = dgt
        dh = _nt(dgt, wg_ref[...]) + _nt(dup, wu_ref[...])
        dg_ref[...] += jnp.sum(dh * xh, axis=0, keepdims=True)
        dxh = dh * g_ref[...]
        dx = dy_v + r * (dxh - xh * jnp.mean(dxh * xh, axis=-1, keepdims=True))
        dx_ref[...] = dx
        dxb_ref[...] = dx.astype(MXU_DTYPE)

    row = lambda w: pl.BlockSpec((tm, w), lambda i: (i, 0))
    return pl.pallas_call(
        body, name="mlp_bwd", grid=(T // tm,),
        in_specs=[row(D_MODEL), row(D_MODEL), row(dff), row(dff), _const((1, D_MODEL)),
                  _const((D_MODEL, dff)), _const((D_MODEL, dff)), _const((dff, D_MODEL))],
        out_specs=[row(D_MODEL), row(D_MODEL), row(dff), row(dff), row(dff), row(D_MODEL), row(D_MODEL),
                   _const((1, D_MODEL))],
        out_shape=[jax.ShapeDtypeStruct((T, D_MODEL), F32), jax.ShapeDtypeStruct((T, D_MODEL), MXU_DTYPE),
                   jax.ShapeDtypeStruct((T, dff), MXU_DTYPE), jax.ShapeDtypeStruct((T, dff), MXU_DTYPE),
                   jax.ShapeDtypeStruct((T, dff), MXU_DTYPE), jax.ShapeDtypeStruct((T, D_MODEL), MXU_DTYPE),
                   jax.ShapeDtypeStruct((T, D_MODEL), MXU_DTYPE), jax.ShapeDtypeStruct((1, D_MODEL), F32)],
        compiler_params=_params(("arbitrary",), VMEM_LARGE),
    )(dy, x2, gt, up, g2, wg, wu, wd)


def _matmul_tn(a, b, tn, name):
    T, K = a.shape
    N = b.shape[1]
    tt = min(512, T)

    def body(a_ref, b_ref, o_ref):
        @pl.when(pl.program_id(1) == 0)
        def _():
            o_ref[...] = jnp.zeros_like(o_ref)

        o_ref[...] += _tn(a_ref[...], b_ref[...])

    return pl.pallas_call(
        body, name=name, grid=(N // tn, T // tt),
        in_specs=[pl.BlockSpec((tt, K), lambda n, t: (t, 0)), pl.BlockSpec((tt, tn), lambda n, t: (t, n))],
        out_specs=pl.BlockSpec((K, tn), lambda n, t: (0, n)),
        out_shape=jax.ShapeDtypeStruct((K, N), F32),
        compiler_params=_params(("parallel", "arbitrary"), VMEM_LARGE),
    )(a, b)


def _outproj_bwd(dx2b, att, att_x, rec, ga, gr, wout):
    T = att.shape[0]
    tm = TOKEN_TILE

    def body(dx_ref, a_ref, ax_ref, r_ref, ga_ref, gr_ref, w_ref, datt_ref, delta_ref, drec_ref, mix_ref, dga_ref, dgr_ref):
        @pl.when(pl.program_id(0) == 0)
        def _():
            dga_ref[...] = jnp.zeros_like(dga_ref)
            dgr_ref[...] = jnp.zeros_like(dgr_ref)

        dmix = _nt(dx_ref[...], w_ref[...])

        def norm_bwd(v, g, dn):
            rr = lax.rsqrt(jnp.mean(v * v, axis=-1, keepdims=True) + NORM_EPS)
            vh = v * rr
            dvh = dn * g
            dv = rr * (dvh - vh * jnp.mean(dvh * vh, axis=-1, keepdims=True))
            return vh, dv, jnp.sum(dn * vh, axis=0, keepdims=True)

        a = a_ref[...]
        ah, datt, dga = norm_bwd(a, ga_ref[...], dmix[:, :ATT_WIDTH])
        rh, drec, dgr = norm_bwd(r_ref[...], gr_ref[...], dmix[:, ATT_WIDTH:])
        dga_ref[...] += dga
        dgr_ref[...] += dgr
        mix_ref[:, :ATT_WIDTH] = (ah * ga_ref[...]).astype(MXU_DTYPE)
        mix_ref[:, ATT_WIDTH:] = (rh * gr_ref[...]).astype(MXU_DTYPE)
        dattb = datt.astype(MXU_DTYPE)
        datt_ref[...] = dattb
        drec_ref[...] = drec
        lo = _lo_mask()
        prod = dattb.astype(F32) * ax_ref[...]
        for p in range(N_PAIR):
            delta_ref[:, PAIR * p:PAIR * (p + 1)] = _half_sums(prod[:, PAIR * p:PAIR * (p + 1)], lo)

    row = lambda w: pl.BlockSpec((tm, w), lambda i: (i, 0))
    return pl.pallas_call(
        body, name="outproj_bwd", grid=(T // tm,),
        in_specs=[row(D_MODEL), row(ATT_WIDTH), row(ATT_WIDTH), row(LRU_WIDTH), _const((1, ATT_WIDTH)), _const((1, LRU_WIDTH)),
                  _const((D_MODEL, D_MODEL))],
        out_specs=[row(ATT_WIDTH), row(ATT_WIDTH), row(LRU_WIDTH), row(D_MODEL),
                   _const((1, ATT_WIDTH)), _const((1, LRU_WIDTH))],
        out_shape=[jax.ShapeDtypeStruct((T, ATT_WIDTH), MXU_DTYPE), jax.ShapeDtypeStruct((T, ATT_WIDTH), F32),
                   jax.ShapeDtypeStruct((T, LRU_WIDTH), F32), jax.ShapeDtypeStruct((T, D_MODEL), MXU_DTYPE),
                   jax.ShapeDtypeStruct((1, ATT_WIDTH), F32), jax.ShapeDtypeStruct((1, LRU_WIDTH), F32)],
        compiler_params=_params(("arbitrary",)),
    )(dx2b, att, att_x, rec, ga, gr, wout)


def _lru_bwd(drec, lg, h, lx, cw, cb, wa, ba, wx, bx, lam, bl, seq):
    tc = min(LRU_TILE, seq)
    nc = seq // tc
    T = bl * seq
    n = tc

    def body(dr_ref, lg_ref, h_ref, hp_ref, lx_ref, lxp_ref, cw_ref, cb_ref, wa_ref, ba_ref, wx_ref, bx_ref, lam_ref,
             dlx_ref, dlg_ref, dwa_ref, dwx_ref, small_ref, gc_ref, dxn_ref):
        b, i = pl.program_id(0), pl.program_id(1)
        ir = nc - 1 - i

        @pl.when((b == 0) & (i == 0))
        def _():
            dwa_ref[...] = jnp.zeros_like(dwa_ref)
            dwx_ref[...] = jnp.zeros_like(dwx_ref)
            small_ref[...] = jnp.zeros_like(small_ref)

        @pl.when(i == 0)
        def _():
            gc_ref[...] = jnp.zeros_like(gc_ref)
            dxn_ref[...] = jnp.zeros_like(dxn_ref)

        cw = cw_ref[...]
        lam_v = lam_ref[...]
        lxv = lx_ref[...]
        prev8 = jnp.where(ir > 0, lxp_ref[...], 0.0)
        xc, shifted = _conv_taps(lxv, prev8, cw, cb_ref[...])
        r, ig, sp, a, mult = _lru_gates(xc, wa_ref[...], ba_ref[...], wx_ref[...], bx_ref[...], lam_v)
        hv = h_ref[...]
        drv = dr_ref[...]
        g, dg = _gelu_parts(lg_ref[...])
        dlg_ref[...] = drv * hv * dg
        dh = drv * g

        row = lax.broadcasted_iota(jnp.int32, (n, LRU_WIDTH), 0)
        A = jnp.where(row < n - 1, pltpu.roll(a, n - 1, 0), 0.0)
        B = dh + jnp.where(row == n - 1, gc_ref[0:1, :], 0.0)
        k = 1
        while k < n:
            a_s = jnp.where(row < n - k, pltpu.roll(A, n - k, 0), 1.0)
            b_s = jnp.where(row < n - k, pltpu.roll(B, n - k, 0), 0.0)
            B = B + A * b_s
            A = A * a_s
            k *= 2
        gs = B
        gc_ref[0:1, :] = a[0:1, :] * gs[0:1, :]

        hprev8 = jnp.where(ir > 0, hp_ref[...], 0.0)
        h_prev = pltpu.roll(jnp.concatenate([hprev8, hv], axis=0), 1, 0)[8:]
        da = gs * h_prev
        ix = ig * xc
        dmult = gs * ix
        dig = gs * mult * xc
        dxc = gs * mult * ig
        dlog_a = da * a - dmult * (a * a) / mult
        dr_gate = dlog_a * (-LRU_C * sp)
        dza = dr_gate * r * (1.0 - r)
        dzx = dig * ig * (1.0 - ig)
        dzab = dza.astype(MXU_DTYPE)
        dzxb = dzx.astype(MXU_DTYPE)
        xcb = xc.astype(MXU_DTYPE)
        dwa_ref[...] += _tn(xcb, dzab)
        dwx_ref[...] += _tn(xcb, dzxb)
        dxc = dxc + _nt(dzab, wa_ref[...]) + _nt(dzxb, wx_ref[...])

        ds = jnp.concatenate([dxc, dxn_ref[...]], axis=0)
        dlx = cw[CONV_WIDTH - 1:CONV_WIDTH] * dxc
        for k in range(1, CONV_WIDTH):
            dlx = dlx + cw[CONV_WIDTH - 1 - k:CONV_WIDTH - k] * pltpu.roll(ds, n + 8 - k, 0)[:n]
        dlx_ref[...] = dlx
        dxn_ref[...] = dxc[0:8, :]

        colsum = lambda v: jnp.sum(v, axis=0, keepdims=True)
        small_ref[0:1, :] += colsum(dza)
        small_ref[1:2, :] += colsum(dzx)
        small_ref[2:3, :] += colsum(dlog_a * r) * (LRU_C * _sigmoid(-lam_v))
        small_ref[3:4, :] += colsum(dxc)
        for k in range(CONV_WIDTH):
            j = CONV_WIDTH - 1 - k
            small_ref[4 + j:5 + j, :] += colsum(dxc * shifted[k])

    tile = pl.BlockSpec((tc, LRU_WIDTH), lambda b, i: (b * nc + (nc - 1 - i), 0))
    prev = pl.BlockSpec((8, LRU_WIDTH), lambda b, i: (jnp.maximum((b * seq + (nc - 1 - i) * tc) // 8 - 1, 0), 0))
    vec = _const((1, LRU_WIDTH))
    mat = _const((LRU_WIDTH, LRU_WIDTH))
    return pl.pallas_call(
        body, name="lru_bwd", grid=(bl, nc),
        in_specs=[tile, tile, tile, prev, tile, prev, _const((CONV_WIDTH, LRU_WIDTH)), vec, mat, vec, mat, vec, vec],
        out_specs=[tile, tile, mat, mat, _const((8, LRU_WIDTH))],
        out_shape=[jax.ShapeDtypeStruct((T, LRU_WIDTH), F32), jax.ShapeDtypeStruct((T, LRU_WIDTH), F32),
                   jax.ShapeDtypeStruct((LRU_WIDTH, LRU_WIDTH), F32), jax.ShapeDtypeStruct((LRU_WIDTH, LRU_WIDTH), F32),
                   jax.ShapeDtypeStruct((8, LRU_WIDTH), F32)],
        scratch_shapes=[pltpu.VMEM((8, LRU_WIDTH), F32), pltpu.VMEM((8, LRU_WIDTH), F32)],
        compiler_params=_params(("arbitrary", "arbitrary")),
    )(drec, lg, h, h, lx, lx, cw, cb, wa, ba, wx, bx, lam)


def _attn_bwd(qn, kn, vb, dob, lse, delta, frow, fstart, bl, seq, slabs):
    tq = min(ATT_TILE, seq)
    nq = seq // tq
    T = bl * seq
    n = len(slabs)

    def body(fs_ref, q_ref, k_ref, v_ref, do_ref, lse_ref, dl_ref, fr_ref, *rest):
        s_in, (dq_ref, dk_ref, dv_ref, df_ref), s_out, sems = rest[:n], rest[n:n + 4], rest[n + 4:2 * n + 4], rest[2 * n + 4:]
        b, p, j = pl.program_id(0), pl.program_id(1), pl.program_id(2)
        copies = _chip_copies(s_in, s_out, *sems, scatter=True)

        @pl.when((b == 0) & (p == 0) & (j == 0))
        def _():
            for cp in copies:
                cp.start()

        @pl.when(j == 0)
        def _():
            dq_ref[...] = jnp.zeros_like(dq_ref)

        lane = lax.broadcasted_iota(jnp.int32, (1, PAIR), 1)
        rows = lax.broadcasted_iota(jnp.int32, (tq, tq), 0)
        cols = lax.broadcasted_iota(jnp.int32, (tq, tq), 1)
        causal = cols <= rows
        kv = k_ref[...]
        vv = v_ref[...]
        dk = jnp.zeros((tq, PAIR), F32)
        dv = jnp.zeros((tq, PAIR), F32)
        for hh in range(2):
            hm = (lane >= HEAD_DIM * hh) & (lane < HEAD_DIM * (hh + 1))
            kh = jnp.where(hm, kv, jnp.zeros_like(kv))
            fk = fr_ref[0, 0, hh:hh + 1, :]
            base = ((b * N_PAIR + p) * 2 + hh) * nq
            c0 = HEAD_DIM * hh

            def block(i, carry, masked, hm=hm, kh=kh, fk=fk, base=base, c0=c0):
                dk, dv, df = carry
                start = pl.multiple_of(i * tq, tq)
                qi = q_ref[pl.ds(start, tq), :]
                doi = do_ref[pl.ds(start, tq), :]
                qh = jnp.where(hm, qi, jnp.zeros_like(qi))
                doh = jnp.where(hm, doi, jnp.zeros_like(doi))
                s = _nt(qh, kv) - (fk - fs_ref[base + i])
                if masked:
                    s = jnp.where(causal, s, NEG)
                pr = jnp.exp(s - lse_ref[pl.ds(start, tq), c0:c0 + 1])
                dp = _nt(doh, vv)
                ds = pr * (dp - dl_ref[pl.ds(start, tq), c0:c0 + 1])
                dsb = ds.astype(MXU_DTYPE)
                dv = dv + _tn(pr.astype(MXU_DTYPE), doh)
                dk = dk + _tn(dsb, qh)
                dq_ref[pl.ds(start, tq), :] += jnp.dot(dsb, kh, preferred_element_type=F32)
                df = df - jnp.sum(ds, axis=0, keepdims=True)
                return dk, dv, df

            carry = block(j, (dk, dv, jnp.zeros((1, tq), F32)), True)
            dk, dv, df = lax.fori_loop(j + 1, nq, functools.partial(block, masked=False), carry)
            df_ref[0, 0, hh:hh + 1, :] = df
        dk_ref[...] = dk
        dv_ref[...] = dv

        @pl.when((b == bl - 1) & (p == N_PAIR - 1) & (j == nq - 1))
        def _():
            for cp in copies:
                cp.wait()

    blk = pl.BlockSpec((tq, PAIR), lambda b, p, j: (b * nq + j, p))
    full = pl.BlockSpec((seq, PAIR), lambda b, p, j: (b, p))
    fblk = pl.BlockSpec((1, 1, 2, tq), lambda b, p, j: (b, p, 0, j))
    hbm = pl.BlockSpec(memory_space=pl.ANY)
    return pl.pallas_call(
        body, name="attn_bwd", grid=(bl, N_PAIR, nq),
        in_specs=[pl.BlockSpec(memory_space=pltpu.SMEM), full, blk, blk, full, full, full, fblk] + [hbm] * n,
        out_specs=[full, blk, blk, fblk] + [hbm] * n,
        out_shape=[jax.ShapeDtypeStruct((T, ATT_WIDTH), F32), jax.ShapeDtypeStruct((T, ATT_WIDTH), F32),
                   jax.ShapeDtypeStruct((T, ATT_WIDTH), F32), jax.ShapeDtypeStruct((bl, N_PAIR, 2, seq), F32)]
        + [jax.ShapeDtypeStruct(s.shape, s.dtype) for s in slabs],
        scratch_shapes=_chip_sems(n),
        compiler_params=_params(("arbitrary", "arbitrary", "arbitrary"), VMEM_LARGE),
    )(fstart, qn, kn, vb, dob, lse, delta, frow, *slabs)


def _forget_bwd(dfcol, f2d, bf, bl, seq):
    def body(d_ref, z_ref, b_ref, o_ref, db_ref):
        @pl.when(pl.program_id(0) == 0)
        def _():
            db_ref[...] = jnp.zeros_like(db_ref)

        d = d_ref[...]
        row = lax.broadcasted_iota(jnp.int32, (seq, F_PAD), 0)
        k = 1
        while k < seq:
            d = d + jnp.where(row < seq - k, pltpu.roll(d, seq - k, 0), 0.0)
            k *= 2
        dz = d * _sigmoid(-(z_ref[...] + b_ref[...]))
        o_ref[...] = dz
        db_ref[...] += jnp.sum(dz, axis=0, keepdims=True)

    blk = pl.BlockSpec((seq, F_PAD), lambda b: (b, 0))
    return pl.pallas_call(
        body, name="forget_bwd", grid=(bl,),
        in_specs=[blk, blk, _const((1, F_PAD))],
        out_specs=[blk, _const((1, F_PAD))],
        out_shape=[jax.ShapeDtypeStruct(f2d.shape, F32), jax.ShapeDtypeStruct((1, F_PAD), F32)],
        compiler_params=_params(("arbitrary",)),
    )(dfcol, f2d, bf)


def _inproj_bwd(dq, dk, dv, qkv, df, dlx, dlg, x2d, dx2, g1, gq2, gk2, wcat):
    T = x2d.shape[0]
    tm = TOKEN_TILE

    def body(dq_ref, dk_ref, dv_ref, qkv_ref, df_ref, dlx_ref, dlg_ref, x_ref, dx2_ref, g1_ref, gq_ref, gk_ref, w_ref,
             gx_ref, dp_ref, h_ref, dg1_ref, dgq_ref, dgk_ref):
        @pl.when(pl.program_id(0) == 0)
        def _():
            dg1_ref[...] = jnp.zeros_like(dg1_ref)
            dgq_ref[...] = jnp.zeros_like(dgq_ref)
            dgk_ref[...] = jnp.zeros_like(dgk_ref)

        lo = _lo_mask()

        def head_norm_bwd(t, g2, dy):
            rr = lax.rsqrt(_half_sums(t * t, lo) * (1.0 / HEAD_DIM) + NORM_EPS)
            th = t * rr
            dth = dy * g2
            mm = _half_sums(dth * th, lo) * (1.0 / HEAD_DIM)
            return rr * (dth - th * mm), jnp.sum(dy * th, axis=0, keepdims=True)

        dgq = jnp.zeros((1, PAIR), F32)
        dgk = jnp.zeros((1, PAIR), F32)
        for p in range(N_PAIR):
            cq = slice(PAIR * p, PAIR * (p + 1))
            ck = slice(ATT_WIDTH + PAIR * p, ATT_WIDTH + PAIR * (p + 1))
            dqp, g_ = head_norm_bwd(qkv_ref[:, cq], gq_ref[...], dq_ref[:, cq] * QK_SCALE)
            dgq = dgq + g_
            dp_ref[:, cq] = dqp.astype(MXU_DTYPE)
            dkp, g_ = head_norm_bwd(qkv_ref[:, ck], gk_ref[...], dk_ref[:, cq])
            dgk = dgk + g_
            dp_ref[:, ck] = dkp.astype(MXU_DTYPE)
        dgq_ref[...] += dgq
        dgk_ref[...] += dgk
        f0 = 3 * ATT_WIDTH
        dp_ref[:, 2 * ATT_WIDTH:f0] = dv_ref[...].astype(MXU_DTYPE)
        dp_ref[:, f0:f0 + F_PAD] = df_ref[...].astype(MXU_DTYPE)
        dp_ref[:, f0 + F_PAD:f0 + F_PAD + LRU_WIDTH] = dlx_ref[...].astype(MXU_DTYPE)
        dp_ref[:, f0 + F_PAD + LRU_WIDTH:] = dlg_ref[...].astype(MXU_DTYPE)
        dh = _nt(dp_ref[...], w_ref[...])
        x = x_ref[...]
        r = lax.rsqrt(jnp.mean(x * x, axis=-1, keepdims=True) + NORM_EPS)
        xh = x * r
        h_ref[...] = (xh * g1_ref[...]).astype(MXU_DTYPE)
        dg1_ref[...] += jnp.sum(dh * xh, axis=0, keepdims=True)
        dxh = dh * g1_ref[...]
        gx_ref[...] = dx2_ref[...] + r * (dxh - xh * jnp.mean(dxh * xh, axis=-1, keepdims=True))

    row = lambda w: pl.BlockSpec((tm, w), lambda i: (i, 0))
    return pl.pallas_call(
        body, name="inproj_bwd", grid=(T // tm,),
        in_specs=[row(ATT_WIDTH), row(ATT_WIDTH), row(ATT_WIDTH), row(3 * ATT_WIDTH), row(F_PAD), row(LRU_WIDTH),
                  row(LRU_WIDTH), row(D_MODEL), row(D_MODEL), _const((1, D_MODEL)), _const((1, PAIR)), _const((1, PAIR)),
                  _const((D_MODEL, N_CAT))],
        out_specs=[row(D_MODEL), row(N_CAT), row(D_MODEL), _const((1, D_MODEL)), _const((1, PAIR)), _const((1, PAIR))],
        out_shape=[jax.ShapeDtypeStruct((T, D_MODEL), F32), jax.ShapeDtypeStruct((T, N_CAT), MXU_DTYPE),
                   jax.ShapeDtypeStruct((T, D_MODEL), MXU_DTYPE), jax.ShapeDtypeStruct((1, D_MODEL), F32),
                   jax.ShapeDtypeStruct((1, PAIR), F32), jax.ShapeDtypeStruct((1, PAIR), F32)],
        compiler_params=_params(("arbitrary",), VMEM_LARGE),
    )(dq, dk, dv, qkv, df, dlx, dlg, x2d, dx2, g1, gq2, gk2, wcat)


def _row_block(rows):
    for rb in (256, 176, 128, 64, 32, 16, 8):
        if rows % rb == 0:
            return rb
    return rows


def _sum_slabs(recv, name):
    _, rows, cols = recv.shape
    rb = _row_block(rows)

    def body(r_ref, o_ref):
        o_ref[...] = ((r_ref[0] + r_ref[1]) + r_ref[2]) + r_ref[3]

    return pl.pallas_call(
        body, name=name, grid=(rows // rb,),
        in_specs=[pl.BlockSpec((4, rb, cols), lambda i: (0, i, 0))],
        out_specs=pl.BlockSpec((rb, cols), lambda i: (i, 0)),
        out_shape=jax.ShapeDtypeStruct((rows, cols), F32),
        compiler_params=_params(("parallel",)),
    )(recv)


def _adamw_math(w, g, m, v):
    m = ADAM_B1 * m + (1.0 - ADAM_B1) * g
    v = ADAM_B2 * v + (1.0 - ADAM_B2) * (g * g)
    m_hat = m / (1.0 - ADAM_B1 ** ADAM_STEP)
    v_hat = v / (1.0 - ADAM_B2 ** ADAM_STEP)
    delta = -ADAM_LR * (m_hat / (jnp.sqrt(v_hat) + ADAM_EPS) + ADAM_WD * w)
    return delta, m, v


def _adamw_pair(mine, theirs, w, m, v, name):
    rows, cols = w.shape
    rb = _row_block(rows)

    def body(a_ref, b_ref, w_ref, m_ref, v_ref, g_ref, d_ref, nm_ref, nv_ref):
        g = a_ref[...] + b_ref[...]
        g_ref[...] = g
        d_ref[...], nm_ref[...], nv_ref[...] = _adamw_math(w_ref[...], g, m_ref[...], v_ref[...])

    blk = pl.BlockSpec((rb, cols), lambda i: (i, 0))
    return pl.pallas_call(
        body, name=name, grid=(rows // rb,),
        in_specs=[blk] * 5, out_specs=[blk] * 4,
        out_shape=[jax.ShapeDtypeStruct((rows, cols), F32)] * 4,
        compiler_params=_params(("parallel",)),
    )(mine, theirs, w, m, v)


def _sum_packs(recv):
    _, rows, cols = recv.shape

    def body(r_ref, o_ref):
        acc = r_ref[0]
        for d in range(1, 8):
            acc = acc + r_ref[d]
        o_ref[...] = acc

    return pl.pallas_call(
        body, name="sum_packs", grid=(1,),
        in_specs=[_const(recv.shape)], out_specs=_const((rows, cols)),
        out_shape=jax.ShapeDtypeStruct((rows, cols), F32),
        compiler_params=_params(("arbitrary",)),
    )(recv)


def _adamw_small(w, g, m, v):
    rows, cols = w.shape

    def body(w_ref, g_ref, m_ref, v_ref, d_ref, nm_ref, nv_ref):
        d_ref[...], nm_ref[...], nv_ref[...] = _adamw_math(w_ref[...], g_ref[...], m_ref[...], v_ref[...])

    blk = _const((rows, cols))
    return pl.pallas_call(
        body, name="adamw_small", grid=(1,),
        in_specs=[blk] * 4, out_specs=[blk] * 3,
        out_shape=[jax.ShapeDtypeStruct((rows, cols), F32)] * 3,
        compiler_params=_params(("arbitrary",)),
    )(w, g, m, v)


SMALL = ["norm1_g", "q_norm_g", "k_norm_g", "b_f", "conv_b", "w_a", "b_a", "w_x", "b_x", "lam",
         "attn_out_g", "lru_out_g", "norm2_g"]


def _pack(parts, rows=None):
    flat = jnp.concatenate([p.reshape(-1) for p in parts])
    n = flat.shape[0]
    total = -(-n // 1024) * 1024 if rows is None else rows * 128
    return jnp.pad(flat, (0, total - n)).reshape(-1, 128)


def _unpack(pack, shapes):
    flat = pack.reshape(-1)
    out, off = [], 0
    for s in shapes:
        n = math.prod(s)
        out.append(flat[off:off + n].reshape(s))
        off += n
    return out


def _block_diag(w):
    eye = jnp.eye(LRU_BLOCKS, dtype=w.dtype)
    return (w[:, :, None, :] * eye[:, None, :, None]).reshape(LRU_WIDTH, LRU_WIDTH)


def _diag_blocks(m):
    m4 = m.reshape(LRU_BLOCKS, LRU_WIDTH // LRU_BLOCKS, LRU_BLOCKS, LRU_WIDTH // LRU_BLOCKS)
    return jnp.stack([m4[n, :, n, :] for n in range(LRU_BLOCKS)])


def kernel(x, norm1_g, w_in, q_norm_g, k_norm_g, b_f, conv_w, conv_b, w_a, b_a, w_x, b_x, lam, attn_out_g, lru_out_g, w_out, norm2_g, w_gate, w_up, w_down, loss_target, m_norm1_g, m_w_in, m_q_norm_g, m_k_norm_g, m_b_f, m_conv_w, m_conv_b, m_w_a, m_b_a, m_w_x, m_b_x, m_lam, m_attn_out_g, m_lru_out_g, m_w_out, m_norm2_g, m_w_gate, m_w_up, m_w_down, v_norm1_g, v_w_in, v_q_norm_g, v_k_norm_g, v_b_f, v_conv_w, v_conv_b, v_w_a, v_b_a, v_w_x, v_b_x, v_lam, v_attn_out_g, v_lru_out_g, v_w_out, v_norm2_g, v_w_gate, v_w_up, v_w_down):
    args = dict(locals())
    bl, seq, _ = x.shape
    T = bl * seq
    tq = min(ATT_TILE, seq)
    nq = seq // tq
    dff = w_gate.shape[2] * 4
    my_chip = 2 * lax.axis_index("x") + lax.axis_index("y")

    g_in, g_cw = _gather_over_chips([w_in[0].astype(MXU_DTYPE), conv_w[0]])
    later_shards = [w_out[0].astype(MXU_DTYPE), w_gate[0].astype(MXU_DTYPE), w_up[0].astype(MXU_DTYPE),
                    w_down[0].astype(MXU_DTYPE)]
    col_cat = lambda g: jnp.transpose(g, (1, 0, 2)).reshape(g.shape[1], -1)
    w_in_full = col_cat(g_in)
    f0 = 3 * ATT_WIDTH
    wcat = jnp.concatenate([w_in_full[:, :f0], jnp.pad(w_in_full[:, f0:f0 + HEADS], ((0, 0), (0, F_PAD - HEADS))),
                            w_in_full[:, f0 + HEADS:]], axis=1)
    cw_full = col_cat(g_cw)
    wa_bd = _block_diag(w_a[0]).astype(MXU_DTYPE)
    wx_bd = _block_diag(w_x[0]).astype(MXU_DTYPE)
    gq2 = jnp.tile(q_norm_g, (1, 2))
    gk2 = jnp.tile(k_norm_g, (1, 2))
    bf_pad = jnp.pad(b_f, ((0, 0), (0, F_PAD - HEADS)))

    x2d = x.reshape(T, D_MODEL)
    target2d = loss_target.reshape(T, D_MODEL)

    qkv, qn, kn, vb, f2d, lx, lg = _inproj(x2d, norm1_g, wcat, gq2, gk2)
    fcol = _forget_cumsum(f2d, bf_pad, bl, seq)
    frow = jnp.transpose(fcol.reshape(bl, seq, F_PAD)[:, :, :HEADS], (0, 2, 1)).reshape(bl, N_PAIR, 2, seq)
    fstart = frow[:, :, :, ::tq].reshape(-1)
    att, att_x, lse, g_out, g_gate, g_up, g_down = _attn_fwd(qn, kn, vb, frow, fstart, bl, seq, later_shards)
    wout_full = g_out.reshape(D_MODEL, D_MODEL)
    wg_full, wu_full = col_cat(g_gate), col_cat(g_up)
    wd_full = g_down.reshape(dff, D_MODEL)
    h, rec = _lru_fwd(lx, lg, cw_full, conv_b, wa_bd, b_a, wx_bd, b_x, lam, bl, seq)
    x2 = _outproj(x2d, att, rec, attn_out_g, lru_out_g, wout_full)
    gt, up, dy, sq_err = _mlp_fwd(x2, norm2_g, wg_full, wu_full, wd_full, target2d)
    loss = lax.psum(sq_err[0, 0] * (0.5 / D_MODEL), ("x", "y", "c"))

    dx2, dx2b, dgtb, dupb, actb, h2b, dyb, dg2 = _mlp_bwd(dy, x2, gt, up, norm2_g, wg_full, wu_full, wd_full)
    dw_down = _matmul_tn(actb, dyb, 512, "dw_down")
    dw_gate = _matmul_tn(h2b, dgtb, dff // 2, "dw_gate")
    dw_up = _matmul_tn(h2b, dupb, dff // 2, "dw_up")
    dattb, delta, drec, mixb, dga, dgr = _outproj_bwd(dx2b, att, att_x, rec, attn_out_g, lru_out_g, wout_full)
    dw_out = _matmul_tn(mixb, dx2b, D_MODEL, "dw_out")
    dlx, dlg, dwa_bd, dwx_bd, lru_small = _lru_bwd(drec, lg, h, lx, cw_full, conv_b, wa_bd, b_a, wx_bd, b_x, lam, bl, seq)
    col_split = lambda g: jnp.transpose(g.reshape(g.shape[0], 4, -1), (1, 0, 2))
    early_slabs = [dw_out.reshape(4, D_MODEL // 4, D_MODEL), col_split(dw_gate), col_split(dw_up),
                   dw_down.reshape(4, dff // 4, D_MODEL)]
    dq, dk, dv, dfrow, *recv_early = _attn_bwd(qn, kn, vb, dattb, lse, delta, frow, fstart, bl, seq, early_slabs)
    dfcol = jnp.pad(jnp.transpose(dfrow.reshape(bl, HEADS, seq), (0, 2, 1)), ((0, 0), (0, 0), (0, F_PAD - HEADS)))
    df, dbf = _forget_bwd(dfcol.reshape(T, F_PAD), f2d, bf_pad, bl, seq)
    grad_x, dprojb, h1b, dg1, dgq, dgk = _inproj_bwd(dq, dk, dv, qkv, df, dlx, dlg, x2d, dx2, norm1_g, gq2, gk2, wcat)
    dwcat = _matmul_tn(h1b, dprojb, N_CAT // 3, "dw_in")
    dw_in = jnp.concatenate([dwcat[:, :f0 + HEADS], dwcat[:, f0 + F_PAD:]], axis=1)

    small_grads = {
        "norm1_g": dg1, "q_norm_g": dgq[:, :HEAD_DIM] + dgq[:, HEAD_DIM:], "k_norm_g": dgk[:, :HEAD_DIM] + dgk[:, HEAD_DIM:],
        "b_f": dbf[:, :HEADS], "conv_b": lru_small[3:4], "w_a": _diag_blocks(dwa_bd)[None], "b_a": lru_small[0:1],
        "w_x": _diag_blocks(dwx_bd)[None], "b_x": lru_small[1:2], "lam": lru_small[2:3],
        "attn_out_g": dga, "lru_out_g": dgr, "norm2_g": dg2}
    pack = _pack([small_grads[n] for n in SMALL] + [lru_small[4:8]])
    recv_in, recv_pack = _exchange_grads([col_split(dw_in)], pack)
    recv = [recv_in] + recv_early
    big = ["w_in", "w_out", "w_gate", "w_up", "w_down"]
    part = [_sum_slabs(r, "sum_" + n) for r, n in zip(recv, big)]
    theirs = _swap_with_sibling(part)
    out = {}
    for n, a, b_ in zip(big, part, theirs):
        g, d, nm, nv = _adamw_pair(a, b_, args[n][0], args["m_" + n][0], args["v_" + n][0], "adamw_" + n)
        out[n] = (g[None], d[None], nm[None], nv[None])

    small_shapes = [args[n].shape for n in SMALL]
    red = _sum_packs(recv_pack)
    g_small = _unpack(red, small_shapes + [(CONV_WIDTH, LRU_WIDTH)])
    g_cw_mine = lax.dynamic_slice_in_dim(g_small[-1], my_chip * (LRU_WIDTH // 4), LRU_WIDTH // 4, axis=1)[None]
    g_list = g_small[:-1] + [g_cw_mine]
    names = SMALL + ["conv_w"]
    rows = pack.shape[0]
    d_p, m_p, v_p = _adamw_small(_pack([args[n] for n in names], rows), _pack(g_list, rows),
                                 _pack([args["m_" + n] for n in names], rows), _pack([args["v_" + n] for n in names], rows))
    shapes = [args[n].shape for n in names]
    for n, g, d, nm, nv in zip(names, g_list, _unpack(d_p, shapes), _unpack(m_p, shapes), _unpack(v_p, shapes)):
        out[n] = (g, d, nm, nv)

    order = ["norm1_g", "w_in", "q_norm_g", "k_norm_g", "b_f", "conv_w", "conv_b", "w_a", "b_a", "w_x", "b_x", "lam",
             "attn_out_g", "lru_out_g", "w_out", "norm2_g", "w_gate", "w_up", "w_down"]
    return (loss, grad_x.reshape(bl, seq, D_MODEL), *[out[n][0] for n in order], *[out[n][1] for n in order],
            *[out[n][2] for n in order], *[out[n][3] for n in order])
```

```python
import functools
import math

import jax
import jax.numpy as jnp
from jax import lax
from jax.experimental import pallas as pl
from jax.experimental.pallas import tpu as pltpu

F32 = jnp.float32
MXU_DTYPE = jnp.bfloat16
MESH = pl.DeviceIdType.MESH

D_MODEL = 1024
ATT_WIDTH = 512
LRU_WIDTH = 512
HEADS = 8
HEAD_DIM = 64
PAIR = 2 * HEAD_DIM
N_PAIR = HEADS // 2
LRU_BLOCKS = 8
CONV_WIDTH = 4
LRU_C = 8.0
NORM_EPS = 1e-6
QK_SCALE = 1.0 / math.sqrt(HEAD_DIM)
F_PAD = 128
N_CAT = 3 * ATT_WIDTH + F_PAD + 2 * LRU_WIDTH
NEG = -1e30

ADAM_LR, ADAM_B1, ADAM_B2, ADAM_EPS, ADAM_WD, ADAM_STEP = 0.001, 0.9, 0.999, 1e-08, 0.01, 10

TOKEN_TILE = 256
ATT_TILE = 512
LRU_TILE = 256
VMEM_SMALL = 32 * 1024 * 1024
VMEM_LARGE = 56 * 1024 * 1024


def _params(sem, vmem=VMEM_SMALL):
    return pltpu.CompilerParams(dimension_semantics=sem, vmem_limit_bytes=vmem)


def _const(shape):
    nd = len(shape)
    return pl.BlockSpec(shape, lambda *_: (0,) * nd)


def _sigmoid(x):
    return 1.0 / (1.0 + jnp.exp(-x))


def _half_sums(t, lo):
    s_lo = jnp.sum(jnp.where(lo, t, 0.0), axis=-1, keepdims=True)
    s_hi = jnp.sum(jnp.where(lo, 0.0, t), axis=-1, keepdims=True)
    return jnp.where(lo, s_lo, s_hi)


def _lo_mask():
    return lax.broadcasted_iota(jnp.int32, (1, PAIR), 1) < HEAD_DIM


def _other_chips(x, y):
    return [(1 - x, y), (x, 1 - y), (1 - x, 1 - y)]


def _chip_copies(ins, outs, send_sems, recv_sems, loc_sems, scatter):
    x, y, c = lax.axis_index("x"), lax.axis_index("y"), lax.axis_index("c")
    me = 2 * x + y
    copies = []
    for w in range(len(ins)):
        copies.append(pltpu.make_async_copy(ins[w].at[me] if scatter else ins[w], outs[w].at[me], loc_sems.at[w]))
        for k, (cx, cy) in enumerate(_other_chips(x, y)):
            copies.append(pltpu.make_async_remote_copy(
                src_ref=ins[w].at[2 * cx + cy] if scatter else ins[w], dst_ref=outs[w].at[me],
                send_sem=send_sems.at[3 * w + k], recv_sem=recv_sems.at[3 * w + k],
                device_id=(cx, cy, c), device_id_type=MESH))
    return copies


def _chip_sems(n):
    return [pltpu.SemaphoreType.DMA((3 * n,)), pltpu.SemaphoreType.DMA((3 * n,)), pltpu.SemaphoreType.DMA((n,))]


def _gather_over_chips(shards):
    n = len(shards)

    def body(*refs):
        copies = _chip_copies(refs[:n], refs[n:2 * n], *refs[2 * n:], scatter=False)
        for cp in copies:
            cp.start()
        for cp in copies:
            cp.wait()

    return pl.pallas_call(
        body, name="gather_weights",
        out_shape=[jax.ShapeDtypeStruct((4,) + s.shape, s.dtype) for s in shards],
        in_specs=[pl.BlockSpec(memory_space=pl.ANY)] * n,
        out_specs=[pl.BlockSpec(memory_space=pl.ANY)] * n,
        scratch_shapes=_chip_sems(n),
    )(*shards)


def _exchange_grads(slabs, pack):
    n = len(slabs)

    def body(*refs):
        ins, pack_in = refs[:n], refs[n]
        outs, pack_out = refs[n + 1:2 * n + 1], refs[2 * n + 1]
        send_sems, recv_sems, loc_sems, psend, precv = refs[2 * n + 2:]
        x, y, c = lax.axis_index("x"), lax.axis_index("y"), lax.axis_index("c")
        dev = 4 * x + 2 * y + c
        copies = _chip_copies(ins, outs, send_sems, recv_sems, loc_sems, scatter=True)
        copies.append(pltpu.make_async_copy(pack_in, pack_out.at[dev], loc_sems.at[n]))
        for k in range(1, 8):
            fx, fy, fc = (k >> 2) & 1, (k >> 1) & 1, k & 1
            tx = (1 - x) if fx else x
            ty = (1 - y) if fy else y
            tc = (1 - c) if fc else c
            copies.append(pltpu.make_async_remote_copy(
                src_ref=pack_in, dst_ref=pack_out.at[dev],
                send_sem=psend.at[k - 1], recv_sem=precv.at[k - 1],
                device_id=(tx, ty, tc), device_id_type=MESH))
        for cp in copies:
            cp.start()
        for cp in copies:
            cp.wait()

    return pl.pallas_call(
        body, name="exchange_grads",
        out_shape=[jax.ShapeDtypeStruct(s.shape, s.dtype) for s in slabs]
        + [jax.ShapeDtypeStruct((8,) + pack.shape, pack.dtype)],
        in_specs=[pl.BlockSpec(memory_space=pl.ANY)] * (n + 1),
        out_specs=[pl.BlockSpec(memory_space=pl.ANY)] * (n + 1),
        scratch_shapes=[pltpu.SemaphoreType.DMA((3 * n,)), pltpu.SemaphoreType.DMA((3 * n,)),
                        pltpu.SemaphoreType.DMA((n + 1,)),
                        pltpu.SemaphoreType.DMA((7,)), pltpu.SemaphoreType.DMA((7,))],
    )(*slabs, pack)


def _swap_with_sibling(arrs):
    n = len(arrs)

    def body(*refs):
        ins, outs = refs[:n], refs[n:2 * n]
        send_sems, recv_sems = refs[2 * n:]
        x, y, c = lax.axis_index("x"), lax.axis_index("y"), lax.axis_index("c")
        copies = []
        for w in range(n):
            cp = pltpu.make_async_remote_copy(
                src_ref=ins[w], dst_ref=outs[w], send_sem=send_sems.at[w], recv_sem=recv_sems.at[w],
                device_id=(x, y, 1 - c), device_id_type=MESH)
            cp.start()
            copies.append(cp)
        for cp in copies:
            cp.wait()

    return pl.pallas_call(
        body, name="swap_sibling",
        out_shape=[jax.ShapeDtypeStruct(a.shape, a.dtype) for a in arrs],
        in_specs=[pl.BlockSpec(memory_space=pl.ANY)] * n,
        out_specs=[pl.BlockSpec(memory_space=pl.ANY)] * n,
        scratch_shapes=[pltpu.SemaphoreType.DMA((n,)), pltpu.SemaphoreType.DMA((n,))],
    )(*arrs)


def _head_norm(t, g2, lo):
    rr = lax.rsqrt(_half_sums(t * t, lo) * (1.0 / HEAD_DIM) + NORM_EPS)
    return t * rr * g2


def _inproj(x2d, g1, wcat, gq2, gk2):
    T = x2d.shape[0]
    tm = TOKEN_TILE

    def body(x_ref, g1_ref, w_ref, gq_ref, gk_ref, qkv_ref, qn_ref, kn_ref, vb_ref, f_ref, lx_ref, lg_ref):
        x = x_ref[...]
        r = lax.rsqrt(jnp.mean(x * x, axis=-1, keepdims=True) + NORM_EPS)
        h = (x * r * g1_ref[...]).astype(MXU_DTYPE)
        proj = jnp.dot(h, w_ref[...], preferred_element_type=F32)
        qkv_ref[...] = proj[:, :3 * ATT_WIDTH]
        lo = _lo_mask()
        for p in range(N_PAIR):
            cols = slice(PAIR * p, PAIR * (p + 1))
            q = proj[:, PAIR * p:PAIR * (p + 1)]
            k = proj[:, ATT_WIDTH + PAIR * p:ATT_WIDTH + PAIR * (p + 1)]
            qn_ref[:, cols] = (_head_norm(q, gq_ref[...], lo) * QK_SCALE).astype(MXU_DTYPE)
            kn_ref[:, cols] = _head_norm(k, gk_ref[...], lo).astype(MXU_DTYPE)
        vb_ref[...] = proj[:, 2 * ATT_WIDTH:3 * ATT_WIDTH].astype(MXU_DTYPE)
        f0 = 3 * ATT_WIDTH
        f_ref[...] = proj[:, f0:f0 + F_PAD]
        lx_ref[...] = proj[:, f0 + F_PAD:f0 + F_PAD + LRU_WIDTH]
        lg_ref[...] = proj[:, f0 + F_PAD + LRU_WIDTH:]

    row = lambda w: pl.BlockSpec((tm, w), lambda i: (i, 0))
    return pl.pallas_call(
        body, name="inproj", grid=(T // tm,),
        in_specs=[row(D_MODEL), _const((1, D_MODEL)), _const((D_MODEL, N_CAT)), _const((1, PAIR)), _const((1, PAIR))],
        out_specs=[row(3 * ATT_WIDTH), row(ATT_WIDTH), row(ATT_WIDTH), row(ATT_WIDTH), row(F_PAD),
                   row(LRU_WIDTH), row(LRU_WIDTH)],
        out_shape=[jax.ShapeDtypeStruct((T, 3 * ATT_WIDTH), F32),
                   jax.ShapeDtypeStruct((T, ATT_WIDTH), MXU_DTYPE), jax.ShapeDtypeStruct((T, ATT_WIDTH), MXU_DTYPE),
                   jax.ShapeDtypeStruct((T, ATT_WIDTH), MXU_DTYPE), jax.ShapeDtypeStruct((T, F_PAD), F32),
                   jax.ShapeDtypeStruct((T, LRU_WIDTH), F32), jax.ShapeDtypeStruct((T, LRU_WIDTH), F32)],
        compiler_params=_params(("parallel",), VMEM_LARGE),
    )(x2d, g1, wcat, gq2, gk2)


def _forget_cumsum(f2d, bf, bl, seq):
    def body(z_ref, b_ref, o_ref):
        z = z_ref[...] + b_ref[...]
        lf = jnp.minimum(z, 0.0) - jnp.log(1.0 + jnp.exp(-jnp.abs(z)))
        row = lax.broadcasted_iota(jnp.int32, (seq, F_PAD), 0)
        k = 1
        while k < seq:
            lf = lf + jnp.where(row >= k, pltpu.roll(lf, k, 0), 0.0)
            k *= 2
        o_ref[...] = lf

    return pl.pallas_call(
        body, name="forget_cumsum", grid=(bl,),
        in_specs=[pl.BlockSpec((seq, F_PAD), lambda b: (b, 0)), _const((1, F_PAD))],
        out_specs=pl.BlockSpec((seq, F_PAD), lambda b: (b, 0)),
        out_shape=jax.ShapeDtypeStruct(f2d.shape, F32),
        compiler_params=_params(("parallel",)),
    )(f2d, bf)


def _attn_fwd(qn, kn, vb, frow, fstart, bl, seq, shards):
    tq = min(ATT_TILE, seq)
    nq = seq // tq
    T = bl * seq
    n = len(shards)

    def body(fs_ref, q_ref, k_ref, v_ref, fr_ref, *rest):
        g_in, (o_ref, ox_ref, lse_ref), g_out, sems = rest[:n], rest[n:n + 3], rest[n + 3:2 * n + 3], rest[2 * n + 3:]
        b, p, i = pl.program_id(0), pl.program_id(1), pl.program_id(2)
        copies = _chip_copies(g_in, g_out, *sems, scatter=False)

        @pl.when((b == 0) & (p == 0) & (i == 0))
        def _():
            for cp in copies:
                cp.start()

        lane = lax.broadcasted_iota(jnp.int32, (1, PAIR), 1)
        rows = lax.broadcasted_iota(jnp.int32, (tq, tq), 0)
        cols = lax.broadcasted_iota(jnp.int32, (tq, tq), 1)
        causal = cols <= rows
        q = q_ref[...]
        hms = [(lane >= HEAD_DIM * hh) & (lane < HEAD_DIM * (hh + 1)) for hh in range(2)]
        qhs = [jnp.where(hm, q, jnp.zeros_like(q)) for hm in hms]
        shifts = [fs_ref[((b * N_PAIR + p) * 2 + hh) * nq + i] for hh in range(2)]

        def block(j, carry, masked):
            start = pl.multiple_of(j * tq, tq)
            k = k_ref[pl.ds(start, tq), :]
            v = v_ref[pl.ds(start, tq), :]
            new = []
            for hh in range(2):
                m, l, acc, acc_lo = carry[hh]
                s = lax.dot_general(qhs[hh], k, (((1,), (1,)), ((), ())), preferred_element_type=F32)
                s = s - (fr_ref[0, 0, hh:hh + 1, pl.ds(start, tq)] - shifts[hh])
                if masked:
                    s = jnp.where(causal, s, NEG)
                m_new = jnp.maximum(m, jnp.max(s, axis=-1, keepdims=True))
                alpha = jnp.exp(m - m_new)
                pe = jnp.exp(s - m_new)
                l = alpha * l + jnp.sum(pe, axis=-1, keepdims=True)
                vh = jnp.where(hms[hh], v, jnp.zeros_like(v))
                pb = pe.astype(MXU_DTYPE)
                p_lo = (pe - pb.astype(F32)).astype(MXU_DTYPE)
                acc = alpha * acc + jnp.dot(pb, vh, preferred_element_type=F32)
                acc_lo = alpha * acc_lo + jnp.dot(p_lo, vh, preferred_element_type=F32)
                new.append((m_new, l, acc, acc_lo))
            return tuple(new)

        init = (jnp.full((tq, 1), NEG, F32), jnp.zeros((tq, 1), F32), jnp.zeros((tq, PAIR), F32),
                jnp.zeros((tq, PAIR), F32))
        carry = lax.fori_loop(0, i, functools.partial(block, masked=False), (init, init))
        carry = block(i, carry, True)
        out = jnp.zeros((tq, PAIR), F32)
        out_x = jnp.zeros((tq, PAIR), F32)
        lse = jnp.zeros((tq, PAIR), F32)
        for hh in range(2):
            m, l, acc, acc_lo = carry[hh]
            inv_l = 1.0 / l
            out = out + acc * inv_l
            out_x = out_x + (acc + acc_lo) * inv_l
            lse = jnp.where(hms[hh], m + jnp.log(l), lse)
        o_ref[...] = out
        ox_ref[...] = out_x
        lse_ref[...] = lse

        @pl.when((b == bl - 1) & (p == N_PAIR - 1) & (i == nq - 1))
        def _():
            for cp in copies:
                cp.wait()

    blk = pl.BlockSpec((tq, PAIR), lambda b, p, i: (b * nq + i, p))
    full = pl.BlockSpec((seq, PAIR), lambda b, p, i: (b, p))
    return pl.pallas_call(
        body, name="attn_fwd", grid=(bl, N_PAIR, nq),
        in_specs=[pl.BlockSpec(memory_space=pltpu.SMEM), blk, full, full,
                  pl.BlockSpec((1, 1, 2, seq), lambda b, p, i: (b, p, 0, 0))] + [pl.BlockSpec(memory_space=pl.ANY)] * n,
        out_specs=[blk, blk, blk] + [pl.BlockSpec(memory_space=pl.ANY)] * n,
        out_shape=[jax.ShapeDtypeStruct((T, ATT_WIDTH), F32)] * 3
        + [jax.ShapeDtypeStruct((4,) + s.shape, s.dtype) for s in shards],
        scratch_shapes=_chip_sems(n),
        compiler_params=_params(("arbitrary", "arbitrary", "arbitrary")),
    )(fstart, qn, kn, vb, frow, *shards)


def _conv_taps(lx, prev8, cw, cb):
    xs = jnp.concatenate([prev8, lx], axis=0)
    shifted = [lx] + [pltpu.roll(xs, k, 0)[8:] for k in range(1, CONV_WIDTH)]
    xc = cb + cw[CONV_WIDTH - 1:CONV_WIDTH] * lx
    for k in range(1, CONV_WIDTH):
        xc = xc + cw[CONV_WIDTH - 1 - k:CONV_WIDTH - k] * shifted[k]
    return xc, shifted


def _lru_gates(xc, wa, ba, wx, bx, lam):
    xb = xc.astype(MXU_DTYPE)
    r = _sigmoid(jnp.dot(xb, wa, preferred_element_type=F32) + ba)
    ig = _sigmoid(jnp.dot(xb, wx, preferred_element_type=F32) + bx)
    sp = jnp.maximum(-lam, 0.0) + jnp.log(1.0 + jnp.exp(-jnp.abs(lam)))
    log_a = -LRU_C * r * sp
    a = jnp.exp(log_a)
    th = jnp.tanh(log_a)
    mult = jnp.sqrt(-2.0 * th / (1.0 - th))
    return r, ig, sp, a, mult


def _gelu_parts(x):
    c0 = math.sqrt(2.0 / math.pi)
    t = jnp.tanh(c0 * (x + 0.044715 * x * x * x))
    g = 0.5 * x * (1.0 + t)
    dg = 0.5 * (1.0 + t) + 0.5 * x * (1.0 - t * t) * c0 * (1.0 + 3.0 * 0.044715 * x * x)
    return g, dg


def _lru_fwd(lx, lg, cw, cb, wa, ba, wx, bx, lam, bl, seq):
    tc = min(LRU_TILE, seq)
    nc = seq // tc
    T = bl * seq

    def body(lx_ref, lxp_ref, lg_ref, cw_ref, cb_ref, wa_ref, ba_ref, wx_ref, bx_ref, lam_ref,
             h_ref, rec_ref, hc_ref):
        i = pl.program_id(1)

        @pl.when(i == 0)
        def _():
            hc_ref[...] = jnp.zeros_like(hc_ref)

        lxv = lx_ref[...]
        prev8 = jnp.where(i > 0, lxp_ref[...], 0.0)
        xc, _ = _conv_taps(lxv, prev8, cw_ref[...], cb_ref[...])
        _, ig, _, a, mult = _lru_gates(xc, wa_ref[...], ba_ref[...], wx_ref[...], bx_ref[...], lam_ref[...])
        u = mult * (ig * xc)
        row = lax.broadcasted_iota(jnp.int32, (tc, LRU_WIDTH), 0)
        A, B = a, u
        k = 1
        while k < tc:
            a_s = jnp.where(row >= k, pltpu.roll(A, k, 0), 1.0)
            b_s = jnp.where(row >= k, pltpu.roll(B, k, 0), 0.0)
            B = A * b_s + B
            A = A * a_s
            k *= 2
        h = A * hc_ref[0:1, :] + B
        hc_ref[0:1, :] = h[tc - 1:tc, :]
        h_ref[...] = h
        g, _ = _gelu_parts(lg_ref[...])
        rec_ref[...] = h * g

    tile = pl.BlockSpec((tc, LRU_WIDTH), lambda b, i: (b * nc + i, 0))
    prev = pl.BlockSpec((8, LRU_WIDTH), lambda b, i: (jnp.maximum((b * seq + i * tc) // 8 - 1, 0), 0))
    vec = _const((1, LRU_WIDTH))
    mat = _const((LRU_WIDTH, LRU_WIDTH))
    return pl.pallas_call(
        body, name="lru_fwd", grid=(bl, nc),
        in_specs=[tile, prev, tile, _const((CONV_WIDTH, LRU_WIDTH)), vec, mat, vec, mat, vec, vec],
        out_specs=[tile, tile],
        out_shape=[jax.ShapeDtypeStruct((T, LRU_WIDTH), F32), jax.ShapeDtypeStruct((T, LRU_WIDTH), F32)],
        scratch_shapes=[pltpu.VMEM((8, LRU_WIDTH), F32)],
        compiler_params=_params(("arbitrary", "arbitrary")),
    )(lx, lx, lg, cw, cb, wa, ba, wx, bx, lam)


def _outproj(x2d, att, rec, ga, gr, wout):
    T = x2d.shape[0]
    tm = TOKEN_TILE

    def body(x_ref, a_ref, r_ref, ga_ref, gr_ref, w_ref, o_ref):
        a = a_ref[...]
        rc = r_ref[...]
        na = a * lax.rsqrt(jnp.mean(a * a, axis=-1, keepdims=True) + NORM_EPS) * ga_ref[...]
        nr = rc * lax.rsqrt(jnp.mean(rc * rc, axis=-1, keepdims=True) + NORM_EPS) * gr_ref[...]
        o_ref[...] = (x_ref[...]
                      + jnp.dot(na.astype(MXU_DTYPE), w_ref[:ATT_WIDTH, :], preferred_element_type=F32)
                      + jnp.dot(nr.astype(MXU_DTYPE), w_ref[ATT_WIDTH:, :], preferred_element_type=F32))

    row = lambda w: pl.BlockSpec((tm, w), lambda i: (i, 0))
    return pl.pallas_call(
        body, name="outproj", grid=(T // tm,),
        in_specs=[row(D_MODEL), row(ATT_WIDTH), row(LRU_WIDTH), _const((1, ATT_WIDTH)), _const((1, LRU_WIDTH)),
                  _const((D_MODEL, D_MODEL))],
        out_specs=row(D_MODEL),
        out_shape=jax.ShapeDtypeStruct((T, D_MODEL), F32),
        compiler_params=_params(("parallel",)),
    )(x2d, att, rec, ga, gr, wout)


def _mlp_fwd(x2, g2, wg, wu, wd, target):
    T = x2.shape[0]
    tm = TOKEN_TILE
    dff = wg.shape[1]

    def body(x_ref, g_ref, wg_ref, wu_ref, wd_ref, t_ref, gt_ref, up_ref, dy_ref, loss_ref):
        @pl.when(pl.program_id(0) == 0)
        def _():
            loss_ref[...] = jnp.zeros_like(loss_ref)

        x = x_ref[...]
        r = lax.rsqrt(jnp.mean(x * x, axis=-1, keepdims=True) + NORM_EPS)
        h = (x * r * g_ref[...]).astype(MXU_DTYPE)
        gt = jnp.dot(h, wg_ref[...], preferred_element_type=F32)
        up = jnp.dot(h, wu_ref[...], preferred_element_type=F32)
        gt_ref[...] = gt
        up_ref[...] = up
        act = (gt * _sigmoid(gt) * up).astype(MXU_DTYPE)
        y = x + jnp.dot(act, wd_ref[...], preferred_element_type=F32)
        e = y - t_ref[...]
        dy_ref[...] = e * (1.0 / D_MODEL)
        loss_ref[...] += jnp.sum(e * e)

    row = lambda w: pl.BlockSpec((tm, w), lambda i: (i, 0))
    return pl.pallas_call(
        body, name="mlp_fwd", grid=(T // tm,),
        in_specs=[row(D_MODEL), _const((1, D_MODEL)), _const((D_MODEL, dff)), _const((D_MODEL, dff)),
                  _const((dff, D_MODEL)), row(D_MODEL)],
        out_specs=[row(dff), row(dff), row(D_MODEL), _const((8, 128))],
        out_shape=[jax.ShapeDtypeStruct((T, dff), F32), jax.ShapeDtypeStruct((T, dff), F32),
                   jax.ShapeDtypeStruct((T, D_MODEL), F32), jax.ShapeDtypeStruct((8, 128), F32)],
        compiler_params=_params(("arbitrary",), VMEM_LARGE),
    )(x2, g2, wg, wu, wd, target)


def _nt(a, b):
    return lax.dot_general(a, b, (((1,), (1,)), ((), ())), preferred_element_type=F32)


def _tn(a, b):
    return lax.dot_general(a, b, (((0,), (0,)), ((), ())), preferred_element_type=F32)


def _mlp_bwd(dy, x2, gt, up, g2, wg, wu, wd):
    T = x2.shape[0]
    tm = TOKEN_TILE
    dff = wg.shape[1]

    def body(dy_ref, x_ref, gt_ref, up_ref, g_ref, wg_ref, wu_ref, wd_ref,
             dx_ref, dxb_ref, dgt_ref, dup_ref, act_ref, h_ref, dyb_ref, dg_ref):
        @pl.when(pl.program_id(0) == 0)
        def _():
            dg_ref[...] = jnp.zeros_like(dg_ref)

        dy_v = dy_ref[...]
        dyb = dy_v.astype(MXU_DTYPE)
        dyb_ref[...] = dyb
        x = x_ref[...]
        r = lax.rsqrt(jnp.mean(x * x, axis=-1, keepdims=True) + NORM_EPS)
        xh = x * r
        h_ref[...] = (xh * g_ref[...]).astype(MXU_DTYPE)
        gt_v = gt_ref[...]
        up_v = up_ref[...]
        sg = _sigmoid(gt_v)
        silu = gt_v * sg
        act_ref[...] = (silu * up_v).astype(MXU_DTYPE)
        dact = _nt(dyb, wd_ref[...])
        dup = (dact * silu).astype(MXU_DTYPE)
        dgt = (dact * up_v * (sg * (1.0 + gt_v * (1.0 - sg)))).astype(MXU_DTYPE)
        dup_ref[...] = dup
        dgt_ref[...] = dgt
        dh = _nt(dgt, wg_ref[...]) + _nt(dup, wu_ref[...])
        dg_ref[...] += jnp.sum(dh * xh, axis=0, keepdims=True)
        dxh = dh * g_ref[...]
        dx = dy_v + r * (dxh - xh * jnp.mean(dxh * xh, axis=-1, keepdims=True))
        dx_ref[...] = dx
        dxb_ref[...] = dx.astype(MXU_DTYPE)

    row = lambda w: pl.BlockSpec((tm, w), lambda i: (i, 0))
    return pl.pallas_call(
        body, name="mlp_bwd", grid=(T // tm,),
        in_specs=[row(D_MODEL), row(D_MODEL), row(dff), row(dff), _const((1, D_MODEL)),
                  _const((D_MODEL, dff)), _const((D_MODEL, dff)), _const((dff, D_MODEL))],
        out_specs=[row(D_MODEL), row(D_MODEL), row(dff), row(dff), row(dff), row(D_MODEL), row(D_MODEL),
                   _const((1, D_MODEL))],
        out_shape=[jax.ShapeDtypeStruct((T, D_MODEL), F32), jax.ShapeDtypeStruct((T, D_MODEL), MXU_DTYPE),
                   jax.ShapeDtypeStruct((T, dff), MXU_DTYPE), jax.ShapeDtypeStruct((T, dff), MXU_DTYPE),
                   jax.ShapeDtypeStruct((T, dff), MXU_DTYPE), jax.ShapeDtypeStruct((T, D_MODEL), MXU_DTYPE),
                   jax.ShapeDtypeStruct((T, D_MODEL), MXU_DTYPE), jax.ShapeDtypeStruct((1, D_MODEL), F32)],
        compiler_params=_params(("arbitrary",), VMEM_LARGE),
    )(dy, x2, gt, up, g2, wg, wu, wd)


def _matmul_tn(a, b, tn, name):
    T, K = a.shape
    N = b.shape[1]
    tt = min(512, T)

    def body(a_ref, b_ref, o_ref):
        @pl.when(pl.program_id(1) == 0)
        def _():
            o_ref[...] = jnp.zeros_like(o_ref)

        o_ref[...] += _tn(a_ref[...], b_ref[...])

    return pl.pallas_call(
        body, name=name, grid=(N // tn, T // tt),
        in_specs=[pl.BlockSpec((tt, K), lambda n, t: (t, 0)), pl.BlockSpec((tt, tn), lambda n, t: (t, n))],
        out_specs=pl.BlockSpec((K, tn), lambda n, t: (0, n)),
        out_shape=jax.ShapeDtypeStruct((K, N), F32),
        compiler_params=_params(("parallel", "arbitrary"), VMEM_LARGE),
    )(a, b)


def _outproj_bwd(dx2b, att, att_x, rec, ga, gr, wout):
    T = att.shape[0]
    tm = TOKEN_TILE

    def body(dx_ref, a_ref, ax_ref, r_ref, ga_ref, gr_ref, w_ref, datt_ref, delta_ref, drec_ref, mix_ref, dga_ref, dgr_ref):
        @pl.when(pl.program_id(0) == 0)
        def _():
            dga_ref[...] = jnp.zeros_like(dga_ref)
            dgr_ref[...] = jnp.zeros_like(dgr_ref)

        dmix = _nt(dx_ref[...], w_ref[...])

        def norm_bwd(v, g, dn):
            rr = lax.rsqrt(jnp.mean(v * v, axis=-1, keepdims=True) + NORM_EPS)
            vh = v * rr
            dvh = dn * g
            dv = rr * (dvh - vh * jnp.mean(dvh * vh, axis=-1, keepdims=True))
            return vh, dv, jnp.sum(dn * vh, axis=0, keepdims=True)

        a = a_ref[...]
        ah, datt, dga = norm_bwd(a, ga_ref[...], dmix[:, :ATT_WIDTH])
        rh, drec, dgr = norm_bwd(r_ref[...], gr_ref[...], dmix[:, ATT_WIDTH:])
        dga_ref[...] += dga
        dgr_ref[...] += dgr
        mix_ref[:, :ATT_WIDTH] = (ah * ga_ref[...]).astype(MXU_DTYPE)
        mix_ref[:, ATT_WIDTH:] = (rh * gr_ref[...]).astype(MXU_DTYPE)
        dattb = datt.astype(MXU_DTYPE)
        datt_ref[...] = dattb
        drec_ref[...] = drec
        lo = _lo_mask()
        prod = dattb.astype(F32) * ax_ref[...]
        for p in range(N_PAIR):
            delta_ref[:, PAIR * p:PAIR * (p + 1)] = _half_sums(prod[:, PAIR * p:PAIR * (p + 1)], lo)

    row = lambda w: pl.BlockSpec((tm, w), lambda i: (i, 0))
    return pl.pallas_call(
        body, name="outproj_bwd", grid=(T // tm,),
        in_specs=[row(D_MODEL), row(ATT_WIDTH), row(ATT_WIDTH), row(LRU_WIDTH), _const((1, ATT_WIDTH)), _const((1, LRU_WIDTH)),
                  _const((D_MODEL, D_MODEL))],
        out_specs=[row(ATT_WIDTH), row(ATT_WIDTH), row(LRU_WIDTH), row(D_MODEL),
                   _const((1, ATT_WIDTH)), _const((1, LRU_WIDTH))],
        out_shape=[jax.ShapeDtypeStruct((T, ATT_WIDTH), MXU_DTYPE), jax.ShapeDtypeStruct((T, ATT_WIDTH), F32),
                   jax.ShapeDtypeStruct((T, LRU_WIDTH), F32), jax.ShapeDtypeStruct((T, D_MODEL), MXU_DTYPE),
                   jax.ShapeDtypeStruct((1, ATT_WIDTH), F32), jax.ShapeDtypeStruct((1, LRU_WIDTH), F32)],
        compiler_params=_params(("arbitrary",)),
    )(dx2b, att, att_x, rec, ga, gr, wout)


def _lru_bwd(drec, lg, h, lx, cw, cb, wa, ba, wx, bx, lam, bl, seq):
    tc = min(LRU_TILE, seq)
    nc = seq // tc
    T = bl * seq
    n = tc

    def body(dr_ref, lg_ref, h_ref, hp_ref, lx_ref, lxp_ref, cw_ref, cb_ref, wa_ref, ba_ref, wx_ref, bx_ref, lam_ref,
             dlx_ref, dlg_ref, dwa_ref, dwx_ref, small_ref, gc_ref, dxn_ref):
        b, i = pl.program_id(0), pl.program_id(1)
        ir = nc - 1 - i

        @pl.when((b == 0) & (i == 0))
        def _():
            dwa_ref[...] = jnp.zeros_like(dwa_ref)
            dwx_ref[...] = jnp.zeros_like(dwx_ref)
            small_ref[...] = jnp.zeros_like(small_ref)

        @pl.when(i == 0)
        def _():
            gc_ref[...] = jnp.zeros_like(gc_ref)
            dxn_ref[...] = jnp.zeros_like(dxn_ref)

        cw = cw_ref[...]
        lam_v = lam_ref[...]
        lxv = lx_ref[...]
        prev8 = jnp.where(ir > 0, lxp_ref[...], 0.0)
        xc, shifted = _conv_taps(lxv, prev8, cw, cb_ref[...])
        r, ig, sp, a, mult = _lru_gates(xc, wa_ref[...], ba_ref[...], wx_ref[...], bx_ref[...], lam_v)
        hv = h_ref[...]
        drv = dr_ref[...]
        g, dg = _gelu_parts(lg_ref[...])
        dlg_ref[...] = drv * hv * dg
        dh = drv * g

        row = lax.broadcasted_iota(jnp.int32, (n, LRU_WIDTH), 0)
        A = jnp.where(row < n - 1, pltpu.roll(a, n - 1, 0), 0.0)
        B = dh + jnp.where(row == n - 1, gc_ref[0:1, :], 0.0)
        k = 1
        while k < n:
            a_s = jnp.where(row < n - k, pltpu.roll(A, n - k, 0), 1.0)
            b_s = jnp.where(row < n - k, pltpu.roll(B, n - k, 0), 0.0)
            B = B + A * b_s
            A = A * a_s
            k *= 2
        gs = B
        gc_ref[0:1, :] = a[0:1, :] * gs[0:1, :]

        hprev8 = jnp.where(ir > 0, hp_ref[...], 0.0)
        h_prev = pltpu.roll(jnp.concatenate([hprev8, hv], axis=0), 1, 0)[8:]
        da = gs * h_prev
        ix = ig * xc
        dmult = gs * ix
        dig = gs * mult * xc
        dxc = gs * mult * ig
        dlog_a = da * a - dmult * (a * a) / mult
        dr_gate = dlog_a * (-LRU_C * sp)
        dza = dr_gate * r * (1.0 - r)
        dzx = dig * ig * (1.0 - ig)
        dzab = dza.astype(MXU_DTYPE)
        dzxb = dzx.astype(MXU_DTYPE)
        xcb = xc.astype(MXU_DTYPE)
        dwa_ref[...] += _tn(xcb, dzab)
        dwx_ref[...] += _tn(xcb, dzxb)
        dxc = dxc + _nt(dzab, wa_ref[...]) + _nt(dzxb, wx_ref[...])

        ds = jnp.concatenate([dxc, dxn_ref[...]], axis=0)
        dlx = cw[CONV_WIDTH - 1:CONV_WIDTH] * dxc
        for k in range(1, CONV_WIDTH):
            dlx = dlx + cw[CONV_WIDTH - 1 - k:CONV_WIDTH - k] * pltpu.roll(ds, n + 8 - k, 0)[:n]
        dlx_ref[...] = dlx
        dxn_ref[...] = dxc[0:8, :]

        colsum = lambda v: jnp.sum(v, axis=0, keepdims=True)
        small_ref[0:1, :] += colsum(dza)
        small_ref[1:2, :] += colsum(dzx)
        small_ref[2:3, :] += colsum(dlog_a * r) * (LRU_C * _sigmoid(-lam_v))
        small_ref[3:4, :] += colsum(dxc)
        for k in range(CONV_WIDTH):
            j = CONV_WIDTH - 1 - k
            small_ref[4 + j:5 + j, :] += colsum(dxc * shifted[k])

    tile = pl.BlockSpec((tc, LRU_WIDTH), lambda b, i: (b * nc + (nc - 1 - i), 0))
    prev = pl.BlockSpec((8, LRU_WIDTH), lambda b, i: (jnp.maximum((b * seq + (nc - 1 - i) * tc) // 8 - 1, 0), 0))
    vec = _const((1, LRU_WIDTH))
    mat = _const((LRU_WIDTH, LRU_WIDTH))
    return pl.pallas_call(
        body, name="lru_bwd", grid=(bl, nc),
        in_specs=[tile, tile, tile, prev, tile, prev, _const((CONV_WIDTH, LRU_WIDTH)), vec, mat, vec, mat, vec, vec],
        out_specs=[tile, tile, mat, mat, _const((8, LRU_WIDTH))],
        out_shape=[jax.ShapeDtypeStruct((T, LRU_WIDTH), F32), jax.ShapeDtypeStruct((T, LRU_WIDTH), F32),
                   jax.ShapeDtypeStruct((LRU_WIDTH, LRU_WIDTH), F32), jax.ShapeDtypeStruct((LRU_WIDTH, LRU_WIDTH), F32),
                   jax.ShapeDtypeStruct((8, LRU_WIDTH), F32)],
        scratch_shapes=[pltpu.VMEM((8, LRU_WIDTH), F32), pltpu.VMEM((8, LRU_WIDTH), F32)],
        compiler_params=_params(("arbitrary", "arbitrary")),
    )(drec, lg, h, h, lx, lx, cw, cb, wa, ba, wx, bx, lam)


def _attn_bwd(qn, kn, vb, dob, lse, delta, frow, fstart, bl, seq, slabs):
    tq = min(ATT_TILE, seq)
    nq = seq // tq
    T = bl * seq
    n = len(slabs)

    def body(fs_ref, q_ref, k_ref, v_ref, do_ref, lse_ref, dl_ref, fr_ref, *rest):
        s_in, (dq_ref, dk_ref, dv_ref, df_ref), s_out, sems = rest[:n], rest[n:n + 4], rest[n + 4:2 * n + 4], rest[2 * n + 4:]
        b, p, j = pl.program_id(0), pl.program_id(1), pl.program_id(2)
        copies = _chip_copies(s_in, s_out, *sems, scatter=True)

        @pl.when((b == 0) & (p == 0) & (j == 0))
        def _():
            for cp in copies:
                cp.start()

        @pl.when(j == 0)
        def _():
            dq_ref[...] = jnp.zeros_like(dq_ref)

        lane = lax.broadcasted_iota(jnp.int32, (1, PAIR), 1)
        rows = lax.broadcasted_iota(jnp.int32, (tq, tq), 0)
        cols = lax.broadcasted_iota(jnp.int32, (tq, tq), 1)
        causal = cols <= rows
        kv = k_ref[...]
        vv = v_ref[...]
        hms = [(lane >= HEAD_DIM * hh) & (lane < HEAD_DIM * (hh + 1)) for hh in range(2)]
        khs = [jnp.where(hm, kv, jnp.zeros_like(kv)) for hm in hms]
        fks = [fr_ref[0, 0, hh:hh + 1, :] for hh in range(2)]
        bases = [((b * N_PAIR + p) * 2 + hh) * nq for hh in range(2)]

        def block(i, carry, masked):
            dk, dv, dfs = carry
            start = pl.multiple_of(i * tq, tq)
            qi = q_ref[pl.ds(start, tq), :]
            doi = do_ref[pl.ds(start, tq), :]
            dq = jnp.zeros((tq, PAIR), F32)
            new_dfs = []
            for hh in range(2):
                c0 = HEAD_DIM * hh
                qh = jnp.where(hms[hh], qi, jnp.zeros_like(qi))
                doh = jnp.where(hms[hh], doi, jnp.zeros_like(doi))
                s = _nt(qh, kv) - (fks[hh] - fs_ref[bases[hh] + i])
                if masked:
                    s = jnp.where(causal, s, NEG)
                pr = jnp.exp(s - lse_ref[pl.ds(start, tq), c0:c0 + 1])
                dp = _nt(doh, vv)
                ds = pr * (dp - dl_ref[pl.ds(start, tq), c0:c0 + 1])
                dsb = ds.astype(MXU_DTYPE)
                dv = dv + _tn(pr.astype(MXU_DTYPE), doh)
                dk = dk + _tn(dsb, qh)
                dq = dq + jnp.dot(dsb, khs[hh], preferred_element_type=F32)
                new_dfs.append(dfs[hh] - jnp.sum(ds, axis=0, keepdims=True))
            dq_ref[pl.ds(start, tq), :] += dq
            return dk, dv, tuple(new_dfs)

        zero = jnp.zeros((tq, PAIR), F32)
        carry = block(j, (zero, zero, (jnp.zeros((1, tq), F32), jnp.zeros((1, tq), F32))), True)
        dk, dv, dfs = lax.fori_loop(j + 1, nq, functools.partial(block, masked=False), carry)
        for hh in range(2):
            df_ref[0, 0, hh:hh + 1, :] = dfs[hh]
        dk_ref[...] = dk
        dv_ref[...] = dv

        @pl.when((b == bl - 1) & (p == N_PAIR - 1) & (j == nq - 1))
        def _():
            for cp in copies:
                cp.wait()

    blk = pl.BlockSpec((tq, PAIR), lambda b, p, j: (b * nq + j, p))
    full = pl.BlockSpec((seq, PAIR), lambda b, p, j: (b, p))
    fblk = pl.BlockSpec((1, 1, 2, tq), lambda b, p, j: (b, p, 0, j))
    hbm = pl.BlockSpec(memory_space=pl.ANY)
    return pl.pallas_call(
        body, name="attn_bwd", grid=(bl, N_PAIR, nq),
        in_specs=[pl.BlockSpec(memory_space=pltpu.SMEM), full, blk, blk, full, full, full, fblk] + [hbm] * n,
        out_specs=[full, blk, blk, fblk] + [hbm] * n,
        out_shape=[jax.ShapeDtypeStruct((T, ATT_WIDTH), F32), jax.ShapeDtypeStruct((T, ATT_WIDTH), F32),
                   jax.ShapeDtypeStruct((T, ATT_WIDTH), F32), jax.ShapeDtypeStruct((bl, N_PAIR, 2, seq), F32)]
        + [jax.ShapeDtypeStruct(s.shape, s.dtype) for s in slabs],
        scratch_shapes=_chip_sems(n),
        compiler_params=_params(("arbitrary", "arbitrary", "arbitrary"), VMEM_LARGE),
    )(fstart, qn, kn, vb, dob, lse, delta, frow, *slabs)


def _forget_bwd(dfcol, f2d, bf, bl, seq):
    def body(d_ref, z_ref, b_ref, o_ref, db_ref):
        @pl.when(pl.program_id(0) == 0)
        def _():
            db_ref[...] = jnp.zeros_like(db_ref)

        d = d_ref[...]
        row = lax.broadcasted_iota(jnp.int32, (seq, F_PAD), 0)
        k = 1
        while k < seq:
            d = d + jnp.where(row < seq - k, pltpu.roll(d, seq - k, 0), 0.0)
            k *= 2
        dz = d * _sigmoid(-(z_ref[...] + b_ref[...]))
        o_ref[...] = dz
        db_ref[...] += jnp.sum(dz, axis=0, keepdims=True)

    blk = pl.BlockSpec((seq, F_PAD), lambda b: (b, 0))
    return pl.pallas_call(
        body, name="forget_bwd", grid=(bl,),
        in_specs=[blk, blk, _const((1, F_PAD))],
        out_specs=[blk, _const((1, F_PAD))],
        out_shape=[jax.ShapeDtypeStruct(f2d.shape, F32), jax.ShapeDtypeStruct((1, F_PAD), F32)],
        compiler_params=_params(("arbitrary",)),
    )(dfcol, f2d, bf)


def _inproj_bwd(dq, dk, dv, qkv, df, dlx, dlg, x2d, dx2, g1, gq2, gk2, wcat):
    T = x2d.shape[0]
    tm = TOKEN_TILE

    def body(dq_ref, dk_ref, dv_ref, qkv_ref, df_ref, dlx_ref, dlg_ref, x_ref, dx2_ref, g1_ref, gq_ref, gk_ref, w_ref,
             gx_ref, dp_ref, h_ref, dg1_ref, dgq_ref, dgk_ref):
        @pl.when(pl.program_id(0) == 0)
        def _():
            dg1_ref[...] = jnp.zeros_like(dg1_ref)
            dgq_ref[...] = jnp.zeros_like(dgq_ref)
            dgk_ref[...] = jnp.zeros_like(dgk_ref)

        lo = _lo_mask()

        def head_norm_bwd(t, g2, dy):
            rr = lax.rsqrt(_half_sums(t * t, lo) * (1.0 / HEAD_DIM) + NORM_EPS)
            th = t * rr
            dth = dy * g2
            mm = _half_sums(dth * th, lo) * (1.0 / HEAD_DIM)
            return rr * (dth - th * mm), jnp.sum(dy * th, axis=0, keepdims=True)

        dgq = jnp.zeros((1, PAIR), F32)
        dgk = jnp.zeros((1, PAIR), F32)
        for p in range(N_PAIR):
            cq = slice(PAIR * p, PAIR * (p + 1))
            ck = slice(ATT_WIDTH + PAIR * p, ATT_WIDTH + PAIR * (p + 1))
            dqp, g_ = head_norm_bwd(qkv_ref[:, cq], gq_ref[...], dq_ref[:, cq] * QK_SCALE)
            dgq = dgq + g_
            dp_ref[:, cq] = dqp.astype(MXU_DTYPE)
            dkp, g_ = head_norm_bwd(qkv_ref[:, ck], gk_ref[...], dk_ref[:, cq])
            dgk = dgk + g_
            dp_ref[:, ck] = dkp.astype(MXU_DTYPE)
        dgq_ref[...] += dgq
        dgk_ref[...] += dgk
        f0 = 3 * ATT_WIDTH
        dp_ref[:, 2 * ATT_WIDTH:f0] = dv_ref[...].astype(MXU_DTYPE)
        dp_ref[:, f0:f0 + F_PAD] = df_ref[...].astype(MXU_DTYPE)
        dp_ref[:, f0 + F_PAD:f0 + F_PAD + LRU_WIDTH] = dlx_ref[...].astype(MXU_DTYPE)
        dp_ref[:, f0 + F_PAD + LRU_WIDTH:] = dlg_ref[...].astype(MXU_DTYPE)
        dh = _nt(dp_ref[...], w_ref[...])
        x = x_ref[...]
        r = lax.rsqrt(jnp.mean(x * x, axis=-1, keepdims=True) + NORM_EPS)
        xh = x * r
        h_ref[...] = (xh * g1_ref[...]).astype(MXU_DTYPE)
        dg1_ref[...] += jnp.sum(dh * xh, axis=0, keepdims=True)
        dxh = dh * g1_ref[...]
        gx_ref[...] = dx2_ref[...] + r * (dxh - xh * jnp.mean(dxh * xh, axis=-1, keepdims=True))

    row = lambda w: pl.BlockSpec((tm, w), lambda i: (i, 0))
    return pl.pallas_call(
        body, name="inproj_bwd", grid=(T // tm,),
        in_specs=[row(ATT_WIDTH), row(ATT_WIDTH), row(ATT_WIDTH), row(3 * ATT_WIDTH), row(F_PAD), row(LRU_WIDTH),
                  row(LRU_WIDTH), row(D_MODEL), row(D_MODEL), _const((1, D_MODEL)), _const((1, PAIR)), _const((1, PAIR)),
                  _const((D_MODEL, N_CAT))],
        out_specs=[row(D_MODEL), row(N_CAT), row(D_MODEL), _const((1, D_MODEL)), _const((1, PAIR)), _const((1, PAIR))],
        out_shape=[jax.ShapeDtypeStruct((T, D_MODEL), F32), jax.ShapeDtypeStruct((T, N_CAT), MXU_DTYPE),
                   jax.ShapeDtypeStruct((T, D_MODEL), MXU_DTYPE), jax.ShapeDtypeStruct((1, D_MODEL), F32),
                   jax.ShapeDtypeStruct((1, PAIR), F32), jax.ShapeDtypeStruct((1, PAIR), F32)],
        compiler_params=_params(("arbitrary",), VMEM_LARGE),
    )(dq, dk, dv, qkv, df, dlx, dlg, x2d, dx2, g1, gq2, gk2, wcat)


def _row_block(rows):
    for rb in (256, 176, 128, 64, 32, 16, 8):
        if rows % rb == 0:
            return rb
    return rows


def _sum_slabs(recv, name):
    _, rows, cols = recv.shape
    rb = _row_block(rows)

    def body(r_ref, o_ref):
        part = [r_ref[s].astype(F32) for s in range(4)]
        o_ref[...] = ((part[0] + part[1]) + part[2]) + part[3]

    return pl.pallas_call(
        body, name=name, grid=(rows // rb,),
        in_specs=[pl.BlockSpec((4, rb, cols), lambda i: (0, i, 0))],
        out_specs=pl.BlockSpec((rb, cols), lambda i: (i, 0)),
        out_shape=jax.ShapeDtypeStruct((rows, cols), F32),
        compiler_params=_params(("parallel",)),
    )(recv)


def _adamw_math(w, g, m, v):
    m = ADAM_B1 * m + (1.0 - ADAM_B1) * g
    v = ADAM_B2 * v + (1.0 - ADAM_B2) * (g * g)
    m_hat = m / (1.0 - ADAM_B1 ** ADAM_STEP)
    v_hat = v / (1.0 - ADAM_B2 ** ADAM_STEP)
    delta = -ADAM_LR * (m_hat / (jnp.sqrt(v_hat) + ADAM_EPS) + ADAM_WD * w)
    return delta, m, v


def _adamw_pair(mine, theirs, w, m, v, name):
    rows, cols = w.shape
    rb = _row_block(rows)

    def body(a_ref, b_ref, w_ref, m_ref, v_ref, g_ref, d_ref, nm_ref, nv_ref):
        g = a_ref[...] + b_ref[...]
        g_ref[...] = g
        d_ref[...], nm_ref[...], nv_ref[...] = _adamw_math(w_ref[...], g, m_ref[...], v_ref[...])

    blk = pl.BlockSpec((rb, cols), lambda i: (i, 0))
    return pl.pallas_call(
        body, name=name, grid=(rows // rb,),
        in_specs=[blk] * 5, out_specs=[blk] * 4,
        out_shape=[jax.ShapeDtypeStruct((rows, cols), F32)] * 4,
        compiler_params=_params(("parallel",)),
    )(mine, theirs, w, m, v)


def _sum_packs(recv):
    _, rows, cols = recv.shape

    def body(r_ref, o_ref):
        acc = r_ref[0]
        for d in range(1, 8):
            acc = acc + r_ref[d]
        o_ref[...] = acc

    return pl.pallas_call(
        body, name="sum_packs", grid=(1,),
        in_specs=[_const(recv.shape)], out_specs=_const((rows, cols)),
        out_shape=jax.ShapeDtypeStruct((rows, cols), F32),
        compiler_params=_params(("arbitrary",)),
    )(recv)


def _adamw_small(w, g, m, v):
    rows, cols = w.shape

    def body(w_ref, g_ref, m_ref, v_ref, d_ref, nm_ref, nv_ref):
        d_ref[...], nm_ref[...], nv_ref[...] = _adamw_math(w_ref[...], g_ref[...], m_ref[...], v_ref[...])

    blk = _const((rows, cols))
    return pl.pallas_call(
        body, name="adamw_small", grid=(1,),
        in_specs=[blk] * 4, out_specs=[blk] * 3,
        out_shape=[jax.ShapeDtypeStruct((rows, cols), F32)] * 3,
        compiler_params=_params(("arbitrary",)),
    )(w, g, m, v)


SMALL = ["norm1_g", "q_norm_g", "k_norm_g", "b_f", "conv_b", "w_a", "b_a", "w_x", "b_x", "lam",
         "attn_out_g", "lru_out_g", "norm2_g"]


def _pack(parts, rows=None):
    flat = jnp.concatenate([p.reshape(-1) for p in parts])
    n = flat.shape[0]
    total = -(-n // 1024) * 1024 if rows is None else rows * 128
    return jnp.pad(flat, (0, total - n)).reshape(-1, 128)


def _unpack(pack, shapes):
    flat = pack.reshape(-1)
    out, off = [], 0
    for s in shapes:
        n = math.prod(s)
        out.append(flat[off:off + n].reshape(s))
        off += n
    return out


def _block_diag(w):
    eye = jnp.eye(LRU_BLOCKS, dtype=w.dtype)
    return (w[:, :, None, :] * eye[:, None, :, None]).reshape(LRU_WIDTH, LRU_WIDTH)


def _diag_blocks(m):
    m4 = m.reshape(LRU_BLOCKS, LRU_WIDTH // LRU_BLOCKS, LRU_BLOCKS, LRU_WIDTH // LRU_BLOCKS)
    return jnp.stack([m4[n, :, n, :] for n in range(LRU_BLOCKS)])


def kernel(x, norm1_g, w_in, q_norm_g, k_norm_g, b_f, conv_w, conv_b, w_a, b_a, w_x, b_x, lam, attn_out_g, lru_out_g, w_out, norm2_g, w_gate, w_up, w_down, loss_target, m_norm1_g, m_w_in, m_q_norm_g, m_k_norm_g, m_b_f, m_conv_w, m_conv_b, m_w_a, m_b_a, m_w_x, m_b_x, m_lam, m_attn_out_g, m_lru_out_g, m_w_out, m_norm2_g, m_w_gate, m_w_up, m_w_down, v_norm1_g, v_w_in, v_q_norm_g, v_k_norm_g, v_b_f, v_conv_w, v_conv_b, v_w_a, v_b_a, v_w_x, v_b_x, v_lam, v_attn_out_g, v_lru_out_g, v_w_out, v_norm2_g, v_w_gate, v_w_up, v_w_down):
    args = dict(locals())
    bl, seq, _ = x.shape
    T = bl * seq
    tq = min(ATT_TILE, seq)
    nq = seq // tq
    dff = w_gate.shape[2] * 4
    my_chip = 2 * lax.axis_index("x") + lax.axis_index("y")

    g_in, g_cw = _gather_over_chips([w_in[0].astype(MXU_DTYPE), conv_w[0]])
    later_shards = [w_out[0].astype(MXU_DTYPE), w_gate[0].astype(MXU_DTYPE), w_up[0].astype(MXU_DTYPE),
                    w_down[0].astype(MXU_DTYPE)]
    col_cat = lambda g: jnp.transpose(g, (1, 0, 2)).reshape(g.shape[1], -1)
    w_in_full = col_cat(g_in)
    f0 = 3 * ATT_WIDTH
    wcat = jnp.concatenate([w_in_full[:, :f0], jnp.pad(w_in_full[:, f0:f0 + HEADS], ((0, 0), (0, F_PAD - HEADS))),
                            w_in_full[:, f0 + HEADS:]], axis=1)
    cw_full = col_cat(g_cw)
    wa_bd = _block_diag(w_a[0]).astype(MXU_DTYPE)
    wx_bd = _block_diag(w_x[0]).astype(MXU_DTYPE)
    gq2 = jnp.tile(q_norm_g, (1, 2))
    gk2 = jnp.tile(k_norm_g, (1, 2))
    bf_pad = jnp.pad(b_f, ((0, 0), (0, F_PAD - HEADS)))

    x2d = x.reshape(T, D_MODEL)
    target2d = loss_target.reshape(T, D_MODEL)

    qkv, qn, kn, vb, f2d, lx, lg = _inproj(x2d, norm1_g, wcat, gq2, gk2)
    fcol = _forget_cumsum(f2d, bf_pad, bl, seq)
    frow = jnp.transpose(fcol.reshape(bl, seq, F_PAD)[:, :, :HEADS], (0, 2, 1)).reshape(bl, N_PAIR, 2, seq)
    fstart = frow[:, :, :, ::tq].reshape(-1)
    att, att_x, lse, g_out, g_gate, g_up, g_down = _attn_fwd(qn, kn, vb, frow, fstart, bl, seq, later_shards)
    wout_full = g_out.reshape(D_MODEL, D_MODEL)
    wg_full, wu_full = col_cat(g_gate), col_cat(g_up)
    wd_full = g_down.reshape(dff, D_MODEL)
    h, rec = _lru_fwd(lx, lg, cw_full, conv_b, wa_bd, b_a, wx_bd, b_x, lam, bl, seq)
    x2 = _outproj(x2d, att, rec, attn_out_g, lru_out_g, wout_full)
    gt, up, dy, sq_err = _mlp_fwd(x2, norm2_g, wg_full, wu_full, wd_full, target2d)
    loss = lax.psum(sq_err[0, 0] * (0.5 / D_MODEL), ("x", "y", "c"))

    dx2, dx2b, dgtb, dupb, actb, h2b, dyb, dg2 = _mlp_bwd(dy, x2, gt, up, norm2_g, wg_full, wu_full, wd_full)
    dw_down = _matmul_tn(actb, dyb, 512, "dw_down")
    dw_gate = _matmul_tn(h2b, dgtb, dff // 2, "dw_gate")
    dw_up = _matmul_tn(h2b, dupb, dff // 2, "dw_up")
    dattb, delta, drec, mixb, dga, dgr = _outproj_bwd(dx2b, att, att_x, rec, attn_out_g, lru_out_g, wout_full)
    dw_out = _matmul_tn(mixb, dx2b, D_MODEL, "dw_out")
    dlx, dlg, dwa_bd, dwx_bd, lru_small = _lru_bwd(drec, lg, h, lx, cw_full, conv_b, wa_bd, b_a, wx_bd, b_x, lam, bl, seq)
    col_split = lambda g: jnp.transpose(g.reshape(g.shape[0], 4, -1), (1, 0, 2))
    early_slabs = [dw_out.reshape(4, D_MODEL // 4, D_MODEL), col_split(dw_gate), col_split(dw_up),
                   dw_down.reshape(4, dff // 4, D_MODEL)]
    dq, dk, dv, dfrow, *recv_early = _attn_bwd(qn, kn, vb, dattb, lse, delta, frow, fstart, bl, seq, early_slabs)
    dfcol = jnp.pad(jnp.transpose(dfrow.reshape(bl, HEADS, seq), (0, 2, 1)), ((0, 0), (0, 0), (0, F_PAD - HEADS)))
    df, dbf = _forget_bwd(dfcol.reshape(T, F_PAD), f2d, bf_pad, bl, seq)
    grad_x, dprojb, h1b, dg1, dgq, dgk = _inproj_bwd(dq, dk, dv, qkv, df, dlx, dlg, x2d, dx2, norm1_g, gq2, gk2, wcat)
    dwcat = _matmul_tn(h1b, dprojb, N_CAT // 3, "dw_in")
    dw_in = jnp.concatenate([dwcat[:, :f0 + HEADS], dwcat[:, f0 + F_PAD:]], axis=1)

    small_grads = {
        "norm1_g": dg1, "q_norm_g": dgq[:, :HEAD_DIM] + dgq[:, HEAD_DIM:], "k_norm_g": dgk[:, :HEAD_DIM] + dgk[:, HEAD_DIM:],
        "b_f": dbf[:, :HEADS], "conv_b": lru_small[3:4], "w_a": _diag_blocks(dwa_bd)[None], "b_a": lru_small[0:1],
        "w_x": _diag_blocks(dwx_bd)[None], "b_x": lru_small[1:2], "lam": lru_small[2:3],
        "attn_out_g": dga, "lru_out_g": dgr, "norm2_g": dg2}
    pack = _pack([small_grads[n] for n in SMALL] + [lru_small[4:8]])
    recv_in, recv_pack = _exchange_grads([col_split(dw_in).astype(jnp.bfloat16)], pack)
    recv = [recv_in] + recv_early
    big = ["w_in", "w_out", "w_gate", "w_up", "w_down"]
    part = [_sum_slabs(r, "sum_" + n) for r, n in zip(recv, big)]
    theirs = _swap_with_sibling(part)
    out = {}
    for n, a, b_ in zip(big, part, theirs):
        g, d, nm, nv = _adamw_pair(a, b_, args[n][0], args["m_" + n][0], args["v_" + n][0], "adamw_" + n)
        out[n] = (g[None], d[None], nm[None], nv[None])

    small_shapes = [args[n].shape for n in SMALL]
    red = _sum_packs(recv_pack)
    g_small = _unpack(red, small_shapes + [(CONV_WIDTH, LRU_WIDTH)])
    g_cw_mine = lax.dynamic_slice_in_dim(g_small[-1], my_chip * (LRU_WIDTH // 4), LRU_WIDTH // 4, axis=1)[None]
    g_list = g_small[:-1] + [g_cw_mine]
    names = SMALL + ["conv_w"]
    rows = pack.shape[0]
    d_p, m_p, v_p = _adamw_small(_pack([args[n] for n in names], rows), _pack(g_list, rows),
                                 _pack([args["m_" + n] for n in names], rows), _pack([args["v_" + n] for n in names], rows))
    shapes = [args[n].shape for n in names]
    for n, g, d, nm, nv in zip(names, g_list, _unpack(d_p, shapes), _unpack(m_p, shapes), _unpack(v_p, shapes)):
        out[n] = (g, d, nm, nv)

    order = ["norm1_g", "w_in", "q_norm_g", "k_norm_g", "b_f", "conv_w", "conv_b", "w_a", "b_a", "w_x", "b_x", "lam",
             "attn_out_g", "lru_out_g", "w_out", "norm2_g", "w_gate", "w_up", "w_down"]
    return (loss, grad_x.reshape(bl, seq, D_MODEL), *[out[n][0] for n in order], *[out[n][1] for n in order],
            *[out[n][2] for n in order], *[out[n][3] for n in order])
```

```python
import functools
import math

import jax
import jax.numpy as jnp
from jax import lax
from jax.experimental import pallas as pl
from jax.experimental.pallas import tpu as pltpu

F32 = jnp.float32
MXU_DTYPE = jnp.bfloat16
MESH = pl.DeviceIdType.MESH

D_MODEL = 1024
ATT_WIDTH = 512
LRU_WIDTH = 512
HEADS = 8
HEAD_DIM = 64
PAIR = 2 * HEAD_DIM
N_PAIR = HEADS // 2
LRU_BLOCKS = 8
CONV_WIDTH = 4
LRU_C = 8.0
NORM_EPS = 1e-6
QK_SCALE = 1.0 / math.sqrt(HEAD_DIM)
F_PAD = 128
N_CAT = 3 * ATT_WIDTH + F_PAD + 2 * LRU_WIDTH
NEG = -1e30

ADAM_LR, ADAM_B1, ADAM_B2, ADAM_EPS, ADAM_WD, ADAM_STEP = 0.001, 0.9, 0.999, 1e-08, 0.01, 10

TOKEN_TILE = 256
ATT_TILE = 512
LRU_TILE = 256
VMEM_SMALL = 32 * 1024 * 1024
VMEM_LARGE = 56 * 1024 * 1024


def _params(sem, vmem=VMEM_SMALL):
    return pltpu.CompilerParams(dimension_semantics=sem, vmem_limit_bytes=vmem)


def _const(shape):
    nd = len(shape)
    return pl.BlockSpec(shape, lambda *_: (0,) * nd)


def _sigmoid(x):
    return 1.0 / (1.0 + jnp.exp(-x))


def _half_sums(t, lo):
    s_lo = jnp.sum(jnp.where(lo, t, 0.0), axis=-1, keepdims=True)
    s_hi = jnp.sum(jnp.where(lo, 0.0, t), axis=-1, keepdims=True)
    return jnp.where(lo, s_lo, s_hi)


def _lo_mask():
    return lax.broadcasted_iota(jnp.int32, (1, PAIR), 1) < HEAD_DIM


def _other_chips(x, y):
    return [(1 - x, y), (x, 1 - y), (1 - x, 1 - y)]


def _chip_copies(ins, outs, send_sems, recv_sems, loc_sems, scatter):
    x, y, c = lax.axis_index("x"), lax.axis_index("y"), lax.axis_index("c")
    me = 2 * x + y
    copies = []
    for w in range(len(ins)):
        copies.append(pltpu.make_async_copy(ins[w].at[me] if scatter else ins[w], outs[w].at[me], loc_sems.at[w]))
        for k, (cx, cy) in enumerate(_other_chips(x, y)):
            copies.append(pltpu.make_async_remote_copy(
                src_ref=ins[w].at[2 * cx + cy] if scatter else ins[w], dst_ref=outs[w].at[me],
                send_sem=send_sems.at[3 * w + k], recv_sem=recv_sems.at[3 * w + k],
                device_id=(cx, cy, c), device_id_type=MESH))
    return copies


def _chip_sems(n):
    return [pltpu.SemaphoreType.DMA((3 * n,)), pltpu.SemaphoreType.DMA((3 * n,)), pltpu.SemaphoreType.DMA((n,))]


def _gather_over_chips(shards):
    n = len(shards)

    def body(*refs):
        copies = _chip_copies(refs[:n], refs[n:2 * n], *refs[2 * n:], scatter=False)
        for cp in copies:
            cp.start()
        for cp in copies:
            cp.wait()

    return pl.pallas_call(
        body, name="gather_weights",
        out_shape=[jax.ShapeDtypeStruct((4,) + s.shape, s.dtype) for s in shards],
        in_specs=[pl.BlockSpec(memory_space=pl.ANY)] * n,
        out_specs=[pl.BlockSpec(memory_space=pl.ANY)] * n,
        scratch_shapes=_chip_sems(n),
    )(*shards)


def _exchange_grads(slabs, pack):
    n = len(slabs)

    def body(*refs):
        ins, pack_in = refs[:n], refs[n]
        outs, pack_out = refs[n + 1:2 * n + 1], refs[2 * n + 1]
        send_sems, recv_sems, loc_sems, psend, precv = refs[2 * n + 2:]
        x, y, c = lax.axis_index("x"), lax.axis_index("y"), lax.axis_index("c")
        dev = 4 * x + 2 * y + c
        copies = _chip_copies(ins, outs, send_sems, recv_sems, loc_sems, scatter=True)
        copies.append(pltpu.make_async_copy(pack_in, pack_out.at[dev], loc_sems.at[n]))
        for k in range(1, 8):
            fx, fy, fc = (k >> 2) & 1, (k >> 1) & 1, k & 1
            tx = (1 - x) if fx else x
            ty = (1 - y) if fy else y
            tc = (1 - c) if fc else c
            copies.append(pltpu.make_async_remote_copy(
                src_ref=pack_in, dst_ref=pack_out.at[dev],
                send_sem=psend.at[k - 1], recv_sem=precv.at[k - 1],
                device_id=(tx, ty, tc), device_id_type=MESH))
        for cp in copies:
            cp.start()
        for cp in copies:
            cp.wait()

    return pl.pallas_call(
        body, name="exchange_grads",
        out_shape=[jax.ShapeDtypeStruct(s.shape, s.dtype) for s in slabs]
        + [jax.ShapeDtypeStruct((8,) + pack.shape, pack.dtype)],
        in_specs=[pl.BlockSpec(memory_space=pl.ANY)] * (n + 1),
        out_specs=[pl.BlockSpec(memory_space=pl.ANY)] * (n + 1),
        scratch_shapes=[pltpu.SemaphoreType.DMA((3 * n,)), pltpu.SemaphoreType.DMA((3 * n,)),
                        pltpu.SemaphoreType.DMA((n + 1,)),
                        pltpu.SemaphoreType.DMA((7,)), pltpu.SemaphoreType.DMA((7,))],
    )(*slabs, pack)


def _swap_with_sibling(arrs):
    n = len(arrs)

    def body(*refs):
        ins, outs = refs[:n], refs[n:2 * n]
        send_sems, recv_sems = refs[2 * n:]
        x, y, c = lax.axis_index("x"), lax.axis_index("y"), lax.axis_index("c")
        copies = []
        for w in range(n):
            cp = pltpu.make_async_remote_copy(
                src_ref=ins[w], dst_ref=outs[w], send_sem=send_sems.at[w], recv_sem=recv_sems.at[w],
                device_id=(x, y, 1 - c), device_id_type=MESH)
            cp.start()
            copies.append(cp)
        for cp in copies:
            cp.wait()

    return pl.pallas_call(
        body, name="swap_sibling",
        out_shape=[jax.ShapeDtypeStruct(a.shape, a.dtype) for a in arrs],
        in_specs=[pl.BlockSpec(memory_space=pl.ANY)] * n,
        out_specs=[pl.BlockSpec(memory_space=pl.ANY)] * n,
        scratch_shapes=[pltpu.SemaphoreType.DMA((n,)), pltpu.SemaphoreType.DMA((n,))],
    )(*arrs)


def _head_norm(t, g2, lo):
    rr = lax.rsqrt(_half_sums(t * t, lo) * (1.0 / HEAD_DIM) + NORM_EPS)
    return t * rr * g2


def _inproj(x2d, g1, wcat, gq2, gk2):
    T = x2d.shape[0]
    tm = TOKEN_TILE

    def body(x_ref, g1_ref, w_ref, gq_ref, gk_ref, qkv_ref, qn_ref, kn_ref, vb_ref, f_ref, lx_ref, lg_ref):
        x = x_ref[...]
        r = lax.rsqrt(jnp.mean(x * x, axis=-1, keepdims=True) + NORM_EPS)
        h = (x * r * g1_ref[...]).astype(MXU_DTYPE)
        proj = jnp.dot(h, w_ref[...], preferred_element_type=F32)
        qkv_ref[...] = proj[:, :3 * ATT_WIDTH]
        lo = _lo_mask()
        for p in range(N_PAIR):
            cols = slice(PAIR * p, PAIR * (p + 1))
            q = proj[:, PAIR * p:PAIR * (p + 1)]
            k = proj[:, ATT_WIDTH + PAIR * p:ATT_WIDTH + PAIR * (p + 1)]
            qn_ref[:, cols] = (_head_norm(q, gq_ref[...], lo) * QK_SCALE).astype(MXU_DTYPE)
            kn_ref[:, cols] = _head_norm(k, gk_ref[...], lo).astype(MXU_DTYPE)
        vb_ref[...] = proj[:, 2 * ATT_WIDTH:3 * ATT_WIDTH].astype(MXU_DTYPE)
        f0 = 3 * ATT_WIDTH
        f_ref[...] = proj[:, f0:f0 + F_PAD]
        lx_ref[...] = proj[:, f0 + F_PAD:f0 + F_PAD + LRU_WIDTH]
        lg_ref[...] = proj[:, f0 + F_PAD + LRU_WIDTH:]

    row = lambda w: pl.BlockSpec((tm, w), lambda i: (i, 0))
    return pl.pallas_call(
        body, name="inproj", grid=(T // tm,),
        in_specs=[row(D_MODEL), _const((1, D_MODEL)), _const((D_MODEL, N_CAT)), _const((1, PAIR)), _const((1, PAIR))],
        out_specs=[row(3 * ATT_WIDTH), row(ATT_WIDTH), row(ATT_WIDTH), row(ATT_WIDTH), row(F_PAD),
                   row(LRU_WIDTH), row(LRU_WIDTH)],
        out_shape=[jax.ShapeDtypeStruct((T, 3 * ATT_WIDTH), F32),
                   jax.ShapeDtypeStruct((T, ATT_WIDTH), MXU_DTYPE), jax.ShapeDtypeStruct((T, ATT_WIDTH), MXU_DTYPE),
                   jax.ShapeDtypeStruct((T, ATT_WIDTH), MXU_DTYPE), jax.ShapeDtypeStruct((T, F_PAD), F32),
                   jax.ShapeDtypeStruct((T, LRU_WIDTH), F32), jax.ShapeDtypeStruct((T, LRU_WIDTH), F32)],
        compiler_params=_params(("parallel",), VMEM_LARGE),
    )(x2d, g1, wcat, gq2, gk2)


def _forget_cumsum(f2d, bf, bl, seq):
    def body(z_ref, b_ref, o_ref):
        z = z_ref[...] + b_ref[...]
        lf = jnp.minimum(z, 0.0) - jnp.log(1.0 + jnp.exp(-jnp.abs(z)))
        row = lax.broadcasted_iota(jnp.int32, (seq, F_PAD), 0)
        k = 1
        while k < seq:
            lf = lf + jnp.where(row >= k, pltpu.roll(lf, k, 0), 0.0)
            k *= 2
        o_ref[...] = lf

    return pl.pallas_call(
        body, name="forget_cumsum", grid=(bl,),
        in_specs=[pl.BlockSpec((seq, F_PAD), lambda b: (b, 0)), _const((1, F_PAD))],
        out_specs=pl.BlockSpec((seq, F_PAD), lambda b: (b, 0)),
        out_shape=jax.ShapeDtypeStruct(f2d.shape, F32),
        compiler_params=_params(("parallel",)),
    )(f2d, bf)


def _attn_fwd(qn, kn, vb, frow, fstart, bl, seq, shards):
    tq = min(ATT_TILE, seq)
    nq = seq // tq
    T = bl * seq
    n = len(shards)

    def body(fs_ref, q_ref, k_ref, v_ref, fr_ref, *rest):
        g_in, (o_ref, ox_ref, lse_ref), g_out, sems = rest[:n], rest[n:n + 3], rest[n + 3:2 * n + 3], rest[2 * n + 3:]
        b, p, i = pl.program_id(0), pl.program_id(1), pl.program_id(2)
        copies = _chip_copies(g_in, g_out, *sems, scatter=False)

        @pl.when((b == 0) & (p == 0) & (i == 0))
        def _():
            for cp in copies:
                cp.start()

        lane = lax.broadcasted_iota(jnp.int32, (1, PAIR), 1)
        rows = lax.broadcasted_iota(jnp.int32, (tq, tq), 0)
        cols = lax.broadcasted_iota(jnp.int32, (tq, tq), 1)
        causal = cols <= rows
        q = q_ref[...]
        hms = [(lane >= HEAD_DIM * hh) & (lane < HEAD_DIM * (hh + 1)) for hh in range(2)]
        qhs = [jnp.where(hm, q, jnp.zeros_like(q)) for hm in hms]
        shifts = [fs_ref[((b * N_PAIR + p) * 2 + hh) * nq + i] for hh in range(2)]

        def block(j, carry, masked):
            start = pl.multiple_of(j * tq, tq)
            k = k_ref[pl.ds(start, tq), :]
            v = v_ref[pl.ds(start, tq), :]
            new = []
            for hh in range(2):
                m, l, acc, acc_lo = carry[hh]
                s = lax.dot_general(qhs[hh], k, (((1,), (1,)), ((), ())), preferred_element_type=F32)
                s = s - (fr_ref[0, 0, hh:hh + 1, pl.ds(start, tq)] - shifts[hh])
                if masked:
                    s = jnp.where(causal, s, NEG)
                m_new = jnp.maximum(m, jnp.max(s, axis=-1, keepdims=True))
                alpha = jnp.exp(m - m_new)
                pe = jnp.exp(s - m_new)
                l = alpha * l + jnp.sum(pe, axis=-1, keepdims=True)
                vh = jnp.where(hms[hh], v, jnp.zeros_like(v))
                pb = pe.astype(MXU_DTYPE)
                p_lo = (pe - pb.astype(F32)).astype(MXU_DTYPE)
                acc = alpha * acc + jnp.dot(pb, vh, preferred_element_type=F32)
                acc_lo = alpha * acc_lo + jnp.dot(p_lo, vh, preferred_element_type=F32)
                new.append((m_new, l, acc, acc_lo))
            return tuple(new)

        init = (jnp.full((tq, 1), NEG, F32), jnp.zeros((tq, 1), F32), jnp.zeros((tq, PAIR), F32),
                jnp.zeros((tq, PAIR), F32))
        carry = lax.fori_loop(0, i, functools.partial(block, masked=False), (init, init))
        carry = block(i, carry, True)
        out = jnp.zeros((tq, PAIR), F32)
        out_x = jnp.zeros((tq, PAIR), F32)
        lse = jnp.zeros((tq, PAIR), F32)
        for hh in range(2):
            m, l, acc, acc_lo = carry[hh]
            inv_l = 1.0 / l
            out = out + acc * inv_l
            out_x = out_x + (acc + acc_lo) * inv_l
            lse = jnp.where(hms[hh], m + jnp.log(l), lse)
        o_ref[...] = out
        ox_ref[...] = out_x
        lse_ref[...] = lse

        @pl.when((b == bl - 1) & (p == N_PAIR - 1) & (i == nq - 1))
        def _():
            for cp in copies:
                cp.wait()

    blk = pl.BlockSpec((tq, PAIR), lambda b, p, i: (b * nq + i, p))
    full = pl.BlockSpec((seq, PAIR), lambda b, p, i: (b, p))
    return pl.pallas_call(
        body, name="attn_fwd", grid=(bl, N_PAIR, nq),
        in_specs=[pl.BlockSpec(memory_space=pltpu.SMEM), blk, full, full,
                  pl.BlockSpec((1, 1, 2, seq), lambda b, p, i: (b, p, 0, 0))] + [pl.BlockSpec(memory_space=pl.ANY)] * n,
        out_specs=[blk, blk, blk] + [pl.BlockSpec(memory_space=pl.ANY)] * n,
        out_shape=[jax.ShapeDtypeStruct((T, ATT_WIDTH), F32)] * 3
        + [jax.ShapeDtypeStruct((4,) + s.shape, s.dtype) for s in shards],
        scratch_shapes=_chip_sems(n),
        compiler_params=_params(("arbitrary", "arbitrary", "arbitrary")),
    )(fstart, qn, kn, vb, frow, *shards)


def _conv_taps(lx, prev8, cw, cb):
    xs = jnp.concatenate([prev8, lx], axis=0)
    shifted = [lx] + [pltpu.roll(xs, k, 0)[8:] for k in range(1, CONV_WIDTH)]
    xc = cb + cw[CONV_WIDTH - 1:CONV_WIDTH] * lx
    for k in range(1, CONV_WIDTH):
        xc = xc + cw[CONV_WIDTH - 1 - k:CONV_WIDTH - k] * shifted[k]
    return xc, shifted


def _lru_gates(xc, wa, ba, wx, bx, lam):
    xb = xc.astype(MXU_DTYPE)
    r = _sigmoid(jnp.dot(xb, wa, preferred_element_type=F32) + ba)
    ig = _sigmoid(jnp.dot(xb, wx, preferred_element_type=F32) + bx)
    sp = jnp.maximum(-lam, 0.0) + jnp.log(1.0 + jnp.exp(-jnp.abs(lam)))
    log_a = -LRU_C * r * sp
    a = jnp.exp(log_a)
    th = jnp.tanh(log_a)
    mult = jnp.sqrt(-2.0 * th / (1.0 - th))
    return r, ig, sp, a, mult


def _gelu_parts(x):
    c0 = math.sqrt(2.0 / math.pi)
    t = jnp.tanh(c0 * (x + 0.044715 * x * x * x))
    g = 0.5 * x * (1.0 + t)
    dg = 0.5 * (1.0 + t) + 0.5 * x * (1.0 - t * t) * c0 * (1.0 + 3.0 * 0.044715 * x * x)
    return g, dg


def _lru_fwd(lx, lg, cw, cb, wa, ba, wx, bx, lam, bl, seq):
    tc = min(LRU_TILE, seq)
    nc = seq // tc
    T = bl * seq

    def body(lx_ref, lxp_ref, lg_ref, cw_ref, cb_ref, wa_ref, ba_ref, wx_ref, bx_ref, lam_ref,
             h_ref, rec_ref, hc_ref):
        i = pl.program_id(1)

        @pl.when(i == 0)
        def _():
            hc_ref[...] = jnp.zeros_like(hc_ref)

        lxv = lx_ref[...]
        prev8 = jnp.where(i > 0, lxp_ref[...], 0.0)
        xc, _ = _conv_taps(lxv, prev8, cw_ref[...], cb_ref[...])
        _, ig, _, a, mult = _lru_gates(xc, wa_ref[...], ba_ref[...], wx_ref[...], bx_ref[...], lam_ref[...])
        u = mult * (ig * xc)
        row = lax.broadcasted_iota(jnp.int32, (tc, LRU_WIDTH), 0)
        A, B = a, u
        k = 1
        while k < tc:
            a_s = jnp.where(row >= k, pltpu.roll(A, k, 0), 1.0)
            b_s = jnp.where(row >= k, pltpu.roll(B, k, 0), 0.0)
            B = A * b_s + B
            A = A * a_s
            k *= 2
        h = A * hc_ref[0:1, :] + B
        hc_ref[0:1, :] = h[tc - 1:tc, :]
        h_ref[...] = h
        g, _ = _gelu_parts(lg_ref[...])
        rec_ref[...] = h * g

    tile = pl.BlockSpec((tc, LRU_WIDTH), lambda b, i: (b * nc + i, 0))
    prev = pl.BlockSpec((8, LRU_WIDTH), lambda b, i: (jnp.maximum((b * seq + i * tc) // 8 - 1, 0), 0))
    vec = _const((1, LRU_WIDTH))
    mat = _const((LRU_WIDTH, LRU_WIDTH))
    return pl.pallas_call(
        body, name="lru_fwd", grid=(bl, nc),
        in_specs=[tile, prev, tile, _const((CONV_WIDTH, LRU_WIDTH)), vec, mat, vec, mat, vec, vec],
        out_specs=[tile, tile],
        out_shape=[jax.ShapeDtypeStruct((T, LRU_WIDTH), F32), jax.ShapeDtypeStruct((T, LRU_WIDTH), F32)],
        scratch_shapes=[pltpu.VMEM((8, LRU_WIDTH), F32)],
        compiler_params=_params(("arbitrary", "arbitrary")),
    )(lx, lx, lg, cw, cb, wa, ba, wx, bx, lam)


def _outproj(x2d, att, rec, ga, gr, wout):
    T = x2d.shape[0]
    tm = TOKEN_TILE

    def body(x_ref, a_ref, r_ref, ga_ref, gr_ref, w_ref, o_ref):
        a = a_ref[...]
        rc = r_ref[...]
        na = a * lax.rsqrt(jnp.mean(a * a, axis=-1, keepdims=True) + NORM_EPS) * ga_ref[...]
        nr = rc * lax.rsqrt(jnp.mean(rc * rc, axis=-1, keepdims=True) + NORM_EPS) * gr_ref[...]
        o_ref[...] = (x_ref[...]
                      + jnp.dot(na.astype(MXU_DTYPE), w_ref[:ATT_WIDTH, :], preferred_element_type=F32)
                      + jnp.dot(nr.astype(MXU_DTYPE), w_ref[ATT_WIDTH:, :], preferred_element_type=F32))

    row = lambda w: pl.BlockSpec((tm, w), lambda i: (i, 0))
    return pl.pallas_call(
        body, name="outproj", grid=(T // tm,),
        in_specs=[row(D_MODEL), row(ATT_WIDTH), row(LRU_WIDTH), _const((1, ATT_WIDTH)), _const((1, LRU_WIDTH)),
                  _const((D_MODEL, D_MODEL))],
        out_specs=row(D_MODEL),
        out_shape=jax.ShapeDtypeStruct((T, D_MODEL), F32),
        compiler_params=_params(("parallel",)),
    )(x2d, att, rec, ga, gr, wout)


def _mlp_fwd(x2, g2, wg, wu, wd, target):
    T = x2.shape[0]
    tm = TOKEN_TILE
    dff = wg.shape[1]

    def body(x_ref, g_ref, wg_ref, wu_ref, wd_ref, t_ref, gt_ref, up_ref, dy_ref, loss_ref):
        @pl.when(pl.program_id(0) == 0)
        def _():
            loss_ref[...] = jnp.zeros_like(loss_ref)

        x = x_ref[...]
        r = lax.rsqrt(jnp.mean(x * x, axis=-1, keepdims=True) + NORM_EPS)
        h = (x * r * g_ref[...]).astype(MXU_DTYPE)
        gt = jnp.dot(h, wg_ref[...], preferred_element_type=F32)
        up = jnp.dot(h, wu_ref[...], preferred_element_type=F32)
        gt_ref[...] = gt
        up_ref[...] = up
        act = (gt * _sigmoid(gt) * up).astype(MXU_DTYPE)
        y = x + jnp.dot(act, wd_ref[...], preferred_element_type=F32)
        e = y - t_ref[...]
        dy_ref[...] = e * (1.0 / D_MODEL)
        loss_ref[...] += jnp.sum(e * e)

    row = lambda w: pl.BlockSpec((tm, w), lambda i: (i, 0))
    return pl.pallas_call(
        body, name="mlp_fwd", grid=(T // tm,),
        in_specs=[row(D_MODEL), _const((1, D_MODEL)), _const((D_MODEL, dff)), _const((D_MODEL, dff)),
                  _const((dff, D_MODEL)), row(D_MODEL)],
        out_specs=[row(dff), row(dff), row(D_MODEL), _const((8, 128))],
        out_shape=[jax.ShapeDtypeStruct((T, dff), F32), jax.ShapeDtypeStruct((T, dff), F32),
                   jax.ShapeDtypeStruct((T, D_MODEL), F32), jax.ShapeDtypeStruct((8, 128), F32)],
        compiler_params=_params(("arbitrary",), VMEM_LARGE),
    )(x2, g2, wg, wu, wd, target)


def _nt(a, b):
    return lax.dot_general(a, b, (((1,), (1,)), ((), ())), preferred_element_type=F32)


def _tn(a, b):
    return lax.dot_general(a, b, (((0,), (0,)), ((), ())), preferred_element_type=F32)


def _mlp_bwd(dy, x2, gt, up, g2, wg, wu, wd):
    T = x2.shape[0]
    tm = TOKEN_TILE
    dff = wg.shape[1]

    def body(dy_ref, x_ref, gt_ref, up_ref, g_ref, wg_ref, wu_ref, wd_ref,
             dx_ref, dxb_ref, dgt_ref, dup_ref, act_ref, h_ref, dyb_ref, dg_ref):
        @pl.when(pl.program_id(0) == 0)
        def _():
            dg_ref[...] = jnp.zeros_like(dg_ref)

        dy_v = dy_ref[...]
        dyb = dy_v.astype(MXU_DTYPE)
        dyb_ref[...] = dyb
        x = x_ref[...]
        r = lax.rsqrt(jnp.mean(x * x, axis=-1, keepdims=True) + NORM_EPS)
        xh = x * r
        h_ref[...] = (xh * g_ref[...]).astype(MXU_DTYPE)
        gt_v = gt_ref[...]
        up_v = up_ref[...]
        sg = _sigmoid(gt_v)
        silu = gt_v * sg
        act_ref[...] = (silu * up_v).astype(MXU_DTYPE)
        dact = _nt(dyb, wd_ref[...])
        dup = (dact * silu).astype(MXU_DTYPE)
        dgt = (dact * up_v * (sg * (1.0 + gt_v * (1.0 - sg)))).astype(MXU_DTYPE)
        dup_ref[...] = dup
        dgt_ref[...] = dgt
        dh = _nt(dgt, wg_ref[...]) + _nt(dup, wu_ref[...])
        dg_ref[...] += jnp.sum(dh * xh, axis=0, keepdims=True)
        dxh = dh * g_ref[...]
        dx = dy_v + r * (dxh - xh * jnp.mean(dxh * xh, axis=-1, keepdims=True))
        dx_ref[...] = dx
        dxb_ref[...] = dx.astype(MXU_DTYPE)

    row = lambda w: pl.BlockSpec((tm, w), lambda i: (i, 0))
    return pl.pallas_call(
        body, name="mlp_bwd", grid=(T // tm,),
        in_specs=[row(D_MODEL), row(D_MODEL), row(dff), row(dff), _const((1, D_MODEL)),
                  _const((D_MODEL, dff)), _const((D_MODEL, dff)), _const((dff, D_MODEL))],
        out_specs=[row(D_MODEL), row(D_MODEL), row(dff), row(dff), row(dff), row(D_MODEL), row(D_MODEL),
                   _const((1, D_MODEL))],
        out_shape=[jax.ShapeDtypeStruct((T, D_MODEL), F32), jax.ShapeDtypeStruct((T, D_MODEL), MXU_DTYPE),
                   jax.ShapeDtypeStruct((T, dff), MXU_DTYPE), jax.ShapeDtypeStruct((T, dff), MXU_DTYPE),
                   jax.ShapeDtypeStruct((T, dff), MXU_DTYPE), jax.ShapeDtypeStruct((T, D_MODEL), MXU_DTYPE),
                   jax.ShapeDtypeStruct((T, D_MODEL), MXU_DTYPE), jax.ShapeDtypeStruct((1, D_MODEL), F32)],
        compiler_params=_params(("arbitrary",), VMEM_LARGE),
    )(dy, x2, gt, up, g2, wg, wu, wd)


def _matmul_tn(a, b, tn, name):
    T, K = a.shape
    N = b.shape[1]
    tt = min(512, T)

    def body(a_ref, b_ref, o_ref):
        @pl.when(pl.program_id(1) == 0)
        def _():
            o_ref[...] = jnp.zeros_like(o_ref)

        o_ref[...] += _tn(a_ref[...], b_ref[...])

    return pl.pallas_call(
        body, name=name, grid=(N // tn, T // tt),
        in_specs=[pl.BlockSpec((tt, K), lambda n, t: (t, 0)), pl.BlockSpec((tt, tn), lambda n, t: (t, n))],
        out_specs=pl.BlockSpec((K, tn), lambda n, t: (0, n)),
        out_shape=jax.ShapeDtypeStruct((K, N), F32),
        compiler_params=_params(("parallel", "arbitrary"), VMEM_LARGE),
    )(a, b)


def _outproj_bwd(dx2b, att, att_x, rec, ga, gr, wout):
    T = att.shape[0]
    tm = TOKEN_TILE

    def body(dx_ref, a_ref, ax_ref, r_ref, ga_ref, gr_ref, w_ref, datt_ref, delta_ref, drec_ref, mix_ref, dga_ref, dgr_ref):
        @pl.when(pl.program_id(0) == 0)
        def _():
            dga_ref[...] = jnp.zeros_like(dga_ref)
            dgr_ref[...] = jnp.zeros_like(dgr_ref)

        dmix = _nt(dx_ref[...], w_ref[...])

        def norm_bwd(v, g, dn):
            rr = lax.rsqrt(jnp.mean(v * v, axis=-1, keepdims=True) + NORM_EPS)
            vh = v * rr
            dvh = dn * g
            dv = rr * (dvh - vh * jnp.mean(dvh * vh, axis=-1, keepdims=True))
            return vh, dv, jnp.sum(dn * vh, axis=0, keepdims=True)

        a = a_ref[...]
        ah, datt, dga = norm_bwd(a, ga_ref[...], dmix[:, :ATT_WIDTH])
        rh, drec, dgr = norm_bwd(r_ref[...], gr_ref[...], dmix[:, ATT_WIDTH:])
        dga_ref[...] += dga
        dgr_ref[...] += dgr
        mix_ref[:, :ATT_WIDTH] = (ah * ga_ref[...]).astype(MXU_DTYPE)
        mix_ref[:, ATT_WIDTH:] = (rh * gr_ref[...]).astype(MXU_DTYPE)
        dattb = datt.astype(MXU_DTYPE)
        datt_ref[...] = dattb
        drec_ref[...] = drec
        lo = _lo_mask()
        prod = dattb.astype(F32) * ax_ref[...]
        for p in range(N_PAIR):
            delta_ref[:, PAIR * p:PAIR * (p + 1)] = _half_sums(prod[:, PAIR * p:PAIR * (p + 1)], lo)

    row = lambda w: pl.BlockSpec((tm, w), lambda i: (i, 0))
    return pl.pallas_call(
        body, name="outproj_bwd", grid=(T // tm,),
        in_specs=[row(D_MODEL), row(ATT_WIDTH), row(ATT_WIDTH), row(LRU_WIDTH), _const((1, ATT_WIDTH)), _const((1, LRU_WIDTH)),
                  _const((D_MODEL, D_MODEL))],
        out_specs=[row(ATT_WIDTH), row(ATT_WIDTH), row(LRU_WIDTH), row(D_MODEL),
                   _const((1, ATT_WIDTH)), _const((1, LRU_WIDTH))],
        out_shape=[jax.ShapeDtypeStruct((T, ATT_WIDTH), MXU_DTYPE), jax.ShapeDtypeStruct((T, ATT_WIDTH), F32),
                   jax.ShapeDtypeStruct((T, LRU_WIDTH), F32), jax.ShapeDtypeStruct((T, D_MODEL), MXU_DTYPE),
                   jax.ShapeDtypeStruct((1, ATT_WIDTH), F32), jax.ShapeDtypeStruct((1, LRU_WIDTH), F32)],
        compiler_params=_params(("arbitrary",)),
    )(dx2b, att, att_x, rec, ga, gr, wout)


def _lru_bwd(drec, lg, h, lx, cw, cb, wa, ba, wx, bx, lam, bl, seq):
    tc = min(LRU_TILE, seq)
    nc = seq // tc
    T = bl * seq
    n = tc

    def body(dr_ref, lg_ref, h_ref, hp_ref, lx_ref, lxp_ref, cw_ref, cb_ref, wa_ref, ba_ref, wx_ref, bx_ref, lam_ref,
             dlx_ref, dlg_ref, dwa_ref, dwx_ref, small_ref, gc_ref, dxn_ref):
        b, i = pl.program_id(0), pl.program_id(1)
        ir = nc - 1 - i

        @pl.when((b == 0) & (i == 0))
        def _():
            dwa_ref[...] = jnp.zeros_like(dwa_ref)
            dwx_ref[...] = jnp.zeros_like(dwx_ref)
            small_ref[...] = jnp.zeros_like(small_ref)

        @pl.when(i == 0)
        def _():
            gc_ref[...] = jnp.zeros_like(gc_ref)
            dxn_ref[...] = jnp.zeros_like(dxn_ref)

        cw = cw_ref[...]
        lam_v = lam_ref[...]
        lxv = lx_ref[...]
        prev8 = jnp.where(ir > 0, lxp_ref[...], 0.0)
        xc, shifted = _conv_taps(lxv, prev8, cw, cb_ref[...])
        r, ig, sp, a, mult = _lru_gates(xc, wa_ref[...], ba_ref[...], wx_ref[...], bx_ref[...], lam_v)
        hv = h_ref[...]
        drv = dr_ref[...]
        g, dg = _gelu_parts(lg_ref[...])
        dlg_ref[...] = drv * hv * dg
        dh = drv * g

        row = lax.broadcasted_iota(jnp.int32, (n, LRU_WIDTH), 0)
        A = jnp.where(row < n - 1, pltpu.roll(a, n - 1, 0), 0.0)
        B = dh + jnp.where(row == n - 1, gc_ref[0:1, :], 0.0)
        k = 1
        while k < n:
            a_s = jnp.where(row < n - k, pltpu.roll(A, n - k, 0), 1.0)
            b_s = jnp.where(row < n - k, pltpu.roll(B, n - k, 0), 0.0)
            B = B + A * b_s
            A = A * a_s
            k *= 2
        gs = B
        gc_ref[0:1, :] = a[0:1, :] * gs[0:1, :]

        hprev8 = jnp.where(ir > 0, hp_ref[...], 0.0)
        h_prev = pltpu.roll(jnp.concatenate([hprev8, hv], axis=0), 1, 0)[8:]
        da = gs * h_prev
        ix = ig * xc
        dmult = gs * ix
        dig = gs * mult * xc
        dxc = gs * mult * ig
        dlog_a = da * a - dmult * (a * a) / mult
        dr_gate = dlog_a * (-LRU_C * sp)
        dza = dr_gate * r * (1.0 - r)
        dzx = dig * ig * (1.0 - ig)
        dzab = dza.astype(MXU_DTYPE)
        dzxb = dzx.astype(MXU_DTYPE)
        xcb = xc.astype(MXU_DTYPE)
        dwa_ref[...] += _tn(xcb, dzab)
        dwx_ref[...] += _tn(xcb, dzxb)
        dxc = dxc + _nt(dzab, wa_ref[...]) + _nt(dzxb, wx_ref[...])

        ds = jnp.concatenate([dxc, dxn_ref[...]], axis=0)
        dlx = cw[CONV_WIDTH - 1:CONV_WIDTH] * dxc
        for k in range(1, CONV_WIDTH):
            dlx = dlx + cw[CONV_WIDTH - 1 - k:CONV_WIDTH - k] * pltpu.roll(ds, n + 8 - k, 0)[:n]
        dlx_ref[...] = dlx
        dxn_ref[...] = dxc[0:8, :]

        colsum = lambda v: jnp.sum(v, axis=0, keepdims=True)
        small_ref[0:1, :] += colsum(dza)
        small_ref[1:2, :] += colsum(dzx)
        small_ref[2:3, :] += colsum(dlog_a * r) * (LRU_C * _sigmoid(-lam_v))
        small_ref[3:4, :] += colsum(dxc)
        for k in range(CONV_WIDTH):
            j = CONV_WIDTH - 1 - k
            small_ref[4 + j:5 + j, :] += colsum(dxc * shifted[k])

    tile = pl.BlockSpec((tc, LRU_WIDTH), lambda b, i: (b * nc + (nc - 1 - i), 0))
    prev = pl.BlockSpec((8, LRU_WIDTH), lambda b, i: (jnp.maximum((b * seq + (nc - 1 - i) * tc) // 8 - 1, 0), 0))
    vec = _const((1, LRU_WIDTH))
    mat = _const((LRU_WIDTH, LRU_WIDTH))
    return pl.pallas_call(
        body, name="lru_bwd", grid=(bl, nc),
        in_specs=[tile, tile, tile, prev, tile, prev, _const((CONV_WIDTH, LRU_WIDTH)), vec, mat, vec, mat, vec, vec],
        out_specs=[tile, tile, mat, mat, _const((8, LRU_WIDTH))],
        out_shape=[jax.ShapeDtypeStruct((T, LRU_WIDTH), F32), jax.ShapeDtypeStruct((T, LRU_WIDTH), F32),
                   jax.ShapeDtypeStruct((LRU_WIDTH, LRU_WIDTH), F32), jax.ShapeDtypeStruct((LRU_WIDTH, LRU_WIDTH), F32),
                   jax.ShapeDtypeStruct((8, LRU_WIDTH), F32)],
        scratch_shapes=[pltpu.VMEM((8, LRU_WIDTH), F32), pltpu.VMEM((8, LRU_WIDTH), F32)],
        compiler_params=_params(("arbitrary", "arbitrary")),
    )(drec, lg, h, h, lx, lx, cw, cb, wa, ba, wx, bx, lam)


def _attn_bwd(qn, kn, vb, dob, lse, delta, frow, fstart, bl, seq, slabs):
    tq = min(ATT_TILE, seq)
    nq = seq // tq
    T = bl * seq
    n = len(slabs)

    def body(fs_ref, q_ref, k_ref, v_ref, do_ref, lse_ref, dl_ref, fr_ref, *rest):
        s_in, (dq_ref, dk_ref, dv_ref, df_ref), s_out, sems = rest[:n], rest[n:n + 4], rest[n + 4:2 * n + 4], rest[2 * n + 4:]
        b, p, j = pl.program_id(0), pl.program_id(1), pl.program_id(2)
        copies = _chip_copies(s_in, s_out, *sems, scatter=True)

        @pl.when((b == 0) & (p == 0) & (j == 0))
        def _():
            for cp in copies:
                cp.start()

        @pl.when(j == 0)
        def _():
            dq_ref[...] = jnp.zeros_like(dq_ref)

        lane = lax.broadcasted_iota(jnp.int32, (1, PAIR), 1)
        rows = lax.broadcasted_iota(jnp.int32, (tq, tq), 0)
        cols = lax.broadcasted_iota(jnp.int32, (tq, tq), 1)
        causal = cols <= rows
        kv = k_ref[...]
        vv = v_ref[...]
        hms = [(lane >= HEAD_DIM * hh) & (lane < HEAD_DIM * (hh + 1)) for hh in range(2)]
        khs = [jnp.where(hm, kv, jnp.zeros_like(kv)) for hm in hms]
        fks = [fr_ref[0, 0, hh:hh + 1, :] for hh in range(2)]
        bases = [((b * N_PAIR + p) * 2 + hh) * nq for hh in range(2)]

        def block(i, carry, masked):
            dk, dv, dfs = carry
            start = pl.multiple_of(i * tq, tq)
            qi = q_ref[pl.ds(start, tq), :]
            doi = do_ref[pl.ds(start, tq), :]
            dq = jnp.zeros((tq, PAIR), F32)
            new_dfs = []
            for hh in range(2):
                c0 = HEAD_DIM * hh
                qh = jnp.where(hms[hh], qi, jnp.zeros_like(qi))
                doh = jnp.where(hms[hh], doi, jnp.zeros_like(doi))
                s = _nt(qh, kv) - (fks[hh] - fs_ref[bases[hh] + i])
                if masked:
                    s = jnp.where(causal, s, NEG)
                pr = jnp.exp(s - lse_ref[pl.ds(start, tq), c0:c0 + 1])
                dp = _nt(doh, vv)
                ds = pr * (dp - dl_ref[pl.ds(start, tq), c0:c0 + 1])
                dsb = ds.astype(MXU_DTYPE)
                dv = dv + _tn(pr.astype(MXU_DTYPE), doh)
                dk = dk + _tn(dsb, qh)
                dq = dq + jnp.dot(dsb, khs[hh], preferred_element_type=F32)
                new_dfs.append(dfs[hh] - jnp.sum(ds, axis=0, keepdims=True))
            dq_ref[pl.ds(start, tq), :] += dq
            return dk, dv, tuple(new_dfs)

        zero = jnp.zeros((tq, PAIR), F32)
        carry = block(j, (zero, zero, (jnp.zeros((1, tq), F32), jnp.zeros((1, tq), F32))), True)
        dk, dv, dfs = lax.fori_loop(j + 1, nq, functools.partial(block, masked=False), carry)
        for hh in range(2):
            df_ref[0, 0, hh:hh + 1, :] = dfs[hh]
        dk_ref[...] = dk
        dv_ref[...] = dv

        @pl.when((b == bl - 1) & (p == N_PAIR - 1) & (j == nq - 1))
        def _():
            for cp in copies:
                cp.wait()

    blk = pl.BlockSpec((tq, PAIR), lambda b, p, j: (b * nq + j, p))
    full = pl.BlockSpec((seq, PAIR), lambda b, p, j: (b, p))
    fblk = pl.BlockSpec((1, 1, 2, tq), lambda b, p, j: (b, p, 0, j))
    hbm = pl.BlockSpec(memory_space=pl.ANY)
    return pl.pallas_call(
        body, name="attn_bwd", grid=(bl, N_PAIR, nq),
        in_specs=[pl.BlockSpec(memory_space=pltpu.SMEM), full, blk, blk, full, full, full, fblk] + [hbm] * n,
        out_specs=[full, blk, blk, fblk] + [hbm] * n,
        out_shape=[jax.ShapeDtypeStruct((T, ATT_WIDTH), F32), jax.ShapeDtypeStruct((T, ATT_WIDTH), F32),
                   jax.ShapeDtypeStruct((T, ATT_WIDTH), F32), jax.ShapeDtypeStruct((bl, N_PAIR, 2, seq), F32)]
        + [jax.ShapeDtypeStruct(s.shape, s.dtype) for s in slabs],
        scratch_shapes=_chip_sems(n),
        compiler_params=_params(("arbitrary", "arbitrary", "arbitrary"), VMEM_LARGE),
    )(fstart, qn, kn, vb, dob, lse, delta, frow, *slabs)


def _forget_bwd(dfcol, f2d, bf, bl, seq):
    def body(d_ref, z_ref, b_ref, o_ref, db_ref):
        @pl.when(pl.program_id(0) == 0)
        def _():
            db_ref[...] = jnp.zeros_like(db_ref)

        d = d_ref[...]
        row = lax.broadcasted_iota(jnp.int32, (seq, F_PAD), 0)
        k = 1
        while k < seq:
            d = d + jnp.where(row < seq - k, pltpu.roll(d, seq - k, 0), 0.0)
            k *= 2
        dz = d * _sigmoid(-(z_ref[...] + b_ref[...]))
        o_ref[...] = dz
        db_ref[...] += jnp.sum(dz, axis=0, keepdims=True)

    blk = pl.BlockSpec((seq, F_PAD), lambda b: (b, 0))
    return pl.pallas_call(
        body, name="forget_bwd", grid=(bl,),
        in_specs=[blk, blk, _const((1, F_PAD))],
        out_specs=[blk, _const((1, F_PAD))],
        out_shape=[jax.ShapeDtypeStruct(f2d.shape, F32), jax.ShapeDtypeStruct((1, F_PAD), F32)],
        compiler_params=_params(("arbitrary",)),
    )(dfcol, f2d, bf)


def _inproj_bwd(dq, dk, dv, qkv, df, dlx, dlg, x2d, dx2, g1, gq2, gk2, wcat):
    T = x2d.shape[0]
    tm = TOKEN_TILE

    def body(dq_ref, dk_ref, dv_ref, qkv_ref, df_ref, dlx_ref, dlg_ref, x_ref, dx2_ref, g1_ref, gq_ref, gk_ref, w_ref,
             gx_ref, dp_ref, h_ref, dg1_ref, dgq_ref, dgk_ref):
        @pl.when(pl.program_id(0) == 0)
        def _():
            dg1_ref[...] = jnp.zeros_like(dg1_ref)
            dgq_ref[...] = jnp.zeros_like(dgq_ref)
            dgk_ref[...] = jnp.zeros_like(dgk_ref)

        lo = _lo_mask()

        def head_norm_bwd(t, g2, dy):
            rr = lax.rsqrt(_half_sums(t * t, lo) * (1.0 / HEAD_DIM) + NORM_EPS)
            th = t * rr
            dth = dy * g2
            mm = _half_sums(dth * th, lo) * (1.0 / HEAD_DIM)
            return rr * (dth - th * mm), jnp.sum(dy * th, axis=0, keepdims=True)

        dgq = jnp.zeros((1, PAIR), F32)
        dgk = jnp.zeros((1, PAIR), F32)
        for p in range(N_PAIR):
            cq = slice(PAIR * p, PAIR * (p + 1))
            ck = slice(ATT_WIDTH + PAIR * p, ATT_WIDTH + PAIR * (p + 1))
            dqp, g_ = head_norm_bwd(qkv_ref[:, cq], gq_ref[...], dq_ref[:, cq] * QK_SCALE)
            dgq = dgq + g_
            dp_ref[:, cq] = dqp.astype(MXU_DTYPE)
            dkp, g_ = head_norm_bwd(qkv_ref[:, ck], gk_ref[...], dk_ref[:, cq])
            dgk = dgk + g_
            dp_ref[:, ck] = dkp.astype(MXU_DTYPE)
        dgq_ref[...] += dgq
        dgk_ref[...] += dgk
        f0 = 3 * ATT_WIDTH
        dp_ref[:, 2 * ATT_WIDTH:f0] = dv_ref[...].astype(MXU_DTYPE)
        dp_ref[:, f0:f0 + F_PAD] = df_ref[...].astype(MXU_DTYPE)
        dp_ref[:, f0 + F_PAD:f0 + F_PAD + LRU_WIDTH] = dlx_ref[...].astype(MXU_DTYPE)
        dp_ref[:, f0 + F_PAD + LRU_WIDTH:] = dlg_ref[...].astype(MXU_DTYPE)
        dh = _nt(dp_ref[...], w_ref[...])
        x = x_ref[...]
        r = lax.rsqrt(jnp.mean(x * x, axis=-1, keepdims=True) + NORM_EPS)
        xh = x * r
        h_ref[...] = (xh * g1_ref[...]).astype(MXU_DTYPE)
        dg1_ref[...] += jnp.sum(dh * xh, axis=0, keepdims=True)
        dxh = dh * g1_ref[...]
        gx_ref[...] = dx2_ref[...] + r * (dxh - xh * jnp.mean(dxh * xh, axis=-1, keepdims=True))

    row = lambda w: pl.BlockSpec((tm, w), lambda i: (i, 0))
    return pl.pallas_call(
        body, name="inproj_bwd", grid=(T // tm,),
        in_specs=[row(ATT_WIDTH), row(ATT_WIDTH), row(ATT_WIDTH), row(3 * ATT_WIDTH), row(F_PAD), row(LRU_WIDTH),
                  row(LRU_WIDTH), row(D_MODEL), row(D_MODEL), _const((1, D_MODEL)), _const((1, PAIR)), _const((1, PAIR)),
                  _const((D_MODEL, N_CAT))],
        out_specs=[row(D_MODEL), row(N_CAT), row(D_MODEL), _const((1, D_MODEL)), _const((1, PAIR)), _const((1, PAIR))],
        out_shape=[jax.ShapeDtypeStruct((T, D_MODEL), F32), jax.ShapeDtypeStruct((T, N_CAT), MXU_DTYPE),
                   jax.ShapeDtypeStruct((T, D_MODEL), MXU_DTYPE), jax.ShapeDtypeStruct((1, D_MODEL), F32),
                   jax.ShapeDtypeStruct((1, PAIR), F32), jax.ShapeDtypeStruct((1, PAIR), F32)],
        compiler_params=_params(("arbitrary",), VMEM_LARGE),
    )(dq, dk, dv, qkv, df, dlx, dlg, x2d, dx2, g1, gq2, gk2, wcat)


def _row_block(rows):
    for rb in (256, 176, 128, 64, 32, 16, 8):
        if rows % rb == 0:
            return rb
    return rows


def _sum_slabs(recv, name):
    _, rows, cols = recv.shape
    rb = _row_block(rows)

    def body(r_ref, o_ref):
        part = [r_ref[s].astype(F32) for s in range(4)]
        o_ref[...] = ((part[0] + part[1]) + part[2]) + part[3]

    return pl.pallas_call(
        body, name=name, grid=(rows // rb,),
        in_specs=[pl.BlockSpec((4, rb, cols), lambda i: (0, i, 0))],
        out_specs=pl.BlockSpec((rb, cols), lambda i: (i, 0)),
        out_shape=jax.ShapeDtypeStruct((rows, cols), F32),
        compiler_params=_params(("parallel",)),
    )(recv)


def _adamw_math(w, g, m, v):
    m = ADAM_B1 * m + (1.0 - ADAM_B1) * g
    v = ADAM_B2 * v + (1.0 - ADAM_B2) * (g * g)
    m_hat = m / (1.0 - ADAM_B1 ** ADAM_STEP)
    v_hat = v / (1.0 - ADAM_B2 ** ADAM_STEP)
    delta = -ADAM_LR * (m_hat / (jnp.sqrt(v_hat) + ADAM_EPS) + ADAM_WD * w)
    return delta, m, v


def _adamw_pair(mine, theirs, w, m, v, name):
    _, rows, cols = w.shape
    rb = _row_block(rows)

    def body(a_ref, b_ref, w_ref, m_ref, v_ref, g_ref, d_ref, nm_ref, nv_ref):
        g = a_ref[...] + b_ref[...]
        g_ref[0] = g
        d_ref[0], nm_ref[0], nv_ref[0] = _adamw_math(w_ref[0], g, m_ref[0], v_ref[0])

    blk = pl.BlockSpec((rb, cols), lambda i: (i, 0))
    blk3 = pl.BlockSpec((1, rb, cols), lambda i: (0, i, 0))
    return pl.pallas_call(
        body, name=name, grid=(rows // rb,),
        in_specs=[blk, blk, blk3, blk3, blk3], out_specs=[blk3] * 4,
        out_shape=[jax.ShapeDtypeStruct((1, rows, cols), F32)] * 4,
        compiler_params=_params(("parallel",)),
    )(mine, theirs, w, m, v)


def _sum_packs(recv):
    _, rows, cols = recv.shape

    def body(r_ref, o_ref):
        acc = r_ref[0]
        for d in range(1, 8):
            acc = acc + r_ref[d]
        o_ref[...] = acc

    return pl.pallas_call(
        body, name="sum_packs", grid=(1,),
        in_specs=[_const(recv.shape)], out_specs=_const((rows, cols)),
        out_shape=jax.ShapeDtypeStruct((rows, cols), F32),
        compiler_params=_params(("arbitrary",)),
    )(recv)


def _adamw_small(w, g, m, v):
    rows, cols = w.shape

    def body(w_ref, g_ref, m_ref, v_ref, d_ref, nm_ref, nv_ref):
        d_ref[...], nm_ref[...], nv_ref[...] = _adamw_math(w_ref[...], g_ref[...], m_ref[...], v_ref[...])

    blk = _const((rows, cols))
    return pl.pallas_call(
        body, name="adamw_small", grid=(1,),
        in_specs=[blk] * 4, out_specs=[blk] * 3,
        out_shape=[jax.ShapeDtypeStruct((rows, cols), F32)] * 3,
        compiler_params=_params(("arbitrary",)),
    )(w, g, m, v)


SMALL = ["norm1_g", "q_norm_g", "k_norm_g", "b_f", "conv_b", "w_a", "b_a", "w_x", "b_x", "lam",
         "attn_out_g", "lru_out_g", "norm2_g"]


def _pack(parts, rows=None):
    flat = jnp.concatenate([p.reshape(-1) for p in parts])
    n = flat.shape[0]
    total = -(-n // 1024) * 1024 if rows is None else rows * 128
    return jnp.pad(flat, (0, total - n)).reshape(-1, 128)


def _unpack(pack, shapes):
    flat = pack.reshape(-1)
    out, off = [], 0
    for s in shapes:
        n = math.prod(s)
        out.append(flat[off:off + n].reshape(s))
        off += n
    return out


def _cat_shards(g, pad_at=None, pad=0):
    _, rows, w = g.shape
    pieces = []
    for s in range(4):
        lo, hi = s * w, (s + 1) * w
        if pad_at is not None and lo < pad_at <= hi:
            pieces += [g[s][:, :pad_at - lo], jnp.zeros((rows, pad), g.dtype)]
            if pad_at < hi:
                pieces.append(g[s][:, pad_at - lo:])
        else:
            pieces.append(g[s])
    return jnp.concatenate(pieces, axis=1)


def _split_shards(full, pad_at=None, pad=0):
    w = (full.shape[1] - pad) // 4
    slabs = []
    for s in range(4):
        lo, hi = s * w, (s + 1) * w
        if pad_at is not None and lo < pad_at < hi:
            slabs.append(jnp.concatenate([full[:, lo:pad_at], full[:, pad_at + pad:hi + pad]], axis=1))
        elif pad_at is not None and lo >= pad_at:
            slabs.append(full[:, lo + pad:hi + pad])
        else:
            slabs.append(full[:, lo:hi])
    return jnp.stack(slabs)


def _block_diag(w):
    eye = jnp.eye(LRU_BLOCKS, dtype=w.dtype)
    return (w[:, :, None, :] * eye[:, None, :, None]).reshape(LRU_WIDTH, LRU_WIDTH)


def _diag_blocks(m):
    m4 = m.reshape(LRU_BLOCKS, LRU_WIDTH // LRU_BLOCKS, LRU_BLOCKS, LRU_WIDTH // LRU_BLOCKS)
    return jnp.stack([m4[n, :, n, :] for n in range(LRU_BLOCKS)])


def kernel(x, norm1_g, w_in, q_norm_g, k_norm_g, b_f, conv_w, conv_b, w_a, b_a, w_x, b_x, lam, attn_out_g, lru_out_g, w_out, norm2_g, w_gate, w_up, w_down, loss_target, m_norm1_g, m_w_in, m_q_norm_g, m_k_norm_g, m_b_f, m_conv_w, m_conv_b, m_w_a, m_b_a, m_w_x, m_b_x, m_lam, m_attn_out_g, m_lru_out_g, m_w_out, m_norm2_g, m_w_gate, m_w_up, m_w_down, v_norm1_g, v_w_in, v_q_norm_g, v_k_norm_g, v_b_f, v_conv_w, v_conv_b, v_w_a, v_b_a, v_w_x, v_b_x, v_lam, v_attn_out_g, v_lru_out_g, v_w_out, v_norm2_g, v_w_gate, v_w_up, v_w_down):
    args = dict(locals())
    bl, seq, _ = x.shape
    T = bl * seq
    tq = min(ATT_TILE, seq)
    nq = seq // tq
    dff = w_gate.shape[2] * 4
    my_chip = 2 * lax.axis_index("x") + lax.axis_index("y")

    g_in, g_cw = _gather_over_chips([w_in[0].astype(MXU_DTYPE), conv_w[0]])
    later_shards = [w_out[0].astype(MXU_DTYPE), w_gate[0].astype(MXU_DTYPE), w_up[0].astype(MXU_DTYPE),
                    w_down[0].astype(MXU_DTYPE)]
    f0 = 3 * ATT_WIDTH
    wcat = _cat_shards(g_in, f0 + HEADS, F_PAD - HEADS)
    cw_full = _cat_shards(g_cw)
    wa_bd = _block_diag(w_a[0]).astype(MXU_DTYPE)
    wx_bd = _block_diag(w_x[0]).astype(MXU_DTYPE)
    gq2 = jnp.tile(q_norm_g, (1, 2))
    gk2 = jnp.tile(k_norm_g, (1, 2))
    bf_pad = jnp.pad(b_f, ((0, 0), (0, F_PAD - HEADS)))

    x2d = x.reshape(T, D_MODEL)
    target2d = loss_target.reshape(T, D_MODEL)

    qkv, qn, kn, vb, f2d, lx, lg = _inproj(x2d, norm1_g, wcat, gq2, gk2)
    fcol = _forget_cumsum(f2d, bf_pad, bl, seq)
    frow = jnp.transpose(fcol.reshape(bl, seq, F_PAD)[:, :, :HEADS], (0, 2, 1)).reshape(bl, N_PAIR, 2, seq)
    fstart = frow[:, :, :, ::tq].reshape(-1)
    att, att_x, lse, g_out, g_gate, g_up, g_down = _attn_fwd(qn, kn, vb, frow, fstart, bl, seq, later_shards)
    wout_full = g_out.reshape(D_MODEL, D_MODEL)
    wg_full, wu_full = _cat_shards(g_gate), _cat_shards(g_up)
    wd_full = g_down.reshape(dff, D_MODEL)
    h, rec = _lru_fwd(lx, lg, cw_full, conv_b, wa_bd, b_a, wx_bd, b_x, lam, bl, seq)
    x2 = _outproj(x2d, att, rec, attn_out_g, lru_out_g, wout_full)
    gt, up, dy, sq_err = _mlp_fwd(x2, norm2_g, wg_full, wu_full, wd_full, target2d)
    loss = lax.psum(sq_err[0, 0] * (0.5 / D_MODEL), ("x", "y", "c"))

    dx2, dx2b, dgtb, dupb, actb, h2b, dyb, dg2 = _mlp_bwd(dy, x2, gt, up, norm2_g, wg_full, wu_full, wd_full)
    dw_down = _matmul_tn(actb, dyb, 512, "dw_down")
    dw_gate = _matmul_tn(h2b, dgtb, dff // 2, "dw_gate")
    dw_up = _matmul_tn(h2b, dupb, dff // 2, "dw_up")
    dattb, delta, drec, mixb, dga, dgr = _outproj_bwd(dx2b, att, att_x, rec, attn_out_g, lru_out_g, wout_full)
    dw_out = _matmul_tn(mixb, dx2b, D_MODEL, "dw_out")
    dlx, dlg, dwa_bd, dwx_bd, lru_small = _lru_bwd(drec, lg, h, lx, cw_full, conv_b, wa_bd, b_a, wx_bd, b_x, lam, bl, seq)
    early_slabs = [dw_out.reshape(4, D_MODEL // 4, D_MODEL), _split_shards(dw_gate), _split_shards(dw_up),
                   dw_down.reshape(4, dff // 4, D_MODEL)]
    dq, dk, dv, dfrow, *recv_early = _attn_bwd(qn, kn, vb, dattb, lse, delta, frow, fstart, bl, seq, early_slabs)
    dfcol = jnp.pad(jnp.transpose(dfrow.reshape(bl, HEADS, seq), (0, 2, 1)), ((0, 0), (0, 0), (0, F_PAD - HEADS)))
    df, dbf = _forget_bwd(dfcol.reshape(T, F_PAD), f2d, bf_pad, bl, seq)
    grad_x, dprojb, h1b, dg1, dgq, dgk = _inproj_bwd(dq, dk, dv, qkv, df, dlx, dlg, x2d, dx2, norm1_g, gq2, gk2, wcat)
    dwcat = _matmul_tn(h1b, dprojb, N_CAT // 3, "dw_in")

    small_grads = {
        "norm1_g": dg1, "q_norm_g": dgq[:, :HEAD_DIM] + dgq[:, HEAD_DIM:], "k_norm_g": dgk[:, :HEAD_DIM] + dgk[:, HEAD_DIM:],
        "b_f": dbf[:, :HEADS], "conv_b": lru_small[3:4], "w_a": _diag_blocks(dwa_bd)[None], "b_a": lru_small[0:1],
        "w_x": _diag_blocks(dwx_bd)[None], "b_x": lru_small[1:2], "lam": lru_small[2:3],
        "attn_out_g": dga, "lru_out_g": dgr, "norm2_g": dg2}
    pack = _pack([small_grads[n] for n in SMALL] + [lru_small[4:8]])
    dw_in_slabs = _split_shards(dwcat.astype(jnp.bfloat16), f0 + HEADS, F_PAD - HEADS)
    recv_in, recv_pack = _exchange_grads([dw_in_slabs], pack)
    recv = [recv_in] + recv_early
    big = ["w_in", "w_out", "w_gate", "w_up", "w_down"]
    part = [_sum_slabs(r, "sum_" + n) for r, n in zip(recv, big)]
    theirs = _swap_with_sibling(part)
    out = {}
    for n, a, b_ in zip(big, part, theirs):
        out[n] = tuple(_adamw_pair(a, b_, args[n], args["m_" + n], args["v_" + n], "adamw_" + n))

    small_shapes = [args[n].shape for n in SMALL]
    red = _sum_packs(recv_pack)
    g_small = _unpack(red, small_shapes + [(CONV_WIDTH, LRU_WIDTH)])
    g_cw_mine = lax.dynamic_slice_in_dim(g_small[-1], my_chip * (LRU_WIDTH // 4), LRU_WIDTH // 4, axis=1)[None]
    g_list = g_small[:-1] + [g_cw_mine]
    names = SMALL + ["conv_w"]
    rows = pack.shape[0]
    d_p, m_p, v_p = _adamw_small(_pack([args[n] for n in names], rows), _pack(g_list, rows),
                                 _pack([args["m_" + n] for n in names], rows), _pack([args["v_" + n] for n in names], rows))
    shapes = [args[n].shape for n in names]
    for n, g, d, nm, nv in zip(names, g_list, _unpack(d_p, shapes), _unpack(m_p, shapes), _unpack(v_p, shapes)):
        out[n] = (g, d, nm, nv)

    order = ["norm1_g", "w_in", "q_norm_g", "k_norm_g", "b_f", "conv_w", "conv_b", "w_a", "b_a", "w_x", "b_x", "lam",
             "attn_out_g", "lru_out_g", "w_out", "norm2_g", "w_gate", "w_up", "w_down"]
    return (loss, grad_x.reshape(bl, seq, D_MODEL), *[out[n][0] for n in order], *[out[n][1] for n in order],
            *[out[n][2] for n in order], *[out[n][3] for n in order])
```

```python
import functools
import math

import jax
import jax.numpy as jnp
from jax import lax
from jax.experimental import pallas as pl
from jax.experimental.pallas import tpu as pltpu

F32 = jnp.float32
MXU_DTYPE = jnp.bfloat16
MESH = pl.DeviceIdType.MESH

D_MODEL = 1024
ATT_WIDTH = 512
LRU_WIDTH = 512
HEADS = 8
HEAD_DIM = 64
PAIR = 2 * HEAD_DIM
N_PAIR = HEADS // 2
LRU_BLOCKS = 8
CONV_WIDTH = 4
LRU_C = 8.0
NORM_EPS = 1e-6
QK_SCALE = 1.0 / math.sqrt(HEAD_DIM)
F_PAD = 128
N_CAT = 3 * ATT_WIDTH + F_PAD + 2 * LRU_WIDTH
NEG = -1e30

ADAM_LR, ADAM_B1, ADAM_B2, ADAM_EPS, ADAM_WD, ADAM_STEP = 0.001, 0.9, 0.999, 1e-08, 0.01, 10

TOKEN_TILE = 256
ATT_TILE = 512
LRU_TILE = 256
VMEM_SMALL = 32 * 1024 * 1024
VMEM_LARGE = 56 * 1024 * 1024


def _params(sem, vmem=VMEM_SMALL):
    return pltpu.CompilerParams(dimension_semantics=sem, vmem_limit_bytes=vmem)


def _const(shape):
    nd = len(shape)
    return pl.BlockSpec(shape, lambda *_: (0,) * nd)


def _sigmoid(x):
    return 1.0 / (1.0 + jnp.exp(-x))


def _nt(a, b):
    return lax.dot_general(a, b, (((1,), (1,)), ((), ())), preferred_element_type=F32)


def _tn(a, b):
    return lax.dot_general(a, b, (((0,), (0,)), ((), ())), preferred_element_type=F32)


def _half_sums(t, lo):
    s_lo = jnp.sum(jnp.where(lo, t, 0.0), axis=-1, keepdims=True)
    s_hi = jnp.sum(jnp.where(lo, 0.0, t), axis=-1, keepdims=True)
    return jnp.where(lo, s_lo, s_hi)


def _lo_mask():
    return lax.broadcasted_iota(jnp.int32, (1, PAIR), 1) < HEAD_DIM


def _other_chips(x, y):
    return [(1 - x, y), (x, 1 - y), (1 - x, 1 - y)]


def _chip_copies(ins, outs, send_sems, recv_sems, loc_sems, scatter):
    x, y, c = lax.axis_index("x"), lax.axis_index("y"), lax.axis_index("c")
    me = 2 * x + y
    copies = []
    for w in range(len(ins)):
        copies.append(pltpu.make_async_copy(ins[w].at[me] if scatter else ins[w], outs[w].at[me], loc_sems.at[w]))
        for k, (cx, cy) in enumerate(_other_chips(x, y)):
            copies.append(pltpu.make_async_remote_copy(
                src_ref=ins[w].at[2 * cx + cy] if scatter else ins[w], dst_ref=outs[w].at[me],
                send_sem=send_sems.at[3 * w + k], recv_sem=recv_sems.at[3 * w + k],
                device_id=(cx, cy, c), device_id_type=MESH))
    return copies


def _chip_sems(n):
    return [pltpu.SemaphoreType.DMA((3 * n,)), pltpu.SemaphoreType.DMA((3 * n,)), pltpu.SemaphoreType.DMA((n,))]


def _gather_over_chips(shards):
    n = len(shards)

    def body(*refs):
        copies = _chip_copies(refs[:n], refs[n:2 * n], *refs[2 * n:], scatter=False)
        for cp in copies:
            cp.start()
        for cp in copies:
            cp.wait()

    return pl.pallas_call(
        body, name="gather_weights",
        out_shape=[jax.ShapeDtypeStruct((4,) + s.shape, s.dtype) for s in shards],
        in_specs=[pl.BlockSpec(memory_space=pl.ANY)] * n,
        out_specs=[pl.BlockSpec(memory_space=pl.ANY)] * n,
        scratch_shapes=_chip_sems(n),
    )(*shards)


def _exchange_grads(slabs, pack):
    n = len(slabs)

    def body(*refs):
        ins, pack_in = refs[:n], refs[n]
        outs, pack_out = refs[n + 1:2 * n + 1], refs[2 * n + 1]
        send_sems, recv_sems, loc_sems, psend, precv = refs[2 * n + 2:]
        x, y, c = lax.axis_index("x"), lax.axis_index("y"), lax.axis_index("c")
        dev = 4 * x + 2 * y + c
        copies = _chip_copies(ins, outs, send_sems, recv_sems, loc_sems, scatter=True)
        copies.append(pltpu.make_async_copy(pack_in, pack_out.at[dev], loc_sems.at[n]))
        for k in range(1, 8):
            fx, fy, fc = (k >> 2) & 1, (k >> 1) & 1, k & 1
            tx = (1 - x) if fx else x
            ty = (1 - y) if fy else y
            tc = (1 - c) if fc else c
            copies.append(pltpu.make_async_remote_copy(
                src_ref=pack_in, dst_ref=pack_out.at[dev],
                send_sem=psend.at[k - 1], recv_sem=precv.at[k - 1],
                device_id=(tx, ty, tc), device_id_type=MESH))
        for cp in copies:
            cp.start()
        for cp in copies:
            cp.wait()

    return pl.pallas_call(
        body, name="exchange_grads",
        out_shape=[jax.ShapeDtypeStruct(s.shape, s.dtype) for s in slabs]
        + [jax.ShapeDtypeStruct((8,) + pack.shape, pack.dtype)],
        in_specs=[pl.BlockSpec(memory_space=pl.ANY)] * (n + 1),
        out_specs=[pl.BlockSpec(memory_space=pl.ANY)] * (n + 1),
        scratch_shapes=[pltpu.SemaphoreType.DMA((3 * n,)), pltpu.SemaphoreType.DMA((3 * n,)),
                        pltpu.SemaphoreType.DMA((n + 1,)),
                        pltpu.SemaphoreType.DMA((7,)), pltpu.SemaphoreType.DMA((7,))],
    )(*slabs, pack)


def _swap_with_sibling(arrs):
    n = len(arrs)

    def body(*refs):
        ins, outs = refs[:n], refs[n:2 * n]
        send_sems, recv_sems = refs[2 * n:]
        x, y, c = lax.axis_index("x"), lax.axis_index("y"), lax.axis_index("c")
        copies = []
        for w in range(n):
            cp = pltpu.make_async_remote_copy(
                src_ref=ins[w], dst_ref=outs[w], send_sem=send_sems.at[w], recv_sem=recv_sems.at[w],
                device_id=(x, y, 1 - c), device_id_type=MESH)
            cp.start()
            copies.append(cp)
        for cp in copies:
            cp.wait()

    return pl.pallas_call(
        body, name="swap_sibling",
        out_shape=[jax.ShapeDtypeStruct(a.shape, a.dtype) for a in arrs],
        in_specs=[pl.BlockSpec(memory_space=pl.ANY)] * n,
        out_specs=[pl.BlockSpec(memory_space=pl.ANY)] * n,
        scratch_shapes=[pltpu.SemaphoreType.DMA((n,)), pltpu.SemaphoreType.DMA((n,))],
    )(*arrs)


def _head_norm(t, g2, lo):
    rr = lax.rsqrt(_half_sums(t * t, lo) * (1.0 / HEAD_DIM) + NORM_EPS)
    return t * rr * g2


def _inproj(x2d, g1, wcat, gq2, gk2):
    T = x2d.shape[0]
    tm = TOKEN_TILE

    def body(x_ref, g1_ref, w_ref, gq_ref, gk_ref, qkv_ref, qn_ref, kn_ref, vb_ref, f_ref, lx_ref, lg_ref):
        x = x_ref[...]
        r = lax.rsqrt(jnp.mean(x * x, axis=-1, keepdims=True) + NORM_EPS)
        h = (x * r * g1_ref[...]).astype(MXU_DTYPE)
        proj = _nt(h, w_ref[...])
        qkv_ref[...] = proj[:, :3 * ATT_WIDTH]
        lo = _lo_mask()
        for p in range(N_PAIR):
            cols = slice(PAIR * p, PAIR * (p + 1))
            q = proj[:, PAIR * p:PAIR * (p + 1)]
            k = proj[:, ATT_WIDTH + PAIR * p:ATT_WIDTH + PAIR * (p + 1)]
            qn_ref[:, cols] = (_head_norm(q, gq_ref[...], lo) * QK_SCALE).astype(MXU_DTYPE)
            kn_ref[:, cols] = _head_norm(k, gk_ref[...], lo).astype(MXU_DTYPE)
        vb_ref[...] = proj[:, 2 * ATT_WIDTH:3 * ATT_WIDTH].astype(MXU_DTYPE)
        f0 = 3 * ATT_WIDTH
        f_ref[...] = proj[:, f0:f0 + F_PAD]
        lx_ref[...] = proj[:, f0 + F_PAD:f0 + F_PAD + LRU_WIDTH]
        lg_ref[...] = proj[:, f0 + F_PAD + LRU_WIDTH:]

    row = lambda w: pl.BlockSpec((tm, w), lambda i: (i, 0))
    return pl.pallas_call(
        body, name="inproj", grid=(T // tm,),
        in_specs=[row(D_MODEL), _const((1, D_MODEL)), _const((N_CAT, D_MODEL)), _const((1, PAIR)), _const((1, PAIR))],
        out_specs=[row(3 * ATT_WIDTH), row(ATT_WIDTH), row(ATT_WIDTH), row(ATT_WIDTH), row(F_PAD),
                   row(LRU_WIDTH), row(LRU_WIDTH)],
        out_shape=[jax.ShapeDtypeStruct((T, 3 * ATT_WIDTH), F32),
                   jax.ShapeDtypeStruct((T, ATT_WIDTH), MXU_DTYPE), jax.ShapeDtypeStruct((T, ATT_WIDTH), MXU_DTYPE),
                   jax.ShapeDtypeStruct((T, ATT_WIDTH), MXU_DTYPE), jax.ShapeDtypeStruct((T, F_PAD), F32),
                   jax.ShapeDtypeStruct((T, LRU_WIDTH), F32), jax.ShapeDtypeStruct((T, LRU_WIDTH), F32)],
        compiler_params=_params(("parallel",), VMEM_LARGE),
    )(x2d, g1, wcat, gq2, gk2)


def _forget_cumsum(f2d, bf, bl, seq):
    def body(z_ref, b_ref, o_ref):
        z = z_ref[...] + b_ref[...]
        lf = jnp.minimum(z, 0.0) - jnp.log(1.0 + jnp.exp(-jnp.abs(z)))
        row = lax.broadcasted_iota(jnp.int32, (seq, F_PAD), 0)
        k = 1
        while k < seq:
            lf = lf + jnp.where(row >= k, pltpu.roll(lf, k, 0), 0.0)
            k *= 2
        o_ref[...] = lf

    return pl.pallas_call(
        body, name="forget_cumsum", grid=(bl,),
        in_specs=[pl.BlockSpec((seq, F_PAD), lambda b: (b, 0)), _const((1, F_PAD))],
        out_specs=pl.BlockSpec((seq, F_PAD), lambda b: (b, 0)),
        out_shape=jax.ShapeDtypeStruct(f2d.shape, F32),
        compiler_params=_params(("parallel",)),
    )(f2d, bf)


def _attn_fwd(qn, kn, vb, frow, fstart, bl, seq, shards):
    tq = min(ATT_TILE, seq)
    nq = seq // tq
    T = bl * seq
    n = len(shards)

    def body(fs_ref, q_ref, k_ref, v_ref, fr_ref, *rest):
        g_in, (o_ref, ox_ref, lse_ref), g_out, sems = rest[:n], rest[n:n + 3], rest[n + 3:2 * n + 3], rest[2 * n + 3:]
        b, p, i = pl.program_id(0), pl.program_id(1), pl.program_id(2)
        copies = _chip_copies(g_in, g_out, *sems, scatter=False)

        @pl.when((b == 0) & (p == 0) & (i == 0))
        def _():
            for cp in copies:
                cp.start()

        lane = lax.broadcasted_iota(jnp.int32, (1, PAIR), 1)
        rows = lax.broadcasted_iota(jnp.int32, (tq, tq), 0)
        cols = lax.broadcasted_iota(jnp.int32, (tq, tq), 1)
        causal = cols <= rows
        q = q_ref[...]
        hms = [(lane >= HEAD_DIM * hh) & (lane < HEAD_DIM * (hh + 1)) for hh in range(2)]
        qhs = [jnp.where(hm, q, jnp.zeros_like(q)) for hm in hms]
        shifts = [fs_ref[((b * N_PAIR + p) * 2 + hh) * nq + i] for hh in range(2)]

        def block(j, carry, masked):
            start = pl.multiple_of(j * tq, tq)
            k = k_ref[pl.ds(start, tq), :]
            v = v_ref[pl.ds(start, tq), :]
            new = []
            for hh in range(2):
                m, l, acc, acc_lo = carry[hh]
                s = lax.dot_general(qhs[hh], k, (((1,), (1,)), ((), ())), preferred_element_type=F32)
                s = s - (fr_ref[0, 0, hh:hh + 1, pl.ds(start, tq)] - shifts[hh])
                if masked:
                    s = jnp.where(causal, s, NEG)
                m_new = jnp.maximum(m, jnp.max(s, axis=-1, keepdims=True))
                alpha = jnp.exp(m - m_new)
                pe = jnp.exp(s - m_new)
                l = alpha * l + jnp.sum(pe, axis=-1, keepdims=True)
                vh = jnp.where(hms[hh], v, jnp.zeros_like(v))
                pb = pe.astype(MXU_DTYPE)
                p_lo = (pe - pb.astype(F32)).astype(MXU_DTYPE)
                acc = alpha * acc + jnp.dot(pb, vh, preferred_element_type=F32)
                acc_lo = alpha * acc_lo + jnp.dot(p_lo, vh, preferred_element_type=F32)
                new.append((m_new, l, acc, acc_lo))
            return tuple(new)

        init = (jnp.full((tq, 1), NEG, F32), jnp.zeros((tq, 1), F32), jnp.zeros((tq, PAIR), F32),
                jnp.zeros((tq, PAIR), F32))
        carry = lax.fori_loop(0, i, functools.partial(block, masked=False), (init, init))
        carry = block(i, carry, True)
        out = jnp.zeros((tq, PAIR), F32)
        out_x = jnp.zeros((tq, PAIR), F32)
        lse = jnp.zeros((tq, PAIR), F32)
        for hh in range(2):
            m, l, acc, acc_lo = carry[hh]
            inv_l = 1.0 / l
            out = out + acc * inv_l
            out_x = out_x + (acc + acc_lo) * inv_l
            lse = jnp.where(hms[hh], m + jnp.log(l), lse)
        o_ref[...] = out
        ox_ref[...] = out_x
        lse_ref[...] = lse

        @pl.when((b == bl - 1) & (p == N_PAIR - 1) & (i == nq - 1))
        def _():
            for cp in copies:
                cp.wait()

    blk = pl.BlockSpec((tq, PAIR), lambda b, p, i: (b * nq + i, p))
    full = pl.BlockSpec((seq, PAIR), lambda b, p, i: (b, p))
    return pl.pallas_call(
        body, name="attn_fwd", grid=(bl, N_PAIR, nq),
        in_specs=[pl.BlockSpec(memory_space=pltpu.SMEM), blk, full, full,
                  pl.BlockSpec((1, 1, 2, seq), lambda b, p, i: (b, p, 0, 0))] + [pl.BlockSpec(memory_space=pl.ANY)] * n,
        out_specs=[blk, blk, blk] + [pl.BlockSpec(memory_space=pl.ANY)] * n,
        out_shape=[jax.ShapeDtypeStruct((T, ATT_WIDTH), F32)] * 3
        + [jax.ShapeDtypeStruct((4,) + s.shape, s.dtype) for s in shards],
        scratch_shapes=_chip_sems(n),
        compiler_params=_params(("arbitrary", "arbitrary", "arbitrary")),
    )(fstart, qn, kn, vb, frow, *shards)


def _conv_taps(lx, prev8, cw, cb):
    xs = jnp.concatenate([prev8, lx], axis=0)
    shifted = [lx] + [pltpu.roll(xs, k, 0)[8:] for k in range(1, CONV_WIDTH)]
    xc = cb + cw[CONV_WIDTH - 1:CONV_WIDTH] * lx
    for k in range(1, CONV_WIDTH):
        xc = xc + cw[CONV_WIDTH - 1 - k:CONV_WIDTH - k] * shifted[k]
    return xc, shifted


def _lru_gates(xc, wa, ba, wx, bx, lam):
    xb = xc.astype(MXU_DTYPE)
    r = _sigmoid(jnp.dot(xb, wa, preferred_element_type=F32) + ba)
    ig = _sigmoid(jnp.dot(xb, wx, preferred_element_type=F32) + bx)
    sp = jnp.maximum(-lam, 0.0) + jnp.log(1.0 + jnp.exp(-jnp.abs(lam)))
    log_a = -LRU_C * r * sp
    a = jnp.exp(log_a)
    th = jnp.tanh(log_a)
    mult = jnp.sqrt(-2.0 * th / (1.0 - th))
    return r, ig, sp, a, mult


def _gelu_parts(x):
    c0 = math.sqrt(2.0 / math.pi)
    t = jnp.tanh(c0 * (x + 0.044715 * x * x * x))
    g = 0.5 * x * (1.0 + t)
    dg = 0.5 * (1.0 + t) + 0.5 * x * (1.0 - t * t) * c0 * (1.0 + 3.0 * 0.044715 * x * x)
    return g, dg


def _lru_fwd(lx, lg, cw, cb, wa, ba, wx, bx, lam, bl, seq):
    tc = min(LRU_TILE, seq)
    nc = seq // tc
    T = bl * seq

    def body(lx_ref, lxp_ref, lg_ref, cw_ref, cb_ref, wa_ref, ba_ref, wx_ref, bx_ref, lam_ref,
             h_ref, rec_ref, hc_ref):
        i = pl.program_id(1)

        @pl.when(i == 0)
        def _():
            hc_ref[...] = jnp.zeros_like(hc_ref)

        lxv = lx_ref[...]
        prev8 = jnp.where(i > 0, lxp_ref[...], 0.0)
        xc, _ = _conv_taps(lxv, prev8, cw_ref[...], cb_ref[...])
        _, ig, _, a, mult = _lru_gates(xc, wa_ref[...], ba_ref[...], wx_ref[...], bx_ref[...], lam_ref[...])
        u = mult * (ig * xc)
        row = lax.broadcasted_iota(jnp.int32, (tc, LRU_WIDTH), 0)
        A, B = a, u
        k = 1
        while k < tc:
            a_s = jnp.where(row >= k, pltpu.roll(A, k, 0), 1.0)
            b_s = jnp.where(row >= k, pltpu.roll(B, k, 0), 0.0)
            B = A * b_s + B
            A = A * a_s
            k *= 2
        h = A * hc_ref[0:1, :] + B
        hc_ref[0:1, :] = h[tc - 1:tc, :]
        h_ref[...] = h
        g, _ = _gelu_parts(lg_ref[...])
        rec_ref[...] = h * g

    tile = pl.BlockSpec((tc, LRU_WIDTH), lambda b, i: (b * nc + i, 0))
    prev = pl.BlockSpec((8, LRU_WIDTH), lambda b, i: (jnp.maximum((b * seq + i * tc) // 8 - 1, 0), 0))
    vec = _const((1, LRU_WIDTH))
    mat = _const((LRU_WIDTH, LRU_WIDTH))
    return pl.pallas_call(
        body, name="lru_fwd", grid=(bl, nc),
        in_specs=[tile, prev, tile, _const((CONV_WIDTH, LRU_WIDTH)), vec, mat, vec, mat, vec, vec],
        out_specs=[tile, tile],
        out_shape=[jax.ShapeDtypeStruct((T, LRU_WIDTH), F32), jax.ShapeDtypeStruct((T, LRU_WIDTH), F32)],
        scratch_shapes=[pltpu.VMEM((8, LRU_WIDTH), F32)],
        compiler_params=_params(("arbitrary", "arbitrary")),
    )(lx, lx, lg, cw, cb, wa, ba, wx, bx, lam)


def _outproj(x2d, att, rec, ga, gr, wout):
    T = x2d.shape[0]
    tm = TOKEN_TILE

    def body(x_ref, a_ref, r_ref, ga_ref, gr_ref, w_ref, o_ref):
        a = a_ref[...]
        rc = r_ref[...]
        na = a * lax.rsqrt(jnp.mean(a * a, axis=-1, keepdims=True) + NORM_EPS) * ga_ref[...]
        nr = rc * lax.rsqrt(jnp.mean(rc * rc, axis=-1, keepdims=True) + NORM_EPS) * gr_ref[...]
        o_ref[...] = (x_ref[...]
                      + jnp.dot(na.astype(MXU_DTYPE), w_ref[:ATT_WIDTH, :], preferred_element_type=F32)
                      + jnp.dot(nr.astype(MXU_DTYPE), w_ref[ATT_WIDTH:, :], preferred_element_type=F32))

    row = lambda w: pl.BlockSpec((tm, w), lambda i: (i, 0))
    return pl.pallas_call(
        body, name="outproj", grid=(T // tm,),
        in_specs=[row(D_MODEL), row(ATT_WIDTH), row(LRU_WIDTH), _const((1, ATT_WIDTH)), _const((1, LRU_WIDTH)),
                  _const((D_MODEL, D_MODEL))],
        out_specs=row(D_MODEL),
        out_shape=jax.ShapeDtypeStruct((T, D_MODEL), F32),
        compiler_params=_params(("parallel",)),
    )(x2d, att, rec, ga, gr, wout)


def _mlp_fwd(x2, g2, wg, wu, wd, target):
    T = x2.shape[0]
    tm = TOKEN_TILE
    dff = wg.shape[0]

    def body(x_ref, g_ref, wg_ref, wu_ref, wd_ref, t_ref, gt_ref, up_ref, dy_ref, loss_ref):
        @pl.when(pl.program_id(0) == 0)
        def _():
            loss_ref[...] = jnp.zeros_like(loss_ref)

        x = x_ref[...]
        r = lax.rsqrt(jnp.mean(x * x, axis=-1, keepdims=True) + NORM_EPS)
        h = (x * r * g_ref[...]).astype(MXU_DTYPE)
        gt = _nt(h, wg_ref[...])
        up = _nt(h, wu_ref[...])
        gt_ref[...] = gt
        up_ref[...] = up
        act = (gt * _sigmoid(gt) * up).astype(MXU_DTYPE)
        y = x + jnp.dot(act, wd_ref[...], preferred_element_type=F32)
        e = y - t_ref[...]
        dy_ref[...] = e * (1.0 / D_MODEL)
        loss_ref[...] += jnp.sum(e * e)

    row = lambda w: pl.BlockSpec((tm, w), lambda i: (i, 0))
    return pl.pallas_call(
        body, name="mlp_fwd", grid=(T // tm,),
        in_specs=[row(D_MODEL), _const((1, D_MODEL)), _const((dff, D_MODEL)), _const((dff, D_MODEL)),
                  _const((dff, D_MODEL)), row(D_MODEL)],
        out_specs=[row(dff), row(dff), row(D_MODEL), _const((8, 128))],
        out_shape=[jax.ShapeDtypeStruct((T, dff), F32), jax.ShapeDtypeStruct((T, dff), F32),
                   jax.ShapeDtypeStruct((T, D_MODEL), F32), jax.ShapeDtypeStruct((8, 128), F32)],
        compiler_params=_params(("arbitrary",), VMEM_LARGE),
    )(x2, g2, wg, wu, wd, target)


def _mlp_bwd(dy, x2, gt, up, g2, wg, wu, wd):
    T = x2.shape[0]
    tm = TOKEN_TILE
    dff = wg.shape[0]

    def body(dy_ref, x_ref, gt_ref, up_ref, g_ref, wg_ref, wu_ref, wd_ref,
             dx_ref, dxb_ref, dgt_ref, dup_ref, act_ref, h_ref, dyb_ref, dg_ref):
        @pl.when(pl.program_id(0) == 0)
        def _():
            dg_ref[...] = jnp.zeros_like(dg_ref)

        dy_v = dy_ref[...]
        dyb = dy_v.astype(MXU_DTYPE)
        dyb_ref[...] = dyb
        x = x_ref[...]
        r = lax.rsqrt(jnp.mean(x * x, axis=-1, keepdims=True) + NORM_EPS)
        xh = x * r
        h_ref[...] = (xh * g_ref[...]).astype(MXU_DTYPE)
        gt_v = gt_ref[...]
        up_v = up_ref[...]
        sg = _sigmoid(gt_v)
        silu = gt_v * sg
        act_ref[...] = (silu * up_v).astype(MXU_DTYPE)
        dact = _nt(dyb, wd_ref[...])
        dup = (dact * silu).astype(MXU_DTYPE)
        dgt = (dact * up_v * (sg * (1.0 + gt_v * (1.0 - sg)))).astype(MXU_DTYPE)
        dup_ref[...] = dup
        dgt_ref[...] = dgt
        dh = (jnp.dot(dgt, wg_ref[...], preferred_element_type=F32)
              + jnp.dot(dup, wu_ref[...], preferred_element_type=F32))
        dg_ref[...] += jnp.sum(dh * xh, axis=0, keepdims=True)
        dxh = dh * g_ref[...]
        dx = dy_v + r * (dxh - xh * jnp.mean(dxh * xh, axis=-1, keepdims=True))
        dx_ref[...] = dx
        dxb_ref[...] = dx.astype(MXU_DTYPE)

    row = lambda w: pl.BlockSpec((tm, w), lambda i: (i, 0))
    return pl.pallas_call(
        body, name="mlp_bwd", grid=(T // tm,),
        in_specs=[row(D_MODEL), row(D_MODEL), row(dff), row(dff), _const((1, D_MODEL)),
                  _const((dff, D_MODEL)), _const((dff, D_MODEL)), _const((dff, D_MODEL))],
        out_specs=[row(D_MODEL), row(D_MODEL), row(dff), row(dff), row(dff), row(D_MODEL), row(D_MODEL),
                   _const((1, D_MODEL))],
        out_shape=[jax.ShapeDtypeStruct((T, D_MODEL), F32), jax.ShapeDtypeStruct((T, D_MODEL), MXU_DTYPE),
                   jax.ShapeDtypeStruct((T, dff), MXU_DTYPE), jax.ShapeDtypeStruct((T, dff), MXU_DTYPE),
                   jax.ShapeDtypeStruct((T, dff), MXU_DTYPE), jax.ShapeDtypeStruct((T, D_MODEL), MXU_DTYPE),
                   jax.ShapeDtypeStruct((T, D_MODEL), MXU_DTYPE), jax.ShapeDtypeStruct((1, D_MODEL), F32)],
        compiler_params=_params(("arbitrary",), VMEM_LARGE),
    )(dy, x2, gt, up, g2, wg, wu, wd)


def _matmul_tn(a, b, tn, name):
    T, K = a.shape
    N = b.shape[1]
    tt = min(512, T)

    def body(a_ref, b_ref, o_ref):
        @pl.when(pl.program_id(1) == 0)
        def _():
            o_ref[...] = jnp.zeros_like(o_ref)

        o_ref[...] += _tn(a_ref[...], b_ref[...])

    return pl.pallas_call(
        body, name=name, grid=(N // tn, T // tt),
        in_specs=[pl.BlockSpec((tt, K), lambda n, t: (t, 0)), pl.BlockSpec((tt, tn), lambda n, t: (t, n))],
        out_specs=pl.BlockSpec((K, tn), lambda n, t: (0, n)),
        out_shape=jax.ShapeDtypeStruct((K, N), F32),
        compiler_params=_params(("parallel", "arbitrary"), VMEM_LARGE),
    )(a, b)


def _outproj_bwd(dx2b, att, att_x, rec, ga, gr, wout):
    T = att.shape[0]
    tm = TOKEN_TILE

    def body(dx_ref, a_ref, ax_ref, r_ref, ga_ref, gr_ref, w_ref, datt_ref, delta_ref, drec_ref, mix_ref, dga_ref, dgr_ref):
        @pl.when(pl.program_id(0) == 0)
        def _():
            dga_ref[...] = jnp.zeros_like(dga_ref)
            dgr_ref[...] = jnp.zeros_like(dgr_ref)

        dmix = _nt(dx_ref[...], w_ref[...])

        def norm_bwd(v, g, dn):
            rr = lax.rsqrt(jnp.mean(v * v, axis=-1, keepdims=True) + NORM_EPS)
            vh = v * rr
            dvh = dn * g
            dv = rr * (dvh - vh * jnp.mean(dvh * vh, axis=-1, keepdims=True))
            return vh, dv, jnp.sum(dn * vh, axis=0, keepdims=True)

        a = a_ref[...]
        ah, datt, dga = norm_bwd(a, ga_ref[...], dmix[:, :ATT_WIDTH])
        rh, drec, dgr = norm_bwd(r_ref[...], gr_ref[...], dmix[:, ATT_WIDTH:])
        dga_ref[...] += dga
        dgr_ref[...] += dgr
        mix_ref[:, :ATT_WIDTH] = (ah * ga_ref[...]).astype(MXU_DTYPE)
        mix_ref[:, ATT_WIDTH:] = (rh * gr_ref[...]).astype(MXU_DTYPE)
        dattb = datt.astype(MXU_DTYPE)
        datt_ref[...] = dattb
        drec_ref[...] = drec
        lo = _lo_mask()
        prod = dattb.astype(F32) * ax_ref[...]
        for p in range(N_PAIR):
            delta_ref[:, PAIR * p:PAIR * (p + 1)] = _half_sums(prod[:, PAIR * p:PAIR * (p + 1)], lo)

    row = lambda w: pl.BlockSpec((tm, w), lambda i: (i, 0))
    return pl.pallas_call(
        body, name="outproj_bwd", grid=(T // tm,),
        in_specs=[row(D_MODEL), row(ATT_WIDTH), row(ATT_WIDTH), row(LRU_WIDTH), _const((1, ATT_WIDTH)), _const((1, LRU_WIDTH)),
                  _const((D_MODEL, D_MODEL))],
        out_specs=[row(ATT_WIDTH), row(ATT_WIDTH), row(LRU_WIDTH), row(D_MODEL),
                   _const((1, ATT_WIDTH)), _const((1, LRU_WIDTH))],
        out_shape=[jax.ShapeDtypeStruct((T, ATT_WIDTH), MXU_DTYPE), jax.ShapeDtypeStruct((T, ATT_WIDTH), F32),
                   jax.ShapeDtypeStruct((T, LRU_WIDTH), F32), jax.ShapeDtypeStruct((T, D_MODEL), MXU_DTYPE),
                   jax.ShapeDtypeStruct((1, ATT_WIDTH), F32), jax.ShapeDtypeStruct((1, LRU_WIDTH), F32)],
        compiler_params=_params(("arbitrary",)),
    )(dx2b, att, att_x, rec, ga, gr, wout)


def _lru_bwd(drec, lg, h, lx, cw, cb, wa, ba, wx, bx, lam, bl, seq):
    tc = min(LRU_TILE, seq)
    nc = seq // tc
    T = bl * seq
    n = tc

    def body(dr_ref, lg_ref, h_ref, hp_ref, lx_ref, lxp_ref, cw_ref, cb_ref, wa_ref, ba_ref, wx_ref, bx_ref, lam_ref,
             dlx_ref, dlg_ref, dwa_ref, dwx_ref, small_ref, gc_ref, dxn_ref):
        b, i = pl.program_id(0), pl.program_id(1)
        ir = nc - 1 - i

        @pl.when((b == 0) & (i == 0))
        def _():
            dwa_ref[...] = jnp.zeros_like(dwa_ref)
            dwx_ref[...] = jnp.zeros_like(dwx_ref)
            small_ref[...] = jnp.zeros_like(small_ref)

        @pl.when(i == 0)
        def _():
            gc_ref[...] = jnp.zeros_like(gc_ref)
            dxn_ref[...] = jnp.zeros_like(dxn_ref)

        cw = cw_ref[...]
        lam_v = lam_ref[...]
        lxv = lx_ref[...]
        prev8 = jnp.where(ir > 0, lxp_ref[...], 0.0)
        xc, shifted = _conv_taps(lxv, prev8, cw, cb_ref[...])
        r, ig, sp, a, mult = _lru_gates(xc, wa_ref[...], ba_ref[...], wx_ref[...], bx_ref[...], lam_v)
        hv = h_ref[...]
        drv = dr_ref[...]
        g, dg = _gelu_parts(lg_ref[...])
        dlg_ref[...] = drv * hv * dg
        dh = drv * g

        row = lax.broadcasted_iota(jnp.int32, (n, LRU_WIDTH), 0)
        A = jnp.where(row < n - 1, pltpu.roll(a, n - 1, 0), 0.0)
        B = dh + jnp.where(row == n - 1, gc_ref[0:1, :], 0.0)
        k = 1
        while k < n:
            a_s = jnp.where(row < n - k, pltpu.roll(A, n - k, 0), 1.0)
            b_s = jnp.where(row < n - k, pltpu.roll(B, n - k, 0), 0.0)
            B = B + A * b_s
            A = A * a_s
            k *= 2
        gs = B
        gc_ref[0:1, :] = a[0:1, :] * gs[0:1, :]

        hprev8 = jnp.where(ir > 0, hp_ref[...], 0.0)
        h_prev = pltpu.roll(jnp.concatenate([hprev8, hv], axis=0), 1, 0)[8:]
        da = gs * h_prev
        ix = ig * xc
        dmult = gs * ix
        dig = gs * mult * xc
        dxc = gs * mult * ig
        dlog_a = da * a - dmult * (a * a) / mult
        dr_gate = dlog_a * (-LRU_C * sp)
        dza = dr_gate * r * (1.0 - r)
        dzx = dig * ig * (1.0 - ig)
        dzab = dza.astype(MXU_DTYPE)
        dzxb = dzx.astype(MXU_DTYPE)
        xcb = xc.astype(MXU_DTYPE)
        dwa_ref[...] += _tn(xcb, dzab)
        dwx_ref[...] += _tn(xcb, dzxb)
        dxc = dxc + _nt(dzab, wa_ref[...]) + _nt(dzxb, wx_ref[...])

        ds = jnp.concatenate([dxc, dxn_ref[...]], axis=0)
        dlx = cw[CONV_WIDTH - 1:CONV_WIDTH] * dxc
        for k in range(1, CONV_WIDTH):
            dlx = dlx + cw[CONV_WIDTH - 1 - k:CONV_WIDTH - k] * pltpu.roll(ds, n + 8 - k, 0)[:n]
        dlx_ref[...] = dlx
        dxn_ref[...] = dxc[0:8, :]

        colsum = lambda v: jnp.sum(v, axis=0, keepdims=True)
        small_ref[0:1, :] += colsum(dza)
        small_ref[1:2, :] += colsum(dzx)
        small_ref[2:3, :] += colsum(dlog_a * r) * (LRU_C * _sigmoid(-lam_v))
        small_ref[3:4, :] += colsum(dxc)
        for k in range(CONV_WIDTH):
            j = CONV_WIDTH - 1 - k
            small_ref[4 + j:5 + j, :] += colsum(dxc * shifted[k])

    tile = pl.BlockSpec((tc, LRU_WIDTH), lambda b, i: (b * nc + (nc - 1 - i), 0))
    prev = pl.BlockSpec((8, LRU_WIDTH), lambda b, i: (jnp.maximum((b * seq + (nc - 1 - i) * tc) // 8 - 1, 0), 0))
    vec = _const((1, LRU_WIDTH))
    mat = _const((LRU_WIDTH, LRU_WIDTH))
    return pl.pallas_call(
        body, name="lru_bwd", grid=(bl, nc),
        in_specs=[tile, tile, tile, prev, tile, prev, _const((CONV_WIDTH, LRU_WIDTH)), vec, mat, vec, mat, vec, vec],
        out_specs=[tile, tile, mat, mat, _const((8, LRU_WIDTH))],
        out_shape=[jax.ShapeDtypeStruct((T, LRU_WIDTH), F32), jax.ShapeDtypeStruct((T, LRU_WIDTH), F32),
                   jax.ShapeDtypeStruct((LRU_WIDTH, LRU_WIDTH), F32), jax.ShapeDtypeStruct((LRU_WIDTH, LRU_WIDTH), F32),
                   jax.ShapeDtypeStruct((8, LRU_WIDTH), F32)],
        scratch_shapes=[pltpu.VMEM((8, LRU_WIDTH), F32), pltpu.VMEM((8, LRU_WIDTH), F32)],
        compiler_params=_params(("arbitrary", "arbitrary")),
    )(drec, lg, h, h, lx, lx, cw, cb, wa, ba, wx, bx, lam)


def _attn_bwd(qn, kn, vb, dob, lse, delta, frow, fstart, bl, seq, slabs):
    tq = min(ATT_TILE, seq)
    nq = seq // tq
    T = bl * seq
    n = len(slabs)

    def body(fs_ref, q_ref, k_ref, v_ref, do_ref, lse_ref, dl_ref, fr_ref, *rest):
        s_in, (dq_ref, dk_ref, dv_ref, df_ref), s_out, sems = rest[:n], rest[n:n + 4], rest[n + 4:2 * n + 4], rest[2 * n + 4:]
        b, p, j = pl.program_id(0), pl.program_id(1), pl.program_id(2)
        copies = _chip_copies(s_in, s_out, *sems, scatter=True)

        @pl.when((b == 0) & (p == 0) & (j == 0))
        def _():
            for cp in copies:
                cp.start()

        @pl.when(j == 0)
        def _():
            dq_ref[...] = jnp.zeros_like(dq_ref)

        lane = lax.broadcasted_iota(jnp.int32, (1, PAIR), 1)
        rows = lax.broadcasted_iota(jnp.int32, (tq, tq), 0)
        cols = lax.broadcasted_iota(jnp.int32, (tq, tq), 1)
        causal = cols <= rows
        kv = k_ref[...]
        vv = v_ref[...]
        hms = [(lane >= HEAD_DIM * hh) & (lane < HEAD_DIM * (hh + 1)) for hh in range(2)]
        khs = [jnp.where(hm, kv, jnp.zeros_like(kv)) for hm in hms]
        fks = [fr_ref[0, 0, hh:hh + 1, :] for hh in range(2)]
        bases = [((b * N_PAIR + p) * 2 + hh) * nq for hh in range(2)]

        def block(i, carry, masked):
            dk, dv, dfs = carry
            start = pl.multiple_of(i * tq, tq)
            qi = q_ref[pl.ds(start, tq), :]
            doi = do_ref[pl.ds(start, tq), :]
            dq = jnp.zeros((tq, PAIR), F32)
            new_dfs = []
            for hh in range(2):
                c0 = HEAD_DIM * hh
                qh = jnp.where(hms[hh], qi, jnp.zeros_like(qi))
                doh = jnp.where(hms[hh], doi, jnp.zeros_like(doi))
                s = _nt(qh, kv) - (fks[hh] - fs_ref[bases[hh] + i])
                if masked:
                    s = jnp.where(causal, s, NEG)
                pr = jnp.exp(s - lse_ref[pl.ds(start, tq), c0:c0 + 1])
                dp = _nt(doh, vv)
                ds = pr * (dp - dl_ref[pl.ds(start, tq), c0:c0 + 1])
                dsb = ds.astype(MXU_DTYPE)
                dv = dv + _tn(pr.astype(MXU_DTYPE), doh)
                dk = dk + _tn(dsb, qh)
                dq = dq + jnp.dot(dsb, khs[hh], preferred_element_type=F32)
                new_dfs.append(dfs[hh] - jnp.sum(ds, axis=0, keepdims=True))
            dq_ref[pl.ds(start, tq), :] += dq
            return dk, dv, tuple(new_dfs)

        zero = jnp.zeros((tq, PAIR), F32)
        carry = block(j, (zero, zero, (jnp.zeros((1, tq), F32), jnp.zeros((1, tq), F32))), True)
        dk, dv, dfs = lax.fori_loop(j + 1, nq, functools.partial(block, masked=False), carry)
        for hh in range(2):
            df_ref[0, 0, hh:hh + 1, :] = dfs[hh]
        dk_ref[...] = dk
        dv_ref[...] = dv

        @pl.when((b == bl - 1) & (p == N_PAIR - 1) & (j == nq - 1))
        def _():
            for cp in copies:
                cp.wait()

    blk = pl.BlockSpec((tq, PAIR), lambda b, p, j: (b * nq + j, p))
    full = pl.BlockSpec((seq, PAIR), lambda b, p, j: (b, p))
    fblk = pl.BlockSpec((1, 1, 2, tq), lambda b, p, j: (b, p, 0, j))
    hbm = pl.BlockSpec(memory_space=pl.ANY)
    return pl.pallas_call(
        body, name="attn_bwd", grid=(bl, N_PAIR, nq),
        in_specs=[pl.BlockSpec(memory_space=pltpu.SMEM), full, blk, blk, full, full, full, fblk] + [hbm] * n,
        out_specs=[full, blk, blk, fblk] + [hbm] * n,
        out_shape=[jax.ShapeDtypeStruct((T, ATT_WIDTH), F32), jax.ShapeDtypeStruct((T, ATT_WIDTH), F32),
                   jax.ShapeDtypeStruct((T, ATT_WIDTH), F32), jax.ShapeDtypeStruct((bl, N_PAIR, 2, seq), F32)]
        + [jax.ShapeDtypeStruct(s.shape, s.dtype) for s in slabs],
        scratch_shapes=_chip_sems(n),
        compiler_params=_params(("arbitrary", "arbitrary", "arbitrary"), VMEM_LARGE),
    )(fstart, qn, kn, vb, dob, lse, delta, frow, *slabs)


def _forget_bwd(dfcol, f2d, bf, bl, seq):
    def body(d_ref, z_ref, b_ref, o_ref, db_ref):
        @pl.when(pl.program_id(0) == 0)
        def _():
            db_ref[...] = jnp.zeros_like(db_ref)

        d = d_ref[...]
        row = lax.broadcasted_iota(jnp.int32, (seq, F_PAD), 0)
        k = 1
        while k < seq:
            d = d + jnp.where(row < seq - k, pltpu.roll(d, seq - k, 0), 0.0)
            k *= 2
        dz = d * _sigmoid(-(z_ref[...] + b_ref[...]))
        o_ref[...] = dz
        db_ref[...] += jnp.sum(dz, axis=0, keepdims=True)

    blk = pl.BlockSpec((seq, F_PAD), lambda b: (b, 0))
    return pl.pallas_call(
        body, name="forget_bwd", grid=(bl,),
        in_specs=[blk, blk, _const((1, F_PAD))],
        out_specs=[blk, _const((1, F_PAD))],
        out_shape=[jax.ShapeDtypeStruct(f2d.shape, F32), jax.ShapeDtypeStruct((1, F_PAD), F32)],
        compiler_params=_params(("arbitrary",)),
    )(dfcol, f2d, bf)


def _inproj_bwd(dq, dk, dv, qkv, df, dlx, dlg, x2d, dx2, g1, gq2, gk2, wcat):
    T = x2d.shape[0]
    tm = TOKEN_TILE

    def body(dq_ref, dk_ref, dv_ref, qkv_ref, df_ref, dlx_ref, dlg_ref, x_ref, dx2_ref, g1_ref, gq_ref, gk_ref, w_ref,
             gx_ref, dp_ref, h_ref, dg1_ref, dgq_ref, dgk_ref):
        @pl.when(pl.program_id(0) == 0)
        def _():
            dg1_ref[...] = jnp.zeros_like(dg1_ref)
            dgq_ref[...] = jnp.zeros_like(dgq_ref)
            dgk_ref[...] = jnp.zeros_like(dgk_ref)

        lo = _lo_mask()

        def head_norm_bwd(t, g2, dy):
            rr = lax.rsqrt(_half_sums(t * t, lo) * (1.0 / HEAD_DIM) + NORM_EPS)
            th = t * rr
            dth = dy * g2
            mm = _half_sums(dth * th, lo) * (1.0 / HEAD_DIM)
            return rr * (dth - th * mm), jnp.sum(dy * th, axis=0, keepdims=True)

        dgq = jnp.zeros((1, PAIR), F32)
        dgk = jnp.zeros((1, PAIR), F32)
        for p in range(N_PAIR):
            cq = slice(PAIR * p, PAIR * (p + 1))
            ck = slice(ATT_WIDTH + PAIR * p, ATT_WIDTH + PAIR * (p + 1))
            dqp, g_ = head_norm_bwd(qkv_ref[:, cq], gq_ref[...], dq_ref[:, cq] * QK_SCALE)
            dgq = dgq + g_
            dp_ref[:, cq] = dqp.astype(MXU_DTYPE)
            dkp, g_ = head_norm_bwd(qkv_ref[:, ck], gk_ref[...], dk_ref[:, cq])
            dgk = dgk + g_
            dp_ref[:, ck] = dkp.astype(MXU_DTYPE)
        dgq_ref[...] += dgq
        dgk_ref[...] += dgk
        f0 = 3 * ATT_WIDTH
        dp_ref[:, 2 * ATT_WIDTH:f0] = dv_ref[...].astype(MXU_DTYPE)
        dp_ref[:, f0:f0 + F_PAD] = df_ref[...].astype(MXU_DTYPE)
        dp_ref[:, f0 + F_PAD:f0 + F_PAD + LRU_WIDTH] = dlx_ref[...].astype(MXU_DTYPE)
        dp_ref[:, f0 + F_PAD + LRU_WIDTH:] = dlg_ref[...].astype(MXU_DTYPE)
        dh = jnp.dot(dp_ref[...], w_ref[...], preferred_element_type=F32)
        x = x_ref[...]
        r = lax.rsqrt(jnp.mean(x * x, axis=-1, keepdims=True) + NORM_EPS)
        xh = x * r
        h_ref[...] = (xh * g1_ref[...]).astype(MXU_DTYPE)
        dg1_ref[...] += jnp.sum(dh * xh, axis=0, keepdims=True)
        dxh = dh * g1_ref[...]
        gx_ref[...] = dx2_ref[...] + r * (dxh - xh * jnp.mean(dxh * xh, axis=-1, keepdims=True))

    row = lambda w: pl.BlockSpec((tm, w), lambda i: (i, 0))
    return pl.pallas_call(
        body, name="inproj_bwd", grid=(T // tm,),
        in_specs=[row(ATT_WIDTH), row(ATT_WIDTH), row(ATT_WIDTH), row(3 * ATT_WIDTH), row(F_PAD), row(LRU_WIDTH),
                  row(LRU_WIDTH), row(D_MODEL), row(D_MODEL), _const((1, D_MODEL)), _const((1, PAIR)), _const((1, PAIR)),
                  _const((N_CAT, D_MODEL))],
        out_specs=[row(D_MODEL), row(N_CAT), row(D_MODEL), _const((1, D_MODEL)), _const((1, PAIR)), _const((1, PAIR))],
        out_shape=[jax.ShapeDtypeStruct((T, D_MODEL), F32), jax.ShapeDtypeStruct((T, N_CAT), MXU_DTYPE),
                   jax.ShapeDtypeStruct((T, D_MODEL), MXU_DTYPE), jax.ShapeDtypeStruct((1, D_MODEL), F32),
                   jax.ShapeDtypeStruct((1, PAIR), F32), jax.ShapeDtypeStruct((1, PAIR), F32)],
        compiler_params=_params(("arbitrary",), VMEM_LARGE),
    )(dq, dk, dv, qkv, df, dlx, dlg, x2d, dx2, g1, gq2, gk2, wcat)


ELEMENTWISE_COLS = 256


def _sum_slabs(recv, name):
    _, rows, cols = recv.shape
    cb = ELEMENTWISE_COLS

    def body(r_ref, o_ref):
        part = [r_ref[s].astype(F32) for s in range(4)]
        o_ref[...] = ((part[0] + part[1]) + part[2]) + part[3]

    return pl.pallas_call(
        body, name=name, grid=(cols // cb,),
        in_specs=[pl.BlockSpec((4, rows, cb), lambda i: (0, 0, i))],
        out_specs=pl.BlockSpec((rows, cb), lambda i: (0, i)),
        out_shape=jax.ShapeDtypeStruct((rows, cols), F32),
        compiler_params=_params(("parallel",)),
    )(recv)


def _adamw_math(w, g, m, v):
    m = ADAM_B1 * m + (1.0 - ADAM_B1) * g
    v = ADAM_B2 * v + (1.0 - ADAM_B2) * (g * g)
    m_hat = m / (1.0 - ADAM_B1 ** ADAM_STEP)
    v_hat = v / (1.0 - ADAM_B2 ** ADAM_STEP)
    delta = -ADAM_LR * (m_hat / (jnp.sqrt(v_hat) + ADAM_EPS) + ADAM_WD * w)
    return delta, m, v


def _adamw_pair(mine, theirs, w, m, v, name):
    rows, cols = w.shape
    cb = ELEMENTWISE_COLS

    def body(a_ref, b_ref, w_ref, m_ref, v_ref, g_ref, d_ref, nm_ref, nv_ref):
        g = a_ref[...] + b_ref[...]
        g_ref[...] = g
        d_ref[...], nm_ref[...], nv_ref[...] = _adamw_math(w_ref[...], g, m_ref[...], v_ref[...])

    blk = pl.BlockSpec((rows, cb), lambda i: (0, i))
    return pl.pallas_call(
        body, name=name, grid=(cols // cb,),
        in_specs=[blk] * 5, out_specs=[blk] * 4,
        out_shape=[jax.ShapeDtypeStruct((rows, cols), F32)] * 4,
        compiler_params=_params(("parallel",)),
    )(mine, theirs, w, m, v)


def _sum_packs(recv):
    _, rows, cols = recv.shape

    def body(r_ref, o_ref):
        acc = r_ref[0]
        for d in range(1, 8):
            acc = acc + r_ref[d]
        o_ref[...] = acc

    return pl.pallas_call(
        body, name="sum_packs", grid=(1,),
        in_specs=[_const(recv.shape)], out_specs=_const((rows, cols)),
        out_shape=jax.ShapeDtypeStruct((rows, cols), F32),
        compiler_params=_params(("arbitrary",)),
    )(recv)


def _adamw_small(w, g, m, v):
    rows, cols = w.shape

    def body(w_ref, g_ref, m_ref, v_ref, d_ref, nm_ref, nv_ref):
        d_ref[...], nm_ref[...], nv_ref[...] = _adamw_math(w_ref[...], g_ref[...], m_ref[...], v_ref[...])

    blk = _const((rows, cols))
    return pl.pallas_call(
        body, name="adamw_small", grid=(1,),
        in_specs=[blk] * 4, out_specs=[blk] * 3,
        out_shape=[jax.ShapeDtypeStruct((rows, cols), F32)] * 3,
        compiler_params=_params(("arbitrary",)),
    )(w, g, m, v)


SMALL = ["norm1_g", "q_norm_g", "k_norm_g", "b_f", "conv_b", "w_a", "b_a", "w_x", "b_x", "lam",
         "attn_out_g", "lru_out_g", "norm2_g"]


def _pack(parts, rows=None):
    flat = jnp.concatenate([p.reshape(-1) for p in parts])
    n = flat.shape[0]
    total = -(-n // 1024) * 1024 if rows is None else rows * 128
    return jnp.pad(flat, (0, total - n)).reshape(-1, 128)


def _unpack(pack, shapes):
    flat = pack.reshape(-1)
    out, off = [], 0
    for s in shapes:
        n = math.prod(s)
        out.append(flat[off:off + n].reshape(s))
        off += n
    return out


def _cat_shards(g, pad_at=None, pad=0):
    _, rows, w = g.shape
    pieces = []
    for s in range(4):
        lo, hi = s * w, (s + 1) * w
        if pad_at is not None and lo < pad_at <= hi:
            pieces += [g[s][:, :pad_at - lo], jnp.zeros((rows, pad), g.dtype)]
            if pad_at < hi:
                pieces.append(g[s][:, pad_at - lo:])
        else:
            pieces.append(g[s])
    return jnp.concatenate(pieces, axis=1)


def _block_diag(w):
    eye = jnp.eye(LRU_BLOCKS, dtype=w.dtype)
    return (w[:, :, None, :] * eye[:, None, :, None]).reshape(LRU_WIDTH, LRU_WIDTH)


def _diag_blocks(m):
    m4 = m.reshape(LRU_BLOCKS, LRU_WIDTH // LRU_BLOCKS, LRU_BLOCKS, LRU_WIDTH // LRU_BLOCKS)
    return jnp.stack([m4[n, :, n, :] for n in range(LRU_BLOCKS)])


def kernel(x, norm1_g, w_in, q_norm_g, k_norm_g, b_f, conv_w, conv_b, w_a, b_a, w_x, b_x, lam, attn_out_g, lru_out_g, w_out, norm2_g, w_gate, w_up, w_down, loss_target, m_norm1_g, m_w_in, m_q_norm_g, m_k_norm_g, m_b_f, m_conv_w, m_conv_b, m_w_a, m_b_a, m_w_x, m_b_x, m_lam, m_attn_out_g, m_lru_out_g, m_w_out, m_norm2_g, m_w_gate, m_w_up, m_w_down, v_norm1_g, v_w_in, v_q_norm_g, v_k_norm_g, v_b_f, v_conv_w, v_conv_b, v_w_a, v_b_a, v_w_x, v_b_x, v_lam, v_attn_out_g, v_lru_out_g, v_w_out, v_norm2_g, v_w_gate, v_w_up, v_w_down):
    args = dict(locals())
    bl, seq, _ = x.shape
    T = bl * seq
    tq = min(ATT_TILE, seq)
    nq = seq // tq
    dff = w_gate.shape[2] * 4
    my_chip = 2 * lax.axis_index("x") + lax.axis_index("y")

    def transposed(name):
        return name.endswith(("w_in", "w_gate", "w_up"))

    def shard2d(name):
        return jnp.swapaxes(args[name], 1, 2)[0] if transposed(name) else args[name][0]

    g_in, g_cw = _gather_over_chips([shard2d("w_in").astype(MXU_DTYPE), conv_w[0]])
    later_shards = [shard2d(n).astype(MXU_DTYPE) for n in ("w_out", "w_gate", "w_up", "w_down")]
    f0 = 3 * ATT_WIDTH
    w_in_t = g_in.reshape(-1, D_MODEL)
    wcat = jnp.concatenate([w_in_t[:f0 + HEADS], jnp.zeros((F_PAD - HEADS, D_MODEL), w_in_t.dtype),
                            w_in_t[f0 + HEADS:]], axis=0)
    cw_full = _cat_shards(g_cw)
    wa_bd = _block_diag(w_a[0]).astype(MXU_DTYPE)
    wx_bd = _block_diag(w_x[0]).astype(MXU_DTYPE)
    gq2 = jnp.tile(q_norm_g, (1, 2))
    gk2 = jnp.tile(k_norm_g, (1, 2))
    bf_pad = jnp.pad(b_f, ((0, 0), (0, F_PAD - HEADS)))

    x2d = x.reshape(T, D_MODEL)
    target2d = loss_target.reshape(T, D_MODEL)

    qkv, qn, kn, vb, f2d, lx, lg = _inproj(x2d, norm1_g, wcat, gq2, gk2)
    fcol = _forget_cumsum(f2d, bf_pad, bl, seq)
    frow = jnp.transpose(fcol.reshape(bl, seq, F_PAD)[:, :, :HEADS], (0, 2, 1)).reshape(bl, N_PAIR, 2, seq)
    fstart = frow[:, :, :, ::tq].reshape(-1)
    att, att_x, lse, g_out, g_gate, g_up, g_down = _attn_fwd(qn, kn, vb, frow, fstart, bl, seq, later_shards)
    wout_full = g_out.reshape(D_MODEL, D_MODEL)
    wg_full, wu_full = g_gate.reshape(dff, D_MODEL), g_up.reshape(dff, D_MODEL)
    wd_full = g_down.reshape(dff, D_MODEL)
    h, rec = _lru_fwd(lx, lg, cw_full, conv_b, wa_bd, b_a, wx_bd, b_x, lam, bl, seq)
    x2 = _outproj(x2d, att, rec, attn_out_g, lru_out_g, wout_full)
    gt, up, dy, sq_err = _mlp_fwd(x2, norm2_g, wg_full, wu_full, wd_full, target2d)
    loss = lax.psum(sq_err[0, 0] * (0.5 / D_MODEL), ("x", "y", "c"))

    dx2, dx2b, dgtb, dupb, actb, h2b, dyb, dg2 = _mlp_bwd(dy, x2, gt, up, norm2_g, wg_full, wu_full, wd_full)
    dw_down = _matmul_tn(actb, dyb, 512, "dw_down")
    dw_gate = _matmul_tn(dgtb, h2b, 512, "dw_gate")
    dw_up = _matmul_tn(dupb, h2b, 512, "dw_up")
    dattb, delta, drec, mixb, dga, dgr = _outproj_bwd(dx2b, att, att_x, rec, attn_out_g, lru_out_g, wout_full)
    dw_out = _matmul_tn(mixb, dx2b, D_MODEL, "dw_out")
    dlx, dlg, dwa_bd, dwx_bd, lru_small = _lru_bwd(drec, lg, h, lx, cw_full, conv_b, wa_bd, b_a, wx_bd, b_x, lam, bl, seq)
    early_slabs = [dw_out.reshape(4, D_MODEL // 4, D_MODEL), dw_gate.reshape(4, dff // 4, D_MODEL),
                   dw_up.reshape(4, dff // 4, D_MODEL), dw_down.reshape(4, dff // 4, D_MODEL)]
    dq, dk, dv, dfrow, *recv_early = _attn_bwd(qn, kn, vb, dattb, lse, delta, frow, fstart, bl, seq, early_slabs)
    dfcol = jnp.pad(jnp.transpose(dfrow.reshape(bl, HEADS, seq), (0, 2, 1)), ((0, 0), (0, 0), (0, F_PAD - HEADS)))
    df, dbf = _forget_bwd(dfcol.reshape(T, F_PAD), f2d, bf_pad, bl, seq)
    grad_x, dprojb, h1b, dg1, dgq, dgk = _inproj_bwd(dq, dk, dv, qkv, df, dlx, dlg, x2d, dx2, norm1_g, gq2, gk2, wcat)
    dwcat = _matmul_tn(dprojb, h1b, 512, "dw_in")

    small_grads = {
        "norm1_g": dg1, "q_norm_g": dgq[:, :HEAD_DIM] + dgq[:, HEAD_DIM:], "k_norm_g": dgk[:, :HEAD_DIM] + dgk[:, HEAD_DIM:],
        "b_f": dbf[:, :HEADS], "conv_b": lru_small[3:4], "w_a": _diag_blocks(dwa_bd)[None], "b_a": lru_small[0:1],
        "w_x": _diag_blocks(dwx_bd)[None], "b_x": lru_small[1:2], "lam": lru_small[2:3],
        "attn_out_g": dga, "lru_out_g": dgr, "norm2_g": dg2}
    pack = _pack([small_grads[n] for n in SMALL] + [lru_small[4:8]])
    dw_in_slabs = jnp.concatenate([dwcat[:f0 + HEADS], dwcat[f0 + F_PAD:]], axis=0).astype(jnp.bfloat16).reshape(
        4, -1, D_MODEL)
    recv_in, recv_pack = _exchange_grads([dw_in_slabs], pack)
    recv = [recv_in] + recv_early
    big = ["w_in", "w_out", "w_gate", "w_up", "w_down"]
    part = [_sum_slabs(r, "sum_" + n) for r, n in zip(recv, big)]
    theirs = _swap_with_sibling(part)
    out = {}
    for n, a, b_ in zip(big, part, theirs):
        res = _adamw_pair(a, b_, shard2d(n), shard2d("m_" + n), shard2d("v_" + n), "adamw_" + n)
        out[n] = tuple(jnp.swapaxes(r[None], 1, 2) if transposed(n) else r[None] for r in res)

    small_shapes = [args[n].shape for n in SMALL]
    red = _sum_packs(recv_pack)
    g_small = _unpack(red, small_shapes + [(CONV_WIDTH, LRU_WIDTH)])
    g_cw_mine = lax.dynamic_slice_in_dim(g_small[-1], my_chip * (LRU_WIDTH // 4), LRU_WIDTH // 4, axis=1)[None]
    g_list = g_small[:-1] + [g_cw_mine]
    names = SMALL + ["conv_w"]
    rows = pack.shape[0]
    d_p, m_p, v_p = _adamw_small(_pack([args[n] for n in names], rows), _pack(g_list, rows),
                                 _pack([args["m_" + n] for n in names], rows), _pack([args["v_" + n] for n in names], rows))
    shapes = [args[n].shape for n in names]
    for n, g, d, nm, nv in zip(names, g_list, _unpack(d_p, shapes), _unpack(m_p, shapes), _unpack(v_p, shapes)):
        out[n] = (g, d, nm, nv)

    order = ["norm1_g", "w_in", "q_norm_g", "k_norm_g", "b_f", "conv_w", "conv_b", "w_a", "b_a", "w_x", "b_x", "lam",
             "attn_out_g", "lru_out_g", "w_out", "norm2_g", "w_gate", "w_up", "w_down"]
    return (loss, grad_x.reshape(bl, seq, D_MODEL), *[out[n][0] for n in order], *[out[n][1] for n in order],
            *[out[n][2] for n in order], *[out[n][3] for n in order])
```

```python
import functools
import math

import jax
import jax.numpy as jnp
from jax import lax
from jax.experimental import pallas as pl
from jax.experimental.pallas import tpu as pltpu

F32 = jnp.float32
MXU_DTYPE = jnp.bfloat16
MESH = pl.DeviceIdType.MESH

D_MODEL = 1024
ATT_WIDTH = 512
LRU_WIDTH = 512
HEADS = 8
HEAD_DIM = 64
PAIR = 2 * HEAD_DIM
N_PAIR = HEADS // 2
LRU_BLOCKS = 8
CONV_WIDTH = 4
LRU_C = 8.0
NORM_EPS = 1e-6
QK_SCALE = 1.0 / math.sqrt(HEAD_DIM)
F_PAD = 128
N_CAT = 3 * ATT_WIDTH + F_PAD + 2 * LRU_WIDTH
NEG = -1e30

ADAM_LR, ADAM_B1, ADAM_B2, ADAM_EPS, ADAM_WD, ADAM_STEP = 0.001, 0.9, 0.999, 1e-08, 0.01, 10

TOKEN_TILE = 256
ATT_TILE = 512
LRU_TILE = 256
VMEM_SMALL = 32 * 1024 * 1024
VMEM_LARGE = 56 * 1024 * 1024


def _params(sem, vmem=VMEM_SMALL):
    return pltpu.CompilerParams(dimension_semantics=sem, vmem_limit_bytes=vmem)


def _const(shape):
    nd = len(shape)
    return pl.BlockSpec(shape, lambda *_: (0,) * nd)


def _sigmoid(x):
    return 1.0 / (1.0 + jnp.exp(-x))


def _nt(a, b):
    return lax.dot_general(a, b, (((1,), (1,)), ((), ())), preferred_element_type=F32)


def _tn(a, b):
    return lax.dot_general(a, b, (((0,), (0,)), ((), ())), preferred_element_type=F32)


def _half_sums(t, lo):
    s_lo = jnp.sum(jnp.where(lo, t, 0.0), axis=-1, keepdims=True)
    s_hi = jnp.sum(jnp.where(lo, 0.0, t), axis=-1, keepdims=True)
    return jnp.where(lo, s_lo, s_hi)


def _lo_mask():
    return lax.broadcasted_iota(jnp.int32, (1, PAIR), 1) < HEAD_DIM


def _other_chips(x, y):
    return [(1 - x, y), (x, 1 - y), (1 - x, 1 - y)]


def _chip_copies(ins, outs, send_sems, recv_sems, loc_sems, scatter):
    x, y, c = lax.axis_index("x"), lax.axis_index("y"), lax.axis_index("c")
    me = 2 * x + y
    copies = []
    for w in range(len(ins)):
        copies.append(pltpu.make_async_copy(ins[w].at[me] if scatter else ins[w], outs[w].at[me], loc_sems.at[w]))
        for k, (cx, cy) in enumerate(_other_chips(x, y)):
            copies.append(pltpu.make_async_remote_copy(
                src_ref=ins[w].at[2 * cx + cy] if scatter else ins[w], dst_ref=outs[w].at[me],
                send_sem=send_sems.at[3 * w + k], recv_sem=recv_sems.at[3 * w + k],
                device_id=(cx, cy, c), device_id_type=MESH))
    return copies


def _chip_sems(n):
    return [pltpu.SemaphoreType.DMA((3 * n,)), pltpu.SemaphoreType.DMA((3 * n,)), pltpu.SemaphoreType.DMA((n,))]


def _gather_over_chips(shards):
    n = len(shards)

    def body(*refs):
        copies = _chip_copies(refs[:n], refs[n:2 * n], *refs[2 * n:], scatter=False)
        for cp in copies:
            cp.start()
        for cp in copies:
            cp.wait()

    return pl.pallas_call(
        body, name="gather_weights",
        out_shape=[jax.ShapeDtypeStruct((4,) + s.shape, s.dtype) for s in shards],
        in_specs=[pl.BlockSpec(memory_space=pl.ANY)] * n,
        out_specs=[pl.BlockSpec(memory_space=pl.ANY)] * n,
        scratch_shapes=_chip_sems(n),
    )(*shards)


def _exchange_grads(slabs, packs):
    n, npk = len(slabs), len(packs)

    def body(*refs):
        ins, pack_in = refs[:n], refs[n:n + npk]
        outs, pack_out = refs[n + npk:2 * n + npk], refs[2 * n + npk:2 * (n + npk)]
        send_sems, recv_sems, loc_sems, psend, precv = refs[2 * (n + npk):]
        x, y, c = lax.axis_index("x"), lax.axis_index("y"), lax.axis_index("c")
        dev = 4 * x + 2 * y + c
        copies = _chip_copies(ins, outs, send_sems, recv_sems, loc_sems, scatter=True)
        for j in range(npk):
            copies.append(pltpu.make_async_copy(pack_in[j], pack_out[j].at[dev], loc_sems.at[n + j]))
            for k in range(1, 8):
                fx, fy, fc = (k >> 2) & 1, (k >> 1) & 1, k & 1
                tx = (1 - x) if fx else x
                ty = (1 - y) if fy else y
                tc = (1 - c) if fc else c
                copies.append(pltpu.make_async_remote_copy(
                    src_ref=pack_in[j], dst_ref=pack_out[j].at[dev],
                    send_sem=psend.at[7 * j + k - 1], recv_sem=precv.at[7 * j + k - 1],
                    device_id=(tx, ty, tc), device_id_type=MESH))
        for cp in copies:
            cp.start()
        for cp in copies:
            cp.wait()

    return pl.pallas_call(
        body, name="exchange_grads",
        out_shape=[jax.ShapeDtypeStruct(s.shape, s.dtype) for s in slabs]
        + [jax.ShapeDtypeStruct((8,) + p.shape, p.dtype) for p in packs],
        in_specs=[pl.BlockSpec(memory_space=pl.ANY)] * (n + npk),
        out_specs=[pl.BlockSpec(memory_space=pl.ANY)] * (n + npk),
        scratch_shapes=[pltpu.SemaphoreType.DMA((3 * n,)), pltpu.SemaphoreType.DMA((3 * n,)),
                        pltpu.SemaphoreType.DMA((n + npk,)),
                        pltpu.SemaphoreType.DMA((7 * npk,)), pltpu.SemaphoreType.DMA((7 * npk,))],
    )(*slabs, *packs)


def _swap_with_sibling(arrs):
    n = len(arrs)

    def body(*refs):
        ins, outs = refs[:n], refs[n:2 * n]
        send_sems, recv_sems = refs[2 * n:]
        x, y, c = lax.axis_index("x"), lax.axis_index("y"), lax.axis_index("c")
        copies = []
        for w in range(n):
            cp = pltpu.make_async_remote_copy(
                src_ref=ins[w], dst_ref=outs[w], send_sem=send_sems.at[w], recv_sem=recv_sems.at[w],
                device_id=(x, y, 1 - c), device_id_type=MESH)
            cp.start()
            copies.append(cp)
        for cp in copies:
            cp.wait()

    return pl.pallas_call(
        body, name="swap_sibling",
        out_shape=[jax.ShapeDtypeStruct(a.shape, a.dtype) for a in arrs],
        in_specs=[pl.BlockSpec(memory_space=pl.ANY)] * n,
        out_specs=[pl.BlockSpec(memory_space=pl.ANY)] * n,
        scratch_shapes=[pltpu.SemaphoreType.DMA((n,)), pltpu.SemaphoreType.DMA((n,))],
    )(*arrs)


def _head_norm(t, g2, lo):
    rr = lax.rsqrt(_half_sums(t * t, lo) * (1.0 / HEAD_DIM) + NORM_EPS)
    return t * rr * g2


def _inproj(x2d, g1, wcat, gq2, gk2):
    T = x2d.shape[0]
    tm = TOKEN_TILE

    def body(x_ref, g1_ref, w_ref, gq_ref, gk_ref, qkv_ref, qn_ref, kn_ref, vb_ref, f_ref, lx_ref, lg_ref):
        x = x_ref[...]
        r = lax.rsqrt(jnp.mean(x * x, axis=-1, keepdims=True) + NORM_EPS)
        h = (x * r * g1_ref[...]).astype(MXU_DTYPE)
        proj = _nt(h, w_ref[...])
        qkv_ref[...] = proj[:, :3 * ATT_WIDTH]
        lo = _lo_mask()
        for p in range(N_PAIR):
            cols = slice(PAIR * p, PAIR * (p + 1))
            q = proj[:, PAIR * p:PAIR * (p + 1)]
            k = proj[:, ATT_WIDTH + PAIR * p:ATT_WIDTH + PAIR * (p + 1)]
            qn_ref[:, cols] = (_head_norm(q, gq_ref[...], lo) * QK_SCALE).astype(MXU_DTYPE)
            kn_ref[:, cols] = _head_norm(k, gk_ref[...], lo).astype(MXU_DTYPE)
        vb_ref[...] = proj[:, 2 * ATT_WIDTH:3 * ATT_WIDTH].astype(MXU_DTYPE)
        f0 = 3 * ATT_WIDTH
        f_ref[...] = proj[:, f0:f0 + F_PAD]
        lx_ref[...] = proj[:, f0 + F_PAD:f0 + F_PAD + LRU_WIDTH]
        lg_ref[...] = proj[:, f0 + F_PAD + LRU_WIDTH:]

    row = lambda w: pl.BlockSpec((tm, w), lambda i: (i, 0))
    return pl.pallas_call(
        body, name="inproj", grid=(T // tm,),
        in_specs=[row(D_MODEL), _const((1, D_MODEL)), _const((N_CAT, D_MODEL)), _const((1, PAIR)), _const((1, PAIR))],
        out_specs=[row(3 * ATT_WIDTH), row(ATT_WIDTH), row(ATT_WIDTH), row(ATT_WIDTH), row(F_PAD),
                   row(LRU_WIDTH), row(LRU_WIDTH)],
        out_shape=[jax.ShapeDtypeStruct((T, 3 * ATT_WIDTH), F32),
                   jax.ShapeDtypeStruct((T, ATT_WIDTH), MXU_DTYPE), jax.ShapeDtypeStruct((T, ATT_WIDTH), MXU_DTYPE),
                   jax.ShapeDtypeStruct((T, ATT_WIDTH), MXU_DTYPE), jax.ShapeDtypeStruct((T, F_PAD), F32),
                   jax.ShapeDtypeStruct((T, LRU_WIDTH), F32), jax.ShapeDtypeStruct((T, LRU_WIDTH), F32)],
        compiler_params=_params(("parallel",), VMEM_LARGE),
    )(x2d, g1, wcat, gq2, gk2)


def _forget_cumsum(f2d, bf, bl, seq):
    def body(z_ref, b_ref, o_ref):
        z = z_ref[...] + b_ref[...]
        lf = jnp.minimum(z, 0.0) - jnp.log(1.0 + jnp.exp(-jnp.abs(z)))
        row = lax.broadcasted_iota(jnp.int32, (seq, F_PAD), 0)
        k = 1
        while k < seq:
            lf = lf + jnp.where(row >= k, pltpu.roll(lf, k, 0), 0.0)
            k *= 2
        o_ref[...] = lf

    return pl.pallas_call(
        body, name="forget_cumsum", grid=(bl,),
        in_specs=[pl.BlockSpec((seq, F_PAD), lambda b: (b, 0)), _const((1, F_PAD))],
        out_specs=pl.BlockSpec((seq, F_PAD), lambda b: (b, 0)),
        out_shape=jax.ShapeDtypeStruct(f2d.shape, F32),
        compiler_params=_params(("parallel",)),
    )(f2d, bf)


def _attn_fwd(qn, kn, vb, frow, fstart, bl, seq, shards):
    tq = min(ATT_TILE, seq)
    nq = seq // tq
    T = bl * seq
    n = len(shards)

    def body(fs_ref, q_ref, k_ref, v_ref, fr_ref, *rest):
        g_in, (o_ref, ox_ref, lse_ref), g_out, sems = rest[:n], rest[n:n + 3], rest[n + 3:2 * n + 3], rest[2 * n + 3:]
        b, p, i = pl.program_id(0), pl.program_id(1), pl.program_id(2)
        copies = _chip_copies(g_in, g_out, *sems, scatter=False)

        @pl.when((b == 0) & (p == 0) & (i == 0))
        def _():
            for cp in copies:
                cp.start()

        lane = lax.broadcasted_iota(jnp.int32, (1, PAIR), 1)
        rows = lax.broadcasted_iota(jnp.int32, (tq, tq), 0)
        cols = lax.broadcasted_iota(jnp.int32, (tq, tq), 1)
        causal = cols <= rows
        q = q_ref[...]
        hms = [(lane >= HEAD_DIM * hh) & (lane < HEAD_DIM * (hh + 1)) for hh in range(2)]
        qhs = [jnp.where(hm, q, jnp.zeros_like(q)) for hm in hms]
        shifts = [fs_ref[((b * N_PAIR + p) * 2 + hh) * nq + i] for hh in range(2)]

        def block(j, carry, masked):
            start = pl.multiple_of(j * tq, tq)
            k = k_ref[pl.ds(start, tq), :]
            v = v_ref[pl.ds(start, tq), :]
            new = []
            for hh in range(2):
                m, l, acc, acc_lo = carry[hh]
                s = lax.dot_general(qhs[hh], k, (((1,), (1,)), ((), ())), preferred_element_type=F32)
                s = s - (fr_ref[0, 0, hh:hh + 1, pl.ds(start, tq)] - shifts[hh])
                if masked:
                    s = jnp.where(causal, s, NEG)
                m_new = jnp.maximum(m, jnp.max(s, axis=-1, keepdims=True))
                alpha = jnp.exp(m - m_new)
                pe = jnp.exp(s - m_new)
                l = alpha * l + jnp.sum(pe, axis=-1, keepdims=True)
                vh = jnp.where(hms[hh], v, jnp.zeros_like(v))
                pb = pe.astype(MXU_DTYPE)
                p_lo = (pe - pb.astype(F32)).astype(MXU_DTYPE)
                acc = alpha * acc + jnp.dot(pb, vh, preferred_element_type=F32)
                acc_lo = alpha * acc_lo + jnp.dot(p_lo, vh, preferred_element_type=F32)
                new.append((m_new, l, acc, acc_lo))
            return tuple(new)

        init = (jnp.full((tq, 1), NEG, F32), jnp.zeros((tq, 1), F32), jnp.zeros((tq, PAIR), F32),
                jnp.zeros((tq, PAIR), F32))
        carry = lax.fori_loop(0, i, functools.partial(block, masked=False), (init, init))
        carry = block(i, carry, True)
        out = jnp.zeros((tq, PAIR), F32)
        out_x = jnp.zeros((tq, PAIR), F32)
        lse = jnp.zeros((tq, PAIR), F32)
        for hh in range(2):
            m, l, acc, acc_lo = carry[hh]
            inv_l = 1.0 / l
            out = out + acc * inv_l
            out_x = out_x + (acc + acc_lo) * inv_l
            lse = jnp.where(hms[hh], m + jnp.log(l), lse)
        o_ref[...] = out
        ox_ref[...] = out_x
        lse_ref[...] = lse

        @pl.when((b == bl - 1) & (p == N_PAIR - 1) & (i == nq - 1))
        def _():
            for cp in copies:
                cp.wait()

    blk = pl.BlockSpec((tq, PAIR), lambda b, p, i: (b * nq + i, p))
    full = pl.BlockSpec((seq, PAIR), lambda b, p, i: (b, p))
    return pl.pallas_call(
        body, name="attn_fwd", grid=(bl, N_PAIR, nq),
        in_specs=[pl.BlockSpec(memory_space=pltpu.SMEM), blk, full, full,
                  pl.BlockSpec((1, 1, 2, seq), lambda b, p, i: (b, p, 0, 0))] + [pl.BlockSpec(memory_space=pl.ANY)] * n,
        out_specs=[blk, blk, blk] + [pl.BlockSpec(memory_space=pl.ANY)] * n,
        out_shape=[jax.ShapeDtypeStruct((T, ATT_WIDTH), F32)] * 3
        + [jax.ShapeDtypeStruct((4,) + s.shape, s.dtype) for s in shards],
        scratch_shapes=_chip_sems(n),
        compiler_params=_params(("arbitrary", "arbitrary", "arbitrary")),
    )(fstart, qn, kn, vb, frow, *shards)


def _conv_taps(lx, prev8, cw, cb):
    xs = jnp.concatenate([prev8, lx], axis=0)
    shifted = [lx] + [pltpu.roll(xs, k, 0)[8:] for k in range(1, CONV_WIDTH)]
    xc = cb + cw[CONV_WIDTH - 1:CONV_WIDTH] * lx
    for k in range(1, CONV_WIDTH):
        xc = xc + cw[CONV_WIDTH - 1 - k:CONV_WIDTH - k] * shifted[k]
    return xc, shifted


def _lru_gates(xc, wa, ba, wx, bx, lam):
    xb = xc.astype(MXU_DTYPE)
    r = _sigmoid(jnp.dot(xb, wa, preferred_element_type=F32) + ba)
    ig = _sigmoid(jnp.dot(xb, wx, preferred_element_type=F32) + bx)
    sp = jnp.maximum(-lam, 0.0) + jnp.log(1.0 + jnp.exp(-jnp.abs(lam)))
    log_a = -LRU_C * r * sp
    a = jnp.exp(log_a)
    th = jnp.tanh(log_a)
    mult = jnp.sqrt(-2.0 * th / (1.0 - th))
    return r, ig, sp, a, mult


def _gelu_parts(x):
    c0 = math.sqrt(2.0 / math.pi)
    t = jnp.tanh(c0 * (x + 0.044715 * x * x * x))
    g = 0.5 * x * (1.0 + t)
    dg = 0.5 * (1.0 + t) + 0.5 * x * (1.0 - t * t) * c0 * (1.0 + 3.0 * 0.044715 * x * x)
    return g, dg


def _lru_fwd(lx, lg, cw, cb, wa, ba, wx, bx, lam, bl, seq):
    tc = min(LRU_TILE, seq)
    nc = seq // tc
    T = bl * seq

    def body(lx_ref, lxp_ref, lg_ref, cw_ref, cb_ref, wa_ref, ba_ref, wx_ref, bx_ref, lam_ref,
             h_ref, rec_ref, hc_ref):
        i = pl.program_id(1)

        @pl.when(i == 0)
        def _():
            hc_ref[...] = jnp.zeros_like(hc_ref)

        lxv = lx_ref[...]
        prev8 = jnp.where(i > 0, lxp_ref[...], 0.0)
        xc, _ = _conv_taps(lxv, prev8, cw_ref[...], cb_ref[...])
        _, ig, _, a, mult = _lru_gates(xc, wa_ref[...], ba_ref[...], wx_ref[...], bx_ref[...], lam_ref[...])
        u = mult * (ig * xc)
        sub = lax.broadcasted_iota(jnp.int32, (tc, LRU_WIDTH), 0) & 7
        A, B = a, u
        for k in (1, 2, 4):
            a_s = jnp.where(sub >= k, pltpu.roll(A, k, 0), 1.0)
            b_s = jnp.where(sub >= k, pltpu.roll(B, k, 0), 0.0)
            B = A * b_s + B
            A = A * a_s
        carry = hc_ref[0:1, :]
        groups = []
        for g in range(tc // 8):
            hg = A[8 * g:8 * (g + 1)] * carry + B[8 * g:8 * (g + 1)]
            groups.append(hg)
            carry = hg[7:8]
        h = jnp.concatenate(groups, axis=0)
        hc_ref[0:1, :] = carry
        h_ref[...] = h
        g, _ = _gelu_parts(lg_ref[...])
        rec_ref[...] = h * g

    tile = pl.BlockSpec((tc, LRU_WIDTH), lambda b, i: (b * nc + i, 0))
    prev = pl.BlockSpec((8, LRU_WIDTH), lambda b, i: (jnp.maximum((b * seq + i * tc) // 8 - 1, 0), 0))
    vec = _const((1, LRU_WIDTH))
    mat = _const((LRU_WIDTH, LRU_WIDTH))
    return pl.pallas_call(
        body, name="lru_fwd", grid=(bl, nc),
        in_specs=[tile, prev, tile, _const((CONV_WIDTH, LRU_WIDTH)), vec, mat, vec, mat, vec, vec],
        out_specs=[tile, tile],
        out_shape=[jax.ShapeDtypeStruct((T, LRU_WIDTH), F32), jax.ShapeDtypeStruct((T, LRU_WIDTH), F32)],
        scratch_shapes=[pltpu.VMEM((8, LRU_WIDTH), F32)],
        compiler_params=_params(("arbitrary", "arbitrary")),
    )(lx, lx, lg, cw, cb, wa, ba, wx, bx, lam)


def _outproj(x2d, att, rec, ga, gr, wout):
    T = x2d.shape[0]
    tm = TOKEN_TILE

    def body(x_ref, a_ref, r_ref, ga_ref, gr_ref, w_ref, o_ref):
        a = a_ref[...]
        rc = r_ref[...]
        na = a * lax.rsqrt(jnp.mean(a * a, axis=-1, keepdims=True) + NORM_EPS) * ga_ref[...]
        nr = rc * lax.rsqrt(jnp.mean(rc * rc, axis=-1, keepdims=True) + NORM_EPS) * gr_ref[...]
        o_ref[...] = (x_ref[...]
                      + jnp.dot(na.astype(MXU_DTYPE), w_ref[:ATT_WIDTH, :], preferred_element_type=F32)
                      + jnp.dot(nr.astype(MXU_DTYPE), w_ref[ATT_WIDTH:, :], preferred_element_type=F32))

    row = lambda w: pl.BlockSpec((tm, w), lambda i: (i, 0))
    return pl.pallas_call(
        body, name="outproj", grid=(T // tm,),
        in_specs=[row(D_MODEL), row(ATT_WIDTH), row(LRU_WIDTH), _const((1, ATT_WIDTH)), _const((1, LRU_WIDTH)),
                  _const((D_MODEL, D_MODEL))],
        out_specs=row(D_MODEL),
        out_shape=jax.ShapeDtypeStruct((T, D_MODEL), F32),
        compiler_params=_params(("parallel",)),
    )(x2d, att, rec, ga, gr, wout)


def _mlp_fwd(x2, g2, wg, wu, wd, target):
    T = x2.shape[0]
    tm = TOKEN_TILE
    dff = wg.shape[0]

    def body(x_ref, g_ref, wg_ref, wu_ref, wd_ref, t_ref, gt_ref, up_ref, dy_ref, loss_ref):
        @pl.when(pl.program_id(0) == 0)
        def _():
            loss_ref[...] = jnp.zeros_like(loss_ref)

        x = x_ref[...]
        r = lax.rsqrt(jnp.mean(x * x, axis=-1, keepdims=True) + NORM_EPS)
        h = (x * r * g_ref[...]).astype(MXU_DTYPE)
        gt = _nt(h, wg_ref[...])
        up = _nt(h, wu_ref[...])
        gt_ref[...] = gt
        up_ref[...] = up
        act = (gt * _sigmoid(gt) * up).astype(MXU_DTYPE)
        y = x + jnp.dot(act, wd_ref[...], preferred_element_type=F32)
        e = y - t_ref[...]
        dy_ref[...] = e * (1.0 / D_MODEL)
        loss_ref[...] += jnp.sum(e * e)

    row = lambda w: pl.BlockSpec((tm, w), lambda i: (i, 0))
    return pl.pallas_call(
        body, name="mlp_fwd", grid=(T // tm,),
        in_specs=[row(D_MODEL), _const((1, D_MODEL)), _const((dff, D_MODEL)), _const((dff, D_MODEL)),
                  _const((dff, D_MODEL)), row(D_MODEL)],
        out_specs=[row(dff), row(dff), row(D_MODEL), _const((8, 128))],
        out_shape=[jax.ShapeDtypeStruct((T, dff), F32), jax.ShapeDtypeStruct((T, dff), F32),
                   jax.ShapeDtypeStruct((T, D_MODEL), F32), jax.ShapeDtypeStruct((8, 128), F32)],
        compiler_params=_params(("arbitrary",), VMEM_LARGE),
    )(x2, g2, wg, wu, wd, target)


def _mlp_bwd(dy, x2, gt, up, g2, wg, wu, wd):
    T = x2.shape[0]
    tm = TOKEN_TILE
    dff = wg.shape[0]

    def body(dy_ref, x_ref, gt_ref, up_ref, g_ref, wg_ref, wu_ref, wd_ref,
             dx_ref, dxb_ref, dgt_ref, dup_ref, act_ref, h_ref, dyb_ref, dg_ref):
        @pl.when(pl.program_id(0) == 0)
        def _():
            dg_ref[...] = jnp.zeros_like(dg_ref)

        dy_v = dy_ref[...]
        dyb = dy_v.astype(MXU_DTYPE)
        dyb_ref[...] = dyb
        x = x_ref[...]
        r = lax.rsqrt(jnp.mean(x * x, axis=-1, keepdims=True) + NORM_EPS)
        xh = x * r
        h_ref[...] = (xh * g_ref[...]).astype(MXU_DTYPE)
        gt_v = gt_ref[...]
        up_v = up_ref[...]
        sg = _sigmoid(gt_v)
        silu = gt_v * sg
        act_ref[...] = (silu * up_v).astype(MXU_DTYPE)
        dact = _nt(dyb, wd_ref[...])
        dup = (dact * silu).astype(MXU_DTYPE)
        dgt = (dact * up_v * (sg * (1.0 + gt_v * (1.0 - sg)))).astype(MXU_DTYPE)
        dup_ref[...] = dup
        dgt_ref[...] = dgt
        dh = (jnp.dot(dgt, wg_ref[...], preferred_element_type=F32)
              + jnp.dot(dup, wu_ref[...], preferred_element_type=F32))
        dg_ref[...] += jnp.sum(dh * xh, axis=0, keepdims=True)
        dxh = dh * g_ref[...]
        dx = dy_v + r * (dxh - xh * jnp.mean(dxh * xh, axis=-1, keepdims=True))
        dx_ref[...] = dx
        dxb_ref[...] = dx.astype(MXU_DTYPE)

    row = lambda w: pl.BlockSpec((tm, w), lambda i: (i, 0))
    return pl.pallas_call(
        body, name="mlp_bwd", grid=(T // tm,),
        in_specs=[row(D_MODEL), row(D_MODEL), row(dff), row(dff), _const((1, D_MODEL)),
                  _const((dff, D_MODEL)), _const((dff, D_MODEL)), _const((dff, D_MODEL))],
        out_specs=[row(D_MODEL), row(D_MODEL), row(dff), row(dff), row(dff), row(D_MODEL), row(D_MODEL),
                   _const((1, D_MODEL))],
        out_shape=[jax.ShapeDtypeStruct((T, D_MODEL), F32), jax.ShapeDtypeStruct((T, D_MODEL), MXU_DTYPE),
                   jax.ShapeDtypeStruct((T, dff), MXU_DTYPE), jax.ShapeDtypeStruct((T, dff), MXU_DTYPE),
                   jax.ShapeDtypeStruct((T, dff), MXU_DTYPE), jax.ShapeDtypeStruct((T, D_MODEL), MXU_DTYPE),
                   jax.ShapeDtypeStruct((T, D_MODEL), MXU_DTYPE), jax.ShapeDtypeStruct((1, D_MODEL), F32)],
        compiler_params=_params(("arbitrary",), VMEM_LARGE),
    )(dy, x2, gt, up, g2, wg, wu, wd)


def _matmul_tn(a, b, tn, name):
    T, K = a.shape
    N = b.shape[1]
    tt = min(512, T)

    def body(a_ref, b_ref, o_ref):
        @pl.when(pl.program_id(1) == 0)
        def _():
            o_ref[...] = jnp.zeros_like(o_ref)

        o_ref[...] += _tn(a_ref[...], b_ref[...])

    return pl.pallas_call(
        body, name=name, grid=(N // tn, T // tt),
        in_specs=[pl.BlockSpec((tt, K), lambda n, t: (t, 0)), pl.BlockSpec((tt, tn), lambda n, t: (t, n))],
        out_specs=pl.BlockSpec((K, tn), lambda n, t: (0, n)),
        out_shape=jax.ShapeDtypeStruct((K, N), F32),
        compiler_params=_params(("parallel", "arbitrary"), VMEM_LARGE),
    )(a, b)


def _outproj_bwd(dx2b, att, att_x, rec, ga, gr, wout):
    T = att.shape[0]
    tm = TOKEN_TILE

    def body(dx_ref, a_ref, ax_ref, r_ref, ga_ref, gr_ref, w_ref, datt_ref, delta_ref, drec_ref, mix_ref, dga_ref, dgr_ref):
        @pl.when(pl.program_id(0) == 0)
        def _():
            dga_ref[...] = jnp.zeros_like(dga_ref)
            dgr_ref[...] = jnp.zeros_like(dgr_ref)

        dmix = _nt(dx_ref[...], w_ref[...])

        def norm_bwd(v, g, dn):
            rr = lax.rsqrt(jnp.mean(v * v, axis=-1, keepdims=True) + NORM_EPS)
            vh = v * rr
            dvh = dn * g
            dv = rr * (dvh - vh * jnp.mean(dvh * vh, axis=-1, keepdims=True))
            return vh, dv, jnp.sum(dn * vh, axis=0, keepdims=True)

        a = a_ref[...]
        ah, datt, dga = norm_bwd(a, ga_ref[...], dmix[:, :ATT_WIDTH])
        rh, drec, dgr = norm_bwd(r_ref[...], gr_ref[...], dmix[:, ATT_WIDTH:])
        dga_ref[...] += dga
        dgr_ref[...] += dgr
        mix_ref[:, :ATT_WIDTH] = (ah * ga_ref[...]).astype(MXU_DTYPE)
        mix_ref[:, ATT_WIDTH:] = (rh * gr_ref[...]).astype(MXU_DTYPE)
        dattb = datt.astype(MXU_DTYPE)
        datt_ref[...] = dattb
        drec_ref[...] = drec
        lo = _lo_mask()
        prod = dattb.astype(F32) * ax_ref[...]
        for p in range(N_PAIR):
            delta_ref[:, PAIR * p:PAIR * (p + 1)] = _half_sums(prod[:, PAIR * p:PAIR * (p + 1)], lo)

    row = lambda w: pl.BlockSpec((tm, w), lambda i: (i, 0))
    return pl.pallas_call(
        body, name="outproj_bwd", grid=(T // tm,),
        in_specs=[row(D_MODEL), row(ATT_WIDTH), row(ATT_WIDTH), row(LRU_WIDTH), _const((1, ATT_WIDTH)), _const((1, LRU_WIDTH)),
                  _const((D_MODEL, D_MODEL))],
        out_specs=[row(ATT_WIDTH), row(ATT_WIDTH), row(LRU_WIDTH), row(D_MODEL),
                   _const((1, ATT_WIDTH)), _const((1, LRU_WIDTH))],
        out_shape=[jax.ShapeDtypeStruct((T, ATT_WIDTH), MXU_DTYPE), jax.ShapeDtypeStruct((T, ATT_WIDTH), F32),
                   jax.ShapeDtypeStruct((T, LRU_WIDTH), F32), jax.ShapeDtypeStruct((T, D_MODEL), MXU_DTYPE),
                   jax.ShapeDtypeStruct((1, ATT_WIDTH), F32), jax.ShapeDtypeStruct((1, LRU_WIDTH), F32)],
        compiler_params=_params(("arbitrary",)),
    )(dx2b, att, att_x, rec, ga, gr, wout)


def _lru_bwd(drec, lg, h, lx, cw, cb, wa, ba, wx, bx, lam, bl, seq):
    tc = min(LRU_TILE, seq)
    nc = seq // tc
    T = bl * seq
    n = tc

    def body(dr_ref, lg_ref, h_ref, hp_ref, lx_ref, lxp_ref, cw_ref, cb_ref, wa_ref, ba_ref, wx_ref, bx_ref, lam_ref,
             dlx_ref, dlg_ref, dwa_ref, dwx_ref, small_ref, gc_ref, dxn_ref):
        b, i = pl.program_id(0), pl.program_id(1)
        ir = nc - 1 - i

        @pl.when((b == 0) & (i == 0))
        def _():
            dwa_ref[...] = jnp.zeros_like(dwa_ref)
            dwx_ref[...] = jnp.zeros_like(dwx_ref)
            small_ref[...] = jnp.zeros_like(small_ref)

        @pl.when(i == 0)
        def _():
            gc_ref[...] = jnp.zeros_like(gc_ref)
            dxn_ref[...] = jnp.zeros_like(dxn_ref)

        cw = cw_ref[...]
        lam_v = lam_ref[...]
        lxv = lx_ref[...]
        prev8 = jnp.where(ir > 0, lxp_ref[...], 0.0)
        xc, shifted = _conv_taps(lxv, prev8, cw, cb_ref[...])
        r, ig, sp, a, mult = _lru_gates(xc, wa_ref[...], ba_ref[...], wx_ref[...], bx_ref[...], lam_v)
        hv = h_ref[...]
        drv = dr_ref[...]
        g, dg = _gelu_parts(lg_ref[...])
        dlg_ref[...] = drv * hv * dg
        dh = drv * g

        row = lax.broadcasted_iota(jnp.int32, (n, LRU_WIDTH), 0)
        sub = row & 7
        A = jnp.where(row < n - 1, pltpu.roll(a, n - 1, 0), 0.0)
        B = dh + jnp.where(row == n - 1, gc_ref[0:1, :], 0.0)
        for k in (1, 2, 4):
            a_s = jnp.where(sub < 8 - k, pltpu.roll(A, n - k, 0), 1.0)
            b_s = jnp.where(sub < 8 - k, pltpu.roll(B, n - k, 0), 0.0)
            B = B + A * b_s
            A = A * a_s
        carry = jnp.zeros((1, LRU_WIDTH), F32)
        groups = [None] * (n // 8)
        for g in reversed(range(n // 8)):
            gg = B[8 * g:8 * (g + 1)] + A[8 * g:8 * (g + 1)] * carry
            groups[g] = gg
            carry = gg[0:1]
        gs = jnp.concatenate(groups, axis=0)
        gc_ref[0:1, :] = a[0:1, :] * carry

        hprev8 = jnp.where(ir > 0, hp_ref[...], 0.0)
        h_prev = pltpu.roll(jnp.concatenate([hprev8, hv], axis=0), 1, 0)[8:]
        da = gs * h_prev
        ix = ig * xc
        dmult = gs * ix
        dig = gs * mult * xc
        dxc = gs * mult * ig
        dlog_a = da * a - dmult * (a * a) / mult
        dr_gate = dlog_a * (-LRU_C * sp)
        dza = dr_gate * r * (1.0 - r)
        dzx = dig * ig * (1.0 - ig)
        dzab = dza.astype(MXU_DTYPE)
        dzxb = dzx.astype(MXU_DTYPE)
        xcb = xc.astype(MXU_DTYPE)
        dwa_ref[...] += _tn(xcb, dzab)
        dwx_ref[...] += _tn(xcb, dzxb)
        dxc = dxc + _nt(dzab, wa_ref[...]) + _nt(dzxb, wx_ref[...])

        ds = jnp.concatenate([dxc, dxn_ref[...]], axis=0)
        dlx = cw[CONV_WIDTH - 1:CONV_WIDTH] * dxc
        for k in range(1, CONV_WIDTH):
            dlx = dlx + cw[CONV_WIDTH - 1 - k:CONV_WIDTH - k] * pltpu.roll(ds, n + 8 - k, 0)[:n]
        dlx_ref[...] = dlx
        dxn_ref[...] = dxc[0:8, :]

        colsum = lambda v: jnp.sum(v, axis=0, keepdims=True)
        small_ref[0:1, :] += colsum(dza)
        small_ref[1:2, :] += colsum(dzx)
        small_ref[2:3, :] += colsum(dlog_a * r) * (LRU_C * _sigmoid(-lam_v))
        small_ref[3:4, :] += colsum(dxc)
        for k in range(CONV_WIDTH):
            j = CONV_WIDTH - 1 - k
            small_ref[4 + j:5 + j, :] += colsum(dxc * shifted[k])

    tile = pl.BlockSpec((tc, LRU_WIDTH), lambda b, i: (b * nc + (nc - 1 - i), 0))
    prev = pl.BlockSpec((8, LRU_WIDTH), lambda b, i: (jnp.maximum((b * seq + (nc - 1 - i) * tc) // 8 - 1, 0), 0))
    vec = _const((1, LRU_WIDTH))
    mat = _const((LRU_WIDTH, LRU_WIDTH))
    return pl.pallas_call(
        body, name="lru_bwd", grid=(bl, nc),
        in_specs=[tile, tile, tile, prev, tile, prev, _const((CONV_WIDTH, LRU_WIDTH)), vec, mat, vec, mat, vec, vec],
        out_specs=[tile, tile, mat, mat, _const((8, LRU_WIDTH))],
        out_shape=[jax.ShapeDtypeStruct((T, LRU_WIDTH), F32), jax.ShapeDtypeStruct((T, LRU_WIDTH), F32),
                   jax.ShapeDtypeStruct((LRU_WIDTH, LRU_WIDTH), F32), jax.ShapeDtypeStruct((LRU_WIDTH, LRU_WIDTH), F32),
                   jax.ShapeDtypeStruct((8, LRU_WIDTH), F32)],
        scratch_shapes=[pltpu.VMEM((8, LRU_WIDTH), F32), pltpu.VMEM((8, LRU_WIDTH), F32)],
        compiler_params=_params(("arbitrary", "arbitrary")),
    )(drec, lg, h, h, lx, lx, cw, cb, wa, ba, wx, bx, lam)


def _attn_bwd(qn, kn, vb, dob, lse, delta, frow, fstart, bl, seq, slabs):
    tq = min(ATT_TILE, seq)
    nq = seq // tq
    T = bl * seq
    n = len(slabs)

    def body(fs_ref, q_ref, k_ref, v_ref, do_ref, lse_ref, dl_ref, fr_ref, *rest):
        s_in, (dq_ref, dk_ref, dv_ref, df_ref), s_out, sems = rest[:n], rest[n:n + 4], rest[n + 4:2 * n + 4], rest[2 * n + 4:]
        b, p, j = pl.program_id(0), pl.program_id(1), pl.program_id(2)
        copies = _chip_copies(s_in, s_out, *sems, scatter=True)

        @pl.when((b == 0) & (p == 0) & (j == 0))
        def _():
            for cp in copies:
                cp.start()

        @pl.when(j == 0)
        def _():
            dq_ref[...] = jnp.zeros_like(dq_ref)

        lane = lax.broadcasted_iota(jnp.int32, (1, PAIR), 1)
        rows = lax.broadcasted_iota(jnp.int32, (tq, tq), 0)
        cols = lax.broadcasted_iota(jnp.int32, (tq, tq), 1)
        causal = cols <= rows
        kv = k_ref[...]
        vv = v_ref[...]
        hms = [(lane >= HEAD_DIM * hh) & (lane < HEAD_DIM * (hh + 1)) for hh in range(2)]
        khs = [jnp.where(hm, kv, jnp.zeros_like(kv)) for hm in hms]
        fks = [fr_ref[0, 0, hh:hh + 1, :] for hh in range(2)]
        bases = [((b * N_PAIR + p) * 2 + hh) * nq for hh in range(2)]

        def block(i, carry, masked):
            dk, dv, dfs = carry
            start = pl.multiple_of(i * tq, tq)
            qi = q_ref[pl.ds(start, tq), :]
            doi = do_ref[pl.ds(start, tq), :]
            dq = jnp.zeros((tq, PAIR), F32)
            new_dfs = []
            for hh in range(2):
                c0 = HEAD_DIM * hh
                qh = jnp.where(hms[hh], qi, jnp.zeros_like(qi))
                doh = jnp.where(hms[hh], doi, jnp.zeros_like(doi))
                s = _nt(qh, kv) - (fks[hh] - fs_ref[bases[hh] + i])
                if masked:
                    s = jnp.where(causal, s, NEG)
                pr = jnp.exp(s - lse_ref[pl.ds(start, tq), c0:c0 + 1])
                dp = _nt(doh, vv)
                ds = pr * (dp - dl_ref[pl.ds(start, tq), c0:c0 + 1])
                dsb = ds.astype(MXU_DTYPE)
                dv = dv + _tn(pr.astype(MXU_DTYPE), doh)
                dk = dk + _tn(dsb, qh)
                dq = dq + jnp.dot(dsb, khs[hh], preferred_element_type=F32)
                new_dfs.append(dfs[hh] - jnp.sum(ds, axis=0, keepdims=True))
            dq_ref[pl.ds(start, tq), :] += dq
            return dk, dv, tuple(new_dfs)

        zero = jnp.zeros((tq, PAIR), F32)
        carry = block(j, (zero, zero, (jnp.zeros((1, tq), F32), jnp.zeros((1, tq), F32))), True)
        dk, dv, dfs = lax.fori_loop(j + 1, nq, functools.partial(block, masked=False), carry)
        for hh in range(2):
            df_ref[0, 0, hh:hh + 1, :] = dfs[hh]
        dk_ref[...] = dk
        dv_ref[...] = dv

        @pl.when((b == bl - 1) & (p == N_PAIR - 1) & (j == nq - 1))
        def _():
            for cp in copies:
                cp.wait()

    blk = pl.BlockSpec((tq, PAIR), lambda b, p, j: (b * nq + j, p))
    full = pl.BlockSpec((seq, PAIR), lambda b, p, j: (b, p))
    fblk = pl.BlockSpec((1, 1, 2, tq), lambda b, p, j: (b, p, 0, j))
    hbm = pl.BlockSpec(memory_space=pl.ANY)
    return pl.pallas_call(
        body, name="attn_bwd", grid=(bl, N_PAIR, nq),
        in_specs=[pl.BlockSpec(memory_space=pltpu.SMEM), full, blk, blk, full, full, full, fblk] + [hbm] * n,
        out_specs=[full, blk, blk, fblk] + [hbm] * n,
        out_shape=[jax.ShapeDtypeStruct((T, ATT_WIDTH), F32), jax.ShapeDtypeStruct((T, ATT_WIDTH), F32),
                   jax.ShapeDtypeStruct((T, ATT_WIDTH), F32), jax.ShapeDtypeStruct((bl, N_PAIR, 2, seq), F32)]
        + [jax.ShapeDtypeStruct(s.shape, s.dtype) for s in slabs],
        scratch_shapes=_chip_sems(n),
        compiler_params=_params(("arbitrary", "arbitrary", "arbitrary"), VMEM_LARGE),
    )(fstart, qn, kn, vb, dob, lse, delta, frow, *slabs)


def _forget_bwd(dfcol, f2d, bf, bl, seq):
    def body(d_ref, z_ref, b_ref, o_ref, db_ref):
        @pl.when(pl.program_id(0) == 0)
        def _():
            db_ref[...] = jnp.zeros_like(db_ref)

        d = d_ref[...]
        row = lax.broadcasted_iota(jnp.int32, (seq, F_PAD), 0)
        k = 1
        while k < seq:
            d = d + jnp.where(row < seq - k, pltpu.roll(d, seq - k, 0), 0.0)
            k *= 2
        dz = d * _sigmoid(-(z_ref[...] + b_ref[...]))
        o_ref[...] = dz
        db_ref[...] += jnp.sum(dz, axis=0, keepdims=True)

    blk = pl.BlockSpec((seq, F_PAD), lambda b: (b, 0))
    return pl.pallas_call(
        body, name="forget_bwd", grid=(bl,),
        in_specs=[blk, blk, _const((1, F_PAD))],
        out_specs=[blk, _const((1, F_PAD))],
        out_shape=[jax.ShapeDtypeStruct(f2d.shape, F32), jax.ShapeDtypeStruct((1, F_PAD), F32)],
        compiler_params=_params(("arbitrary",)),
    )(dfcol, f2d, bf)


def _inproj_bwd(dq, dk, dv, qkv, df, dlx, dlg, x2d, dx2, g1, gq2, gk2, wcat):
    T = x2d.shape[0]
    tm = TOKEN_TILE

    def body(dq_ref, dk_ref, dv_ref, qkv_ref, df_ref, dlx_ref, dlg_ref, x_ref, dx2_ref, g1_ref, gq_ref, gk_ref, w_ref,
             gx_ref, dp_ref, h_ref, dg1_ref, dgq_ref, dgk_ref):
        @pl.when(pl.program_id(0) == 0)
        def _():
            dg1_ref[...] = jnp.zeros_like(dg1_ref)
            dgq_ref[...] = jnp.zeros_like(dgq_ref)
            dgk_ref[...] = jnp.zeros_like(dgk_ref)

        lo = _lo_mask()

        def head_norm_bwd(t, g2, dy):
            rr = lax.rsqrt(_half_sums(t * t, lo) * (1.0 / HEAD_DIM) + NORM_EPS)
            th = t * rr
            dth = dy * g2
            mm = _half_sums(dth * th, lo) * (1.0 / HEAD_DIM)
            return rr * (dth - th * mm), jnp.sum(dy * th, axis=0, keepdims=True)

        dgq = jnp.zeros((1, PAIR), F32)
        dgk = jnp.zeros((1, PAIR), F32)
        for p in range(N_PAIR):
            cq = slice(PAIR * p, PAIR * (p + 1))
            ck = slice(ATT_WIDTH + PAIR * p, ATT_WIDTH + PAIR * (p + 1))
            dqp, g_ = head_norm_bwd(qkv_ref[:, cq], gq_ref[...], dq_ref[:, cq] * QK_SCALE)
            dgq = dgq + g_
            dp_ref[:, cq] = dqp.astype(MXU_DTYPE)
            dkp, g_ = head_norm_bwd(qkv_ref[:, ck], gk_ref[...], dk_ref[:, cq])
            dgk = dgk + g_
            dp_ref[:, ck] = dkp.astype(MXU_DTYPE)
        dgq_ref[...] += dgq
        dgk_ref[...] += dgk
        f0 = 3 * ATT_WIDTH
        dp_ref[:, 2 * ATT_WIDTH:f0] = dv_ref[...].astype(MXU_DTYPE)
        dp_ref[:, f0:f0 + F_PAD] = df_ref[...].astype(MXU_DTYPE)
        dp_ref[:, f0 + F_PAD:f0 + F_PAD + LRU_WIDTH] = dlx_ref[...].astype(MXU_DTYPE)
        dp_ref[:, f0 + F_PAD + LRU_WIDTH:] = dlg_ref[...].astype(MXU_DTYPE)
        dh = jnp.dot(dp_ref[...], w_ref[...], preferred_element_type=F32)
        x = x_ref[...]
        r = lax.rsqrt(jnp.mean(x * x, axis=-1, keepdims=True) + NORM_EPS)
        xh = x * r
        h_ref[...] = (xh * g1_ref[...]).astype(MXU_DTYPE)
        dg1_ref[...] += jnp.sum(dh * xh, axis=0, keepdims=True)
        dxh = dh * g1_ref[...]
        gx_ref[...] = dx2_ref[...] + r * (dxh - xh * jnp.mean(dxh * xh, axis=-1, keepdims=True))

    row = lambda w: pl.BlockSpec((tm, w), lambda i: (i, 0))
    return pl.pallas_call(
        body, name="inproj_bwd", grid=(T // tm,),
        in_specs=[row(ATT_WIDTH), row(ATT_WIDTH), row(ATT_WIDTH), row(3 * ATT_WIDTH), row(F_PAD), row(LRU_WIDTH),
                  row(LRU_WIDTH), row(D_MODEL), row(D_MODEL), _const((1, D_MODEL)), _const((1, PAIR)), _const((1, PAIR)),
                  _const((N_CAT, D_MODEL))],
        out_specs=[row(D_MODEL), row(N_CAT), row(D_MODEL), _const((1, D_MODEL)), _const((1, PAIR)), _const((1, PAIR))],
        out_shape=[jax.ShapeDtypeStruct((T, D_MODEL), F32), jax.ShapeDtypeStruct((T, N_CAT), MXU_DTYPE),
                   jax.ShapeDtypeStruct((T, D_MODEL), MXU_DTYPE), jax.ShapeDtypeStruct((1, D_MODEL), F32),
                   jax.ShapeDtypeStruct((1, PAIR), F32), jax.ShapeDtypeStruct((1, PAIR), F32)],
        compiler_params=_params(("arbitrary",), VMEM_LARGE),
    )(dq, dk, dv, qkv, df, dlx, dlg, x2d, dx2, g1, gq2, gk2, wcat)


ELEMENTWISE_COLS = 256


def _sum_slabs(recv, name):
    _, rows, cols = recv.shape
    cb = ELEMENTWISE_COLS

    def body(r_ref, o_ref):
        part = [r_ref[s].astype(F32) for s in range(4)]
        o_ref[...] = ((part[0] + part[1]) + part[2]) + part[3]

    return pl.pallas_call(
        body, name=name, grid=(cols // cb,),
        in_specs=[pl.BlockSpec((4, rows, cb), lambda i: (0, 0, i))],
        out_specs=pl.BlockSpec((rows, cb), lambda i: (0, i)),
        out_shape=jax.ShapeDtypeStruct((rows, cols), F32),
        compiler_params=_params(("parallel",)),
    )(recv)


def _adamw_math(w, g, m, v):
    m = ADAM_B1 * m + (1.0 - ADAM_B1) * g
    v = ADAM_B2 * v + (1.0 - ADAM_B2) * (g * g)
    m_hat = m / (1.0 - ADAM_B1 ** ADAM_STEP)
    v_hat = v / (1.0 - ADAM_B2 ** ADAM_STEP)
    delta = -ADAM_LR * (m_hat / (jnp.sqrt(v_hat) + ADAM_EPS) + ADAM_WD * w)
    return delta, m, v


def _adamw_pair(mine, theirs, w, m, v, name):
    rows, cols = w.shape
    cb = ELEMENTWISE_COLS

    def body(a_ref, b_ref, w_ref, m_ref, v_ref, g_ref, d_ref, nm_ref, nv_ref):
        g = a_ref[...] + b_ref[...]
        g_ref[...] = g
        d_ref[...], nm_ref[...], nv_ref[...] = _adamw_math(w_ref[...], g, m_ref[...], v_ref[...])

    blk = pl.BlockSpec((rows, cb), lambda i: (0, i))
    return pl.pallas_call(
        body, name=name, grid=(cols // cb,),
        in_specs=[blk] * 5, out_specs=[blk] * 4,
        out_shape=[jax.ShapeDtypeStruct((rows, cols), F32)] * 4,
        compiler_params=_params(("parallel",)),
    )(mine, theirs, w, m, v)


VEC_ROW = {"norm1_g": 0, "norm2_g": 1, "attn_out_g": 2, "lru_out_g": 3, "q_norm_g": 4, "k_norm_g": 5, "b_f": 6,
           "b_a": 8, "b_x": 9, "lam": 10, "conv_b": 11}
LOSS_ROW, CONV_W_ROW, PACK_ROWS = 7, 12, 16
SMALL = list(VEC_ROW) + ["conv_w", "w_a", "w_x"]


def _pack_small(dg1, dg2, dga, dgr, dgq, dgk, dbf, sq_err, lru_small, dwa_bd, dwx_bd):
    blk = LRU_WIDTH // LRU_BLOCKS

    def body(dg1_ref, dg2_ref, dga_ref, dgr_ref, dgq_ref, dgk_ref, dbf_ref, err_ref, lru_ref, wa_ref, wx_ref,
             v_ref, oa_ref, ox_ref):
        v_ref[...] = jnp.zeros_like(v_ref)
        v_ref[0:1, :] = dg1_ref[...]
        v_ref[1:2, :] = dg2_ref[...]
        v_ref[2:3, 0:ATT_WIDTH] = dga_ref[...]
        v_ref[3:4, 0:LRU_WIDTH] = dgr_ref[...]
        for row, ref in ((4, dgq_ref), (5, dgk_ref)):
            g = ref[...]
            v_ref[row:row + 1, 0:PAIR] = g + pltpu.roll(g, HEAD_DIM, 1)
        v_ref[6:7, 0:F_PAD] = dbf_ref[...]
        v_ref[LOSS_ROW:LOSS_ROW + 1, 0:128] = err_ref[0:1, :] * (0.5 / D_MODEL)
        v_ref[8:16, 0:LRU_WIDTH] = lru_ref[...]
        for src, dst in ((wa_ref, oa_ref), (wx_ref, ox_ref)):
            for nb in range(LRU_BLOCKS):
                tile = src[blk * nb:blk * (nb + 1), PAIR * (nb // 2):PAIR * (nb // 2 + 1)]
                if nb % 2:
                    tile = pltpu.roll(tile, blk, 1)
                dst[nb] = tile[:, 0:blk]

    ins = [dg1, dg2, dga, dgr, dgq, dgk, dbf, sq_err, lru_small, dwa_bd, dwx_bd]
    outs = [jax.ShapeDtypeStruct((PACK_ROWS, D_MODEL), F32), jax.ShapeDtypeStruct((LRU_BLOCKS, blk, blk), F32),
            jax.ShapeDtypeStruct((LRU_BLOCKS, blk, blk), F32)]
    return pl.pallas_call(
        body, name="pack_small", grid=(1,),
        in_specs=[_const(a.shape) for a in ins], out_specs=[_const(o.shape) for o in outs], out_shape=outs,
        compiler_params=_params(("arbitrary",)),
    )(*ins)


def _adamw_small(recv_v, recv_a, recv_x, params):
    names = list(params)
    flat = [a for n in names for a in params[n]]

    def body(rv_ref, ra_ref, rx_ref, *refs):
        ins, loss_ref, outs = refs[:len(flat)], refs[len(flat)], refs[len(flat) + 1:]
        x, y = lax.axis_index("x"), lax.axis_index("y")
        me = 2 * x + y

        def total(r):
            acc = r[0]
            for d in range(1, 8):
                acc = acc + r[d]
            return acc

        gv, ga, gx = total(rv_ref), total(ra_ref), total(rx_ref)
        loss_ref[...] = gv[LOSS_ROW:LOSS_ROW + 1, 0:128]
        for i, n in enumerate(names):
            w_ref, m_ref, v_ref = ins[3 * i:3 * i + 3]
            g_ref, d_ref, nm_ref, nv_ref = outs[4 * i:4 * i + 4]
            if n in VEC_ROW:
                g = gv[VEC_ROW[n]:VEC_ROW[n] + 1, 0:w_ref.shape[1]]
                w, m, v = w_ref[...], m_ref[...], v_ref[...]
            else:
                if n == "conv_w":
                    full = gv[CONV_W_ROW:CONV_W_ROW + CONV_WIDTH, 0:LRU_WIDTH]
                    width = LRU_WIDTH // 4
                    g = jnp.zeros((CONV_WIDTH, width), F32)
                    for s in range(4):
                        g = jnp.where(me == s, full[:, width * s:width * (s + 1)], g)
                else:
                    g = ga if n == "w_a" else gx
                w, m, v = w_ref[0], m_ref[0], v_ref[0]
            d, nm, nv = _adamw_math(w, g, m, v)
            for ref, val in ((g_ref, g), (d_ref, d), (nm_ref, nm), (nv_ref, nv)):
                if n in VEC_ROW:
                    ref[...] = val
                else:
                    ref[0] = val

    out_shape = [jax.ShapeDtypeStruct((1, 128), F32)] + [jax.ShapeDtypeStruct(params[n][0].shape, F32)
                                                          for n in names for _ in range(4)]
    res = pl.pallas_call(
        body, name="adamw_small", grid=(1,),
        in_specs=[_const(a.shape) for a in (recv_v, recv_a, recv_x, *flat)],
        out_specs=[_const(o.shape) for o in out_shape], out_shape=out_shape,
        compiler_params=_params(("arbitrary",)),
    )(recv_v, recv_a, recv_x, *flat)
    return res[0], {n: tuple(res[1 + 4 * i:5 + 4 * i]) for i, n in enumerate(names)}


def _cat_shards(g, pad_at=None, pad=0):
    _, rows, w = g.shape
    pieces = []
    for s in range(4):
        lo, hi = s * w, (s + 1) * w
        if pad_at is not None and lo < pad_at <= hi:
            pieces += [g[s][:, :pad_at - lo], jnp.zeros((rows, pad), g.dtype)]
            if pad_at < hi:
                pieces.append(g[s][:, pad_at - lo:])
        else:
            pieces.append(g[s])
    return jnp.concatenate(pieces, axis=1)


def _block_diag(w):
    eye = jnp.eye(LRU_BLOCKS, dtype=w.dtype)
    return (w[:, :, None, :] * eye[:, None, :, None]).reshape(LRU_WIDTH, LRU_WIDTH)


def kernel(x, norm1_g, w_in, q_norm_g, k_norm_g, b_f, conv_w, conv_b, w_a, b_a, w_x, b_x, lam, attn_out_g, lru_out_g, w_out, norm2_g, w_gate, w_up, w_down, loss_target, m_norm1_g, m_w_in, m_q_norm_g, m_k_norm_g, m_b_f, m_conv_w, m_conv_b, m_w_a, m_b_a, m_w_x, m_b_x, m_lam, m_attn_out_g, m_lru_out_g, m_w_out, m_norm2_g, m_w_gate, m_w_up, m_w_down, v_norm1_g, v_w_in, v_q_norm_g, v_k_norm_g, v_b_f, v_conv_w, v_conv_b, v_w_a, v_b_a, v_w_x, v_b_x, v_lam, v_attn_out_g, v_lru_out_g, v_w_out, v_norm2_g, v_w_gate, v_w_up, v_w_down):
    args = dict(locals())
    bl, seq, _ = x.shape
    T = bl * seq
    tq = min(ATT_TILE, seq)
    nq = seq // tq
    dff = w_gate.shape[2] * 4

    def transposed(name):
        return name.endswith(("w_in", "w_gate", "w_up"))

    def shard2d(name):
        return jnp.swapaxes(args[name], 1, 2)[0] if transposed(name) else args[name][0]

    g_in, g_cw = _gather_over_chips([shard2d("w_in").astype(MXU_DTYPE), conv_w[0]])
    later_shards = [shard2d(n).astype(MXU_DTYPE) for n in ("w_out", "w_gate", "w_up", "w_down")]
    f0 = 3 * ATT_WIDTH
    w_in_t = g_in.reshape(-1, D_MODEL)
    wcat = jnp.concatenate([w_in_t[:f0 + HEADS], jnp.zeros((F_PAD - HEADS, D_MODEL), w_in_t.dtype),
                            w_in_t[f0 + HEADS:]], axis=0)
    cw_full = _cat_shards(g_cw)
    wa_bd = _block_diag(w_a[0]).astype(MXU_DTYPE)
    wx_bd = _block_diag(w_x[0]).astype(MXU_DTYPE)
    gq2 = jnp.tile(q_norm_g, (1, 2))
    gk2 = jnp.tile(k_norm_g, (1, 2))
    bf_pad = jnp.pad(b_f, ((0, 0), (0, F_PAD - HEADS)))

    x2d = x.reshape(T, D_MODEL)
    target2d = loss_target.reshape(T, D_MODEL)

    qkv, qn, kn, vb, f2d, lx, lg = _inproj(x2d, norm1_g, wcat, gq2, gk2)
    fcol = _forget_cumsum(f2d, bf_pad, bl, seq)
    frow = jnp.transpose(fcol.reshape(bl, seq, F_PAD)[:, :, :HEADS], (0, 2, 1)).reshape(bl, N_PAIR, 2, seq)
    fstart = frow[:, :, :, ::tq].reshape(-1)
    att, att_x, lse, g_out, g_gate, g_up, g_down = _attn_fwd(qn, kn, vb, frow, fstart, bl, seq, later_shards)
    wout_full = g_out.reshape(D_MODEL, D_MODEL)
    wg_full, wu_full = g_gate.reshape(dff, D_MODEL), g_up.reshape(dff, D_MODEL)
    wd_full = g_down.reshape(dff, D_MODEL)
    h, rec = _lru_fwd(lx, lg, cw_full, conv_b, wa_bd, b_a, wx_bd, b_x, lam, bl, seq)
    x2 = _outproj(x2d, att, rec, attn_out_g, lru_out_g, wout_full)
    gt, up, dy, sq_err = _mlp_fwd(x2, norm2_g, wg_full, wu_full, wd_full, target2d)

    dx2, dx2b, dgtb, dupb, actb, h2b, dyb, dg2 = _mlp_bwd(dy, x2, gt, up, norm2_g, wg_full, wu_full, wd_full)
    dw_down = _matmul_tn(actb, dyb, D_MODEL, "dw_down")
    dw_gate = _matmul_tn(dgtb, h2b, D_MODEL, "dw_gate")
    dw_up = _matmul_tn(dupb, h2b, D_MODEL, "dw_up")
    dattb, delta, drec, mixb, dga, dgr = _outproj_bwd(dx2b, att, att_x, rec, attn_out_g, lru_out_g, wout_full)
    dw_out = _matmul_tn(mixb, dx2b, D_MODEL, "dw_out")
    dlx, dlg, dwa_bd, dwx_bd, lru_small = _lru_bwd(drec, lg, h, lx, cw_full, conv_b, wa_bd, b_a, wx_bd, b_x, lam, bl, seq)
    early_slabs = [dw_out.reshape(4, D_MODEL // 4, D_MODEL), dw_gate.reshape(4, dff // 4, D_MODEL),
                   dw_up.reshape(4, dff // 4, D_MODEL), dw_down.reshape(4, dff // 4, D_MODEL)]
    dq, dk, dv, dfrow, *recv_early = _attn_bwd(qn, kn, vb, dattb, lse, delta, frow, fstart, bl, seq, early_slabs)
    dfcol = jnp.pad(jnp.transpose(dfrow.reshape(bl, HEADS, seq), (0, 2, 1)), ((0, 0), (0, 0), (0, F_PAD - HEADS)))
    df, dbf = _forget_bwd(dfcol.reshape(T, F_PAD), f2d, bf_pad, bl, seq)
    grad_x, dprojb, h1b, dg1, dgq, dgk = _inproj_bwd(dq, dk, dv, qkv, df, dlx, dlg, x2d, dx2, norm1_g, gq2, gk2, wcat)
    dwcat = _matmul_tn(dprojb, h1b, D_MODEL, "dw_in")

    pack_v, pack_a, pack_x = _pack_small(dg1, dg2, dga, dgr, dgq, dgk, dbf, sq_err, lru_small, dwa_bd, dwx_bd)
    dw_in_slabs = jnp.concatenate([dwcat[:f0 + HEADS], dwcat[f0 + F_PAD:]], axis=0).astype(jnp.bfloat16).reshape(
        4, -1, D_MODEL)
    recv_in, recv_v, recv_a, recv_x = _exchange_grads([dw_in_slabs], [pack_v, pack_a, pack_x])
    recv = [recv_in] + recv_early
    big = ["w_in", "w_out", "w_gate", "w_up", "w_down"]
    part = [_sum_slabs(r, "sum_" + n) for r, n in zip(recv, big)]
    theirs = _swap_with_sibling(part)
    out = {}
    for n, a, b_ in zip(big, part, theirs):
        res = _adamw_pair(a, b_, shard2d(n), shard2d("m_" + n), shard2d("v_" + n), "adamw_" + n)
        out[n] = tuple(jnp.swapaxes(r[None], 1, 2) if transposed(n) else r[None] for r in res)
    loss_row, small_out = _adamw_small(recv_v, recv_a, recv_x,
                                       {n: (args[n], args["m_" + n], args["v_" + n]) for n in SMALL})
    out.update(small_out)
    loss = loss_row[0, 0]

    order = ["norm1_g", "w_in", "q_norm_g", "k_norm_g", "b_f", "conv_w", "conv_b", "w_a", "b_a", "w_x", "b_x", "lam",
             "attn_out_g", "lru_out_g", "w_out", "norm2_g", "w_gate", "w_up", "w_down"]
    return (loss, grad_x.reshape(bl, seq, D_MODEL), *[out[n][0] for n in order], *[out[n][1] for n in order],
            *[out[n][2] for n in order], *[out[n][3] for n in order])
```

```python
import functools
import math

import jax
import jax.numpy as jnp
from jax import lax
from jax.experimental import pallas as pl
from jax.experimental.pallas import tpu as pltpu

F32 = jnp.float32
MXU_DTYPE = jnp.bfloat16
MESH = pl.DeviceIdType.MESH

D_MODEL = 1024
ATT_WIDTH = 512
LRU_WIDTH = 512
HEADS = 8
HEAD_DIM = 64
PAIR = 2 * HEAD_DIM
N_PAIR = HEADS // 2
LRU_BLOCKS = 8
CONV_WIDTH = 4
LRU_C = 8.0
NORM_EPS = 1e-6
QK_SCALE = 1.0 / math.sqrt(HEAD_DIM)
F_PAD = 128
N_CAT = 3 * ATT_WIDTH + F_PAD + 2 * LRU_WIDTH
NEG = -1e30

ADAM_LR, ADAM_B1, ADAM_B2, ADAM_EPS, ADAM_WD, ADAM_STEP = 0.001, 0.9, 0.999, 1e-08, 0.01, 10

TOKEN_TILE = 256
ATT_TILE = 512
LRU_TILE = 256
VMEM_SMALL = 32 * 1024 * 1024
VMEM_LARGE = 56 * 1024 * 1024


def _params(sem, vmem=VMEM_SMALL):
    return pltpu.CompilerParams(dimension_semantics=sem, vmem_limit_bytes=vmem)


def _const(shape):
    nd = len(shape)
    return pl.BlockSpec(shape, lambda *_: (0,) * nd)


def _sigmoid(x):
    return 1.0 / (1.0 + jnp.exp(-x))


def _nt(a, b):
    return lax.dot_general(a, b, (((1,), (1,)), ((), ())), preferred_element_type=F32)


def _tn(a, b):
    return lax.dot_general(a, b, (((0,), (0,)), ((), ())), preferred_element_type=F32)


def _half_sums(t, lo):
    s_lo = jnp.sum(jnp.where(lo, t, 0.0), axis=-1, keepdims=True)
    s_hi = jnp.sum(jnp.where(lo, 0.0, t), axis=-1, keepdims=True)
    return jnp.where(lo, s_lo, s_hi)


def _lo_mask():
    return lax.broadcasted_iota(jnp.int32, (1, PAIR), 1) < HEAD_DIM


def _other_chips(x, y):
    return [(1 - x, y), (x, 1 - y), (1 - x, 1 - y)]


def _chip_copies(ins, outs, send_sems, recv_sems, loc_sems, scatter):
    x, y, c = lax.axis_index("x"), lax.axis_index("y"), lax.axis_index("c")
    me = 2 * x + y
    copies = []
    for w in range(len(ins)):
        copies.append(pltpu.make_async_copy(ins[w].at[me] if scatter else ins[w], outs[w].at[me], loc_sems.at[w]))
        for k, (cx, cy) in enumerate(_other_chips(x, y)):
            copies.append(pltpu.make_async_remote_copy(
                src_ref=ins[w].at[2 * cx + cy] if scatter else ins[w], dst_ref=outs[w].at[me],
                send_sem=send_sems.at[3 * w + k], recv_sem=recv_sems.at[3 * w + k],
                device_id=(cx, cy, c), device_id_type=MESH))
    return copies


def _chip_sems(n):
    return [pltpu.SemaphoreType.DMA((3 * n,)), pltpu.SemaphoreType.DMA((3 * n,)), pltpu.SemaphoreType.DMA((n,))]


def _gather_split(shard, small):
    half = shard.shape[1] // 2

    def body(w_ref, s_ref, ow_ref, os_ref, ici_send, ici_recv, d2d_send, d2d_recv, sm_send, sm_recv, loc_sems):
        x, y, c = lax.axis_index("x"), lax.axis_index("y"), lax.axis_index("c")
        me = 2 * x + y
        mine = pl.ds(pl.multiple_of(c * half, half), half)
        local = [pltpu.make_async_copy(w_ref, ow_ref.at[me], loc_sems.at[0]),
                 pltpu.make_async_copy(s_ref, os_ref.at[me], loc_sems.at[1])]
        fetch, little, forward = [], [], []
        for k, (cx, cy) in enumerate(_other_chips(x, y)):
            src_chip = 2 * cx + cy
            fetch.append(pltpu.make_async_remote_copy(
                src_ref=w_ref.at[:, mine], dst_ref=ow_ref.at[me, :, mine], send_sem=ici_send.at[k],
                recv_sem=ici_recv.at[k], device_id=(cx, cy, c), device_id_type=MESH))
            little.append(pltpu.make_async_remote_copy(
                src_ref=s_ref, dst_ref=os_ref.at[me], send_sem=sm_send.at[k], recv_sem=sm_recv.at[k],
                device_id=(cx, cy, c), device_id_type=MESH))
            forward.append(pltpu.make_async_remote_copy(
                src_ref=ow_ref.at[src_chip, :, mine], dst_ref=ow_ref.at[src_chip, :, mine], send_sem=d2d_send.at[k],
                recv_sem=d2d_recv.at[k], device_id=(x, y, 1 - c), device_id_type=MESH))
        for cp in local + fetch + little:
            cp.start()
        for k in range(3):
            fetch[k].wait_recv()
            forward[k].start()
        for cp in fetch:
            cp.wait_send()
        for cp in little + forward + local:
            cp.wait()

    return pl.pallas_call(
        body, name="gather_weights",
        out_shape=[jax.ShapeDtypeStruct((4,) + shard.shape, shard.dtype),
                   jax.ShapeDtypeStruct((4,) + small.shape, small.dtype)],
        in_specs=[pl.BlockSpec(memory_space=pl.ANY)] * 2,
        out_specs=[pl.BlockSpec(memory_space=pl.ANY)] * 2,
        scratch_shapes=[pltpu.SemaphoreType.DMA((3,))] * 6 + [pltpu.SemaphoreType.DMA((2,))],
    )(shard, small)


def _device_copies(packs_in, packs_out, psend, precv, loc_sems, loc_base):
    x, y, c = lax.axis_index("x"), lax.axis_index("y"), lax.axis_index("c")
    dev = 4 * x + 2 * y + c
    copies = []
    for j in range(len(packs_in)):
        copies.append(pltpu.make_async_copy(packs_in[j], packs_out[j].at[dev], loc_sems.at[loc_base + j]))
        for k in range(1, 8):
            fx, fy, fc = (k >> 2) & 1, (k >> 1) & 1, k & 1
            tx = (1 - x) if fx else x
            ty = (1 - y) if fy else y
            tc = (1 - c) if fc else c
            copies.append(pltpu.make_async_remote_copy(
                src_ref=packs_in[j], dst_ref=packs_out[j].at[dev],
                send_sem=psend.at[7 * j + k - 1], recv_sem=precv.at[7 * j + k - 1],
                device_id=(tx, ty, tc), device_id_type=MESH))
    return copies


def _exchange_grads(slabs, packs):
    n, npk = len(slabs), len(packs)

    def body(*refs):
        ins, pack_in = refs[:n], refs[n:n + npk]
        outs, pack_out = refs[n + npk:2 * n + npk], refs[2 * n + npk:2 * (n + npk)]
        send_sems, recv_sems, loc_sems, psend, precv = refs[2 * (n + npk):]
        copies = _chip_copies(ins, outs, send_sems, recv_sems, loc_sems, scatter=True)
        copies += _device_copies(pack_in, pack_out, psend, precv, loc_sems, n)
        for cp in copies:
            cp.start()
        for cp in copies:
            cp.wait()

    return pl.pallas_call(
        body, name="exchange_grads",
        out_shape=[jax.ShapeDtypeStruct(s.shape, s.dtype) for s in slabs]
        + [jax.ShapeDtypeStruct((8,) + p.shape, p.dtype) for p in packs],
        in_specs=[pl.BlockSpec(memory_space=pl.ANY)] * (n + npk),
        out_specs=[pl.BlockSpec(memory_space=pl.ANY)] * (n + npk),
        scratch_shapes=[pltpu.SemaphoreType.DMA((3 * n,)), pltpu.SemaphoreType.DMA((3 * n,)),
                        pltpu.SemaphoreType.DMA((n + npk,)),
                        pltpu.SemaphoreType.DMA((7 * npk,)), pltpu.SemaphoreType.DMA((7 * npk,))],
    )(*slabs, *packs)


def _swap_with_sibling(arrs):
    n = len(arrs)

    def body(*refs):
        ins, outs = refs[:n], refs[n:2 * n]
        send_sems, recv_sems = refs[2 * n:]
        x, y, c = lax.axis_index("x"), lax.axis_index("y"), lax.axis_index("c")
        copies = []
        for w in range(n):
            cp = pltpu.make_async_remote_copy(
                src_ref=ins[w], dst_ref=outs[w], send_sem=send_sems.at[w], recv_sem=recv_sems.at[w],
                device_id=(x, y, 1 - c), device_id_type=MESH)
            cp.start()
            copies.append(cp)
        for cp in copies:
            cp.wait()

    return pl.pallas_call(
        body, name="swap_sibling",
        out_shape=[jax.ShapeDtypeStruct(a.shape, a.dtype) for a in arrs],
        in_specs=[pl.BlockSpec(memory_space=pl.ANY)] * n,
        out_specs=[pl.BlockSpec(memory_space=pl.ANY)] * n,
        scratch_shapes=[pltpu.SemaphoreType.DMA((n,)), pltpu.SemaphoreType.DMA((n,))],
    )(*arrs)


def _head_norm(t, g2, lo):
    rr = lax.rsqrt(_half_sums(t * t, lo) * (1.0 / HEAD_DIM) + NORM_EPS)
    return t * rr * g2


def _inproj(x2d, g1, wcat, gq2, gk2):
    T = x2d.shape[0]
    tm = TOKEN_TILE

    def body(x_ref, g1_ref, w_ref, gq_ref, gk_ref, qkv_ref, qn_ref, kn_ref, vb_ref, f_ref, lx_ref, lg_ref):
        x = x_ref[...]
        r = lax.rsqrt(jnp.mean(x * x, axis=-1, keepdims=True) + NORM_EPS)
        h = (x * r * g1_ref[...]).astype(MXU_DTYPE)
        proj = _nt(h, w_ref[...])
        qkv_ref[...] = proj[:, :3 * ATT_WIDTH]
        lo = _lo_mask()
        for p in range(N_PAIR):
            cols = slice(PAIR * p, PAIR * (p + 1))
            q = proj[:, PAIR * p:PAIR * (p + 1)]
            k = proj[:, ATT_WIDTH + PAIR * p:ATT_WIDTH + PAIR * (p + 1)]
            qn_ref[:, cols] = (_head_norm(q, gq_ref[...], lo) * QK_SCALE).astype(MXU_DTYPE)
            kn_ref[:, cols] = _head_norm(k, gk_ref[...], lo).astype(MXU_DTYPE)
        vb_ref[...] = proj[:, 2 * ATT_WIDTH:3 * ATT_WIDTH].astype(MXU_DTYPE)
        f0 = 3 * ATT_WIDTH
        f_ref[...] = proj[:, f0:f0 + F_PAD]
        lx_ref[...] = proj[:, f0 + F_PAD:f0 + F_PAD + LRU_WIDTH]
        lg_ref[...] = proj[:, f0 + F_PAD + LRU_WIDTH:]

    row = lambda w: pl.BlockSpec((tm, w), lambda i: (i, 0))
    return pl.pallas_call(
        body, name="inproj", grid=(T // tm,),
        in_specs=[row(D_MODEL), _const((1, D_MODEL)), _const((N_CAT, D_MODEL)), _const((1, PAIR)), _const((1, PAIR))],
        out_specs=[row(3 * ATT_WIDTH), row(ATT_WIDTH), row(ATT_WIDTH), row(ATT_WIDTH), row(F_PAD),
                   row(LRU_WIDTH), row(LRU_WIDTH)],
        out_shape=[jax.ShapeDtypeStruct((T, 3 * ATT_WIDTH), F32),
                   jax.ShapeDtypeStruct((T, ATT_WIDTH), MXU_DTYPE), jax.ShapeDtypeStruct((T, ATT_WIDTH), MXU_DTYPE),
                   jax.ShapeDtypeStruct((T, ATT_WIDTH), MXU_DTYPE), jax.ShapeDtypeStruct((T, F_PAD), F32),
                   jax.ShapeDtypeStruct((T, LRU_WIDTH), F32), jax.ShapeDtypeStruct((T, LRU_WIDTH), F32)],
        compiler_params=_params(("parallel",), VMEM_LARGE),
    )(x2d, g1, wcat, gq2, gk2)


def _forget_cumsum(f2d, bf, bl, seq):
    def body(z_ref, b_ref, o_ref):
        z = z_ref[...] + b_ref[...]
        lf = jnp.minimum(z, 0.0) - jnp.log(1.0 + jnp.exp(-jnp.abs(z)))
        row = lax.broadcasted_iota(jnp.int32, (seq, F_PAD), 0)
        k = 1
        while k < seq:
            lf = lf + jnp.where(row >= k, pltpu.roll(lf, k, 0), 0.0)
            k *= 2
        o_ref[...] = lf

    return pl.pallas_call(
        body, name="forget_cumsum", grid=(bl,),
        in_specs=[pl.BlockSpec((seq, F_PAD), lambda b: (b, 0)), _const((1, F_PAD))],
        out_specs=pl.BlockSpec((seq, F_PAD), lambda b: (b, 0)),
        out_shape=jax.ShapeDtypeStruct(f2d.shape, F32),
        compiler_params=_params(("parallel",)),
    )(f2d, bf)


def _attn_fwd(qn, kn, vb, frow, fstart, bl, seq, shards):
    tq = min(ATT_TILE, seq)
    nq = seq // tq
    T = bl * seq
    n = len(shards)

    def body(fs_ref, q_ref, k_ref, v_ref, fr_ref, *rest):
        g_in, (o_ref, ox_ref, lse_ref), g_out, sems = rest[:n], rest[n:n + 3], rest[n + 3:2 * n + 3], rest[2 * n + 3:]
        b, p, i = pl.program_id(0), pl.program_id(1), pl.program_id(2)
        copies = _chip_copies(g_in, g_out, *sems, scatter=False)

        @pl.when((b == 0) & (p == 0) & (i == 0))
        def _():
            for cp in copies:
                cp.start()

        lane = lax.broadcasted_iota(jnp.int32, (1, PAIR), 1)
        rows = lax.broadcasted_iota(jnp.int32, (tq, tq), 0)
        cols = lax.broadcasted_iota(jnp.int32, (tq, tq), 1)
        causal = cols <= rows
        q = q_ref[...]
        hms = [(lane >= HEAD_DIM * hh) & (lane < HEAD_DIM * (hh + 1)) for hh in range(2)]
        qhs = [jnp.where(hm, q, jnp.zeros_like(q)) for hm in hms]
        shifts = [fs_ref[((b * N_PAIR + p) * 2 + hh) * nq + i] for hh in range(2)]

        def block(j, carry, masked):
            start = pl.multiple_of(j * tq, tq)
            k = k_ref[pl.ds(start, tq), :]
            v = v_ref[pl.ds(start, tq), :]
            new = []
            for hh in range(2):
                m, l, acc, acc_lo = carry[hh]
                s = lax.dot_general(qhs[hh], k, (((1,), (1,)), ((), ())), preferred_element_type=F32)
                s = s - (fr_ref[0, 0, hh:hh + 1, pl.ds(start, tq)] - shifts[hh])
                if masked:
                    s = jnp.where(causal, s, NEG)
                m_new = jnp.maximum(m, jnp.max(s, axis=-1, keepdims=True))
                alpha = jnp.exp(m - m_new)
                pe = jnp.exp(s - m_new)
                l = alpha * l + jnp.sum(pe, axis=-1, keepdims=True)
                vh = jnp.where(hms[hh], v, jnp.zeros_like(v))
                pb = pe.astype(MXU_DTYPE)
                p_lo = (pe - pb.astype(F32)).astype(MXU_DTYPE)
                acc = alpha * acc + jnp.dot(pb, vh, preferred_element_type=F32)
                acc_lo = alpha * acc_lo + jnp.dot(p_lo, vh, preferred_element_type=F32)
                new.append((m_new, l, acc, acc_lo))
            return tuple(new)

        init = (jnp.full((tq, 1), NEG, F32), jnp.zeros((tq, 1), F32), jnp.zeros((tq, PAIR), F32),
                jnp.zeros((tq, PAIR), F32))
        carry = lax.fori_loop(0, i, functools.partial(block, masked=False), (init, init))
        carry = block(i, carry, True)
        out = jnp.zeros((tq, PAIR), F32)
        out_x = jnp.zeros((tq, PAIR), F32)
        lse = jnp.zeros((tq, PAIR), F32)
        for hh in range(2):
            m, l, acc, acc_lo = carry[hh]
            inv_l = 1.0 / l
            out = out + acc * inv_l
            out_x = out_x + (acc + acc_lo) * inv_l
            lse = jnp.where(hms[hh], m + jnp.log(l), lse)
        o_ref[...] = out
        ox_ref[...] = out_x
        lse_ref[...] = lse

        @pl.when((b == bl - 1) & (p == N_PAIR - 1) & (i == nq - 1))
        def _():
            for cp in copies:
                cp.wait()

    blk = pl.BlockSpec((tq, PAIR), lambda b, p, i: (b * nq + i, p))
    full = pl.BlockSpec((seq, PAIR), lambda b, p, i: (b, p))
    return pl.pallas_call(
        body, name="attn_fwd", grid=(bl, N_PAIR, nq),
        in_specs=[pl.BlockSpec(memory_space=pltpu.SMEM), blk, full, full,
                  pl.BlockSpec((1, 1, 2, seq), lambda b, p, i: (b, p, 0, 0))] + [pl.BlockSpec(memory_space=pl.ANY)] * n,
        out_specs=[blk, blk, blk] + [pl.BlockSpec(memory_space=pl.ANY)] * n,
        out_shape=[jax.ShapeDtypeStruct((T, ATT_WIDTH), F32)] * 3
        + [jax.ShapeDtypeStruct((4,) + s.shape, s.dtype) for s in shards],
        scratch_shapes=_chip_sems(n),
        compiler_params=_params(("arbitrary", "arbitrary", "arbitrary")),
    )(fstart, qn, kn, vb, frow, *shards)


def _conv_taps(lx, prev8, cw, cb):
    xs = jnp.concatenate([prev8, lx], axis=0)
    shifted = [lx] + [pltpu.roll(xs, k, 0)[8:] for k in range(1, CONV_WIDTH)]
    xc = cb + cw[CONV_WIDTH - 1:CONV_WIDTH] * lx
    for k in range(1, CONV_WIDTH):
        xc = xc + cw[CONV_WIDTH - 1 - k:CONV_WIDTH - k] * shifted[k]
    return xc, shifted


def _lru_gates(xc, wa, ba, wx, bx, lam):
    xb = xc.astype(MXU_DTYPE)
    r = _sigmoid(jnp.dot(xb, wa, preferred_element_type=F32) + ba)
    ig = _sigmoid(jnp.dot(xb, wx, preferred_element_type=F32) + bx)
    sp = jnp.maximum(-lam, 0.0) + jnp.log(1.0 + jnp.exp(-jnp.abs(lam)))
    log_a = -LRU_C * r * sp
    a = jnp.exp(log_a)
    th = jnp.tanh(log_a)
    mult = jnp.sqrt(-2.0 * th / (1.0 - th))
    return r, ig, sp, a, mult


def _gelu_parts(x):
    c0 = math.sqrt(2.0 / math.pi)
    t = jnp.tanh(c0 * (x + 0.044715 * x * x * x))
    g = 0.5 * x * (1.0 + t)
    dg = 0.5 * (1.0 + t) + 0.5 * x * (1.0 - t * t) * c0 * (1.0 + 3.0 * 0.044715 * x * x)
    return g, dg


def _lru_fwd(lx, lg, cw, cb, wa, ba, wx, bx, lam, bl, seq):
    tc = min(LRU_TILE, seq)
    nc = seq // tc
    T = bl * seq

    def body(lx_ref, lxp_ref, lg_ref, cw_ref, cb_ref, wa_ref, ba_ref, wx_ref, bx_ref, lam_ref,
             h_ref, rec_ref, hc_ref):
        i = pl.program_id(1)

        @pl.when(i == 0)
        def _():
            hc_ref[...] = jnp.zeros_like(hc_ref)

        lxv = lx_ref[...]
        prev8 = jnp.where(i > 0, lxp_ref[...], 0.0)
        xc, _ = _conv_taps(lxv, prev8, cw_ref[...], cb_ref[...])
        _, ig, _, a, mult = _lru_gates(xc, wa_ref[...], ba_ref[...], wx_ref[...], bx_ref[...], lam_ref[...])
        u = mult * (ig * xc)
        sub = lax.broadcasted_iota(jnp.int32, (tc, LRU_WIDTH), 0) & 7
        A, B = a, u
        for k in (1, 2, 4):
            a_s = jnp.where(sub >= k, pltpu.roll(A, k, 0), 1.0)
            b_s = jnp.where(sub >= k, pltpu.roll(B, k, 0), 0.0)
            B = A * b_s + B
            A = A * a_s
        carry = hc_ref[0:1, :]
        groups = []
        for g in range(tc // 8):
            hg = A[8 * g:8 * (g + 1)] * carry + B[8 * g:8 * (g + 1)]
            groups.append(hg)
            carry = hg[7:8]
        h = jnp.concatenate(groups, axis=0)
        hc_ref[0:1, :] = carry
        h_ref[...] = h
        g, _ = _gelu_parts(lg_ref[...])
        rec_ref[...] = h * g

    tile = pl.BlockSpec((tc, LRU_WIDTH), lambda b, i: (b * nc + i, 0))
    prev = pl.BlockSpec((8, LRU_WIDTH), lambda b, i: (jnp.maximum((b * seq + i * tc) // 8 - 1, 0), 0))
    vec = _const((1, LRU_WIDTH))
    mat = _const((LRU_WIDTH, LRU_WIDTH))
    return pl.pallas_call(
        body, name="lru_fwd", grid=(bl, nc),
        in_specs=[tile, prev, tile, _const((CONV_WIDTH, LRU_WIDTH)), vec, mat, vec, mat, vec, vec],
        out_specs=[tile, tile],
        out_shape=[jax.ShapeDtypeStruct((T, LRU_WIDTH), F32), jax.ShapeDtypeStruct((T, LRU_WIDTH), F32)],
        scratch_shapes=[pltpu.VMEM((8, LRU_WIDTH), F32)],
        compiler_params=_params(("arbitrary", "arbitrary")),
    )(lx, lx, lg, cw, cb, wa, ba, wx, bx, lam)


def _outproj(x2d, att, rec, ga, gr, wout):
    T = x2d.shape[0]
    tm = TOKEN_TILE

    def body(x_ref, a_ref, r_ref, ga_ref, gr_ref, w_ref, o_ref):
        a = a_ref[...]
        rc = r_ref[...]
        na = a * lax.rsqrt(jnp.mean(a * a, axis=-1, keepdims=True) + NORM_EPS) * ga_ref[...]
        nr = rc * lax.rsqrt(jnp.mean(rc * rc, axis=-1, keepdims=True) + NORM_EPS) * gr_ref[...]
        o_ref[...] = (x_ref[...]
                      + jnp.dot(na.astype(MXU_DTYPE), w_ref[:ATT_WIDTH, :], preferred_element_type=F32)
                      + jnp.dot(nr.astype(MXU_DTYPE), w_ref[ATT_WIDTH:, :], preferred_element_type=F32))

    row = lambda w: pl.BlockSpec((tm, w), lambda i: (i, 0))
    return pl.pallas_call(
        body, name="outproj", grid=(T // tm,),
        in_specs=[row(D_MODEL), row(ATT_WIDTH), row(LRU_WIDTH), _const((1, ATT_WIDTH)), _const((1, LRU_WIDTH)),
                  _const((D_MODEL, D_MODEL))],
        out_specs=row(D_MODEL),
        out_shape=jax.ShapeDtypeStruct((T, D_MODEL), F32),
        compiler_params=_params(("parallel",)),
    )(x2d, att, rec, ga, gr, wout)


def _mlp_fwd(x2, g2, wg, wu, wd, target):
    T = x2.shape[0]
    tm = TOKEN_TILE
    dff = wg.shape[0]

    def body(x_ref, g_ref, wg_ref, wu_ref, wd_ref, t_ref, gt_ref, up_ref, dy_ref, loss_ref):
        @pl.when(pl.program_id(0) == 0)
        def _():
            loss_ref[...] = jnp.zeros_like(loss_ref)

        x = x_ref[...]
        r = lax.rsqrt(jnp.mean(x * x, axis=-1, keepdims=True) + NORM_EPS)
        h = (x * r * g_ref[...]).astype(MXU_DTYPE)
        gt = _nt(h, wg_ref[...])
        up = _nt(h, wu_ref[...])
        gt_ref[...] = gt
        up_ref[...] = up
        act = (gt * _sigmoid(gt) * up).astype(MXU_DTYPE)
        y = x + jnp.dot(act, wd_ref[...], preferred_element_type=F32)
        e = y - t_ref[...]
        dy_ref[...] = e * (1.0 / D_MODEL)
        loss_ref[...] += jnp.sum(e * e)

    row = lambda w: pl.BlockSpec((tm, w), lambda i: (i, 0))
    return pl.pallas_call(
        body, name="mlp_fwd", grid=(T // tm,),
        in_specs=[row(D_MODEL), _const((1, D_MODEL)), _const((dff, D_MODEL)), _const((dff, D_MODEL)),
                  _const((dff, D_MODEL)), row(D_MODEL)],
        out_specs=[row(dff), row(dff), row(D_MODEL), _const((8, 128))],
        out_shape=[jax.ShapeDtypeStruct((T, dff), F32), jax.ShapeDtypeStruct((T, dff), F32),
                   jax.ShapeDtypeStruct((T, D_MODEL), F32), jax.ShapeDtypeStruct((8, 128), F32)],
        compiler_params=_params(("arbitrary",), VMEM_LARGE),
    )(x2, g2, wg, wu, wd, target)


def _mlp_bwd(dy, x2, gt, up, g2, wg, wu, wd):
    T = x2.shape[0]
    tm = TOKEN_TILE
    dff = wg.shape[0]

    def body(dy_ref, x_ref, gt_ref, up_ref, g_ref, wg_ref, wu_ref, wd_ref,
             dx_ref, dxb_ref, dgt_ref, dup_ref, act_ref, h_ref, dyb_ref, dg_ref):
        @pl.when(pl.program_id(0) == 0)
        def _():
            dg_ref[...] = jnp.zeros_like(dg_ref)

        dy_v = dy_ref[...]
        dyb = dy_v.astype(MXU_DTYPE)
        dyb_ref[...] = dyb
        x = x_ref[...]
        r = lax.rsqrt(jnp.mean(x * x, axis=-1, keepdims=True) + NORM_EPS)
        xh = x * r
        h_ref[...] = (xh * g_ref[...]).astype(MXU_DTYPE)
        gt_v = gt_ref[...]
        up_v = up_ref[...]
        sg = _sigmoid(gt_v)
        silu = gt_v * sg
        act_ref[...] = (silu * up_v).astype(MXU_DTYPE)
        dact = _nt(dyb, wd_ref[...])
        dup = (dact * silu).astype(MXU_DTYPE)
        dgt = (dact * up_v * (sg * (1.0 + gt_v * (1.0 - sg)))).astype(MXU_DTYPE)
        dup_ref[...] = dup
        dgt_ref[...] = dgt
        dh = (jnp.dot(dgt, wg_ref[...], preferred_element_type=F32)
              + jnp.dot(dup, wu_ref[...], preferred_element_type=F32))
        dg_ref[...] += jnp.sum(dh * xh, axis=0, keepdims=True)
        dxh = dh * g_ref[...]
        dx = dy_v + r * (dxh - xh * jnp.mean(dxh * xh, axis=-1, keepdims=True))
        dx_ref[...] = dx
        dxb_ref[...] = dx.astype(MXU_DTYPE)

    row = lambda w: pl.BlockSpec((tm, w), lambda i: (i, 0))
    return pl.pallas_call(
        body, name="mlp_bwd", grid=(T // tm,),
        in_specs=[row(D_MODEL), row(D_MODEL), row(dff), row(dff), _const((1, D_MODEL)),
                  _const((dff, D_MODEL)), _const((dff, D_MODEL)), _const((dff, D_MODEL))],
        out_specs=[row(D_MODEL), row(D_MODEL), row(dff), row(dff), row(dff), row(D_MODEL), row(D_MODEL),
                   _const((1, D_MODEL))],
        out_shape=[jax.ShapeDtypeStruct((T, D_MODEL), F32), jax.ShapeDtypeStruct((T, D_MODEL), MXU_DTYPE),
                   jax.ShapeDtypeStruct((T, dff), MXU_DTYPE), jax.ShapeDtypeStruct((T, dff), MXU_DTYPE),
                   jax.ShapeDtypeStruct((T, dff), MXU_DTYPE), jax.ShapeDtypeStruct((T, D_MODEL), MXU_DTYPE),
                   jax.ShapeDtypeStruct((T, D_MODEL), MXU_DTYPE), jax.ShapeDtypeStruct((1, D_MODEL), F32)],
        compiler_params=_params(("arbitrary",), VMEM_LARGE),
    )(dy, x2, gt, up, g2, wg, wu, wd)


def _matmul_tn(a, b, tn, name):
    T, K = a.shape
    N = b.shape[1]
    tt = min(512, T)

    def body(a_ref, b_ref, o_ref):
        @pl.when(pl.program_id(1) == 0)
        def _():
            o_ref[...] = jnp.zeros_like(o_ref)

        o_ref[...] += _tn(a_ref[...], b_ref[...])

    return pl.pallas_call(
        body, name=name, grid=(N // tn, T // tt),
        in_specs=[pl.BlockSpec((tt, K), lambda n, t: (t, 0)), pl.BlockSpec((tt, tn), lambda n, t: (t, n))],
        out_specs=pl.BlockSpec((K, tn), lambda n, t: (0, n)),
        out_shape=jax.ShapeDtypeStruct((K, N), F32),
        compiler_params=_params(("parallel", "arbitrary"), VMEM_LARGE),
    )(a, b)


def _outproj_bwd(dx2b, att, att_x, rec, ga, gr, wout):
    T = att.shape[0]
    tm = TOKEN_TILE

    def body(dx_ref, a_ref, ax_ref, r_ref, ga_ref, gr_ref, w_ref, datt_ref, delta_ref, drec_ref, mix_ref, dga_ref, dgr_ref):
        @pl.when(pl.program_id(0) == 0)
        def _():
            dga_ref[...] = jnp.zeros_like(dga_ref)
            dgr_ref[...] = jnp.zeros_like(dgr_ref)

        dmix = _nt(dx_ref[...], w_ref[...])

        def norm_bwd(v, g, dn):
            rr = lax.rsqrt(jnp.mean(v * v, axis=-1, keepdims=True) + NORM_EPS)
            vh = v * rr
            dvh = dn * g
            dv = rr * (dvh - vh * jnp.mean(dvh * vh, axis=-1, keepdims=True))
            return vh, dv, jnp.sum(dn * vh, axis=0, keepdims=True)

        a = a_ref[...]
        ah, datt, dga = norm_bwd(a, ga_ref[...], dmix[:, :ATT_WIDTH])
        rh, drec, dgr = norm_bwd(r_ref[...], gr_ref[...], dmix[:, ATT_WIDTH:])
        dga_ref[...] += dga
        dgr_ref[...] += dgr
        mix_ref[:, :ATT_WIDTH] = (ah * ga_ref[...]).astype(MXU_DTYPE)
        mix_ref[:, ATT_WIDTH:] = (rh * gr_ref[...]).astype(MXU_DTYPE)
        dattb = datt.astype(MXU_DTYPE)
        datt_ref[...] = dattb
        drec_ref[...] = drec
        lo = _lo_mask()
        prod = dattb.astype(F32) * ax_ref[...]
        for p in range(N_PAIR):
            delta_ref[:, PAIR * p:PAIR * (p + 1)] = _half_sums(prod[:, PAIR * p:PAIR * (p + 1)], lo)

    row = lambda w: pl.BlockSpec((tm, w), lambda i: (i, 0))
    return pl.pallas_call(
        body, name="outproj_bwd", grid=(T // tm,),
        in_specs=[row(D_MODEL), row(ATT_WIDTH), row(ATT_WIDTH), row(LRU_WIDTH), _const((1, ATT_WIDTH)), _const((1, LRU_WIDTH)),
                  _const((D_MODEL, D_MODEL))],
        out_specs=[row(ATT_WIDTH), row(ATT_WIDTH), row(LRU_WIDTH), row(D_MODEL),
                   _const((1, ATT_WIDTH)), _const((1, LRU_WIDTH))],
        out_shape=[jax.ShapeDtypeStruct((T, ATT_WIDTH), MXU_DTYPE), jax.ShapeDtypeStruct((T, ATT_WIDTH), F32),
                   jax.ShapeDtypeStruct((T, LRU_WIDTH), F32), jax.ShapeDtypeStruct((T, D_MODEL), MXU_DTYPE),
                   jax.ShapeDtypeStruct((1, ATT_WIDTH), F32), jax.ShapeDtypeStruct((1, LRU_WIDTH), F32)],
        compiler_params=_params(("arbitrary",)),
    )(dx2b, att, att_x, rec, ga, gr, wout)


def _lru_bwd(drec, lg, h, lx, cw, cb, wa, ba, wx, bx, lam, bl, seq):
    tc = min(LRU_TILE, seq)
    nc = seq // tc
    T = bl * seq
    n = tc

    def body(dr_ref, lg_ref, h_ref, hp_ref, lx_ref, lxp_ref, cw_ref, cb_ref, wa_ref, ba_ref, wx_ref, bx_ref, lam_ref,
             dlx_ref, dlg_ref, dwa_ref, dwx_ref, small_ref, gc_ref, dxn_ref):
        b, i = pl.program_id(0), pl.program_id(1)
        ir = nc - 1 - i

        @pl.when((b == 0) & (i == 0))
        def _():
            dwa_ref[...] = jnp.zeros_like(dwa_ref)
            dwx_ref[...] = jnp.zeros_like(dwx_ref)
            small_ref[...] = jnp.zeros_like(small_ref)

        @pl.when(i == 0)
        def _():
            gc_ref[...] = jnp.zeros_like(gc_ref)
            dxn_ref[...] = jnp.zeros_like(dxn_ref)

        cw = cw_ref[...]
        lam_v = lam_ref[...]
        lxv = lx_ref[...]
        prev8 = jnp.where(ir > 0, lxp_ref[...], 0.0)
        xc, shifted = _conv_taps(lxv, prev8, cw, cb_ref[...])
        r, ig, sp, a, mult = _lru_gates(xc, wa_ref[...], ba_ref[...], wx_ref[...], bx_ref[...], lam_v)
        hv = h_ref[...]
        drv = dr_ref[...]
        g, dg = _gelu_parts(lg_ref[...])
        dlg_ref[...] = drv * hv * dg
        dh = drv * g

        row = lax.broadcasted_iota(jnp.int32, (n, LRU_WIDTH), 0)
        sub = row & 7
        A = jnp.where(row < n - 1, pltpu.roll(a, n - 1, 0), 0.0)
        B = dh + jnp.where(row == n - 1, gc_ref[0:1, :], 0.0)
        for k in (1, 2, 4):
            a_s = jnp.where(sub < 8 - k, pltpu.roll(A, n - k, 0), 1.0)
            b_s = jnp.where(sub < 8 - k, pltpu.roll(B, n - k, 0), 0.0)
            B = B + A * b_s
            A = A * a_s
        carry = jnp.zeros((1, LRU_WIDTH), F32)
        groups = [None] * (n // 8)
        for g in reversed(range(n // 8)):
            gg = B[8 * g:8 * (g + 1)] + A[8 * g:8 * (g + 1)] * carry
            groups[g] = gg
            carry = gg[0:1]
        gs = jnp.concatenate(groups, axis=0)
        gc_ref[0:1, :] = a[0:1, :] * carry

        hprev8 = jnp.where(ir > 0, hp_ref[...], 0.0)
        h_prev = pltpu.roll(jnp.concatenate([hprev8, hv], axis=0), 1, 0)[8:]
        da = gs * h_prev
        ix = ig * xc
        dmult = gs * ix
        dig = gs * mult * xc
        dxc = gs * mult * ig
        dlog_a = da * a - dmult * (a * a) / mult
        dr_gate = dlog_a * (-LRU_C * sp)
        dza = dr_gate * r * (1.0 - r)
        dzx = dig * ig * (1.0 - ig)
        dzab = dza.astype(MXU_DTYPE)
        dzxb = dzx.astype(MXU_DTYPE)
        xcb = xc.astype(MXU_DTYPE)
        dwa_ref[...] += _tn(xcb, dzab)
        dwx_ref[...] += _tn(xcb, dzxb)
        dxc = dxc + _nt(dzab, wa_ref[...]) + _nt(dzxb, wx_ref[...])

        ds = jnp.concatenate([dxc, dxn_ref[...]], axis=0)
        dlx = cw[CONV_WIDTH - 1:CONV_WIDTH] * dxc
        for k in range(1, CONV_WIDTH):
            dlx = dlx + cw[CONV_WIDTH - 1 - k:CONV_WIDTH - k] * pltpu.roll(ds, n + 8 - k, 0)[:n]
        dlx_ref[...] = dlx
        dxn_ref[...] = dxc[0:8, :]

        colsum = lambda v: jnp.sum(v, axis=0, keepdims=True)
        small_ref[0:1, :] += colsum(dza)
        small_ref[1:2, :] += colsum(dzx)
        small_ref[2:3, :] += colsum(dlog_a * r) * (LRU_C * _sigmoid(-lam_v))
        small_ref[3:4, :] += colsum(dxc)
        for k in range(CONV_WIDTH):
            j = CONV_WIDTH - 1 - k
            small_ref[4 + j:5 + j, :] += colsum(dxc * shifted[k])

    tile = pl.BlockSpec((tc, LRU_WIDTH), lambda b, i: (b * nc + (nc - 1 - i), 0))
    prev = pl.BlockSpec((8, LRU_WIDTH), lambda b, i: (jnp.maximum((b * seq + (nc - 1 - i) * tc) // 8 - 1, 0), 0))
    vec = _const((1, LRU_WIDTH))
    mat = _const((LRU_WIDTH, LRU_WIDTH))
    return pl.pallas_call(
        body, name="lru_bwd", grid=(bl, nc),
        in_specs=[tile, tile, tile, prev, tile, prev, _const((CONV_WIDTH, LRU_WIDTH)), vec, mat, vec, mat, vec, vec],
        out_specs=[tile, tile, mat, mat, _const((8, LRU_WIDTH))],
        out_shape=[jax.ShapeDtypeStruct((T, LRU_WIDTH), F32), jax.ShapeDtypeStruct((T, LRU_WIDTH), F32),
                   jax.ShapeDtypeStruct((LRU_WIDTH, LRU_WIDTH), F32), jax.ShapeDtypeStruct((LRU_WIDTH, LRU_WIDTH), F32),
                   jax.ShapeDtypeStruct((8, LRU_WIDTH), F32)],
        scratch_shapes=[pltpu.VMEM((8, LRU_WIDTH), F32), pltpu.VMEM((8, LRU_WIDTH), F32)],
        compiler_params=_params(("arbitrary", "arbitrary")),
    )(drec, lg, h, h, lx, lx, cw, cb, wa, ba, wx, bx, lam)


def _attn_bwd(qn, kn, vb, dob, lse, delta, frow, fstart, bl, seq, slabs, packs):
    tq = min(ATT_TILE, seq)
    nq = seq // tq
    T = bl * seq
    n, npk = len(slabs), len(packs)
    nx = n + npk

    def body(fs_ref, q_ref, k_ref, v_ref, do_ref, lse_ref, dl_ref, fr_ref, *rest):
        x_in, (dq_ref, dk_ref, dv_ref, df_ref), x_out = rest[:nx], rest[nx:nx + 4], rest[nx + 4:2 * nx + 4]
        send_sems, recv_sems, loc_sems, psend, precv = rest[2 * nx + 4:]
        b, p, j = pl.program_id(0), pl.program_id(1), pl.program_id(2)
        copies = _chip_copies(x_in[:n], x_out[:n], send_sems, recv_sems, loc_sems, scatter=True)
        copies += _device_copies(x_in[n:], x_out[n:], psend, precv, loc_sems, n)

        @pl.when((b == 0) & (p == 0) & (j == 0))
        def _():
            for cp in copies:
                cp.start()

        @pl.when(j == 0)
        def _():
            dq_ref[...] = jnp.zeros_like(dq_ref)

        lane = lax.broadcasted_iota(jnp.int32, (1, PAIR), 1)
        rows = lax.broadcasted_iota(jnp.int32, (tq, tq), 0)
        cols = lax.broadcasted_iota(jnp.int32, (tq, tq), 1)
        causal = cols <= rows
        kv = k_ref[...]
        vv = v_ref[...]
        hms = [(lane >= HEAD_DIM * hh) & (lane < HEAD_DIM * (hh + 1)) for hh in range(2)]
        khs = [jnp.where(hm, kv, jnp.zeros_like(kv)) for hm in hms]
        fks = [fr_ref[0, 0, hh:hh + 1, :] for hh in range(2)]
        bases = [((b * N_PAIR + p) * 2 + hh) * nq for hh in range(2)]

        def block(i, carry, masked):
            dk, dv, dfs = carry
            start = pl.multiple_of(i * tq, tq)
            qi = q_ref[pl.ds(start, tq), :]
            doi = do_ref[pl.ds(start, tq), :]
            dq = jnp.zeros((tq, PAIR), F32)
            new_dfs = []
            for hh in range(2):
                c0 = HEAD_DIM * hh
                qh = jnp.where(hms[hh], qi, jnp.zeros_like(qi))
                doh = jnp.where(hms[hh], doi, jnp.zeros_like(doi))
                s = _nt(qh, kv) - (fks[hh] - fs_ref[bases[hh] + i])
                if masked:
                    s = jnp.where(causal, s, NEG)
                pr = jnp.exp(s - lse_ref[pl.ds(start, tq), c0:c0 + 1])
                dp = _nt(doh, vv)
                ds = pr * (dp - dl_ref[pl.ds(start, tq), c0:c0 + 1])
                dsb = ds.astype(MXU_DTYPE)
                dv = dv + _tn(pr.astype(MXU_DTYPE), doh)
                dk = dk + _tn(dsb, qh)
                dq = dq + jnp.dot(dsb, khs[hh], preferred_element_type=F32)
                new_dfs.append(dfs[hh] - jnp.sum(ds, axis=0, keepdims=True))
            dq_ref[pl.ds(start, tq), :] += dq
            return dk, dv, tuple(new_dfs)

        zero = jnp.zeros((tq, PAIR), F32)
        carry = block(j, (zero, zero, (jnp.zeros((1, tq), F32), jnp.zeros((1, tq), F32))), True)
        dk, dv, dfs = lax.fori_loop(j + 1, nq, functools.partial(block, masked=False), carry)
        for hh in range(2):
            df_ref[0, 0, hh:hh + 1, :] = dfs[hh]
        dk_ref[...] = dk
        dv_ref[...] = dv

        @pl.when((b == bl - 1) & (p == N_PAIR - 1) & (j == nq - 1))
        def _():
            for cp in copies:
                cp.wait()

    blk = pl.BlockSpec((tq, PAIR), lambda b, p, j: (b * nq + j, p))
    full = pl.BlockSpec((seq, PAIR), lambda b, p, j: (b, p))
    fblk = pl.BlockSpec((1, 1, 2, tq), lambda b, p, j: (b, p, 0, j))
    hbm = pl.BlockSpec(memory_space=pl.ANY)
    return pl.pallas_call(
        body, name="attn_bwd", grid=(bl, N_PAIR, nq),
        in_specs=[pl.BlockSpec(memory_space=pltpu.SMEM), full, blk, blk, full, full, full, fblk] + [hbm] * nx,
        out_specs=[full, blk, blk, fblk] + [hbm] * nx,
        out_shape=[jax.ShapeDtypeStruct((T, ATT_WIDTH), F32), jax.ShapeDtypeStruct((T, ATT_WIDTH), F32),
                   jax.ShapeDtypeStruct((T, ATT_WIDTH), F32), jax.ShapeDtypeStruct((bl, N_PAIR, 2, seq), F32)]
        + [jax.ShapeDtypeStruct(s.shape, s.dtype) for s in slabs]
        + [jax.ShapeDtypeStruct((8,) + p.shape, p.dtype) for p in packs],
        scratch_shapes=[pltpu.SemaphoreType.DMA((3 * n,)), pltpu.SemaphoreType.DMA((3 * n,)),
                        pltpu.SemaphoreType.DMA((nx,)),
                        pltpu.SemaphoreType.DMA((7 * npk,)), pltpu.SemaphoreType.DMA((7 * npk,))],
        compiler_params=_params(("arbitrary", "arbitrary", "arbitrary"), VMEM_LARGE),
    )(fstart, qn, kn, vb, dob, lse, delta, frow, *slabs, *packs)


def _forget_bwd(dfcol, f2d, bf, bl, seq):
    def body(d_ref, z_ref, b_ref, o_ref, db_ref):
        @pl.when(pl.program_id(0) == 0)
        def _():
            db_ref[...] = jnp.zeros_like(db_ref)

        d = d_ref[...]
        row = lax.broadcasted_iota(jnp.int32, (seq, F_PAD), 0)
        k = 1
        while k < seq:
            d = d + jnp.where(row < seq - k, pltpu.roll(d, seq - k, 0), 0.0)
            k *= 2
        dz = d * _sigmoid(-(z_ref[...] + b_ref[...]))
        o_ref[...] = dz
        db_ref[...] += jnp.sum(dz, axis=0, keepdims=True)

    blk = pl.BlockSpec((seq, F_PAD), lambda b: (b, 0))
    return pl.pallas_call(
        body, name="forget_bwd", grid=(bl,),
        in_specs=[blk, blk, _const((1, F_PAD))],
        out_specs=[blk, _const((1, F_PAD))],
        out_shape=[jax.ShapeDtypeStruct(f2d.shape, F32), jax.ShapeDtypeStruct((1, F_PAD), F32)],
        compiler_params=_params(("arbitrary",)),
    )(dfcol, f2d, bf)


def _inproj_bwd(dq, dk, dv, qkv, df, dlx, dlg, x2d, dx2, g1, gq2, gk2, wcat):
    T = x2d.shape[0]
    tm = TOKEN_TILE

    def body(dq_ref, dk_ref, dv_ref, qkv_ref, df_ref, dlx_ref, dlg_ref, x_ref, dx2_ref, g1_ref, gq_ref, gk_ref, w_ref,
             gx_ref, dp_ref, h_ref, dg1_ref, dgq_ref, dgk_ref):
        @pl.when(pl.program_id(0) == 0)
        def _():
            dg1_ref[...] = jnp.zeros_like(dg1_ref)
            dgq_ref[...] = jnp.zeros_like(dgq_ref)
            dgk_ref[...] = jnp.zeros_like(dgk_ref)

        lo = _lo_mask()

        def head_norm_bwd(t, g2, dy):
            rr = lax.rsqrt(_half_sums(t * t, lo) * (1.0 / HEAD_DIM) + NORM_EPS)
            th = t * rr
            dth = dy * g2
            mm = _half_sums(dth * th, lo) * (1.0 / HEAD_DIM)
            return rr * (dth - th * mm), jnp.sum(dy * th, axis=0, keepdims=True)

        dgq = jnp.zeros((1, PAIR), F32)
        dgk = jnp.zeros((1, PAIR), F32)
        for p in range(N_PAIR):
            cq = slice(PAIR * p, PAIR * (p + 1))
            ck = slice(ATT_WIDTH + PAIR * p, ATT_WIDTH + PAIR * (p + 1))
            dqp, g_ = head_norm_bwd(qkv_ref[:, cq], gq_ref[...], dq_ref[:, cq] * QK_SCALE)
            dgq = dgq + g_
            dp_ref[:, cq] = dqp.astype(MXU_DTYPE)
            dkp, g_ = head_norm_bwd(qkv_ref[:, ck], gk_ref[...], dk_ref[:, cq])
            dgk = dgk + g_
            dp_ref[:, ck] = dkp.astype(MXU_DTYPE)
        dgq_ref[...] += dgq
        dgk_ref[...] += dgk
        f0 = 3 * ATT_WIDTH
        dp_ref[:, 2 * ATT_WIDTH:f0] = dv_ref[...].astype(MXU_DTYPE)
        dp_ref[:, f0:f0 + F_PAD] = df_ref[...].astype(MXU_DTYPE)
        dp_ref[:, f0 + F_PAD:f0 + F_PAD + LRU_WIDTH] = dlx_ref[...].astype(MXU_DTYPE)
        dp_ref[:, f0 + F_PAD + LRU_WIDTH:] = dlg_ref[...].astype(MXU_DTYPE)
        dh = jnp.dot(dp_ref[...], w_ref[...], preferred_element_type=F32)
        x = x_ref[...]
        r = lax.rsqrt(jnp.mean(x * x, axis=-1, keepdims=True) + NORM_EPS)
        xh = x * r
        h_ref[...] = (xh * g1_ref[...]).astype(MXU_DTYPE)
        dg1_ref[...] += jnp.sum(dh * xh, axis=0, keepdims=True)
        dxh = dh * g1_ref[...]
        gx_ref[...] = dx2_ref[...] + r * (dxh - xh * jnp.mean(dxh * xh, axis=-1, keepdims=True))

    row = lambda w: pl.BlockSpec((tm, w), lambda i: (i, 0))
    return pl.pallas_call(
        body, name="inproj_bwd", grid=(T // tm,),
        in_specs=[row(ATT_WIDTH), row(ATT_WIDTH), row(ATT_WIDTH), row(3 * ATT_WIDTH), row(F_PAD), row(LRU_WIDTH),
                  row(LRU_WIDTH), row(D_MODEL), row(D_MODEL), _const((1, D_MODEL)), _const((1, PAIR)), _const((1, PAIR)),
                  _const((N_CAT, D_MODEL))],
        out_specs=[row(D_MODEL), row(N_CAT), row(D_MODEL), _const((1, D_MODEL)), _const((1, PAIR)), _const((1, PAIR))],
        out_shape=[jax.ShapeDtypeStruct((T, D_MODEL), F32), jax.ShapeDtypeStruct((T, N_CAT), MXU_DTYPE),
                   jax.ShapeDtypeStruct((T, D_MODEL), MXU_DTYPE), jax.ShapeDtypeStruct((1, D_MODEL), F32),
                   jax.ShapeDtypeStruct((1, PAIR), F32), jax.ShapeDtypeStruct((1, PAIR), F32)],
        compiler_params=_params(("arbitrary",), VMEM_LARGE),
    )(dq, dk, dv, qkv, df, dlx, dlg, x2d, dx2, g1, gq2, gk2, wcat)


ELEMENTWISE_COLS = 256


def _sum_slabs(recv, name):
    _, rows, cols = recv.shape
    cb = ELEMENTWISE_COLS

    def body(r_ref, o_ref):
        part = [r_ref[s].astype(F32) for s in range(4)]
        o_ref[...] = ((part[0] + part[1]) + part[2]) + part[3]

    return pl.pallas_call(
        body, name=name, grid=(cols // cb,),
        in_specs=[pl.BlockSpec((4, rows, cb), lambda i: (0, 0, i))],
        out_specs=pl.BlockSpec((rows, cb), lambda i: (0, i)),
        out_shape=jax.ShapeDtypeStruct((rows, cols), F32),
        compiler_params=_params(("parallel",)),
    )(recv)


def _adamw_math(w, g, m, v):
    m = ADAM_B1 * m + (1.0 - ADAM_B1) * g
    v = ADAM_B2 * v + (1.0 - ADAM_B2) * (g * g)
    m_hat = m / (1.0 - ADAM_B1 ** ADAM_STEP)
    v_hat = v / (1.0 - ADAM_B2 ** ADAM_STEP)
    delta = -ADAM_LR * (m_hat / (jnp.sqrt(v_hat) + ADAM_EPS) + ADAM_WD * w)
    return delta, m, v


def _adamw_pair(mine, theirs, w, m, v, name):
    rows, cols = w.shape
    cb = ELEMENTWISE_COLS

    def body(a_ref, b_ref, w_ref, m_ref, v_ref, g_ref, d_ref, nm_ref, nv_ref):
        g = a_ref[...] + b_ref[...]
        g_ref[...] = g
        d_ref[...], nm_ref[...], nv_ref[...] = _adamw_math(w_ref[...], g, m_ref[...], v_ref[...])

    blk = pl.BlockSpec((rows, cb), lambda i: (0, i))
    return pl.pallas_call(
        body, name=name, grid=(cols // cb,),
        in_specs=[blk] * 5, out_specs=[blk] * 4,
        out_shape=[jax.ShapeDtypeStruct((rows, cols), F32)] * 4,
        compiler_params=_params(("parallel",)),
    )(mine, theirs, w, m, v)


VEC_ROW = {"norm1_g": 0, "norm2_g": 1, "attn_out_g": 2, "lru_out_g": 3, "q_norm_g": 4, "k_norm_g": 5, "b_f": 6,
           "b_a": 8, "b_x": 9, "lam": 10, "conv_b": 11}
LOSS_ROW, CONV_W_ROW, PACK_ROWS = 7, 12, 16
SMALL = list(VEC_ROW) + ["conv_w", "w_a", "w_x"]


def _pack_small(dg1, dg2, dga, dgr, dgq, dgk, dbf, sq_err, lru_small):
    def body(dg1_ref, dg2_ref, dga_ref, dgr_ref, dgq_ref, dgk_ref, dbf_ref, err_ref, lru_ref, v_ref):
        v_ref[...] = jnp.zeros_like(v_ref)
        v_ref[0:1, :] = dg1_ref[...]
        v_ref[1:2, :] = dg2_ref[...]
        v_ref[2:3, 0:ATT_WIDTH] = dga_ref[...]
        v_ref[3:4, 0:LRU_WIDTH] = dgr_ref[...]
        for row, ref in ((4, dgq_ref), (5, dgk_ref)):
            g = ref[...]
            v_ref[row:row + 1, 0:PAIR] = g + pltpu.roll(g, HEAD_DIM, 1)
        v_ref[6:7, 0:F_PAD] = dbf_ref[...]
        v_ref[LOSS_ROW:LOSS_ROW + 1, 0:128] = err_ref[0:1, :] * (0.5 / D_MODEL)
        v_ref[8:16, 0:LRU_WIDTH] = lru_ref[...]

    ins = [dg1, dg2, dga, dgr, dgq, dgk, dbf, sq_err, lru_small]
    return pl.pallas_call(
        body, name="pack_small", grid=(1,),
        in_specs=[_const(a.shape) for a in ins], out_specs=_const((PACK_ROWS, D_MODEL)),
        out_shape=jax.ShapeDtypeStruct((PACK_ROWS, D_MODEL), F32),
        compiler_params=_params(("arbitrary",)),
    )(*ins)


def _diag_blocks(dwa_bd, dwx_bd):
    blk = LRU_WIDTH // LRU_BLOCKS

    def body(wa_ref, wx_ref, oa_ref, ox_ref):
        for src, dst in ((wa_ref, oa_ref), (wx_ref, ox_ref)):
            for nb in range(LRU_BLOCKS):
                tile = src[blk * nb:blk * (nb + 1), PAIR * (nb // 2):PAIR * (nb // 2 + 1)]
                if nb % 2:
                    tile = pltpu.roll(tile, blk, 1)
                dst[nb] = tile[:, 0:blk]

    out = jax.ShapeDtypeStruct((LRU_BLOCKS, blk, blk), F32)
    return pl.pallas_call(
        body, name="diag_blocks", grid=(1,),
        in_specs=[_const(dwa_bd.shape)] * 2, out_specs=[_const(out.shape)] * 2, out_shape=[out, out],
        compiler_params=_params(("arbitrary",)),
    )(dwa_bd, dwx_bd)


def _adamw_small(recv_v, recv_a, recv_x, params):
    names = list(params)
    flat = [a for n in names for a in params[n]]

    def body(rv_ref, ra_ref, rx_ref, *refs):
        ins, loss_ref, outs = refs[:len(flat)], refs[len(flat)], refs[len(flat) + 1:]
        x, y = lax.axis_index("x"), lax.axis_index("y")
        me = 2 * x + y

        def total(r):
            acc = r[0]
            for d in range(1, 8):
                acc = acc + r[d]
            return acc

        gv, ga, gx = total(rv_ref), total(ra_ref), total(rx_ref)
        loss_ref[...] = gv[LOSS_ROW:LOSS_ROW + 1, 0:128]
        for i, n in enumerate(names):
            w_ref, m_ref, v_ref = ins[3 * i:3 * i + 3]
            g_ref, d_ref, nm_ref, nv_ref = outs[4 * i:4 * i + 4]
            if n in VEC_ROW:
                g = gv[VEC_ROW[n]:VEC_ROW[n] + 1, 0:w_ref.shape[1]]
                w, m, v = w_ref[...], m_ref[...], v_ref[...]
            else:
                if n == "conv_w":
                    full = gv[CONV_W_ROW:CONV_W_ROW + CONV_WIDTH, 0:LRU_WIDTH]
                    width = LRU_WIDTH // 4
                    g = jnp.zeros((CONV_WIDTH, width), F32)
                    for s in range(4):
                        g = jnp.where(me == s, full[:, width * s:width * (s + 1)], g)
                else:
                    g = ga if n == "w_a" else gx
                w, m, v = w_ref[0], m_ref[0], v_ref[0]
            d, nm, nv = _adamw_math(w, g, m, v)
            for ref, val in ((g_ref, g), (d_ref, d), (nm_ref, nm), (nv_ref, nv)):
                if n in VEC_ROW:
                    ref[...] = val
                else:
                    ref[0] = val

    out_shape = [jax.ShapeDtypeStruct((1, 128), F32)] + [jax.ShapeDtypeStruct(params[n][0].shape, F32)
                                                          for n in names for _ in range(4)]
    res = pl.pallas_call(
        body, name="adamw_small", grid=(1,),
        in_specs=[_const(a.shape) for a in (recv_v, recv_a, recv_x, *flat)],
        out_specs=[_const(o.shape) for o in out_shape], out_shape=out_shape,
        compiler_params=_params(("arbitrary",)),
    )(recv_v, recv_a, recv_x, *flat)
    return res[0], {n: tuple(res[1 + 4 * i:5 + 4 * i]) for i, n in enumerate(names)}


def _cat_shards(g, pad_at=None, pad=0):
    _, rows, w = g.shape
    pieces = []
    for s in range(4):
        lo, hi = s * w, (s + 1) * w
        if pad_at is not None and lo < pad_at <= hi:
            pieces += [g[s][:, :pad_at - lo], jnp.zeros((rows, pad), g.dtype)]
            if pad_at < hi:
                pieces.append(g[s][:, pad_at - lo:])
        else:
            pieces.append(g[s])
    return jnp.concatenate(pieces, axis=1)


def _block_diag(w):
    eye = jnp.eye(LRU_BLOCKS, dtype=w.dtype)
    return (w[:, :, None, :] * eye[:, None, :, None]).reshape(LRU_WIDTH, LRU_WIDTH)


def kernel(x, norm1_g, w_in, q_norm_g, k_norm_g, b_f, conv_w, conv_b, w_a, b_a, w_x, b_x, lam, attn_out_g, lru_out_g, w_out, norm2_g, w_gate, w_up, w_down, loss_target, m_norm1_g, m_w_in, m_q_norm_g, m_k_norm_g, m_b_f, m_conv_w, m_conv_b, m_w_a, m_b_a, m_w_x, m_b_x, m_lam, m_attn_out_g, m_lru_out_g, m_w_out, m_norm2_g, m_w_gate, m_w_up, m_w_down, v_norm1_g, v_w_in, v_q_norm_g, v_k_norm_g, v_b_f, v_conv_w, v_conv_b, v_w_a, v_b_a, v_w_x, v_b_x, v_lam, v_attn_out_g, v_lru_out_g, v_w_out, v_norm2_g, v_w_gate, v_w_up, v_w_down):
    args = dict(locals())
    bl, seq, _ = x.shape
    T = bl * seq
    tq = min(ATT_TILE, seq)
    nq = seq // tq
    dff = w_gate.shape[2] * 4

    def transposed(name):
        return name.endswith(("w_in", "w_gate", "w_up"))

    def shard2d(name):
        return jnp.swapaxes(args[name], 1, 2)[0] if transposed(name) else args[name][0]

    g_in, g_cw = _gather_split(shard2d("w_in").astype(MXU_DTYPE), conv_w[0])
    later_shards = [shard2d(n).astype(MXU_DTYPE) for n in ("w_out", "w_gate", "w_up", "w_down")]
    f0 = 3 * ATT_WIDTH
    w_in_t = g_in.reshape(-1, D_MODEL)
    wcat = jnp.concatenate([w_in_t[:f0 + HEADS], jnp.zeros((F_PAD - HEADS, D_MODEL), w_in_t.dtype),
                            w_in_t[f0 + HEADS:]], axis=0)
    cw_full = _cat_shards(g_cw)
    wa_bd = _block_diag(w_a[0]).astype(MXU_DTYPE)
    wx_bd = _block_diag(w_x[0]).astype(MXU_DTYPE)
    gq2 = jnp.tile(q_norm_g, (1, 2))
    gk2 = jnp.tile(k_norm_g, (1, 2))
    bf_pad = jnp.pad(b_f, ((0, 0), (0, F_PAD - HEADS)))

    x2d = x.reshape(T, D_MODEL)
    target2d = loss_target.reshape(T, D_MODEL)

    qkv, qn, kn, vb, f2d, lx, lg = _inproj(x2d, norm1_g, wcat, gq2, gk2)
    fcol = _forget_cumsum(f2d, bf_pad, bl, seq)
    frow = jnp.transpose(fcol.reshape(bl, seq, F_PAD)[:, :, :HEADS], (0, 2, 1)).reshape(bl, N_PAIR, 2, seq)
    fstart = frow[:, :, :, ::tq].reshape(-1)
    att, att_x, lse, g_out, g_gate, g_up, g_down = _attn_fwd(qn, kn, vb, frow, fstart, bl, seq, later_shards)
    wout_full = g_out.reshape(D_MODEL, D_MODEL)
    wg_full, wu_full = g_gate.reshape(dff, D_MODEL), g_up.reshape(dff, D_MODEL)
    wd_full = g_down.reshape(dff, D_MODEL)
    h, rec = _lru_fwd(lx, lg, cw_full, conv_b, wa_bd, b_a, wx_bd, b_x, lam, bl, seq)
    x2 = _outproj(x2d, att, rec, attn_out_g, lru_out_g, wout_full)
    gt, up, dy, sq_err = _mlp_fwd(x2, norm2_g, wg_full, wu_full, wd_full, target2d)

    dx2, dx2b, dgtb, dupb, actb, h2b, dyb, dg2 = _mlp_bwd(dy, x2, gt, up, norm2_g, wg_full, wu_full, wd_full)
    dw_down = _matmul_tn(actb, dyb, D_MODEL, "dw_down")
    dw_gate = _matmul_tn(dgtb, h2b, D_MODEL, "dw_gate")
    dw_up = _matmul_tn(dupb, h2b, D_MODEL, "dw_up")
    dattb, delta, drec, mixb, dga, dgr = _outproj_bwd(dx2b, att, att_x, rec, attn_out_g, lru_out_g, wout_full)
    dw_out = _matmul_tn(mixb, dx2b, D_MODEL, "dw_out")
    dlx, dlg, dwa_bd, dwx_bd, lru_small = _lru_bwd(drec, lg, h, lx, cw_full, conv_b, wa_bd, b_a, wx_bd, b_x, lam, bl, seq)
    early_slabs = [dw_out.reshape(4, D_MODEL // 4, D_MODEL), dw_gate.reshape(4, dff // 4, D_MODEL),
                   dw_up.reshape(4, dff // 4, D_MODEL), dw_down.reshape(4, dff // 4, D_MODEL)]
    pack_a, pack_x = _diag_blocks(dwa_bd, dwx_bd)
    dq, dk, dv, dfrow, *recv_early = _attn_bwd(qn, kn, vb, dattb, lse, delta, frow, fstart, bl, seq, early_slabs,
                                               [pack_a, pack_x])
    recv_early, (recv_a, recv_x) = recv_early[:4], recv_early[4:]
    dfcol = jnp.pad(jnp.transpose(dfrow.reshape(bl, HEADS, seq), (0, 2, 1)), ((0, 0), (0, 0), (0, F_PAD - HEADS)))
    df, dbf = _forget_bwd(dfcol.reshape(T, F_PAD), f2d, bf_pad, bl, seq)
    grad_x, dprojb, h1b, dg1, dgq, dgk = _inproj_bwd(dq, dk, dv, qkv, df, dlx, dlg, x2d, dx2, norm1_g, gq2, gk2, wcat)
    dwcat = _matmul_tn(dprojb, h1b, D_MODEL, "dw_in")

    pack_v = _pack_small(dg1, dg2, dga, dgr, dgq, dgk, dbf, sq_err, lru_small)
    dw_in_slabs = jnp.concatenate([dwcat[:f0 + HEADS], dwcat[f0 + F_PAD:]], axis=0).astype(jnp.bfloat16).reshape(
        4, -1, D_MODEL)
    recv_in, recv_v = _exchange_grads([dw_in_slabs], [pack_v])
    recv = [recv_in] + recv_early
    big = ["w_in", "w_out", "w_gate", "w_up", "w_down"]
    part = [_sum_slabs(r, "sum_" + n) for r, n in zip(recv, big)]
    theirs = _swap_with_sibling(part)
    out = {}
    for n, a, b_ in zip(big, part, theirs):
        res = _adamw_pair(a, b_, shard2d(n), shard2d("m_" + n), shard2d("v_" + n), "adamw_" + n)
        out[n] = tuple(jnp.swapaxes(r[None], 1, 2) if transposed(n) else r[None] for r in res)
    loss_row, small_out = _adamw_small(recv_v, recv_a, recv_x,
                                       {n: (args[n], args["m_" + n], args["v_" + n]) for n in SMALL})
    out.update(small_out)
    loss = loss_row[0, 0]

    order = ["norm1_g", "w_in", "q_norm_g", "k_norm_g", "b_f", "conv_w", "conv_b", "w_a", "b_a", "w_x", "b_x", "lam",
             "attn_out_g", "lru_out_g", "w_out", "norm2_g", "w_gate", "w_up", "w_down"]
    return (loss, grad_x.reshape(bl, seq, D_MODEL), *[out[n][0] for n in order], *[out[n][1] for n in order],
            *[out[n][2] for n in order], *[out[n][3] for n in order])
```

```python
import functools
import math

import jax
import jax.numpy as jnp
from jax import lax
from jax.experimental import pallas as pl
from jax.experimental.pallas import tpu as pltpu

F32 = jnp.float32
MXU_DTYPE = jnp.bfloat16
MESH = pl.DeviceIdType.MESH

D_MODEL = 1024
ATT_WIDTH = 512
LRU_WIDTH = 512
HEADS = 8
HEAD_DIM = 64
PAIR = 2 * HEAD_DIM
N_PAIR = HEADS // 2
LRU_BLOCKS = 8
CONV_WIDTH = 4
LRU_C = 8.0
NORM_EPS = 1e-6
QK_SCALE = 1.0 / math.sqrt(HEAD_DIM)
F_PAD = 128
N_CAT = 3 * ATT_WIDTH + F_PAD + 2 * LRU_WIDTH
NEG = -1e30

ADAM_LR, ADAM_B1, ADAM_B2, ADAM_EPS, ADAM_WD, ADAM_STEP = 0.001, 0.9, 0.999, 1e-08, 0.01, 10

TOKEN_TILE = 256
ATT_TILE = 512
LRU_TILE = 256
VMEM_SMALL = 32 * 1024 * 1024
VMEM_LARGE = 56 * 1024 * 1024


def _params(sem, vmem=VMEM_SMALL):
    return pltpu.CompilerParams(dimension_semantics=sem, vmem_limit_bytes=vmem)


def _const(shape):
    nd = len(shape)
    return pl.BlockSpec(shape, lambda *_: (0,) * nd)


def _sigmoid(x):
    return 1.0 / (1.0 + jnp.exp(-x))


def _nt(a, b):
    return lax.dot_general(a, b, (((1,), (1,)), ((), ())), preferred_element_type=F32)


def _tn(a, b):
    return lax.dot_general(a, b, (((0,), (0,)), ((), ())), preferred_element_type=F32)


def _half_sums(t, lo):
    s_lo = jnp.sum(jnp.where(lo, t, 0.0), axis=-1, keepdims=True)
    s_hi = jnp.sum(jnp.where(lo, 0.0, t), axis=-1, keepdims=True)
    return jnp.where(lo, s_lo, s_hi)


def _lo_mask():
    return lax.broadcasted_iota(jnp.int32, (1, PAIR), 1) < HEAD_DIM


def _other_chips(x, y):
    return [(1 - x, y), (x, 1 - y), (1 - x, 1 - y)]


def _chip_copies(ins, outs, send_sems, recv_sems, loc_sems, scatter):
    x, y, c = lax.axis_index("x"), lax.axis_index("y"), lax.axis_index("c")
    me = 2 * x + y
    copies = []
    for w in range(len(ins)):
        copies.append(pltpu.make_async_copy(ins[w].at[me] if scatter else ins[w], outs[w].at[me], loc_sems.at[w]))
        for k, (cx, cy) in enumerate(_other_chips(x, y)):
            copies.append(pltpu.make_async_remote_copy(
                src_ref=ins[w].at[2 * cx + cy] if scatter else ins[w], dst_ref=outs[w].at[me],
                send_sem=send_sems.at[3 * w + k], recv_sem=recv_sems.at[3 * w + k],
                device_id=(cx, cy, c), device_id_type=MESH))
    return copies


def _chip_sems(n):
    return [pltpu.SemaphoreType.DMA((3 * n,)), pltpu.SemaphoreType.DMA((3 * n,)), pltpu.SemaphoreType.DMA((n,))]


def _gather_split(shard, small):
    half = shard.shape[1] // 2

    def body(w_ref, s_ref, ow_ref, os_ref, ici_send, ici_recv, d2d_send, d2d_recv, sm_send, sm_recv, loc_sems):
        x, y, c = lax.axis_index("x"), lax.axis_index("y"), lax.axis_index("c")
        me = 2 * x + y
        mine = pl.ds(pl.multiple_of(c * half, half), half)
        local = [pltpu.make_async_copy(w_ref, ow_ref.at[me], loc_sems.at[0]),
                 pltpu.make_async_copy(s_ref, os_ref.at[me], loc_sems.at[1])]
        fetch, little, forward = [], [], []
        for k, (cx, cy) in enumerate(_other_chips(x, y)):
            src_chip = 2 * cx + cy
            fetch.append(pltpu.make_async_remote_copy(
                src_ref=w_ref.at[:, mine], dst_ref=ow_ref.at[me, :, mine], send_sem=ici_send.at[k],
                recv_sem=ici_recv.at[k], device_id=(cx, cy, c), device_id_type=MESH))
            little.append(pltpu.make_async_remote_copy(
                src_ref=s_ref, dst_ref=os_ref.at[me], send_sem=sm_send.at[k], recv_sem=sm_recv.at[k],
                device_id=(cx, cy, c), device_id_type=MESH))
            forward.append(pltpu.make_async_remote_copy(
                src_ref=ow_ref.at[src_chip, :, mine], dst_ref=ow_ref.at[src_chip, :, mine], send_sem=d2d_send.at[k],
                recv_sem=d2d_recv.at[k], device_id=(x, y, 1 - c), device_id_type=MESH))
        for cp in local + fetch + little:
            cp.start()
        for k in range(3):
            fetch[k].wait_recv()
            forward[k].start()
        for cp in fetch:
            cp.wait_send()
        for cp in little + forward + local:
            cp.wait()

    return pl.pallas_call(
        body, name="gather_weights",
        out_shape=[jax.ShapeDtypeStruct((4,) + shard.shape, shard.dtype),
                   jax.ShapeDtypeStruct((4,) + small.shape, small.dtype)],
        in_specs=[pl.BlockSpec(memory_space=pl.ANY)] * 2,
        out_specs=[pl.BlockSpec(memory_space=pl.ANY)] * 2,
        scratch_shapes=[pltpu.SemaphoreType.DMA((3,))] * 6 + [pltpu.SemaphoreType.DMA((2,))],
    )(shard, small)


def _device_copies(packs_in, packs_out, psend, precv, loc_sems, loc_base):
    x, y, c = lax.axis_index("x"), lax.axis_index("y"), lax.axis_index("c")
    dev = 4 * x + 2 * y + c
    copies = []
    for j in range(len(packs_in)):
        copies.append(pltpu.make_async_copy(packs_in[j], packs_out[j].at[dev], loc_sems.at[loc_base + j]))
        for k in range(1, 8):
            fx, fy, fc = (k >> 2) & 1, (k >> 1) & 1, k & 1
            tx = (1 - x) if fx else x
            ty = (1 - y) if fy else y
            tc = (1 - c) if fc else c
            copies.append(pltpu.make_async_remote_copy(
                src_ref=packs_in[j], dst_ref=packs_out[j].at[dev],
                send_sem=psend.at[7 * j + k - 1], recv_sem=precv.at[7 * j + k - 1],
                device_id=(tx, ty, tc), device_id_type=MESH))
    return copies


def _exchange_grads(slabs, packs):
    n, npk = len(slabs), len(packs)

    def body(*refs):
        ins, pack_in = refs[:n], refs[n:n + npk]
        outs, pack_out = refs[n + npk:2 * n + npk], refs[2 * n + npk:2 * (n + npk)]
        send_sems, recv_sems, loc_sems, psend, precv = refs[2 * (n + npk):]
        copies = _chip_copies(ins, outs, send_sems, recv_sems, loc_sems, scatter=True)
        copies += _device_copies(pack_in, pack_out, psend, precv, loc_sems, n)
        for cp in copies:
            cp.start()
        for cp in copies:
            cp.wait()

    return pl.pallas_call(
        body, name="exchange_grads",
        out_shape=[jax.ShapeDtypeStruct(s.shape, s.dtype) for s in slabs]
        + [jax.ShapeDtypeStruct((8,) + p.shape, p.dtype) for p in packs],
        in_specs=[pl.BlockSpec(memory_space=pl.ANY)] * (n + npk),
        out_specs=[pl.BlockSpec(memory_space=pl.ANY)] * (n + npk),
        scratch_shapes=[pltpu.SemaphoreType.DMA((3 * n,)), pltpu.SemaphoreType.DMA((3 * n,)),
                        pltpu.SemaphoreType.DMA((n + npk,)),
                        pltpu.SemaphoreType.DMA((7 * npk,)), pltpu.SemaphoreType.DMA((7 * npk,))],
    )(*slabs, *packs)


def _swap_with_sibling(arrs):
    n = len(arrs)

    def body(*refs):
        ins, outs = refs[:n], refs[n:2 * n]
        send_sems, recv_sems = refs[2 * n:]
        x, y, c = lax.axis_index("x"), lax.axis_index("y"), lax.axis_index("c")
        copies = []
        for w in range(n):
            cp = pltpu.make_async_remote_copy(
                src_ref=ins[w], dst_ref=outs[w], send_sem=send_sems.at[w], recv_sem=recv_sems.at[w],
                device_id=(x, y, 1 - c), device_id_type=MESH)
            cp.start()
            copies.append(cp)
        for cp in copies:
            cp.wait()

    return pl.pallas_call(
        body, name="swap_sibling",
        out_shape=[jax.ShapeDtypeStruct(a.shape, a.dtype) for a in arrs],
        in_specs=[pl.BlockSpec(memory_space=pl.ANY)] * n,
        out_specs=[pl.BlockSpec(memory_space=pl.ANY)] * n,
        scratch_shapes=[pltpu.SemaphoreType.DMA((n,)), pltpu.SemaphoreType.DMA((n,))],
    )(*arrs)


def _head_norm(t, g2, lo):
    rr = lax.rsqrt(_half_sums(t * t, lo) * (1.0 / HEAD_DIM) + NORM_EPS)
    return t * rr * g2


def _inproj(x2d, g1, wcat, gq2, gk2):
    T = x2d.shape[0]
    tm = TOKEN_TILE

    def body(x_ref, g1_ref, w_ref, gq_ref, gk_ref, qkv_ref, qn_ref, kn_ref, vb_ref, f_ref, lx_ref, lg_ref):
        x = x_ref[...]
        r = lax.rsqrt(jnp.mean(x * x, axis=-1, keepdims=True) + NORM_EPS)
        h = (x * r * g1_ref[...]).astype(MXU_DTYPE)
        proj = _nt(h, w_ref[...])
        qkv_ref[...] = proj[:, :3 * ATT_WIDTH]
        lo = _lo_mask()
        for p in range(N_PAIR):
            cols = slice(PAIR * p, PAIR * (p + 1))
            q = proj[:, PAIR * p:PAIR * (p + 1)]
            k = proj[:, ATT_WIDTH + PAIR * p:ATT_WIDTH + PAIR * (p + 1)]
            qn_ref[:, cols] = (_head_norm(q, gq_ref[...], lo) * QK_SCALE).astype(MXU_DTYPE)
            kn_ref[:, cols] = _head_norm(k, gk_ref[...], lo).astype(MXU_DTYPE)
        vb_ref[...] = proj[:, 2 * ATT_WIDTH:3 * ATT_WIDTH].astype(MXU_DTYPE)
        f0 = 3 * ATT_WIDTH
        f_ref[...] = proj[:, f0:f0 + F_PAD]
        lx_ref[...] = proj[:, f0 + F_PAD:f0 + F_PAD + LRU_WIDTH]
        lg_ref[...] = proj[:, f0 + F_PAD + LRU_WIDTH:]

    row = lambda w: pl.BlockSpec((tm, w), lambda i: (i, 0))
    return pl.pallas_call(
        body, name="inproj", grid=(T // tm,),
        in_specs=[row(D_MODEL), _const((1, D_MODEL)), _const((N_CAT, D_MODEL)), _const((1, PAIR)), _const((1, PAIR))],
        out_specs=[row(3 * ATT_WIDTH), row(ATT_WIDTH), row(ATT_WIDTH), row(ATT_WIDTH), row(F_PAD),
                   row(LRU_WIDTH), row(LRU_WIDTH)],
        out_shape=[jax.ShapeDtypeStruct((T, 3 * ATT_WIDTH), F32),
                   jax.ShapeDtypeStruct((T, ATT_WIDTH), MXU_DTYPE), jax.ShapeDtypeStruct((T, ATT_WIDTH), MXU_DTYPE),
                   jax.ShapeDtypeStruct((T, ATT_WIDTH), MXU_DTYPE), jax.ShapeDtypeStruct((T, F_PAD), F32),
                   jax.ShapeDtypeStruct((T, LRU_WIDTH), F32), jax.ShapeDtypeStruct((T, LRU_WIDTH), F32)],
        compiler_params=_params(("parallel",), VMEM_LARGE),
    )(x2d, g1, wcat, gq2, gk2)


def _forget_cumsum(f2d, bf, bl, seq):
    def body(z_ref, b_ref, o_ref):
        z = z_ref[...] + b_ref[...]
        lf = jnp.minimum(z, 0.0) - jnp.log(1.0 + jnp.exp(-jnp.abs(z)))
        row = lax.broadcasted_iota(jnp.int32, (seq, F_PAD), 0)
        k = 1
        while k < seq:
            lf = lf + jnp.where(row >= k, pltpu.roll(lf, k, 0), 0.0)
            k *= 2
        o_ref[...] = lf

    return pl.pallas_call(
        body, name="forget_cumsum", grid=(bl,),
        in_specs=[pl.BlockSpec((seq, F_PAD), lambda b: (b, 0)), _const((1, F_PAD))],
        out_specs=pl.BlockSpec((seq, F_PAD), lambda b: (b, 0)),
        out_shape=jax.ShapeDtypeStruct(f2d.shape, F32),
        compiler_params=_params(("parallel",)),
    )(f2d, bf)


def _attn_fwd(qn, kn, vb, frow, fstart, bl, seq, shards):
    tq = min(ATT_TILE, seq)
    nq = seq // tq
    T = bl * seq
    n = len(shards)

    def body(fs_ref, q_ref, k_ref, v_ref, fr_ref, *rest):
        g_in, (o_ref, lse_ref), g_out, sems = rest[:n], rest[n:n + 2], rest[n + 2:2 * n + 2], rest[2 * n + 2:]
        b, p, i = pl.program_id(0), pl.program_id(1), pl.program_id(2)
        copies = _chip_copies(g_in, g_out, *sems, scatter=False)

        @pl.when((b == 0) & (p == 0) & (i == 0))
        def _():
            for cp in copies:
                cp.start()

        lane = lax.broadcasted_iota(jnp.int32, (1, PAIR), 1)
        rows = lax.broadcasted_iota(jnp.int32, (tq, tq), 0)
        cols = lax.broadcasted_iota(jnp.int32, (tq, tq), 1)
        causal = cols <= rows
        q = q_ref[...]
        hms = [(lane >= HEAD_DIM * hh) & (lane < HEAD_DIM * (hh + 1)) for hh in range(2)]
        qhs = [jnp.where(hm, q, jnp.zeros_like(q)) for hm in hms]
        shifts = [fs_ref[((b * N_PAIR + p) * 2 + hh) * nq + i] for hh in range(2)]
        sum_lane = [HEAD_DIM * (1 - hh) for hh in range(2)]

        def block(j, carry, masked):
            start = pl.multiple_of(j * tq, tq)
            k = k_ref[pl.ds(start, tq), :]
            v = v_ref[pl.ds(start, tq), :]
            new = []
            for hh in range(2):
                m, acc = carry[hh]
                s = lax.dot_general(qhs[hh], k, (((1,), (1,)), ((), ())), preferred_element_type=F32)
                s = s - (fr_ref[0, 0, hh:hh + 1, pl.ds(start, tq)] - shifts[hh])
                if masked:
                    s = jnp.where(causal, s, NEG)
                m_new = jnp.maximum(m, jnp.max(s, axis=-1, keepdims=True))
                alpha = jnp.exp(m - m_new)
                pe = jnp.exp(s - m_new)
                vh = jnp.where(hms[hh], v, jnp.where(lane == sum_lane[hh], 1.0, 0.0).astype(v.dtype))
                pb = pe.astype(MXU_DTYPE)
                p_lo = (pe - pb.astype(F32)).astype(MXU_DTYPE)
                acc = (alpha * acc + jnp.dot(pb, vh, preferred_element_type=F32)
                       + jnp.dot(p_lo, vh, preferred_element_type=F32))
                new.append((m_new, acc))
            return tuple(new)

        init = (jnp.full((tq, 1), NEG, F32), jnp.zeros((tq, PAIR), F32))
        carry = lax.fori_loop(0, i, functools.partial(block, masked=False), (init, init))
        carry = block(i, carry, True)
        out = jnp.zeros((tq, PAIR), F32)
        lse = jnp.zeros((tq, PAIR), F32)
        for hh in range(2):
            m, acc = carry[hh]
            l = acc[:, sum_lane[hh]:sum_lane[hh] + 1]
            out = jnp.where(hms[hh], acc * (1.0 / l), out)
            lse = jnp.where(hms[hh], m + jnp.log(l), lse)
        o_ref[...] = out
        lse_ref[...] = lse

        @pl.when((b == bl - 1) & (p == N_PAIR - 1) & (i == nq - 1))
        def _():
            for cp in copies:
                cp.wait()

    blk = pl.BlockSpec((tq, PAIR), lambda b, p, i: (b * nq + i, p))
    full = pl.BlockSpec((seq, PAIR), lambda b, p, i: (b, p))
    return pl.pallas_call(
        body, name="attn_fwd", grid=(bl, N_PAIR, nq),
        in_specs=[pl.BlockSpec(memory_space=pltpu.SMEM), blk, full, full,
                  pl.BlockSpec((1, 1, 2, seq), lambda b, p, i: (b, p, 0, 0))] + [pl.BlockSpec(memory_space=pl.ANY)] * n,
        out_specs=[blk, blk] + [pl.BlockSpec(memory_space=pl.ANY)] * n,
        out_shape=[jax.ShapeDtypeStruct((T, ATT_WIDTH), F32)] * 2
        + [jax.ShapeDtypeStruct((4,) + s.shape, s.dtype) for s in shards],
        scratch_shapes=_chip_sems(n),
        compiler_params=_params(("arbitrary", "arbitrary", "arbitrary")),
    )(fstart, qn, kn, vb, frow, *shards)


def _conv_taps(lx, prev8, cw, cb):
    xs = jnp.concatenate([prev8, lx], axis=0)
    shifted = [lx] + [pltpu.roll(xs, k, 0)[8:] for k in range(1, CONV_WIDTH)]
    xc = cb + cw[CONV_WIDTH - 1:CONV_WIDTH] * lx
    for k in range(1, CONV_WIDTH):
        xc = xc + cw[CONV_WIDTH - 1 - k:CONV_WIDTH - k] * shifted[k]
    return xc, shifted


def _lru_gates(xc, wa, ba, wx, bx, lam):
    xb = xc.astype(MXU_DTYPE)
    r = _sigmoid(jnp.dot(xb, wa, preferred_element_type=F32) + ba)
    ig = _sigmoid(jnp.dot(xb, wx, preferred_element_type=F32) + bx)
    sp = jnp.maximum(-lam, 0.0) + jnp.log(1.0 + jnp.exp(-jnp.abs(lam)))
    log_a = -LRU_C * r * sp
    a = jnp.exp(log_a)
    th = jnp.tanh(log_a)
    mult = jnp.sqrt(-2.0 * th / (1.0 - th))
    return r, ig, sp, a, mult


def _gelu_parts(x):
    c0 = math.sqrt(2.0 / math.pi)
    t = jnp.tanh(c0 * (x + 0.044715 * x * x * x))
    g = 0.5 * x * (1.0 + t)
    dg = 0.5 * (1.0 + t) + 0.5 * x * (1.0 - t * t) * c0 * (1.0 + 3.0 * 0.044715 * x * x)
    return g, dg


def _lru_fwd(lx, lg, cw, cb, wa, ba, wx, bx, lam, bl, seq):
    tc = min(LRU_TILE, seq)
    nc = seq // tc
    T = bl * seq

    def body(lx_ref, lxp_ref, lg_ref, cw_ref, cb_ref, wa_ref, ba_ref, wx_ref, bx_ref, lam_ref,
             h_ref, rec_ref, hc_ref):
        i = pl.program_id(1)

        @pl.when(i == 0)
        def _():
            hc_ref[...] = jnp.zeros_like(hc_ref)

        lxv = lx_ref[...]
        prev8 = jnp.where(i > 0, lxp_ref[...], 0.0)
        xc, _ = _conv_taps(lxv, prev8, cw_ref[...], cb_ref[...])
        _, ig, _, a, mult = _lru_gates(xc, wa_ref[...], ba_ref[...], wx_ref[...], bx_ref[...], lam_ref[...])
        u = mult * (ig * xc)
        sub = lax.broadcasted_iota(jnp.int32, (tc, LRU_WIDTH), 0) & 7
        A, B = a, u
        for k in (1, 2, 4):
            a_s = jnp.where(sub >= k, pltpu.roll(A, k, 0), 1.0)
            b_s = jnp.where(sub >= k, pltpu.roll(B, k, 0), 0.0)
            B = A * b_s + B
            A = A * a_s
        carry = hc_ref[0:1, :]
        groups = []
        for g in range(tc // 8):
            hg = A[8 * g:8 * (g + 1)] * carry + B[8 * g:8 * (g + 1)]
            groups.append(hg)
            carry = hg[7:8]
        h = jnp.concatenate(groups, axis=0)
        hc_ref[0:1, :] = carry
        h_ref[...] = h
        g, _ = _gelu_parts(lg_ref[...])
        rec_ref[...] = h * g

    tile = pl.BlockSpec((tc, LRU_WIDTH), lambda b, i: (b * nc + i, 0))
    prev = pl.BlockSpec((8, LRU_WIDTH), lambda b, i: (jnp.maximum((b * seq + i * tc) // 8 - 1, 0), 0))
    vec = _const((1, LRU_WIDTH))
    mat = _const((LRU_WIDTH, LRU_WIDTH))
    return pl.pallas_call(
        body, name="lru_fwd", grid=(bl, nc),
        in_specs=[tile, prev, tile, _const((CONV_WIDTH, LRU_WIDTH)), vec, mat, vec, mat, vec, vec],
        out_specs=[tile, tile],
        out_shape=[jax.ShapeDtypeStruct((T, LRU_WIDTH), F32), jax.ShapeDtypeStruct((T, LRU_WIDTH), F32)],
        scratch_shapes=[pltpu.VMEM((8, LRU_WIDTH), F32)],
        compiler_params=_params(("arbitrary", "arbitrary")),
    )(lx, lx, lg, cw, cb, wa, ba, wx, bx, lam)


def _outproj(x2d, att, rec, ga, gr, wout):
    T = x2d.shape[0]
    tm = TOKEN_TILE

    def body(x_ref, a_ref, r_ref, ga_ref, gr_ref, w_ref, o_ref):
        a = a_ref[...]
        rc = r_ref[...]
        na = a * lax.rsqrt(jnp.mean(a * a, axis=-1, keepdims=True) + NORM_EPS) * ga_ref[...]
        nr = rc * lax.rsqrt(jnp.mean(rc * rc, axis=-1, keepdims=True) + NORM_EPS) * gr_ref[...]
        o_ref[...] = (x_ref[...]
                      + jnp.dot(na.astype(MXU_DTYPE), w_ref[:ATT_WIDTH, :], preferred_element_type=F32)
                      + jnp.dot(nr.astype(MXU_DTYPE), w_ref[ATT_WIDTH:, :], preferred_element_type=F32))

    row = lambda w: pl.BlockSpec((tm, w), lambda i: (i, 0))
    return pl.pallas_call(
        body, name="outproj", grid=(T // tm,),
        in_specs=[row(D_MODEL), row(ATT_WIDTH), row(LRU_WIDTH), _const((1, ATT_WIDTH)), _const((1, LRU_WIDTH)),
                  _const((D_MODEL, D_MODEL))],
        out_specs=row(D_MODEL),
        out_shape=jax.ShapeDtypeStruct((T, D_MODEL), F32),
        compiler_params=_params(("parallel",)),
    )(x2d, att, rec, ga, gr, wout)


def _mlp_fwd(x2, g2, wg, wu, wd, target):
    T = x2.shape[0]
    tm = TOKEN_TILE
    dff = wg.shape[0]

    def body(x_ref, g_ref, wg_ref, wu_ref, wd_ref, t_ref, gt_ref, up_ref, dy_ref, loss_ref):
        @pl.when(pl.program_id(0) == 0)
        def _():
            loss_ref[...] = jnp.zeros_like(loss_ref)

        x = x_ref[...]
        r = lax.rsqrt(jnp.mean(x * x, axis=-1, keepdims=True) + NORM_EPS)
        h = (x * r * g_ref[...]).astype(MXU_DTYPE)
        gt = _nt(h, wg_ref[...])
        up = _nt(h, wu_ref[...])
        gt_ref[...] = gt
        up_ref[...] = up
        act = (gt * _sigmoid(gt) * up).astype(MXU_DTYPE)
        y = x + jnp.dot(act, wd_ref[...], preferred_element_type=F32)
        e = y - t_ref[...]
        dy_ref[...] = e * (1.0 / D_MODEL)
        loss_ref[...] += jnp.sum(e * e)

    row = lambda w: pl.BlockSpec((tm, w), lambda i: (i, 0))
    return pl.pallas_call(
        body, name="mlp_fwd", grid=(T // tm,),
        in_specs=[row(D_MODEL), _const((1, D_MODEL)), _const((dff, D_MODEL)), _const((dff, D_MODEL)),
                  _const((dff, D_MODEL)), row(D_MODEL)],
        out_specs=[row(dff), row(dff), row(D_MODEL), _const((8, 128))],
        out_shape=[jax.ShapeDtypeStruct((T, dff), F32), jax.ShapeDtypeStruct((T, dff), F32),
                   jax.ShapeDtypeStruct((T, D_MODEL), F32), jax.ShapeDtypeStruct((8, 128), F32)],
        compiler_params=_params(("arbitrary",), VMEM_LARGE),
    )(x2, g2, wg, wu, wd, target)


def _mlp_bwd(dy, x2, gt, up, g2, wg, wu, wd):
    T = x2.shape[0]
    tm = TOKEN_TILE
    dff = wg.shape[0]

    def body(dy_ref, x_ref, gt_ref, up_ref, g_ref, wg_ref, wu_ref, wd_ref,
             dx_ref, dxb_ref, dgt_ref, dup_ref, act_ref, h_ref, dyb_ref, dg_ref):
        @pl.when(pl.program_id(0) == 0)
        def _():
            dg_ref[...] = jnp.zeros_like(dg_ref)

        dy_v = dy_ref[...]
        dyb = dy_v.astype(MXU_DTYPE)
        dyb_ref[...] = dyb
        x = x_ref[...]
        r = lax.rsqrt(jnp.mean(x * x, axis=-1, keepdims=True) + NORM_EPS)
        xh = x * r
        h_ref[...] = (xh * g_ref[...]).astype(MXU_DTYPE)
        gt_v = gt_ref[...]
        up_v = up_ref[...]
        sg = _sigmoid(gt_v)
        silu = gt_v * sg
        act_ref[...] = (silu * up_v).astype(MXU_DTYPE)
        dact = _nt(dyb, wd_ref[...])
        dup = (dact * silu).astype(MXU_DTYPE)
        dgt = (dact * up_v * (sg * (1.0 + gt_v * (1.0 - sg)))).astype(MXU_DTYPE)
        dup_ref[...] = dup
        dgt_ref[...] = dgt
        dh = (jnp.dot(dgt, wg_ref[...], preferred_element_type=F32)
              + jnp.dot(dup, wu_ref[...], preferred_element_type=F32))
        dg_ref[...] += jnp.sum(dh * xh, axis=0, keepdims=True)
        dxh = dh * g_ref[...]
        dx = dy_v + r * (dxh - xh * jnp.mean(dxh * xh, axis=-1, keepdims=True))
        dx_ref[...] = dx
        dxb_ref[...] = dx.astype(MXU_DTYPE)

    row = lambda w: pl.BlockSpec((tm, w), lambda i: (i, 0))
    return pl.pallas_call(
        body, name="mlp_bwd", grid=(T // tm,),
        in_specs=[row(D_MODEL), row(D_MODEL), row(dff), row(dff), _const((1, D_MODEL)),
                  _const((dff, D_MODEL)), _const((dff, D_MODEL)), _const((dff, D_MODEL))],
        out_specs=[row(D_MODEL), row(D_MODEL), row(dff), row(dff), row(dff), row(D_MODEL), row(D_MODEL),
                   _const((1, D_MODEL))],
        out_shape=[jax.ShapeDtypeStruct((T, D_MODEL), F32), jax.ShapeDtypeStruct((T, D_MODEL), MXU_DTYPE),
                   jax.ShapeDtypeStruct((T, dff), MXU_DTYPE), jax.ShapeDtypeStruct((T, dff), MXU_DTYPE),
                   jax.ShapeDtypeStruct((T, dff), MXU_DTYPE), jax.ShapeDtypeStruct((T, D_MODEL), MXU_DTYPE),
                   jax.ShapeDtypeStruct((T, D_MODEL), MXU_DTYPE), jax.ShapeDtypeStruct((1, D_MODEL), F32)],
        compiler_params=_params(("arbitrary",), VMEM_LARGE),
    )(dy, x2, gt, up, g2, wg, wu, wd)


def _matmul_tn(a, b, tn, name):
    T, K = a.shape
    N = b.shape[1]
    tt = min(512, T)

    def body(a_ref, b_ref, o_ref):
        @pl.when(pl.program_id(1) == 0)
        def _():
            o_ref[...] = jnp.zeros_like(o_ref)

        o_ref[...] += _tn(a_ref[...], b_ref[...])

    return pl.pallas_call(
        body, name=name, grid=(N // tn, T // tt),
        in_specs=[pl.BlockSpec((tt, K), lambda n, t: (t, 0)), pl.BlockSpec((tt, tn), lambda n, t: (t, n))],
        out_specs=pl.BlockSpec((K, tn), lambda n, t: (0, n)),
        out_shape=jax.ShapeDtypeStruct((K, N), F32),
        compiler_params=_params(("parallel", "arbitrary"), VMEM_LARGE),
    )(a, b)


def _outproj_bwd(dx2b, att, rec, ga, gr, wout):
    T = att.shape[0]
    tm = TOKEN_TILE

    def body(dx_ref, a_ref, r_ref, ga_ref, gr_ref, w_ref, datt_ref, delta_ref, drec_ref, mix_ref, dga_ref, dgr_ref):
        @pl.when(pl.program_id(0) == 0)
        def _():
            dga_ref[...] = jnp.zeros_like(dga_ref)
            dgr_ref[...] = jnp.zeros_like(dgr_ref)

        dmix = _nt(dx_ref[...], w_ref[...])

        def norm_bwd(v, g, dn):
            rr = lax.rsqrt(jnp.mean(v * v, axis=-1, keepdims=True) + NORM_EPS)
            vh = v * rr
            dvh = dn * g
            dv = rr * (dvh - vh * jnp.mean(dvh * vh, axis=-1, keepdims=True))
            return vh, dv, jnp.sum(dn * vh, axis=0, keepdims=True)

        a = a_ref[...]
        ah, datt, dga = norm_bwd(a, ga_ref[...], dmix[:, :ATT_WIDTH])
        rh, drec, dgr = norm_bwd(r_ref[...], gr_ref[...], dmix[:, ATT_WIDTH:])
        dga_ref[...] += dga
        dgr_ref[...] += dgr
        mix_ref[:, :ATT_WIDTH] = (ah * ga_ref[...]).astype(MXU_DTYPE)
        mix_ref[:, ATT_WIDTH:] = (rh * gr_ref[...]).astype(MXU_DTYPE)
        dattb = datt.astype(MXU_DTYPE)
        datt_ref[...] = dattb
        drec_ref[...] = drec
        lo = _lo_mask()
        prod = dattb.astype(F32) * a
        for p in range(N_PAIR):
            delta_ref[:, PAIR * p:PAIR * (p + 1)] = _half_sums(prod[:, PAIR * p:PAIR * (p + 1)], lo)

    row = lambda w: pl.BlockSpec((tm, w), lambda i: (i, 0))
    return pl.pallas_call(
        body, name="outproj_bwd", grid=(T // tm,),
        in_specs=[row(D_MODEL), row(ATT_WIDTH), row(LRU_WIDTH), _const((1, ATT_WIDTH)), _const((1, LRU_WIDTH)),
                  _const((D_MODEL, D_MODEL))],
        out_specs=[row(ATT_WIDTH), row(ATT_WIDTH), row(LRU_WIDTH), row(D_MODEL),
                   _const((1, ATT_WIDTH)), _const((1, LRU_WIDTH))],
        out_shape=[jax.ShapeDtypeStruct((T, ATT_WIDTH), MXU_DTYPE), jax.ShapeDtypeStruct((T, ATT_WIDTH), F32),
                   jax.ShapeDtypeStruct((T, LRU_WIDTH), F32), jax.ShapeDtypeStruct((T, D_MODEL), MXU_DTYPE),
                   jax.ShapeDtypeStruct((1, ATT_WIDTH), F32), jax.ShapeDtypeStruct((1, LRU_WIDTH), F32)],
        compiler_params=_params(("arbitrary",)),
    )(dx2b, att, rec, ga, gr, wout)


def _lru_bwd(drec, lg, h, lx, cw, cb, wa, ba, wx, bx, lam, bl, seq):
    tc = min(LRU_TILE, seq)
    nc = seq // tc
    T = bl * seq
    n = tc

    def body(dr_ref, lg_ref, h_ref, hp_ref, lx_ref, lxp_ref, cw_ref, cb_ref, wa_ref, ba_ref, wx_ref, bx_ref, lam_ref,
             dlx_ref, dlg_ref, dwa_ref, dwx_ref, small_ref, gc_ref, dxn_ref):
        b, i = pl.program_id(0), pl.program_id(1)
        ir = nc - 1 - i

        @pl.when((b == 0) & (i == 0))
        def _():
            dwa_ref[...] = jnp.zeros_like(dwa_ref)
            dwx_ref[...] = jnp.zeros_like(dwx_ref)
            small_ref[...] = jnp.zeros_like(small_ref)

        @pl.when(i == 0)
        def _():
            gc_ref[...] = jnp.zeros_like(gc_ref)
            dxn_ref[...] = jnp.zeros_like(dxn_ref)

        cw = cw_ref[...]
        lam_v = lam_ref[...]
        lxv = lx_ref[...]
        prev8 = jnp.where(ir > 0, lxp_ref[...], 0.0)
        xc, shifted = _conv_taps(lxv, prev8, cw, cb_ref[...])
        r, ig, sp, a, mult = _lru_gates(xc, wa_ref[...], ba_ref[...], wx_ref[...], bx_ref[...], lam_v)
        hv = h_ref[...]
        drv = dr_ref[...]
        g, dg = _gelu_parts(lg_ref[...])
        dlg_ref[...] = drv * hv * dg
        dh = drv * g

        row = lax.broadcasted_iota(jnp.int32, (n, LRU_WIDTH), 0)
        sub = row & 7
        A = jnp.where(row < n - 1, pltpu.roll(a, n - 1, 0), 0.0)
        B = dh + jnp.where(row == n - 1, gc_ref[0:1, :], 0.0)
        for k in (1, 2, 4):
            a_s = jnp.where(sub < 8 - k, pltpu.roll(A, n - k, 0), 1.0)
            b_s = jnp.where(sub < 8 - k, pltpu.roll(B, n - k, 0), 0.0)
            B = B + A * b_s
            A = A * a_s
        carry = jnp.zeros((1, LRU_WIDTH), F32)
        groups = [None] * (n // 8)
        for g in reversed(range(n // 8)):
            gg = B[8 * g:8 * (g + 1)] + A[8 * g:8 * (g + 1)] * carry
            groups[g] = gg
            carry = gg[0:1]
        gs = jnp.concatenate(groups, axis=0)
        gc_ref[0:1, :] = a[0:1, :] * carry

        hprev8 = jnp.where(ir > 0, hp_ref[...], 0.0)
        h_prev = pltpu.roll(jnp.concatenate([hprev8, hv], axis=0), 1, 0)[8:]
        da = gs * h_prev
        ix = ig * xc
        dmult = gs * ix
        dig = gs * mult * xc
        dxc = gs * mult * ig
        dlog_a = da * a - dmult * (a * a) / mult
        dr_gate = dlog_a * (-LRU_C * sp)
        dza = dr_gate * r * (1.0 - r)
        dzx = dig * ig * (1.0 - ig)
        dzab = dza.astype(MXU_DTYPE)
        dzxb = dzx.astype(MXU_DTYPE)
        xcb = xc.astype(MXU_DTYPE)
        dwa_ref[...] += _tn(xcb, dzab)
        dwx_ref[...] += _tn(xcb, dzxb)
        dxc = dxc + _nt(dzab, wa_ref[...]) + _nt(dzxb, wx_ref[...])

        ds = jnp.concatenate([dxc, dxn_ref[...]], axis=0)
        dlx = cw[CONV_WIDTH - 1:CONV_WIDTH] * dxc
        for k in range(1, CONV_WIDTH):
            dlx = dlx + cw[CONV_WIDTH - 1 - k:CONV_WIDTH - k] * pltpu.roll(ds, n + 8 - k, 0)[:n]
        dlx_ref[...] = dlx
        dxn_ref[...] = dxc[0:8, :]

        colsum = lambda v: jnp.sum(v, axis=0, keepdims=True)
        small_ref[0:1, :] += colsum(dza)
        small_ref[1:2, :] += colsum(dzx)
        small_ref[2:3, :] += colsum(dlog_a * r) * (LRU_C * _sigmoid(-lam_v))
        small_ref[3:4, :] += colsum(dxc)
        for k in range(CONV_WIDTH):
            j = CONV_WIDTH - 1 - k
            small_ref[4 + j:5 + j, :] += colsum(dxc * shifted[k])

    tile = pl.BlockSpec((tc, LRU_WIDTH), lambda b, i: (b * nc + (nc - 1 - i), 0))
    prev = pl.BlockSpec((8, LRU_WIDTH), lambda b, i: (jnp.maximum((b * seq + (nc - 1 - i) * tc) // 8 - 1, 0), 0))
    vec = _const((1, LRU_WIDTH))
    mat = _const((LRU_WIDTH, LRU_WIDTH))
    return pl.pallas_call(
        body, name="lru_bwd", grid=(bl, nc),
        in_specs=[tile, tile, tile, prev, tile, prev, _const((CONV_WIDTH, LRU_WIDTH)), vec, mat, vec, mat, vec, vec],
        out_specs=[tile, tile, mat, mat, _const((8, LRU_WIDTH))],
        out_shape=[jax.ShapeDtypeStruct((T, LRU_WIDTH), F32), jax.ShapeDtypeStruct((T, LRU_WIDTH), F32),
                   jax.ShapeDtypeStruct((LRU_WIDTH, LRU_WIDTH), F32), jax.ShapeDtypeStruct((LRU_WIDTH, LRU_WIDTH), F32),
                   jax.ShapeDtypeStruct((8, LRU_WIDTH), F32)],
        scratch_shapes=[pltpu.VMEM((8, LRU_WIDTH), F32), pltpu.VMEM((8, LRU_WIDTH), F32)],
        compiler_params=_params(("arbitrary", "arbitrary")),
    )(drec, lg, h, h, lx, lx, cw, cb, wa, ba, wx, bx, lam)


def _attn_bwd(qn, kn, vb, dob, lse, delta, frow, fstart, bl, seq, slabs, packs):
    tq = min(ATT_TILE, seq)
    nq = seq // tq
    T = bl * seq
    n, npk = len(slabs), len(packs)
    nx = n + npk

    def body(fs_ref, q_ref, k_ref, v_ref, do_ref, lse_ref, dl_ref, fr_ref, *rest):
        x_in, (dq_ref, dk_ref, dv_ref, df_ref), x_out = rest[:nx], rest[nx:nx + 4], rest[nx + 4:2 * nx + 4]
        send_sems, recv_sems, loc_sems, psend, precv = rest[2 * nx + 4:]
        b, p, j = pl.program_id(0), pl.program_id(1), pl.program_id(2)
        copies = _chip_copies(x_in[:n], x_out[:n], send_sems, recv_sems, loc_sems, scatter=True)
        copies += _device_copies(x_in[n:], x_out[n:], psend, precv, loc_sems, n)

        @pl.when((b == 0) & (p == 0) & (j == 0))
        def _():
            for cp in copies:
                cp.start()

        @pl.when(j == 0)
        def _():
            dq_ref[...] = jnp.zeros_like(dq_ref)

        lane = lax.broadcasted_iota(jnp.int32, (1, PAIR), 1)
        rows = lax.broadcasted_iota(jnp.int32, (tq, tq), 0)
        cols = lax.broadcasted_iota(jnp.int32, (tq, tq), 1)
        causal = cols <= rows
        kv = k_ref[...]
        vv = v_ref[...]
        hms = [(lane >= HEAD_DIM * hh) & (lane < HEAD_DIM * (hh + 1)) for hh in range(2)]
        khs = [jnp.where(hm, kv, jnp.zeros_like(kv)) for hm in hms]
        fks = [fr_ref[0, 0, hh:hh + 1, :] for hh in range(2)]
        bases = [((b * N_PAIR + p) * 2 + hh) * nq for hh in range(2)]

        def block(i, carry, masked):
            dk, dv, dfs = carry
            start = pl.multiple_of(i * tq, tq)
            qi = q_ref[pl.ds(start, tq), :]
            doi = do_ref[pl.ds(start, tq), :]
            dq = jnp.zeros((tq, PAIR), F32)
            new_dfs = []
            for hh in range(2):
                c0 = HEAD_DIM * hh
                qh = jnp.where(hms[hh], qi, jnp.zeros_like(qi))
                doh = jnp.where(hms[hh], doi, jnp.zeros_like(doi))
                s = _nt(qh, kv) - (fks[hh] - fs_ref[bases[hh] + i])
                if masked:
                    s = jnp.where(causal, s, NEG)
                pr = jnp.exp(s - lse_ref[pl.ds(start, tq), c0:c0 + 1])
                dp = _nt(doh, vv)
                ds = pr * (dp - dl_ref[pl.ds(start, tq), c0:c0 + 1])
                dsb = ds.astype(MXU_DTYPE)
                dv = dv + _tn(pr.astype(MXU_DTYPE), doh)
                dk = dk + _tn(dsb, qh)
                dq = dq + jnp.dot(dsb, khs[hh], preferred_element_type=F32)
                new_dfs.append(dfs[hh] - jnp.sum(ds, axis=0, keepdims=True))
            dq_ref[pl.ds(start, tq), :] += dq
            return dk, dv, tuple(new_dfs)

        zero = jnp.zeros((tq, PAIR), F32)
        carry = block(j, (zero, zero, (jnp.zeros((1, tq), F32), jnp.zeros((1, tq), F32))), True)
        dk, dv, dfs = lax.fori_loop(j + 1, nq, functools.partial(block, masked=False), carry)
        for hh in range(2):
            df_ref[0, 0, hh:hh + 1, :] = dfs[hh]
        dk_ref[...] = dk
        dv_ref[...] = dv

        @pl.when((b == bl - 1) & (p == N_PAIR - 1) & (j == nq - 1))
        def _():
            for cp in copies:
                cp.wait()

    blk = pl.BlockSpec((tq, PAIR), lambda b, p, j: (b * nq + j, p))
    full = pl.BlockSpec((seq, PAIR), lambda b, p, j: (b, p))
    fblk = pl.BlockSpec((1, 1, 2, tq), lambda b, p, j: (b, p, 0, j))
    hbm = pl.BlockSpec(memory_space=pl.ANY)
    return pl.pallas_call(
        body, name="attn_bwd", grid=(bl, N_PAIR, nq),
        in_specs=[pl.BlockSpec(memory_space=pltpu.SMEM), full, blk, blk, full, full, full, fblk] + [hbm] * nx,
        out_specs=[full, blk, blk, fblk] + [hbm] * nx,
        out_shape=[jax.ShapeDtypeStruct((T, ATT_WIDTH), F32), jax.ShapeDtypeStruct((T, ATT_WIDTH), F32),
                   jax.ShapeDtypeStruct((T, ATT_WIDTH), F32), jax.ShapeDtypeStruct((bl, N_PAIR, 2, seq), F32)]
        + [jax.ShapeDtypeStruct(s.shape, s.dtype) for s in slabs]
        + [jax.ShapeDtypeStruct((8,) + p.shape, p.dtype) for p in packs],
        scratch_shapes=[pltpu.SemaphoreType.DMA((3 * n,)), pltpu.SemaphoreType.DMA((3 * n,)),
                        pltpu.SemaphoreType.DMA((nx,)),
                        pltpu.SemaphoreType.DMA((7 * npk,)), pltpu.SemaphoreType.DMA((7 * npk,))],
        compiler_params=_params(("arbitrary", "arbitrary", "arbitrary"), VMEM_LARGE),
    )(fstart, qn, kn, vb, dob, lse, delta, frow, *slabs, *packs)


def _forget_bwd(dfcol, f2d, bf, bl, seq):
    def body(d_ref, z_ref, b_ref, o_ref, db_ref):
        @pl.when(pl.program_id(0) == 0)
        def _():
            db_ref[...] = jnp.zeros_like(db_ref)

        d = d_ref[...]
        row = lax.broadcasted_iota(jnp.int32, (seq, F_PAD), 0)
        k = 1
        while k < seq:
            d = d + jnp.where(row < seq - k, pltpu.roll(d, seq - k, 0), 0.0)
            k *= 2
        dz = d * _sigmoid(-(z_ref[...] + b_ref[...]))
        o_ref[...] = dz
        db_ref[...] += jnp.sum(dz, axis=0, keepdims=True)

    blk = pl.BlockSpec((seq, F_PAD), lambda b: (b, 0))
    return pl.pallas_call(
        body, name="forget_bwd", grid=(bl,),
        in_specs=[blk, blk, _const((1, F_PAD))],
        out_specs=[blk, _const((1, F_PAD))],
        out_shape=[jax.ShapeDtypeStruct(f2d.shape, F32), jax.ShapeDtypeStruct((1, F_PAD), F32)],
        compiler_params=_params(("arbitrary",)),
    )(dfcol, f2d, bf)


def _inproj_bwd(dq, dk, dv, qkv, df, dlx, dlg, x2d, dx2, g1, gq2, gk2, wcat):
    T = x2d.shape[0]
    tm = TOKEN_TILE

    def body(dq_ref, dk_ref, dv_ref, qkv_ref, df_ref, dlx_ref, dlg_ref, x_ref, dx2_ref, g1_ref, gq_ref, gk_ref, w_ref,
             gx_ref, dp_ref, h_ref, dg1_ref, dgq_ref, dgk_ref):
        @pl.when(pl.program_id(0) == 0)
        def _():
            dg1_ref[...] = jnp.zeros_like(dg1_ref)
            dgq_ref[...] = jnp.zeros_like(dgq_ref)
            dgk_ref[...] = jnp.zeros_like(dgk_ref)

        lo = _lo_mask()

        def head_norm_bwd(t, g2, dy):
            rr = lax.rsqrt(_half_sums(t * t, lo) * (1.0 / HEAD_DIM) + NORM_EPS)
            th = t * rr
            dth = dy * g2
            mm = _half_sums(dth * th, lo) * (1.0 / HEAD_DIM)
            return rr * (dth - th * mm), jnp.sum(dy * th, axis=0, keepdims=True)

        dgq = jnp.zeros((1, PAIR), F32)
        dgk = jnp.zeros((1, PAIR), F32)
        for p in range(N_PAIR):
            cq = slice(PAIR * p, PAIR * (p + 1))
            ck = slice(ATT_WIDTH + PAIR * p, ATT_WIDTH + PAIR * (p + 1))
            dqp, g_ = head_norm_bwd(qkv_ref[:, cq], gq_ref[...], dq_ref[:, cq] * QK_SCALE)
            dgq = dgq + g_
            dp_ref[:, cq] = dqp.astype(MXU_DTYPE)
            dkp, g_ = head_norm_bwd(qkv_ref[:, ck], gk_ref[...], dk_ref[:, cq])
            dgk = dgk + g_
            dp_ref[:, ck] = dkp.astype(MXU_DTYPE)
        dgq_ref[...] += dgq
        dgk_ref[...] += dgk
        f0 = 3 * ATT_WIDTH
        dp_ref[:, 2 * ATT_WIDTH:f0] = dv_ref[...].astype(MXU_DTYPE)
        dp_ref[:, f0:f0 + F_PAD] = df_ref[...].astype(MXU_DTYPE)
        dp_ref[:, f0 + F_PAD:f0 + F_PAD + LRU_WIDTH] = dlx_ref[...].astype(MXU_DTYPE)
        dp_ref[:, f0 + F_PAD + LRU_WIDTH:] = dlg_ref[...].astype(MXU_DTYPE)
        dh = jnp.dot(dp_ref[...], w_ref[...], preferred_element_type=F32)
        x = x_ref[...]
        r = lax.rsqrt(jnp.mean(x * x, axis=-1, keepdims=True) + NORM_EPS)
        xh = x * r
        h_ref[...] = (xh * g1_ref[...]).astype(MXU_DTYPE)
        dg1_ref[...] += jnp.sum(dh * xh, axis=0, keepdims=True)
        dxh = dh * g1_ref[...]
        gx_ref[...] = dx2_ref[...] + r * (dxh - xh * jnp.mean(dxh * xh, axis=-1, keepdims=True))

    row = lambda w: pl.BlockSpec((tm, w), lambda i: (i, 0))
    return pl.pallas_call(
        body, name="inproj_bwd", grid=(T // tm,),
        in_specs=[row(ATT_WIDTH), row(ATT_WIDTH), row(ATT_WIDTH), row(3 * ATT_WIDTH), row(F_PAD), row(LRU_WIDTH),
                  row(LRU_WIDTH), row(D_MODEL), row(D_MODEL), _const((1, D_MODEL)), _const((1, PAIR)), _const((1, PAIR)),
                  _const((N_CAT, D_MODEL))],
        out_specs=[row(D_MODEL), row(N_CAT), row(D_MODEL), _const((1, D_MODEL)), _const((1, PAIR)), _const((1, PAIR))],
        out_shape=[jax.ShapeDtypeStruct((T, D_MODEL), F32), jax.ShapeDtypeStruct((T, N_CAT), MXU_DTYPE),
                   jax.ShapeDtypeStruct((T, D_MODEL), MXU_DTYPE), jax.ShapeDtypeStruct((1, D_MODEL), F32),
                   jax.ShapeDtypeStruct((1, PAIR), F32), jax.ShapeDtypeStruct((1, PAIR), F32)],
        compiler_params=_params(("arbitrary",), VMEM_LARGE),
    )(dq, dk, dv, qkv, df, dlx, dlg, x2d, dx2, g1, gq2, gk2, wcat)


ELEMENTWISE_COLS = 256


def _sum_slabs(recv, name):
    _, rows, cols = recv.shape
    cb = ELEMENTWISE_COLS

    def body(r_ref, o_ref):
        part = [r_ref[s].astype(F32) for s in range(4)]
        o_ref[...] = ((part[0] + part[1]) + part[2]) + part[3]

    return pl.pallas_call(
        body, name=name, grid=(cols // cb,),
        in_specs=[pl.BlockSpec((4, rows, cb), lambda i: (0, 0, i))],
        out_specs=pl.BlockSpec((rows, cb), lambda i: (0, i)),
        out_shape=jax.ShapeDtypeStruct((rows, cols), F32),
        compiler_params=_params(("parallel",)),
    )(recv)


def _adamw_math(w, g, m, v):
    m = ADAM_B1 * m + (1.0 - ADAM_B1) * g
    v = ADAM_B2 * v + (1.0 - ADAM_B2) * (g * g)
    m_hat = m / (1.0 - ADAM_B1 ** ADAM_STEP)
    v_hat = v / (1.0 - ADAM_B2 ** ADAM_STEP)
    delta = -ADAM_LR * (m_hat / (jnp.sqrt(v_hat) + ADAM_EPS) + ADAM_WD * w)
    return delta, m, v


def _adamw_pair(mine, theirs, w, m, v, name):
    rows, cols = w.shape
    cb = ELEMENTWISE_COLS

    def body(a_ref, b_ref, w_ref, m_ref, v_ref, g_ref, d_ref, nm_ref, nv_ref):
        g = a_ref[...] + b_ref[...]
        g_ref[...] = g
        d_ref[...], nm_ref[...], nv_ref[...] = _adamw_math(w_ref[...], g, m_ref[...], v_ref[...])

    blk = pl.BlockSpec((rows, cb), lambda i: (0, i))
    return pl.pallas_call(
        body, name=name, grid=(cols // cb,),
        in_specs=[blk] * 5, out_specs=[blk] * 4,
        out_shape=[jax.ShapeDtypeStruct((rows, cols), F32)] * 4,
        compiler_params=_params(("parallel",)),
    )(mine, theirs, w, m, v)


VEC_ROW = {"norm1_g": 0, "norm2_g": 1, "attn_out_g": 2, "lru_out_g": 3, "q_norm_g": 4, "k_norm_g": 5, "b_f": 6,
           "b_a": 8, "b_x": 9, "lam": 10, "conv_b": 11}
LOSS_ROW, CONV_W_ROW, PACK_ROWS = 7, 12, 16
SMALL = list(VEC_ROW) + ["conv_w", "w_a", "w_x"]


def _pack_small(dg1, dg2, dga, dgr, dgq, dgk, dbf, sq_err, lru_small):
    def body(dg1_ref, dg2_ref, dga_ref, dgr_ref, dgq_ref, dgk_ref, dbf_ref, err_ref, lru_ref, v_ref):
        v_ref[...] = jnp.zeros_like(v_ref)
        v_ref[0:1, :] = dg1_ref[...]
        v_ref[1:2, :] = dg2_ref[...]
        v_ref[2:3, 0:ATT_WIDTH] = dga_ref[...]
        v_ref[3:4, 0:LRU_WIDTH] = dgr_ref[...]
        for row, ref in ((4, dgq_ref), (5, dgk_ref)):
            g = ref[...]
            v_ref[row:row + 1, 0:PAIR] = g + pltpu.roll(g, HEAD_DIM, 1)
        v_ref[6:7, 0:F_PAD] = dbf_ref[...]
        v_ref[LOSS_ROW:LOSS_ROW + 1, 0:128] = err_ref[0:1, :] * (0.5 / D_MODEL)
        v_ref[8:16, 0:LRU_WIDTH] = lru_ref[...]

    ins = [dg1, dg2, dga, dgr, dgq, dgk, dbf, sq_err, lru_small]
    return pl.pallas_call(
        body, name="pack_small", grid=(1,),
        in_specs=[_const(a.shape) for a in ins], out_specs=_const((PACK_ROWS, D_MODEL)),
        out_shape=jax.ShapeDtypeStruct((PACK_ROWS, D_MODEL), F32),
        compiler_params=_params(("arbitrary",)),
    )(*ins)


def _diag_blocks(dwa_bd, dwx_bd):
    blk = LRU_WIDTH // LRU_BLOCKS

    def body(wa_ref, wx_ref, oa_ref, ox_ref):
        for src, dst in ((wa_ref, oa_ref), (wx_ref, ox_ref)):
            for nb in range(LRU_BLOCKS):
                tile = src[blk * nb:blk * (nb + 1), PAIR * (nb // 2):PAIR * (nb // 2 + 1)]
                if nb % 2:
                    tile = pltpu.roll(tile, blk, 1)
                dst[nb] = tile[:, 0:blk]

    out = jax.ShapeDtypeStruct((LRU_BLOCKS, blk, blk), F32)
    return pl.pallas_call(
        body, name="diag_blocks", grid=(1,),
        in_specs=[_const(dwa_bd.shape)] * 2, out_specs=[_const(out.shape)] * 2, out_shape=[out, out],
        compiler_params=_params(("arbitrary",)),
    )(dwa_bd, dwx_bd)


def _adamw_small(recv_v, recv_a, recv_x, params):
    names = list(params)
    flat = [a for n in names for a in params[n]]

    def body(rv_ref, ra_ref, rx_ref, *refs):
        ins, loss_ref, outs = refs[:len(flat)], refs[len(flat)], refs[len(flat) + 1:]
        x, y = lax.axis_index("x"), lax.axis_index("y")
        me = 2 * x + y

        def total(r):
            acc = r[0]
            for d in range(1, 8):
                acc = acc + r[d]
            return acc

        gv, ga, gx = total(rv_ref), total(ra_ref), total(rx_ref)
        loss_ref[...] = gv[LOSS_ROW:LOSS_ROW + 1, 0:128]
        for i, n in enumerate(names):
            w_ref, m_ref, v_ref = ins[3 * i:3 * i + 3]
            g_ref, d_ref, nm_ref, nv_ref = outs[4 * i:4 * i + 4]
            if n in VEC_ROW:
                g = gv[VEC_ROW[n]:VEC_ROW[n] + 1, 0:w_ref.shape[1]]
                w, m, v = w_ref[...], m_ref[...], v_ref[...]
            else:
                if n == "conv_w":
                    full = gv[CONV_W_ROW:CONV_W_ROW + CONV_WIDTH, 0:LRU_WIDTH]
                    width = LRU_WIDTH // 4
                    g = jnp.zeros((CONV_WIDTH, width), F32)
                    for s in range(4):
                        g = jnp.where(me == s, full[:, width * s:width * (s + 1)], g)
                else:
                    g = ga if n == "w_a" else gx
                w, m, v = w_ref[0], m_ref[0], v_ref[0]
            d, nm, nv = _adamw_math(w, g, m, v)
            for ref, val in ((g_ref, g), (d_ref, d), (nm_ref, nm), (nv_ref, nv)):
                if n in VEC_ROW:
                    ref[...] = val
                else:
                    ref[0] = val

    out_shape = [jax.ShapeDtypeStruct((1, 128), F32)] + [jax.ShapeDtypeStruct(params[n][0].shape, F32)
                                                          for n in names for _ in range(4)]
    res = pl.pallas_call(
        body, name="adamw_small", grid=(1,),
        in_specs=[_const(a.shape) for a in (recv_v, recv_a, recv_x, *flat)],
        out_specs=[_const(o.shape) for o in out_shape], out_shape=out_shape,
        compiler_params=_params(("arbitrary",)),
    )(recv_v, recv_a, recv_x, *flat)
    return res[0], {n: tuple(res[1 + 4 * i:5 + 4 * i]) for i, n in enumerate(names)}


def _cat_shards(g, pad_at=None, pad=0):
    _, rows, w = g.shape
    pieces = []
    for s in range(4):
        lo, hi = s * w, (s + 1) * w
        if pad_at is not None and lo < pad_at <= hi:
            pieces += [g[s][:, :pad_at - lo], jnp.zeros((rows, pad), g.dtype)]
            if pad_at < hi:
                pieces.append(g[s][:, pad_at - lo:])
        else:
            pieces.append(g[s])
    return jnp.concatenate(pieces, axis=1)


def _block_diag(w):
    eye = jnp.eye(LRU_BLOCKS, dtype=w.dtype)
    return (w[:, :, None, :] * eye[:, None, :, None]).reshape(LRU_WIDTH, LRU_WIDTH)


def kernel(x, norm1_g, w_in, q_norm_g, k_norm_g, b_f, conv_w, conv_b, w_a, b_a, w_x, b_x, lam, attn_out_g, lru_out_g, w_out, norm2_g, w_gate, w_up, w_down, loss_target, m_norm1_g, m_w_in, m_q_norm_g, m_k_norm_g, m_b_f, m_conv_w, m_conv_b, m_w_a, m_b_a, m_w_x, m_b_x, m_lam, m_attn_out_g, m_lru_out_g, m_w_out, m_norm2_g, m_w_gate, m_w_up, m_w_down, v_norm1_g, v_w_in, v_q_norm_g, v_k_norm_g, v_b_f, v_conv_w, v_conv_b, v_w_a, v_b_a, v_w_x, v_b_x, v_lam, v_attn_out_g, v_lru_out_g, v_w_out, v_norm2_g, v_w_gate, v_w_up, v_w_down):
    args = dict(locals())
    bl, seq, _ = x.shape
    T = bl * seq
    tq = min(ATT_TILE, seq)
    nq = seq // tq
    dff = w_gate.shape[2] * 4

    def transposed(name):
        return name.endswith(("w_in", "w_gate", "w_up"))

    def shard2d(name):
        return jnp.swapaxes(args[name], 1, 2)[0] if transposed(name) else args[name][0]

    g_in, g_cw = _gather_split(shard2d("w_in").astype(MXU_DTYPE), conv_w[0])
    later_shards = [shard2d(n).astype(MXU_DTYPE) for n in ("w_out", "w_gate", "w_up", "w_down")]
    f0 = 3 * ATT_WIDTH
    w_in_t = g_in.reshape(-1, D_MODEL)
    wcat = jnp.concatenate([w_in_t[:f0 + HEADS], jnp.zeros((F_PAD - HEADS, D_MODEL), w_in_t.dtype),
                            w_in_t[f0 + HEADS:]], axis=0)
    cw_full = _cat_shards(g_cw)
    wa_bd = _block_diag(w_a[0]).astype(MXU_DTYPE)
    wx_bd = _block_diag(w_x[0]).astype(MXU_DTYPE)
    gq2 = jnp.tile(q_norm_g, (1, 2))
    gk2 = jnp.tile(k_norm_g, (1, 2))
    bf_pad = jnp.pad(b_f, ((0, 0), (0, F_PAD - HEADS)))

    x2d = x.reshape(T, D_MODEL)
    target2d = loss_target.reshape(T, D_MODEL)

    qkv, qn, kn, vb, f2d, lx, lg = _inproj(x2d, norm1_g, wcat, gq2, gk2)
    fcol = _forget_cumsum(f2d, bf_pad, bl, seq)
    frow = jnp.transpose(fcol.reshape(bl, seq, F_PAD)[:, :, :HEADS], (0, 2, 1)).reshape(bl, N_PAIR, 2, seq)
    fstart = frow[:, :, :, ::tq].reshape(-1)
    att, lse, g_out, g_gate, g_up, g_down = _attn_fwd(qn, kn, vb, frow, fstart, bl, seq, later_shards)
    wout_full = g_out.reshape(D_MODEL, D_MODEL)
    wg_full, wu_full = g_gate.reshape(dff, D_MODEL), g_up.reshape(dff, D_MODEL)
    wd_full = g_down.reshape(dff, D_MODEL)
    h, rec = _lru_fwd(lx, lg, cw_full, conv_b, wa_bd, b_a, wx_bd, b_x, lam, bl, seq)
    x2 = _outproj(x2d, att, rec, attn_out_g, lru_out_g, wout_full)
    gt, up, dy, sq_err = _mlp_fwd(x2, norm2_g, wg_full, wu_full, wd_full, target2d)

    dx2, dx2b, dgtb, dupb, actb, h2b, dyb, dg2 = _mlp_bwd(dy, x2, gt, up, norm2_g, wg_full, wu_full, wd_full)
    dw_down = _matmul_tn(actb, dyb, D_MODEL, "dw_down")
    dw_gate = _matmul_tn(dgtb, h2b, D_MODEL, "dw_gate")
    dw_up = _matmul_tn(dupb, h2b, D_MODEL, "dw_up")
    dattb, delta, drec, mixb, dga, dgr = _outproj_bwd(dx2b, att, rec, attn_out_g, lru_out_g, wout_full)
    dw_out = _matmul_tn(mixb, dx2b, D_MODEL, "dw_out")
    dlx, dlg, dwa_bd, dwx_bd, lru_small = _lru_bwd(drec, lg, h, lx, cw_full, conv_b, wa_bd, b_a, wx_bd, b_x, lam, bl, seq)
    early_slabs = [dw_out.reshape(4, D_MODEL // 4, D_MODEL), dw_gate.reshape(4, dff // 4, D_MODEL),
                   dw_up.reshape(4, dff // 4, D_MODEL), dw_down.reshape(4, dff // 4, D_MODEL)]
    pack_a, pack_x = _diag_blocks(dwa_bd, dwx_bd)
    dq, dk, dv, dfrow, *recv_early = _attn_bwd(qn, kn, vb, dattb, lse, delta, frow, fstart, bl, seq, early_slabs,
                                               [pack_a, pack_x])
    recv_early, (recv_a, recv_x) = recv_early[:4], recv_early[4:]
    dfcol = jnp.pad(jnp.transpose(dfrow.reshape(bl, HEADS, seq), (0, 2, 1)), ((0, 0), (0, 0), (0, F_PAD - HEADS)))
    df, dbf = _forget_bwd(dfcol.reshape(T, F_PAD), f2d, bf_pad, bl, seq)
    grad_x, dprojb, h1b, dg1, dgq, dgk = _inproj_bwd(dq, dk, dv, qkv, df, dlx, dlg, x2d, dx2, norm1_g, gq2, gk2, wcat)
    dwcat = _matmul_tn(dprojb, h1b, D_MODEL, "dw_in")

    pack_v = _pack_small(dg1, dg2, dga, dgr, dgq, dgk, dbf, sq_err, lru_small)
    dw_in_slabs = jnp.concatenate([dwcat[:f0 + HEADS], dwcat[f0 + F_PAD:]], axis=0).astype(jnp.bfloat16).reshape(
        4, -1, D_MODEL)
    recv_in, recv_v = _exchange_grads([dw_in_slabs], [pack_v])
    recv = [recv_in] + recv_early
    big = ["w_in", "w_out", "w_gate", "w_up", "w_down"]
    part = [_sum_slabs(r, "sum_" + n) for r, n in zip(recv, big)]
    theirs = _swap_with_sibling(part)
    out = {}
    for n, a, b_ in zip(big, part, theirs):
        res = _adamw_pair(a, b_, shard2d(n), shard2d("m_" + n), shard2d("v_" + n), "adamw_" + n)
        out[n] = tuple(jnp.swapaxes(r[None], 1, 2) if transposed(n) else r[None] for r in res)
    loss_row, small_out = _adamw_small(recv_v, recv_a, recv_x,
                                       {n: (args[n], args["m_" + n], args["v_" + n]) for n in SMALL})
    out.update(small_out)
    loss = loss_row[0, 0]

    order = ["norm1_g", "w_in", "q_norm_g", "k_norm_g", "b_f", "conv_w", "conv_b", "w_a", "b_a", "w_x", "b_x", "lam",
             "attn_out_g", "lru_out_g", "w_out", "norm2_g", "w_gate", "w_up", "w_down"]
    return (loss, grad_x.reshape(bl, seq, D_MODEL), *[out[n][0] for n in order], *[out[n][1] for n in order],
            *[out[n][2] for n in order], *[out[n][3] for n in order])
```

```python
import functools
import math

import jax
import jax.numpy as jnp
from jax import lax
from jax.experimental import pallas as pl
from jax.experimental.pallas import tpu as pltpu

F32 = jnp.float32
MXU_DTYPE = jnp.bfloat16
MESH = pl.DeviceIdType.MESH

D_MODEL = 1024
ATT_WIDTH = 512
LRU_WIDTH = 512
HEADS = 8
HEAD_DIM = 64
PAIR = 2 * HEAD_DIM
N_PAIR = HEADS // 2
LRU_BLOCKS = 8
CONV_WIDTH = 4
LRU_C = 8.0
NORM_EPS = 1e-6
QK_SCALE = 1.0 / math.sqrt(HEAD_DIM)
F_PAD = 128
N_CAT = 3 * ATT_WIDTH + F_PAD + 2 * LRU_WIDTH
NEG = -1e30

ADAM_LR, ADAM_B1, ADAM_B2, ADAM_EPS, ADAM_WD, ADAM_STEP = 0.001, 0.9, 0.999, 1e-08, 0.01, 10

TOKEN_TILE = 256
ATT_TILE = 512
LRU_TILE = 256
VMEM_SMALL = 32 * 1024 * 1024
VMEM_LARGE = 56 * 1024 * 1024


def _params(sem, vmem=VMEM_SMALL):
    return pltpu.CompilerParams(dimension_semantics=sem, vmem_limit_bytes=vmem)


def _const(shape):
    nd = len(shape)
    return pl.BlockSpec(shape, lambda *_: (0,) * nd)


def _sigmoid(x):
    return 1.0 / (1.0 + jnp.exp(-x))


def _nt(a, b):
    return lax.dot_general(a, b, (((1,), (1,)), ((), ())), preferred_element_type=F32)


def _tn(a, b):
    return lax.dot_general(a, b, (((0,), (0,)), ((), ())), preferred_element_type=F32)


def _half_sums(t, lo):
    s_lo = jnp.sum(jnp.where(lo, t, 0.0), axis=-1, keepdims=True)
    s_hi = jnp.sum(jnp.where(lo, 0.0, t), axis=-1, keepdims=True)
    return jnp.where(lo, s_lo, s_hi)


def _lo_mask():
    return lax.broadcasted_iota(jnp.int32, (1, PAIR), 1) < HEAD_DIM


def _other_chips(x, y):
    return [(1 - x, y), (x, 1 - y), (1 - x, 1 - y)]


def _chip_copies(ins, outs, send_sems, recv_sems, loc_sems, scatter):
    x, y, c = lax.axis_index("x"), lax.axis_index("y"), lax.axis_index("c")
    me = 2 * x + y
    copies = []
    for w in range(len(ins)):
        copies.append(pltpu.make_async_copy(ins[w].at[me] if scatter else ins[w], outs[w].at[me], loc_sems.at[w]))
        for k, (cx, cy) in enumerate(_other_chips(x, y)):
            copies.append(pltpu.make_async_remote_copy(
                src_ref=ins[w].at[2 * cx + cy] if scatter else ins[w], dst_ref=outs[w].at[me],
                send_sem=send_sems.at[3 * w + k], recv_sem=recv_sems.at[3 * w + k],
                device_id=(cx, cy, c), device_id_type=MESH))
    return copies


def _chip_sems(n):
    return [pltpu.SemaphoreType.DMA((3 * n,)), pltpu.SemaphoreType.DMA((3 * n,)), pltpu.SemaphoreType.DMA((n,))]


def _gather_split(shard, small):
    half = shard.shape[1] // 2

    def body(w_ref, s_ref, ow_ref, os_ref, ici_send, ici_recv, d2d_send, d2d_recv, sm_send, sm_recv, loc_sems):
        x, y, c = lax.axis_index("x"), lax.axis_index("y"), lax.axis_index("c")
        me = 2 * x + y
        mine = pl.ds(pl.multiple_of(c * half, half), half)
        local = [pltpu.make_async_copy(w_ref, ow_ref.at[me], loc_sems.at[0]),
                 pltpu.make_async_copy(s_ref, os_ref.at[me], loc_sems.at[1])]
        fetch, little, forward = [], [], []
        for k, (cx, cy) in enumerate(_other_chips(x, y)):
            src_chip = 2 * cx + cy
            fetch.append(pltpu.make_async_remote_copy(
                src_ref=w_ref.at[:, mine], dst_ref=ow_ref.at[me, :, mine], send_sem=ici_send.at[k],
                recv_sem=ici_recv.at[k], device_id=(cx, cy, c), device_id_type=MESH))
            little.append(pltpu.make_async_remote_copy(
                src_ref=s_ref, dst_ref=os_ref.at[me], send_sem=sm_send.at[k], recv_sem=sm_recv.at[k],
                device_id=(cx, cy, c), device_id_type=MESH))
            forward.append(pltpu.make_async_remote_copy(
                src_ref=ow_ref.at[src_chip, :, mine], dst_ref=ow_ref.at[src_chip, :, mine], send_sem=d2d_send.at[k],
                recv_sem=d2d_recv.at[k], device_id=(x, y, 1 - c), device_id_type=MESH))
        for cp in local + fetch + little:
            cp.start()
        for k in range(3):
            fetch[k].wait_recv()
            forward[k].start()
        for cp in fetch:
            cp.wait_send()
        for cp in little + forward + local:
            cp.wait()

    return pl.pallas_call(
        body, name="gather_weights",
        out_shape=[jax.ShapeDtypeStruct((4,) + shard.shape, shard.dtype),
                   jax.ShapeDtypeStruct((4,) + small.shape, small.dtype)],
        in_specs=[pl.BlockSpec(memory_space=pl.ANY)] * 2,
        out_specs=[pl.BlockSpec(memory_space=pl.ANY)] * 2,
        scratch_shapes=[pltpu.SemaphoreType.DMA((3,))] * 6 + [pltpu.SemaphoreType.DMA((2,))],
    )(shard, small)


def _device_copies(packs_in, packs_out, psend, precv, loc_sems, loc_base):
    x, y, c = lax.axis_index("x"), lax.axis_index("y"), lax.axis_index("c")
    dev = 4 * x + 2 * y + c
    copies = []
    for j in range(len(packs_in)):
        copies.append(pltpu.make_async_copy(packs_in[j], packs_out[j].at[dev], loc_sems.at[loc_base + j]))
        for k in range(1, 8):
            fx, fy, fc = (k >> 2) & 1, (k >> 1) & 1, k & 1
            tx = (1 - x) if fx else x
            ty = (1 - y) if fy else y
            tc = (1 - c) if fc else c
            copies.append(pltpu.make_async_remote_copy(
                src_ref=packs_in[j], dst_ref=packs_out[j].at[dev],
                send_sem=psend.at[7 * j + k - 1], recv_sem=precv.at[7 * j + k - 1],
                device_id=(tx, ty, tc), device_id_type=MESH))
    return copies


def _exchange_grads(slabs, packs):
    n, npk = len(slabs), len(packs)

    def body(*refs):
        ins, pack_in = refs[:n], refs[n:n + npk]
        outs, pack_out = refs[n + npk:2 * n + npk], refs[2 * n + npk:2 * (n + npk)]
        send_sems, recv_sems, loc_sems, psend, precv = refs[2 * (n + npk):]
        copies = _chip_copies(ins, outs, send_sems, recv_sems, loc_sems, scatter=True)
        copies += _device_copies(pack_in, pack_out, psend, precv, loc_sems, n)
        for cp in copies:
            cp.start()
        for cp in copies:
            cp.wait()

    return pl.pallas_call(
        body, name="exchange_grads",
        out_shape=[jax.ShapeDtypeStruct(s.shape, s.dtype) for s in slabs]
        + [jax.ShapeDtypeStruct((8,) + p.shape, p.dtype) for p in packs],
        in_specs=[pl.BlockSpec(memory_space=pl.ANY)] * (n + npk),
        out_specs=[pl.BlockSpec(memory_space=pl.ANY)] * (n + npk),
        scratch_shapes=[pltpu.SemaphoreType.DMA((3 * n,)), pltpu.SemaphoreType.DMA((3 * n,)),
                        pltpu.SemaphoreType.DMA((n + npk,)),
                        pltpu.SemaphoreType.DMA((7 * npk,)), pltpu.SemaphoreType.DMA((7 * npk,))],
    )(*slabs, *packs)


def _swap_with_sibling(arrs):
    n = len(arrs)

    def body(*refs):
        ins, outs = refs[:n], refs[n:2 * n]
        send_sems, recv_sems = refs[2 * n:]
        x, y, c = lax.axis_index("x"), lax.axis_index("y"), lax.axis_index("c")
        copies = []
        for w in range(n):
            cp = pltpu.make_async_remote_copy(
                src_ref=ins[w], dst_ref=outs[w], send_sem=send_sems.at[w], recv_sem=recv_sems.at[w],
                device_id=(x, y, 1 - c), device_id_type=MESH)
            cp.start()
            copies.append(cp)
        for cp in copies:
            cp.wait()

    return pl.pallas_call(
        body, name="swap_sibling",
        out_shape=[jax.ShapeDtypeStruct(a.shape, a.dtype) for a in arrs],
        in_specs=[pl.BlockSpec(memory_space=pl.ANY)] * n,
        out_specs=[pl.BlockSpec(memory_space=pl.ANY)] * n,
        scratch_shapes=[pltpu.SemaphoreType.DMA((n,)), pltpu.SemaphoreType.DMA((n,))],
    )(*arrs)


def _head_norm(t, g2, lo):
    rr = lax.rsqrt(_half_sums(t * t, lo) * (1.0 / HEAD_DIM) + NORM_EPS)
    return t * rr * g2


def _inproj(x2d, g1, wcat, gq2, gk2):
    T = x2d.shape[0]
    tm = TOKEN_TILE

    def body(x_ref, g1_ref, w_ref, gq_ref, gk_ref, qkv_ref, qn_ref, kn_ref, vb_ref, f_ref, lx_ref, lg_ref):
        x = x_ref[...]
        r = lax.rsqrt(jnp.mean(x * x, axis=-1, keepdims=True) + NORM_EPS)
        h = (x * r * g1_ref[...]).astype(MXU_DTYPE)
        proj = _nt(h, w_ref[...])
        qkv_ref[...] = proj[:, :3 * ATT_WIDTH]
        lo = _lo_mask()
        for p in range(N_PAIR):
            cols = slice(PAIR * p, PAIR * (p + 1))
            q = proj[:, PAIR * p:PAIR * (p + 1)]
            k = proj[:, ATT_WIDTH + PAIR * p:ATT_WIDTH + PAIR * (p + 1)]
            qn_ref[:, cols] = (_head_norm(q, gq_ref[...], lo) * QK_SCALE).astype(MXU_DTYPE)
            kn_ref[:, cols] = _head_norm(k, gk_ref[...], lo).astype(MXU_DTYPE)
        vb_ref[...] = proj[:, 2 * ATT_WIDTH:3 * ATT_WIDTH].astype(MXU_DTYPE)
        f0 = 3 * ATT_WIDTH
        f_ref[...] = proj[:, f0:f0 + F_PAD]
        lx_ref[...] = proj[:, f0 + F_PAD:f0 + F_PAD + LRU_WIDTH]
        lg_ref[...] = proj[:, f0 + F_PAD + LRU_WIDTH:]

    row = lambda w: pl.BlockSpec((tm, w), lambda i: (i, 0))
    return pl.pallas_call(
        body, name="inproj", grid=(T // tm,),
        in_specs=[row(D_MODEL), _const((1, D_MODEL)), _const((N_CAT, D_MODEL)), _const((1, PAIR)), _const((1, PAIR))],
        out_specs=[row(3 * ATT_WIDTH), row(ATT_WIDTH), row(ATT_WIDTH), row(ATT_WIDTH), row(F_PAD),
                   row(LRU_WIDTH), row(LRU_WIDTH)],
        out_shape=[jax.ShapeDtypeStruct((T, 3 * ATT_WIDTH), F32),
                   jax.ShapeDtypeStruct((T, ATT_WIDTH), MXU_DTYPE), jax.ShapeDtypeStruct((T, ATT_WIDTH), MXU_DTYPE),
                   jax.ShapeDtypeStruct((T, ATT_WIDTH), MXU_DTYPE), jax.ShapeDtypeStruct((T, F_PAD), F32),
                   jax.ShapeDtypeStruct((T, LRU_WIDTH), F32), jax.ShapeDtypeStruct((T, LRU_WIDTH), F32)],
        compiler_params=_params(("parallel",), VMEM_LARGE),
    )(x2d, g1, wcat, gq2, gk2)


def _forget_cumsum(f2d, bf, bl, seq):
    def body(z_ref, b_ref, o_ref):
        z = z_ref[...] + b_ref[...]
        lf = jnp.minimum(z, 0.0) - jnp.log(1.0 + jnp.exp(-jnp.abs(z)))
        row = lax.broadcasted_iota(jnp.int32, (seq, F_PAD), 0)
        k = 1
        while k < seq:
            lf = lf + jnp.where(row >= k, pltpu.roll(lf, k, 0), 0.0)
            k *= 2
        o_ref[...] = lf

    return pl.pallas_call(
        body, name="forget_cumsum", grid=(bl,),
        in_specs=[pl.BlockSpec((seq, F_PAD), lambda b: (b, 0)), _const((1, F_PAD))],
        out_specs=pl.BlockSpec((seq, F_PAD), lambda b: (b, 0)),
        out_shape=jax.ShapeDtypeStruct(f2d.shape, F32),
        compiler_params=_params(("parallel",)),
    )(f2d, bf)


def _attn_fwd(qn, kn, vb, frow, fstart, bl, seq, shards):
    tq = min(ATT_TILE, seq)
    nq = seq // tq
    T = bl * seq
    n = len(shards)

    def body(fs_ref, q_ref, k_ref, v_ref, fr_ref, *rest):
        g_in, (o_ref, lse_ref), g_out, sems = rest[:n], rest[n:n + 2], rest[n + 2:2 * n + 2], rest[2 * n + 2:]
        b, p, i = pl.program_id(0), pl.program_id(1), pl.program_id(2)
        copies = _chip_copies(g_in, g_out, *sems, scatter=False)

        @pl.when((b == 0) & (p == 0) & (i == 0))
        def _():
            for cp in copies:
                cp.start()

        lane = lax.broadcasted_iota(jnp.int32, (1, PAIR), 1)
        rows = lax.broadcasted_iota(jnp.int32, (tq, tq), 0)
        cols = lax.broadcasted_iota(jnp.int32, (tq, tq), 1)
        causal = cols <= rows
        q = q_ref[...]
        hms = [(lane >= HEAD_DIM * hh) & (lane < HEAD_DIM * (hh + 1)) for hh in range(2)]
        qhs = [jnp.where(hm, q, jnp.zeros_like(q)) for hm in hms]
        shifts = [fs_ref[((b * N_PAIR + p) * 2 + hh) * nq + i] for hh in range(2)]
        sum_lane = [HEAD_DIM * (1 - hh) for hh in range(2)]

        def block(j, carry, masked):
            start = pl.multiple_of(j * tq, tq)
            k = k_ref[pl.ds(start, tq), :]
            v = v_ref[pl.ds(start, tq), :]
            new = []
            for hh in range(2):
                m, acc = carry[hh]
                s = lax.dot_general(qhs[hh], k, (((1,), (1,)), ((), ())), preferred_element_type=F32)
                s = s - (fr_ref[0, 0, hh:hh + 1, pl.ds(start, tq)] - shifts[hh])
                if masked:
                    s = jnp.where(causal, s, NEG)
                m_new = jnp.maximum(m, jnp.max(s, axis=-1, keepdims=True))
                alpha = jnp.exp(m - m_new)
                pe = jnp.exp(s - m_new)
                vh = jnp.where(hms[hh], v, jnp.where(lane == sum_lane[hh], 1.0, 0.0).astype(v.dtype))
                pb = pe.astype(MXU_DTYPE)
                p_lo = (pe - pb.astype(F32)).astype(MXU_DTYPE)
                acc = (alpha * acc + jnp.dot(pb, vh, preferred_element_type=F32)
                       + jnp.dot(p_lo, vh, preferred_element_type=F32))
                new.append((m_new, acc))
            return tuple(new)

        init = (jnp.full((tq, 1), NEG, F32), jnp.zeros((tq, PAIR), F32))
        carry = lax.fori_loop(0, i, functools.partial(block, masked=False), (init, init))
        carry = block(i, carry, True)
        out = jnp.zeros((tq, PAIR), F32)
        lse = jnp.zeros((tq, PAIR), F32)
        for hh in range(2):
            m, acc = carry[hh]
            l = acc[:, sum_lane[hh]:sum_lane[hh] + 1]
            out = jnp.where(hms[hh], acc * (1.0 / l), out)
            lse = jnp.where(hms[hh], m + jnp.log(l), lse)
        o_ref[...] = out
        lse_ref[...] = lse

        @pl.when((b == bl - 1) & (p == N_PAIR - 1) & (i == nq - 1))
        def _():
            for cp in copies:
                cp.wait()

    blk = pl.BlockSpec((tq, PAIR), lambda b, p, i: (b * nq + i, p))
    full = pl.BlockSpec((seq, PAIR), lambda b, p, i: (b, p))
    return pl.pallas_call(
        body, name="attn_fwd", grid=(bl, N_PAIR, nq),
        in_specs=[pl.BlockSpec(memory_space=pltpu.SMEM), blk, full, full,
                  pl.BlockSpec((1, 1, 2, seq), lambda b, p, i: (b, p, 0, 0))] + [pl.BlockSpec(memory_space=pl.ANY)] * n,
        out_specs=[blk, blk] + [pl.BlockSpec(memory_space=pl.ANY)] * n,
        out_shape=[jax.ShapeDtypeStruct((T, ATT_WIDTH), F32)] * 2
        + [jax.ShapeDtypeStruct((4,) + s.shape, s.dtype) for s in shards],
        scratch_shapes=_chip_sems(n),
        compiler_params=_params(("arbitrary", "arbitrary", "arbitrary")),
    )(fstart, qn, kn, vb, frow, *shards)


def _conv_taps(lx, prev8, cw, cb):
    xs = jnp.concatenate([prev8, lx], axis=0)
    shifted = [lx] + [pltpu.roll(xs, k, 0)[8:] for k in range(1, CONV_WIDTH)]
    xc = cb + cw[CONV_WIDTH - 1:CONV_WIDTH] * lx
    for k in range(1, CONV_WIDTH):
        xc = xc + cw[CONV_WIDTH - 1 - k:CONV_WIDTH - k] * shifted[k]
    return xc, shifted


def _lru_gates(xc, wa, ba, wx, bx, lam):
    xb = xc.astype(MXU_DTYPE)
    r = _sigmoid(jnp.dot(xb, wa, preferred_element_type=F32) + ba)
    ig = _sigmoid(jnp.dot(xb, wx, preferred_element_type=F32) + bx)
    sp = jnp.maximum(-lam, 0.0) + jnp.log(1.0 + jnp.exp(-jnp.abs(lam)))
    log_a = -LRU_C * r * sp
    a = jnp.exp(log_a)
    th = jnp.tanh(log_a)
    mult = jnp.sqrt(-2.0 * th / (1.0 - th))
    return r, ig, sp, a, mult


def _gelu_parts(x):
    c0 = math.sqrt(2.0 / math.pi)
    t = jnp.tanh(c0 * (x + 0.044715 * x * x * x))
    g = 0.5 * x * (1.0 + t)
    dg = 0.5 * (1.0 + t) + 0.5 * x * (1.0 - t * t) * c0 * (1.0 + 3.0 * 0.044715 * x * x)
    return g, dg


def _lru_fwd(lx, lg, cw, cb, wa, ba, wx, bx, lam, bl, seq):
    tc = min(LRU_TILE, seq)
    nc = seq // tc
    T = bl * seq

    def body(lx_ref, lxp_ref, lg_ref, cw_ref, cb_ref, wa_ref, ba_ref, wx_ref, bx_ref, lam_ref,
             h_ref, rec_ref, hc_ref):
        i = pl.program_id(1)

        @pl.when(i == 0)
        def _():
            hc_ref[...] = jnp.zeros_like(hc_ref)

        lxv = lx_ref[...]
        prev8 = jnp.where(i > 0, lxp_ref[...], 0.0)
        xc, _ = _conv_taps(lxv, prev8, cw_ref[...], cb_ref[...])
        _, ig, _, a, mult = _lru_gates(xc, wa_ref[...], ba_ref[...], wx_ref[...], bx_ref[...], lam_ref[...])
        u = mult * (ig * xc)
        sub = lax.broadcasted_iota(jnp.int32, (tc, LRU_WIDTH), 0) & 7
        A, B = a, u
        for k in (1, 2, 4):
            a_s = jnp.where(sub >= k, pltpu.roll(A, k, 0), 1.0)
            b_s = jnp.where(sub >= k, pltpu.roll(B, k, 0), 0.0)
            B = A * b_s + B
            A = A * a_s
        carry = hc_ref[0:1, :]
        groups = []
        for g in range(tc // 8):
            hg = A[8 * g:8 * (g + 1)] * carry + B[8 * g:8 * (g + 1)]
            groups.append(hg)
            carry = hg[7:8]
        h = jnp.concatenate(groups, axis=0)
        hc_ref[0:1, :] = carry
        h_ref[...] = h
        g, _ = _gelu_parts(lg_ref[...])
        rec_ref[...] = h * g

    tile = pl.BlockSpec((tc, LRU_WIDTH), lambda b, i: (b * nc + i, 0))
    prev = pl.BlockSpec((8, LRU_WIDTH), lambda b, i: (jnp.maximum((b * seq + i * tc) // 8 - 1, 0), 0))
    vec = _const((1, LRU_WIDTH))
    mat = _const((LRU_WIDTH, LRU_WIDTH))
    return pl.pallas_call(
        body, name="lru_fwd", grid=(bl, nc),
        in_specs=[tile, prev, tile, _const((CONV_WIDTH, LRU_WIDTH)), vec, mat, vec, mat, vec, vec],
        out_specs=[tile, tile],
        out_shape=[jax.ShapeDtypeStruct((T, LRU_WIDTH), F32), jax.ShapeDtypeStruct((T, LRU_WIDTH), F32)],
        scratch_shapes=[pltpu.VMEM((8, LRU_WIDTH), F32)],
        compiler_params=_params(("arbitrary", "arbitrary")),
    )(lx, lx, lg, cw, cb, wa, ba, wx, bx, lam)


def _outproj(x2d, att, rec, ga, gr, wout):
    T = x2d.shape[0]
    tm = TOKEN_TILE

    def body(x_ref, a_ref, r_ref, ga_ref, gr_ref, w_ref, o_ref):
        a = a_ref[...]
        rc = r_ref[...]
        na = a * lax.rsqrt(jnp.mean(a * a, axis=-1, keepdims=True) + NORM_EPS) * ga_ref[...]
        nr = rc * lax.rsqrt(jnp.mean(rc * rc, axis=-1, keepdims=True) + NORM_EPS) * gr_ref[...]
        o_ref[...] = (x_ref[...]
                      + jnp.dot(na.astype(MXU_DTYPE), w_ref[:ATT_WIDTH, :], preferred_element_type=F32)
                      + jnp.dot(nr.astype(MXU_DTYPE), w_ref[ATT_WIDTH:, :], preferred_element_type=F32))

    row = lambda w: pl.BlockSpec((tm, w), lambda i: (i, 0))
    return pl.pallas_call(
        body, name="outproj", grid=(T // tm,),
        in_specs=[row(D_MODEL), row(ATT_WIDTH), row(LRU_WIDTH), _const((1, ATT_WIDTH)), _const((1, LRU_WIDTH)),
                  _const((D_MODEL, D_MODEL))],
        out_specs=row(D_MODEL),
        out_shape=jax.ShapeDtypeStruct((T, D_MODEL), F32),
        compiler_params=_params(("parallel",)),
    )(x2d, att, rec, ga, gr, wout)


def _mlp_fwd(x2, g2, wg, wu, wd, target):
    T = x2.shape[0]
    tm = TOKEN_TILE
    dff = wg.shape[0]

    def body(x_ref, g_ref, wg_ref, wu_ref, wd_ref, t_ref, gt_ref, up_ref, dy_ref, loss_ref):
        @pl.when(pl.program_id(0) == 0)
        def _():
            loss_ref[...] = jnp.zeros_like(loss_ref)

        x = x_ref[...]
        r = lax.rsqrt(jnp.mean(x * x, axis=-1, keepdims=True) + NORM_EPS)
        h = (x * r * g_ref[...]).astype(MXU_DTYPE)
        gt = _nt(h, wg_ref[...])
        up = _nt(h, wu_ref[...])
        gt_ref[...] = gt
        up_ref[...] = up
        act = (gt * _sigmoid(gt) * up).astype(MXU_DTYPE)
        y = x + jnp.dot(act, wd_ref[...], preferred_element_type=F32)
        e = y - t_ref[...]
        dy_ref[...] = e * (1.0 / D_MODEL)
        loss_ref[...] += jnp.sum(e * e)

    row = lambda w: pl.BlockSpec((tm, w), lambda i: (i, 0))
    return pl.pallas_call(
        body, name="mlp_fwd", grid=(T // tm,),
        in_specs=[row(D_MODEL), _const((1, D_MODEL)), _const((dff, D_MODEL)), _const((dff, D_MODEL)),
                  _const((dff, D_MODEL)), row(D_MODEL)],
        out_specs=[row(dff), row(dff), row(D_MODEL), _const((8, 128))],
        out_shape=[jax.ShapeDtypeStruct((T, dff), F32), jax.ShapeDtypeStruct((T, dff), F32),
                   jax.ShapeDtypeStruct((T, D_MODEL), F32), jax.ShapeDtypeStruct((8, 128), F32)],
        compiler_params=_params(("arbitrary",), VMEM_LARGE),
    )(x2, g2, wg, wu, wd, target)


def _mlp_bwd(dy, x2, gt, up, g2, wg, wu, wd):
    T = x2.shape[0]
    tm = TOKEN_TILE
    dff = wg.shape[0]

    def body(dy_ref, x_ref, gt_ref, up_ref, g_ref, wg_ref, wu_ref, wd_ref,
             dx_ref, dxb_ref, dgt_ref, dup_ref, act_ref, h_ref, dyb_ref, dg_ref):
        @pl.when(pl.program_id(0) == 0)
        def _():
            dg_ref[...] = jnp.zeros_like(dg_ref)

        dy_v = dy_ref[...]
        dyb = dy_v.astype(MXU_DTYPE)
        dyb_ref[...] = dyb
        x = x_ref[...]
        r = lax.rsqrt(jnp.mean(x * x, axis=-1, keepdims=True) + NORM_EPS)
        xh = x * r
        h_ref[...] = (xh * g_ref[...]).astype(MXU_DTYPE)
        gt_v = gt_ref[...]
        up_v = up_ref[...]
        sg = _sigmoid(gt_v)
        silu = gt_v * sg
        act_ref[...] = (silu * up_v).astype(MXU_DTYPE)
        dact = _nt(dyb, wd_ref[...])
        dup = (dact * silu).astype(MXU_DTYPE)
        dgt = (dact * up_v * (sg * (1.0 + gt_v * (1.0 - sg)))).astype(MXU_DTYPE)
        dup_ref[...] = dup
        dgt_ref[...] = dgt
        dh = (jnp.dot(dgt, wg_ref[...], preferred_element_type=F32)
              + jnp.dot(dup, wu_ref[...], preferred_element_type=F32))
        dg_ref[...] += jnp.sum(dh * xh, axis=0, keepdims=True)
        dxh = dh * g_ref[...]
        dx = dy_v + r * (dxh - xh * jnp.mean(dxh * xh, axis=-1, keepdims=True))
        dx_ref[...] = dx
        dxb_ref[...] = dx.astype(MXU_DTYPE)

    row = lambda w: pl.BlockSpec((tm, w), lambda i: (i, 0))
    return pl.pallas_call(
        body, name="mlp_bwd", grid=(T // tm,),
        in_specs=[row(D_MODEL), row(D_MODEL), row(dff), row(dff), _const((1, D_MODEL)),
                  _const((dff, D_MODEL)), _const((dff, D_MODEL)), _const((dff, D_MODEL))],
        out_specs=[row(D_MODEL), row(D_MODEL), row(dff), row(dff), row(dff), row(D_MODEL), row(D_MODEL),
                   _const((1, D_MODEL))],
        out_shape=[jax.ShapeDtypeStruct((T, D_MODEL), F32), jax.ShapeDtypeStruct((T, D_MODEL), MXU_DTYPE),
                   jax.ShapeDtypeStruct((T, dff), MXU_DTYPE), jax.ShapeDtypeStruct((T, dff), MXU_DTYPE),
                   jax.ShapeDtypeStruct((T, dff), MXU_DTYPE), jax.ShapeDtypeStruct((T, D_MODEL), MXU_DTYPE),
                   jax.ShapeDtypeStruct((T, D_MODEL), MXU_DTYPE), jax.ShapeDtypeStruct((1, D_MODEL), F32)],
        compiler_params=_params(("arbitrary",), VMEM_LARGE),
    )(dy, x2, gt, up, g2, wg, wu, wd)


def _matmul_tn(a, b, tn, name):
    T, K = a.shape
    N = b.shape[1]
    tt = min(512, T)

    def body(a_ref, b_ref, o_ref):
        @pl.when(pl.program_id(1) == 0)
        def _():
            o_ref[...] = jnp.zeros_like(o_ref)

        o_ref[...] += _tn(a_ref[...], b_ref[...])

    return pl.pallas_call(
        body, name=name, grid=(N // tn, T // tt),
        in_specs=[pl.BlockSpec((tt, K), lambda n, t: (t, 0)), pl.BlockSpec((tt, tn), lambda n, t: (t, n))],
        out_specs=pl.BlockSpec((K, tn), lambda n, t: (0, n)),
        out_shape=jax.ShapeDtypeStruct((K, N), F32),
        compiler_params=_params(("parallel", "arbitrary"), VMEM_LARGE),
    )(a, b)


def _outproj_bwd(dx2b, att, rec, ga, gr, wout):
    T = att.shape[0]
    tm = TOKEN_TILE

    def body(dx_ref, a_ref, r_ref, ga_ref, gr_ref, w_ref, datt_ref, delta_ref, drec_ref, mix_ref, dga_ref, dgr_ref):
        @pl.when(pl.program_id(0) == 0)
        def _():
            dga_ref[...] = jnp.zeros_like(dga_ref)
            dgr_ref[...] = jnp.zeros_like(dgr_ref)

        dmix = _nt(dx_ref[...], w_ref[...])

        def norm_bwd(v, g, dn):
            rr = lax.rsqrt(jnp.mean(v * v, axis=-1, keepdims=True) + NORM_EPS)
            vh = v * rr
            dvh = dn * g
            dv = rr * (dvh - vh * jnp.mean(dvh * vh, axis=-1, keepdims=True))
            return vh, dv, jnp.sum(dn * vh, axis=0, keepdims=True)

        a = a_ref[...]
        ah, datt, dga = norm_bwd(a, ga_ref[...], dmix[:, :ATT_WIDTH])
        rh, drec, dgr = norm_bwd(r_ref[...], gr_ref[...], dmix[:, ATT_WIDTH:])
        dga_ref[...] += dga
        dgr_ref[...] += dgr
        mix_ref[:, :ATT_WIDTH] = (ah * ga_ref[...]).astype(MXU_DTYPE)
        mix_ref[:, ATT_WIDTH:] = (rh * gr_ref[...]).astype(MXU_DTYPE)
        dattb = datt.astype(MXU_DTYPE)
        datt_ref[...] = dattb
        drec_ref[...] = drec
        lo = _lo_mask()
        prod = dattb.astype(F32) * a
        for p in range(N_PAIR):
            delta_ref[:, PAIR * p:PAIR * (p + 1)] = _half_sums(prod[:, PAIR * p:PAIR * (p + 1)], lo)

    row = lambda w: pl.BlockSpec((tm, w), lambda i: (i, 0))
    return pl.pallas_call(
        body, name="outproj_bwd", grid=(T // tm,),
        in_specs=[row(D_MODEL), row(ATT_WIDTH), row(LRU_WIDTH), _const((1, ATT_WIDTH)), _const((1, LRU_WIDTH)),
                  _const((D_MODEL, D_MODEL))],
        out_specs=[row(ATT_WIDTH), row(ATT_WIDTH), row(LRU_WIDTH), row(D_MODEL),
                   _const((1, ATT_WIDTH)), _const((1, LRU_WIDTH))],
        out_shape=[jax.ShapeDtypeStruct((T, ATT_WIDTH), MXU_DTYPE), jax.ShapeDtypeStruct((T, ATT_WIDTH), F32),
                   jax.ShapeDtypeStruct((T, LRU_WIDTH), F32), jax.ShapeDtypeStruct((T, D_MODEL), MXU_DTYPE),
                   jax.ShapeDtypeStruct((1, ATT_WIDTH), F32), jax.ShapeDtypeStruct((1, LRU_WIDTH), F32)],
        compiler_params=_params(("arbitrary",)),
    )(dx2b, att, rec, ga, gr, wout)


def _lru_bwd(drec, lg, h, lx, cw, cb, wa, ba, wx, bx, lam, bl, seq):
    tc = min(LRU_TILE, seq)
    nc = seq // tc
    T = bl * seq
    n = tc

    def body(dr_ref, lg_ref, h_ref, hp_ref, lx_ref, lxp_ref, cw_ref, cb_ref, wa_ref, ba_ref, wx_ref, bx_ref, lam_ref,
             dlx_ref, dlg_ref, dwa_ref, dwx_ref, small_ref, gc_ref, dxn_ref):
        b, i = pl.program_id(0), pl.program_id(1)
        ir = nc - 1 - i

        @pl.when((b == 0) & (i == 0))
        def _():
            dwa_ref[...] = jnp.zeros_like(dwa_ref)
            dwx_ref[...] = jnp.zeros_like(dwx_ref)
            small_ref[...] = jnp.zeros_like(small_ref)

        @pl.when(i == 0)
        def _():
            gc_ref[...] = jnp.zeros_like(gc_ref)
            dxn_ref[...] = jnp.zeros_like(dxn_ref)

        cw = cw_ref[...]
        lam_v = lam_ref[...]
        lxv = lx_ref[...]
        prev8 = jnp.where(ir > 0, lxp_ref[...], 0.0)
        xc, shifted = _conv_taps(lxv, prev8, cw, cb_ref[...])
        r, ig, sp, a, mult = _lru_gates(xc, wa_ref[...], ba_ref[...], wx_ref[...], bx_ref[...], lam_v)
        hv = h_ref[...]
        drv = dr_ref[...]
        g, dg = _gelu_parts(lg_ref[...])
        dlg_ref[...] = drv * hv * dg
        dh = drv * g

        row = lax.broadcasted_iota(jnp.int32, (n, LRU_WIDTH), 0)
        sub = row & 7
        A = jnp.where(row < n - 1, pltpu.roll(a, n - 1, 0), 0.0)
        B = dh + jnp.where(row == n - 1, gc_ref[0:1, :], 0.0)
        for k in (1, 2, 4):
            a_s = jnp.where(sub < 8 - k, pltpu.roll(A, n - k, 0), 1.0)
            b_s = jnp.where(sub < 8 - k, pltpu.roll(B, n - k, 0), 0.0)
            B = B + A * b_s
            A = A * a_s
        carry = jnp.zeros((1, LRU_WIDTH), F32)
        groups = [None] * (n // 8)
        for g in reversed(range(n // 8)):
            gg = B[8 * g:8 * (g + 1)] + A[8 * g:8 * (g + 1)] * carry
            groups[g] = gg
            carry = gg[0:1]
        gs = jnp.concatenate(groups, axis=0)
        gc_ref[0:1, :] = a[0:1, :] * carry

        hprev8 = jnp.where(ir > 0, hp_ref[...], 0.0)
        h_prev = pltpu.roll(jnp.concatenate([hprev8, hv], axis=0), 1, 0)[8:]
        da = gs * h_prev
        ix = ig * xc
        dmult = gs * ix
        dig = gs * mult * xc
        dxc = gs * mult * ig
        dlog_a = da * a - dmult * (a * a) / mult
        dr_gate = dlog_a * (-LRU_C * sp)
        dza = dr_gate * r * (1.0 - r)
        dzx = dig * ig * (1.0 - ig)
        dzab = dza.astype(MXU_DTYPE)
        dzxb = dzx.astype(MXU_DTYPE)
        xcb = xc.astype(MXU_DTYPE)
        dwa_ref[...] += _tn(xcb, dzab)
        dwx_ref[...] += _tn(xcb, dzxb)
        dxc = dxc + _nt(dzab, wa_ref[...]) + _nt(dzxb, wx_ref[...])

        ds = jnp.concatenate([dxc, dxn_ref[...]], axis=0)
        dlx = cw[CONV_WIDTH - 1:CONV_WIDTH] * dxc
        for k in range(1, CONV_WIDTH):
            dlx = dlx + cw[CONV_WIDTH - 1 - k:CONV_WIDTH - k] * pltpu.roll(ds, n + 8 - k, 0)[:n]
        dlx_ref[...] = dlx
        dxn_ref[...] = dxc[0:8, :]

        colsum = lambda v: jnp.sum(v, axis=0, keepdims=True)
        small_ref[0:1, :] += colsum(dza)
        small_ref[1:2, :] += colsum(dzx)
        small_ref[2:3, :] += colsum(dlog_a * r) * (LRU_C * _sigmoid(-lam_v))
        small_ref[3:4, :] += colsum(dxc)
        for k in range(CONV_WIDTH):
            j = CONV_WIDTH - 1 - k
            small_ref[4 + j:5 + j, :] += colsum(dxc * shifted[k])

    tile = pl.BlockSpec((tc, LRU_WIDTH), lambda b, i: (b * nc + (nc - 1 - i), 0))
    prev = pl.BlockSpec((8, LRU_WIDTH), lambda b, i: (jnp.maximum((b * seq + (nc - 1 - i) * tc) // 8 - 1, 0), 0))
    vec = _const((1, LRU_WIDTH))
    mat = _const((LRU_WIDTH, LRU_WIDTH))
    return pl.pallas_call(
        body, name="lru_bwd", grid=(bl, nc),
        in_specs=[tile, tile, tile, prev, tile, prev, _const((CONV_WIDTH, LRU_WIDTH)), vec, mat, vec, mat, vec, vec],
        out_specs=[tile, tile, mat, mat, _const((8, LRU_WIDTH))],
        out_shape=[jax.ShapeDtypeStruct((T, LRU_WIDTH), F32), jax.ShapeDtypeStruct((T, LRU_WIDTH), F32),
                   jax.ShapeDtypeStruct((LRU_WIDTH, LRU_WIDTH), F32), jax.ShapeDtypeStruct((LRU_WIDTH, LRU_WIDTH), F32),
                   jax.ShapeDtypeStruct((8, LRU_WIDTH), F32)],
        scratch_shapes=[pltpu.VMEM((8, LRU_WIDTH), F32), pltpu.VMEM((8, LRU_WIDTH), F32)],
        compiler_params=_params(("arbitrary", "arbitrary")),
    )(drec, lg, h, h, lx, lx, cw, cb, wa, ba, wx, bx, lam)


def _attn_bwd(qn, kn, vb, dob, qt, dot_, lse, delta, frow, fstart, bl, seq, slabs, packs):
    tq = min(ATT_TILE, seq)
    nq = seq // tq
    T = bl * seq
    n, npk = len(slabs), len(packs)
    nx = n + npk

    def body(fs_ref, q_ref, k_ref, v_ref, do_ref, qt_ref, dot_ref, lse_ref, dl_ref, fr_ref, *rest):
        x_in, (dq_ref, dk_ref, dv_ref, df_ref), x_out = rest[:nx], rest[nx:nx + 4], rest[nx + 4:2 * nx + 4]
        send_sems, recv_sems, loc_sems, psend, precv = rest[2 * nx + 4:]
        b, p, j = pl.program_id(0), pl.program_id(1), pl.program_id(2)
        copies = _chip_copies(x_in[:n], x_out[:n], send_sems, recv_sems, loc_sems, scatter=True)
        copies += _device_copies(x_in[n:], x_out[n:], psend, precv, loc_sems, n)

        @pl.when((b == 0) & (p == 0) & (j == 0))
        def _():
            for cp in copies:
                cp.start()

        @pl.when(j == 0)
        def _():
            dq_ref[...] = jnp.zeros_like(dq_ref)

        lane = lax.broadcasted_iota(jnp.int32, (1, PAIR), 1)
        rows = lax.broadcasted_iota(jnp.int32, (tq, tq), 0)
        cols = lax.broadcasted_iota(jnp.int32, (tq, tq), 1)
        causal = cols <= rows
        kv = k_ref[...]
        vv = v_ref[...]
        hms = [(lane >= HEAD_DIM * hh) & (lane < HEAD_DIM * (hh + 1)) for hh in range(2)]
        srow = lax.broadcasted_iota(jnp.int32, (PAIR, 1), 0)
        hms_t = [(srow >= HEAD_DIM * hh) & (srow < HEAD_DIM * (hh + 1)) for hh in range(2)]
        khs = [jnp.where(hm, kv, jnp.zeros_like(kv)) for hm in hms]
        fks = [fr_ref[0, 0, hh:hh + 1, :] for hh in range(2)]
        bases = [((b * N_PAIR + p) * 2 + hh) * nq for hh in range(2)]

        def block(i, carry, masked):
            dk, dv, dfs = carry
            start = pl.multiple_of(i * tq, tq)
            qi = q_ref[pl.ds(start, tq), :]
            doi = do_ref[pl.ds(start, tq), :]
            qti = qt_ref[:, pl.ds(start, tq)]
            doti = dot_ref[:, pl.ds(start, tq)]
            dq = jnp.zeros((tq, PAIR), F32)
            new_dfs = []
            for hh in range(2):
                c0 = HEAD_DIM * hh
                qh = jnp.where(hms[hh], qi, jnp.zeros_like(qi))
                doh = jnp.where(hms[hh], doi, jnp.zeros_like(doi))
                s = _nt(qh, kv) - (fks[hh] - fs_ref[bases[hh] + i])
                if masked:
                    s = jnp.where(causal, s, NEG)
                pr = jnp.exp(s - lse_ref[pl.ds(start, tq), c0:c0 + 1])
                dp = _nt(doh, vv)
                ds = pr * (dp - dl_ref[pl.ds(start, tq), c0:c0 + 1])
                dsb = ds.astype(MXU_DTYPE)
                dv = dv + jnp.dot(jnp.where(hms_t[hh], doti, jnp.zeros_like(doti)), pr.astype(MXU_DTYPE),
                                  preferred_element_type=F32)
                dk = dk + jnp.dot(jnp.where(hms_t[hh], qti, jnp.zeros_like(qti)), dsb, preferred_element_type=F32)
                dq = dq + jnp.dot(dsb, khs[hh], preferred_element_type=F32)
                new_dfs.append(dfs[hh] - jnp.sum(ds, axis=0, keepdims=True))
            dq_ref[pl.ds(start, tq), :] += dq
            return dk, dv, tuple(new_dfs)

        zero = jnp.zeros((PAIR, tq), F32)
        carry = block(j, (zero, zero, (jnp.zeros((1, tq), F32), jnp.zeros((1, tq), F32))), True)
        dk, dv, dfs = lax.fori_loop(j + 1, nq, functools.partial(block, masked=False), carry)
        for hh in range(2):
            df_ref[0, 0, hh:hh + 1, :] = dfs[hh]
        dk_ref[...] = dk.T
        dv_ref[...] = dv.T

        @pl.when((b == bl - 1) & (p == N_PAIR - 1) & (j == nq - 1))
        def _():
            for cp in copies:
                cp.wait()

    blk = pl.BlockSpec((tq, PAIR), lambda b, p, j: (b * nq + j, p))
    full = pl.BlockSpec((seq, PAIR), lambda b, p, j: (b, p))
    fblk = pl.BlockSpec((1, 1, 2, tq), lambda b, p, j: (b, p, 0, j))
    full_t = pl.BlockSpec((PAIR, seq), lambda b, p, j: (p, b))
    hbm = pl.BlockSpec(memory_space=pl.ANY)
    return pl.pallas_call(
        body, name="attn_bwd", grid=(bl, N_PAIR, nq),
        in_specs=[pl.BlockSpec(memory_space=pltpu.SMEM), full, blk, blk, full, full_t, full_t, full, full, fblk]
        + [hbm] * nx,
        out_specs=[full, blk, blk, fblk] + [hbm] * nx,
        out_shape=[jax.ShapeDtypeStruct((T, ATT_WIDTH), F32), jax.ShapeDtypeStruct((T, ATT_WIDTH), F32),
                   jax.ShapeDtypeStruct((T, ATT_WIDTH), F32), jax.ShapeDtypeStruct((bl, N_PAIR, 2, seq), F32)]
        + [jax.ShapeDtypeStruct(s.shape, s.dtype) for s in slabs]
        + [jax.ShapeDtypeStruct((8,) + p.shape, p.dtype) for p in packs],
        scratch_shapes=[pltpu.SemaphoreType.DMA((3 * n,)), pltpu.SemaphoreType.DMA((3 * n,)),
                        pltpu.SemaphoreType.DMA((nx,)),
                        pltpu.SemaphoreType.DMA((7 * npk,)), pltpu.SemaphoreType.DMA((7 * npk,))],
        compiler_params=_params(("arbitrary", "arbitrary", "arbitrary"), VMEM_LARGE),
    )(fstart, qn, kn, vb, dob, qt, dot_, lse, delta, frow, *slabs, *packs)


def _forget_bwd(dfcol, f2d, bf, bl, seq):
    def body(d_ref, z_ref, b_ref, o_ref, db_ref):
        @pl.when(pl.program_id(0) == 0)
        def _():
            db_ref[...] = jnp.zeros_like(db_ref)

        d = d_ref[...]
        row = lax.broadcasted_iota(jnp.int32, (seq, F_PAD), 0)
        k = 1
        while k < seq:
            d = d + jnp.where(row < seq - k, pltpu.roll(d, seq - k, 0), 0.0)
            k *= 2
        dz = d * _sigmoid(-(z_ref[...] + b_ref[...]))
        o_ref[...] = dz
        db_ref[...] += jnp.sum(dz, axis=0, keepdims=True)

    blk = pl.BlockSpec((seq, F_PAD), lambda b: (b, 0))
    return pl.pallas_call(
        body, name="forget_bwd", grid=(bl,),
        in_specs=[blk, blk, _const((1, F_PAD))],
        out_specs=[blk, _const((1, F_PAD))],
        out_shape=[jax.ShapeDtypeStruct(f2d.shape, F32), jax.ShapeDtypeStruct((1, F_PAD), F32)],
        compiler_params=_params(("arbitrary",)),
    )(dfcol, f2d, bf)


def _inproj_bwd(dq, dk, dv, qkv, df, dlx, dlg, x2d, dx2, g1, gq2, gk2, wcat):
    T = x2d.shape[0]
    tm = TOKEN_TILE

    def body(dq_ref, dk_ref, dv_ref, qkv_ref, df_ref, dlx_ref, dlg_ref, x_ref, dx2_ref, g1_ref, gq_ref, gk_ref, w_ref,
             gx_ref, dp_ref, h_ref, dg1_ref, dgq_ref, dgk_ref):
        @pl.when(pl.program_id(0) == 0)
        def _():
            dg1_ref[...] = jnp.zeros_like(dg1_ref)
            dgq_ref[...] = jnp.zeros_like(dgq_ref)
            dgk_ref[...] = jnp.zeros_like(dgk_ref)

        lo = _lo_mask()

        def head_norm_bwd(t, g2, dy):
            rr = lax.rsqrt(_half_sums(t * t, lo) * (1.0 / HEAD_DIM) + NORM_EPS)
            th = t * rr
            dth = dy * g2
            mm = _half_sums(dth * th, lo) * (1.0 / HEAD_DIM)
            return rr * (dth - th * mm), jnp.sum(dy * th, axis=0, keepdims=True)

        dgq = jnp.zeros((1, PAIR), F32)
        dgk = jnp.zeros((1, PAIR), F32)
        for p in range(N_PAIR):
            cq = slice(PAIR * p, PAIR * (p + 1))
            ck = slice(ATT_WIDTH + PAIR * p, ATT_WIDTH + PAIR * (p + 1))
            dqp, g_ = head_norm_bwd(qkv_ref[:, cq], gq_ref[...], dq_ref[:, cq] * QK_SCALE)
            dgq = dgq + g_
            dp_ref[:, cq] = dqp.astype(MXU_DTYPE)
            dkp, g_ = head_norm_bwd(qkv_ref[:, ck], gk_ref[...], dk_ref[:, cq])
            dgk = dgk + g_
            dp_ref[:, ck] = dkp.astype(MXU_DTYPE)
        dgq_ref[...] += dgq
        dgk_ref[...] += dgk
        f0 = 3 * ATT_WIDTH
        dp_ref[:, 2 * ATT_WIDTH:f0] = dv_ref[...].astype(MXU_DTYPE)
        dp_ref[:, f0:f0 + F_PAD] = df_ref[...].astype(MXU_DTYPE)
        dp_ref[:, f0 + F_PAD:f0 + F_PAD + LRU_WIDTH] = dlx_ref[...].astype(MXU_DTYPE)
        dp_ref[:, f0 + F_PAD + LRU_WIDTH:] = dlg_ref[...].astype(MXU_DTYPE)
        dh = jnp.dot(dp_ref[...], w_ref[...], preferred_element_type=F32)
        x = x_ref[...]
        r = lax.rsqrt(jnp.mean(x * x, axis=-1, keepdims=True) + NORM_EPS)
        xh = x * r
        h_ref[...] = (xh * g1_ref[...]).astype(MXU_DTYPE)
        dg1_ref[...] += jnp.sum(dh * xh, axis=0, keepdims=True)
        dxh = dh * g1_ref[...]
        gx_ref[...] = dx2_ref[...] + r * (dxh - xh * jnp.mean(dxh * xh, axis=-1, keepdims=True))

    row = lambda w: pl.BlockSpec((tm, w), lambda i: (i, 0))
    return pl.pallas_call(
        body, name="inproj_bwd", grid=(T // tm,),
        in_specs=[row(ATT_WIDTH), row(ATT_WIDTH), row(ATT_WIDTH), row(3 * ATT_WIDTH), row(F_PAD), row(LRU_WIDTH),
                  row(LRU_WIDTH), row(D_MODEL), row(D_MODEL), _const((1, D_MODEL)), _const((1, PAIR)), _const((1, PAIR)),
                  _const((N_CAT, D_MODEL))],
        out_specs=[row(D_MODEL), row(N_CAT), row(D_MODEL), _const((1, D_MODEL)), _const((1, PAIR)), _const((1, PAIR))],
        out_shape=[jax.ShapeDtypeStruct((T, D_MODEL), F32), jax.ShapeDtypeStruct((T, N_CAT), MXU_DTYPE),
                   jax.ShapeDtypeStruct((T, D_MODEL), MXU_DTYPE), jax.ShapeDtypeStruct((1, D_MODEL), F32),
                   jax.ShapeDtypeStruct((1, PAIR), F32), jax.ShapeDtypeStruct((1, PAIR), F32)],
        compiler_params=_params(("arbitrary",), VMEM_LARGE),
    )(dq, dk, dv, qkv, df, dlx, dlg, x2d, dx2, g1, gq2, gk2, wcat)


ELEMENTWISE_COLS = 256


def _sum_slabs(recv, name):
    _, rows, cols = recv.shape
    cb = ELEMENTWISE_COLS

    def body(r_ref, o_ref):
        part = [r_ref[s].astype(F32) for s in range(4)]
        o_ref[...] = ((part[0] + part[1]) + part[2]) + part[3]

    return pl.pallas_call(
        body, name=name, grid=(cols // cb,),
        in_specs=[pl.BlockSpec((4, rows, cb), lambda i: (0, 0, i))],
        out_specs=pl.BlockSpec((rows, cb), lambda i: (0, i)),
        out_shape=jax.ShapeDtypeStruct((rows, cols), F32),
        compiler_params=_params(("parallel",)),
    )(recv)


def _adamw_math(w, g, m, v):
    m = ADAM_B1 * m + (1.0 - ADAM_B1) * g
    v = ADAM_B2 * v + (1.0 - ADAM_B2) * (g * g)
    m_hat = m / (1.0 - ADAM_B1 ** ADAM_STEP)
    v_hat = v / (1.0 - ADAM_B2 ** ADAM_STEP)
    delta = -ADAM_LR * (m_hat / (jnp.sqrt(v_hat) + ADAM_EPS) + ADAM_WD * w)
    return delta, m, v


def _adamw_pair(mine, theirs, w, m, v, name):
    rows, cols = w.shape
    cb = ELEMENTWISE_COLS

    def body(a_ref, b_ref, w_ref, m_ref, v_ref, g_ref, d_ref, nm_ref, nv_ref):
        g = a_ref[...] + b_ref[...]
        g_ref[...] = g
        d_ref[...], nm_ref[...], nv_ref[...] = _adamw_math(w_ref[...], g, m_ref[...], v_ref[...])

    blk = pl.BlockSpec((rows, cb), lambda i: (0, i))
    return pl.pallas_call(
        body, name=name, grid=(cols // cb,),
        in_specs=[blk] * 5, out_specs=[blk] * 4,
        out_shape=[jax.ShapeDtypeStruct((rows, cols), F32)] * 4,
        compiler_params=_params(("parallel",)),
    )(mine, theirs, w, m, v)


VEC_ROW = {"norm1_g": 0, "norm2_g": 1, "attn_out_g": 2, "lru_out_g": 3, "q_norm_g": 4, "k_norm_g": 5, "b_f": 6,
           "b_a": 8, "b_x": 9, "lam": 10, "conv_b": 11}
LOSS_ROW, CONV_W_ROW, PACK_ROWS = 7, 12, 16
SMALL = list(VEC_ROW) + ["conv_w", "w_a", "w_x"]


def _pack_small(dg1, dg2, dga, dgr, dgq, dgk, dbf, sq_err, lru_small):
    def body(dg1_ref, dg2_ref, dga_ref, dgr_ref, dgq_ref, dgk_ref, dbf_ref, err_ref, lru_ref, v_ref):
        v_ref[...] = jnp.zeros_like(v_ref)
        v_ref[0:1, :] = dg1_ref[...]
        v_ref[1:2, :] = dg2_ref[...]
        v_ref[2:3, 0:ATT_WIDTH] = dga_ref[...]
        v_ref[3:4, 0:LRU_WIDTH] = dgr_ref[...]
        for row, ref in ((4, dgq_ref), (5, dgk_ref)):
            g = ref[...]
            v_ref[row:row + 1, 0:PAIR] = g + pltpu.roll(g, HEAD_DIM, 1)
        v_ref[6:7, 0:F_PAD] = dbf_ref[...]
        v_ref[LOSS_ROW:LOSS_ROW + 1, 0:128] = err_ref[0:1, :] * (0.5 / D_MODEL)
        v_ref[8:16, 0:LRU_WIDTH] = lru_ref[...]

    ins = [dg1, dg2, dga, dgr, dgq, dgk, dbf, sq_err, lru_small]
    return pl.pallas_call(
        body, name="pack_small", grid=(1,),
        in_specs=[_const(a.shape) for a in ins], out_specs=_const((PACK_ROWS, D_MODEL)),
        out_shape=jax.ShapeDtypeStruct((PACK_ROWS, D_MODEL), F32),
        compiler_params=_params(("arbitrary",)),
    )(*ins)


def _diag_blocks(dwa_bd, dwx_bd):
    blk = LRU_WIDTH // LRU_BLOCKS

    def body(wa_ref, wx_ref, oa_ref, ox_ref):
        for src, dst in ((wa_ref, oa_ref), (wx_ref, ox_ref)):
            for nb in range(LRU_BLOCKS):
                tile = src[blk * nb:blk * (nb + 1), PAIR * (nb // 2):PAIR * (nb // 2 + 1)]
                if nb % 2:
                    tile = pltpu.roll(tile, blk, 1)
                dst[nb] = tile[:, 0:blk]

    out = jax.ShapeDtypeStruct((LRU_BLOCKS, blk, blk), F32)
    return pl.pallas_call(
        body, name="diag_blocks", grid=(1,),
        in_specs=[_const(dwa_bd.shape)] * 2, out_specs=[_const(out.shape)] * 2, out_shape=[out, out],
        compiler_params=_params(("arbitrary",)),
    )(dwa_bd, dwx_bd)


def _adamw_small(recv_v, recv_a, recv_x, params):
    names = list(params)
    flat = [a for n in names for a in params[n]]

    def body(rv_ref, ra_ref, rx_ref, *refs):
        ins, loss_ref, outs = refs[:len(flat)], refs[len(flat)], refs[len(flat) + 1:]
        x, y = lax.axis_index("x"), lax.axis_index("y")
        me = 2 * x + y

        def total(r):
            acc = r[0]
            for d in range(1, 8):
                acc = acc + r[d]
            return acc

        gv, ga, gx = total(rv_ref), total(ra_ref), total(rx_ref)
        loss_ref[...] = gv[LOSS_ROW:LOSS_ROW + 1, 0:128]
        for i, n in enumerate(names):
            w_ref, m_ref, v_ref = ins[3 * i:3 * i + 3]
            g_ref, d_ref, nm_ref, nv_ref = outs[4 * i:4 * i + 4]
            if n in VEC_ROW:
                g = gv[VEC_ROW[n]:VEC_ROW[n] + 1, 0:w_ref.shape[1]]
                w, m, v = w_ref[...], m_ref[...], v_ref[...]
            else:
                if n == "conv_w":
                    full = gv[CONV_W_ROW:CONV_W_ROW + CONV_WIDTH, 0:LRU_WIDTH]
                    width = LRU_WIDTH // 4
                    g = jnp.zeros((CONV_WIDTH, width), F32)
                    for s in range(4):
                        g = jnp.where(me == s, full[:, width * s:width * (s + 1)], g)
                else:
                    g = ga if n == "w_a" else gx
                w, m, v = w_ref[0], m_ref[0], v_ref[0]
            d, nm, nv = _adamw_math(w, g, m, v)
            for ref, val in ((g_ref, g), (d_ref, d), (nm_ref, nm), (nv_ref, nv)):
                if n in VEC_ROW:
                    ref[...] = val
                else:
                    ref[0] = val

    out_shape = [jax.ShapeDtypeStruct((1, 128), F32)] + [jax.ShapeDtypeStruct(params[n][0].shape, F32)
                                                          for n in names for _ in range(4)]
    res = pl.pallas_call(
        body, name="adamw_small", grid=(1,),
        in_specs=[_const(a.shape) for a in (recv_v, recv_a, recv_x, *flat)],
        out_specs=[_const(o.shape) for o in out_shape], out_shape=out_shape,
        compiler_params=_params(("arbitrary",)),
    )(recv_v, recv_a, recv_x, *flat)
    return res[0], {n: tuple(res[1 + 4 * i:5 + 4 * i]) for i, n in enumerate(names)}


def _cat_shards(g, pad_at=None, pad=0):
    _, rows, w = g.shape
    pieces = []
    for s in range(4):
        lo, hi = s * w, (s + 1) * w
        if pad_at is not None and lo < pad_at <= hi:
            pieces += [g[s][:, :pad_at - lo], jnp.zeros((rows, pad), g.dtype)]
            if pad_at < hi:
                pieces.append(g[s][:, pad_at - lo:])
        else:
            pieces.append(g[s])
    return jnp.concatenate(pieces, axis=1)


def _block_diag(w):
    eye = jnp.eye(LRU_BLOCKS, dtype=w.dtype)
    return (w[:, :, None, :] * eye[:, None, :, None]).reshape(LRU_WIDTH, LRU_WIDTH)


def kernel(x, norm1_g, w_in, q_norm_g, k_norm_g, b_f, conv_w, conv_b, w_a, b_a, w_x, b_x, lam, attn_out_g, lru_out_g, w_out, norm2_g, w_gate, w_up, w_down, loss_target, m_norm1_g, m_w_in, m_q_norm_g, m_k_norm_g, m_b_f, m_conv_w, m_conv_b, m_w_a, m_b_a, m_w_x, m_b_x, m_lam, m_attn_out_g, m_lru_out_g, m_w_out, m_norm2_g, m_w_gate, m_w_up, m_w_down, v_norm1_g, v_w_in, v_q_norm_g, v_k_norm_g, v_b_f, v_conv_w, v_conv_b, v_w_a, v_b_a, v_w_x, v_b_x, v_lam, v_attn_out_g, v_lru_out_g, v_w_out, v_norm2_g, v_w_gate, v_w_up, v_w_down):
    args = dict(locals())
    bl, seq, _ = x.shape
    T = bl * seq
    tq = min(ATT_TILE, seq)
    nq = seq // tq
    dff = w_gate.shape[2] * 4

    def transposed(name):
        return name.endswith(("w_in", "w_gate", "w_up"))

    def shard2d(name):
        return jnp.swapaxes(args[name], 1, 2)[0] if transposed(name) else args[name][0]

    g_in, g_cw = _gather_split(shard2d("w_in").astype(MXU_DTYPE), conv_w[0])
    later_shards = [shard2d(n).astype(MXU_DTYPE) for n in ("w_out", "w_gate", "w_up", "w_down")]
    f0 = 3 * ATT_WIDTH
    w_in_t = g_in.reshape(-1, D_MODEL)
    wcat = jnp.concatenate([w_in_t[:f0 + HEADS], jnp.zeros((F_PAD - HEADS, D_MODEL), w_in_t.dtype),
                            w_in_t[f0 + HEADS:]], axis=0)
    cw_full = _cat_shards(g_cw)
    wa_bd = _block_diag(w_a[0]).astype(MXU_DTYPE)
    wx_bd = _block_diag(w_x[0]).astype(MXU_DTYPE)
    gq2 = jnp.tile(q_norm_g, (1, 2))
    gk2 = jnp.tile(k_norm_g, (1, 2))
    bf_pad = jnp.pad(b_f, ((0, 0), (0, F_PAD - HEADS)))

    x2d = x.reshape(T, D_MODEL)
    target2d = loss_target.reshape(T, D_MODEL)

    qkv, qn, kn, vb, f2d, lx, lg = _inproj(x2d, norm1_g, wcat, gq2, gk2)
    fcol = _forget_cumsum(f2d, bf_pad, bl, seq)
    frow = jnp.transpose(fcol.reshape(bl, seq, F_PAD)[:, :, :HEADS], (0, 2, 1)).reshape(bl, N_PAIR, 2, seq)
    fstart = frow[:, :, :, ::tq].reshape(-1)
    att, lse, g_out, g_gate, g_up, g_down = _attn_fwd(qn, kn, vb, frow, fstart, bl, seq, later_shards)
    wout_full = g_out.reshape(D_MODEL, D_MODEL)
    wg_full, wu_full = g_gate.reshape(dff, D_MODEL), g_up.reshape(dff, D_MODEL)
    wd_full = g_down.reshape(dff, D_MODEL)
    h, rec = _lru_fwd(lx, lg, cw_full, conv_b, wa_bd, b_a, wx_bd, b_x, lam, bl, seq)
    x2 = _outproj(x2d, att, rec, attn_out_g, lru_out_g, wout_full)
    gt, up, dy, sq_err = _mlp_fwd(x2, norm2_g, wg_full, wu_full, wd_full, target2d)

    dx2, dx2b, dgtb, dupb, actb, h2b, dyb, dg2 = _mlp_bwd(dy, x2, gt, up, norm2_g, wg_full, wu_full, wd_full)
    dw_down = _matmul_tn(actb, dyb, D_MODEL, "dw_down")
    dw_gate = _matmul_tn(dgtb, h2b, D_MODEL, "dw_gate")
    dw_up = _matmul_tn(dupb, h2b, D_MODEL, "dw_up")
    dattb, delta, drec, mixb, dga, dgr = _outproj_bwd(dx2b, att, rec, attn_out_g, lru_out_g, wout_full)
    dw_out = _matmul_tn(mixb, dx2b, D_MODEL, "dw_out")
    dlx, dlg, dwa_bd, dwx_bd, lru_small = _lru_bwd(drec, lg, h, lx, cw_full, conv_b, wa_bd, b_a, wx_bd, b_x, lam, bl, seq)
    early_slabs = [dw_out.reshape(4, D_MODEL // 4, D_MODEL), dw_gate.reshape(4, dff // 4, D_MODEL),
                   dw_up.reshape(4, dff // 4, D_MODEL), dw_down.reshape(4, dff // 4, D_MODEL)]
    pack_a, pack_x = _diag_blocks(dwa_bd, dwx_bd)
    dq, dk, dv, dfrow, *recv_early = _attn_bwd(qn, kn, vb, dattb, qn.T, dattb.T, lse, delta, frow, fstart, bl, seq,
                                               early_slabs, [pack_a, pack_x])
    recv_early, (recv_a, recv_x) = recv_early[:4], recv_early[4:]
    dfcol = jnp.pad(jnp.transpose(dfrow.reshape(bl, HEADS, seq), (0, 2, 1)), ((0, 0), (0, 0), (0, F_PAD - HEADS)))
    df, dbf = _forget_bwd(dfcol.reshape(T, F_PAD), f2d, bf_pad, bl, seq)
    grad_x, dprojb, h1b, dg1, dgq, dgk = _inproj_bwd(dq, dk, dv, qkv, df, dlx, dlg, x2d, dx2, norm1_g, gq2, gk2, wcat)
    dwcat = _matmul_tn(dprojb, h1b, D_MODEL, "dw_in")

    pack_v = _pack_small(dg1, dg2, dga, dgr, dgq, dgk, dbf, sq_err, lru_small)
    dw_in_slabs = jnp.concatenate([dwcat[:f0 + HEADS], dwcat[f0 + F_PAD:]], axis=0).astype(jnp.bfloat16).reshape(
        4, -1, D_MODEL)
    recv_in, recv_v = _exchange_grads([dw_in_slabs], [pack_v])
    recv = [recv_in] + recv_early
    big = ["w_in", "w_out", "w_gate", "w_up", "w_down"]
    part = [_sum_slabs(r, "sum_" + n) for r, n in zip(recv, big)]
    theirs = _swap_with_sibling(part)
    out = {}
    for n, a, b_ in zip(big, part, theirs):
        res = _adamw_pair(a, b_, shard2d(n), shard2d("m_" + n), shard2d("v_" + n), "adamw_" + n)
        out[n] = tuple(jnp.swapaxes(r[None], 1, 2) if transposed(n) else r[None] for r in res)
    loss_row, small_out = _adamw_small(recv_v, recv_a, recv_x,
                                       {n: (args[n], args["m_" + n], args["v_" + n]) for n in SMALL})
    out.update(small_out)
    loss = loss_row[0, 0]

    order = ["norm1_g", "w_in", "q_norm_g", "k_norm_g", "b_f", "conv_w", "conv_b", "w_a", "b_a", "w_x", "b_x", "lam",
             "attn_out_g", "lru_out_g", "w_out", "norm2_g", "w_gate", "w_up", "w_down"]
    return (loss, grad_x.reshape(bl, seq, D_MODEL), *[out[n][0] for n in order], *[out[n][1] for n in order],
            *[out[n][2] for n in order], *[out[n][3] for n in order])
```

```python
import functools
import math

import jax
import jax.numpy as jnp
from jax import lax
from jax.experimental import pallas as pl
from jax.experimental.pallas import tpu as pltpu

F32 = jnp.float32
MXU_DTYPE = jnp.bfloat16
MESH = pl.DeviceIdType.MESH

D_MODEL = 1024
ATT_WIDTH = 512
LRU_WIDTH = 512
HEADS = 8
HEAD_DIM = 64
PAIR = 2 * HEAD_DIM
N_PAIR = HEADS // 2
LRU_BLOCKS = 8
CONV_WIDTH = 4
LRU_C = 8.0
NORM_EPS = 1e-6
QK_SCALE = 1.0 / math.sqrt(HEAD_DIM)
F_PAD = 128
N_CAT = 3 * ATT_WIDTH + F_PAD + 2 * LRU_WIDTH
NEG = -1e30

ADAM_LR, ADAM_B1, ADAM_B2, ADAM_EPS, ADAM_WD, ADAM_STEP = 0.001, 0.9, 0.999, 1e-08, 0.01, 10

TOKEN_TILE = 256
ATT_TILE = 512
LRU_TILE = 256
VMEM_SMALL = 32 * 1024 * 1024
VMEM_LARGE = 56 * 1024 * 1024


def _params(sem, vmem=VMEM_SMALL):
    return pltpu.CompilerParams(dimension_semantics=sem, vmem_limit_bytes=vmem)


def _const(shape):
    nd = len(shape)
    return pl.BlockSpec(shape, lambda *_: (0,) * nd)


def _sigmoid(x):
    return 1.0 / (1.0 + jnp.exp(-x))


def _nt(a, b):
    return lax.dot_general(a, b, (((1,), (1,)), ((), ())), preferred_element_type=F32)


def _tn(a, b):
    return lax.dot_general(a, b, (((0,), (0,)), ((), ())), preferred_element_type=F32)


def _half_sums(t, lo):
    s_lo = jnp.sum(jnp.where(lo, t, 0.0), axis=-1, keepdims=True)
    s_hi = jnp.sum(jnp.where(lo, 0.0, t), axis=-1, keepdims=True)
    return jnp.where(lo, s_lo, s_hi)


def _lo_mask():
    return lax.broadcasted_iota(jnp.int32, (1, PAIR), 1) < HEAD_DIM


def _other_chips(x, y):
    return [(1 - x, y), (x, 1 - y), (1 - x, 1 - y)]


def _chip_copies(ins, outs, send_sems, recv_sems, loc_sems, scatter):
    x, y, c = lax.axis_index("x"), lax.axis_index("y"), lax.axis_index("c")
    me = 2 * x + y
    copies = []
    for w in range(len(ins)):
        copies.append(pltpu.make_async_copy(ins[w].at[me] if scatter else ins[w], outs[w].at[me], loc_sems.at[w]))
        for k, (cx, cy) in enumerate(_other_chips(x, y)):
            copies.append(pltpu.make_async_remote_copy(
                src_ref=ins[w].at[2 * cx + cy] if scatter else ins[w], dst_ref=outs[w].at[me],
                send_sem=send_sems.at[3 * w + k], recv_sem=recv_sems.at[3 * w + k],
                device_id=(cx, cy, c), device_id_type=MESH))
    return copies


def _chip_sems(n):
    return [pltpu.SemaphoreType.DMA((3 * n,)), pltpu.SemaphoreType.DMA((3 * n,)), pltpu.SemaphoreType.DMA((n,))]


def _gather_split(shard, small):
    half = shard.shape[1] // 2

    def body(w_ref, s_ref, ow_ref, os_ref, ici_send, ici_recv, d2d_send, d2d_recv, sm_send, sm_recv, loc_sems):
        x, y, c = lax.axis_index("x"), lax.axis_index("y"), lax.axis_index("c")
        me = 2 * x + y
        mine = pl.ds(pl.multiple_of(c * half, half), half)
        local = [pltpu.make_async_copy(w_ref, ow_ref.at[me], loc_sems.at[0]),
                 pltpu.make_async_copy(s_ref, os_ref.at[me], loc_sems.at[1])]
        fetch, little, forward = [], [], []
        for k, (cx, cy) in enumerate(_other_chips(x, y)):
            src_chip = 2 * cx + cy
            fetch.append(pltpu.make_async_remote_copy(
                src_ref=w_ref.at[:, mine], dst_ref=ow_ref.at[me, :, mine], send_sem=ici_send.at[k],
                recv_sem=ici_recv.at[k], device_id=(cx, cy, c), device_id_type=MESH))
            little.append(pltpu.make_async_remote_copy(
                src_ref=s_ref, dst_ref=os_ref.at[me], send_sem=sm_send.at[k], recv_sem=sm_recv.at[k],
                device_id=(cx, cy, c), device_id_type=MESH))
            forward.append(pltpu.make_async_remote_copy(
                src_ref=ow_ref.at[src_chip, :, mine], dst_ref=ow_ref.at[src_chip, :, mine], send_sem=d2d_send.at[k],
                recv_sem=d2d_recv.at[k], device_id=(x, y, 1 - c), device_id_type=MESH))
        for cp in local + fetch + little:
            cp.start()
        for k in range(3):
            fetch[k].wait_recv()
            forward[k].start()
        for cp in fetch:
            cp.wait_send()
        for cp in little + forward + local:
            cp.wait()

    return pl.pallas_call(
        body, name="gather_weights",
        out_shape=[jax.ShapeDtypeStruct((4,) + shard.shape, shard.dtype),
                   jax.ShapeDtypeStruct((4,) + small.shape, small.dtype)],
        in_specs=[pl.BlockSpec(memory_space=pl.ANY)] * 2,
        out_specs=[pl.BlockSpec(memory_space=pl.ANY)] * 2,
        scratch_shapes=[pltpu.SemaphoreType.DMA((3,))] * 6 + [pltpu.SemaphoreType.DMA((2,))],
    )(shard, small)


def _device_copies(packs_in, packs_out, psend, precv, loc_sems, loc_base):
    x, y, c = lax.axis_index("x"), lax.axis_index("y"), lax.axis_index("c")
    dev = 4 * x + 2 * y + c
    copies = []
    for j in range(len(packs_in)):
        copies.append(pltpu.make_async_copy(packs_in[j], packs_out[j].at[dev], loc_sems.at[loc_base + j]))
        for k in range(1, 8):
            fx, fy, fc = (k >> 2) & 1, (k >> 1) & 1, k & 1
            tx = (1 - x) if fx else x
            ty = (1 - y) if fy else y
            tc = (1 - c) if fc else c
            copies.append(pltpu.make_async_remote_copy(
                src_ref=packs_in[j], dst_ref=packs_out[j].at[dev],
                send_sem=psend.at[7 * j + k - 1], recv_sem=precv.at[7 * j + k - 1],
                device_id=(tx, ty, tc), device_id_type=MESH))
    return copies


def _swap_with_sibling(arrs, packs):
    n, npk = len(arrs), len(packs)

    def body(*refs):
        ins, pack_in = refs[:n], refs[n:n + npk]
        outs, pack_out = refs[n + npk:2 * n + npk], refs[2 * n + npk:2 * (n + npk)]
        send_sems, recv_sems, loc_sems, psend, precv = refs[2 * (n + npk):]
        x, y, c = lax.axis_index("x"), lax.axis_index("y"), lax.axis_index("c")
        copies = [pltpu.make_async_remote_copy(
            src_ref=ins[w], dst_ref=outs[w], send_sem=send_sems.at[w], recv_sem=recv_sems.at[w],
            device_id=(x, y, 1 - c), device_id_type=MESH) for w in range(n)]
        copies += _device_copies(pack_in, pack_out, psend, precv, loc_sems, 0)
        for cp in copies:
            cp.start()
        for cp in copies:
            cp.wait()

    return pl.pallas_call(
        body, name="swap_sibling",
        out_shape=[jax.ShapeDtypeStruct(a.shape, a.dtype) for a in arrs]
        + [jax.ShapeDtypeStruct((8,) + p.shape, p.dtype) for p in packs],
        in_specs=[pl.BlockSpec(memory_space=pl.ANY)] * (n + npk),
        out_specs=[pl.BlockSpec(memory_space=pl.ANY)] * (n + npk),
        scratch_shapes=[pltpu.SemaphoreType.DMA((n,)), pltpu.SemaphoreType.DMA((n,)), pltpu.SemaphoreType.DMA((npk,)),
                        pltpu.SemaphoreType.DMA((7 * npk,)), pltpu.SemaphoreType.DMA((7 * npk,))],
    )(*arrs, *packs)


def _head_norm(t, g2, lo):
    rr = lax.rsqrt(_half_sums(t * t, lo) * (1.0 / HEAD_DIM) + NORM_EPS)
    return t * rr * g2


def _inproj(x2d, g1, wcat, gq2, gk2):
    T = x2d.shape[0]
    tm = TOKEN_TILE

    def body(x_ref, g1_ref, w_ref, gq_ref, gk_ref, qkv_ref, qn_ref, qt_ref, kn_ref, vb_ref, f_ref, lx_ref, lg_ref, h_ref):
        x = x_ref[...]
        r = lax.rsqrt(jnp.mean(x * x, axis=-1, keepdims=True) + NORM_EPS)
        h = (x * r * g1_ref[...]).astype(MXU_DTYPE)
        h_ref[...] = h
        proj = _nt(h, w_ref[...])
        qkv_ref[...] = proj[:, :3 * ATT_WIDTH]
        lo = _lo_mask()
        for p in range(N_PAIR):
            cols = slice(PAIR * p, PAIR * (p + 1))
            q = proj[:, PAIR * p:PAIR * (p + 1)]
            k = proj[:, ATT_WIDTH + PAIR * p:ATT_WIDTH + PAIR * (p + 1)]
            qs = _head_norm(q, gq_ref[...], lo) * QK_SCALE
            qn_ref[:, cols] = qs.astype(MXU_DTYPE)
            qt_ref[cols, :] = qs.T.astype(MXU_DTYPE)
            kn_ref[:, cols] = _head_norm(k, gk_ref[...], lo).astype(MXU_DTYPE)
        vb_ref[...] = proj[:, 2 * ATT_WIDTH:3 * ATT_WIDTH].astype(MXU_DTYPE)
        f0 = 3 * ATT_WIDTH
        f_ref[...] = proj[:, f0:f0 + F_PAD]
        lx_ref[...] = proj[:, f0 + F_PAD:f0 + F_PAD + LRU_WIDTH]
        lg_ref[...] = proj[:, f0 + F_PAD + LRU_WIDTH:]

    row = lambda w: pl.BlockSpec((tm, w), lambda i: (i, 0))
    return pl.pallas_call(
        body, name="inproj", grid=(T // tm,),
        in_specs=[row(D_MODEL), _const((1, D_MODEL)), _const((N_CAT, D_MODEL)), _const((1, PAIR)), _const((1, PAIR))],
        out_specs=[row(3 * ATT_WIDTH), row(ATT_WIDTH), pl.BlockSpec((ATT_WIDTH, tm), lambda i: (0, i)), row(ATT_WIDTH),
                   row(ATT_WIDTH), row(F_PAD), row(LRU_WIDTH), row(LRU_WIDTH), row(D_MODEL)],
        out_shape=[jax.ShapeDtypeStruct((T, 3 * ATT_WIDTH), F32),
                   jax.ShapeDtypeStruct((T, ATT_WIDTH), MXU_DTYPE), jax.ShapeDtypeStruct((ATT_WIDTH, T), MXU_DTYPE),
                   jax.ShapeDtypeStruct((T, ATT_WIDTH), MXU_DTYPE),
                   jax.ShapeDtypeStruct((T, ATT_WIDTH), MXU_DTYPE), jax.ShapeDtypeStruct((T, F_PAD), F32),
                   jax.ShapeDtypeStruct((T, LRU_WIDTH), F32), jax.ShapeDtypeStruct((T, LRU_WIDTH), F32),
                   jax.ShapeDtypeStruct((T, D_MODEL), MXU_DTYPE)],
        compiler_params=_params(("parallel",), VMEM_LARGE),
    )(x2d, g1, wcat, gq2, gk2)


def _forget_cumsum(f2d, bf, bl, seq):
    def body(z_ref, b_ref, o_ref):
        z = z_ref[...] + b_ref[...]
        lf = jnp.minimum(z, 0.0) - jnp.log(1.0 + jnp.exp(-jnp.abs(z)))
        row = lax.broadcasted_iota(jnp.int32, (seq, F_PAD), 0)
        k = 1
        while k < seq:
            lf = lf + jnp.where(row >= k, pltpu.roll(lf, k, 0), 0.0)
            k *= 2
        o_ref[...] = lf

    return pl.pallas_call(
        body, name="forget_cumsum", grid=(bl,),
        in_specs=[pl.BlockSpec((seq, F_PAD), lambda b: (b, 0)), _const((1, F_PAD))],
        out_specs=pl.BlockSpec((seq, F_PAD), lambda b: (b, 0)),
        out_shape=jax.ShapeDtypeStruct(f2d.shape, F32),
        compiler_params=_params(("parallel",)),
    )(f2d, bf)


def _attn_fwd(qn, kn, vb, frow, fstart, bl, seq, shards):
    tq = min(ATT_TILE, seq)
    nq = seq // tq
    T = bl * seq
    n = len(shards)

    def body(fs_ref, q_ref, k_ref, v_ref, fr_ref, *rest):
        g_in, (o_ref, lse_ref), g_out, sems = rest[:n], rest[n:n + 2], rest[n + 2:2 * n + 2], rest[2 * n + 2:]
        b, p, i = pl.program_id(0), pl.program_id(1), pl.program_id(2)
        copies = _chip_copies(g_in, g_out, *sems, scatter=False)

        @pl.when((b == 0) & (p == 0) & (i == 0))
        def _():
            for cp in copies:
                cp.start()

        lane = lax.broadcasted_iota(jnp.int32, (1, PAIR), 1)
        rows = lax.broadcasted_iota(jnp.int32, (tq, tq), 0)
        cols = lax.broadcasted_iota(jnp.int32, (tq, tq), 1)
        causal = cols <= rows
        q = q_ref[...]
        hms = [(lane >= HEAD_DIM * hh) & (lane < HEAD_DIM * (hh + 1)) for hh in range(2)]
        qhs = [jnp.where(hm, q, jnp.zeros_like(q)) for hm in hms]
        shifts = [fs_ref[((b * N_PAIR + p) * 2 + hh) * nq + i] for hh in range(2)]
        sum_lane = [HEAD_DIM * (1 - hh) for hh in range(2)]

        def block(j, carry, masked):
            start = pl.multiple_of(j * tq, tq)
            k = k_ref[pl.ds(start, tq), :]
            v = v_ref[pl.ds(start, tq), :]
            new = []
            for hh in range(2):
                m, acc = carry[hh]
                s = lax.dot_general(qhs[hh], k, (((1,), (1,)), ((), ())), preferred_element_type=F32)
                s = s - (fr_ref[0, 0, hh:hh + 1, pl.ds(start, tq)] - shifts[hh])
                if masked:
                    s = jnp.where(causal, s, NEG)
                m_new = jnp.maximum(m, jnp.max(s, axis=-1, keepdims=True))
                alpha = jnp.exp(m - m_new)
                pe = jnp.exp(s - m_new)
                vh = jnp.where(hms[hh], v, jnp.where(lane == sum_lane[hh], 1.0, 0.0).astype(v.dtype))
                pb = pe.astype(MXU_DTYPE)
                p_lo = (pe - pb.astype(F32)).astype(MXU_DTYPE)
                acc = (alpha * acc + jnp.dot(pb, vh, preferred_element_type=F32)
                       + jnp.dot(p_lo, vh, preferred_element_type=F32))
                new.append((m_new, acc))
            return tuple(new)

        init = (jnp.full((tq, 1), NEG, F32), jnp.zeros((tq, PAIR), F32))
        carry = lax.fori_loop(0, i, functools.partial(block, masked=False), (init, init))
        carry = block(i, carry, True)
        out = jnp.zeros((tq, PAIR), F32)
        lse = jnp.zeros((tq, PAIR), F32)
        for hh in range(2):
            m, acc = carry[hh]
            l = acc[:, sum_lane[hh]:sum_lane[hh] + 1]
            out = jnp.where(hms[hh], acc * (1.0 / l), out)
            lse = jnp.where(hms[hh], m + jnp.log(l), lse)
        o_ref[...] = out
        lse_ref[...] = lse

        @pl.when((b == bl - 1) & (p == N_PAIR - 1) & (i == nq - 1))
        def _():
            for cp in copies:
                cp.wait()

    blk = pl.BlockSpec((tq, PAIR), lambda b, p, i: (b * nq + i, p))
    full = pl.BlockSpec((seq, PAIR), lambda b, p, i: (b, p))
    return pl.pallas_call(
        body, name="attn_fwd", grid=(bl, N_PAIR, nq),
        in_specs=[pl.BlockSpec(memory_space=pltpu.SMEM), blk, full, full,
                  pl.BlockSpec((1, 1, 2, seq), lambda b, p, i: (b, p, 0, 0))] + [pl.BlockSpec(memory_space=pl.ANY)] * n,
        out_specs=[blk, blk] + [pl.BlockSpec(memory_space=pl.ANY)] * n,
        out_shape=[jax.ShapeDtypeStruct((T, ATT_WIDTH), F32)] * 2
        + [jax.ShapeDtypeStruct((4,) + s.shape, s.dtype) for s in shards],
        scratch_shapes=_chip_sems(n),
        compiler_params=_params(("arbitrary", "arbitrary", "arbitrary")),
    )(fstart, qn, kn, vb, frow, *shards)


def _conv_taps(lx, prev8, cw, cb):
    xs = jnp.concatenate([prev8, lx], axis=0)
    shifted = [lx] + [pltpu.roll(xs, k, 0)[8:] for k in range(1, CONV_WIDTH)]
    xc = cb + cw[CONV_WIDTH - 1:CONV_WIDTH] * lx
    for k in range(1, CONV_WIDTH):
        xc = xc + cw[CONV_WIDTH - 1 - k:CONV_WIDTH - k] * shifted[k]
    return xc, shifted


def _lru_gates(xc, wa, ba, wx, bx, lam):
    xb = xc.astype(MXU_DTYPE)
    r = _sigmoid(jnp.dot(xb, wa, preferred_element_type=F32) + ba)
    ig = _sigmoid(jnp.dot(xb, wx, preferred_element_type=F32) + bx)
    sp = jnp.maximum(-lam, 0.0) + jnp.log(1.0 + jnp.exp(-jnp.abs(lam)))
    log_a = -LRU_C * r * sp
    a = jnp.exp(log_a)
    th = jnp.tanh(log_a)
    mult = jnp.sqrt(-2.0 * th / (1.0 - th))
    return r, ig, sp, a, mult


def _gelu_parts(x):
    c0 = math.sqrt(2.0 / math.pi)
    t = jnp.tanh(c0 * (x + 0.044715 * x * x * x))
    g = 0.5 * x * (1.0 + t)
    dg = 0.5 * (1.0 + t) + 0.5 * x * (1.0 - t * t) * c0 * (1.0 + 3.0 * 0.044715 * x * x)
    return g, dg


def _lru_fwd(lx, lg, cw, cb, wa, ba, wx, bx, lam, bl, seq):
    tc = min(LRU_TILE, seq)
    nc = seq // tc
    T = bl * seq

    def body(lx_ref, lxp_ref, lg_ref, cw_ref, cb_ref, wa_ref, ba_ref, wx_ref, bx_ref, lam_ref,
             h_ref, rec_ref, hc_ref):
        i = pl.program_id(1)

        @pl.when(i == 0)
        def _():
            hc_ref[...] = jnp.zeros_like(hc_ref)

        lxv = lx_ref[...]
        prev8 = jnp.where(i > 0, lxp_ref[...], 0.0)
        xc, _ = _conv_taps(lxv, prev8, cw_ref[...], cb_ref[...])
        _, ig, _, a, mult = _lru_gates(xc, wa_ref[...], ba_ref[...], wx_ref[...], bx_ref[...], lam_ref[...])
        u = mult * (ig * xc)
        sub = lax.broadcasted_iota(jnp.int32, (tc, LRU_WIDTH), 0) & 7
        A, B = a, u
        for k in (1, 2, 4):
            a_s = jnp.where(sub >= k, pltpu.roll(A, k, 0), 1.0)
            b_s = jnp.where(sub >= k, pltpu.roll(B, k, 0), 0.0)
            B = A * b_s + B
            A = A * a_s
        carry = hc_ref[0:1, :]
        groups = []
        for g in range(tc // 8):
            hg = A[8 * g:8 * (g + 1)] * carry + B[8 * g:8 * (g + 1)]
            groups.append(hg)
            carry = hg[7:8]
        h = jnp.concatenate(groups, axis=0)
        hc_ref[0:1, :] = carry
        h_ref[...] = h
        g, _ = _gelu_parts(lg_ref[...])
        rec_ref[...] = h * g

    tile = pl.BlockSpec((tc, LRU_WIDTH), lambda b, i: (b * nc + i, 0))
    prev = pl.BlockSpec((8, LRU_WIDTH), lambda b, i: (jnp.maximum((b * seq + i * tc) // 8 - 1, 0), 0))
    vec = _const((1, LRU_WIDTH))
    mat = _const((LRU_WIDTH, LRU_WIDTH))
    return pl.pallas_call(
        body, name="lru_fwd", grid=(bl, nc),
        in_specs=[tile, prev, tile, _const((CONV_WIDTH, LRU_WIDTH)), vec, mat, vec, mat, vec, vec],
        out_specs=[tile, tile],
        out_shape=[jax.ShapeDtypeStruct((T, LRU_WIDTH), F32), jax.ShapeDtypeStruct((T, LRU_WIDTH), F32)],
        scratch_shapes=[pltpu.VMEM((8, LRU_WIDTH), F32)],
        compiler_params=_params(("arbitrary", "arbitrary")),
    )(lx, lx, lg, cw, cb, wa, ba, wx, bx, lam)


def _outproj(x2d, att, rec, ga, gr, wout):
    T = x2d.shape[0]
    tm = TOKEN_TILE

    def body(x_ref, a_ref, r_ref, ga_ref, gr_ref, w_ref, o_ref):
        a = a_ref[...]
        rc = r_ref[...]
        na = a * lax.rsqrt(jnp.mean(a * a, axis=-1, keepdims=True) + NORM_EPS) * ga_ref[...]
        nr = rc * lax.rsqrt(jnp.mean(rc * rc, axis=-1, keepdims=True) + NORM_EPS) * gr_ref[...]
        o_ref[...] = (x_ref[...]
                      + jnp.dot(na.astype(MXU_DTYPE), w_ref[:ATT_WIDTH, :], preferred_element_type=F32)
                      + jnp.dot(nr.astype(MXU_DTYPE), w_ref[ATT_WIDTH:, :], preferred_element_type=F32))

    row = lambda w: pl.BlockSpec((tm, w), lambda i: (i, 0))
    return pl.pallas_call(
        body, name="outproj", grid=(T // tm,),
        in_specs=[row(D_MODEL), row(ATT_WIDTH), row(LRU_WIDTH), _const((1, ATT_WIDTH)), _const((1, LRU_WIDTH)),
                  _const((D_MODEL, D_MODEL))],
        out_specs=row(D_MODEL),
        out_shape=jax.ShapeDtypeStruct((T, D_MODEL), F32),
        compiler_params=_params(("parallel",)),
    )(x2d, att, rec, ga, gr, wout)


def _mlp_fwd(x2, g2, wg, wu, wd, target):
    T = x2.shape[0]
    tm = TOKEN_TILE
    dff = wg.shape[0]

    def body(x_ref, g_ref, wg_ref, wu_ref, wd_ref, t_ref, gt_ref, up_ref, dy_ref, loss_ref):
        @pl.when(pl.program_id(0) == 0)
        def _():
            loss_ref[...] = jnp.zeros_like(loss_ref)

        x = x_ref[...]
        r = lax.rsqrt(jnp.mean(x * x, axis=-1, keepdims=True) + NORM_EPS)
        h = (x * r * g_ref[...]).astype(MXU_DTYPE)
        gt = _nt(h, wg_ref[...])
        up = _nt(h, wu_ref[...])
        gt_ref[...] = gt
        up_ref[...] = up
        act = (gt * _sigmoid(gt) * up).astype(MXU_DTYPE)
        y = x + jnp.dot(act, wd_ref[...], preferred_element_type=F32)
        e = y - t_ref[...]
        dy_ref[...] = e * (1.0 / D_MODEL)
        loss_ref[...] += jnp.sum(e * e)

    row = lambda w: pl.BlockSpec((tm, w), lambda i: (i, 0))
    return pl.pallas_call(
        body, name="mlp_fwd", grid=(T // tm,),
        in_specs=[row(D_MODEL), _const((1, D_MODEL)), _const((dff, D_MODEL)), _const((dff, D_MODEL)),
                  _const((dff, D_MODEL)), row(D_MODEL)],
        out_specs=[row(dff), row(dff), row(D_MODEL), _const((8, 128))],
        out_shape=[jax.ShapeDtypeStruct((T, dff), F32), jax.ShapeDtypeStruct((T, dff), F32),
                   jax.ShapeDtypeStruct((T, D_MODEL), F32), jax.ShapeDtypeStruct((8, 128), F32)],
        compiler_params=_params(("arbitrary",), VMEM_LARGE),
    )(x2, g2, wg, wu, wd, target)


def _mlp_bwd(dy, x2, gt, up, g2, wg, wu, wd):
    T = x2.shape[0]
    tm = TOKEN_TILE
    dff = wg.shape[0]

    def body(dy_ref, x_ref, gt_ref, up_ref, g_ref, wg_ref, wu_ref, wd_ref,
             dx_ref, dxb_ref, dgt_ref, dup_ref, act_ref, h_ref, dyb_ref, dg_ref):
        @pl.when(pl.program_id(0) == 0)
        def _():
            dg_ref[...] = jnp.zeros_like(dg_ref)

        dy_v = dy_ref[...]
        dyb = dy_v.astype(MXU_DTYPE)
        dyb_ref[...] = dyb
        x = x_ref[...]
        r = lax.rsqrt(jnp.mean(x * x, axis=-1, keepdims=True) + NORM_EPS)
        xh = x * r
        h_ref[...] = (xh * g_ref[...]).astype(MXU_DTYPE)
        gt_v = gt_ref[...]
        up_v = up_ref[...]
        sg = _sigmoid(gt_v)
        silu = gt_v * sg
        act_ref[...] = (silu * up_v).astype(MXU_DTYPE)
        dact = _nt(dyb, wd_ref[...])
        dup = (dact * silu).astype(MXU_DTYPE)
        dgt = (dact * up_v * (sg * (1.0 + gt_v * (1.0 - sg)))).astype(MXU_DTYPE)
        dup_ref[...] = dup
        dgt_ref[...] = dgt
        dh = (jnp.dot(dgt, wg_ref[...], preferred_element_type=F32)
              + jnp.dot(dup, wu_ref[...], preferred_element_type=F32))
        dg_ref[...] += jnp.sum(dh * xh, axis=0, keepdims=True)
        dxh = dh * g_ref[...]
        dx = dy_v + r * (dxh - xh * jnp.mean(dxh * xh, axis=-1, keepdims=True))
        dx_ref[...] = dx
        dxb_ref[...] = dx.astype(MXU_DTYPE)

    row = lambda w: pl.BlockSpec((tm, w), lambda i: (i, 0))
    return pl.pallas_call(
        body, name="mlp_bwd", grid=(T // tm,),
        in_specs=[row(D_MODEL), row(D_MODEL), row(dff), row(dff), _const((1, D_MODEL)),
                  _const((dff, D_MODEL)), _const((dff, D_MODEL)), _const((dff, D_MODEL))],
        out_specs=[row(D_MODEL), row(D_MODEL), row(dff), row(dff), row(dff), row(D_MODEL), row(D_MODEL),
                   _const((1, D_MODEL))],
        out_shape=[jax.ShapeDtypeStruct((T, D_MODEL), F32), jax.ShapeDtypeStruct((T, D_MODEL), MXU_DTYPE),
                   jax.ShapeDtypeStruct((T, dff), MXU_DTYPE), jax.ShapeDtypeStruct((T, dff), MXU_DTYPE),
                   jax.ShapeDtypeStruct((T, dff), MXU_DTYPE), jax.ShapeDtypeStruct((T, D_MODEL), MXU_DTYPE),
                   jax.ShapeDtypeStruct((T, D_MODEL), MXU_DTYPE), jax.ShapeDtypeStruct((1, D_MODEL), F32)],
        compiler_params=_params(("arbitrary",), VMEM_LARGE),
    )(dy, x2, gt, up, g2, wg, wu, wd)


def _matmul_tn(a, b, tn, name):
    T, K = a.shape
    N = b.shape[1]
    tt = min(512, T)

    def body(a_ref, b_ref, o_ref):
        @pl.when(pl.program_id(1) == 0)
        def _():
            o_ref[...] = jnp.zeros_like(o_ref)

        o_ref[...] += _tn(a_ref[...], b_ref[...])

    return pl.pallas_call(
        body, name=name, grid=(N // tn, T // tt),
        in_specs=[pl.BlockSpec((tt, K), lambda n, t: (t, 0)), pl.BlockSpec((tt, tn), lambda n, t: (t, n))],
        out_specs=pl.BlockSpec((K, tn), lambda n, t: (0, n)),
        out_shape=jax.ShapeDtypeStruct((K, N), F32),
        compiler_params=_params(("parallel", "arbitrary"), VMEM_LARGE),
    )(a, b)


def _outproj_bwd(dx2b, att, rec, ga, gr, wout):
    T = att.shape[0]
    tm = TOKEN_TILE

    def body(dx_ref, a_ref, r_ref, ga_ref, gr_ref, w_ref, datt_ref, dattt_ref, delta_ref, drec_ref, mix_ref, dga_ref,
             dgr_ref):
        @pl.when(pl.program_id(0) == 0)
        def _():
            dga_ref[...] = jnp.zeros_like(dga_ref)
            dgr_ref[...] = jnp.zeros_like(dgr_ref)

        dmix = _nt(dx_ref[...], w_ref[...])

        def norm_bwd(v, g, dn):
            rr = lax.rsqrt(jnp.mean(v * v, axis=-1, keepdims=True) + NORM_EPS)
            vh = v * rr
            dvh = dn * g
            dv = rr * (dvh - vh * jnp.mean(dvh * vh, axis=-1, keepdims=True))
            return vh, dv, jnp.sum(dn * vh, axis=0, keepdims=True)

        a = a_ref[...]
        ah, datt, dga = norm_bwd(a, ga_ref[...], dmix[:, :ATT_WIDTH])
        rh, drec, dgr = norm_bwd(r_ref[...], gr_ref[...], dmix[:, ATT_WIDTH:])
        dga_ref[...] += dga
        dgr_ref[...] += dgr
        mix_ref[:, :ATT_WIDTH] = (ah * ga_ref[...]).astype(MXU_DTYPE)
        mix_ref[:, ATT_WIDTH:] = (rh * gr_ref[...]).astype(MXU_DTYPE)
        dattb = datt.astype(MXU_DTYPE)
        datt_ref[...] = dattb
        dattt_ref[...] = datt.T.astype(MXU_DTYPE)
        drec_ref[...] = drec
        lo = _lo_mask()
        prod = dattb.astype(F32) * a
        for p in range(N_PAIR):
            delta_ref[:, PAIR * p:PAIR * (p + 1)] = _half_sums(prod[:, PAIR * p:PAIR * (p + 1)], lo)

    row = lambda w: pl.BlockSpec((tm, w), lambda i: (i, 0))
    return pl.pallas_call(
        body, name="outproj_bwd", grid=(T // tm,),
        in_specs=[row(D_MODEL), row(ATT_WIDTH), row(LRU_WIDTH), _const((1, ATT_WIDTH)), _const((1, LRU_WIDTH)),
                  _const((D_MODEL, D_MODEL))],
        out_specs=[row(ATT_WIDTH), pl.BlockSpec((ATT_WIDTH, tm), lambda i: (0, i)), row(ATT_WIDTH), row(LRU_WIDTH),
                   row(D_MODEL), _const((1, ATT_WIDTH)), _const((1, LRU_WIDTH))],
        out_shape=[jax.ShapeDtypeStruct((T, ATT_WIDTH), MXU_DTYPE), jax.ShapeDtypeStruct((ATT_WIDTH, T), MXU_DTYPE),
                   jax.ShapeDtypeStruct((T, ATT_WIDTH), F32),
                   jax.ShapeDtypeStruct((T, LRU_WIDTH), F32), jax.ShapeDtypeStruct((T, D_MODEL), MXU_DTYPE),
                   jax.ShapeDtypeStruct((1, ATT_WIDTH), F32), jax.ShapeDtypeStruct((1, LRU_WIDTH), F32)],
        compiler_params=_params(("arbitrary",)),
    )(dx2b, att, rec, ga, gr, wout)


def _lru_bwd(drec, lg, h, lx, cw, cb, wa, ba, wx, bx, lam, bl, seq):
    tc = min(LRU_TILE, seq)
    nc = seq // tc
    T = bl * seq
    n = tc

    def body(dr_ref, lg_ref, h_ref, hp_ref, lx_ref, lxp_ref, cw_ref, cb_ref, wa_ref, ba_ref, wx_ref, bx_ref, lam_ref,
             dlx_ref, dlg_ref, dwa_ref, dwx_ref, small_ref, gc_ref, dxn_ref):
        b, i = pl.program_id(0), pl.program_id(1)
        ir = nc - 1 - i

        @pl.when((b == 0) & (i == 0))
        def _():
            dwa_ref[...] = jnp.zeros_like(dwa_ref)
            dwx_ref[...] = jnp.zeros_like(dwx_ref)
            small_ref[...] = jnp.zeros_like(small_ref)

        @pl.when(i == 0)
        def _():
            gc_ref[...] = jnp.zeros_like(gc_ref)
            dxn_ref[...] = jnp.zeros_like(dxn_ref)

        cw = cw_ref[...]
        lam_v = lam_ref[...]
        lxv = lx_ref[...]
        prev8 = jnp.where(ir > 0, lxp_ref[...], 0.0)
        xc, shifted = _conv_taps(lxv, prev8, cw, cb_ref[...])
        r, ig, sp, a, mult = _lru_gates(xc, wa_ref[...], ba_ref[...], wx_ref[...], bx_ref[...], lam_v)
        hv = h_ref[...]
        drv = dr_ref[...]
        g, dg = _gelu_parts(lg_ref[...])
        dlg_ref[...] = drv * hv * dg
        dh = drv * g

        row = lax.broadcasted_iota(jnp.int32, (n, LRU_WIDTH), 0)
        sub = row & 7
        A = jnp.where(row < n - 1, pltpu.roll(a, n - 1, 0), 0.0)
        B = dh + jnp.where(row == n - 1, gc_ref[0:1, :], 0.0)
        for k in (1, 2, 4):
            a_s = jnp.where(sub < 8 - k, pltpu.roll(A, n - k, 0), 1.0)
            b_s = jnp.where(sub < 8 - k, pltpu.roll(B, n - k, 0), 0.0)
            B = B + A * b_s
            A = A * a_s
        carry = jnp.zeros((1, LRU_WIDTH), F32)
        groups = [None] * (n // 8)
        for g in reversed(range(n // 8)):
            gg = B[8 * g:8 * (g + 1)] + A[8 * g:8 * (g + 1)] * carry
            groups[g] = gg
            carry = gg[0:1]
        gs = jnp.concatenate(groups, axis=0)
        gc_ref[0:1, :] = a[0:1, :] * carry

        hprev8 = jnp.where(ir > 0, hp_ref[...], 0.0)
        h_prev = pltpu.roll(jnp.concatenate([hprev8, hv], axis=0), 1, 0)[8:]
        da = gs * h_prev
        ix = ig * xc
        dmult = gs * ix
        dig = gs * mult * xc
        dxc = gs * mult * ig
        dlog_a = da * a - dmult * (a * a) / mult
        dr_gate = dlog_a * (-LRU_C * sp)
        dza = dr_gate * r * (1.0 - r)
        dzx = dig * ig * (1.0 - ig)
        dzab = dza.astype(MXU_DTYPE)
        dzxb = dzx.astype(MXU_DTYPE)
        xcb = xc.astype(MXU_DTYPE)
        dwa_ref[...] += _tn(xcb, dzab)
        dwx_ref[...] += _tn(xcb, dzxb)
        dxc = dxc + _nt(dzab, wa_ref[...]) + _nt(dzxb, wx_ref[...])

        ds = jnp.concatenate([dxc, dxn_ref[...]], axis=0)
        dlx = cw[CONV_WIDTH - 1:CONV_WIDTH] * dxc
        for k in range(1, CONV_WIDTH):
            dlx = dlx + cw[CONV_WIDTH - 1 - k:CONV_WIDTH - k] * pltpu.roll(ds, n + 8 - k, 0)[:n]
        dlx_ref[...] = dlx
        dxn_ref[...] = dxc[0:8, :]

        colsum = lambda v: jnp.sum(v, axis=0, keepdims=True)
        small_ref[0:1, :] += colsum(dza)
        small_ref[1:2, :] += colsum(dzx)
        small_ref[2:3, :] += colsum(dlog_a * r) * (LRU_C * _sigmoid(-lam_v))
        small_ref[3:4, :] += colsum(dxc)
        for k in range(CONV_WIDTH):
            j = CONV_WIDTH - 1 - k
            small_ref[4 + j:5 + j, :] += colsum(dxc * shifted[k])

    tile = pl.BlockSpec((tc, LRU_WIDTH), lambda b, i: (b * nc + (nc - 1 - i), 0))
    prev = pl.BlockSpec((8, LRU_WIDTH), lambda b, i: (jnp.maximum((b * seq + (nc - 1 - i) * tc) // 8 - 1, 0), 0))
    vec = _const((1, LRU_WIDTH))
    mat = _const((LRU_WIDTH, LRU_WIDTH))
    return pl.pallas_call(
        body, name="lru_bwd", grid=(bl, nc),
        in_specs=[tile, tile, tile, prev, tile, prev, _const((CONV_WIDTH, LRU_WIDTH)), vec, mat, vec, mat, vec, vec],
        out_specs=[tile, tile, mat, mat, _const((8, LRU_WIDTH))],
        out_shape=[jax.ShapeDtypeStruct((T, LRU_WIDTH), F32), jax.ShapeDtypeStruct((T, LRU_WIDTH), F32),
                   jax.ShapeDtypeStruct((LRU_WIDTH, LRU_WIDTH), F32), jax.ShapeDtypeStruct((LRU_WIDTH, LRU_WIDTH), F32),
                   jax.ShapeDtypeStruct((8, LRU_WIDTH), F32)],
        scratch_shapes=[pltpu.VMEM((8, LRU_WIDTH), F32), pltpu.VMEM((8, LRU_WIDTH), F32)],
        compiler_params=_params(("arbitrary", "arbitrary")),
    )(drec, lg, h, h, lx, lx, cw, cb, wa, ba, wx, bx, lam)


def _attn_bwd(qn, kn, vb, dob, qt, dot_, lse, delta, frow, fstart, bl, seq, slabs, packs):
    tq = min(ATT_TILE, seq)
    nq = seq // tq
    T = bl * seq
    n, npk = len(slabs), len(packs)
    nx = n + npk

    def body(fs_ref, q_ref, k_ref, v_ref, do_ref, qt_ref, dot_ref, lse_ref, dl_ref, fr_ref, *rest):
        x_in, (dq_ref, dk_ref, dv_ref, df_ref), x_out = rest[:nx], rest[nx:nx + 4], rest[nx + 4:2 * nx + 4]
        send_sems, recv_sems, loc_sems, psend, precv = rest[2 * nx + 4:]
        b, p, j = pl.program_id(0), pl.program_id(1), pl.program_id(2)
        copies = _chip_copies(x_in[:n], x_out[:n], send_sems, recv_sems, loc_sems, scatter=True)
        copies += _device_copies(x_in[n:], x_out[n:], psend, precv, loc_sems, n)

        @pl.when((b == 0) & (p == 0) & (j == 0))
        def _():
            for cp in copies:
                cp.start()

        @pl.when(j == 0)
        def _():
            dq_ref[...] = jnp.zeros_like(dq_ref)

        lane = lax.broadcasted_iota(jnp.int32, (1, PAIR), 1)
        rows = lax.broadcasted_iota(jnp.int32, (tq, tq), 0)
        cols = lax.broadcasted_iota(jnp.int32, (tq, tq), 1)
        causal = cols <= rows
        kv = k_ref[...]
        vv = v_ref[...]
        hms = [(lane >= HEAD_DIM * hh) & (lane < HEAD_DIM * (hh + 1)) for hh in range(2)]
        srow = lax.broadcasted_iota(jnp.int32, (PAIR, 1), 0)
        hms_t = [(srow >= HEAD_DIM * hh) & (srow < HEAD_DIM * (hh + 1)) for hh in range(2)]
        khs = [jnp.where(hm, kv, jnp.zeros_like(kv)) for hm in hms]
        fks = [fr_ref[0, 0, hh:hh + 1, :] for hh in range(2)]
        bases = [((b * N_PAIR + p) * 2 + hh) * nq for hh in range(2)]

        def block(i, carry, masked):
            dk, dv, dfs = carry
            start = pl.multiple_of(i * tq, tq)
            qi = q_ref[pl.ds(start, tq), :]
            doi = do_ref[pl.ds(start, tq), :]
            qti = qt_ref[:, pl.ds(start, tq)]
            doti = dot_ref[:, pl.ds(start, tq)]
            dq = jnp.zeros((tq, PAIR), F32)
            new_dfs = []
            for hh in range(2):
                c0 = HEAD_DIM * hh
                qh = jnp.where(hms[hh], qi, jnp.zeros_like(qi))
                doh = jnp.where(hms[hh], doi, jnp.zeros_like(doi))
                s = _nt(qh, kv) - (fks[hh] - fs_ref[bases[hh] + i])
                if masked:
                    s = jnp.where(causal, s, NEG)
                pr = jnp.exp(s - lse_ref[pl.ds(start, tq), c0:c0 + 1])
                dp = _nt(doh, vv)
                ds = pr * (dp - dl_ref[pl.ds(start, tq), c0:c0 + 1])
                dsb = ds.astype(MXU_DTYPE)
                dv = dv + jnp.dot(jnp.where(hms_t[hh], doti, jnp.zeros_like(doti)), pr.astype(MXU_DTYPE),
                                  preferred_element_type=F32)
                dk = dk + jnp.dot(jnp.where(hms_t[hh], qti, jnp.zeros_like(qti)), dsb, preferred_element_type=F32)
                dq = dq + jnp.dot(dsb, khs[hh], preferred_element_type=F32)
                new_dfs.append(dfs[hh] - jnp.sum(ds, axis=0, keepdims=True))
            dq_ref[pl.ds(start, tq), :] += dq
            return dk, dv, tuple(new_dfs)

        zero = jnp.zeros((PAIR, tq), F32)
        carry = block(j, (zero, zero, (jnp.zeros((1, tq), F32), jnp.zeros((1, tq), F32))), True)
        dk, dv, dfs = lax.fori_loop(j + 1, nq, functools.partial(block, masked=False), carry)
        for hh in range(2):
            df_ref[0, 0, hh:hh + 1, :] = dfs[hh]
        dk_ref[...] = dk.T
        dv_ref[...] = dv.T

        @pl.when((b == bl - 1) & (p == N_PAIR - 1) & (j == nq - 1))
        def _():
            for cp in copies:
                cp.wait()

    blk = pl.BlockSpec((tq, PAIR), lambda b, p, j: (b * nq + j, p))
    full = pl.BlockSpec((seq, PAIR), lambda b, p, j: (b, p))
    fblk = pl.BlockSpec((1, 1, 2, tq), lambda b, p, j: (b, p, 0, j))
    full_t = pl.BlockSpec((PAIR, seq), lambda b, p, j: (p, b))
    hbm = pl.BlockSpec(memory_space=pl.ANY)
    return pl.pallas_call(
        body, name="attn_bwd", grid=(bl, N_PAIR, nq),
        in_specs=[pl.BlockSpec(memory_space=pltpu.SMEM), full, blk, blk, full, full_t, full_t, full, full, fblk]
        + [hbm] * nx,
        out_specs=[full, blk, blk, fblk] + [hbm] * nx,
        out_shape=[jax.ShapeDtypeStruct((T, ATT_WIDTH), F32), jax.ShapeDtypeStruct((T, ATT_WIDTH), F32),
                   jax.ShapeDtypeStruct((T, ATT_WIDTH), F32), jax.ShapeDtypeStruct((bl, N_PAIR, 2, seq), F32)]
        + [jax.ShapeDtypeStruct(s.shape, s.dtype) for s in slabs]
        + [jax.ShapeDtypeStruct((8,) + p.shape, p.dtype) for p in packs],
        scratch_shapes=[pltpu.SemaphoreType.DMA((3 * n,)), pltpu.SemaphoreType.DMA((3 * n,)),
                        pltpu.SemaphoreType.DMA((nx,)),
                        pltpu.SemaphoreType.DMA((7 * npk,)), pltpu.SemaphoreType.DMA((7 * npk,))],
        compiler_params=_params(("arbitrary", "arbitrary", "arbitrary"), VMEM_LARGE),
    )(fstart, qn, kn, vb, dob, qt, dot_, lse, delta, frow, *slabs, *packs)


def _forget_bwd(dfcol, f2d, bf, bl, seq):
    def body(d_ref, z_ref, b_ref, o_ref, db_ref):
        @pl.when(pl.program_id(0) == 0)
        def _():
            db_ref[...] = jnp.zeros_like(db_ref)

        d = d_ref[...]
        row = lax.broadcasted_iota(jnp.int32, (seq, F_PAD), 0)
        k = 1
        while k < seq:
            d = d + jnp.where(row < seq - k, pltpu.roll(d, seq - k, 0), 0.0)
            k *= 2
        dz = d * _sigmoid(-(z_ref[...] + b_ref[...]))
        o_ref[...] = dz
        db_ref[...] += jnp.sum(dz, axis=0, keepdims=True)

    blk = pl.BlockSpec((seq, F_PAD), lambda b: (b, 0))
    return pl.pallas_call(
        body, name="forget_bwd", grid=(bl,),
        in_specs=[blk, blk, _const((1, F_PAD))],
        out_specs=[blk, _const((1, F_PAD))],
        out_shape=[jax.ShapeDtypeStruct(f2d.shape, F32), jax.ShapeDtypeStruct((1, F_PAD), F32)],
        compiler_params=_params(("arbitrary",)),
    )(dfcol, f2d, bf)


def _dproj(dq, dk, dv, qkv, df, dlx, dlg, gq2, gk2):
    T = dq.shape[0]
    tm = TOKEN_TILE

    def body(dq_ref, dk_ref, dv_ref, qkv_ref, df_ref, dlx_ref, dlg_ref, gq_ref, gk_ref, dp_ref, dgq_ref, dgk_ref):
        @pl.when(pl.program_id(0) == 0)
        def _():
            dgq_ref[...] = jnp.zeros_like(dgq_ref)
            dgk_ref[...] = jnp.zeros_like(dgk_ref)

        lo = _lo_mask()

        def head_norm_bwd(t, g2, dy):
            rr = lax.rsqrt(_half_sums(t * t, lo) * (1.0 / HEAD_DIM) + NORM_EPS)
            th = t * rr
            dth = dy * g2
            mm = _half_sums(dth * th, lo) * (1.0 / HEAD_DIM)
            return rr * (dth - th * mm), jnp.sum(dy * th, axis=0, keepdims=True)

        dgq = jnp.zeros((1, PAIR), F32)
        dgk = jnp.zeros((1, PAIR), F32)
        for p in range(N_PAIR):
            cq = slice(PAIR * p, PAIR * (p + 1))
            ck = slice(ATT_WIDTH + PAIR * p, ATT_WIDTH + PAIR * (p + 1))
            dqp, g_ = head_norm_bwd(qkv_ref[:, cq], gq_ref[...], dq_ref[:, cq] * QK_SCALE)
            dgq = dgq + g_
            dp_ref[:, cq] = dqp.astype(MXU_DTYPE)
            dkp, g_ = head_norm_bwd(qkv_ref[:, ck], gk_ref[...], dk_ref[:, cq])
            dgk = dgk + g_
            dp_ref[:, ck] = dkp.astype(MXU_DTYPE)
        dgq_ref[...] += dgq
        dgk_ref[...] += dgk
        f0 = 3 * ATT_WIDTH
        dp_ref[:, 2 * ATT_WIDTH:f0] = dv_ref[...].astype(MXU_DTYPE)
        dp_ref[:, f0:f0 + F_PAD] = df_ref[...].astype(MXU_DTYPE)
        dp_ref[:, f0 + F_PAD:f0 + F_PAD + LRU_WIDTH] = dlx_ref[...].astype(MXU_DTYPE)
        dp_ref[:, f0 + F_PAD + LRU_WIDTH:] = dlg_ref[...].astype(MXU_DTYPE)

    row = lambda w: pl.BlockSpec((tm, w), lambda i: (i, 0))
    return pl.pallas_call(
        body, name="dproj", grid=(T // tm,),
        in_specs=[row(ATT_WIDTH), row(ATT_WIDTH), row(ATT_WIDTH), row(3 * ATT_WIDTH), row(F_PAD), row(LRU_WIDTH),
                  row(LRU_WIDTH), _const((1, PAIR)), _const((1, PAIR))],
        out_specs=[row(N_CAT), _const((1, PAIR)), _const((1, PAIR))],
        out_shape=[jax.ShapeDtypeStruct((T, N_CAT), MXU_DTYPE), jax.ShapeDtypeStruct((1, PAIR), F32),
                   jax.ShapeDtypeStruct((1, PAIR), F32)],
        compiler_params=_params(("arbitrary",)),
    )(dq, dk, dv, qkv, df, dlx, dlg, gq2, gk2)


def _inproj_bwd(dproj, x2d, dx2, g1, wcat, slabs):
    T = x2d.shape[0]
    tm = TOKEN_TILE
    n = len(slabs)
    steps = T // tm

    def body(dp_ref, x_ref, dx2_ref, g1_ref, w_ref, *rest):
        s_in, (gx_ref, dg1_ref), s_out, sems = rest[:n], rest[n:n + 2], rest[n + 2:2 * n + 2], rest[2 * n + 2:]
        i = pl.program_id(0)
        copies = _chip_copies(s_in, s_out, *sems, scatter=True)

        @pl.when(i == 0)
        def _():
            dg1_ref[...] = jnp.zeros_like(dg1_ref)
            for cp in copies:
                cp.start()

        dh = jnp.dot(dp_ref[...], w_ref[...], preferred_element_type=F32)
        x = x_ref[...]
        r = lax.rsqrt(jnp.mean(x * x, axis=-1, keepdims=True) + NORM_EPS)
        xh = x * r
        dg1_ref[...] += jnp.sum(dh * xh, axis=0, keepdims=True)
        dxh = dh * g1_ref[...]
        gx_ref[...] = dx2_ref[...] + r * (dxh - xh * jnp.mean(dxh * xh, axis=-1, keepdims=True))

        @pl.when(i == steps - 1)
        def _():
            for cp in copies:
                cp.wait()

    row = lambda w: pl.BlockSpec((tm, w), lambda i: (i, 0))
    hbm = pl.BlockSpec(memory_space=pl.ANY)
    return pl.pallas_call(
        body, name="inproj_bwd", grid=(steps,),
        in_specs=[row(N_CAT), row(D_MODEL), row(D_MODEL), _const((1, D_MODEL)), _const((N_CAT, D_MODEL))] + [hbm] * n,
        out_specs=[row(D_MODEL), _const((1, D_MODEL))] + [hbm] * n,
        out_shape=[jax.ShapeDtypeStruct((T, D_MODEL), F32), jax.ShapeDtypeStruct((1, D_MODEL), F32)]
        + [jax.ShapeDtypeStruct(a.shape, a.dtype) for a in slabs],
        scratch_shapes=_chip_sems(n),
        compiler_params=_params(("arbitrary",), VMEM_LARGE),
    )(dproj, x2d, dx2, g1, wcat, *slabs)


ELEMENTWISE_COLS = 256


def _sum_slabs(recv, name):
    _, rows, cols = recv.shape
    cb = ELEMENTWISE_COLS

    def body(r_ref, o_ref):
        part = [r_ref[s].astype(F32) for s in range(4)]
        o_ref[...] = ((part[0] + part[1]) + part[2]) + part[3]

    return pl.pallas_call(
        body, name=name, grid=(cols // cb,),
        in_specs=[pl.BlockSpec((4, rows, cb), lambda i: (0, 0, i))],
        out_specs=pl.BlockSpec((rows, cb), lambda i: (0, i)),
        out_shape=jax.ShapeDtypeStruct((rows, cols), F32),
        compiler_params=_params(("parallel",)),
    )(recv)


def _adamw_math(w, g, m, v):
    m = ADAM_B1 * m + (1.0 - ADAM_B1) * g
    v = ADAM_B2 * v + (1.0 - ADAM_B2) * (g * g)
    m_hat = m / (1.0 - ADAM_B1 ** ADAM_STEP)
    v_hat = v / (1.0 - ADAM_B2 ** ADAM_STEP)
    delta = -ADAM_LR * (m_hat / (jnp.sqrt(v_hat) + ADAM_EPS) + ADAM_WD * w)
    return delta, m, v


def _adamw_pair(mine, theirs, w, m, v, name):
    rows, cols = w.shape
    cb = ELEMENTWISE_COLS

    def body(a_ref, b_ref, w_ref, m_ref, v_ref, g_ref, d_ref, nm_ref, nv_ref):
        g = a_ref[...] + b_ref[...]
        g_ref[...] = g
        d_ref[...], nm_ref[...], nv_ref[...] = _adamw_math(w_ref[...], g, m_ref[...], v_ref[...])

    blk = pl.BlockSpec((rows, cb), lambda i: (0, i))
    return pl.pallas_call(
        body, name=name, grid=(cols // cb,),
        in_specs=[blk] * 5, out_specs=[blk] * 4,
        out_shape=[jax.ShapeDtypeStruct((rows, cols), F32)] * 4,
        compiler_params=_params(("parallel",)),
    )(mine, theirs, w, m, v)


VEC_ROW = {"norm1_g": 0, "norm2_g": 1, "attn_out_g": 2, "lru_out_g": 3, "q_norm_g": 4, "k_norm_g": 5, "b_f": 6,
           "b_a": 8, "b_x": 9, "lam": 10, "conv_b": 11}
LOSS_ROW, CONV_W_ROW, PACK_ROWS = 7, 12, 16
SMALL = list(VEC_ROW) + ["conv_w", "w_a", "w_x"]


def _pack_small(dg1, dg2, dga, dgr, dgq, dgk, dbf, sq_err, lru_small):
    def body(dg1_ref, dg2_ref, dga_ref, dgr_ref, dgq_ref, dgk_ref, dbf_ref, err_ref, lru_ref, v_ref):
        v_ref[...] = jnp.zeros_like(v_ref)
        v_ref[0:1, :] = dg1_ref[...]
        v_ref[1:2, :] = dg2_ref[...]
        v_ref[2:3, 0:ATT_WIDTH] = dga_ref[...]
        v_ref[3:4, 0:LRU_WIDTH] = dgr_ref[...]
        for row, ref in ((4, dgq_ref), (5, dgk_ref)):
            g = ref[...]
            v_ref[row:row + 1, 0:PAIR] = g + pltpu.roll(g, HEAD_DIM, 1)
        v_ref[6:7, 0:F_PAD] = dbf_ref[...]
        v_ref[LOSS_ROW:LOSS_ROW + 1, 0:128] = err_ref[0:1, :] * (0.5 / D_MODEL)
        v_ref[8:16, 0:LRU_WIDTH] = lru_ref[...]

    ins = [dg1, dg2, dga, dgr, dgq, dgk, dbf, sq_err, lru_small]
    return pl.pallas_call(
        body, name="pack_small", grid=(1,),
        in_specs=[_const(a.shape) for a in ins], out_specs=_const((PACK_ROWS, D_MODEL)),
        out_shape=jax.ShapeDtypeStruct((PACK_ROWS, D_MODEL), F32),
        compiler_params=_params(("arbitrary",)),
    )(*ins)


def _diag_blocks(dwa_bd, dwx_bd):
    blk = LRU_WIDTH // LRU_BLOCKS

    def body(wa_ref, wx_ref, oa_ref, ox_ref):
        for src, dst in ((wa_ref, oa_ref), (wx_ref, ox_ref)):
            for nb in range(LRU_BLOCKS):
                tile = src[blk * nb:blk * (nb + 1), PAIR * (nb // 2):PAIR * (nb // 2 + 1)]
                if nb % 2:
                    tile = pltpu.roll(tile, blk, 1)
                dst[nb] = tile[:, 0:blk]

    out = jax.ShapeDtypeStruct((LRU_BLOCKS, blk, blk), F32)
    return pl.pallas_call(
        body, name="diag_blocks", grid=(1,),
        in_specs=[_const(dwa_bd.shape)] * 2, out_specs=[_const(out.shape)] * 2, out_shape=[out, out],
        compiler_params=_params(("arbitrary",)),
    )(dwa_bd, dwx_bd)


def _adamw_small(recv_v, recv_a, recv_x, params):
    names = list(params)
    flat = [a for n in names for a in params[n]]

    def body(rv_ref, ra_ref, rx_ref, *refs):
        ins, loss_ref, outs = refs[:len(flat)], refs[len(flat)], refs[len(flat) + 1:]
        x, y = lax.axis_index("x"), lax.axis_index("y")
        me = 2 * x + y

        def total(r):
            acc = r[0]
            for d in range(1, 8):
                acc = acc + r[d]
            return acc

        gv, ga, gx = total(rv_ref), total(ra_ref), total(rx_ref)
        loss_ref[...] = gv[LOSS_ROW:LOSS_ROW + 1, 0:128]
        for i, n in enumerate(names):
            w_ref, m_ref, v_ref = ins[3 * i:3 * i + 3]
            g_ref, d_ref, nm_ref, nv_ref = outs[4 * i:4 * i + 4]
            if n in VEC_ROW:
                g = gv[VEC_ROW[n]:VEC_ROW[n] + 1, 0:w_ref.shape[1]]
                w, m, v = w_ref[...], m_ref[...], v_ref[...]
            else:
                if n == "conv_w":
                    full = gv[CONV_W_ROW:CONV_W_ROW + CONV_WIDTH, 0:LRU_WIDTH]
                    width = LRU_WIDTH // 4
                    g = jnp.zeros((CONV_WIDTH, width), F32)
                    for s in range(4):
                        g = jnp.where(me == s, full[:, width * s:width * (s + 1)], g)
                else:
                    g = ga if n == "w_a" else gx
                w, m, v = w_ref[0], m_ref[0], v_ref[0]
            d, nm, nv = _adamw_math(w, g, m, v)
            for ref, val in ((g_ref, g), (d_ref, d), (nm_ref, nm), (nv_ref, nv)):
                if n in VEC_ROW:
                    ref[...] = val
                else:
                    ref[0] = val

    out_shape = [jax.ShapeDtypeStruct((1, 128), F32)] + [jax.ShapeDtypeStruct(params[n][0].shape, F32)
                                                          for n in names for _ in range(4)]
    res = pl.pallas_call(
        body, name="adamw_small", grid=(1,),
        in_specs=[_const(a.shape) for a in (recv_v, recv_a, recv_x, *flat)],
        out_specs=[_const(o.shape) for o in out_shape], out_shape=out_shape,
        compiler_params=_params(("arbitrary",)),
    )(recv_v, recv_a, recv_x, *flat)
    return res[0], {n: tuple(res[1 + 4 * i:5 + 4 * i]) for i, n in enumerate(names)}


def _cat_shards(g, pad_at=None, pad=0):
    _, rows, w = g.shape
    pieces = []
    for s in range(4):
        lo, hi = s * w, (s + 1) * w
        if pad_at is not None and lo < pad_at <= hi:
            pieces += [g[s][:, :pad_at - lo], jnp.zeros((rows, pad), g.dtype)]
            if pad_at < hi:
                pieces.append(g[s][:, pad_at - lo:])
        else:
            pieces.append(g[s])
    return jnp.concatenate(pieces, axis=1)


def _block_diag(w):
    eye = jnp.eye(LRU_BLOCKS, dtype=w.dtype)
    return (w[:, :, None, :] * eye[:, None, :, None]).reshape(LRU_WIDTH, LRU_WIDTH)


def kernel(x, norm1_g, w_in, q_norm_g, k_norm_g, b_f, conv_w, conv_b, w_a, b_a, w_x, b_x, lam, attn_out_g, lru_out_g, w_out, norm2_g, w_gate, w_up, w_down, loss_target, m_norm1_g, m_w_in, m_q_norm_g, m_k_norm_g, m_b_f, m_conv_w, m_conv_b, m_w_a, m_b_a, m_w_x, m_b_x, m_lam, m_attn_out_g, m_lru_out_g, m_w_out, m_norm2_g, m_w_gate, m_w_up, m_w_down, v_norm1_g, v_w_in, v_q_norm_g, v_k_norm_g, v_b_f, v_conv_w, v_conv_b, v_w_a, v_b_a, v_w_x, v_b_x, v_lam, v_attn_out_g, v_lru_out_g, v_w_out, v_norm2_g, v_w_gate, v_w_up, v_w_down):
    args = dict(locals())
    bl, seq, _ = x.shape
    T = bl * seq
    tq = min(ATT_TILE, seq)
    nq = seq // tq
    dff = w_gate.shape[2] * 4

    def transposed(name):
        return name.endswith(("w_in", "w_gate", "w_up"))

    def shard2d(name):
        return jnp.swapaxes(args[name], 1, 2)[0] if transposed(name) else args[name][0]

    g_in, g_cw = _gather_split(shard2d("w_in").astype(MXU_DTYPE), conv_w[0])
    later_shards = [shard2d(n).astype(MXU_DTYPE) for n in ("w_out", "w_gate", "w_up", "w_down")]
    f0 = 3 * ATT_WIDTH
    w_in_t = g_in.reshape(-1, D_MODEL)
    wcat = jnp.concatenate([w_in_t[:f0 + HEADS], jnp.zeros((F_PAD - HEADS, D_MODEL), w_in_t.dtype),
                            w_in_t[f0 + HEADS:]], axis=0)
    cw_full = _cat_shards(g_cw)
    wa_bd = _block_diag(w_a[0]).astype(MXU_DTYPE)
    wx_bd = _block_diag(w_x[0]).astype(MXU_DTYPE)
    gq2 = jnp.tile(q_norm_g, (1, 2))
    gk2 = jnp.tile(k_norm_g, (1, 2))
    bf_pad = jnp.pad(b_f, ((0, 0), (0, F_PAD - HEADS)))

    x2d = x.reshape(T, D_MODEL)
    target2d = loss_target.reshape(T, D_MODEL)

    qkv, qn, qn_t, kn, vb, f2d, lx, lg, h1b = _inproj(x2d, norm1_g, wcat, gq2, gk2)
    fcol = _forget_cumsum(f2d, bf_pad, bl, seq)
    frow = jnp.transpose(fcol.reshape(bl, seq, F_PAD)[:, :, :HEADS], (0, 2, 1)).reshape(bl, N_PAIR, 2, seq)
    fstart = frow[:, :, :, ::tq].reshape(-1)
    att, lse, g_out, g_gate, g_up, g_down = _attn_fwd(qn, kn, vb, frow, fstart, bl, seq, later_shards)
    wout_full = g_out.reshape(D_MODEL, D_MODEL)
    wg_full, wu_full = g_gate.reshape(dff, D_MODEL), g_up.reshape(dff, D_MODEL)
    wd_full = g_down.reshape(dff, D_MODEL)
    h, rec = _lru_fwd(lx, lg, cw_full, conv_b, wa_bd, b_a, wx_bd, b_x, lam, bl, seq)
    x2 = _outproj(x2d, att, rec, attn_out_g, lru_out_g, wout_full)
    gt, up, dy, sq_err = _mlp_fwd(x2, norm2_g, wg_full, wu_full, wd_full, target2d)

    dx2, dx2b, dgtb, dupb, actb, h2b, dyb, dg2 = _mlp_bwd(dy, x2, gt, up, norm2_g, wg_full, wu_full, wd_full)
    dw_down = _matmul_tn(actb, dyb, D_MODEL, "dw_down")
    dw_gate = _matmul_tn(dgtb, h2b, D_MODEL, "dw_gate")
    dw_up = _matmul_tn(dupb, h2b, D_MODEL, "dw_up")
    dattb, dattb_t, delta, drec, mixb, dga, dgr = _outproj_bwd(dx2b, att, rec, attn_out_g, lru_out_g, wout_full)
    dw_out = _matmul_tn(mixb, dx2b, D_MODEL, "dw_out")
    dlx, dlg, dwa_bd, dwx_bd, lru_small = _lru_bwd(drec, lg, h, lx, cw_full, conv_b, wa_bd, b_a, wx_bd, b_x, lam, bl, seq)
    early_slabs = [dw_out.reshape(4, D_MODEL // 4, D_MODEL), dw_gate.reshape(4, dff // 4, D_MODEL),
                   dw_up.reshape(4, dff // 4, D_MODEL), dw_down.reshape(4, dff // 4, D_MODEL)]
    pack_a, pack_x = _diag_blocks(dwa_bd, dwx_bd)
    dq, dk, dv, dfrow, *recv_early = _attn_bwd(qn, kn, vb, dattb, qn_t, dattb_t, lse, delta, frow, fstart, bl, seq,
                                               early_slabs, [pack_a, pack_x])
    recv_early, (recv_a, recv_x) = recv_early[:4], recv_early[4:]
    dfcol = jnp.pad(jnp.transpose(dfrow.reshape(bl, HEADS, seq), (0, 2, 1)), ((0, 0), (0, 0), (0, F_PAD - HEADS)))
    df, dbf = _forget_bwd(dfcol.reshape(T, F_PAD), f2d, bf_pad, bl, seq)
    dprojb, dgq, dgk = _dproj(dq, dk, dv, qkv, df, dlx, dlg, gq2, gk2)
    dwcat = _matmul_tn(dprojb, h1b, D_MODEL, "dw_in")
    dw_in_slabs = jnp.concatenate([dwcat[:f0 + HEADS], dwcat[f0 + F_PAD:]], axis=0).astype(jnp.bfloat16).reshape(
        4, -1, D_MODEL)
    grad_x, dg1, recv_in = _inproj_bwd(dprojb, x2d, dx2, norm1_g, wcat, [dw_in_slabs])

    pack_v = _pack_small(dg1, dg2, dga, dgr, dgq, dgk, dbf, sq_err, lru_small)
    recv = [recv_in] + recv_early
    big = ["w_in", "w_out", "w_gate", "w_up", "w_down"]
    part = [_sum_slabs(r, "sum_" + n) for r, n in zip(recv, big)]
    *theirs, recv_v = _swap_with_sibling(part, [pack_v])
    out = {}
    for n, a, b_ in zip(big, part, theirs):
        res = _adamw_pair(a, b_, shard2d(n), shard2d("m_" + n), shard2d("v_" + n), "adamw_" + n)
        out[n] = tuple(jnp.swapaxes(r[None], 1, 2) if transposed(n) else r[None] for r in res)
    loss_row, small_out = _adamw_small(recv_v, recv_a, recv_x,
                                       {n: (args[n], args["m_" + n], args["v_" + n]) for n in SMALL})
    out.update(small_out)
    loss = loss_row[0, 0]

    order = ["norm1_g", "w_in", "q_norm_g", "k_norm_g", "b_f", "conv_w", "conv_b", "w_a", "b_a", "w_x", "b_x", "lam",
             "attn_out_g", "lru_out_g", "w_out", "norm2_g", "w_gate", "w_up", "w_down"]
    return (loss, grad_x.reshape(bl, seq, D_MODEL), *[out[n][0] for n in order], *[out[n][1] for n in order],
            *[out[n][2] for n in order], *[out[n][3] for n in order])
```

```python
import functools
import math

import jax
import jax.numpy as jnp
from jax import lax
from jax.experimental import pallas as pl
from jax.experimental.pallas import tpu as pltpu

F32 = jnp.float32
MXU_DTYPE = jnp.bfloat16
MESH = pl.DeviceIdType.MESH

D_MODEL = 1024
ATT_WIDTH = 512
LRU_WIDTH = 512
HEADS = 8
HEAD_DIM = 64
PAIR = 2 * HEAD_DIM
N_PAIR = HEADS // 2
LRU_BLOCKS = 8
CONV_WIDTH = 4
LRU_C = 8.0
NORM_EPS = 1e-6
QK_SCALE = 1.0 / math.sqrt(HEAD_DIM)
F_PAD = 128
N_CAT = 3 * ATT_WIDTH + F_PAD + 2 * LRU_WIDTH
NEG = -1e30

ADAM_LR, ADAM_B1, ADAM_B2, ADAM_EPS, ADAM_WD, ADAM_STEP = 0.001, 0.9, 0.999, 1e-08, 0.01, 10

TOKEN_TILE = 256
ATT_TILE = 512
LRU_TILE = 256
VMEM_SMALL = 32 * 1024 * 1024
VMEM_LARGE = 56 * 1024 * 1024


def _params(sem, vmem=VMEM_SMALL):
    return pltpu.CompilerParams(dimension_semantics=sem, vmem_limit_bytes=vmem)


def _const(shape):
    nd = len(shape)
    return pl.BlockSpec(shape, lambda *_: (0,) * nd)


def _sigmoid(x):
    return 1.0 / (1.0 + jnp.exp(-x))


def _nt(a, b):
    return lax.dot_general(a, b, (((1,), (1,)), ((), ())), preferred_element_type=F32)


def _tn(a, b):
    return lax.dot_general(a, b, (((0,), (0,)), ((), ())), preferred_element_type=F32)


def _half_sums(t, lo):
    s_lo = jnp.sum(jnp.where(lo, t, 0.0), axis=-1, keepdims=True)
    s_hi = jnp.sum(jnp.where(lo, 0.0, t), axis=-1, keepdims=True)
    return jnp.where(lo, s_lo, s_hi)


def _lo_mask():
    return lax.broadcasted_iota(jnp.int32, (1, PAIR), 1) < HEAD_DIM


def _other_chips(x, y):
    return [(1 - x, y), (x, 1 - y), (1 - x, 1 - y)]


def _chip_copies(ins, outs, send_sems, recv_sems, loc_sems, scatter):
    x, y, c = lax.axis_index("x"), lax.axis_index("y"), lax.axis_index("c")
    me = 2 * x + y
    copies = []
    for w in range(len(ins)):
        copies.append(pltpu.make_async_copy(ins[w].at[me] if scatter else ins[w], outs[w].at[me], loc_sems.at[w]))
        for k, (cx, cy) in enumerate(_other_chips(x, y)):
            copies.append(pltpu.make_async_remote_copy(
                src_ref=ins[w].at[2 * cx + cy] if scatter else ins[w], dst_ref=outs[w].at[me],
                send_sem=send_sems.at[3 * w + k], recv_sem=recv_sems.at[3 * w + k],
                device_id=(cx, cy, c), device_id_type=MESH))
    return copies


def _chip_sems(n):
    return [pltpu.SemaphoreType.DMA((3 * n,)), pltpu.SemaphoreType.DMA((3 * n,)), pltpu.SemaphoreType.DMA((n,))]


def _gather_split(shard, small):
    half = shard.shape[1] // 2

    def body(w_ref, s_ref, ow_ref, os_ref, ici_send, ici_recv, d2d_send, d2d_recv, sm_send, sm_recv, loc_sems):
        x, y, c = lax.axis_index("x"), lax.axis_index("y"), lax.axis_index("c")
        me = 2 * x + y
        mine = pl.ds(pl.multiple_of(c * half, half), half)
        local = [pltpu.make_async_copy(w_ref, ow_ref.at[me], loc_sems.at[0]),
                 pltpu.make_async_copy(s_ref, os_ref.at[me], loc_sems.at[1])]
        fetch, little, forward = [], [], []
        for k, (cx, cy) in enumerate(_other_chips(x, y)):
            src_chip = 2 * cx + cy
            fetch.append(pltpu.make_async_remote_copy(
                src_ref=w_ref.at[:, mine], dst_ref=ow_ref.at[me, :, mine], send_sem=ici_send.at[k],
                recv_sem=ici_recv.at[k], device_id=(cx, cy, c), device_id_type=MESH))
            little.append(pltpu.make_async_remote_copy(
                src_ref=s_ref, dst_ref=os_ref.at[me], send_sem=sm_send.at[k], recv_sem=sm_recv.at[k],
                device_id=(cx, cy, c), device_id_type=MESH))
            forward.append(pltpu.make_async_remote_copy(
                src_ref=ow_ref.at[src_chip, :, mine], dst_ref=ow_ref.at[src_chip, :, mine], send_sem=d2d_send.at[k],
                recv_sem=d2d_recv.at[k], device_id=(x, y, 1 - c), device_id_type=MESH))
        for cp in local + fetch + little:
            cp.start()
        for k in range(3):
            fetch[k].wait_recv()
            forward[k].start()
        for cp in fetch:
            cp.wait_send()
        for cp in little + forward + local:
            cp.wait()

    return pl.pallas_call(
        body, name="gather_weights",
        out_shape=[jax.ShapeDtypeStruct((4,) + shard.shape, shard.dtype),
                   jax.ShapeDtypeStruct((4,) + small.shape, small.dtype)],
        in_specs=[pl.BlockSpec(memory_space=pl.ANY)] * 2,
        out_specs=[pl.BlockSpec(memory_space=pl.ANY)] * 2,
        scratch_shapes=[pltpu.SemaphoreType.DMA((3,))] * 6 + [pltpu.SemaphoreType.DMA((2,))],
    )(shard, small)


def _device_copies(packs_in, packs_out, psend, precv, loc_sems, loc_base):
    x, y, c = lax.axis_index("x"), lax.axis_index("y"), lax.axis_index("c")
    dev = 4 * x + 2 * y + c
    copies = []
    for j in range(len(packs_in)):
        copies.append(pltpu.make_async_copy(packs_in[j], packs_out[j].at[dev], loc_sems.at[loc_base + j]))
        for k in range(1, 8):
            fx, fy, fc = (k >> 2) & 1, (k >> 1) & 1, k & 1
            tx = (1 - x) if fx else x
            ty = (1 - y) if fy else y
            tc = (1 - c) if fc else c
            copies.append(pltpu.make_async_remote_copy(
                src_ref=packs_in[j], dst_ref=packs_out[j].at[dev],
                send_sem=psend.at[7 * j + k - 1], recv_sem=precv.at[7 * j + k - 1],
                device_id=(tx, ty, tc), device_id_type=MESH))
    return copies


def _swap_with_sibling(arrs, packs):
    n, npk = len(arrs), len(packs)

    def body(*refs):
        ins, pack_in = refs[:n], refs[n:n + npk]
        outs, pack_out = refs[n + npk:2 * n + npk], refs[2 * n + npk:2 * (n + npk)]
        send_sems, recv_sems, loc_sems, psend, precv = refs[2 * (n + npk):]
        x, y, c = lax.axis_index("x"), lax.axis_index("y"), lax.axis_index("c")
        copies = [pltpu.make_async_remote_copy(
            src_ref=ins[w], dst_ref=outs[w], send_sem=send_sems.at[w], recv_sem=recv_sems.at[w],
            device_id=(x, y, 1 - c), device_id_type=MESH) for w in range(n)]
        copies += _device_copies(pack_in, pack_out, psend, precv, loc_sems, 0)
        for cp in copies:
            cp.start()
        for cp in copies:
            cp.wait()

    return pl.pallas_call(
        body, name="swap_sibling",
        out_shape=[jax.ShapeDtypeStruct(a.shape, a.dtype) for a in arrs]
        + [jax.ShapeDtypeStruct((8,) + p.shape, p.dtype) for p in packs],
        in_specs=[pl.BlockSpec(memory_space=pl.ANY)] * (n + npk),
        out_specs=[pl.BlockSpec(memory_space=pl.ANY)] * (n + npk),
        scratch_shapes=[pltpu.SemaphoreType.DMA((n,)), pltpu.SemaphoreType.DMA((n,)), pltpu.SemaphoreType.DMA((npk,)),
                        pltpu.SemaphoreType.DMA((7 * npk,)), pltpu.SemaphoreType.DMA((7 * npk,))],
    )(*arrs, *packs)


def _head_norm(t, g2, lo):
    rr = lax.rsqrt(_half_sums(t * t, lo) * (1.0 / HEAD_DIM) + NORM_EPS)
    return t * rr * g2


def _inproj(x2d, g1, wcat, gq2, gk2):
    T = x2d.shape[0]
    tm = TOKEN_TILE

    def body(x_ref, g1_ref, w_ref, gq_ref, gk_ref, qkv_ref, qn_ref, qt_ref, kn_ref, vb_ref, f_ref, lx_ref, lg_ref, h_ref):
        x = x_ref[...]
        r = lax.rsqrt(jnp.mean(x * x, axis=-1, keepdims=True) + NORM_EPS)
        h = (x * r * g1_ref[...]).astype(MXU_DTYPE)
        h_ref[...] = h
        proj = _nt(h, w_ref[...])
        qkv_ref[...] = proj[:, :3 * ATT_WIDTH]
        lo = _lo_mask()
        for p in range(N_PAIR):
            cols = slice(PAIR * p, PAIR * (p + 1))
            q = proj[:, PAIR * p:PAIR * (p + 1)]
            k = proj[:, ATT_WIDTH + PAIR * p:ATT_WIDTH + PAIR * (p + 1)]
            qs = _head_norm(q, gq_ref[...], lo) * QK_SCALE
            qn_ref[:, cols] = qs.astype(MXU_DTYPE)
            qt_ref[cols, :] = qs.T.astype(MXU_DTYPE)
            kn_ref[:, cols] = _head_norm(k, gk_ref[...], lo).astype(MXU_DTYPE)
        vb_ref[...] = proj[:, 2 * ATT_WIDTH:3 * ATT_WIDTH].astype(MXU_DTYPE)
        f0 = 3 * ATT_WIDTH
        f_ref[...] = proj[:, f0:f0 + F_PAD]
        lx_ref[...] = proj[:, f0 + F_PAD:f0 + F_PAD + LRU_WIDTH]
        lg_ref[...] = proj[:, f0 + F_PAD + LRU_WIDTH:]

    row = lambda w: pl.BlockSpec((tm, w), lambda i: (i, 0))
    return pl.pallas_call(
        body, name="inproj", grid=(T // tm,),
        in_specs=[row(D_MODEL), _const((1, D_MODEL)), _const((N_CAT, D_MODEL)), _const((1, PAIR)), _const((1, PAIR))],
        out_specs=[row(3 * ATT_WIDTH), row(ATT_WIDTH), pl.BlockSpec((ATT_WIDTH, tm), lambda i: (0, i)), row(ATT_WIDTH),
                   row(ATT_WIDTH), row(F_PAD), row(LRU_WIDTH), row(LRU_WIDTH), row(D_MODEL)],
        out_shape=[jax.ShapeDtypeStruct((T, 3 * ATT_WIDTH), F32),
                   jax.ShapeDtypeStruct((T, ATT_WIDTH), MXU_DTYPE), jax.ShapeDtypeStruct((ATT_WIDTH, T), MXU_DTYPE),
                   jax.ShapeDtypeStruct((T, ATT_WIDTH), MXU_DTYPE),
                   jax.ShapeDtypeStruct((T, ATT_WIDTH), MXU_DTYPE), jax.ShapeDtypeStruct((T, F_PAD), F32),
                   jax.ShapeDtypeStruct((T, LRU_WIDTH), F32), jax.ShapeDtypeStruct((T, LRU_WIDTH), F32),
                   jax.ShapeDtypeStruct((T, D_MODEL), MXU_DTYPE)],
        compiler_params=_params(("parallel",), VMEM_LARGE),
    )(x2d, g1, wcat, gq2, gk2)


def _forget_cumsum(f2d, bf, bl, seq):
    def body(z_ref, b_ref, o_ref):
        z = z_ref[...] + b_ref[...]
        lf = jnp.minimum(z, 0.0) - jnp.log(1.0 + jnp.exp(-jnp.abs(z)))
        row = lax.broadcasted_iota(jnp.int32, (seq, F_PAD), 0)
        k = 1
        while k < seq:
            lf = lf + jnp.where(row >= k, pltpu.roll(lf, k, 0), 0.0)
            k *= 2
        o_ref[...] = lf

    return pl.pallas_call(
        body, name="forget_cumsum", grid=(bl,),
        in_specs=[pl.BlockSpec((seq, F_PAD), lambda b: (b, 0)), _const((1, F_PAD))],
        out_specs=pl.BlockSpec((seq, F_PAD), lambda b: (b, 0)),
        out_shape=jax.ShapeDtypeStruct(f2d.shape, F32),
        compiler_params=_params(("parallel",)),
    )(f2d, bf)


def _attn_fwd(qn, kn, vb, frow, fstart, bl, seq, shards):
    tq = min(ATT_TILE, seq)
    nq = seq // tq
    T = bl * seq
    n = len(shards)

    def body(fs_ref, q_ref, k_ref, v_ref, fr_ref, *rest):
        g_in, (o_ref, lse_ref), g_out, sems = rest[:n], rest[n:n + 2], rest[n + 2:2 * n + 2], rest[2 * n + 2:]
        b, p, i = pl.program_id(0), pl.program_id(1), pl.program_id(2)
        copies = _chip_copies(g_in, g_out, *sems, scatter=False)

        @pl.when((b == 0) & (p == 0) & (i == 0))
        def _():
            for cp in copies:
                cp.start()

        lane = lax.broadcasted_iota(jnp.int32, (1, PAIR), 1)
        rows = lax.broadcasted_iota(jnp.int32, (tq, tq), 0)
        cols = lax.broadcasted_iota(jnp.int32, (tq, tq), 1)
        causal = cols <= rows
        q = q_ref[...]
        hms = [(lane >= HEAD_DIM * hh) & (lane < HEAD_DIM * (hh + 1)) for hh in range(2)]
        qhs = [jnp.where(hm, q, jnp.zeros_like(q)) for hm in hms]
        shifts = [fs_ref[((b * N_PAIR + p) * 2 + hh) * nq + i] for hh in range(2)]
        sum_lane = [HEAD_DIM * (1 - hh) for hh in range(2)]

        def block(j, carry, masked):
            start = pl.multiple_of(j * tq, tq)
            k = k_ref[pl.ds(start, tq), :]
            v = v_ref[pl.ds(start, tq), :]
            new = []
            for hh in range(2):
                m, acc = carry[hh]
                s = lax.dot_general(qhs[hh], k, (((1,), (1,)), ((), ())), preferred_element_type=F32)
                s = s - (fr_ref[0, 0, hh:hh + 1, pl.ds(start, tq)] - shifts[hh])
                if masked:
                    s = jnp.where(causal, s, NEG)
                m_new = jnp.maximum(m, jnp.max(s, axis=-1, keepdims=True))
                alpha = jnp.exp(m - m_new)
                pe = jnp.exp(s - m_new)
                vh = jnp.where(hms[hh], v, jnp.where(lane == sum_lane[hh], 1.0, 0.0).astype(v.dtype))
                pb = pe.astype(MXU_DTYPE)
                p_lo = (pe - pb.astype(F32)).astype(MXU_DTYPE)
                acc = (alpha * acc + jnp.dot(pb, vh, preferred_element_type=F32)
                       + jnp.dot(p_lo, vh, preferred_element_type=F32))
                new.append((m_new, acc))
            return tuple(new)

        init = (jnp.full((tq, 1), NEG, F32), jnp.zeros((tq, PAIR), F32))
        carry = lax.fori_loop(0, i, functools.partial(block, masked=False), (init, init))
        carry = block(i, carry, True)
        out = jnp.zeros((tq, PAIR), F32)
        lse = jnp.zeros((tq, PAIR), F32)
        for hh in range(2):
            m, acc = carry[hh]
            l = acc[:, sum_lane[hh]:sum_lane[hh] + 1]
            out = jnp.where(hms[hh], acc * (1.0 / l), out)
            lse = jnp.where(hms[hh], m + jnp.log(l), lse)
        o_ref[...] = out
        lse_ref[...] = lse

        @pl.when((b == bl - 1) & (p == N_PAIR - 1) & (i == nq - 1))
        def _():
            for cp in copies:
                cp.wait()

    blk = pl.BlockSpec((tq, PAIR), lambda b, p, i: (b * nq + i, p))
    full = pl.BlockSpec((seq, PAIR), lambda b, p, i: (b, p))
    return pl.pallas_call(
        body, name="attn_fwd", grid=(bl, N_PAIR, nq),
        in_specs=[pl.BlockSpec(memory_space=pltpu.SMEM), blk, full, full,
                  pl.BlockSpec((1, 1, 2, seq), lambda b, p, i: (b, p, 0, 0))] + [pl.BlockSpec(memory_space=pl.ANY)] * n,
        out_specs=[blk, blk] + [pl.BlockSpec(memory_space=pl.ANY)] * n,
        out_shape=[jax.ShapeDtypeStruct((T, ATT_WIDTH), F32)] * 2
        + [jax.ShapeDtypeStruct((4,) + s.shape, s.dtype) for s in shards],
        scratch_shapes=_chip_sems(n),
        compiler_params=_params(("arbitrary", "arbitrary", "arbitrary")),
    )(fstart, qn, kn, vb, frow, *shards)


def _conv_taps(lx, prev8, cw, cb):
    xs = jnp.concatenate([prev8, lx], axis=0)
    shifted = [lx] + [pltpu.roll(xs, k, 0)[8:] for k in range(1, CONV_WIDTH)]
    xc = cb + cw[CONV_WIDTH - 1:CONV_WIDTH] * lx
    for k in range(1, CONV_WIDTH):
        xc = xc + cw[CONV_WIDTH - 1 - k:CONV_WIDTH - k] * shifted[k]
    return xc, shifted


def _lru_gates(xc, wa, ba, wx, bx, lam):
    xb = xc.astype(MXU_DTYPE)
    r = _sigmoid(jnp.dot(xb, wa, preferred_element_type=F32) + ba)
    ig = _sigmoid(jnp.dot(xb, wx, preferred_element_type=F32) + bx)
    sp = jnp.maximum(-lam, 0.0) + jnp.log(1.0 + jnp.exp(-jnp.abs(lam)))
    log_a = -LRU_C * r * sp
    a = jnp.exp(log_a)
    th = jnp.tanh(log_a)
    mult = jnp.sqrt(-2.0 * th / (1.0 - th))
    return r, ig, sp, a, mult


def _gelu_parts(x):
    c0 = math.sqrt(2.0 / math.pi)
    t = jnp.tanh(c0 * (x + 0.044715 * x * x * x))
    g = 0.5 * x * (1.0 + t)
    dg = 0.5 * (1.0 + t) + 0.5 * x * (1.0 - t * t) * c0 * (1.0 + 3.0 * 0.044715 * x * x)
    return g, dg


def _lru_fwd(lx, lg, cw, cb, wa, ba, wx, bx, lam, bl, seq):
    tc = min(LRU_TILE, seq)
    nc = seq // tc
    T = bl * seq

    def body(lx_ref, lxp_ref, lg_ref, cw_ref, cb_ref, wa_ref, ba_ref, wx_ref, bx_ref, lam_ref,
             h_ref, rec_ref, hc_ref):
        i = pl.program_id(1)

        @pl.when(i == 0)
        def _():
            hc_ref[...] = jnp.zeros_like(hc_ref)

        lxv = lx_ref[...]
        prev8 = jnp.where(i > 0, lxp_ref[...], 0.0)
        xc, _ = _conv_taps(lxv, prev8, cw_ref[...], cb_ref[...])
        _, ig, _, a, mult = _lru_gates(xc, wa_ref[...], ba_ref[...], wx_ref[...], bx_ref[...], lam_ref[...])
        u = mult * (ig * xc)
        sub = lax.broadcasted_iota(jnp.int32, (tc, LRU_WIDTH), 0) & 7
        A, B = a, u
        for k in (1, 2, 4):
            a_s = jnp.where(sub >= k, pltpu.roll(A, k, 0), 1.0)
            b_s = jnp.where(sub >= k, pltpu.roll(B, k, 0), 0.0)
            B = A * b_s + B
            A = A * a_s
        carry = hc_ref[0:1, :]
        groups = []
        for g in range(tc // 8):
            hg = A[8 * g:8 * (g + 1)] * carry + B[8 * g:8 * (g + 1)]
            groups.append(hg)
            carry = hg[7:8]
        h = jnp.concatenate(groups, axis=0)
        hc_ref[0:1, :] = carry
        h_ref[...] = h
        g, _ = _gelu_parts(lg_ref[...])
        rec_ref[...] = h * g

    tile = pl.BlockSpec((tc, LRU_WIDTH), lambda b, i: (b * nc + i, 0))
    prev = pl.BlockSpec((8, LRU_WIDTH), lambda b, i: (jnp.maximum((b * seq + i * tc) // 8 - 1, 0), 0))
    vec = _const((1, LRU_WIDTH))
    mat = _const((LRU_WIDTH, LRU_WIDTH))
    return pl.pallas_call(
        body, name="lru_fwd", grid=(bl, nc),
        in_specs=[tile, prev, tile, _const((CONV_WIDTH, LRU_WIDTH)), vec, mat, vec, mat, vec, vec],
        out_specs=[tile, tile],
        out_shape=[jax.ShapeDtypeStruct((T, LRU_WIDTH), F32), jax.ShapeDtypeStruct((T, LRU_WIDTH), F32)],
        scratch_shapes=[pltpu.VMEM((8, LRU_WIDTH), F32)],
        compiler_params=_params(("arbitrary", "arbitrary")),
    )(lx, lx, lg, cw, cb, wa, ba, wx, bx, lam)


def _outproj(x2d, att, rec, ga, gr, wout):
    T = x2d.shape[0]
    tm = TOKEN_TILE

    def body(x_ref, a_ref, r_ref, ga_ref, gr_ref, w_ref, o_ref):
        a = a_ref[...]
        rc = r_ref[...]
        na = a * lax.rsqrt(jnp.mean(a * a, axis=-1, keepdims=True) + NORM_EPS) * ga_ref[...]
        nr = rc * lax.rsqrt(jnp.mean(rc * rc, axis=-1, keepdims=True) + NORM_EPS) * gr_ref[...]
        o_ref[...] = (x_ref[...]
                      + jnp.dot(na.astype(MXU_DTYPE), w_ref[:ATT_WIDTH, :], preferred_element_type=F32)
                      + jnp.dot(nr.astype(MXU_DTYPE), w_ref[ATT_WIDTH:, :], preferred_element_type=F32))

    row = lambda w: pl.BlockSpec((tm, w), lambda i: (i, 0))
    return pl.pallas_call(
        body, name="outproj", grid=(T // tm,),
        in_specs=[row(D_MODEL), row(ATT_WIDTH), row(LRU_WIDTH), _const((1, ATT_WIDTH)), _const((1, LRU_WIDTH)),
                  _const((D_MODEL, D_MODEL))],
        out_specs=row(D_MODEL),
        out_shape=jax.ShapeDtypeStruct((T, D_MODEL), F32),
        compiler_params=_params(("parallel",)),
    )(x2d, att, rec, ga, gr, wout)


def _mlp_fwd(x2, g2, wg, wu, wd, target):
    T = x2.shape[0]
    tm = TOKEN_TILE
    dff = wg.shape[0]

    def body(x_ref, g_ref, wg_ref, wu_ref, wd_ref, t_ref, gt_ref, up_ref, dy_ref, loss_ref):
        @pl.when(pl.program_id(0) == 0)
        def _():
            loss_ref[...] = jnp.zeros_like(loss_ref)

        x = x_ref[...]
        r = lax.rsqrt(jnp.mean(x * x, axis=-1, keepdims=True) + NORM_EPS)
        h = (x * r * g_ref[...]).astype(MXU_DTYPE)
        gt = _nt(h, wg_ref[...])
        up = _nt(h, wu_ref[...])
        gt_ref[...] = gt
        up_ref[...] = up
        act = (gt * _sigmoid(gt) * up).astype(MXU_DTYPE)
        y = x + jnp.dot(act, wd_ref[...], preferred_element_type=F32)
        e = y - t_ref[...]
        dy_ref[...] = e * (1.0 / D_MODEL)
        loss_ref[...] += jnp.sum(e * e)

    row = lambda w: pl.BlockSpec((tm, w), lambda i: (i, 0))
    return pl.pallas_call(
        body, name="mlp_fwd", grid=(T // tm,),
        in_specs=[row(D_MODEL), _const((1, D_MODEL)), _const((dff, D_MODEL)), _const((dff, D_MODEL)),
                  _const((dff, D_MODEL)), row(D_MODEL)],
        out_specs=[row(dff), row(dff), row(D_MODEL), _const((8, 128))],
        out_shape=[jax.ShapeDtypeStruct((T, dff), F32), jax.ShapeDtypeStruct((T, dff), F32),
                   jax.ShapeDtypeStruct((T, D_MODEL), F32), jax.ShapeDtypeStruct((8, 128), F32)],
        compiler_params=_params(("arbitrary",), VMEM_LARGE),
    )(x2, g2, wg, wu, wd, target)


def _mlp_bwd(dy, x2, gt, up, g2, wg, wu, wd):
    T = x2.shape[0]
    tm = TOKEN_TILE
    dff = wg.shape[0]

    def body(dy_ref, x_ref, gt_ref, up_ref, g_ref, wg_ref, wu_ref, wd_ref,
             dx_ref, dxb_ref, dgt_ref, dup_ref, act_ref, h_ref, dyb_ref, dg_ref):
        @pl.when(pl.program_id(0) == 0)
        def _():
            dg_ref[...] = jnp.zeros_like(dg_ref)

        dy_v = dy_ref[...]
        dyb = dy_v.astype(MXU_DTYPE)
        dyb_ref[...] = dyb
        x = x_ref[...]
        r = lax.rsqrt(jnp.mean(x * x, axis=-1, keepdims=True) + NORM_EPS)
        xh = x * r
        h_ref[...] = (xh * g_ref[...]).astype(MXU_DTYPE)
        gt_v = gt_ref[...]
        up_v = up_ref[...]
        sg = _sigmoid(gt_v)
        silu = gt_v * sg
        act_ref[...] = (silu * up_v).astype(MXU_DTYPE)
        dact = _nt(dyb, wd_ref[...])
        dup = (dact * silu).astype(MXU_DTYPE)
        dgt = (dact * up_v * (sg * (1.0 + gt_v * (1.0 - sg)))).astype(MXU_DTYPE)
        dup_ref[...] = dup
        dgt_ref[...] = dgt
        dh = (jnp.dot(dgt, wg_ref[...], preferred_element_type=F32)
              + jnp.dot(dup, wu_ref[...], preferred_element_type=F32))
        dg_ref[...] += jnp.sum(dh * xh, axis=0, keepdims=True)
        dxh = dh * g_ref[...]
        dx = dy_v + r * (dxh - xh * jnp.mean(dxh * xh, axis=-1, keepdims=True))
        dx_ref[...] = dx
        dxb_ref[...] = dx.astype(MXU_DTYPE)

    row = lambda w: pl.BlockSpec((tm, w), lambda i: (i, 0))
    return pl.pallas_call(
        body, name="mlp_bwd", grid=(T // tm,),
        in_specs=[row(D_MODEL), row(D_MODEL), row(dff), row(dff), _const((1, D_MODEL)),
                  _const((dff, D_MODEL)), _const((dff, D_MODEL)), _const((dff, D_MODEL))],
        out_specs=[row(D_MODEL), row(D_MODEL), row(dff), row(dff), row(dff), row(D_MODEL), row(D_MODEL),
                   _const((1, D_MODEL))],
        out_shape=[jax.ShapeDtypeStruct((T, D_MODEL), F32), jax.ShapeDtypeStruct((T, D_MODEL), MXU_DTYPE),
                   jax.ShapeDtypeStruct((T, dff), MXU_DTYPE), jax.ShapeDtypeStruct((T, dff), MXU_DTYPE),
                   jax.ShapeDtypeStruct((T, dff), MXU_DTYPE), jax.ShapeDtypeStruct((T, D_MODEL), MXU_DTYPE),
                   jax.ShapeDtypeStruct((T, D_MODEL), MXU_DTYPE), jax.ShapeDtypeStruct((1, D_MODEL), F32)],
        compiler_params=_params(("arbitrary",), VMEM_LARGE),
    )(dy, x2, gt, up, g2, wg, wu, wd)


def _matmul_tn(a, b, tn, name):
    T, K = a.shape
    N = b.shape[1]
    tt = min(1024, T)

    def body(a_ref, b_ref, o_ref):
        @pl.when(pl.program_id(1) == 0)
        def _():
            o_ref[...] = jnp.zeros_like(o_ref)

        o_ref[...] += _tn(a_ref[...], b_ref[...])

    return pl.pallas_call(
        body, name=name, grid=(N // tn, T // tt),
        in_specs=[pl.BlockSpec((tt, K), lambda n, t: (t, 0)), pl.BlockSpec((tt, tn), lambda n, t: (t, n))],
        out_specs=pl.BlockSpec((K, tn), lambda n, t: (0, n)),
        out_shape=jax.ShapeDtypeStruct((K, N), F32),
        compiler_params=_params(("parallel", "arbitrary"), VMEM_LARGE),
    )(a, b)


def _outproj_bwd(dx2b, att, rec, ga, gr, wout):
    T = att.shape[0]
    tm = TOKEN_TILE

    def body(dx_ref, a_ref, r_ref, ga_ref, gr_ref, w_ref, datt_ref, dattt_ref, delta_ref, drec_ref, mix_ref, dga_ref,
             dgr_ref):
        @pl.when(pl.program_id(0) == 0)
        def _():
            dga_ref[...] = jnp.zeros_like(dga_ref)
            dgr_ref[...] = jnp.zeros_like(dgr_ref)

        dmix = _nt(dx_ref[...], w_ref[...])

        def norm_bwd(v, g, dn):
            rr = lax.rsqrt(jnp.mean(v * v, axis=-1, keepdims=True) + NORM_EPS)
            vh = v * rr
            dvh = dn * g
            dv = rr * (dvh - vh * jnp.mean(dvh * vh, axis=-1, keepdims=True))
            return vh, dv, jnp.sum(dn * vh, axis=0, keepdims=True)

        a = a_ref[...]
        ah, datt, dga = norm_bwd(a, ga_ref[...], dmix[:, :ATT_WIDTH])
        rh, drec, dgr = norm_bwd(r_ref[...], gr_ref[...], dmix[:, ATT_WIDTH:])
        dga_ref[...] += dga
        dgr_ref[...] += dgr
        mix_ref[:, :ATT_WIDTH] = (ah * ga_ref[...]).astype(MXU_DTYPE)
        mix_ref[:, ATT_WIDTH:] = (rh * gr_ref[...]).astype(MXU_DTYPE)
        dattb = datt.astype(MXU_DTYPE)
        datt_ref[...] = dattb
        dattt_ref[...] = datt.T.astype(MXU_DTYPE)
        drec_ref[...] = drec
        lo = _lo_mask()
        prod = dattb.astype(F32) * a
        for p in range(N_PAIR):
            delta_ref[:, PAIR * p:PAIR * (p + 1)] = _half_sums(prod[:, PAIR * p:PAIR * (p + 1)], lo)

    row = lambda w: pl.BlockSpec((tm, w), lambda i: (i, 0))
    return pl.pallas_call(
        body, name="outproj_bwd", grid=(T // tm,),
        in_specs=[row(D_MODEL), row(ATT_WIDTH), row(LRU_WIDTH), _const((1, ATT_WIDTH)), _const((1, LRU_WIDTH)),
                  _const((D_MODEL, D_MODEL))],
        out_specs=[row(ATT_WIDTH), pl.BlockSpec((ATT_WIDTH, tm), lambda i: (0, i)), row(ATT_WIDTH), row(LRU_WIDTH),
                   row(D_MODEL), _const((1, ATT_WIDTH)), _const((1, LRU_WIDTH))],
        out_shape=[jax.ShapeDtypeStruct((T, ATT_WIDTH), MXU_DTYPE), jax.ShapeDtypeStruct((ATT_WIDTH, T), MXU_DTYPE),
                   jax.ShapeDtypeStruct((T, ATT_WIDTH), F32),
                   jax.ShapeDtypeStruct((T, LRU_WIDTH), F32), jax.ShapeDtypeStruct((T, D_MODEL), MXU_DTYPE),
                   jax.ShapeDtypeStruct((1, ATT_WIDTH), F32), jax.ShapeDtypeStruct((1, LRU_WIDTH), F32)],
        compiler_params=_params(("arbitrary",)),
    )(dx2b, att, rec, ga, gr, wout)


def _lru_bwd(drec, lg, h, lx, cw, cb, wa, ba, wx, bx, lam, bl, seq):
    tc = min(LRU_TILE, seq)
    nc = seq // tc
    T = bl * seq
    n = tc

    def body(dr_ref, lg_ref, h_ref, hp_ref, lx_ref, lxp_ref, cw_ref, cb_ref, wa_ref, ba_ref, wx_ref, bx_ref, lam_ref,
             dlx_ref, dlg_ref, dwa_ref, dwx_ref, small_ref, gc_ref, dxn_ref):
        b, i = pl.program_id(0), pl.program_id(1)
        ir = nc - 1 - i

        @pl.when((b == 0) & (i == 0))
        def _():
            dwa_ref[...] = jnp.zeros_like(dwa_ref)
            dwx_ref[...] = jnp.zeros_like(dwx_ref)
            small_ref[...] = jnp.zeros_like(small_ref)

        @pl.when(i == 0)
        def _():
            gc_ref[...] = jnp.zeros_like(gc_ref)
            dxn_ref[...] = jnp.zeros_like(dxn_ref)

        cw = cw_ref[...]
        lam_v = lam_ref[...]
        lxv = lx_ref[...]
        prev8 = jnp.where(ir > 0, lxp_ref[...], 0.0)
        xc, shifted = _conv_taps(lxv, prev8, cw, cb_ref[...])
        r, ig, sp, a, mult = _lru_gates(xc, wa_ref[...], ba_ref[...], wx_ref[...], bx_ref[...], lam_v)
        hv = h_ref[...]
        drv = dr_ref[...]
        g, dg = _gelu_parts(lg_ref[...])
        dlg_ref[...] = drv * hv * dg
        dh = drv * g

        row = lax.broadcasted_iota(jnp.int32, (n, LRU_WIDTH), 0)
        sub = row & 7
        A = jnp.where(row < n - 1, pltpu.roll(a, n - 1, 0), 0.0)
        B = dh + jnp.where(row == n - 1, gc_ref[0:1, :], 0.0)
        for k in (1, 2, 4):
            a_s = jnp.where(sub < 8 - k, pltpu.roll(A, n - k, 0), 1.0)
            b_s = jnp.where(sub < 8 - k, pltpu.roll(B, n - k, 0), 0.0)
            B = B + A * b_s
            A = A * a_s
        carry = jnp.zeros((1, LRU_WIDTH), F32)
        groups = [None] * (n // 8)
        for g in reversed(range(n // 8)):
            gg = B[8 * g:8 * (g + 1)] + A[8 * g:8 * (g + 1)] * carry
            groups[g] = gg
            carry = gg[0:1]
        gs = jnp.concatenate(groups, axis=0)
        gc_ref[0:1, :] = a[0:1, :] * carry

        hprev8 = jnp.where(ir > 0, hp_ref[...], 0.0)
        h_prev = pltpu.roll(jnp.concatenate([hprev8, hv], axis=0), 1, 0)[8:]
        da = gs * h_prev
        ix = ig * xc
        dmult = gs * ix
        dig = gs * mult * xc
        dxc = gs * mult * ig
        dlog_a = da * a - dmult * (a * a) / mult
        dr_gate = dlog_a * (-LRU_C * sp)
        dza = dr_gate * r * (1.0 - r)
        dzx = dig * ig * (1.0 - ig)
        dzab = dza.astype(MXU_DTYPE)
        dzxb = dzx.astype(MXU_DTYPE)
        xcb = xc.astype(MXU_DTYPE)
        dwa_ref[...] += _tn(xcb, dzab)
        dwx_ref[...] += _tn(xcb, dzxb)
        dxc = dxc + _nt(dzab, wa_ref[...]) + _nt(dzxb, wx_ref[...])

        ds = jnp.concatenate([dxc, dxn_ref[...]], axis=0)
        dlx = cw[CONV_WIDTH - 1:CONV_WIDTH] * dxc
        for k in range(1, CONV_WIDTH):
            dlx = dlx + cw[CONV_WIDTH - 1 - k:CONV_WIDTH - k] * pltpu.roll(ds, n + 8 - k, 0)[:n]
        dlx_ref[...] = dlx
        dxn_ref[...] = dxc[0:8, :]

        colsum = lambda v: jnp.sum(v, axis=0, keepdims=True)
        small_ref[0:1, :] += colsum(dza)
        small_ref[1:2, :] += colsum(dzx)
        small_ref[2:3, :] += colsum(dlog_a * r) * (LRU_C * _sigmoid(-lam_v))
        small_ref[3:4, :] += colsum(dxc)
        for k in range(CONV_WIDTH):
            j = CONV_WIDTH - 1 - k
            small_ref[4 + j:5 + j, :] += colsum(dxc * shifted[k])

    tile = pl.BlockSpec((tc, LRU_WIDTH), lambda b, i: (b * nc + (nc - 1 - i), 0))
    prev = pl.BlockSpec((8, LRU_WIDTH), lambda b, i: (jnp.maximum((b * seq + (nc - 1 - i) * tc) // 8 - 1, 0), 0))
    vec = _const((1, LRU_WIDTH))
    mat = _const((LRU_WIDTH, LRU_WIDTH))
    return pl.pallas_call(
        body, name="lru_bwd", grid=(bl, nc),
        in_specs=[tile, tile, tile, prev, tile, prev, _const((CONV_WIDTH, LRU_WIDTH)), vec, mat, vec, mat, vec, vec],
        out_specs=[tile, tile, mat, mat, _const((8, LRU_WIDTH))],
        out_shape=[jax.ShapeDtypeStruct((T, LRU_WIDTH), F32), jax.ShapeDtypeStruct((T, LRU_WIDTH), F32),
                   jax.ShapeDtypeStruct((LRU_WIDTH, LRU_WIDTH), F32), jax.ShapeDtypeStruct((LRU_WIDTH, LRU_WIDTH), F32),
                   jax.ShapeDtypeStruct((8, LRU_WIDTH), F32)],
        scratch_shapes=[pltpu.VMEM((8, LRU_WIDTH), F32), pltpu.VMEM((8, LRU_WIDTH), F32)],
        compiler_params=_params(("arbitrary", "arbitrary")),
    )(drec, lg, h, h, lx, lx, cw, cb, wa, ba, wx, bx, lam)


def _attn_bwd(qn, kn, vb, dob, qt, dot_, lse, delta, frow, fstart, bl, seq, slabs, packs):
    tq = min(ATT_TILE, seq)
    nq = seq // tq
    T = bl * seq
    n, npk = len(slabs), len(packs)
    nx = n + npk

    def body(fs_ref, q_ref, k_ref, v_ref, do_ref, qt_ref, dot_ref, lse_ref, dl_ref, fr_ref, *rest):
        x_in, (dq_ref, dk_ref, dv_ref, df_ref), x_out = rest[:nx], rest[nx:nx + 4], rest[nx + 4:2 * nx + 4]
        send_sems, recv_sems, loc_sems, psend, precv = rest[2 * nx + 4:]
        b, p, j = pl.program_id(0), pl.program_id(1), pl.program_id(2)
        copies = _chip_copies(x_in[:n], x_out[:n], send_sems, recv_sems, loc_sems, scatter=True)
        copies += _device_copies(x_in[n:], x_out[n:], psend, precv, loc_sems, n)

        @pl.when((b == 0) & (p == 0) & (j == 0))
        def _():
            for cp in copies:
                cp.start()

        @pl.when(j == 0)
        def _():
            dq_ref[...] = jnp.zeros_like(dq_ref)

        lane = lax.broadcasted_iota(jnp.int32, (1, PAIR), 1)
        rows = lax.broadcasted_iota(jnp.int32, (tq, tq), 0)
        cols = lax.broadcasted_iota(jnp.int32, (tq, tq), 1)
        causal = cols <= rows
        kv = k_ref[...]
        vv = v_ref[...]
        hms = [(lane >= HEAD_DIM * hh) & (lane < HEAD_DIM * (hh + 1)) for hh in range(2)]
        srow = lax.broadcasted_iota(jnp.int32, (PAIR, 1), 0)
        hms_t = [(srow >= HEAD_DIM * hh) & (srow < HEAD_DIM * (hh + 1)) for hh in range(2)]
        khs = [jnp.where(hm, kv, jnp.zeros_like(kv)) for hm in hms]
        fks = [fr_ref[0, 0, hh:hh + 1, :] for hh in range(2)]
        bases = [((b * N_PAIR + p) * 2 + hh) * nq for hh in range(2)]

        def block(i, carry, masked):
            dk, dv, dfs = carry
            start = pl.multiple_of(i * tq, tq)
            qi = q_ref[pl.ds(start, tq), :]
            doi = do_ref[pl.ds(start, tq), :]
            qti = qt_ref[:, pl.ds(start, tq)]
            doti = dot_ref[:, pl.ds(start, tq)]
            dq = jnp.zeros((tq, PAIR), F32)
            new_dfs = []
            for hh in range(2):
                c0 = HEAD_DIM * hh
                qh = jnp.where(hms[hh], qi, jnp.zeros_like(qi))
                doh = jnp.where(hms[hh], doi, jnp.zeros_like(doi))
                s = _nt(qh, kv) - (fks[hh] - fs_ref[bases[hh] + i])
                if masked:
                    s = jnp.where(causal, s, NEG)
                pr = jnp.exp(s - lse_ref[pl.ds(start, tq), c0:c0 + 1])
                dp = _nt(doh, vv)
                ds = pr * (dp - dl_ref[pl.ds(start, tq), c0:c0 + 1])
                dsb = ds.astype(MXU_DTYPE)
                dv = dv + jnp.dot(jnp.where(hms_t[hh], doti, jnp.zeros_like(doti)), pr.astype(MXU_DTYPE),
                                  preferred_element_type=F32)
                dk = dk + jnp.dot(jnp.where(hms_t[hh], qti, jnp.zeros_like(qti)), dsb, preferred_element_type=F32)
                dq = dq + jnp.dot(dsb, khs[hh], preferred_element_type=F32)
                new_dfs.append(dfs[hh] - jnp.sum(ds, axis=0, keepdims=True))
            dq_ref[pl.ds(start, tq), :] += dq
            return dk, dv, tuple(new_dfs)

        zero = jnp.zeros((PAIR, tq), F32)
        carry = block(j, (zero, zero, (jnp.zeros((1, tq), F32), jnp.zeros((1, tq), F32))), True)
        dk, dv, dfs = lax.fori_loop(j + 1, nq, functools.partial(block, masked=False), carry)
        for hh in range(2):
            df_ref[0, 0, hh:hh + 1, :] = dfs[hh]
        dk_ref[...] = dk.T
        dv_ref[...] = dv.T

        @pl.when((b == bl - 1) & (p == N_PAIR - 1) & (j == nq - 1))
        def _():
            for cp in copies:
                cp.wait()

    blk = pl.BlockSpec((tq, PAIR), lambda b, p, j: (b * nq + j, p))
    full = pl.BlockSpec((seq, PAIR), lambda b, p, j: (b, p))
    fblk = pl.BlockSpec((1, 1, 2, tq), lambda b, p, j: (b, p, 0, j))
    full_t = pl.BlockSpec((PAIR, seq), lambda b, p, j: (p, b))
    hbm = pl.BlockSpec(memory_space=pl.ANY)
    return pl.pallas_call(
        body, name="attn_bwd", grid=(bl, N_PAIR, nq),
        in_specs=[pl.BlockSpec(memory_space=pltpu.SMEM), full, blk, blk, full, full_t, full_t, full, full, fblk]
        + [hbm] * nx,
        out_specs=[full, blk, blk, fblk] + [hbm] * nx,
        out_shape=[jax.ShapeDtypeStruct((T, ATT_WIDTH), F32), jax.ShapeDtypeStruct((T, ATT_WIDTH), F32),
                   jax.ShapeDtypeStruct((T, ATT_WIDTH), F32), jax.ShapeDtypeStruct((bl, N_PAIR, 2, seq), F32)]
        + [jax.ShapeDtypeStruct(s.shape, s.dtype) for s in slabs]
        + [jax.ShapeDtypeStruct((8,) + p.shape, p.dtype) for p in packs],
        scratch_shapes=[pltpu.SemaphoreType.DMA((3 * n,)), pltpu.SemaphoreType.DMA((3 * n,)),
                        pltpu.SemaphoreType.DMA((nx,)),
                        pltpu.SemaphoreType.DMA((7 * npk,)), pltpu.SemaphoreType.DMA((7 * npk,))],
        compiler_params=_params(("arbitrary", "arbitrary", "arbitrary"), VMEM_LARGE),
    )(fstart, qn, kn, vb, dob, qt, dot_, lse, delta, frow, *slabs, *packs)


def _forget_bwd(dfcol, f2d, bf, bl, seq):
    def body(d_ref, z_ref, b_ref, o_ref, db_ref):
        @pl.when(pl.program_id(0) == 0)
        def _():
            db_ref[...] = jnp.zeros_like(db_ref)

        d = d_ref[...]
        row = lax.broadcasted_iota(jnp.int32, (seq, F_PAD), 0)
        k = 1
        while k < seq:
            d = d + jnp.where(row < seq - k, pltpu.roll(d, seq - k, 0), 0.0)
            k *= 2
        dz = d * _sigmoid(-(z_ref[...] + b_ref[...]))
        o_ref[...] = dz
        db_ref[...] += jnp.sum(dz, axis=0, keepdims=True)

    blk = pl.BlockSpec((seq, F_PAD), lambda b: (b, 0))
    return pl.pallas_call(
        body, name="forget_bwd", grid=(bl,),
        in_specs=[blk, blk, _const((1, F_PAD))],
        out_specs=[blk, _const((1, F_PAD))],
        out_shape=[jax.ShapeDtypeStruct(f2d.shape, F32), jax.ShapeDtypeStruct((1, F_PAD), F32)],
        compiler_params=_params(("arbitrary",)),
    )(dfcol, f2d, bf)


def _dproj(dq, dk, dv, qkv, df, dlx, dlg, gq2, gk2):
    T = dq.shape[0]
    tm = TOKEN_TILE

    def body(dq_ref, dk_ref, dv_ref, qkv_ref, df_ref, dlx_ref, dlg_ref, gq_ref, gk_ref, dp_ref, dgq_ref, dgk_ref):
        @pl.when(pl.program_id(0) == 0)
        def _():
            dgq_ref[...] = jnp.zeros_like(dgq_ref)
            dgk_ref[...] = jnp.zeros_like(dgk_ref)

        lo = _lo_mask()

        def head_norm_bwd(t, g2, dy):
            rr = lax.rsqrt(_half_sums(t * t, lo) * (1.0 / HEAD_DIM) + NORM_EPS)
            th = t * rr
            dth = dy * g2
            mm = _half_sums(dth * th, lo) * (1.0 / HEAD_DIM)
            return rr * (dth - th * mm), jnp.sum(dy * th, axis=0, keepdims=True)

        dgq = jnp.zeros((1, PAIR), F32)
        dgk = jnp.zeros((1, PAIR), F32)
        for p in range(N_PAIR):
            cq = slice(PAIR * p, PAIR * (p + 1))
            ck = slice(ATT_WIDTH + PAIR * p, ATT_WIDTH + PAIR * (p + 1))
            dqp, g_ = head_norm_bwd(qkv_ref[:, cq], gq_ref[...], dq_ref[:, cq] * QK_SCALE)
            dgq = dgq + g_
            dp_ref[:, cq] = dqp.astype(MXU_DTYPE)
            dkp, g_ = head_norm_bwd(qkv_ref[:, ck], gk_ref[...], dk_ref[:, cq])
            dgk = dgk + g_
            dp_ref[:, ck] = dkp.astype(MXU_DTYPE)
        dgq_ref[...] += dgq
        dgk_ref[...] += dgk
        f0 = 3 * ATT_WIDTH
        dp_ref[:, 2 * ATT_WIDTH:f0] = dv_ref[...].astype(MXU_DTYPE)
        dp_ref[:, f0:f0 + F_PAD] = df_ref[...].astype(MXU_DTYPE)
        dp_ref[:, f0 + F_PAD:f0 + F_PAD + LRU_WIDTH] = dlx_ref[...].astype(MXU_DTYPE)
        dp_ref[:, f0 + F_PAD + LRU_WIDTH:] = dlg_ref[...].astype(MXU_DTYPE)

    row = lambda w: pl.BlockSpec((tm, w), lambda i: (i, 0))
    return pl.pallas_call(
        body, name="dproj", grid=(T // tm,),
        in_specs=[row(ATT_WIDTH), row(ATT_WIDTH), row(ATT_WIDTH), row(3 * ATT_WIDTH), row(F_PAD), row(LRU_WIDTH),
                  row(LRU_WIDTH), _const((1, PAIR)), _const((1, PAIR))],
        out_specs=[row(N_CAT), _const((1, PAIR)), _const((1, PAIR))],
        out_shape=[jax.ShapeDtypeStruct((T, N_CAT), MXU_DTYPE), jax.ShapeDtypeStruct((1, PAIR), F32),
                   jax.ShapeDtypeStruct((1, PAIR), F32)],
        compiler_params=_params(("arbitrary",)),
    )(dq, dk, dv, qkv, df, dlx, dlg, gq2, gk2)


def _inproj_bwd(dproj, x2d, dx2, g1, wcat, slabs):
    T = x2d.shape[0]
    tm = TOKEN_TILE
    n = len(slabs)
    steps = T // tm

    def body(dp_ref, x_ref, dx2_ref, g1_ref, w_ref, *rest):
        s_in, (gx_ref, dg1_ref), s_out, sems = rest[:n], rest[n:n + 2], rest[n + 2:2 * n + 2], rest[2 * n + 2:]
        i = pl.program_id(0)
        copies = _chip_copies(s_in, s_out, *sems, scatter=True)

        @pl.when(i == 0)
        def _():
            dg1_ref[...] = jnp.zeros_like(dg1_ref)
            for cp in copies:
                cp.start()

        dh = jnp.dot(dp_ref[...], w_ref[...], preferred_element_type=F32)
        x = x_ref[...]
        r = lax.rsqrt(jnp.mean(x * x, axis=-1, keepdims=True) + NORM_EPS)
        xh = x * r
        dg1_ref[...] += jnp.sum(dh * xh, axis=0, keepdims=True)
        dxh = dh * g1_ref[...]
        gx_ref[...] = dx2_ref[...] + r * (dxh - xh * jnp.mean(dxh * xh, axis=-1, keepdims=True))

        @pl.when(i == steps - 1)
        def _():
            for cp in copies:
                cp.wait()

    row = lambda w: pl.BlockSpec((tm, w), lambda i: (i, 0))
    hbm = pl.BlockSpec(memory_space=pl.ANY)
    return pl.pallas_call(
        body, name="inproj_bwd", grid=(steps,),
        in_specs=[row(N_CAT), row(D_MODEL), row(D_MODEL), _const((1, D_MODEL)), _const((N_CAT, D_MODEL))] + [hbm] * n,
        out_specs=[row(D_MODEL), _const((1, D_MODEL))] + [hbm] * n,
        out_shape=[jax.ShapeDtypeStruct((T, D_MODEL), F32), jax.ShapeDtypeStruct((1, D_MODEL), F32)]
        + [jax.ShapeDtypeStruct(a.shape, a.dtype) for a in slabs],
        scratch_shapes=_chip_sems(n),
        compiler_params=_params(("arbitrary",), VMEM_LARGE),
    )(dproj, x2d, dx2, g1, wcat, *slabs)


ELEMENTWISE_COLS = 256


def _sum_slabs(recvs):
    n = len(recvs)
    cols = recvs[0].shape[2]
    cb = ELEMENTWISE_COLS

    def body(*refs):
        for r_ref, o_ref in zip(refs[:n], refs[n:]):
            part = [r_ref[s].astype(F32) for s in range(4)]
            o_ref[...] = ((part[0] + part[1]) + part[2]) + part[3]

    return pl.pallas_call(
        body, name="sum_slabs", grid=(cols // cb,),
        in_specs=[pl.BlockSpec((4, r.shape[1], cb), lambda i: (0, 0, i)) for r in recvs],
        out_specs=[pl.BlockSpec((r.shape[1], cb), lambda i: (0, i)) for r in recvs],
        out_shape=[jax.ShapeDtypeStruct(r.shape[1:], F32) for r in recvs],
        compiler_params=_params(("parallel",), VMEM_LARGE),
    )(*recvs)


def _adamw_math(w, g, m, v):
    m = ADAM_B1 * m + (1.0 - ADAM_B1) * g
    v = ADAM_B2 * v + (1.0 - ADAM_B2) * (g * g)
    m_hat = m / (1.0 - ADAM_B1 ** ADAM_STEP)
    v_hat = v / (1.0 - ADAM_B2 ** ADAM_STEP)
    delta = -ADAM_LR * (m_hat / (jnp.sqrt(v_hat) + ADAM_EPS) + ADAM_WD * w)
    return delta, m, v


def _adamw_pairs(groups):
    n = len(groups)
    cols = groups[0][0].shape[1]
    cb = ELEMENTWISE_COLS // 2

    def body(*refs):
        ins, outs = refs[:5 * n], refs[5 * n:]
        for w in range(n):
            a_ref, b_ref, w_ref, m_ref, v_ref = ins[5 * w:5 * w + 5]
            g_ref, d_ref, nm_ref, nv_ref = outs[4 * w:4 * w + 4]
            g = a_ref[...] + b_ref[...]
            g_ref[...] = g
            d_ref[...], nm_ref[...], nv_ref[...] = _adamw_math(w_ref[...], g, m_ref[...], v_ref[...])

    blk = lambda rows: pl.BlockSpec((rows, cb), lambda i: (0, i))
    res = pl.pallas_call(
        body, name="adamw_big", grid=(cols // cb,),
        in_specs=[blk(g[0].shape[0]) for g in groups for _ in range(5)],
        out_specs=[blk(g[0].shape[0]) for g in groups for _ in range(4)],
        out_shape=[jax.ShapeDtypeStruct(g[0].shape, F32) for g in groups for _ in range(4)],
        compiler_params=_params(("parallel",), VMEM_LARGE),
    )(*[a for g in groups for a in g])
    return [res[4 * w:4 * w + 4] for w in range(n)]


VEC_ROW = {"norm1_g": 0, "norm2_g": 1, "attn_out_g": 2, "lru_out_g": 3, "q_norm_g": 4, "k_norm_g": 5, "b_f": 6,
           "b_a": 8, "b_x": 9, "lam": 10, "conv_b": 11}
LOSS_ROW, CONV_W_ROW, PACK_ROWS = 7, 12, 16
SMALL = list(VEC_ROW) + ["conv_w", "w_a", "w_x"]


def _pack_small(dg1, dg2, dga, dgr, dgq, dgk, dbf, sq_err, lru_small):
    def body(dg1_ref, dg2_ref, dga_ref, dgr_ref, dgq_ref, dgk_ref, dbf_ref, err_ref, lru_ref, v_ref):
        v_ref[...] = jnp.zeros_like(v_ref)
        v_ref[0:1, :] = dg1_ref[...]
        v_ref[1:2, :] = dg2_ref[...]
        v_ref[2:3, 0:ATT_WIDTH] = dga_ref[...]
        v_ref[3:4, 0:LRU_WIDTH] = dgr_ref[...]
        for row, ref in ((4, dgq_ref), (5, dgk_ref)):
            g = ref[...]
            v_ref[row:row + 1, 0:PAIR] = g + pltpu.roll(g, HEAD_DIM, 1)
        v_ref[6:7, 0:F_PAD] = dbf_ref[...]
        v_ref[LOSS_ROW:LOSS_ROW + 1, 0:128] = err_ref[0:1, :] * (0.5 / D_MODEL)
        v_ref[8:16, 0:LRU_WIDTH] = lru_ref[...]

    ins = [dg1, dg2, dga, dgr, dgq, dgk, dbf, sq_err, lru_small]
    return pl.pallas_call(
        body, name="pack_small", grid=(1,),
        in_specs=[_const(a.shape) for a in ins], out_specs=_const((PACK_ROWS, D_MODEL)),
        out_shape=jax.ShapeDtypeStruct((PACK_ROWS, D_MODEL), F32),
        compiler_params=_params(("arbitrary",)),
    )(*ins)


def _diag_blocks(dwa_bd, dwx_bd):
    blk = LRU_WIDTH // LRU_BLOCKS

    def body(wa_ref, wx_ref, oa_ref, ox_ref):
        for src, dst in ((wa_ref, oa_ref), (wx_ref, ox_ref)):
            for nb in range(LRU_BLOCKS):
                tile = src[blk * nb:blk * (nb + 1), PAIR * (nb // 2):PAIR * (nb // 2 + 1)]
                if nb % 2:
                    tile = pltpu.roll(tile, blk, 1)
                dst[nb] = tile[:, 0:blk]

    out = jax.ShapeDtypeStruct((LRU_BLOCKS, blk, blk), F32)
    return pl.pallas_call(
        body, name="diag_blocks", grid=(1,),
        in_specs=[_const(dwa_bd.shape)] * 2, out_specs=[_const(out.shape)] * 2, out_shape=[out, out],
        compiler_params=_params(("arbitrary",)),
    )(dwa_bd, dwx_bd)


def _adamw_small(recv_v, recv_a, recv_x, params):
    names = list(params)
    flat = [a for n in names for a in params[n]]

    def body(rv_ref, ra_ref, rx_ref, *refs):
        ins, loss_ref, outs = refs[:len(flat)], refs[len(flat)], refs[len(flat) + 1:]
        x, y = lax.axis_index("x"), lax.axis_index("y")
        me = 2 * x + y

        def total(r):
            acc = r[0]
            for d in range(1, 8):
                acc = acc + r[d]
            return acc

        gv, ga, gx = total(rv_ref), total(ra_ref), total(rx_ref)
        loss_ref[...] = gv[LOSS_ROW:LOSS_ROW + 1, 0:128]
        for i, n in enumerate(names):
            w_ref, m_ref, v_ref = ins[3 * i:3 * i + 3]
            g_ref, d_ref, nm_ref, nv_ref = outs[4 * i:4 * i + 4]
            if n in VEC_ROW:
                g = gv[VEC_ROW[n]:VEC_ROW[n] + 1, 0:w_ref.shape[1]]
                w, m, v = w_ref[...], m_ref[...], v_ref[...]
            else:
                if n == "conv_w":
                    full = gv[CONV_W_ROW:CONV_W_ROW + CONV_WIDTH, 0:LRU_WIDTH]
                    width = LRU_WIDTH // 4
                    g = jnp.zeros((CONV_WIDTH, width), F32)
                    for s in range(4):
                        g = jnp.where(me == s, full[:, width * s:width * (s + 1)], g)
                else:
                    g = ga if n == "w_a" else gx
                w, m, v = w_ref[0], m_ref[0], v_ref[0]
            d, nm, nv = _adamw_math(w, g, m, v)
            for ref, val in ((g_ref, g), (d_ref, d), (nm_ref, nm), (nv_ref, nv)):
                if n in VEC_ROW:
                    ref[...] = val
                else:
                    ref[0] = val

    out_shape = [jax.ShapeDtypeStruct((1, 128), F32)] + [jax.ShapeDtypeStruct(params[n][0].shape, F32)
                                                          for n in names for _ in range(4)]
    res = pl.pallas_call(
        body, name="adamw_small", grid=(1,),
        in_specs=[_const(a.shape) for a in (recv_v, recv_a, recv_x, *flat)],
        out_specs=[_const(o.shape) for o in out_shape], out_shape=out_shape,
        compiler_params=_params(("arbitrary",)),
    )(recv_v, recv_a, recv_x, *flat)
    return res[0], {n: tuple(res[1 + 4 * i:5 + 4 * i]) for i, n in enumerate(names)}


def _cat_shards(g, pad_at=None, pad=0):
    _, rows, w = g.shape
    pieces = []
    for s in range(4):
        lo, hi = s * w, (s + 1) * w
        if pad_at is not None and lo < pad_at <= hi:
            pieces += [g[s][:, :pad_at - lo], jnp.zeros((rows, pad), g.dtype)]
            if pad_at < hi:
                pieces.append(g[s][:, pad_at - lo:])
        else:
            pieces.append(g[s])
    return jnp.concatenate(pieces, axis=1)


def _block_diag(w):
    eye = jnp.eye(LRU_BLOCKS, dtype=w.dtype)
    return (w[:, :, None, :] * eye[:, None, :, None]).reshape(LRU_WIDTH, LRU_WIDTH)


def kernel(x, norm1_g, w_in, q_norm_g, k_norm_g, b_f, conv_w, conv_b, w_a, b_a, w_x, b_x, lam, attn_out_g, lru_out_g, w_out, norm2_g, w_gate, w_up, w_down, loss_target, m_norm1_g, m_w_in, m_q_norm_g, m_k_norm_g, m_b_f, m_conv_w, m_conv_b, m_w_a, m_b_a, m_w_x, m_b_x, m_lam, m_attn_out_g, m_lru_out_g, m_w_out, m_norm2_g, m_w_gate, m_w_up, m_w_down, v_norm1_g, v_w_in, v_q_norm_g, v_k_norm_g, v_b_f, v_conv_w, v_conv_b, v_w_a, v_b_a, v_w_x, v_b_x, v_lam, v_attn_out_g, v_lru_out_g, v_w_out, v_norm2_g, v_w_gate, v_w_up, v_w_down):
    args = dict(locals())
    bl, seq, _ = x.shape
    T = bl * seq
    tq = min(ATT_TILE, seq)
    nq = seq // tq
    dff = w_gate.shape[2] * 4

    def transposed(name):
        return name.endswith(("w_in", "w_gate", "w_up"))

    def shard2d(name):
        return jnp.swapaxes(args[name], 1, 2)[0] if transposed(name) else args[name][0]

    g_in, g_cw = _gather_split(shard2d("w_in").astype(MXU_DTYPE), conv_w[0])
    later_shards = [shard2d(n).astype(MXU_DTYPE) for n in ("w_out", "w_gate", "w_up", "w_down")]
    f0 = 3 * ATT_WIDTH
    w_in_t = g_in.reshape(-1, D_MODEL)
    wcat = jnp.concatenate([w_in_t[:f0 + HEADS], jnp.zeros((F_PAD - HEADS, D_MODEL), w_in_t.dtype),
                            w_in_t[f0 + HEADS:]], axis=0)
    cw_full = _cat_shards(g_cw)
    wa_bd = _block_diag(w_a[0]).astype(MXU_DTYPE)
    wx_bd = _block_diag(w_x[0]).astype(MXU_DTYPE)
    gq2 = jnp.tile(q_norm_g, (1, 2))
    gk2 = jnp.tile(k_norm_g, (1, 2))
    bf_pad = jnp.pad(b_f, ((0, 0), (0, F_PAD - HEADS)))

    x2d = x.reshape(T, D_MODEL)
    target2d = loss_target.reshape(T, D_MODEL)

    qkv, qn, qn_t, kn, vb, f2d, lx, lg, h1b = _inproj(x2d, norm1_g, wcat, gq2, gk2)
    fcol = _forget_cumsum(f2d, bf_pad, bl, seq)
    frow = jnp.transpose(fcol.reshape(bl, seq, F_PAD)[:, :, :HEADS], (0, 2, 1)).reshape(bl, N_PAIR, 2, seq)
    fstart = frow[:, :, :, ::tq].reshape(-1)
    att, lse, g_out, g_gate, g_up, g_down = _attn_fwd(qn, kn, vb, frow, fstart, bl, seq, later_shards)
    wout_full = g_out.reshape(D_MODEL, D_MODEL)
    wg_full, wu_full = g_gate.reshape(dff, D_MODEL), g_up.reshape(dff, D_MODEL)
    wd_full = g_down.reshape(dff, D_MODEL)
    h, rec = _lru_fwd(lx, lg, cw_full, conv_b, wa_bd, b_a, wx_bd, b_x, lam, bl, seq)
    x2 = _outproj(x2d, att, rec, attn_out_g, lru_out_g, wout_full)
    gt, up, dy, sq_err = _mlp_fwd(x2, norm2_g, wg_full, wu_full, wd_full, target2d)

    dx2, dx2b, dgtb, dupb, actb, h2b, dyb, dg2 = _mlp_bwd(dy, x2, gt, up, norm2_g, wg_full, wu_full, wd_full)
    dw_down = _matmul_tn(actb, dyb, D_MODEL, "dw_down")
    dw_gate = _matmul_tn(dgtb, h2b, D_MODEL, "dw_gate")
    dw_up = _matmul_tn(dupb, h2b, D_MODEL, "dw_up")
    dattb, dattb_t, delta, drec, mixb, dga, dgr = _outproj_bwd(dx2b, att, rec, attn_out_g, lru_out_g, wout_full)
    dw_out = _matmul_tn(mixb, dx2b, D_MODEL, "dw_out")
    dlx, dlg, dwa_bd, dwx_bd, lru_small = _lru_bwd(drec, lg, h, lx, cw_full, conv_b, wa_bd, b_a, wx_bd, b_x, lam, bl, seq)
    early_slabs = [dw_out.reshape(4, D_MODEL // 4, D_MODEL), dw_gate.reshape(4, dff // 4, D_MODEL),
                   dw_up.reshape(4, dff // 4, D_MODEL), dw_down.reshape(4, dff // 4, D_MODEL)]
    pack_a, pack_x = _diag_blocks(dwa_bd, dwx_bd)
    dq, dk, dv, dfrow, *recv_early = _attn_bwd(qn, kn, vb, dattb, qn_t, dattb_t, lse, delta, frow, fstart, bl, seq,
                                               early_slabs, [pack_a, pack_x])
    recv_early, (recv_a, recv_x) = recv_early[:4], recv_early[4:]
    dfcol = jnp.pad(jnp.transpose(dfrow.reshape(bl, HEADS, seq), (0, 2, 1)), ((0, 0), (0, 0), (0, F_PAD - HEADS)))
    df, dbf = _forget_bwd(dfcol.reshape(T, F_PAD), f2d, bf_pad, bl, seq)
    dprojb, dgq, dgk = _dproj(dq, dk, dv, qkv, df, dlx, dlg, gq2, gk2)
    dwcat = _matmul_tn(dprojb, h1b, D_MODEL, "dw_in")
    dw_in_slabs = jnp.concatenate([dwcat[:f0 + HEADS], dwcat[f0 + F_PAD:]], axis=0).astype(jnp.bfloat16).reshape(
        4, -1, D_MODEL)
    grad_x, dg1, recv_in = _inproj_bwd(dprojb, x2d, dx2, norm1_g, wcat, [dw_in_slabs])

    pack_v = _pack_small(dg1, dg2, dga, dgr, dgq, dgk, dbf, sq_err, lru_small)
    recv = [recv_in] + recv_early
    big = ["w_in", "w_out", "w_gate", "w_up", "w_down"]
    part = _sum_slabs(recv)
    *theirs, recv_v = _swap_with_sibling(part, [pack_v])
    results = _adamw_pairs([(a, b_, shard2d(n), shard2d("m_" + n), shard2d("v_" + n))
                            for n, a, b_ in zip(big, part, theirs)])
    out = {}
    for n, res in zip(big, results):
        out[n] = tuple(jnp.swapaxes(r[None], 1, 2) if transposed(n) else r[None] for r in res)
    loss_row, small_out = _adamw_small(recv_v, recv_a, recv_x,
                                       {n: (args[n], args["m_" + n], args["v_" + n]) for n in SMALL})
    out.update(small_out)
    loss = loss_row[0, 0]

    order = ["norm1_g", "w_in", "q_norm_g", "k_norm_g", "b_f", "conv_w", "conv_b", "w_a", "b_a", "w_x", "b_x", "lam",
             "attn_out_g", "lru_out_g", "w_out", "norm2_g", "w_gate", "w_up", "w_down"]
    return (loss, grad_x.reshape(bl, seq, D_MODEL), *[out[n][0] for n in order], *[out[n][1] for n in order],
            *[out[n][2] for n in order], *[out[n][3] for n in order])
```

```python
import functools
import math

import jax
import jax.numpy as jnp
from jax import lax
from jax.experimental import pallas as pl
from jax.experimental.pallas import tpu as pltpu

F32 = jnp.float32
MXU_DTYPE = jnp.bfloat16
MESH = pl.DeviceIdType.MESH

D_MODEL = 1024
ATT_WIDTH = 512
LRU_WIDTH = 512
HEADS = 8
HEAD_DIM = 64
PAIR = 2 * HEAD_DIM
N_PAIR = HEADS // 2
LRU_BLOCKS = 8
CONV_WIDTH = 4
LRU_C = 8.0
NORM_EPS = 1e-6
QK_SCALE = 1.0 / math.sqrt(HEAD_DIM)
F_PAD = 128
N_CAT = 3 * ATT_WIDTH + F_PAD + 2 * LRU_WIDTH
NEG = -1e30

ADAM_LR, ADAM_B1, ADAM_B2, ADAM_EPS, ADAM_WD, ADAM_STEP = 0.001, 0.9, 0.999, 1e-08, 0.01, 10

TOKEN_TILE = 256
ATT_TILE = 512
LRU_TILE = 256
VMEM_SMALL = 32 * 1024 * 1024
VMEM_LARGE = 56 * 1024 * 1024


def _params(sem, vmem=VMEM_SMALL):
    return pltpu.CompilerParams(dimension_semantics=sem, vmem_limit_bytes=vmem)


def _const(shape):
    nd = len(shape)
    return pl.BlockSpec(shape, lambda *_: (0,) * nd)


def _sigmoid(x):
    return 1.0 / (1.0 + jnp.exp(-x))


def _nt(a, b):
    return lax.dot_general(a, b, (((1,), (1,)), ((), ())), preferred_element_type=F32)


def _tn(a, b):
    return lax.dot_general(a, b, (((0,), (0,)), ((), ())), preferred_element_type=F32)


def _half_sums(t, lo):
    s_lo = jnp.sum(jnp.where(lo, t, 0.0), axis=-1, keepdims=True)
    s_hi = jnp.sum(jnp.where(lo, 0.0, t), axis=-1, keepdims=True)
    return jnp.where(lo, s_lo, s_hi)


def _lo_mask():
    return lax.broadcasted_iota(jnp.int32, (1, PAIR), 1) < HEAD_DIM


def _other_chips(x, y):
    return [(1 - x, y), (x, 1 - y), (1 - x, 1 - y)]


def _chip_copies(ins, outs, send_sems, recv_sems, loc_sems, scatter):
    x, y, c = lax.axis_index("x"), lax.axis_index("y"), lax.axis_index("c")
    me = 2 * x + y
    copies = []
    for w in range(len(ins)):
        copies.append(pltpu.make_async_copy(ins[w].at[me] if scatter else ins[w], outs[w].at[me], loc_sems.at[w]))
        for k, (cx, cy) in enumerate(_other_chips(x, y)):
            copies.append(pltpu.make_async_remote_copy(
                src_ref=ins[w].at[2 * cx + cy] if scatter else ins[w], dst_ref=outs[w].at[me],
                send_sem=send_sems.at[3 * w + k], recv_sem=recv_sems.at[3 * w + k],
                device_id=(cx, cy, c), device_id_type=MESH))
    return copies


def _chip_sems(n):
    return [pltpu.SemaphoreType.DMA((3 * n,)), pltpu.SemaphoreType.DMA((3 * n,)), pltpu.SemaphoreType.DMA((n,))]


def _gather_split(shard, small):
    half = shard.shape[1] // 2

    def body(w_ref, s_ref, ow_ref, os_ref, ici_send, ici_recv, d2d_send, d2d_recv, sm_send, sm_recv, loc_sems):
        x, y, c = lax.axis_index("x"), lax.axis_index("y"), lax.axis_index("c")
        me = 2 * x + y
        mine = pl.ds(pl.multiple_of(c * half, half), half)
        local = [pltpu.make_async_copy(w_ref, ow_ref.at[me], loc_sems.at[0]),
                 pltpu.make_async_copy(s_ref, os_ref.at[me], loc_sems.at[1])]
        fetch, little, forward = [], [], []
        for k, (cx, cy) in enumerate(_other_chips(x, y)):
            src_chip = 2 * cx + cy
            fetch.append(pltpu.make_async_remote_copy(
                src_ref=w_ref.at[:, mine], dst_ref=ow_ref.at[me, :, mine], send_sem=ici_send.at[k],
                recv_sem=ici_recv.at[k], device_id=(cx, cy, c), device_id_type=MESH))
            little.append(pltpu.make_async_remote_copy(
                src_ref=s_ref, dst_ref=os_ref.at[me], send_sem=sm_send.at[k], recv_sem=sm_recv.at[k],
                device_id=(cx, cy, c), device_id_type=MESH))
            forward.append(pltpu.make_async_remote_copy(
                src_ref=ow_ref.at[src_chip, :, mine], dst_ref=ow_ref.at[src_chip, :, mine], send_sem=d2d_send.at[k],
                recv_sem=d2d_recv.at[k], device_id=(x, y, 1 - c), device_id_type=MESH))
        for cp in local + fetch + little:
            cp.start()
        for k in range(3):
            fetch[k].wait_recv()
            forward[k].start()
        for cp in fetch:
            cp.wait_send()
        for cp in little + forward + local:
            cp.wait()

    return pl.pallas_call(
        body, name="gather_weights",
        out_shape=[jax.ShapeDtypeStruct((4,) + shard.shape, shard.dtype),
                   jax.ShapeDtypeStruct((4,) + small.shape, small.dtype)],
        in_specs=[pl.BlockSpec(memory_space=pl.ANY)] * 2,
        out_specs=[pl.BlockSpec(memory_space=pl.ANY)] * 2,
        scratch_shapes=[pltpu.SemaphoreType.DMA((3,))] * 6 + [pltpu.SemaphoreType.DMA((2,))],
    )(shard, small)


def _device_copies(packs_in, packs_out, psend, precv, loc_sems, loc_base):
    x, y, c = lax.axis_index("x"), lax.axis_index("y"), lax.axis_index("c")
    dev = 4 * x + 2 * y + c
    copies = []
    for j in range(len(packs_in)):
        copies.append(pltpu.make_async_copy(packs_in[j], packs_out[j].at[dev], loc_sems.at[loc_base + j]))
        for k in range(1, 8):
            fx, fy, fc = (k >> 2) & 1, (k >> 1) & 1, k & 1
            tx = (1 - x) if fx else x
            ty = (1 - y) if fy else y
            tc = (1 - c) if fc else c
            copies.append(pltpu.make_async_remote_copy(
                src_ref=packs_in[j], dst_ref=packs_out[j].at[dev],
                send_sem=psend.at[7 * j + k - 1], recv_sem=precv.at[7 * j + k - 1],
                device_id=(tx, ty, tc), device_id_type=MESH))
    return copies


def _swap_with_sibling(arrs, packs):
    n, npk = len(arrs), len(packs)

    def body(*refs):
        ins, pack_in = refs[:n], refs[n:n + npk]
        outs, pack_out = refs[n + npk:2 * n + npk], refs[2 * n + npk:2 * (n + npk)]
        send_sems, recv_sems, loc_sems, psend, precv = refs[2 * (n + npk):]
        x, y, c = lax.axis_index("x"), lax.axis_index("y"), lax.axis_index("c")
        copies = [pltpu.make_async_remote_copy(
            src_ref=ins[w], dst_ref=outs[w], send_sem=send_sems.at[w], recv_sem=recv_sems.at[w],
            device_id=(x, y, 1 - c), device_id_type=MESH) for w in range(n)]
        copies += _device_copies(pack_in, pack_out, psend, precv, loc_sems, 0)
        for cp in copies:
            cp.start()
        for cp in copies:
            cp.wait()

    return pl.pallas_call(
        body, name="swap_sibling",
        out_shape=[jax.ShapeDtypeStruct(a.shape, a.dtype) for a in arrs]
        + [jax.ShapeDtypeStruct((8,) + p.shape, p.dtype) for p in packs],
        in_specs=[pl.BlockSpec(memory_space=pl.ANY)] * (n + npk),
        out_specs=[pl.BlockSpec(memory_space=pl.ANY)] * (n + npk),
        scratch_shapes=[pltpu.SemaphoreType.DMA((n,)), pltpu.SemaphoreType.DMA((n,)), pltpu.SemaphoreType.DMA((npk,)),
                        pltpu.SemaphoreType.DMA((7 * npk,)), pltpu.SemaphoreType.DMA((7 * npk,))],
    )(*arrs, *packs)


def _head_norm(t, g2, lo):
    rr = lax.rsqrt(_half_sums(t * t, lo) * (1.0 / HEAD_DIM) + NORM_EPS)
    return t * rr * g2


def _inproj(x2d, g1, wcat, gq2, gk2):
    T = x2d.shape[0]
    tm = TOKEN_TILE

    def body(x_ref, g1_ref, w_ref, gq_ref, gk_ref, qkv_ref, qn_ref, qt_ref, kn_ref, vb_ref, f_ref, lx_ref, lg_ref, h_ref):
        x = x_ref[...]
        r = lax.rsqrt(jnp.mean(x * x, axis=-1, keepdims=True) + NORM_EPS)
        h = (x * r * g1_ref[...]).astype(MXU_DTYPE)
        h_ref[...] = h
        proj = _nt(h, w_ref[...])
        qkv_ref[...] = proj[:, :3 * ATT_WIDTH]
        lo = _lo_mask()
        for p in range(N_PAIR):
            cols = slice(PAIR * p, PAIR * (p + 1))
            q = proj[:, PAIR * p:PAIR * (p + 1)]
            k = proj[:, ATT_WIDTH + PAIR * p:ATT_WIDTH + PAIR * (p + 1)]
            qs = _head_norm(q, gq_ref[...], lo) * QK_SCALE
            qn_ref[:, cols] = qs.astype(MXU_DTYPE)
            qt_ref[cols, :] = qs.T.astype(MXU_DTYPE)
            kn_ref[:, cols] = _head_norm(k, gk_ref[...], lo).astype(MXU_DTYPE)
        vb_ref[...] = proj[:, 2 * ATT_WIDTH:3 * ATT_WIDTH].astype(MXU_DTYPE)
        f0 = 3 * ATT_WIDTH
        f_ref[...] = proj[:, f0:f0 + F_PAD]
        lx_ref[...] = proj[:, f0 + F_PAD:f0 + F_PAD + LRU_WIDTH]
        lg_ref[...] = proj[:, f0 + F_PAD + LRU_WIDTH:]

    row = lambda w: pl.BlockSpec((tm, w), lambda i: (i, 0))
    return pl.pallas_call(
        body, name="inproj", grid=(T // tm,),
        in_specs=[row(D_MODEL), _const((1, D_MODEL)), _const((N_CAT, D_MODEL)), _const((1, PAIR)), _const((1, PAIR))],
        out_specs=[row(3 * ATT_WIDTH), row(ATT_WIDTH), pl.BlockSpec((ATT_WIDTH, tm), lambda i: (0, i)), row(ATT_WIDTH),
                   row(ATT_WIDTH), row(F_PAD), row(LRU_WIDTH), row(LRU_WIDTH), row(D_MODEL)],
        out_shape=[jax.ShapeDtypeStruct((T, 3 * ATT_WIDTH), F32),
                   jax.ShapeDtypeStruct((T, ATT_WIDTH), MXU_DTYPE), jax.ShapeDtypeStruct((ATT_WIDTH, T), MXU_DTYPE),
                   jax.ShapeDtypeStruct((T, ATT_WIDTH), MXU_DTYPE),
                   jax.ShapeDtypeStruct((T, ATT_WIDTH), MXU_DTYPE), jax.ShapeDtypeStruct((T, F_PAD), F32),
                   jax.ShapeDtypeStruct((T, LRU_WIDTH), F32), jax.ShapeDtypeStruct((T, LRU_WIDTH), F32),
                   jax.ShapeDtypeStruct((T, D_MODEL), MXU_DTYPE)],
        compiler_params=_params(("parallel",), VMEM_LARGE),
    )(x2d, g1, wcat, gq2, gk2)


def _forget_cumsum(f2d, bf, bl, seq):
    def body(z_ref, b_ref, o_ref):
        z = z_ref[...] + b_ref[...]
        lf = jnp.minimum(z, 0.0) - jnp.log(1.0 + jnp.exp(-jnp.abs(z)))
        row = lax.broadcasted_iota(jnp.int32, (seq, F_PAD), 0)
        k = 1
        while k < seq:
            lf = lf + jnp.where(row >= k, pltpu.roll(lf, k, 0), 0.0)
            k *= 2
        o_ref[...] = lf

    return pl.pallas_call(
        body, name="forget_cumsum", grid=(bl,),
        in_specs=[pl.BlockSpec((seq, F_PAD), lambda b: (b, 0)), _const((1, F_PAD))],
        out_specs=pl.BlockSpec((seq, F_PAD), lambda b: (b, 0)),
        out_shape=jax.ShapeDtypeStruct(f2d.shape, F32),
        compiler_params=_params(("parallel",)),
    )(f2d, bf)


def _attn_fwd(qn, kn, vb, frow, fstart, bl, seq, shards):
    tq = min(ATT_TILE, seq)
    nq = seq // tq
    T = bl * seq
    n = len(shards)

    def body(fs_ref, q_ref, k_ref, v_ref, fr_ref, *rest):
        g_in, (o_ref, lse_ref), g_out, sems = rest[:n], rest[n:n + 2], rest[n + 2:2 * n + 2], rest[2 * n + 2:]
        b, p, i = pl.program_id(0), pl.program_id(1), pl.program_id(2)
        copies = _chip_copies(g_in, g_out, *sems, scatter=False)

        @pl.when((b == 0) & (p == 0) & (i == 0))
        def _():
            for cp in copies:
                cp.start()

        lane = lax.broadcasted_iota(jnp.int32, (1, PAIR), 1)
        rows = lax.broadcasted_iota(jnp.int32, (tq, tq), 0)
        cols = lax.broadcasted_iota(jnp.int32, (tq, tq), 1)
        causal = cols <= rows
        q = q_ref[...]
        hms = [(lane >= HEAD_DIM * hh) & (lane < HEAD_DIM * (hh + 1)) for hh in range(2)]
        qhs = [jnp.where(hm, q, jnp.zeros_like(q)) for hm in hms]
        shifts = [fs_ref[((b * N_PAIR + p) * 2 + hh) * nq + i] for hh in range(2)]
        sum_lane = [HEAD_DIM * (1 - hh) for hh in range(2)]

        def block(j, carry, masked):
            start = pl.multiple_of(j * tq, tq)
            k = k_ref[pl.ds(start, tq), :]
            v = v_ref[pl.ds(start, tq), :]
            new = []
            for hh in range(2):
                m, acc = carry[hh]
                s = lax.dot_general(qhs[hh], k, (((1,), (1,)), ((), ())), preferred_element_type=F32)
                s = s - (fr_ref[0, 0, hh:hh + 1, pl.ds(start, tq)] - shifts[hh])
                if masked:
                    s = jnp.where(causal, s, NEG)
                m_new = jnp.maximum(m, jnp.max(s, axis=-1, keepdims=True))
                alpha = jnp.exp(m - m_new)
                pe = jnp.exp(s - m_new)
                vh = jnp.where(hms[hh], v, jnp.where(lane == sum_lane[hh], 1.0, 0.0).astype(v.dtype))
                pb = pe.astype(MXU_DTYPE)
                p_lo = (pe - pb.astype(F32)).astype(MXU_DTYPE)
                acc = (alpha * acc + jnp.dot(pb, vh, preferred_element_type=F32)
                       + jnp.dot(p_lo, vh, preferred_element_type=F32))
                new.append((m_new, acc))
            return tuple(new)

        init = (jnp.full((tq, 1), NEG, F32), jnp.zeros((tq, PAIR), F32))
        carry = lax.fori_loop(0, i, functools.partial(block, masked=False), (init, init))
        carry = block(i, carry, True)
        out = jnp.zeros((tq, PAIR), F32)
        lse = jnp.zeros((tq, PAIR), F32)
        for hh in range(2):
            m, acc = carry[hh]
            l = acc[:, sum_lane[hh]:sum_lane[hh] + 1]
            out = jnp.where(hms[hh], acc * (1.0 / l), out)
            lse = jnp.where(hms[hh], m + jnp.log(l), lse)
        o_ref[...] = out
        lse_ref[...] = lse

        @pl.when((b == bl - 1) & (p == N_PAIR - 1) & (i == nq - 1))
        def _():
            for cp in copies:
                cp.wait()

    blk = pl.BlockSpec((tq, PAIR), lambda b, p, i: (b * nq + i, p))
    full = pl.BlockSpec((seq, PAIR), lambda b, p, i: (b, p))
    return pl.pallas_call(
        body, name="attn_fwd", grid=(bl, N_PAIR, nq),
        in_specs=[pl.BlockSpec(memory_space=pltpu.SMEM), blk, full, full,
                  pl.BlockSpec((1, 1, 2, seq), lambda b, p, i: (b, p, 0, 0))] + [pl.BlockSpec(memory_space=pl.ANY)] * n,
        out_specs=[blk, blk] + [pl.BlockSpec(memory_space=pl.ANY)] * n,
        out_shape=[jax.ShapeDtypeStruct((T, ATT_WIDTH), F32)] * 2
        + [jax.ShapeDtypeStruct((4,) + s.shape, s.dtype) for s in shards],
        scratch_shapes=_chip_sems(n),
        compiler_params=_params(("arbitrary", "arbitrary", "arbitrary")),
    )(fstart, qn, kn, vb, frow, *shards)


def _conv_taps(lx, prev8, cw, cb):
    xs = jnp.concatenate([prev8, lx], axis=0)
    shifted = [lx] + [pltpu.roll(xs, k, 0)[8:] for k in range(1, CONV_WIDTH)]
    xc = cb + cw[CONV_WIDTH - 1:CONV_WIDTH] * lx
    for k in range(1, CONV_WIDTH):
        xc = xc + cw[CONV_WIDTH - 1 - k:CONV_WIDTH - k] * shifted[k]
    return xc, shifted


def _lru_gates(xc, wa, ba, wx, bx, lam):
    xb = xc.astype(MXU_DTYPE)
    r = _sigmoid(jnp.dot(xb, wa, preferred_element_type=F32) + ba)
    ig = _sigmoid(jnp.dot(xb, wx, preferred_element_type=F32) + bx)
    sp = jnp.maximum(-lam, 0.0) + jnp.log(1.0 + jnp.exp(-jnp.abs(lam)))
    log_a = -LRU_C * r * sp
    a = jnp.exp(log_a)
    th = jnp.tanh(log_a)
    mult = jnp.sqrt(-2.0 * th / (1.0 - th))
    return r, ig, sp, a, mult


def _gelu_parts(x):
    c0 = math.sqrt(2.0 / math.pi)
    t = jnp.tanh(c0 * (x + 0.044715 * x * x * x))
    g = 0.5 * x * (1.0 + t)
    dg = 0.5 * (1.0 + t) + 0.5 * x * (1.0 - t * t) * c0 * (1.0 + 3.0 * 0.044715 * x * x)
    return g, dg


def _lru_fwd(lx, lg, cw, cb, wa, ba, wx, bx, lam, bl, seq):
    tc = min(LRU_TILE, seq)
    nc = seq // tc
    T = bl * seq

    def body(lx_ref, lxp_ref, lg_ref, cw_ref, cb_ref, wa_ref, ba_ref, wx_ref, bx_ref, lam_ref,
             h_ref, rec_ref, hc_ref):
        i = pl.program_id(1)

        @pl.when(i == 0)
        def _():
            hc_ref[...] = jnp.zeros_like(hc_ref)

        lxv = lx_ref[...]
        prev8 = jnp.where(i > 0, lxp_ref[...], 0.0)
        xc, _ = _conv_taps(lxv, prev8, cw_ref[...], cb_ref[...])
        _, ig, _, a, mult = _lru_gates(xc, wa_ref[...], ba_ref[...], wx_ref[...], bx_ref[...], lam_ref[...])
        u = mult * (ig * xc)
        sub = lax.broadcasted_iota(jnp.int32, (tc, LRU_WIDTH), 0) & 7
        A, B = a, u
        for k in (1, 2, 4):
            a_s = jnp.where(sub >= k, pltpu.roll(A, k, 0), 1.0)
            b_s = jnp.where(sub >= k, pltpu.roll(B, k, 0), 0.0)
            B = A * b_s + B
            A = A * a_s
        carry = hc_ref[0:1, :]
        groups = []
        for g in range(tc // 8):
            hg = A[8 * g:8 * (g + 1)] * carry + B[8 * g:8 * (g + 1)]
            groups.append(hg)
            carry = hg[7:8]
        h = jnp.concatenate(groups, axis=0)
        hc_ref[0:1, :] = carry
        h_ref[...] = h
        g, _ = _gelu_parts(lg_ref[...])
        rec_ref[...] = h * g

    tile = pl.BlockSpec((tc, LRU_WIDTH), lambda b, i: (b * nc + i, 0))
    prev = pl.BlockSpec((8, LRU_WIDTH), lambda b, i: (jnp.maximum((b * seq + i * tc) // 8 - 1, 0), 0))
    vec = _const((1, LRU_WIDTH))
    mat = _const((LRU_WIDTH, LRU_WIDTH))
    return pl.pallas_call(
        body, name="lru_fwd", grid=(bl, nc),
        in_specs=[tile, prev, tile, _const((CONV_WIDTH, LRU_WIDTH)), vec, mat, vec, mat, vec, vec],
        out_specs=[tile, tile],
        out_shape=[jax.ShapeDtypeStruct((T, LRU_WIDTH), F32), jax.ShapeDtypeStruct((T, LRU_WIDTH), F32)],
        scratch_shapes=[pltpu.VMEM((8, LRU_WIDTH), F32)],
        compiler_params=_params(("arbitrary", "arbitrary")),
    )(lx, lx, lg, cw, cb, wa, ba, wx, bx, lam)


def _outproj(x2d, att, rec, ga, gr, wout):
    T = x2d.shape[0]
    tm = TOKEN_TILE

    def body(x_ref, a_ref, r_ref, ga_ref, gr_ref, w_ref, o_ref):
        a = a_ref[...]
        rc = r_ref[...]
        na = a * lax.rsqrt(jnp.mean(a * a, axis=-1, keepdims=True) + NORM_EPS) * ga_ref[...]
        nr = rc * lax.rsqrt(jnp.mean(rc * rc, axis=-1, keepdims=True) + NORM_EPS) * gr_ref[...]
        o_ref[...] = (x_ref[...]
                      + jnp.dot(na.astype(MXU_DTYPE), w_ref[:ATT_WIDTH, :], preferred_element_type=F32)
                      + jnp.dot(nr.astype(MXU_DTYPE), w_ref[ATT_WIDTH:, :], preferred_element_type=F32))

    row = lambda w: pl.BlockSpec((tm, w), lambda i: (i, 0))
    return pl.pallas_call(
        body, name="outproj", grid=(T // tm,),
        in_specs=[row(D_MODEL), row(ATT_WIDTH), row(LRU_WIDTH), _const((1, ATT_WIDTH)), _const((1, LRU_WIDTH)),
                  _const((D_MODEL, D_MODEL))],
        out_specs=row(D_MODEL),
        out_shape=jax.ShapeDtypeStruct((T, D_MODEL), F32),
        compiler_params=_params(("parallel",)),
    )(x2d, att, rec, ga, gr, wout)


def _mlp_fwd(x2, g2, wg, wu, wd, target):
    T = x2.shape[0]
    tm = TOKEN_TILE
    dff = wg.shape[0]

    def body(x_ref, g_ref, wg_ref, wu_ref, wd_ref, t_ref, gt_ref, up_ref, dy_ref, loss_ref):
        @pl.when(pl.program_id(0) == 0)
        def _():
            loss_ref[...] = jnp.zeros_like(loss_ref)

        x = x_ref[...]
        r = lax.rsqrt(jnp.mean(x * x, axis=-1, keepdims=True) + NORM_EPS)
        h = (x * r * g_ref[...]).astype(MXU_DTYPE)
        gt = _nt(h, wg_ref[...])
        up = _nt(h, wu_ref[...])
        gt_ref[...] = gt
        up_ref[...] = up
        act = (gt * _sigmoid(gt) * up).astype(MXU_DTYPE)
        y = x + jnp.dot(act, wd_ref[...], preferred_element_type=F32)
        e = y - t_ref[...]
        dy_ref[...] = e * (1.0 / D_MODEL)
        loss_ref[...] += jnp.sum(e * e)

    row = lambda w: pl.BlockSpec((tm, w), lambda i: (i, 0))
    return pl.pallas_call(
        body, name="mlp_fwd", grid=(T // tm,),
        in_specs=[row(D_MODEL), _const((1, D_MODEL)), _const((dff, D_MODEL)), _const((dff, D_MODEL)),
                  _const((dff, D_MODEL)), row(D_MODEL)],
        out_specs=[row(dff), row(dff), row(D_MODEL), _const((8, 128))],
        out_shape=[jax.ShapeDtypeStruct((T, dff), F32), jax.ShapeDtypeStruct((T, dff), F32),
                   jax.ShapeDtypeStruct((T, D_MODEL), F32), jax.ShapeDtypeStruct((8, 128), F32)],
        compiler_params=_params(("arbitrary",), VMEM_LARGE),
    )(x2, g2, wg, wu, wd, target)


def _mlp_bwd(dy, x2, gt, up, g2, wg, wu, wd):
    T = x2.shape[0]
    tm = TOKEN_TILE
    dff = wg.shape[0]

    def body(dy_ref, x_ref, gt_ref, up_ref, g_ref, wg_ref, wu_ref, wd_ref,
             dx_ref, dxb_ref, dgt_ref, dup_ref, act_ref, h_ref, dyb_ref, dg_ref):
        @pl.when(pl.program_id(0) == 0)
        def _():
            dg_ref[...] = jnp.zeros_like(dg_ref)

        dy_v = dy_ref[...]
        dyb = dy_v.astype(MXU_DTYPE)
        dyb_ref[...] = dyb
        x = x_ref[...]
        r = lax.rsqrt(jnp.mean(x * x, axis=-1, keepdims=True) + NORM_EPS)
        xh = x * r
        h_ref[...] = (xh * g_ref[...]).astype(MXU_DTYPE)
        gt_v = gt_ref[...]
        up_v = up_ref[...]
        sg = _sigmoid(gt_v)
        silu = gt_v * sg
        act_ref[...] = (silu * up_v).astype(MXU_DTYPE)
        dact = _nt(dyb, wd_ref[...])
        dup = (dact * silu).astype(MXU_DTYPE)
        dgt = (dact * up_v * (sg * (1.0 + gt_v * (1.0 - sg)))).astype(MXU_DTYPE)
        dup_ref[...] = dup
        dgt_ref[...] = dgt
        dh = (jnp.dot(dgt, wg_ref[...], preferred_element_type=F32)
              + jnp.dot(dup, wu_ref[...], preferred_element_type=F32))
        dg_ref[...] += jnp.sum(dh * xh, axis=0, keepdims=True)
        dxh = dh * g_ref[...]
        dx = dy_v + r * (dxh - xh * jnp.mean(dxh * xh, axis=-1, keepdims=True))
        dx_ref[...] = dx
        dxb_ref[...] = dx.astype(MXU_DTYPE)

    row = lambda w: pl.BlockSpec((tm, w), lambda i: (i, 0))
    return pl.pallas_call(
        body, name="mlp_bwd", grid=(T // tm,),
        in_specs=[row(D_MODEL), row(D_MODEL), row(dff), row(dff), _const((1, D_MODEL)),
                  _const((dff, D_MODEL)), _const((dff, D_MODEL)), _const((dff, D_MODEL))],
        out_specs=[row(D_MODEL), row(D_MODEL), row(dff), row(dff), row(dff), row(D_MODEL), row(D_MODEL),
                   _const((1, D_MODEL))],
        out_shape=[jax.ShapeDtypeStruct((T, D_MODEL), F32), jax.ShapeDtypeStruct((T, D_MODEL), MXU_DTYPE),
                   jax.ShapeDtypeStruct((T, dff), MXU_DTYPE), jax.ShapeDtypeStruct((T, dff), MXU_DTYPE),
                   jax.ShapeDtypeStruct((T, dff), MXU_DTYPE), jax.ShapeDtypeStruct((T, D_MODEL), MXU_DTYPE),
                   jax.ShapeDtypeStruct((T, D_MODEL), MXU_DTYPE), jax.ShapeDtypeStruct((1, D_MODEL), F32)],
        compiler_params=_params(("arbitrary",), VMEM_LARGE),
    )(dy, x2, gt, up, g2, wg, wu, wd)


def _matmul_tn(a, b, tn, name):
    T, K = a.shape
    N = b.shape[1]
    tt = min(1024, T)

    def body(a_ref, b_ref, o_ref):
        @pl.when(pl.program_id(1) == 0)
        def _():
            o_ref[...] = jnp.zeros_like(o_ref)

        o_ref[...] += _tn(a_ref[...], b_ref[...])

    return pl.pallas_call(
        body, name=name, grid=(N // tn, T // tt),
        in_specs=[pl.BlockSpec((tt, K), lambda n, t: (t, 0)), pl.BlockSpec((tt, tn), lambda n, t: (t, n))],
        out_specs=pl.BlockSpec((K, tn), lambda n, t: (0, n)),
        out_shape=jax.ShapeDtypeStruct((K, N), F32),
        compiler_params=_params(("parallel", "arbitrary"), VMEM_LARGE),
    )(a, b)


def _outproj_bwd(dx2b, att, rec, ga, gr, wout):
    T = att.shape[0]
    tm = TOKEN_TILE

    def body(dx_ref, a_ref, r_ref, ga_ref, gr_ref, w_ref, datt_ref, dattt_ref, delta_ref, drec_ref, mix_ref, dga_ref,
             dgr_ref):
        @pl.when(pl.program_id(0) == 0)
        def _():
            dga_ref[...] = jnp.zeros_like(dga_ref)
            dgr_ref[...] = jnp.zeros_like(dgr_ref)

        dmix = _nt(dx_ref[...], w_ref[...])

        def norm_bwd(v, g, dn):
            rr = lax.rsqrt(jnp.mean(v * v, axis=-1, keepdims=True) + NORM_EPS)
            vh = v * rr
            dvh = dn * g
            dv = rr * (dvh - vh * jnp.mean(dvh * vh, axis=-1, keepdims=True))
            return vh, dv, jnp.sum(dn * vh, axis=0, keepdims=True)

        a = a_ref[...]
        ah, datt, dga = norm_bwd(a, ga_ref[...], dmix[:, :ATT_WIDTH])
        rh, drec, dgr = norm_bwd(r_ref[...], gr_ref[...], dmix[:, ATT_WIDTH:])
        dga_ref[...] += dga
        dgr_ref[...] += dgr
        mix_ref[:, :ATT_WIDTH] = (ah * ga_ref[...]).astype(MXU_DTYPE)
        mix_ref[:, ATT_WIDTH:] = (rh * gr_ref[...]).astype(MXU_DTYPE)
        dattb = datt.astype(MXU_DTYPE)
        datt_ref[...] = dattb
        dattt_ref[...] = datt.T.astype(MXU_DTYPE)
        drec_ref[...] = drec
        lo = _lo_mask()
        prod = dattb.astype(F32) * a
        for p in range(N_PAIR):
            delta_ref[:, PAIR * p:PAIR * (p + 1)] = _half_sums(prod[:, PAIR * p:PAIR * (p + 1)], lo)

    row = lambda w: pl.BlockSpec((tm, w), lambda i: (i, 0))
    return pl.pallas_call(
        body, name="outproj_bwd", grid=(T // tm,),
        in_specs=[row(D_MODEL), row(ATT_WIDTH), row(LRU_WIDTH), _const((1, ATT_WIDTH)), _const((1, LRU_WIDTH)),
                  _const((D_MODEL, D_MODEL))],
        out_specs=[row(ATT_WIDTH), pl.BlockSpec((ATT_WIDTH, tm), lambda i: (0, i)), row(ATT_WIDTH), row(LRU_WIDTH),
                   row(D_MODEL), _const((1, ATT_WIDTH)), _const((1, LRU_WIDTH))],
        out_shape=[jax.ShapeDtypeStruct((T, ATT_WIDTH), MXU_DTYPE), jax.ShapeDtypeStruct((ATT_WIDTH, T), MXU_DTYPE),
                   jax.ShapeDtypeStruct((T, ATT_WIDTH), F32),
                   jax.ShapeDtypeStruct((T, LRU_WIDTH), F32), jax.ShapeDtypeStruct((T, D_MODEL), MXU_DTYPE),
                   jax.ShapeDtypeStruct((1, ATT_WIDTH), F32), jax.ShapeDtypeStruct((1, LRU_WIDTH), F32)],
        compiler_params=_params(("arbitrary",)),
    )(dx2b, att, rec, ga, gr, wout)


def _lru_bwd(drec, lg, h, lx, cw, cb, wa, ba, wx, bx, lam, bl, seq):
    tc = min(LRU_TILE, seq)
    nc = seq // tc
    T = bl * seq
    n = tc

    def body(dr_ref, lg_ref, h_ref, hp_ref, lx_ref, lxp_ref, cw_ref, cb_ref, wa_ref, ba_ref, wx_ref, bx_ref, lam_ref,
             dlx_ref, dlg_ref, dwa_ref, dwx_ref, small_ref, gc_ref, dxn_ref):
        b, i = pl.program_id(0), pl.program_id(1)
        ir = nc - 1 - i

        @pl.when((b == 0) & (i == 0))
        def _():
            dwa_ref[...] = jnp.zeros_like(dwa_ref)
            dwx_ref[...] = jnp.zeros_like(dwx_ref)
            small_ref[...] = jnp.zeros_like(small_ref)

        @pl.when(i == 0)
        def _():
            gc_ref[...] = jnp.zeros_like(gc_ref)
            dxn_ref[...] = jnp.zeros_like(dxn_ref)

        cw = cw_ref[...]
        lam_v = lam_ref[...]
        lxv = lx_ref[...]
        prev8 = jnp.where(ir > 0, lxp_ref[...], 0.0)
        xc, shifted = _conv_taps(lxv, prev8, cw, cb_ref[...])
        r, ig, sp, a, mult = _lru_gates(xc, wa_ref[...], ba_ref[...], wx_ref[...], bx_ref[...], lam_v)
        hv = h_ref[...]
        drv = dr_ref[...]
        g, dg = _gelu_parts(lg_ref[...])
        dlg_ref[...] = drv * hv * dg
        dh = drv * g

        row = lax.broadcasted_iota(jnp.int32, (n, LRU_WIDTH), 0)
        sub = row & 7
        A = jnp.where(row < n - 1, pltpu.roll(a, n - 1, 0), 0.0)
        B = dh + jnp.where(row == n - 1, gc_ref[0:1, :], 0.0)
        for k in (1, 2, 4):
            a_s = jnp.where(sub < 8 - k, pltpu.roll(A, n - k, 0), 1.0)
            b_s = jnp.where(sub < 8 - k, pltpu.roll(B, n - k, 0), 0.0)
            B = B + A * b_s
            A = A * a_s
        carry = jnp.zeros((1, LRU_WIDTH), F32)
        groups = [None] * (n // 8)
        for g in reversed(range(n // 8)):
            gg = B[8 * g:8 * (g + 1)] + A[8 * g:8 * (g + 1)] * carry
            groups[g] = gg
            carry = gg[0:1]
        gs = jnp.concatenate(groups, axis=0)
        gc_ref[0:1, :] = a[0:1, :] * carry

        hprev8 = jnp.where(ir > 0, hp_ref[...], 0.0)
        h_prev = pltpu.roll(jnp.concatenate([hprev8, hv], axis=0), 1, 0)[8:]
        da = gs * h_prev
        ix = ig * xc
        dmult = gs * ix
        dig = gs * mult * xc
        dxc = gs * mult * ig
        dlog_a = da * a - dmult * (a * a) / mult
        dr_gate = dlog_a * (-LRU_C * sp)
        dza = dr_gate * r * (1.0 - r)
        dzx = dig * ig * (1.0 - ig)
        dzab = dza.astype(MXU_DTYPE)
        dzxb = dzx.astype(MXU_DTYPE)
        xcb = xc.astype(MXU_DTYPE)
        dwa_ref[...] += _tn(xcb, dzab)
        dwx_ref[...] += _tn(xcb, dzxb)
        dxc = dxc + _nt(dzab, wa_ref[...]) + _nt(dzxb, wx_ref[...])

        ds = jnp.concatenate([dxc, dxn_ref[...]], axis=0)
        dlx = cw[CONV_WIDTH - 1:CONV_WIDTH] * dxc
        for k in range(1, CONV_WIDTH):
            dlx = dlx + cw[CONV_WIDTH - 1 - k:CONV_WIDTH - k] * pltpu.roll(ds, n + 8 - k, 0)[:n]
        dlx_ref[...] = dlx
        dxn_ref[...] = dxc[0:8, :]

        colsum = lambda v: jnp.sum(v, axis=0, keepdims=True)
        small_ref[0:1, :] += colsum(dza)
        small_ref[1:2, :] += colsum(dzx)
        small_ref[2:3, :] += colsum(dlog_a * r) * (LRU_C * _sigmoid(-lam_v))
        small_ref[3:4, :] += colsum(dxc)
        for k in range(CONV_WIDTH):
            j = CONV_WIDTH - 1 - k
            small_ref[4 + j:5 + j, :] += colsum(dxc * shifted[k])

    tile = pl.BlockSpec((tc, LRU_WIDTH), lambda b, i: (b * nc + (nc - 1 - i), 0))
    prev = pl.BlockSpec((8, LRU_WIDTH), lambda b, i: (jnp.maximum((b * seq + (nc - 1 - i) * tc) // 8 - 1, 0), 0))
    vec = _const((1, LRU_WIDTH))
    mat = _const((LRU_WIDTH, LRU_WIDTH))
    return pl.pallas_call(
        body, name="lru_bwd", grid=(bl, nc),
        in_specs=[tile, tile, tile, prev, tile, prev, _const((CONV_WIDTH, LRU_WIDTH)), vec, mat, vec, mat, vec, vec],
        out_specs=[tile, tile, mat, mat, _const((8, LRU_WIDTH))],
        out_shape=[jax.ShapeDtypeStruct((T, LRU_WIDTH), F32), jax.ShapeDtypeStruct((T, LRU_WIDTH), F32),
                   jax.ShapeDtypeStruct((LRU_WIDTH, LRU_WIDTH), F32), jax.ShapeDtypeStruct((LRU_WIDTH, LRU_WIDTH), F32),
                   jax.ShapeDtypeStruct((8, LRU_WIDTH), F32)],
        scratch_shapes=[pltpu.VMEM((8, LRU_WIDTH), F32), pltpu.VMEM((8, LRU_WIDTH), F32)],
        compiler_params=_params(("arbitrary", "arbitrary")),
    )(drec, lg, h, h, lx, lx, cw, cb, wa, ba, wx, bx, lam)


def _attn_bwd(qn, kn, vb, dob, qt, dot_, lse, delta, frow, fstart, bl, seq, slabs, packs):
    tq = min(ATT_TILE, seq)
    nq = seq // tq
    T = bl * seq
    n, npk = len(slabs), len(packs)
    nx = n + npk

    def body(fs_ref, q_ref, k_ref, v_ref, do_ref, qt_ref, dot_ref, lse_ref, dl_ref, fr_ref, *rest):
        x_in, (dq_ref, dk_ref, dv_ref, df_ref), x_out = rest[:nx], rest[nx:nx + 4], rest[nx + 4:2 * nx + 4]
        send_sems, recv_sems, loc_sems, psend, precv = rest[2 * nx + 4:]
        b, p, j = pl.program_id(0), pl.program_id(1), pl.program_id(2)
        copies = _chip_copies(x_in[:n], x_out[:n], send_sems, recv_sems, loc_sems, scatter=True)
        copies += _device_copies(x_in[n:], x_out[n:], psend, precv, loc_sems, n)

        @pl.when((b == 0) & (p == 0) & (j == 0))
        def _():
            for cp in copies:
                cp.start()

        @pl.when(j == 0)
        def _():
            dq_ref[...] = jnp.zeros_like(dq_ref)

        lane = lax.broadcasted_iota(jnp.int32, (1, PAIR), 1)
        rows = lax.broadcasted_iota(jnp.int32, (tq, tq), 0)
        cols = lax.broadcasted_iota(jnp.int32, (tq, tq), 1)
        causal = cols <= rows
        kv = k_ref[...]
        vv = v_ref[...]
        hms = [(lane >= HEAD_DIM * hh) & (lane < HEAD_DIM * (hh + 1)) for hh in range(2)]
        srow = lax.broadcasted_iota(jnp.int32, (PAIR, 1), 0)
        hms_t = [(srow >= HEAD_DIM * hh) & (srow < HEAD_DIM * (hh + 1)) for hh in range(2)]
        khs = [jnp.where(hm, kv, jnp.zeros_like(kv)) for hm in hms]
        fks = [fr_ref[0, 0, hh:hh + 1, :] for hh in range(2)]
        bases = [((b * N_PAIR + p) * 2 + hh) * nq for hh in range(2)]

        def block(i, carry, masked):
            dk, dv, dfs = carry
            start = pl.multiple_of(i * tq, tq)
            qi = q_ref[pl.ds(start, tq), :]
            doi = do_ref[pl.ds(start, tq), :]
            qti = qt_ref[:, pl.ds(start, tq)]
            doti = dot_ref[:, pl.ds(start, tq)]
            dq = jnp.zeros((tq, PAIR), F32)
            new_dfs = []
            for hh in range(2):
                c0 = HEAD_DIM * hh
                qh = jnp.where(hms[hh], qi, jnp.zeros_like(qi))
                doh = jnp.where(hms[hh], doi, jnp.zeros_like(doi))
                s = _nt(qh, kv) - (fks[hh] - fs_ref[bases[hh] + i])
                if masked:
                    s = jnp.where(causal, s, NEG)
                pr = jnp.exp(s - lse_ref[pl.ds(start, tq), c0:c0 + 1])
                dp = _nt(doh, vv)
                ds = pr * (dp - dl_ref[pl.ds(start, tq), c0:c0 + 1])
                dsb = ds.astype(MXU_DTYPE)
                dv = dv + jnp.dot(jnp.where(hms_t[hh], doti, jnp.zeros_like(doti)), pr.astype(MXU_DTYPE),
                                  preferred_element_type=F32)
                dk = dk + jnp.dot(jnp.where(hms_t[hh], qti, jnp.zeros_like(qti)), dsb, preferred_element_type=F32)
                dq = dq + jnp.dot(dsb, khs[hh], preferred_element_type=F32)
                new_dfs.append(dfs[hh] - jnp.sum(ds, axis=0, keepdims=True))
            dq_ref[pl.ds(start, tq), :] += dq
            return dk, dv, tuple(new_dfs)

        zero = jnp.zeros((PAIR, tq), F32)
        carry = block(j, (zero, zero, (jnp.zeros((1, tq), F32), jnp.zeros((1, tq), F32))), True)
        dk, dv, dfs = lax.fori_loop(j + 1, nq, functools.partial(block, masked=False), carry)
        for hh in range(2):
            df_ref[0, 0, hh:hh + 1, :] = dfs[hh]
        dk_ref[...] = dk.T
        dv_ref[...] = dv.T

        @pl.when((b == bl - 1) & (p == N_PAIR - 1) & (j == nq - 1))
        def _():
            for cp in copies:
                cp.wait()

    blk = pl.BlockSpec((tq, PAIR), lambda b, p, j: (b * nq + j, p))
    full = pl.BlockSpec((seq, PAIR), lambda b, p, j: (b, p))
    fblk = pl.BlockSpec((1, 1, 2, tq), lambda b, p, j: (b, p, 0, j))
    full_t = pl.BlockSpec((PAIR, seq), lambda b, p, j: (p, b))
    hbm = pl.BlockSpec(memory_space=pl.ANY)
    return pl.pallas_call(
        body, name="attn_bwd", grid=(bl, N_PAIR, nq),
        in_specs=[pl.BlockSpec(memory_space=pltpu.SMEM), full, blk, blk, full, full_t, full_t, full, full, fblk]
        + [hbm] * nx,
        out_specs=[full, blk, blk, fblk] + [hbm] * nx,
        out_shape=[jax.ShapeDtypeStruct((T, ATT_WIDTH), F32), jax.ShapeDtypeStruct((T, ATT_WIDTH), F32),
                   jax.ShapeDtypeStruct((T, ATT_WIDTH), F32), jax.ShapeDtypeStruct((bl, N_PAIR, 2, seq), F32)]
        + [jax.ShapeDtypeStruct(s.shape, s.dtype) for s in slabs]
        + [jax.ShapeDtypeStruct((8,) + p.shape, p.dtype) for p in packs],
        scratch_shapes=[pltpu.SemaphoreType.DMA((3 * n,)), pltpu.SemaphoreType.DMA((3 * n,)),
                        pltpu.SemaphoreType.DMA((nx,)),
                        pltpu.SemaphoreType.DMA((7 * npk,)), pltpu.SemaphoreType.DMA((7 * npk,))],
        compiler_params=_params(("arbitrary", "arbitrary", "arbitrary"), VMEM_LARGE),
    )(fstart, qn, kn, vb, dob, qt, dot_, lse, delta, frow, *slabs, *packs)


def _forget_bwd(dfcol, f2d, bf, bl, seq):
    def body(d_ref, z_ref, b_ref, o_ref, db_ref):
        @pl.when(pl.program_id(0) == 0)
        def _():
            db_ref[...] = jnp.zeros_like(db_ref)

        d = d_ref[...]
        row = lax.broadcasted_iota(jnp.int32, (seq, F_PAD), 0)
        k = 1
        while k < seq:
            d = d + jnp.where(row < seq - k, pltpu.roll(d, seq - k, 0), 0.0)
            k *= 2
        dz = d * _sigmoid(-(z_ref[...] + b_ref[...]))
        o_ref[...] = dz
        db_ref[...] += jnp.sum(dz, axis=0, keepdims=True)

    blk = pl.BlockSpec((seq, F_PAD), lambda b: (b, 0))
    return pl.pallas_call(
        body, name="forget_bwd", grid=(bl,),
        in_specs=[blk, blk, _const((1, F_PAD))],
        out_specs=[blk, _const((1, F_PAD))],
        out_shape=[jax.ShapeDtypeStruct(f2d.shape, F32), jax.ShapeDtypeStruct((1, F_PAD), F32)],
        compiler_params=_params(("arbitrary",)),
    )(dfcol, f2d, bf)


def _dproj(dq, dk, dv, qkv, df, dlx, dlg, gq2, gk2):
    T = dq.shape[0]
    tm = TOKEN_TILE

    def body(dq_ref, dk_ref, dv_ref, qkv_ref, df_ref, dlx_ref, dlg_ref, gq_ref, gk_ref, dp_ref, dgq_ref, dgk_ref):
        @pl.when(pl.program_id(0) == 0)
        def _():
            dgq_ref[...] = jnp.zeros_like(dgq_ref)
            dgk_ref[...] = jnp.zeros_like(dgk_ref)

        lo = _lo_mask()

        def head_norm_bwd(t, g2, dy):
            rr = lax.rsqrt(_half_sums(t * t, lo) * (1.0 / HEAD_DIM) + NORM_EPS)
            th = t * rr
            dth = dy * g2
            mm = _half_sums(dth * th, lo) * (1.0 / HEAD_DIM)
            return rr * (dth - th * mm), jnp.sum(dy * th, axis=0, keepdims=True)

        dgq = jnp.zeros((1, PAIR), F32)
        dgk = jnp.zeros((1, PAIR), F32)
        for p in range(N_PAIR):
            cq = slice(PAIR * p, PAIR * (p + 1))
            ck = slice(ATT_WIDTH + PAIR * p, ATT_WIDTH + PAIR * (p + 1))
            dqp, g_ = head_norm_bwd(qkv_ref[:, cq], gq_ref[...], dq_ref[:, cq] * QK_SCALE)
            dgq = dgq + g_
            dp_ref[:, cq] = dqp.astype(MXU_DTYPE)
            dkp, g_ = head_norm_bwd(qkv_ref[:, ck], gk_ref[...], dk_ref[:, cq])
            dgk = dgk + g_
            dp_ref[:, ck] = dkp.astype(MXU_DTYPE)
        dgq_ref[...] += dgq
        dgk_ref[...] += dgk
        f0 = 3 * ATT_WIDTH
        dp_ref[:, 2 * ATT_WIDTH:f0] = dv_ref[...].astype(MXU_DTYPE)
        dp_ref[:, f0:f0 + F_PAD] = df_ref[...].astype(MXU_DTYPE)
        dp_ref[:, f0 + F_PAD:f0 + F_PAD + LRU_WIDTH] = dlx_ref[...].astype(MXU_DTYPE)
        dp_ref[:, f0 + F_PAD + LRU_WIDTH:] = dlg_ref[...].astype(MXU_DTYPE)

    row = lambda w: pl.BlockSpec((tm, w), lambda i: (i, 0))
    return pl.pallas_call(
        body, name="dproj", grid=(T // tm,),
        in_specs=[row(ATT_WIDTH), row(ATT_WIDTH), row(ATT_WIDTH), row(3 * ATT_WIDTH), row(F_PAD), row(LRU_WIDTH),
                  row(LRU_WIDTH), _const((1, PAIR)), _const((1, PAIR))],
        out_specs=[row(N_CAT), _const((1, PAIR)), _const((1, PAIR))],
        out_shape=[jax.ShapeDtypeStruct((T, N_CAT), MXU_DTYPE), jax.ShapeDtypeStruct((1, PAIR), F32),
                   jax.ShapeDtypeStruct((1, PAIR), F32)],
        compiler_params=_params(("arbitrary",)),
    )(dq, dk, dv, qkv, df, dlx, dlg, gq2, gk2)


def _inproj_bwd(dproj, x2d, dx2, g1, wcat, slabs):
    T = x2d.shape[0]
    tm = TOKEN_TILE
    n = len(slabs)
    steps = T // tm

    def body(dp_ref, x_ref, dx2_ref, g1_ref, w_ref, *rest):
        s_in, (gx_ref, dg1_ref), s_out, sems = rest[:n], rest[n:n + 2], rest[n + 2:2 * n + 2], rest[2 * n + 2:]
        i = pl.program_id(0)
        copies = _chip_copies(s_in, s_out, *sems, scatter=True)

        @pl.when(i == 0)
        def _():
            dg1_ref[...] = jnp.zeros_like(dg1_ref)
            for cp in copies:
                cp.start()

        dh = jnp.dot(dp_ref[...], w_ref[...], preferred_element_type=F32)
        x = x_ref[...]
        r = lax.rsqrt(jnp.mean(x * x, axis=-1, keepdims=True) + NORM_EPS)
        xh = x * r
        dg1_ref[...] += jnp.sum(dh * xh, axis=0, keepdims=True)
        dxh = dh * g1_ref[...]
        gx_ref[...] = dx2_ref[...] + r * (dxh - xh * jnp.mean(dxh * xh, axis=-1, keepdims=True))

        @pl.when(i == steps - 1)
        def _():
            for cp in copies:
                cp.wait()

    row = lambda w: pl.BlockSpec((tm, w), lambda i: (i, 0))
    hbm = pl.BlockSpec(memory_space=pl.ANY)
    return pl.pallas_call(
        body, name="inproj_bwd", grid=(steps,),
        in_specs=[row(N_CAT), row(D_MODEL), row(D_MODEL), _const((1, D_MODEL)), _const((N_CAT, D_MODEL))] + [hbm] * n,
        out_specs=[row(D_MODEL), _const((1, D_MODEL))] + [hbm] * n,
        out_shape=[jax.ShapeDtypeStruct((T, D_MODEL), F32), jax.ShapeDtypeStruct((1, D_MODEL), F32)]
        + [jax.ShapeDtypeStruct(a.shape, a.dtype) for a in slabs],
        scratch_shapes=_chip_sems(n),
        compiler_params=_params(("arbitrary",), VMEM_LARGE),
    )(dproj, x2d, dx2, g1, wcat, *slabs)


ELEMENTWISE_COLS = 256


def _sum_slabs(recv, name):
    _, rows, cols = recv.shape
    cb = ELEMENTWISE_COLS

    def body(r_ref, o_ref):
        part = [r_ref[s].astype(F32) for s in range(4)]
        o_ref[...] = ((part[0] + part[1]) + part[2]) + part[3]

    return pl.pallas_call(
        body, name=name, grid=(cols // cb,),
        in_specs=[pl.BlockSpec((4, rows, cb), lambda i: (0, 0, i))],
        out_specs=pl.BlockSpec((rows, cb), lambda i: (0, i)),
        out_shape=jax.ShapeDtypeStruct((rows, cols), F32),
        compiler_params=_params(("parallel",)),
    )(recv)


def _adamw_math(w, g, m, v):
    m = ADAM_B1 * m + (1.0 - ADAM_B1) * g
    v = ADAM_B2 * v + (1.0 - ADAM_B2) * (g * g)
    m_hat = m / (1.0 - ADAM_B1 ** ADAM_STEP)
    v_hat = v / (1.0 - ADAM_B2 ** ADAM_STEP)
    delta = -ADAM_LR * (m_hat / (jnp.sqrt(v_hat) + ADAM_EPS) + ADAM_WD * w)
    return delta, m, v


def _adamw_pair(mine, theirs, w, m, v, name):
    rows, cols = w.shape
    cb = ELEMENTWISE_COLS

    def body(a_ref, b_ref, w_ref, m_ref, v_ref, g_ref, d_ref, nm_ref, nv_ref):
        g = a_ref[...] + b_ref[...]
        g_ref[...] = g
        d_ref[...], nm_ref[...], nv_ref[...] = _adamw_math(w_ref[...], g, m_ref[...], v_ref[...])

    blk = pl.BlockSpec((rows, cb), lambda i: (0, i))
    return pl.pallas_call(
        body, name=name, grid=(cols // cb,),
        in_specs=[blk] * 5, out_specs=[blk] * 4,
        out_shape=[jax.ShapeDtypeStruct((rows, cols), F32)] * 4,
        compiler_params=_params(("parallel",)),
    )(mine, theirs, w, m, v)


VEC_ROW = {"norm1_g": 0, "norm2_g": 1, "attn_out_g": 2, "lru_out_g": 3, "q_norm_g": 4, "k_norm_g": 5, "b_f": 6,
           "b_a": 8, "b_x": 9, "lam": 10, "conv_b": 11}
LOSS_ROW, CONV_W_ROW, PACK_ROWS = 7, 12, 16
SMALL = list(VEC_ROW) + ["conv_w", "w_a", "w_x"]


def _pack_small(dg1, dg2, dga, dgr, dgq, dgk, dbf, sq_err, lru_small):
    def body(dg1_ref, dg2_ref, dga_ref, dgr_ref, dgq_ref, dgk_ref, dbf_ref, err_ref, lru_ref, v_ref):
        v_ref[...] = jnp.zeros_like(v_ref)
        v_ref[0:1, :] = dg1_ref[...]
        v_ref[1:2, :] = dg2_ref[...]
        v_ref[2:3, 0:ATT_WIDTH] = dga_ref[...]
        v_ref[3:4, 0:LRU_WIDTH] = dgr_ref[...]
        for row, ref in ((4, dgq_ref), (5, dgk_ref)):
            g = ref[...]
            v_ref[row:row + 1, 0:PAIR] = g + pltpu.roll(g, HEAD_DIM, 1)
        v_ref[6:7, 0:F_PAD] = dbf_ref[...]
        v_ref[LOSS_ROW:LOSS_ROW + 1, 0:128] = err_ref[0:1, :] * (0.5 / D_MODEL)
        v_ref[8:16, 0:LRU_WIDTH] = lru_ref[...]

    ins = [dg1, dg2, dga, dgr, dgq, dgk, dbf, sq_err, lru_small]
    return pl.pallas_call(
        body, name="pack_small", grid=(1,),
        in_specs=[_const(a.shape) for a in ins], out_specs=_const((PACK_ROWS, D_MODEL)),
        out_shape=jax.ShapeDtypeStruct((PACK_ROWS, D_MODEL), F32),
        compiler_params=_params(("arbitrary",)),
    )(*ins)


def _diag_blocks(dwa_bd, dwx_bd):
    blk = LRU_WIDTH // LRU_BLOCKS

    def body(wa_ref, wx_ref, oa_ref, ox_ref):
        for src, dst in ((wa_ref, oa_ref), (wx_ref, ox_ref)):
            for nb in range(LRU_BLOCKS):
                tile = src[blk * nb:blk * (nb + 1), PAIR * (nb // 2):PAIR * (nb // 2 + 1)]
                if nb % 2:
                    tile = pltpu.roll(tile, blk, 1)
                dst[nb] = tile[:, 0:blk]

    out = jax.ShapeDtypeStruct((LRU_BLOCKS, blk, blk), F32)
    return pl.pallas_call(
        body, name="diag_blocks", grid=(1,),
        in_specs=[_const(dwa_bd.shape)] * 2, out_specs=[_const(out.shape)] * 2, out_shape=[out, out],
        compiler_params=_params(("arbitrary",)),
    )(dwa_bd, dwx_bd)


def _adamw_small(recv_v, recv_a, recv_x, params):
    names = list(params)
    flat = [a for n in names for a in params[n]]

    def body(rv_ref, ra_ref, rx_ref, *refs):
        ins, loss_ref, outs = refs[:len(flat)], refs[len(flat)], refs[len(flat) + 1:]
        x, y = lax.axis_index("x"), lax.axis_index("y")
        me = 2 * x + y

        def total(r):
            acc = r[0]
            for d in range(1, 8):
                acc = acc + r[d]
            return acc

        gv, ga, gx = total(rv_ref), total(ra_ref), total(rx_ref)
        loss_ref[...] = gv[LOSS_ROW:LOSS_ROW + 1, 0:128]
        for i, n in enumerate(names):
            w_ref, m_ref, v_ref = ins[3 * i:3 * i + 3]
            g_ref, d_ref, nm_ref, nv_ref = outs[4 * i:4 * i + 4]
            if n in VEC_ROW:
                g = gv[VEC_ROW[n]:VEC_ROW[n] + 1, 0:w_ref.shape[1]]
                w, m, v = w_ref[...], m_ref[...], v_ref[...]
            else:
                if n == "conv_w":
                    full = gv[CONV_W_ROW:CONV_W_ROW + CONV_WIDTH, 0:LRU_WIDTH]
                    width = LRU_WIDTH // 4
                    g = jnp.zeros((CONV_WIDTH, width), F32)
                    for s in range(4):
                        g = jnp.where(me == s, full[:, width * s:width * (s + 1)], g)
                else:
                    g = ga if n == "w_a" else gx
                w, m, v = w_ref[0], m_ref[0], v_ref[0]
            d, nm, nv = _adamw_math(w, g, m, v)
            for ref, val in ((g_ref, g), (d_ref, d), (nm_ref, nm), (nv_ref, nv)):
                if n in VEC_ROW:
                    ref[...] = val
                else:
                    ref[0] = val

    out_shape = [jax.ShapeDtypeStruct((1, 128), F32)] + [jax.ShapeDtypeStruct(params[n][0].shape, F32)
                                                          for n in names for _ in range(4)]
    res = pl.pallas_call(
        body, name="adamw_small", grid=(1,),
        in_specs=[_const(a.shape) for a in (recv_v, recv_a, recv_x, *flat)],
        out_specs=[_const(o.shape) for o in out_shape], out_shape=out_shape,
        compiler_params=_params(("arbitrary",)),
    )(recv_v, recv_a, recv_x, *flat)
    return res[0], {n: tuple(res[1 + 4 * i:5 + 4 * i]) for i, n in enumerate(names)}


def _cat_shards(g, pad_at=None, pad=0):
    _, rows, w = g.shape
    pieces = []
    for s in range(4):
        lo, hi = s * w, (s + 1) * w
        if pad_at is not None and lo < pad_at <= hi:
            pieces += [g[s][:, :pad_at - lo], jnp.zeros((rows, pad), g.dtype)]
            if pad_at < hi:
                pieces.append(g[s][:, pad_at - lo:])
        else:
            pieces.append(g[s])
    return jnp.concatenate(pieces, axis=1)


def _block_diag(w):
    eye = jnp.eye(LRU_BLOCKS, dtype=w.dtype)
    return (w[:, :, None, :] * eye[:, None, :, None]).reshape(LRU_WIDTH, LRU_WIDTH)


def kernel(x, norm1_g, w_in, q_norm_g, k_norm_g, b_f, conv_w, conv_b, w_a, b_a, w_x, b_x, lam, attn_out_g, lru_out_g, w_out, norm2_g, w_gate, w_up, w_down, loss_target, m_norm1_g, m_w_in, m_q_norm_g, m_k_norm_g, m_b_f, m_conv_w, m_conv_b, m_w_a, m_b_a, m_w_x, m_b_x, m_lam, m_attn_out_g, m_lru_out_g, m_w_out, m_norm2_g, m_w_gate, m_w_up, m_w_down, v_norm1_g, v_w_in, v_q_norm_g, v_k_norm_g, v_b_f, v_conv_w, v_conv_b, v_w_a, v_b_a, v_w_x, v_b_x, v_lam, v_attn_out_g, v_lru_out_g, v_w_out, v_norm2_g, v_w_gate, v_w_up, v_w_down):
    args = dict(locals())
    bl, seq, _ = x.shape
    T = bl * seq
    tq = min(ATT_TILE, seq)
    nq = seq // tq
    dff = w_gate.shape[2] * 4

    def transposed(name):
        return name.endswith(("w_in", "w_gate", "w_up"))

    def shard2d(name):
        return jnp.swapaxes(args[name], 1, 2)[0] if transposed(name) else args[name][0]

    g_in, g_cw = _gather_split(shard2d("w_in").astype(MXU_DTYPE), conv_w[0])
    later_shards = [shard2d(n).astype(MXU_DTYPE) for n in ("w_out", "w_gate", "w_up", "w_down")]
    f0 = 3 * ATT_WIDTH
    w_in_t = g_in.reshape(-1, D_MODEL)
    wcat = jnp.concatenate([w_in_t[:f0 + HEADS], jnp.zeros((F_PAD - HEADS, D_MODEL), w_in_t.dtype),
                            w_in_t[f0 + HEADS:]], axis=0)
    cw_full = _cat_shards(g_cw)
    wa_bd = _block_diag(w_a[0]).astype(MXU_DTYPE)
    wx_bd = _block_diag(w_x[0]).astype(MXU_DTYPE)
    gq2 = jnp.tile(q_norm_g, (1, 2))
    gk2 = jnp.tile(k_norm_g, (1, 2))
    bf_pad = jnp.pad(b_f, ((0, 0), (0, F_PAD - HEADS)))

    x2d = x.reshape(T, D_MODEL)
    target2d = loss_target.reshape(T, D_MODEL)

    qkv, qn, qn_t, kn, vb, f2d, lx, lg, h1b = _inproj(x2d, norm1_g, wcat, gq2, gk2)
    fcol = _forget_cumsum(f2d, bf_pad, bl, seq)
    frow = jnp.transpose(fcol.reshape(bl, seq, F_PAD)[:, :, :HEADS], (0, 2, 1)).reshape(bl, N_PAIR, 2, seq)
    fstart = frow[:, :, :, ::tq].reshape(-1)
    att, lse, g_out, g_gate, g_up, g_down = _attn_fwd(qn, kn, vb, frow, fstart, bl, seq, later_shards)
    wout_full = g_out.reshape(D_MODEL, D_MODEL)
    wg_full, wu_full = g_gate.reshape(dff, D_MODEL), g_up.reshape(dff, D_MODEL)
    wd_full = g_down.reshape(dff, D_MODEL)
    h, rec = _lru_fwd(lx, lg, cw_full, conv_b, wa_bd, b_a, wx_bd, b_x, lam, bl, seq)
    x2 = _outproj(x2d, att, rec, attn_out_g, lru_out_g, wout_full)
    gt, up, dy, sq_err = _mlp_fwd(x2, norm2_g, wg_full, wu_full, wd_full, target2d)

    dx2, dx2b, dgtb, dupb, actb, h2b, dyb, dg2 = _mlp_bwd(dy, x2, gt, up, norm2_g, wg_full, wu_full, wd_full)
    dw_down = _matmul_tn(actb, dyb, D_MODEL, "dw_down")
    dw_gate = _matmul_tn(dgtb, h2b, D_MODEL, "dw_gate")
    dw_up = _matmul_tn(dupb, h2b, D_MODEL, "dw_up")
    dattb, dattb_t, delta, drec, mixb, dga, dgr = _outproj_bwd(dx2b, att, rec, attn_out_g, lru_out_g, wout_full)
    dw_out = _matmul_tn(mixb, dx2b, D_MODEL, "dw_out")
    dlx, dlg, dwa_bd, dwx_bd, lru_small = _lru_bwd(drec, lg, h, lx, cw_full, conv_b, wa_bd, b_a, wx_bd, b_x, lam, bl, seq)
    early_slabs = [dw_out.reshape(4, D_MODEL // 4, D_MODEL), dw_gate.reshape(4, dff // 4, D_MODEL),
                   dw_up.reshape(4, dff // 4, D_MODEL), dw_down.reshape(4, dff // 4, D_MODEL)]
    pack_a, pack_x = _diag_blocks(dwa_bd, dwx_bd)
    dq, dk, dv, dfrow, *recv_early = _attn_bwd(qn, kn, vb, dattb, qn_t, dattb_t, lse, delta, frow, fstart, bl, seq,
                                               early_slabs, [pack_a, pack_x])
    recv_early, (recv_a, recv_x) = recv_early[:4], recv_early[4:]
    dfcol = jnp.pad(jnp.transpose(dfrow.reshape(bl, HEADS, seq), (0, 2, 1)), ((0, 0), (0, 0), (0, F_PAD - HEADS)))
    df, dbf = _forget_bwd(dfcol.reshape(T, F_PAD), f2d, bf_pad, bl, seq)
    dprojb, dgq, dgk = _dproj(dq, dk, dv, qkv, df, dlx, dlg, gq2, gk2)
    dwcat = _matmul_tn(dprojb, h1b, D_MODEL, "dw_in")
    dw_in_slabs = jnp.concatenate([dwcat[:f0 + HEADS], dwcat[f0 + F_PAD:]], axis=0).astype(jnp.bfloat16).reshape(
        4, -1, D_MODEL)
    grad_x, dg1, recv_in = _inproj_bwd(dprojb, x2d, dx2, norm1_g, wcat, [dw_in_slabs])

    pack_v = _pack_small(dg1, dg2, dga, dgr, dgq, dgk, dbf, sq_err, lru_small)
    recv = [recv_in] + recv_early
    big = ["w_in", "w_out", "w_gate", "w_up", "w_down"]
    part = [_sum_slabs(r, "sum_" + n) for r, n in zip(recv, big)]
    *theirs, recv_v = _swap_with_sibling(part, [pack_v])
    out = {}
    for n, a, b_ in zip(big, part, theirs):
        res = _adamw_pair(a, b_, shard2d(n), shard2d("m_" + n), shard2d("v_" + n), "adamw_" + n)
        out[n] = tuple(jnp.swapaxes(r[None], 1, 2) if transposed(n) else r[None] for r in res)
    loss_row, small_out = _adamw_small(recv_v, recv_a, recv_x,
                                       {n: (args[n], args["m_" + n], args["v_" + n]) for n in SMALL})
    out.update(small_out)
    loss = loss_row[0, 0]

    order = ["norm1_g", "w_in", "q_norm_g", "k_norm_g", "b_f", "conv_w", "conv_b", "w_a", "b_a", "w_x", "b_x", "lam",
             "attn_out_g", "lru_out_g", "w_out", "norm2_g", "w_gate", "w_up", "w_down"]
    return (loss, grad_x.reshape(bl, seq, D_MODEL), *[out[n][0] for n in order], *[out[n][1] for n in order],
            *[out[n][2] for n in order], *[out[n][3] for n in order])
```

```python
import functools
import math

import jax
import jax.numpy as jnp
from jax import lax
from jax.experimental import pallas as pl
from jax.experimental.pallas import tpu as pltpu

F32 = jnp.float32
MXU_DTYPE = jnp.bfloat16
MESH = pl.DeviceIdType.MESH

D_MODEL = 1024
ATT_WIDTH = 512
LRU_WIDTH = 512
HEADS = 8
HEAD_DIM = 64
PAIR = 2 * HEAD_DIM
N_PAIR = HEADS // 2
LRU_BLOCKS = 8
CONV_WIDTH = 4
LRU_C = 8.0
NORM_EPS = 1e-6
QK_SCALE = 1.0 / math.sqrt(HEAD_DIM)
F_PAD = 128
N_CAT = 3 * ATT_WIDTH + F_PAD + 2 * LRU_WIDTH
NEG = -1e30

ADAM_LR, ADAM_B1, ADAM_B2, ADAM_EPS, ADAM_WD, ADAM_STEP = 0.001, 0.9, 0.999, 1e-08, 0.01, 10

TOKEN_TILE = 256
ATT_TILE = 512
LRU_TILE = 256
VMEM_SMALL = 32 * 1024 * 1024
VMEM_LARGE = 56 * 1024 * 1024


def _params(sem, vmem=VMEM_SMALL):
    return pltpu.CompilerParams(dimension_semantics=sem, vmem_limit_bytes=vmem)


def _const(shape):
    nd = len(shape)
    return pl.BlockSpec(shape, lambda *_: (0,) * nd)


def _sigmoid(x):
    return 1.0 / (1.0 + jnp.exp(-x))


def _nt(a, b):
    return lax.dot_general(a, b, (((1,), (1,)), ((), ())), preferred_element_type=F32)


def _tn(a, b):
    return lax.dot_general(a, b, (((0,), (0,)), ((), ())), preferred_element_type=F32)


def _half_sums(t, lo):
    s_lo = jnp.sum(jnp.where(lo, t, 0.0), axis=-1, keepdims=True)
    s_hi = jnp.sum(jnp.where(lo, 0.0, t), axis=-1, keepdims=True)
    return jnp.where(lo, s_lo, s_hi)


def _lo_mask():
    return lax.broadcasted_iota(jnp.int32, (1, PAIR), 1) < HEAD_DIM


def _other_chips(x, y):
    return [(1 - x, y), (x, 1 - y), (1 - x, 1 - y)]


def _chip_copies(ins, outs, send_sems, recv_sems, loc_sems, scatter):
    x, y, c = lax.axis_index("x"), lax.axis_index("y"), lax.axis_index("c")
    me = 2 * x + y
    copies = []
    for w in range(len(ins)):
        copies.append(pltpu.make_async_copy(ins[w].at[me] if scatter else ins[w], outs[w].at[me], loc_sems.at[w]))
        for k, (cx, cy) in enumerate(_other_chips(x, y)):
            copies.append(pltpu.make_async_remote_copy(
                src_ref=ins[w].at[2 * cx + cy] if scatter else ins[w], dst_ref=outs[w].at[me],
                send_sem=send_sems.at[3 * w + k], recv_sem=recv_sems.at[3 * w + k],
                device_id=(cx, cy, c), device_id_type=MESH))
    return copies


def _chip_sems(n):
    return [pltpu.SemaphoreType.DMA((3 * n,)), pltpu.SemaphoreType.DMA((3 * n,)), pltpu.SemaphoreType.DMA((n,))]


def _norm1_and_gather(x2d, g1, shard, small):
    T = x2d.shape[0]
    tm = 2 * TOKEN_TILE
    steps = T // tm
    half = shard.shape[1] // 2

    def body(x_ref, g1_ref, w_ref, s_ref, h_ref, ow_ref, os_ref, ici_send, ici_recv, d2d_send, d2d_recv, sm_send,
             sm_recv, loc_sems):
        i = pl.program_id(0)
        x, y, c = lax.axis_index("x"), lax.axis_index("y"), lax.axis_index("c")
        me = 2 * x + y
        mine = pl.ds(pl.multiple_of(c * half, half), half)
        local = [pltpu.make_async_copy(w_ref, ow_ref.at[me], loc_sems.at[0]),
                 pltpu.make_async_copy(s_ref, os_ref.at[me], loc_sems.at[1])]
        fetch, little, forward = [], [], []
        for k, (cx, cy) in enumerate(_other_chips(x, y)):
            src_chip = 2 * cx + cy
            fetch.append(pltpu.make_async_remote_copy(
                src_ref=w_ref.at[:, mine], dst_ref=ow_ref.at[me, :, mine], send_sem=ici_send.at[k],
                recv_sem=ici_recv.at[k], device_id=(cx, cy, c), device_id_type=MESH))
            little.append(pltpu.make_async_remote_copy(
                src_ref=s_ref, dst_ref=os_ref.at[me], send_sem=sm_send.at[k], recv_sem=sm_recv.at[k],
                device_id=(cx, cy, c), device_id_type=MESH))
            forward.append(pltpu.make_async_remote_copy(
                src_ref=ow_ref.at[src_chip, :, mine], dst_ref=ow_ref.at[src_chip, :, mine], send_sem=d2d_send.at[k],
                recv_sem=d2d_recv.at[k], device_id=(x, y, 1 - c), device_id_type=MESH))

        @pl.when(i == 0)
        def _():
            for cp in local + fetch + little:
                cp.start()

        xv = x_ref[...]
        r = lax.rsqrt(jnp.mean(xv * xv, axis=-1, keepdims=True) + NORM_EPS)
        h_ref[...] = (xv * r * g1_ref[...]).astype(MXU_DTYPE)

        @pl.when(i == steps - 1)
        def _():
            for k in range(3):
                fetch[k].wait_recv()
                forward[k].start()
            for cp in fetch:
                cp.wait_send()
            for cp in little + forward + local:
                cp.wait()

    hbm = pl.BlockSpec(memory_space=pl.ANY)
    return pl.pallas_call(
        body, name="norm1_gather", grid=(steps,),
        out_shape=[jax.ShapeDtypeStruct((T, D_MODEL), MXU_DTYPE),
                   jax.ShapeDtypeStruct((4,) + shard.shape, shard.dtype),
                   jax.ShapeDtypeStruct((4,) + small.shape, small.dtype)],
        in_specs=[pl.BlockSpec((tm, D_MODEL), lambda i: (i, 0)), _const((1, D_MODEL)), hbm, hbm],
        out_specs=[pl.BlockSpec((tm, D_MODEL), lambda i: (i, 0)), hbm, hbm],
        scratch_shapes=[pltpu.SemaphoreType.DMA((3,))] * 6 + [pltpu.SemaphoreType.DMA((2,))],
        compiler_params=_params(("arbitrary",)),
    )(x2d, g1, shard, small)


def _device_copies(packs_in, packs_out, psend, precv, loc_sems, loc_base):
    x, y, c = lax.axis_index("x"), lax.axis_index("y"), lax.axis_index("c")
    dev = 4 * x + 2 * y + c
    copies = []
    for j in range(len(packs_in)):
        copies.append(pltpu.make_async_copy(packs_in[j], packs_out[j].at[dev], loc_sems.at[loc_base + j]))
        for k in range(1, 8):
            fx, fy, fc = (k >> 2) & 1, (k >> 1) & 1, k & 1
            tx = (1 - x) if fx else x
            ty = (1 - y) if fy else y
            tc = (1 - c) if fc else c
            copies.append(pltpu.make_async_remote_copy(
                src_ref=packs_in[j], dst_ref=packs_out[j].at[dev],
                send_sem=psend.at[7 * j + k - 1], recv_sem=precv.at[7 * j + k - 1],
                device_id=(tx, ty, tc), device_id_type=MESH))
    return copies


def _swap_with_sibling(arrs, packs):
    n, npk = len(arrs), len(packs)

    def body(*refs):
        ins, pack_in = refs[:n], refs[n:n + npk]
        outs, pack_out = refs[n + npk:2 * n + npk], refs[2 * n + npk:2 * (n + npk)]
        send_sems, recv_sems, loc_sems, psend, precv = refs[2 * (n + npk):]
        x, y, c = lax.axis_index("x"), lax.axis_index("y"), lax.axis_index("c")
        copies = [pltpu.make_async_remote_copy(
            src_ref=ins[w], dst_ref=outs[w], send_sem=send_sems.at[w], recv_sem=recv_sems.at[w],
            device_id=(x, y, 1 - c), device_id_type=MESH) for w in range(n)]
        copies += _device_copies(pack_in, pack_out, psend, precv, loc_sems, 0)
        for cp in copies:
            cp.start()
        for cp in copies:
            cp.wait()

    return pl.pallas_call(
        body, name="swap_sibling",
        out_shape=[jax.ShapeDtypeStruct(a.shape, a.dtype) for a in arrs]
        + [jax.ShapeDtypeStruct((8,) + p.shape, p.dtype) for p in packs],
        in_specs=[pl.BlockSpec(memory_space=pl.ANY)] * (n + npk),
        out_specs=[pl.BlockSpec(memory_space=pl.ANY)] * (n + npk),
        scratch_shapes=[pltpu.SemaphoreType.DMA((n,)), pltpu.SemaphoreType.DMA((n,)), pltpu.SemaphoreType.DMA((npk,)),
                        pltpu.SemaphoreType.DMA((7 * npk,)), pltpu.SemaphoreType.DMA((7 * npk,))],
    )(*arrs, *packs)


def _head_norm(t, g2, lo):
    rr = lax.rsqrt(_half_sums(t * t, lo) * (1.0 / HEAD_DIM) + NORM_EPS)
    return t * rr * g2


def _inproj(h1, wcat, gq2, gk2):
    T = h1.shape[0]
    tm = TOKEN_TILE

    def body(h_ref, w_ref, gq_ref, gk_ref, qkv_ref, qn_ref, qt_ref, kn_ref, vb_ref, f_ref, lx_ref, lg_ref):
        proj = _nt(h_ref[...], w_ref[...])
        qkv_ref[...] = proj[:, :3 * ATT_WIDTH]
        lo = _lo_mask()
        for p in range(N_PAIR):
            cols = slice(PAIR * p, PAIR * (p + 1))
            q = proj[:, PAIR * p:PAIR * (p + 1)]
            k = proj[:, ATT_WIDTH + PAIR * p:ATT_WIDTH + PAIR * (p + 1)]
            qs = _head_norm(q, gq_ref[...], lo) * QK_SCALE
            qn_ref[:, cols] = qs.astype(MXU_DTYPE)
            qt_ref[cols, :] = qs.T.astype(MXU_DTYPE)
            kn_ref[:, cols] = _head_norm(k, gk_ref[...], lo).astype(MXU_DTYPE)
        vb_ref[...] = proj[:, 2 * ATT_WIDTH:3 * ATT_WIDTH].astype(MXU_DTYPE)
        f0 = 3 * ATT_WIDTH
        f_ref[...] = proj[:, f0:f0 + F_PAD]
        lx_ref[...] = proj[:, f0 + F_PAD:f0 + F_PAD + LRU_WIDTH]
        lg_ref[...] = proj[:, f0 + F_PAD + LRU_WIDTH:]

    row = lambda w: pl.BlockSpec((tm, w), lambda i: (i, 0))
    return pl.pallas_call(
        body, name="inproj", grid=(T // tm,),
        in_specs=[row(D_MODEL), _const((N_CAT, D_MODEL)), _const((1, PAIR)), _const((1, PAIR))],
        out_specs=[row(3 * ATT_WIDTH), row(ATT_WIDTH), pl.BlockSpec((ATT_WIDTH, tm), lambda i: (0, i)), row(ATT_WIDTH),
                   row(ATT_WIDTH), row(F_PAD), row(LRU_WIDTH), row(LRU_WIDTH)],
        out_shape=[jax.ShapeDtypeStruct((T, 3 * ATT_WIDTH), F32),
                   jax.ShapeDtypeStruct((T, ATT_WIDTH), MXU_DTYPE), jax.ShapeDtypeStruct((ATT_WIDTH, T), MXU_DTYPE),
                   jax.ShapeDtypeStruct((T, ATT_WIDTH), MXU_DTYPE),
                   jax.ShapeDtypeStruct((T, ATT_WIDTH), MXU_DTYPE), jax.ShapeDtypeStruct((T, F_PAD), F32),
                   jax.ShapeDtypeStruct((T, LRU_WIDTH), F32), jax.ShapeDtypeStruct((T, LRU_WIDTH), F32)],
        compiler_params=_params(("parallel",), VMEM_LARGE),
    )(h1, wcat, gq2, gk2)


def _forget_cumsum(f2d, bf, bl, seq):
    def body(z_ref, b_ref, o_ref):
        z = z_ref[...] + b_ref[...]
        lf = jnp.minimum(z, 0.0) - jnp.log(1.0 + jnp.exp(-jnp.abs(z)))
        row = lax.broadcasted_iota(jnp.int32, (seq, F_PAD), 0)
        k = 1
        while k < seq:
            lf = lf + jnp.where(row >= k, pltpu.roll(lf, k, 0), 0.0)
            k *= 2
        o_ref[...] = lf

    return pl.pallas_call(
        body, name="forget_cumsum", grid=(bl,),
        in_specs=[pl.BlockSpec((seq, F_PAD), lambda b: (b, 0)), _const((1, F_PAD))],
        out_specs=pl.BlockSpec((seq, F_PAD), lambda b: (b, 0)),
        out_shape=jax.ShapeDtypeStruct(f2d.shape, F32),
        compiler_params=_params(("parallel",)),
    )(f2d, bf)


def _attn_fwd(qn, kn, vb, frow, fstart, bl, seq, shards):
    tq = min(ATT_TILE, seq)
    nq = seq // tq
    T = bl * seq
    n = len(shards)

    def body(fs_ref, q_ref, k_ref, v_ref, fr_ref, *rest):
        g_in, (o_ref, lse_ref), g_out, sems = rest[:n], rest[n:n + 2], rest[n + 2:2 * n + 2], rest[2 * n + 2:]
        b, p, i = pl.program_id(0), pl.program_id(1), pl.program_id(2)
        copies = _chip_copies(g_in, g_out, *sems, scatter=False)

        @pl.when((b == 0) & (p == 0) & (i == 0))
        def _():
            for cp in copies:
                cp.start()

        lane = lax.broadcasted_iota(jnp.int32, (1, PAIR), 1)
        rows = lax.broadcasted_iota(jnp.int32, (tq, tq), 0)
        cols = lax.broadcasted_iota(jnp.int32, (tq, tq), 1)
        causal = cols <= rows
        q = q_ref[...]
        hms = [(lane >= HEAD_DIM * hh) & (lane < HEAD_DIM * (hh + 1)) for hh in range(2)]
        qhs = [jnp.where(hm, q, jnp.zeros_like(q)) for hm in hms]
        shifts = [fs_ref[((b * N_PAIR + p) * 2 + hh) * nq + i] for hh in range(2)]
        sum_lane = [HEAD_DIM * (1 - hh) for hh in range(2)]

        def block(j, carry, masked):
            start = pl.multiple_of(j * tq, tq)
            k = k_ref[pl.ds(start, tq), :]
            v = v_ref[pl.ds(start, tq), :]
            new = []
            for hh in range(2):
                m, acc = carry[hh]
                s = lax.dot_general(qhs[hh], k, (((1,), (1,)), ((), ())), preferred_element_type=F32)
                s = s - (fr_ref[0, 0, hh:hh + 1, pl.ds(start, tq)] - shifts[hh])
                if masked:
                    s = jnp.where(causal, s, NEG)
                m_new = jnp.maximum(m, jnp.max(s, axis=-1, keepdims=True))
                alpha = jnp.exp(m - m_new)
                pe = jnp.exp(s - m_new)
                vh = jnp.where(hms[hh], v, jnp.where(lane == sum_lane[hh], 1.0, 0.0).astype(v.dtype))
                pb = pe.astype(MXU_DTYPE)
                p_lo = (pe - pb.astype(F32)).astype(MXU_DTYPE)
                acc = (alpha * acc + jnp.dot(pb, vh, preferred_element_type=F32)
                       + jnp.dot(p_lo, vh, preferred_element_type=F32))
                new.append((m_new, acc))
            return tuple(new)

        init = (jnp.full((tq, 1), NEG, F32), jnp.zeros((tq, PAIR), F32))
        carry = lax.fori_loop(0, i, functools.partial(block, masked=False), (init, init))
        carry = block(i, carry, True)
        out = jnp.zeros((tq, PAIR), F32)
        lse = jnp.zeros((tq, PAIR), F32)
        for hh in range(2):
            m, acc = carry[hh]
            l = acc[:, sum_lane[hh]:sum_lane[hh] + 1]
            out = jnp.where(hms[hh], acc * (1.0 / l), out)
            lse = jnp.where(hms[hh], m + jnp.log(l), lse)
        o_ref[...] = out
        lse_ref[...] = lse

        @pl.when((b == bl - 1) & (p == N_PAIR - 1) & (i == nq - 1))
        def _():
            for cp in copies:
                cp.wait()

    blk = pl.BlockSpec((tq, PAIR), lambda b, p, i: (b * nq + i, p))
    full = pl.BlockSpec((seq, PAIR), lambda b, p, i: (b, p))
    return pl.pallas_call(
        body, name="attn_fwd", grid=(bl, N_PAIR, nq),
        in_specs=[pl.BlockSpec(memory_space=pltpu.SMEM), blk, full, full,
                  pl.BlockSpec((1, 1, 2, seq), lambda b, p, i: (b, p, 0, 0))] + [pl.BlockSpec(memory_space=pl.ANY)] * n,
        out_specs=[blk, blk] + [pl.BlockSpec(memory_space=pl.ANY)] * n,
        out_shape=[jax.ShapeDtypeStruct((T, ATT_WIDTH), F32)] * 2
        + [jax.ShapeDtypeStruct((4,) + s.shape, s.dtype) for s in shards],
        scratch_shapes=_chip_sems(n),
        compiler_params=_params(("arbitrary", "arbitrary", "arbitrary")),
    )(fstart, qn, kn, vb, frow, *shards)


def _conv_taps(lx, prev8, cw, cb):
    xs = jnp.concatenate([prev8, lx], axis=0)
    shifted = [lx] + [pltpu.roll(xs, k, 0)[8:] for k in range(1, CONV_WIDTH)]
    xc = cb + cw[CONV_WIDTH - 1:CONV_WIDTH] * lx
    for k in range(1, CONV_WIDTH):
        xc = xc + cw[CONV_WIDTH - 1 - k:CONV_WIDTH - k] * shifted[k]
    return xc, shifted


def _lru_gates(xc, wa, ba, wx, bx, lam):
    xb = xc.astype(MXU_DTYPE)
    r = _sigmoid(jnp.dot(xb, wa, preferred_element_type=F32) + ba)
    ig = _sigmoid(jnp.dot(xb, wx, preferred_element_type=F32) + bx)
    sp = jnp.maximum(-lam, 0.0) + jnp.log(1.0 + jnp.exp(-jnp.abs(lam)))
    log_a = -LRU_C * r * sp
    a = jnp.exp(log_a)
    th = jnp.tanh(log_a)
    mult = jnp.sqrt(-2.0 * th / (1.0 - th))
    return r, ig, sp, a, mult


def _gelu_parts(x):
    c0 = math.sqrt(2.0 / math.pi)
    t = jnp.tanh(c0 * (x + 0.044715 * x * x * x))
    g = 0.5 * x * (1.0 + t)
    dg = 0.5 * (1.0 + t) + 0.5 * x * (1.0 - t * t) * c0 * (1.0 + 3.0 * 0.044715 * x * x)
    return g, dg


def _lru_fwd(lx, lg, cw, cb, wa, ba, wx, bx, lam, bl, seq):
    tc = min(LRU_TILE, seq)
    nc = seq // tc
    T = bl * seq

    def body(lx_ref, lxp_ref, lg_ref, cw_ref, cb_ref, wa_ref, ba_ref, wx_ref, bx_ref, lam_ref,
             h_ref, rec_ref, hc_ref):
        i = pl.program_id(1)

        @pl.when(i == 0)
        def _():
            hc_ref[...] = jnp.zeros_like(hc_ref)

        lxv = lx_ref[...]
        prev8 = jnp.where(i > 0, lxp_ref[...], 0.0)
        xc, _ = _conv_taps(lxv, prev8, cw_ref[...], cb_ref[...])
        _, ig, _, a, mult = _lru_gates(xc, wa_ref[...], ba_ref[...], wx_ref[...], bx_ref[...], lam_ref[...])
        u = mult * (ig * xc)
        sub = lax.broadcasted_iota(jnp.int32, (tc, LRU_WIDTH), 0) & 7
        A, B = a, u
        for k in (1, 2, 4):
            a_s = jnp.where(sub >= k, pltpu.roll(A, k, 0), 1.0)
            b_s = jnp.where(sub >= k, pltpu.roll(B, k, 0), 0.0)
            B = A * b_s + B
            A = A * a_s
        carry = hc_ref[0:1, :]
        groups = []
        for g in range(tc // 8):
            hg = A[8 * g:8 * (g + 1)] * carry + B[8 * g:8 * (g + 1)]
            groups.append(hg)
            carry = hg[7:8]
        h = jnp.concatenate(groups, axis=0)
        hc_ref[0:1, :] = carry
        h_ref[...] = h
        g, _ = _gelu_parts(lg_ref[...])
        rec_ref[...] = h * g

    tile = pl.BlockSpec((tc, LRU_WIDTH), lambda b, i: (b * nc + i, 0))
    prev = pl.BlockSpec((8, LRU_WIDTH), lambda b, i: (jnp.maximum((b * seq + i * tc) // 8 - 1, 0), 0))
    vec = _const((1, LRU_WIDTH))
    mat = _const((LRU_WIDTH, LRU_WIDTH))
    return pl.pallas_call(
        body, name="lru_fwd", grid=(bl, nc),
        in_specs=[tile, prev, tile, _const((CONV_WIDTH, LRU_WIDTH)), vec, mat, vec, mat, vec, vec],
        out_specs=[tile, tile],
        out_shape=[jax.ShapeDtypeStruct((T, LRU_WIDTH), F32), jax.ShapeDtypeStruct((T, LRU_WIDTH), F32)],
        scratch_shapes=[pltpu.VMEM((8, LRU_WIDTH), F32)],
        compiler_params=_params(("arbitrary", "arbitrary")),
    )(lx, lx, lg, cw, cb, wa, ba, wx, bx, lam)


def _outproj(x2d, att, rec, ga, gr, wout):
    T = x2d.shape[0]
    tm = TOKEN_TILE

    def body(x_ref, a_ref, r_ref, ga_ref, gr_ref, w_ref, o_ref):
        a = a_ref[...]
        rc = r_ref[...]
        na = a * lax.rsqrt(jnp.mean(a * a, axis=-1, keepdims=True) + NORM_EPS) * ga_ref[...]
        nr = rc * lax.rsqrt(jnp.mean(rc * rc, axis=-1, keepdims=True) + NORM_EPS) * gr_ref[...]
        o_ref[...] = (x_ref[...]
                      + jnp.dot(na.astype(MXU_DTYPE), w_ref[:ATT_WIDTH, :], preferred_element_type=F32)
                      + jnp.dot(nr.astype(MXU_DTYPE), w_ref[ATT_WIDTH:, :], preferred_element_type=F32))

    row = lambda w: pl.BlockSpec((tm, w), lambda i: (i, 0))
    return pl.pallas_call(
        body, name="outproj", grid=(T // tm,),
        in_specs=[row(D_MODEL), row(ATT_WIDTH), row(LRU_WIDTH), _const((1, ATT_WIDTH)), _const((1, LRU_WIDTH)),
                  _const((D_MODEL, D_MODEL))],
        out_specs=row(D_MODEL),
        out_shape=jax.ShapeDtypeStruct((T, D_MODEL), F32),
        compiler_params=_params(("parallel",)),
    )(x2d, att, rec, ga, gr, wout)


def _mlp_fwd(x2, g2, wg, wu, wd, target):
    T = x2.shape[0]
    tm = TOKEN_TILE
    dff = wg.shape[0]

    def body(x_ref, g_ref, wg_ref, wu_ref, wd_ref, t_ref, gt_ref, up_ref, dy_ref, loss_ref):
        @pl.when(pl.program_id(0) == 0)
        def _():
            loss_ref[...] = jnp.zeros_like(loss_ref)

        x = x_ref[...]
        r = lax.rsqrt(jnp.mean(x * x, axis=-1, keepdims=True) + NORM_EPS)
        h = (x * r * g_ref[...]).astype(MXU_DTYPE)
        gt = _nt(h, wg_ref[...])
        up = _nt(h, wu_ref[...])
        gt_ref[...] = gt
        up_ref[...] = up
        act = (gt * _sigmoid(gt) * up).astype(MXU_DTYPE)
        y = x + jnp.dot(act, wd_ref[...], preferred_element_type=F32)
        e = y - t_ref[...]
        dy_ref[...] = e * (1.0 / D_MODEL)
        loss_ref[...] += jnp.sum(e * e)

    row = lambda w: pl.BlockSpec((tm, w), lambda i: (i, 0))
    return pl.pallas_call(
        body, name="mlp_fwd", grid=(T // tm,),
        in_specs=[row(D_MODEL), _const((1, D_MODEL)), _const((dff, D_MODEL)), _const((dff, D_MODEL)),
                  _const((dff, D_MODEL)), row(D_MODEL)],
        out_specs=[row(dff), row(dff), row(D_MODEL), _const((8, 128))],
        out_shape=[jax.ShapeDtypeStruct((T, dff), F32), jax.ShapeDtypeStruct((T, dff), F32),
                   jax.ShapeDtypeStruct((T, D_MODEL), F32), jax.ShapeDtypeStruct((8, 128), F32)],
        compiler_params=_params(("arbitrary",), VMEM_LARGE),
    )(x2, g2, wg, wu, wd, target)


def _mlp_bwd(dy, x2, gt, up, g2, wg, wu, wd):
    T = x2.shape[0]
    tm = TOKEN_TILE
    dff = wg.shape[0]

    def body(dy_ref, x_ref, gt_ref, up_ref, g_ref, wg_ref, wu_ref, wd_ref,
             dx_ref, dxb_ref, dgt_ref, dup_ref, act_ref, h_ref, dyb_ref, dg_ref):
        @pl.when(pl.program_id(0) == 0)
        def _():
            dg_ref[...] = jnp.zeros_like(dg_ref)

        dy_v = dy_ref[...]
        dyb = dy_v.astype(MXU_DTYPE)
        dyb_ref[...] = dyb
        x = x_ref[...]
        r = lax.rsqrt(jnp.mean(x * x, axis=-1, keepdims=True) + NORM_EPS)
        xh = x * r
        h_ref[...] = (xh * g_ref[...]).astype(MXU_DTYPE)
        gt_v = gt_ref[...]
        up_v = up_ref[...]
        sg = _sigmoid(gt_v)
        silu = gt_v * sg
        act_ref[...] = (silu * up_v).astype(MXU_DTYPE)
        dact = _nt(dyb, wd_ref[...])
        dup = (dact * silu).astype(MXU_DTYPE)
        dgt = (dact * up_v * (sg * (1.0 + gt_v * (1.0 - sg)))).astype(MXU_DTYPE)
        dup_ref[...] = dup
        dgt_ref[...] = dgt
        dh = (jnp.dot(dgt, wg_ref[...], preferred_element_type=F32)
              + jnp.dot(dup, wu_ref[...], preferred_element_type=F32))
        dg_ref[...] += jnp.sum(dh * xh, axis=0, keepdims=True)
        dxh = dh * g_ref[...]
        dx = dy_v + r * (dxh - xh * jnp.mean(dxh * xh, axis=-1, keepdims=True))
        dx_ref[...] = dx
        dxb_ref[...] = dx.astype(MXU_DTYPE)

    row = lambda w: pl.BlockSpec((tm, w), lambda i: (i, 0))
    return pl.pallas_call(
        body, name="mlp_bwd", grid=(T // tm,),
        in_specs=[row(D_MODEL), row(D_MODEL), row(dff), row(dff), _const((1, D_MODEL)),
                  _const((dff, D_MODEL)), _const((dff, D_MODEL)), _const((dff, D_MODEL))],
        out_specs=[row(D_MODEL), row(D_MODEL), row(dff), row(dff), row(dff), row(D_MODEL), row(D_MODEL),
                   _const((1, D_MODEL))],
        out_shape=[jax.ShapeDtypeStruct((T, D_MODEL), F32), jax.ShapeDtypeStruct((T, D_MODEL), MXU_DTYPE),
                   jax.ShapeDtypeStruct((T, dff), MXU_DTYPE), jax.ShapeDtypeStruct((T, dff), MXU_DTYPE),
                   jax.ShapeDtypeStruct((T, dff), MXU_DTYPE), jax.ShapeDtypeStruct((T, D_MODEL), MXU_DTYPE),
                   jax.ShapeDtypeStruct((T, D_MODEL), MXU_DTYPE), jax.ShapeDtypeStruct((1, D_MODEL), F32)],
        compiler_params=_params(("arbitrary",), VMEM_LARGE),
    )(dy, x2, gt, up, g2, wg, wu, wd)


def _matmul_tn(a, b, tn, name):
    T, K = a.shape
    N = b.shape[1]
    tt = min(1024, T)

    def body(a_ref, b_ref, o_ref):
        @pl.when(pl.program_id(1) == 0)
        def _():
            o_ref[...] = jnp.zeros_like(o_ref)

        o_ref[...] += _tn(a_ref[...], b_ref[...])

    return pl.pallas_call(
        body, name=name, grid=(N // tn, T // tt),
        in_specs=[pl.BlockSpec((tt, K), lambda n, t: (t, 0)), pl.BlockSpec((tt, tn), lambda n, t: (t, n))],
        out_specs=pl.BlockSpec((K, tn), lambda n, t: (0, n)),
        out_shape=jax.ShapeDtypeStruct((K, N), F32),
        compiler_params=_params(("parallel", "arbitrary"), VMEM_LARGE),
    )(a, b)


def _outproj_bwd(dx2b, att, rec, ga, gr, wout):
    T = att.shape[0]
    tm = TOKEN_TILE

    def body(dx_ref, a_ref, r_ref, ga_ref, gr_ref, w_ref, datt_ref, dattt_ref, delta_ref, drec_ref, mix_ref, dga_ref,
             dgr_ref):
        @pl.when(pl.program_id(0) == 0)
        def _():
            dga_ref[...] = jnp.zeros_like(dga_ref)
            dgr_ref[...] = jnp.zeros_like(dgr_ref)

        dmix = _nt(dx_ref[...], w_ref[...])

        def norm_bwd(v, g, dn):
            rr = lax.rsqrt(jnp.mean(v * v, axis=-1, keepdims=True) + NORM_EPS)
            vh = v * rr
            dvh = dn * g
            dv = rr * (dvh - vh * jnp.mean(dvh * vh, axis=-1, keepdims=True))
            return vh, dv, jnp.sum(dn * vh, axis=0, keepdims=True)

        a = a_ref[...]
        ah, datt, dga = norm_bwd(a, ga_ref[...], dmix[:, :ATT_WIDTH])
        rh, drec, dgr = norm_bwd(r_ref[...], gr_ref[...], dmix[:, ATT_WIDTH:])
        dga_ref[...] += dga
        dgr_ref[...] += dgr
        mix_ref[:, :ATT_WIDTH] = (ah * ga_ref[...]).astype(MXU_DTYPE)
        mix_ref[:, ATT_WIDTH:] = (rh * gr_ref[...]).astype(MXU_DTYPE)
        dattb = datt.astype(MXU_DTYPE)
        datt_ref[...] = dattb
        dattt_ref[...] = datt.T.astype(MXU_DTYPE)
        drec_ref[...] = drec
        lo = _lo_mask()
        prod = dattb.astype(F32) * a
        for p in range(N_PAIR):
            delta_ref[:, PAIR * p:PAIR * (p + 1)] = _half_sums(prod[:, PAIR * p:PAIR * (p + 1)], lo)

    row = lambda w: pl.BlockSpec((tm, w), lambda i: (i, 0))
    return pl.pallas_call(
        body, name="outproj_bwd", grid=(T // tm,),
        in_specs=[row(D_MODEL), row(ATT_WIDTH), row(LRU_WIDTH), _const((1, ATT_WIDTH)), _const((1, LRU_WIDTH)),
                  _const((D_MODEL, D_MODEL))],
        out_specs=[row(ATT_WIDTH), pl.BlockSpec((ATT_WIDTH, tm), lambda i: (0, i)), row(ATT_WIDTH), row(LRU_WIDTH),
                   row(D_MODEL), _const((1, ATT_WIDTH)), _const((1, LRU_WIDTH))],
        out_shape=[jax.ShapeDtypeStruct((T, ATT_WIDTH), MXU_DTYPE), jax.ShapeDtypeStruct((ATT_WIDTH, T), MXU_DTYPE),
                   jax.ShapeDtypeStruct((T, ATT_WIDTH), F32),
                   jax.ShapeDtypeStruct((T, LRU_WIDTH), F32), jax.ShapeDtypeStruct((T, D_MODEL), MXU_DTYPE),
                   jax.ShapeDtypeStruct((1, ATT_WIDTH), F32), jax.ShapeDtypeStruct((1, LRU_WIDTH), F32)],
        compiler_params=_params(("arbitrary",)),
    )(dx2b, att, rec, ga, gr, wout)


def _lru_bwd(drec, lg, h, lx, cw, cb, wa, ba, wx, bx, lam, bl, seq):
    tc = min(LRU_TILE, seq)
    nc = seq // tc
    T = bl * seq
    n = tc

    def body(dr_ref, lg_ref, h_ref, hp_ref, lx_ref, lxp_ref, cw_ref, cb_ref, wa_ref, ba_ref, wx_ref, bx_ref, lam_ref,
             dlx_ref, dlg_ref, dwa_ref, dwx_ref, small_ref, gc_ref, dxn_ref):
        b, i = pl.program_id(0), pl.program_id(1)
        ir = nc - 1 - i

        @pl.when((b == 0) & (i == 0))
        def _():
            dwa_ref[...] = jnp.zeros_like(dwa_ref)
            dwx_ref[...] = jnp.zeros_like(dwx_ref)
            small_ref[...] = jnp.zeros_like(small_ref)

        @pl.when(i == 0)
        def _():
            gc_ref[...] = jnp.zeros_like(gc_ref)
            dxn_ref[...] = jnp.zeros_like(dxn_ref)

        cw = cw_ref[...]
        lam_v = lam_ref[...]
        lxv = lx_ref[...]
        prev8 = jnp.where(ir > 0, lxp_ref[...], 0.0)
        xc, shifted = _conv_taps(lxv, prev8, cw, cb_ref[...])
        r, ig, sp, a, mult = _lru_gates(xc, wa_ref[...], ba_ref[...], wx_ref[...], bx_ref[...], lam_v)
        hv = h_ref[...]
        drv = dr_ref[...]
        g, dg = _gelu_parts(lg_ref[...])
        dlg_ref[...] = drv * hv * dg
        dh = drv * g

        row = lax.broadcasted_iota(jnp.int32, (n, LRU_WIDTH), 0)
        sub = row & 7
        A = jnp.where(row < n - 1, pltpu.roll(a, n - 1, 0), 0.0)
        B = dh + jnp.where(row == n - 1, gc_ref[0:1, :], 0.0)
        for k in (1, 2, 4):
            a_s = jnp.where(sub < 8 - k, pltpu.roll(A, n - k, 0), 1.0)
            b_s = jnp.where(sub < 8 - k, pltpu.roll(B, n - k, 0), 0.0)
            B = B + A * b_s
            A = A * a_s
        carry = jnp.zeros((1, LRU_WIDTH), F32)
        groups = [None] * (n // 8)
        for g in reversed(range(n // 8)):
            gg = B[8 * g:8 * (g + 1)] + A[8 * g:8 * (g + 1)] * carry
            groups[g] = gg
            carry = gg[0:1]
        gs = jnp.concatenate(groups, axis=0)
        gc_ref[0:1, :] = a[0:1, :] * carry

        hprev8 = jnp.where(ir > 0, hp_ref[...], 0.0)
        h_prev = pltpu.roll(jnp.concatenate([hprev8, hv], axis=0), 1, 0)[8:]
        da = gs * h_prev
        ix = ig * xc
        dmult = gs * ix
        dig = gs * mult * xc
        dxc = gs * mult * ig
        dlog_a = da * a - dmult * (a * a) / mult
        dr_gate = dlog_a * (-LRU_C * sp)
        dza = dr_gate * r * (1.0 - r)
        dzx = dig * ig * (1.0 - ig)
        dzab = dza.astype(MXU_DTYPE)
        dzxb = dzx.astype(MXU_DTYPE)
        xcb = xc.astype(MXU_DTYPE)
        dwa_ref[...] += _tn(xcb, dzab)
        dwx_ref[...] += _tn(xcb, dzxb)
        dxc = dxc + _nt(dzab, wa_ref[...]) + _nt(dzxb, wx_ref[...])

        ds = jnp.concatenate([dxc, dxn_ref[...]], axis=0)
        dlx = cw[CONV_WIDTH - 1:CONV_WIDTH] * dxc
        for k in range(1, CONV_WIDTH):
            dlx = dlx + cw[CONV_WIDTH - 1 - k:CONV_WIDTH - k] * pltpu.roll(ds, n + 8 - k, 0)[:n]
        dlx_ref[...] = dlx
        dxn_ref[...] = dxc[0:8, :]

        colsum = lambda v: jnp.sum(v, axis=0, keepdims=True)
        small_ref[0:1, :] += colsum(dza)
        small_ref[1:2, :] += colsum(dzx)
        small_ref[2:3, :] += colsum(dlog_a * r) * (LRU_C * _sigmoid(-lam_v))
        small_ref[3:4, :] += colsum(dxc)
        for k in range(CONV_WIDTH):
            j = CONV_WIDTH - 1 - k
            small_ref[4 + j:5 + j, :] += colsum(dxc * shifted[k])

    tile = pl.BlockSpec((tc, LRU_WIDTH), lambda b, i: (b * nc + (nc - 1 - i), 0))
    prev = pl.BlockSpec((8, LRU_WIDTH), lambda b, i: (jnp.maximum((b * seq + (nc - 1 - i) * tc) // 8 - 1, 0), 0))
    vec = _const((1, LRU_WIDTH))
    mat = _const((LRU_WIDTH, LRU_WIDTH))
    return pl.pallas_call(
        body, name="lru_bwd", grid=(bl, nc),
        in_specs=[tile, tile, tile, prev, tile, prev, _const((CONV_WIDTH, LRU_WIDTH)), vec, mat, vec, mat, vec, vec],
        out_specs=[tile, tile, mat, mat, _const((8, LRU_WIDTH))],
        out_shape=[jax.ShapeDtypeStruct((T, LRU_WIDTH), F32), jax.ShapeDtypeStruct((T, LRU_WIDTH), F32),
                   jax.ShapeDtypeStruct((LRU_WIDTH, LRU_WIDTH), F32), jax.ShapeDtypeStruct((LRU_WIDTH, LRU_WIDTH), F32),
                   jax.ShapeDtypeStruct((8, LRU_WIDTH), F32)],
        scratch_shapes=[pltpu.VMEM((8, LRU_WIDTH), F32), pltpu.VMEM((8, LRU_WIDTH), F32)],
        compiler_params=_params(("arbitrary", "arbitrary")),
    )(drec, lg, h, h, lx, lx, cw, cb, wa, ba, wx, bx, lam)


def _attn_bwd(qn, kn, vb, dob, qt, dot_, lse, delta, frow, fstart, bl, seq, slabs, packs):
    tq = min(ATT_TILE, seq)
    nq = seq // tq
    T = bl * seq
    n, npk = len(slabs), len(packs)
    nx = n + npk

    def body(fs_ref, q_ref, k_ref, v_ref, do_ref, qt_ref, dot_ref, lse_ref, dl_ref, fr_ref, *rest):
        x_in, (dq_ref, dk_ref, dv_ref, df_ref), x_out = rest[:nx], rest[nx:nx + 4], rest[nx + 4:2 * nx + 4]
        send_sems, recv_sems, loc_sems, psend, precv = rest[2 * nx + 4:]
        b, p, j = pl.program_id(0), pl.program_id(1), pl.program_id(2)
        copies = _chip_copies(x_in[:n], x_out[:n], send_sems, recv_sems, loc_sems, scatter=True)
        copies += _device_copies(x_in[n:], x_out[n:], psend, precv, loc_sems, n)

        @pl.when((b == 0) & (p == 0) & (j == 0))
        def _():
            for cp in copies:
                cp.start()

        @pl.when(j == 0)
        def _():
            dq_ref[...] = jnp.zeros_like(dq_ref)

        lane = lax.broadcasted_iota(jnp.int32, (1, PAIR), 1)
        rows = lax.broadcasted_iota(jnp.int32, (tq, tq), 0)
        cols = lax.broadcasted_iota(jnp.int32, (tq, tq), 1)
        causal = cols <= rows
        kv = k_ref[...]
        vv = v_ref[...]
        hms = [(lane >= HEAD_DIM * hh) & (lane < HEAD_DIM * (hh + 1)) for hh in range(2)]
        srow = lax.broadcasted_iota(jnp.int32, (PAIR, 1), 0)
        hms_t = [(srow >= HEAD_DIM * hh) & (srow < HEAD_DIM * (hh + 1)) for hh in range(2)]
        khs = [jnp.where(hm, kv, jnp.zeros_like(kv)) for hm in hms]
        fks = [fr_ref[0, 0, hh:hh + 1, :] for hh in range(2)]
        bases = [((b * N_PAIR + p) * 2 + hh) * nq for hh in range(2)]

        def block(i, carry, masked):
            dk, dv, dfs = carry
            start = pl.multiple_of(i * tq, tq)
            qi = q_ref[pl.ds(start, tq), :]
            doi = do_ref[pl.ds(start, tq), :]
            qti = qt_ref[:, pl.ds(start, tq)]
            doti = dot_ref[:, pl.ds(start, tq)]
            dq = jnp.zeros((tq, PAIR), F32)
            new_dfs = []
            for hh in range(2):
                c0 = HEAD_DIM * hh
                qh = jnp.where(hms[hh], qi, jnp.zeros_like(qi))
                doh = jnp.where(hms[hh], doi, jnp.zeros_like(doi))
                s = _nt(qh, kv) - (fks[hh] - fs_ref[bases[hh] + i])
                if masked:
                    s = jnp.where(causal, s, NEG)
                pr = jnp.exp(s - lse_ref[pl.ds(start, tq), c0:c0 + 1])
                dp = _nt(doh, vv)
                ds = pr * (dp - dl_ref[pl.ds(start, tq), c0:c0 + 1])
                dsb = ds.astype(MXU_DTYPE)
                dv = dv + jnp.dot(jnp.where(hms_t[hh], doti, jnp.zeros_like(doti)), pr.astype(MXU_DTYPE),
                                  preferred_element_type=F32)
                dk = dk + jnp.dot(jnp.where(hms_t[hh], qti, jnp.zeros_like(qti)), dsb, preferred_element_type=F32)
                dq = dq + jnp.dot(dsb, khs[hh], preferred_element_type=F32)
                new_dfs.append(dfs[hh] - jnp.sum(ds, axis=0, keepdims=True))
            dq_ref[pl.ds(start, tq), :] += dq
            return dk, dv, tuple(new_dfs)

        zero = jnp.zeros((PAIR, tq), F32)
        carry = block(j, (zero, zero, (jnp.zeros((1, tq), F32), jnp.zeros((1, tq), F32))), True)
        dk, dv, dfs = lax.fori_loop(j + 1, nq, functools.partial(block, masked=False), carry)
        for hh in range(2):
            df_ref[0, 0, hh:hh + 1, :] = dfs[hh]
        dk_ref[...] = dk.T
        dv_ref[...] = dv.T

        @pl.when((b == bl - 1) & (p == N_PAIR - 1) & (j == nq - 1))
        def _():
            for cp in copies:
                cp.wait()

    blk = pl.BlockSpec((tq, PAIR), lambda b, p, j: (b * nq + j, p))
    full = pl.BlockSpec((seq, PAIR), lambda b, p, j: (b, p))
    fblk = pl.BlockSpec((1, 1, 2, tq), lambda b, p, j: (b, p, 0, j))
    full_t = pl.BlockSpec((PAIR, seq), lambda b, p, j: (p, b))
    hbm = pl.BlockSpec(memory_space=pl.ANY)
    return pl.pallas_call(
        body, name="attn_bwd", grid=(bl, N_PAIR, nq),
        in_specs=[pl.BlockSpec(memory_space=pltpu.SMEM), full, blk, blk, full, full_t, full_t, full, full, fblk]
        + [hbm] * nx,
        out_specs=[full, blk, blk, fblk] + [hbm] * nx,
        out_shape=[jax.ShapeDtypeStruct((T, ATT_WIDTH), F32), jax.ShapeDtypeStruct((T, ATT_WIDTH), F32),
                   jax.ShapeDtypeStruct((T, ATT_WIDTH), F32), jax.ShapeDtypeStruct((bl, N_PAIR, 2, seq), F32)]
        + [jax.ShapeDtypeStruct(s.shape, s.dtype) for s in slabs]
        + [jax.ShapeDtypeStruct((8,) + p.shape, p.dtype) for p in packs],
        scratch_shapes=[pltpu.SemaphoreType.DMA((3 * n,)), pltpu.SemaphoreType.DMA((3 * n,)),
                        pltpu.SemaphoreType.DMA((nx,)),
                        pltpu.SemaphoreType.DMA((7 * npk,)), pltpu.SemaphoreType.DMA((7 * npk,))],
        compiler_params=_params(("arbitrary", "arbitrary", "arbitrary"), VMEM_LARGE),
    )(fstart, qn, kn, vb, dob, qt, dot_, lse, delta, frow, *slabs, *packs)


def _forget_bwd(dfcol, f2d, bf, bl, seq):
    def body(d_ref, z_ref, b_ref, o_ref, db_ref):
        @pl.when(pl.program_id(0) == 0)
        def _():
            db_ref[...] = jnp.zeros_like(db_ref)

        d = d_ref[...]
        row = lax.broadcasted_iota(jnp.int32, (seq, F_PAD), 0)
        k = 1
        while k < seq:
            d = d + jnp.where(row < seq - k, pltpu.roll(d, seq - k, 0), 0.0)
            k *= 2
        dz = d * _sigmoid(-(z_ref[...] + b_ref[...]))
        o_ref[...] = dz
        db_ref[...] += jnp.sum(dz, axis=0, keepdims=True)

    blk = pl.BlockSpec((seq, F_PAD), lambda b: (b, 0))
    return pl.pallas_call(
        body, name="forget_bwd", grid=(bl,),
        in_specs=[blk, blk, _const((1, F_PAD))],
        out_specs=[blk, _const((1, F_PAD))],
        out_shape=[jax.ShapeDtypeStruct(f2d.shape, F32), jax.ShapeDtypeStruct((1, F_PAD), F32)],
        compiler_params=_params(("arbitrary",)),
    )(dfcol, f2d, bf)


def _dproj(dq, dk, dv, qkv, df, dlx, dlg, gq2, gk2):
    T = dq.shape[0]
    tm = TOKEN_TILE

    def body(dq_ref, dk_ref, dv_ref, qkv_ref, df_ref, dlx_ref, dlg_ref, gq_ref, gk_ref, dp_ref, dgq_ref, dgk_ref):
        @pl.when(pl.program_id(0) == 0)
        def _():
            dgq_ref[...] = jnp.zeros_like(dgq_ref)
            dgk_ref[...] = jnp.zeros_like(dgk_ref)

        lo = _lo_mask()

        def head_norm_bwd(t, g2, dy):
            rr = lax.rsqrt(_half_sums(t * t, lo) * (1.0 / HEAD_DIM) + NORM_EPS)
            th = t * rr
            dth = dy * g2
            mm = _half_sums(dth * th, lo) * (1.0 / HEAD_DIM)
            return rr * (dth - th * mm), jnp.sum(dy * th, axis=0, keepdims=True)

        dgq = jnp.zeros((1, PAIR), F32)
        dgk = jnp.zeros((1, PAIR), F32)
        for p in range(N_PAIR):
            cq = slice(PAIR * p, PAIR * (p + 1))
            ck = slice(ATT_WIDTH + PAIR * p, ATT_WIDTH + PAIR * (p + 1))
            dqp, g_ = head_norm_bwd(qkv_ref[:, cq], gq_ref[...], dq_ref[:, cq] * QK_SCALE)
            dgq = dgq + g_
            dp_ref[:, cq] = dqp.astype(MXU_DTYPE)
            dkp, g_ = head_norm_bwd(qkv_ref[:, ck], gk_ref[...], dk_ref[:, cq])
            dgk = dgk + g_
            dp_ref[:, ck] = dkp.astype(MXU_DTYPE)
        dgq_ref[...] += dgq
        dgk_ref[...] += dgk
        f0 = 3 * ATT_WIDTH
        dp_ref[:, 2 * ATT_WIDTH:f0] = dv_ref[...].astype(MXU_DTYPE)
        dp_ref[:, f0:f0 + F_PAD] = df_ref[...].astype(MXU_DTYPE)
        dp_ref[:, f0 + F_PAD:f0 + F_PAD + LRU_WIDTH] = dlx_ref[...].astype(MXU_DTYPE)
        dp_ref[:, f0 + F_PAD + LRU_WIDTH:] = dlg_ref[...].astype(MXU_DTYPE)

    row = lambda w: pl.BlockSpec((tm, w), lambda i: (i, 0))
    return pl.pallas_call(
        body, name="dproj", grid=(T // tm,),
        in_specs=[row(ATT_WIDTH), row(ATT_WIDTH), row(ATT_WIDTH), row(3 * ATT_WIDTH), row(F_PAD), row(LRU_WIDTH),
                  row(LRU_WIDTH), _const((1, PAIR)), _const((1, PAIR))],
        out_specs=[row(N_CAT), _const((1, PAIR)), _const((1, PAIR))],
        out_shape=[jax.ShapeDtypeStruct((T, N_CAT), MXU_DTYPE), jax.ShapeDtypeStruct((1, PAIR), F32),
                   jax.ShapeDtypeStruct((1, PAIR), F32)],
        compiler_params=_params(("arbitrary",)),
    )(dq, dk, dv, qkv, df, dlx, dlg, gq2, gk2)


def _inproj_bwd(dproj, x2d, dx2, g1, wcat, slabs):
    T = x2d.shape[0]
    tm = TOKEN_TILE
    n = len(slabs)
    steps = T // tm

    def body(dp_ref, x_ref, dx2_ref, g1_ref, w_ref, *rest):
        s_in, (gx_ref, dg1_ref), s_out, sems = rest[:n], rest[n:n + 2], rest[n + 2:2 * n + 2], rest[2 * n + 2:]
        i = pl.program_id(0)
        copies = _chip_copies(s_in, s_out, *sems, scatter=True)

        @pl.when(i == 0)
        def _():
            dg1_ref[...] = jnp.zeros_like(dg1_ref)
            for cp in copies:
                cp.start()

        dh = jnp.dot(dp_ref[...], w_ref[...], preferred_element_type=F32)
        x = x_ref[...]
        r = lax.rsqrt(jnp.mean(x * x, axis=-1, keepdims=True) + NORM_EPS)
        xh = x * r
        dg1_ref[...] += jnp.sum(dh * xh, axis=0, keepdims=True)
        dxh = dh * g1_ref[...]
        gx_ref[...] = dx2_ref[...] + r * (dxh - xh * jnp.mean(dxh * xh, axis=-1, keepdims=True))

        @pl.when(i == steps - 1)
        def _():
            for cp in copies:
                cp.wait()

    row = lambda w: pl.BlockSpec((tm, w), lambda i: (i, 0))
    hbm = pl.BlockSpec(memory_space=pl.ANY)
    return pl.pallas_call(
        body, name="inproj_bwd", grid=(steps,),
        in_specs=[row(N_CAT), row(D_MODEL), row(D_MODEL), _const((1, D_MODEL)), _const((N_CAT, D_MODEL))] + [hbm] * n,
        out_specs=[row(D_MODEL), _const((1, D_MODEL))] + [hbm] * n,
        out_shape=[jax.ShapeDtypeStruct((T, D_MODEL), F32), jax.ShapeDtypeStruct((1, D_MODEL), F32)]
        + [jax.ShapeDtypeStruct(a.shape, a.dtype) for a in slabs],
        scratch_shapes=_chip_sems(n),
        compiler_params=_params(("arbitrary",), VMEM_LARGE),
    )(dproj, x2d, dx2, g1, wcat, *slabs)


ELEMENTWISE_COLS = 256


def _sum_slabs(recvs):
    n = len(recvs)
    cols = recvs[0].shape[2]
    cb = ELEMENTWISE_COLS

    def body(*refs):
        for r_ref, o_ref in zip(refs[:n], refs[n:]):
            part = [r_ref[s].astype(F32) for s in range(4)]
            o_ref[...] = ((part[0] + part[1]) + part[2]) + part[3]

    return pl.pallas_call(
        body, name="sum_slabs", grid=(cols // cb,),
        in_specs=[pl.BlockSpec((4, r.shape[1], cb), lambda i: (0, 0, i)) for r in recvs],
        out_specs=[pl.BlockSpec((r.shape[1], cb), lambda i: (0, i)) for r in recvs],
        out_shape=[jax.ShapeDtypeStruct(r.shape[1:], F32) for r in recvs],
        compiler_params=_params(("parallel",), VMEM_LARGE),
    )(*recvs)


def _adamw_math(w, g, m, v):
    m = ADAM_B1 * m + (1.0 - ADAM_B1) * g
    v = ADAM_B2 * v + (1.0 - ADAM_B2) * (g * g)
    m_hat = m / (1.0 - ADAM_B1 ** ADAM_STEP)
    v_hat = v / (1.0 - ADAM_B2 ** ADAM_STEP)
    delta = -ADAM_LR * (m_hat / (jnp.sqrt(v_hat) + ADAM_EPS) + ADAM_WD * w)
    return delta, m, v


def _adamw_pairs(groups):
    n = len(groups)
    cols = groups[0][0].shape[1]
    cb = ELEMENTWISE_COLS // 2

    def body(*refs):
        ins, outs = refs[:5 * n], refs[5 * n:]
        for w in range(n):
            a_ref, b_ref, w_ref, m_ref, v_ref = ins[5 * w:5 * w + 5]
            g_ref, d_ref, nm_ref, nv_ref = outs[4 * w:4 * w + 4]
            g = a_ref[...] + b_ref[...]
            g_ref[...] = g
            d_ref[...], nm_ref[...], nv_ref[...] = _adamw_math(w_ref[...], g, m_ref[...], v_ref[...])

    blk = lambda rows: pl.BlockSpec((rows, cb), lambda i: (0, i))
    res = pl.pallas_call(
        body, name="adamw_big", grid=(cols // cb,),
        in_specs=[blk(g[0].shape[0]) for g in groups for _ in range(5)],
        out_specs=[blk(g[0].shape[0]) for g in groups for _ in range(4)],
        out_shape=[jax.ShapeDtypeStruct(g[0].shape, F32) for g in groups for _ in range(4)],
        compiler_params=_params(("parallel",), VMEM_LARGE),
    )(*[a for g in groups for a in g])
    return [res[4 * w:4 * w + 4] for w in range(n)]


VEC_ROW = {"norm1_g": 0, "norm2_g": 1, "attn_out_g": 2, "lru_out_g": 3, "q_norm_g": 4, "k_norm_g": 5, "b_f": 6,
           "b_a": 8, "b_x": 9, "lam": 10, "conv_b": 11}
LOSS_ROW, CONV_W_ROW, PACK_ROWS = 7, 12, 16
SMALL = list(VEC_ROW) + ["conv_w", "w_a", "w_x"]


def _pack_small(dg1, dg2, dga, dgr, dgq, dgk, dbf, sq_err, lru_small):
    def body(dg1_ref, dg2_ref, dga_ref, dgr_ref, dgq_ref, dgk_ref, dbf_ref, err_ref, lru_ref, v_ref):
        v_ref[...] = jnp.zeros_like(v_ref)
        v_ref[0:1, :] = dg1_ref[...]
        v_ref[1:2, :] = dg2_ref[...]
        v_ref[2:3, 0:ATT_WIDTH] = dga_ref[...]
        v_ref[3:4, 0:LRU_WIDTH] = dgr_ref[...]
        for row, ref in ((4, dgq_ref), (5, dgk_ref)):
            g = ref[...]
            v_ref[row:row + 1, 0:PAIR] = g + pltpu.roll(g, HEAD_DIM, 1)
        v_ref[6:7, 0:F_PAD] = dbf_ref[...]
        v_ref[LOSS_ROW:LOSS_ROW + 1, 0:128] = err_ref[0:1, :] * (0.5 / D_MODEL)
        v_ref[8:16, 0:LRU_WIDTH] = lru_ref[...]

    ins = [dg1, dg2, dga, dgr, dgq, dgk, dbf, sq_err, lru_small]
    return pl.pallas_call(
        body, name="pack_small", grid=(1,),
        in_specs=[_const(a.shape) for a in ins], out_specs=_const((PACK_ROWS, D_MODEL)),
        out_shape=jax.ShapeDtypeStruct((PACK_ROWS, D_MODEL), F32),
        compiler_params=_params(("arbitrary",)),
    )(*ins)


def _diag_blocks(dwa_bd, dwx_bd):
    blk = LRU_WIDTH // LRU_BLOCKS

    def body(wa_ref, wx_ref, oa_ref, ox_ref):
        for src, dst in ((wa_ref, oa_ref), (wx_ref, ox_ref)):
            for nb in range(LRU_BLOCKS):
                tile = src[blk * nb:blk * (nb + 1), PAIR * (nb // 2):PAIR * (nb // 2 + 1)]
                if nb % 2:
                    tile = pltpu.roll(tile, blk, 1)
                dst[nb] = tile[:, 0:blk]

    out = jax.ShapeDtypeStruct((LRU_BLOCKS, blk, blk), F32)
    return pl.pallas_call(
        body, name="diag_blocks", grid=(1,),
        in_specs=[_const(dwa_bd.shape)] * 2, out_specs=[_const(out.shape)] * 2, out_shape=[out, out],
        compiler_params=_params(("arbitrary",)),
    )(dwa_bd, dwx_bd)


def _adamw_small(recv_v, recv_a, recv_x, params):
    names = list(params)
    flat = [a for n in names for a in params[n]]

    def body(rv_ref, ra_ref, rx_ref, *refs):
        ins, loss_ref, outs = refs[:len(flat)], refs[len(flat)], refs[len(flat) + 1:]
        x, y = lax.axis_index("x"), lax.axis_index("y")
        me = 2 * x + y

        def total(r):
            acc = r[0]
            for d in range(1, 8):
                acc = acc + r[d]
            return acc

        gv, ga, gx = total(rv_ref), total(ra_ref), total(rx_ref)
        loss_ref[...] = gv[LOSS_ROW:LOSS_ROW + 1, 0:128]
        for i, n in enumerate(names):
            w_ref, m_ref, v_ref = ins[3 * i:3 * i + 3]
            g_ref, d_ref, nm_ref, nv_ref = outs[4 * i:4 * i + 4]
            if n in VEC_ROW:
                g = gv[VEC_ROW[n]:VEC_ROW[n] + 1, 0:w_ref.shape[1]]
                w, m, v = w_ref[...], m_ref[...], v_ref[...]
            else:
                if n == "conv_w":
                    full = gv[CONV_W_ROW:CONV_W_ROW + CONV_WIDTH, 0:LRU_WIDTH]
                    width = LRU_WIDTH // 4
                    g = jnp.zeros((CONV_WIDTH, width), F32)
                    for s in range(4):
                        g = jnp.where(me == s, full[:, width * s:width * (s + 1)], g)
                else:
                    g = ga if n == "w_a" else gx
                w, m, v = w_ref[0], m_ref[0], v_ref[0]
            d, nm, nv = _adamw_math(w, g, m, v)
            for ref, val in ((g_ref, g), (d_ref, d), (nm_ref, nm), (nv_ref, nv)):
                if n in VEC_ROW:
                    ref[...] = val
                else:
                    ref[0] = val

    out_shape = [jax.ShapeDtypeStruct((1, 128), F32)] + [jax.ShapeDtypeStruct(params[n][0].shape, F32)
                                                          for n in names for _ in range(4)]
    res = pl.pallas_call(
        body, name="adamw_small", grid=(1,),
        in_specs=[_const(a.shape) for a in (recv_v, recv_a, recv_x, *flat)],
        out_specs=[_const(o.shape) for o in out_shape], out_shape=out_shape,
        compiler_params=_params(("arbitrary",)),
    )(recv_v, recv_a, recv_x, *flat)
    return res[0], {n: tuple(res[1 + 4 * i:5 + 4 * i]) for i, n in enumerate(names)}


def _cat_shards(g, pad_at=None, pad=0):
    _, rows, w = g.shape
    pieces = []
    for s in range(4):
        lo, hi = s * w, (s + 1) * w
        if pad_at is not None and lo < pad_at <= hi:
            pieces += [g[s][:, :pad_at - lo], jnp.zeros((rows, pad), g.dtype)]
            if pad_at < hi:
                pieces.append(g[s][:, pad_at - lo:])
        else:
            pieces.append(g[s])
    return jnp.concatenate(pieces, axis=1)


def _block_diag(w):
    eye = jnp.eye(LRU_BLOCKS, dtype=w.dtype)
    return (w[:, :, None, :] * eye[:, None, :, None]).reshape(LRU_WIDTH, LRU_WIDTH)


def kernel(x, norm1_g, w_in, q_norm_g, k_norm_g, b_f, conv_w, conv_b, w_a, b_a, w_x, b_x, lam, attn_out_g, lru_out_g, w_out, norm2_g, w_gate, w_up, w_down, loss_target, m_norm1_g, m_w_in, m_q_norm_g, m_k_norm_g, m_b_f, m_conv_w, m_conv_b, m_w_a, m_b_a, m_w_x, m_b_x, m_lam, m_attn_out_g, m_lru_out_g, m_w_out, m_norm2_g, m_w_gate, m_w_up, m_w_down, v_norm1_g, v_w_in, v_q_norm_g, v_k_norm_g, v_b_f, v_conv_w, v_conv_b, v_w_a, v_b_a, v_w_x, v_b_x, v_lam, v_attn_out_g, v_lru_out_g, v_w_out, v_norm2_g, v_w_gate, v_w_up, v_w_down):
    args = dict(locals())
    bl, seq, _ = x.shape
    T = bl * seq
    tq = min(ATT_TILE, seq)
    nq = seq // tq
    dff = w_gate.shape[2] * 4

    def transposed(name):
        return name.endswith(("w_in", "w_gate", "w_up"))

    def shard2d(name):
        return jnp.swapaxes(args[name], 1, 2)[0] if transposed(name) else args[name][0]

    x2d = x.reshape(T, D_MODEL)
    target2d = loss_target.reshape(T, D_MODEL)
    h1b, g_in, g_cw = _norm1_and_gather(x2d, norm1_g, shard2d("w_in").astype(MXU_DTYPE), conv_w[0])
    later_shards = [shard2d(n).astype(MXU_DTYPE) for n in ("w_out", "w_gate", "w_up", "w_down")]
    f0 = 3 * ATT_WIDTH
    w_in_t = g_in.reshape(-1, D_MODEL)
    wcat = jnp.concatenate([w_in_t[:f0 + HEADS], jnp.zeros((F_PAD - HEADS, D_MODEL), w_in_t.dtype),
                            w_in_t[f0 + HEADS:]], axis=0)
    cw_full = _cat_shards(g_cw)
    wa_bd = _block_diag(w_a[0]).astype(MXU_DTYPE)
    wx_bd = _block_diag(w_x[0]).astype(MXU_DTYPE)
    gq2 = jnp.tile(q_norm_g, (1, 2))
    gk2 = jnp.tile(k_norm_g, (1, 2))
    bf_pad = jnp.pad(b_f, ((0, 0), (0, F_PAD - HEADS)))

    qkv, qn, qn_t, kn, vb, f2d, lx, lg = _inproj(h1b, wcat, gq2, gk2)
    fcol = _forget_cumsum(f2d, bf_pad, bl, seq)
    frow = jnp.transpose(fcol.reshape(bl, seq, F_PAD)[:, :, :HEADS], (0, 2, 1)).reshape(bl, N_PAIR, 2, seq)
    fstart = frow[:, :, :, ::tq].reshape(-1)
    att, lse, g_out, g_gate, g_up, g_down = _attn_fwd(qn, kn, vb, frow, fstart, bl, seq, later_shards)
    wout_full = g_out.reshape(D_MODEL, D_MODEL)
    wg_full, wu_full = g_gate.reshape(dff, D_MODEL), g_up.reshape(dff, D_MODEL)
    wd_full = g_down.reshape(dff, D_MODEL)
    h, rec = _lru_fwd(lx, lg, cw_full, conv_b, wa_bd, b_a, wx_bd, b_x, lam, bl, seq)
    x2 = _outproj(x2d, att, rec, attn_out_g, lru_out_g, wout_full)
    gt, up, dy, sq_err = _mlp_fwd(x2, norm2_g, wg_full, wu_full, wd_full, target2d)

    dx2, dx2b, dgtb, dupb, actb, h2b, dyb, dg2 = _mlp_bwd(dy, x2, gt, up, norm2_g, wg_full, wu_full, wd_full)
    dw_down = _matmul_tn(actb, dyb, D_MODEL, "dw_down")
    dw_gate = _matmul_tn(dgtb, h2b, D_MODEL, "dw_gate")
    dw_up = _matmul_tn(dupb, h2b, D_MODEL, "dw_up")
    dattb, dattb_t, delta, drec, mixb, dga, dgr = _outproj_bwd(dx2b, att, rec, attn_out_g, lru_out_g, wout_full)
    dw_out = _matmul_tn(mixb, dx2b, D_MODEL, "dw_out")
    dlx, dlg, dwa_bd, dwx_bd, lru_small = _lru_bwd(drec, lg, h, lx, cw_full, conv_b, wa_bd, b_a, wx_bd, b_x, lam, bl, seq)
    early_slabs = [dw_out.reshape(4, D_MODEL // 4, D_MODEL), dw_gate.reshape(4, dff // 4, D_MODEL),
                   dw_up.reshape(4, dff // 4, D_MODEL), dw_down.reshape(4, dff // 4, D_MODEL)]
    pack_a, pack_x = _diag_blocks(dwa_bd, dwx_bd)
    dq, dk, dv, dfrow, *recv_early = _attn_bwd(qn, kn, vb, dattb, qn_t, dattb_t, lse, delta, frow, fstart, bl, seq,
                                               early_slabs, [pack_a, pack_x])
    recv_early, (recv_a, recv_x) = recv_early[:4], recv_early[4:]
    dfcol = jnp.pad(jnp.transpose(dfrow.reshape(bl, HEADS, seq), (0, 2, 1)), ((0, 0), (0, 0), (0, F_PAD - HEADS)))
    df, dbf = _forget_bwd(dfcol.reshape(T, F_PAD), f2d, bf_pad, bl, seq)
    dprojb, dgq, dgk = _dproj(dq, dk, dv, qkv, df, dlx, dlg, gq2, gk2)
    dwcat = _matmul_tn(dprojb, h1b, D_MODEL, "dw_in")
    dw_in_slabs = jnp.concatenate([dwcat[:f0 + HEADS], dwcat[f0 + F_PAD:]], axis=0).astype(jnp.bfloat16).reshape(
        4, -1, D_MODEL)
    grad_x, dg1, recv_in = _inproj_bwd(dprojb, x2d, dx2, norm1_g, wcat, [dw_in_slabs])

    pack_v = _pack_small(dg1, dg2, dga, dgr, dgq, dgk, dbf, sq_err, lru_small)
    recv = [recv_in] + recv_early
    big = ["w_in", "w_out", "w_gate", "w_up", "w_down"]
    part = _sum_slabs(recv)
    *theirs, recv_v = _swap_with_sibling(part, [pack_v])
    results = _adamw_pairs([(a, b_, shard2d(n), shard2d("m_" + n), shard2d("v_" + n))
                            for n, a, b_ in zip(big, part, theirs)])
    out = {}
    for n, res in zip(big, results):
        out[n] = tuple(jnp.swapaxes(r[None], 1, 2) if transposed(n) else r[None] for r in res)
    loss_row, small_out = _adamw_small(recv_v, recv_a, recv_x,
                                       {n: (args[n], args["m_" + n], args["v_" + n]) for n in SMALL})
    out.update(small_out)
    loss = loss_row[0, 0]

    order = ["norm1_g", "w_in", "q_norm_g", "k_norm_g", "b_f", "conv_w", "conv_b", "w_a", "b_a", "w_x", "b_x", "lam",
             "attn_out_g", "lru_out_g", "w_out", "norm2_g", "w_gate", "w_up", "w_down"]
    return (loss, grad_x.reshape(bl, seq, D_MODEL), *[out[n][0] for n in order], *[out[n][1] for n in order],
            *[out[n][2] for n in order], *[out[n][3] for n in order])
```

```python
import functools
import math

import jax
import jax.numpy as jnp
from jax import lax
from jax.experimental import pallas as pl
from jax.experimental.pallas import tpu as pltpu

F32 = jnp.float32
MXU_DTYPE = jnp.bfloat16
MESH = pl.DeviceIdType.MESH

D_MODEL = 1024
ATT_WIDTH = 512
LRU_WIDTH = 512
HEADS = 8
HEAD_DIM = 64
PAIR = 2 * HEAD_DIM
N_PAIR = HEADS // 2
LRU_BLOCKS = 8
CONV_WIDTH = 4
LRU_C = 8.0
NORM_EPS = 1e-6
QK_SCALE = 1.0 / math.sqrt(HEAD_DIM)
F_PAD = 128
N_CAT = 3 * ATT_WIDTH + F_PAD + 2 * LRU_WIDTH
NEG = -1e30

ADAM_LR, ADAM_B1, ADAM_B2, ADAM_EPS, ADAM_WD, ADAM_STEP = 0.001, 0.9, 0.999, 1e-08, 0.01, 10

TOKEN_TILE = 256
ATT_TILE = 512
LRU_TILE = 256
VMEM_SMALL = 32 * 1024 * 1024
VMEM_LARGE = 56 * 1024 * 1024


def _params(sem, vmem=VMEM_SMALL):
    return pltpu.CompilerParams(dimension_semantics=sem, vmem_limit_bytes=vmem)


def _const(shape):
    nd = len(shape)
    return pl.BlockSpec(shape, lambda *_: (0,) * nd)


def _sigmoid(x):
    return 1.0 / (1.0 + jnp.exp(-x))


def _nt(a, b):
    return lax.dot_general(a, b, (((1,), (1,)), ((), ())), preferred_element_type=F32)


def _tn(a, b):
    return lax.dot_general(a, b, (((0,), (0,)), ((), ())), preferred_element_type=F32)


def _half_sums(t, lo):
    s_lo = jnp.sum(jnp.where(lo, t, 0.0), axis=-1, keepdims=True)
    s_hi = jnp.sum(jnp.where(lo, 0.0, t), axis=-1, keepdims=True)
    return jnp.where(lo, s_lo, s_hi)


def _lo_mask():
    return lax.broadcasted_iota(jnp.int32, (1, PAIR), 1) < HEAD_DIM


def _other_chips(x, y):
    return [(1 - x, y), (x, 1 - y), (1 - x, 1 - y)]


def _chip_copies(ins, outs, send_sems, recv_sems, loc_sems, scatter):
    x, y, c = lax.axis_index("x"), lax.axis_index("y"), lax.axis_index("c")
    me = 2 * x + y
    copies = []
    for w in range(len(ins)):
        copies.append(pltpu.make_async_copy(ins[w].at[me] if scatter else ins[w], outs[w].at[me], loc_sems.at[w]))
        for k, (cx, cy) in enumerate(_other_chips(x, y)):
            copies.append(pltpu.make_async_remote_copy(
                src_ref=ins[w].at[2 * cx + cy] if scatter else ins[w], dst_ref=outs[w].at[me],
                send_sem=send_sems.at[3 * w + k], recv_sem=recv_sems.at[3 * w + k],
                device_id=(cx, cy, c), device_id_type=MESH))
    return copies


def _chip_sems(n):
    return [pltpu.SemaphoreType.DMA((3 * n,)), pltpu.SemaphoreType.DMA((3 * n,)), pltpu.SemaphoreType.DMA((n,))]


def _norm1_and_gather(x2d, g1, shard, small):
    T = x2d.shape[0]
    tm = 2 * TOKEN_TILE
    steps = T // tm
    half = shard.shape[1] // 2

    def body(x_ref, g1_ref, w_ref, s_ref, h_ref, ow_ref, os_ref, ici_send, ici_recv, d2d_send, d2d_recv, sm_send,
             sm_recv, loc_sems):
        i = pl.program_id(0)
        x, y, c = lax.axis_index("x"), lax.axis_index("y"), lax.axis_index("c")
        me = 2 * x + y
        mine = pl.ds(pl.multiple_of(c * half, half), half)
        local = [pltpu.make_async_copy(w_ref, ow_ref.at[me], loc_sems.at[0]),
                 pltpu.make_async_copy(s_ref, os_ref.at[me], loc_sems.at[1])]
        fetch, little, forward = [], [], []
        for k, (cx, cy) in enumerate(_other_chips(x, y)):
            src_chip = 2 * cx + cy
            fetch.append(pltpu.make_async_remote_copy(
                src_ref=w_ref.at[:, mine], dst_ref=ow_ref.at[me, :, mine], send_sem=ici_send.at[k],
                recv_sem=ici_recv.at[k], device_id=(cx, cy, c), device_id_type=MESH))
            little.append(pltpu.make_async_remote_copy(
                src_ref=s_ref, dst_ref=os_ref.at[me], send_sem=sm_send.at[k], recv_sem=sm_recv.at[k],
                device_id=(cx, cy, c), device_id_type=MESH))
            forward.append(pltpu.make_async_remote_copy(
                src_ref=ow_ref.at[src_chip, :, mine], dst_ref=ow_ref.at[src_chip, :, mine], send_sem=d2d_send.at[k],
                recv_sem=d2d_recv.at[k], device_id=(x, y, 1 - c), device_id_type=MESH))

        @pl.when(i == 0)
        def _():
            for cp in local + fetch + little:
                cp.start()

        xv = x_ref[...]
        r = lax.rsqrt(jnp.mean(xv * xv, axis=-1, keepdims=True) + NORM_EPS)
        h_ref[...] = (xv * r * g1_ref[...]).astype(MXU_DTYPE)

        @pl.when(i == steps - 1)
        def _():
            for k in range(3):
                fetch[k].wait_recv()
                forward[k].start()
            for cp in fetch:
                cp.wait_send()
            for cp in little + forward + local:
                cp.wait()

    hbm = pl.BlockSpec(memory_space=pl.ANY)
    return pl.pallas_call(
        body, name="norm1_gather", grid=(steps,),
        out_shape=[jax.ShapeDtypeStruct((T, D_MODEL), MXU_DTYPE),
                   jax.ShapeDtypeStruct((4,) + shard.shape, shard.dtype),
                   jax.ShapeDtypeStruct((4,) + small.shape, small.dtype)],
        in_specs=[pl.BlockSpec((tm, D_MODEL), lambda i: (i, 0)), _const((1, D_MODEL)), hbm, hbm],
        out_specs=[pl.BlockSpec((tm, D_MODEL), lambda i: (i, 0)), hbm, hbm],
        scratch_shapes=[pltpu.SemaphoreType.DMA((3,))] * 6 + [pltpu.SemaphoreType.DMA((2,))],
        compiler_params=_params(("arbitrary",)),
    )(x2d, g1, shard, small)


def _device_copies(packs_in, packs_out, psend, precv, loc_sems, loc_base):
    x, y, c = lax.axis_index("x"), lax.axis_index("y"), lax.axis_index("c")
    dev = 4 * x + 2 * y + c
    copies = []
    for j in range(len(packs_in)):
        copies.append(pltpu.make_async_copy(packs_in[j], packs_out[j].at[dev], loc_sems.at[loc_base + j]))
        for k in range(1, 8):
            fx, fy, fc = (k >> 2) & 1, (k >> 1) & 1, k & 1
            tx = (1 - x) if fx else x
            ty = (1 - y) if fy else y
            tc = (1 - c) if fc else c
            copies.append(pltpu.make_async_remote_copy(
                src_ref=packs_in[j], dst_ref=packs_out[j].at[dev],
                send_sem=psend.at[7 * j + k - 1], recv_sem=precv.at[7 * j + k - 1],
                device_id=(tx, ty, tc), device_id_type=MESH))
    return copies


def _swap_with_sibling(arrs, packs):
    n, npk = len(arrs), len(packs)

    def body(*refs):
        ins, pack_in = refs[:n], refs[n:n + npk]
        outs, pack_out = refs[n + npk:2 * n + npk], refs[2 * n + npk:2 * (n + npk)]
        send_sems, recv_sems, loc_sems, psend, precv = refs[2 * (n + npk):]
        x, y, c = lax.axis_index("x"), lax.axis_index("y"), lax.axis_index("c")
        copies = [pltpu.make_async_remote_copy(
            src_ref=ins[w], dst_ref=outs[w], send_sem=send_sems.at[w], recv_sem=recv_sems.at[w],
            device_id=(x, y, 1 - c), device_id_type=MESH) for w in range(n)]
        copies += _device_copies(pack_in, pack_out, psend, precv, loc_sems, 0)
        for cp in copies:
            cp.start()
        for cp in copies:
            cp.wait()

    return pl.pallas_call(
        body, name="swap_sibling",
        out_shape=[jax.ShapeDtypeStruct(a.shape, a.dtype) for a in arrs]
        + [jax.ShapeDtypeStruct((8,) + p.shape, p.dtype) for p in packs],
        in_specs=[pl.BlockSpec(memory_space=pl.ANY)] * (n + npk),
        out_specs=[pl.BlockSpec(memory_space=pl.ANY)] * (n + npk),
        scratch_shapes=[pltpu.SemaphoreType.DMA((n,)), pltpu.SemaphoreType.DMA((n,)), pltpu.SemaphoreType.DMA((npk,)),
                        pltpu.SemaphoreType.DMA((7 * npk,)), pltpu.SemaphoreType.DMA((7 * npk,))],
    )(*arrs, *packs)


def _head_norm(t, g2, lo):
    rr = lax.rsqrt(_half_sums(t * t, lo) * (1.0 / HEAD_DIM) + NORM_EPS)
    return t * rr * g2


def _inproj(h1, wcat, gq2, gk2):
    T = h1.shape[0]
    tm = TOKEN_TILE

    def body(h_ref, w_ref, gq_ref, gk_ref, qkv_ref, qn_ref, qt_ref, kn_ref, vb_ref, f_ref, lx_ref, lg_ref):
        proj = _nt(h_ref[...], w_ref[...])
        qkv_ref[...] = proj[:, :2 * ATT_WIDTH]
        lo = _lo_mask()
        for p in range(N_PAIR):
            cols = slice(PAIR * p, PAIR * (p + 1))
            q = proj[:, PAIR * p:PAIR * (p + 1)]
            k = proj[:, ATT_WIDTH + PAIR * p:ATT_WIDTH + PAIR * (p + 1)]
            qs = _head_norm(q, gq_ref[...], lo) * QK_SCALE
            qn_ref[:, cols] = qs.astype(MXU_DTYPE)
            qt_ref[cols, :] = qs.T.astype(MXU_DTYPE)
            kn_ref[:, cols] = _head_norm(k, gk_ref[...], lo).astype(MXU_DTYPE)
        vb_ref[...] = proj[:, 2 * ATT_WIDTH:3 * ATT_WIDTH].astype(MXU_DTYPE)
        f0 = 3 * ATT_WIDTH
        f_ref[...] = proj[:, f0:f0 + F_PAD]
        lx_ref[...] = proj[:, f0 + F_PAD:f0 + F_PAD + LRU_WIDTH]
        lg_ref[...] = proj[:, f0 + F_PAD + LRU_WIDTH:]

    row = lambda w: pl.BlockSpec((tm, w), lambda i: (i, 0))
    return pl.pallas_call(
        body, name="inproj", grid=(T // tm,),
        in_specs=[row(D_MODEL), _const((N_CAT, D_MODEL)), _const((1, PAIR)), _const((1, PAIR))],
        out_specs=[row(2 * ATT_WIDTH), row(ATT_WIDTH), pl.BlockSpec((ATT_WIDTH, tm), lambda i: (0, i)), row(ATT_WIDTH),
                   row(ATT_WIDTH), row(F_PAD), row(LRU_WIDTH), row(LRU_WIDTH)],
        out_shape=[jax.ShapeDtypeStruct((T, 2 * ATT_WIDTH), F32),
                   jax.ShapeDtypeStruct((T, ATT_WIDTH), MXU_DTYPE), jax.ShapeDtypeStruct((ATT_WIDTH, T), MXU_DTYPE),
                   jax.ShapeDtypeStruct((T, ATT_WIDTH), MXU_DTYPE),
                   jax.ShapeDtypeStruct((T, ATT_WIDTH), MXU_DTYPE), jax.ShapeDtypeStruct((T, F_PAD), F32),
                   jax.ShapeDtypeStruct((T, LRU_WIDTH), F32), jax.ShapeDtypeStruct((T, LRU_WIDTH), F32)],
        compiler_params=_params(("parallel",), VMEM_LARGE),
    )(h1, wcat, gq2, gk2)


def _forget_cumsum(f2d, bf, bl, seq):
    def body(z_ref, b_ref, o_ref):
        z = z_ref[...] + b_ref[...]
        lf = jnp.minimum(z, 0.0) - jnp.log(1.0 + jnp.exp(-jnp.abs(z)))
        row = lax.broadcasted_iota(jnp.int32, (seq, F_PAD), 0)
        k = 1
        while k < seq:
            lf = lf + jnp.where(row >= k, pltpu.roll(lf, k, 0), 0.0)
            k *= 2
        o_ref[...] = lf

    return pl.pallas_call(
        body, name="forget_cumsum", grid=(bl,),
        in_specs=[pl.BlockSpec((seq, F_PAD), lambda b: (b, 0)), _const((1, F_PAD))],
        out_specs=pl.BlockSpec((seq, F_PAD), lambda b: (b, 0)),
        out_shape=jax.ShapeDtypeStruct(f2d.shape, F32),
        compiler_params=_params(("parallel",)),
    )(f2d, bf)


def _attn_fwd(qn, kn, vb, frow, fstart, bl, seq, shards):
    tq = min(ATT_TILE, seq)
    nq = seq // tq
    T = bl * seq
    n = len(shards)

    def body(fs_ref, q_ref, k_ref, v_ref, fr_ref, *rest):
        g_in, (o_ref, lse_ref), g_out, sems = rest[:n], rest[n:n + 2], rest[n + 2:2 * n + 2], rest[2 * n + 2:]
        b, p, i = pl.program_id(0), pl.program_id(1), pl.program_id(2)
        copies = _chip_copies(g_in, g_out, *sems, scatter=False)

        @pl.when((b == 0) & (p == 0) & (i == 0))
        def _():
            for cp in copies:
                cp.start()

        lane = lax.broadcasted_iota(jnp.int32, (1, PAIR), 1)
        rows = lax.broadcasted_iota(jnp.int32, (tq, tq), 0)
        cols = lax.broadcasted_iota(jnp.int32, (tq, tq), 1)
        causal = cols <= rows
        q = q_ref[...]
        hms = [(lane >= HEAD_DIM * hh) & (lane < HEAD_DIM * (hh + 1)) for hh in range(2)]
        qhs = [jnp.where(hm, q, jnp.zeros_like(q)) for hm in hms]
        shifts = [fs_ref[((b * N_PAIR + p) * 2 + hh) * nq + i] for hh in range(2)]
        sum_lane = [HEAD_DIM * (1 - hh) for hh in range(2)]

        def block(j, carry, masked):
            start = pl.multiple_of(j * tq, tq)
            k = k_ref[pl.ds(start, tq), :]
            v = v_ref[pl.ds(start, tq), :]
            new = []
            for hh in range(2):
                m, acc = carry[hh]
                s = lax.dot_general(qhs[hh], k, (((1,), (1,)), ((), ())), preferred_element_type=F32)
                s = s - (fr_ref[0, 0, hh:hh + 1, pl.ds(start, tq)] - shifts[hh])
                if masked:
                    s = jnp.where(causal, s, NEG)
                m_new = jnp.maximum(m, jnp.max(s, axis=-1, keepdims=True))
                alpha = jnp.exp(m - m_new)
                pe = jnp.exp(s - m_new)
                vh = jnp.where(hms[hh], v, jnp.where(lane == sum_lane[hh], 1.0, 0.0).astype(v.dtype))
                pb = pe.astype(MXU_DTYPE)
                p_lo = (pe - pb.astype(F32)).astype(MXU_DTYPE)
                acc = (alpha * acc + jnp.dot(pb, vh, preferred_element_type=F32)
                       + jnp.dot(p_lo, vh, preferred_element_type=F32))
                new.append((m_new, acc))
            return tuple(new)

        init = (jnp.full((tq, 1), NEG, F32), jnp.zeros((tq, PAIR), F32))
        carry = lax.fori_loop(0, i, functools.partial(block, masked=False), (init, init))
        carry = block(i, carry, True)
        out = jnp.zeros((tq, PAIR), F32)
        lse = jnp.zeros((tq, PAIR), F32)
        for hh in range(2):
            m, acc = carry[hh]
            l = acc[:, sum_lane[hh]:sum_lane[hh] + 1]
            out = jnp.where(hms[hh], acc * (1.0 / l), out)
            lse = jnp.where(hms[hh], m + jnp.log(l), lse)
        o_ref[...] = out
        lse_ref[...] = lse

        @pl.when((b == bl - 1) & (p == N_PAIR - 1) & (i == nq - 1))
        def _():
            for cp in copies:
                cp.wait()

    blk = pl.BlockSpec((tq, PAIR), lambda b, p, i: (b * nq + i, p))
    full = pl.BlockSpec((seq, PAIR), lambda b, p, i: (b, p))
    return pl.pallas_call(
        body, name="attn_fwd", grid=(bl, N_PAIR, nq),
        in_specs=[pl.BlockSpec(memory_space=pltpu.SMEM), blk, full, full,
                  pl.BlockSpec((1, 1, 2, seq), lambda b, p, i: (b, p, 0, 0))] + [pl.BlockSpec(memory_space=pl.ANY)] * n,
        out_specs=[blk, blk] + [pl.BlockSpec(memory_space=pl.ANY)] * n,
        out_shape=[jax.ShapeDtypeStruct((T, ATT_WIDTH), F32)] * 2
        + [jax.ShapeDtypeStruct((4,) + s.shape, s.dtype) for s in shards],
        scratch_shapes=_chip_sems(n),
        compiler_params=_params(("arbitrary", "arbitrary", "arbitrary")),
    )(fstart, qn, kn, vb, frow, *shards)


def _conv_taps(lx, prev8, cw, cb):
    xs = jnp.concatenate([prev8, lx], axis=0)
    shifted = [lx] + [pltpu.roll(xs, k, 0)[8:] for k in range(1, CONV_WIDTH)]
    xc = cb + cw[CONV_WIDTH - 1:CONV_WIDTH] * lx
    for k in range(1, CONV_WIDTH):
        xc = xc + cw[CONV_WIDTH - 1 - k:CONV_WIDTH - k] * shifted[k]
    return xc, shifted


def _lru_gates(xc, wa, ba, wx, bx, lam):
    xb = xc.astype(MXU_DTYPE)
    r = _sigmoid(jnp.dot(xb, wa, preferred_element_type=F32) + ba)
    ig = _sigmoid(jnp.dot(xb, wx, preferred_element_type=F32) + bx)
    sp = jnp.maximum(-lam, 0.0) + jnp.log(1.0 + jnp.exp(-jnp.abs(lam)))
    log_a = -LRU_C * r * sp
    a = jnp.exp(log_a)
    th = jnp.tanh(log_a)
    mult = jnp.sqrt(-2.0 * th / (1.0 - th))
    return r, ig, sp, a, mult


def _gelu_parts(x):
    c0 = math.sqrt(2.0 / math.pi)
    t = jnp.tanh(c0 * (x + 0.044715 * x * x * x))
    g = 0.5 * x * (1.0 + t)
    dg = 0.5 * (1.0 + t) + 0.5 * x * (1.0 - t * t) * c0 * (1.0 + 3.0 * 0.044715 * x * x)
    return g, dg


def _lru_fwd(lx, lg, cw, cb, wa, ba, wx, bx, lam, bl, seq):
    tc = min(LRU_TILE, seq)
    nc = seq // tc
    T = bl * seq

    def body(lx_ref, lxp_ref, lg_ref, cw_ref, cb_ref, wa_ref, ba_ref, wx_ref, bx_ref, lam_ref,
             h_ref, rec_ref, hc_ref):
        i = pl.program_id(1)

        @pl.when(i == 0)
        def _():
            hc_ref[...] = jnp.zeros_like(hc_ref)

        lxv = lx_ref[...]
        prev8 = jnp.where(i > 0, lxp_ref[...], 0.0)
        xc, _ = _conv_taps(lxv, prev8, cw_ref[...], cb_ref[...])
        _, ig, _, a, mult = _lru_gates(xc, wa_ref[...], ba_ref[...], wx_ref[...], bx_ref[...], lam_ref[...])
        u = mult * (ig * xc)
        sub = lax.broadcasted_iota(jnp.int32, (tc, LRU_WIDTH), 0) & 7
        A, B = a, u
        for k in (1, 2, 4):
            a_s = jnp.where(sub >= k, pltpu.roll(A, k, 0), 1.0)
            b_s = jnp.where(sub >= k, pltpu.roll(B, k, 0), 0.0)
            B = A * b_s + B
            A = A * a_s
        carry = hc_ref[0:1, :]
        groups = []
        for g in range(tc // 8):
            hg = A[8 * g:8 * (g + 1)] * carry + B[8 * g:8 * (g + 1)]
            groups.append(hg)
            carry = hg[7:8]
        h = jnp.concatenate(groups, axis=0)
        hc_ref[0:1, :] = carry
        h_ref[...] = h
        g, _ = _gelu_parts(lg_ref[...])
        rec_ref[...] = h * g

    tile = pl.BlockSpec((tc, LRU_WIDTH), lambda b, i: (b * nc + i, 0))
    prev = pl.BlockSpec((8, LRU_WIDTH), lambda b, i: (jnp.maximum((b * seq + i * tc) // 8 - 1, 0), 0))
    vec = _const((1, LRU_WIDTH))
    mat = _const((LRU_WIDTH, LRU_WIDTH))
    return pl.pallas_call(
        body, name="lru_fwd", grid=(bl, nc),
        in_specs=[tile, prev, tile, _const((CONV_WIDTH, LRU_WIDTH)), vec, mat, vec, mat, vec, vec],
        out_specs=[tile, tile],
        out_shape=[jax.ShapeDtypeStruct((T, LRU_WIDTH), F32), jax.ShapeDtypeStruct((T, LRU_WIDTH), F32)],
        scratch_shapes=[pltpu.VMEM((8, LRU_WIDTH), F32)],
        compiler_params=_params(("arbitrary", "arbitrary")),
    )(lx, lx, lg, cw, cb, wa, ba, wx, bx, lam)


def _outproj(x2d, att, rec, ga, gr, wout):
    T = x2d.shape[0]
    tm = TOKEN_TILE

    def body(x_ref, a_ref, r_ref, ga_ref, gr_ref, w_ref, o_ref):
        a = a_ref[...]
        rc = r_ref[...]
        na = a * lax.rsqrt(jnp.mean(a * a, axis=-1, keepdims=True) + NORM_EPS) * ga_ref[...]
        nr = rc * lax.rsqrt(jnp.mean(rc * rc, axis=-1, keepdims=True) + NORM_EPS) * gr_ref[...]
        o_ref[...] = (x_ref[...]
                      + jnp.dot(na.astype(MXU_DTYPE), w_ref[:ATT_WIDTH, :], preferred_element_type=F32)
                      + jnp.dot(nr.astype(MXU_DTYPE), w_ref[ATT_WIDTH:, :], preferred_element_type=F32))

    row = lambda w: pl.BlockSpec((tm, w), lambda i: (i, 0))
    return pl.pallas_call(
        body, name="outproj", grid=(T // tm,),
        in_specs=[row(D_MODEL), row(ATT_WIDTH), row(LRU_WIDTH), _const((1, ATT_WIDTH)), _const((1, LRU_WIDTH)),
                  _const((D_MODEL, D_MODEL))],
        out_specs=row(D_MODEL),
        out_shape=jax.ShapeDtypeStruct((T, D_MODEL), F32),
        compiler_params=_params(("parallel",)),
    )(x2d, att, rec, ga, gr, wout)


def _mlp_fwd(x2, g2, wg, wu, wd, target):
    T = x2.shape[0]
    tm = TOKEN_TILE
    dff = wg.shape[0]

    def body(x_ref, g_ref, wg_ref, wu_ref, wd_ref, t_ref, gt_ref, up_ref, dy_ref, loss_ref):
        @pl.when(pl.program_id(0) == 0)
        def _():
            loss_ref[...] = jnp.zeros_like(loss_ref)

        x = x_ref[...]
        r = lax.rsqrt(jnp.mean(x * x, axis=-1, keepdims=True) + NORM_EPS)
        h = (x * r * g_ref[...]).astype(MXU_DTYPE)
        gt = _nt(h, wg_ref[...])
        up = _nt(h, wu_ref[...])
        gt_ref[...] = gt
        up_ref[...] = up
        act = (gt * _sigmoid(gt) * up).astype(MXU_DTYPE)
        y = x + jnp.dot(act, wd_ref[...], preferred_element_type=F32)
        e = y - t_ref[...]
        dy_ref[...] = e * (1.0 / D_MODEL)
        loss_ref[...] += jnp.sum(e * e)

    row = lambda w: pl.BlockSpec((tm, w), lambda i: (i, 0))
    return pl.pallas_call(
        body, name="mlp_fwd", grid=(T // tm,),
        in_specs=[row(D_MODEL), _const((1, D_MODEL)), _const((dff, D_MODEL)), _const((dff, D_MODEL)),
                  _const((dff, D_MODEL)), row(D_MODEL)],
        out_specs=[row(dff), row(dff), row(D_MODEL), _const((8, 128))],
        out_shape=[jax.ShapeDtypeStruct((T, dff), F32), jax.ShapeDtypeStruct((T, dff), F32),
                   jax.ShapeDtypeStruct((T, D_MODEL), F32), jax.ShapeDtypeStruct((8, 128), F32)],
        compiler_params=_params(("arbitrary",), VMEM_LARGE),
    )(x2, g2, wg, wu, wd, target)


def _mlp_bwd(dy, x2, gt, up, g2, wg, wu, wd):
    T = x2.shape[0]
    tm = TOKEN_TILE
    dff = wg.shape[0]

    def body(dy_ref, x_ref, gt_ref, up_ref, g_ref, wg_ref, wu_ref, wd_ref,
             dx_ref, dxb_ref, dgt_ref, dup_ref, act_ref, h_ref, dyb_ref, dg_ref):
        @pl.when(pl.program_id(0) == 0)
        def _():
            dg_ref[...] = jnp.zeros_like(dg_ref)

        dy_v = dy_ref[...]
        dyb = dy_v.astype(MXU_DTYPE)
        dyb_ref[...] = dyb
        x = x_ref[...]
        r = lax.rsqrt(jnp.mean(x * x, axis=-1, keepdims=True) + NORM_EPS)
        xh = x * r
        h_ref[...] = (xh * g_ref[...]).astype(MXU_DTYPE)
        gt_v = gt_ref[...]
        up_v = up_ref[...]
        sg = _sigmoid(gt_v)
        silu = gt_v * sg
        act_ref[...] = (silu * up_v).astype(MXU_DTYPE)
        dact = _nt(dyb, wd_ref[...])
        dup = (dact * silu).astype(MXU_DTYPE)
        dgt = (dact * up_v * (sg * (1.0 + gt_v * (1.0 - sg)))).astype(MXU_DTYPE)
        dup_ref[...] = dup
        dgt_ref[...] = dgt
        dh = (jnp.dot(dgt, wg_ref[...], preferred_element_type=F32)
              + jnp.dot(dup, wu_ref[...], preferred_element_type=F32))
        dg_ref[...] += jnp.sum(dh * xh, axis=0, keepdims=True)
        dxh = dh * g_ref[...]
        dx = dy_v + r * (dxh - xh * jnp.mean(dxh * xh, axis=-1, keepdims=True))
        dx_ref[...] = dx
        dxb_ref[...] = dx.astype(MXU_DTYPE)

    row = lambda w: pl.BlockSpec((tm, w), lambda i: (i, 0))
    return pl.pallas_call(
        body, name="mlp_bwd", grid=(T // tm,),
        in_specs=[row(D_MODEL), row(D_MODEL), row(dff), row(dff), _const((1, D_MODEL)),
                  _const((dff, D_MODEL)), _const((dff, D_MODEL)), _const((dff, D_MODEL))],
        out_specs=[row(D_MODEL), row(D_MODEL), row(dff), row(dff), row(dff), row(D_MODEL), row(D_MODEL),
                   _const((1, D_MODEL))],
        out_shape=[jax.ShapeDtypeStruct((T, D_MODEL), F32), jax.ShapeDtypeStruct((T, D_MODEL), MXU_DTYPE),
                   jax.ShapeDtypeStruct((T, dff), MXU_DTYPE), jax.ShapeDtypeStruct((T, dff), MXU_DTYPE),
                   jax.ShapeDtypeStruct((T, dff), MXU_DTYPE), jax.ShapeDtypeStruct((T, D_MODEL), MXU_DTYPE),
                   jax.ShapeDtypeStruct((T, D_MODEL), MXU_DTYPE), jax.ShapeDtypeStruct((1, D_MODEL), F32)],
        compiler_params=_params(("arbitrary",), VMEM_LARGE),
    )(dy, x2, gt, up, g2, wg, wu, wd)


def _matmul_tn(a, b, tn, name):
    T, K = a.shape
    N = b.shape[1]
    tt = min(1024, T)

    def body(a_ref, b_ref, o_ref):
        @pl.when(pl.program_id(1) == 0)
        def _():
            o_ref[...] = jnp.zeros_like(o_ref)

        o_ref[...] += _tn(a_ref[...], b_ref[...])

    return pl.pallas_call(
        body, name=name, grid=(N // tn, T // tt),
        in_specs=[pl.BlockSpec((tt, K), lambda n, t: (t, 0)), pl.BlockSpec((tt, tn), lambda n, t: (t, n))],
        out_specs=pl.BlockSpec((K, tn), lambda n, t: (0, n)),
        out_shape=jax.ShapeDtypeStruct((K, N), F32),
        compiler_params=_params(("parallel", "arbitrary"), VMEM_LARGE),
    )(a, b)


def _outproj_bwd(dx2b, att, rec, ga, gr, wout):
    T = att.shape[0]
    tm = TOKEN_TILE

    def body(dx_ref, a_ref, r_ref, ga_ref, gr_ref, w_ref, datt_ref, dattt_ref, delta_ref, drec_ref, mix_ref, dga_ref,
             dgr_ref):
        @pl.when(pl.program_id(0) == 0)
        def _():
            dga_ref[...] = jnp.zeros_like(dga_ref)
            dgr_ref[...] = jnp.zeros_like(dgr_ref)

        dmix = _nt(dx_ref[...], w_ref[...])

        def norm_bwd(v, g, dn):
            rr = lax.rsqrt(jnp.mean(v * v, axis=-1, keepdims=True) + NORM_EPS)
            vh = v * rr
            dvh = dn * g
            dv = rr * (dvh - vh * jnp.mean(dvh * vh, axis=-1, keepdims=True))
            return vh, dv, jnp.sum(dn * vh, axis=0, keepdims=True)

        a = a_ref[...]
        ah, datt, dga = norm_bwd(a, ga_ref[...], dmix[:, :ATT_WIDTH])
        rh, drec, dgr = norm_bwd(r_ref[...], gr_ref[...], dmix[:, ATT_WIDTH:])
        dga_ref[...] += dga
        dgr_ref[...] += dgr
        mix_ref[:, :ATT_WIDTH] = (ah * ga_ref[...]).astype(MXU_DTYPE)
        mix_ref[:, ATT_WIDTH:] = (rh * gr_ref[...]).astype(MXU_DTYPE)
        dattb = datt.astype(MXU_DTYPE)
        datt_ref[...] = dattb
        dattt_ref[...] = datt.T.astype(MXU_DTYPE)
        drec_ref[...] = drec
        lo = _lo_mask()
        prod = dattb.astype(F32) * a
        for p in range(N_PAIR):
            delta_ref[:, PAIR * p:PAIR * (p + 1)] = _half_sums(prod[:, PAIR * p:PAIR * (p + 1)], lo)

    row = lambda w: pl.BlockSpec((tm, w), lambda i: (i, 0))
    return pl.pallas_call(
        body, name="outproj_bwd", grid=(T // tm,),
        in_specs=[row(D_MODEL), row(ATT_WIDTH), row(LRU_WIDTH), _const((1, ATT_WIDTH)), _const((1, LRU_WIDTH)),
                  _const((D_MODEL, D_MODEL))],
        out_specs=[row(ATT_WIDTH), pl.BlockSpec((ATT_WIDTH, tm), lambda i: (0, i)), row(ATT_WIDTH), row(LRU_WIDTH),
                   row(D_MODEL), _const((1, ATT_WIDTH)), _const((1, LRU_WIDTH))],
        out_shape=[jax.ShapeDtypeStruct((T, ATT_WIDTH), MXU_DTYPE), jax.ShapeDtypeStruct((ATT_WIDTH, T), MXU_DTYPE),
                   jax.ShapeDtypeStruct((T, ATT_WIDTH), F32),
                   jax.ShapeDtypeStruct((T, LRU_WIDTH), F32), jax.ShapeDtypeStruct((T, D_MODEL), MXU_DTYPE),
                   jax.ShapeDtypeStruct((1, ATT_WIDTH), F32), jax.ShapeDtypeStruct((1, LRU_WIDTH), F32)],
        compiler_params=_params(("arbitrary",)),
    )(dx2b, att, rec, ga, gr, wout)


def _lru_bwd(drec, lg, h, lx, cw, cb, wa, ba, wx, bx, lam, bl, seq):
    tc = min(LRU_TILE, seq)
    nc = seq // tc
    T = bl * seq
    n = tc

    def body(dr_ref, lg_ref, h_ref, hp_ref, lx_ref, lxp_ref, cw_ref, cb_ref, wa_ref, ba_ref, wx_ref, bx_ref, lam_ref,
             dlx_ref, dlg_ref, dwa_ref, dwx_ref, small_ref, gc_ref, dxn_ref):
        b, i = pl.program_id(0), pl.program_id(1)
        ir = nc - 1 - i

        @pl.when((b == 0) & (i == 0))
        def _():
            dwa_ref[...] = jnp.zeros_like(dwa_ref)
            dwx_ref[...] = jnp.zeros_like(dwx_ref)
            small_ref[...] = jnp.zeros_like(small_ref)

        @pl.when(i == 0)
        def _():
            gc_ref[...] = jnp.zeros_like(gc_ref)
            dxn_ref[...] = jnp.zeros_like(dxn_ref)

        cw = cw_ref[...]
        lam_v = lam_ref[...]
        lxv = lx_ref[...]
        prev8 = jnp.where(ir > 0, lxp_ref[...], 0.0)
        xc, shifted = _conv_taps(lxv, prev8, cw, cb_ref[...])
        r, ig, sp, a, mult = _lru_gates(xc, wa_ref[...], ba_ref[...], wx_ref[...], bx_ref[...], lam_v)
        hv = h_ref[...]
        drv = dr_ref[...]
        g, dg = _gelu_parts(lg_ref[...])
        dlg_ref[...] = drv * hv * dg
        dh = drv * g

        row = lax.broadcasted_iota(jnp.int32, (n, LRU_WIDTH), 0)
        sub = row & 7
        A = jnp.where(row < n - 1, pltpu.roll(a, n - 1, 0), 0.0)
        B = dh + jnp.where(row == n - 1, gc_ref[0:1, :], 0.0)
        for k in (1, 2, 4):
            a_s = jnp.where(sub < 8 - k, pltpu.roll(A, n - k, 0), 1.0)
            b_s = jnp.where(sub < 8 - k, pltpu.roll(B, n - k, 0), 0.0)
            B = B + A * b_s
            A = A * a_s
        carry = jnp.zeros((1, LRU_WIDTH), F32)
        groups = [None] * (n // 8)
        for g in reversed(range(n // 8)):
            gg = B[8 * g:8 * (g + 1)] + A[8 * g:8 * (g + 1)] * carry
            groups[g] = gg
            carry = gg[0:1]
        gs = jnp.concatenate(groups, axis=0)
        gc_ref[0:1, :] = a[0:1, :] * carry

        hprev8 = jnp.where(ir > 0, hp_ref[...], 0.0)
        h_prev = pltpu.roll(jnp.concatenate([hprev8, hv], axis=0), 1, 0)[8:]
        da = gs * h_prev
        ix = ig * xc
        dmult = gs * ix
        dig = gs * mult * xc
        dxc = gs * mult * ig
        dlog_a = da * a - dmult * (a * a) / mult
        dr_gate = dlog_a * (-LRU_C * sp)
        dza = dr_gate * r * (1.0 - r)
        dzx = dig * ig * (1.0 - ig)
        dzab = dza.astype(MXU_DTYPE)
        dzxb = dzx.astype(MXU_DTYPE)
        xcb = xc.astype(MXU_DTYPE)
        dwa_ref[...] += _tn(xcb, dzab)
        dwx_ref[...] += _tn(xcb, dzxb)
        dxc = dxc + _nt(dzab, wa_ref[...]) + _nt(dzxb, wx_ref[...])

        ds = jnp.concatenate([dxc, dxn_ref[...]], axis=0)
        dlx = cw[CONV_WIDTH - 1:CONV_WIDTH] * dxc
        for k in range(1, CONV_WIDTH):
            dlx = dlx + cw[CONV_WIDTH - 1 - k:CONV_WIDTH - k] * pltpu.roll(ds, n + 8 - k, 0)[:n]
        dlx_ref[...] = dlx
        dxn_ref[...] = dxc[0:8, :]

        colsum = lambda v: jnp.sum(v, axis=0, keepdims=True)
        small_ref[0:1, :] += colsum(dza)
        small_ref[1:2, :] += colsum(dzx)
        small_ref[2:3, :] += colsum(dlog_a * r) * (LRU_C * _sigmoid(-lam_v))
        small_ref[3:4, :] += colsum(dxc)
        for k in range(CONV_WIDTH):
            j = CONV_WIDTH - 1 - k
            small_ref[4 + j:5 + j, :] += colsum(dxc * shifted[k])

    tile = pl.BlockSpec((tc, LRU_WIDTH), lambda b, i: (b * nc + (nc - 1 - i), 0))
    prev = pl.BlockSpec((8, LRU_WIDTH), lambda b, i: (jnp.maximum((b * seq + (nc - 1 - i) * tc) // 8 - 1, 0), 0))
    vec = _const((1, LRU_WIDTH))
    mat = _const((LRU_WIDTH, LRU_WIDTH))
    return pl.pallas_call(
        body, name="lru_bwd", grid=(bl, nc),
        in_specs=[tile, tile, tile, prev, tile, prev, _const((CONV_WIDTH, LRU_WIDTH)), vec, mat, vec, mat, vec, vec],
        out_specs=[tile, tile, mat, mat, _const((8, LRU_WIDTH))],
        out_shape=[jax.ShapeDtypeStruct((T, LRU_WIDTH), F32), jax.ShapeDtypeStruct((T, LRU_WIDTH), F32),
                   jax.ShapeDtypeStruct((LRU_WIDTH, LRU_WIDTH), F32), jax.ShapeDtypeStruct((LRU_WIDTH, LRU_WIDTH), F32),
                   jax.ShapeDtypeStruct((8, LRU_WIDTH), F32)],
        scratch_shapes=[pltpu.VMEM((8, LRU_WIDTH), F32), pltpu.VMEM((8, LRU_WIDTH), F32)],
        compiler_params=_params(("arbitrary", "arbitrary")),
    )(drec, lg, h, h, lx, lx, cw, cb, wa, ba, wx, bx, lam)


def _attn_bwd(qn, kn, vb, dob, qt, dot_, lse, delta, frow, fstart, bl, seq, slabs, packs):
    tq = min(ATT_TILE, seq)
    nq = seq // tq
    T = bl * seq
    n, npk = len(slabs), len(packs)
    nx = n + npk

    def body(fs_ref, q_ref, k_ref, v_ref, do_ref, qt_ref, dot_ref, lse_ref, dl_ref, fr_ref, *rest):
        x_in, (dq_ref, dk_ref, dv_ref, df_ref), x_out = rest[:nx], rest[nx:nx + 4], rest[nx + 4:2 * nx + 4]
        send_sems, recv_sems, loc_sems, psend, precv = rest[2 * nx + 4:]
        b, p, j = pl.program_id(0), pl.program_id(1), pl.program_id(2)
        copies = _chip_copies(x_in[:n], x_out[:n], send_sems, recv_sems, loc_sems, scatter=True)
        copies += _device_copies(x_in[n:], x_out[n:], psend, precv, loc_sems, n)

        @pl.when((b == 0) & (p == 0) & (j == 0))
        def _():
            for cp in copies:
                cp.start()

        @pl.when(j == 0)
        def _():
            dq_ref[...] = jnp.zeros_like(dq_ref)

        lane = lax.broadcasted_iota(jnp.int32, (1, PAIR), 1)
        rows = lax.broadcasted_iota(jnp.int32, (tq, tq), 0)
        cols = lax.broadcasted_iota(jnp.int32, (tq, tq), 1)
        causal = cols <= rows
        kv = k_ref[...]
        vv = v_ref[...]
        hms = [(lane >= HEAD_DIM * hh) & (lane < HEAD_DIM * (hh + 1)) for hh in range(2)]
        srow = lax.broadcasted_iota(jnp.int32, (PAIR, 1), 0)
        hms_t = [(srow >= HEAD_DIM * hh) & (srow < HEAD_DIM * (hh + 1)) for hh in range(2)]
        khs = [jnp.where(hm, kv, jnp.zeros_like(kv)) for hm in hms]
        fks = [fr_ref[0, 0, hh:hh + 1, :] for hh in range(2)]
        bases = [((b * N_PAIR + p) * 2 + hh) * nq for hh in range(2)]

        def block(i, carry, masked):
            dk, dv, dfs = carry
            start = pl.multiple_of(i * tq, tq)
            qi = q_ref[pl.ds(start, tq), :]
            doi = do_ref[pl.ds(start, tq), :]
            qti = qt_ref[:, pl.ds(start, tq)]
            doti = dot_ref[:, pl.ds(start, tq)]
            dq = jnp.zeros((tq, PAIR), F32)
            new_dfs = []
            for hh in range(2):
                c0 = HEAD_DIM * hh
                qh = jnp.where(hms[hh], qi, jnp.zeros_like(qi))
                doh = jnp.where(hms[hh], doi, jnp.zeros_like(doi))
                s = _nt(qh, kv) - (fks[hh] - fs_ref[bases[hh] + i])
                if masked:
                    s = jnp.where(causal, s, NEG)
                pr = jnp.exp(s - lse_ref[pl.ds(start, tq), c0:c0 + 1])
                dp = _nt(doh, vv)
                ds = pr * (dp - dl_ref[pl.ds(start, tq), c0:c0 + 1])
                dsb = ds.astype(MXU_DTYPE)
                dv = dv + jnp.dot(jnp.where(hms_t[hh], doti, jnp.zeros_like(doti)), pr.astype(MXU_DTYPE),
                                  preferred_element_type=F32)
                dk = dk + jnp.dot(jnp.where(hms_t[hh], qti, jnp.zeros_like(qti)), dsb, preferred_element_type=F32)
                dq = dq + jnp.dot(dsb, khs[hh], preferred_element_type=F32)
                new_dfs.append(dfs[hh] - jnp.sum(ds, axis=0, keepdims=True))
            dq_ref[pl.ds(start, tq), :] += dq
            return dk, dv, tuple(new_dfs)

        zero = jnp.zeros((PAIR, tq), F32)
        carry = block(j, (zero, zero, (jnp.zeros((1, tq), F32), jnp.zeros((1, tq), F32))), True)
        dk, dv, dfs = lax.fori_loop(j + 1, nq, functools.partial(block, masked=False), carry)
        for hh in range(2):
            df_ref[0, 0, hh:hh + 1, :] = dfs[hh]
        dk_ref[...] = dk.T
        dv_ref[...] = dv.T

        @pl.when((b == bl - 1) & (p == N_PAIR - 1) & (j == nq - 1))
        def _():
            for cp in copies:
                cp.wait()

    blk = pl.BlockSpec((tq, PAIR), lambda b, p, j: (b * nq + j, p))
    full = pl.BlockSpec((seq, PAIR), lambda b, p, j: (b, p))
    fblk = pl.BlockSpec((1, 1, 2, tq), lambda b, p, j: (b, p, 0, j))
    full_t = pl.BlockSpec((PAIR, seq), lambda b, p, j: (p, b))
    hbm = pl.BlockSpec(memory_space=pl.ANY)
    return pl.pallas_call(
        body, name="attn_bwd", grid=(bl, N_PAIR, nq),
        in_specs=[pl.BlockSpec(memory_space=pltpu.SMEM), full, blk, blk, full, full_t, full_t, full, full, fblk]
        + [hbm] * nx,
        out_specs=[full, blk, blk, fblk] + [hbm] * nx,
        out_shape=[jax.ShapeDtypeStruct((T, ATT_WIDTH), F32), jax.ShapeDtypeStruct((T, ATT_WIDTH), F32),
                   jax.ShapeDtypeStruct((T, ATT_WIDTH), F32), jax.ShapeDtypeStruct((bl, N_PAIR, 2, seq), F32)]
        + [jax.ShapeDtypeStruct(s.shape, s.dtype) for s in slabs]
        + [jax.ShapeDtypeStruct((8,) + p.shape, p.dtype) for p in packs],
        scratch_shapes=[pltpu.SemaphoreType.DMA((3 * n,)), pltpu.SemaphoreType.DMA((3 * n,)),
                        pltpu.SemaphoreType.DMA((nx,)),
                        pltpu.SemaphoreType.DMA((7 * npk,)), pltpu.SemaphoreType.DMA((7 * npk,))],
        compiler_params=_params(("arbitrary", "arbitrary", "arbitrary"), VMEM_LARGE),
    )(fstart, qn, kn, vb, dob, qt, dot_, lse, delta, frow, *slabs, *packs)


def _forget_bwd(dfcol, f2d, bf, bl, seq):
    def body(d_ref, z_ref, b_ref, o_ref, db_ref):
        @pl.when(pl.program_id(0) == 0)
        def _():
            db_ref[...] = jnp.zeros_like(db_ref)

        d = d_ref[...]
        row = lax.broadcasted_iota(jnp.int32, (seq, F_PAD), 0)
        k = 1
        while k < seq:
            d = d + jnp.where(row < seq - k, pltpu.roll(d, seq - k, 0), 0.0)
            k *= 2
        dz = d * _sigmoid(-(z_ref[...] + b_ref[...]))
        o_ref[...] = dz
        db_ref[...] += jnp.sum(dz, axis=0, keepdims=True)

    blk = pl.BlockSpec((seq, F_PAD), lambda b: (b, 0))
    return pl.pallas_call(
        body, name="forget_bwd", grid=(bl,),
        in_specs=[blk, blk, _const((1, F_PAD))],
        out_specs=[blk, _const((1, F_PAD))],
        out_shape=[jax.ShapeDtypeStruct(f2d.shape, F32), jax.ShapeDtypeStruct((1, F_PAD), F32)],
        compiler_params=_params(("arbitrary",)),
    )(dfcol, f2d, bf)


def _dproj(dq, dk, dv, qkv, df, dlx, dlg, gq2, gk2):
    T = dq.shape[0]
    tm = TOKEN_TILE

    def body(dq_ref, dk_ref, dv_ref, qkv_ref, df_ref, dlx_ref, dlg_ref, gq_ref, gk_ref, dp_ref, dgq_ref, dgk_ref):
        @pl.when(pl.program_id(0) == 0)
        def _():
            dgq_ref[...] = jnp.zeros_like(dgq_ref)
            dgk_ref[...] = jnp.zeros_like(dgk_ref)

        lo = _lo_mask()

        def head_norm_bwd(t, g2, dy):
            rr = lax.rsqrt(_half_sums(t * t, lo) * (1.0 / HEAD_DIM) + NORM_EPS)
            th = t * rr
            dth = dy * g2
            mm = _half_sums(dth * th, lo) * (1.0 / HEAD_DIM)
            return rr * (dth - th * mm), jnp.sum(dy * th, axis=0, keepdims=True)

        dgq = jnp.zeros((1, PAIR), F32)
        dgk = jnp.zeros((1, PAIR), F32)
        for p in range(N_PAIR):
            cq = slice(PAIR * p, PAIR * (p + 1))
            ck = slice(ATT_WIDTH + PAIR * p, ATT_WIDTH + PAIR * (p + 1))
            dqp, g_ = head_norm_bwd(qkv_ref[:, cq], gq_ref[...], dq_ref[:, cq] * QK_SCALE)
            dgq = dgq + g_
            dp_ref[:, cq] = dqp.astype(MXU_DTYPE)
            dkp, g_ = head_norm_bwd(qkv_ref[:, ck], gk_ref[...], dk_ref[:, cq])
            dgk = dgk + g_
            dp_ref[:, ck] = dkp.astype(MXU_DTYPE)
        dgq_ref[...] += dgq
        dgk_ref[...] += dgk
        f0 = 3 * ATT_WIDTH
        dp_ref[:, 2 * ATT_WIDTH:f0] = dv_ref[...].astype(MXU_DTYPE)
        dp_ref[:, f0:f0 + F_PAD] = df_ref[...].astype(MXU_DTYPE)
        dp_ref[:, f0 + F_PAD:f0 + F_PAD + LRU_WIDTH] = dlx_ref[...].astype(MXU_DTYPE)
        dp_ref[:, f0 + F_PAD + LRU_WIDTH:] = dlg_ref[...].astype(MXU_DTYPE)

    row = lambda w: pl.BlockSpec((tm, w), lambda i: (i, 0))
    return pl.pallas_call(
        body, name="dproj", grid=(T // tm,),
        in_specs=[row(ATT_WIDTH), row(ATT_WIDTH), row(ATT_WIDTH), row(2 * ATT_WIDTH), row(F_PAD), row(LRU_WIDTH),
                  row(LRU_WIDTH), _const((1, PAIR)), _const((1, PAIR))],
        out_specs=[row(N_CAT), _const((1, PAIR)), _const((1, PAIR))],
        out_shape=[jax.ShapeDtypeStruct((T, N_CAT), MXU_DTYPE), jax.ShapeDtypeStruct((1, PAIR), F32),
                   jax.ShapeDtypeStruct((1, PAIR), F32)],
        compiler_params=_params(("arbitrary",)),
    )(dq, dk, dv, qkv, df, dlx, dlg, gq2, gk2)


def _inproj_bwd(dproj, x2d, dx2, g1, wcat, slabs):
    T = x2d.shape[0]
    tm = TOKEN_TILE
    n = len(slabs)
    steps = T // tm

    def body(dp_ref, x_ref, dx2_ref, g1_ref, w_ref, *rest):
        s_in, (gx_ref, dg1_ref), s_out, sems = rest[:n], rest[n:n + 2], rest[n + 2:2 * n + 2], rest[2 * n + 2:]
        i = pl.program_id(0)
        copies = _chip_copies(s_in, s_out, *sems, scatter=True)

        @pl.when(i == 0)
        def _():
            dg1_ref[...] = jnp.zeros_like(dg1_ref)
            for cp in copies:
                cp.start()

        dh = jnp.dot(dp_ref[...], w_ref[...], preferred_element_type=F32)
        x = x_ref[...]
        r = lax.rsqrt(jnp.mean(x * x, axis=-1, keepdims=True) + NORM_EPS)
        xh = x * r
        dg1_ref[...] += jnp.sum(dh * xh, axis=0, keepdims=True)
        dxh = dh * g1_ref[...]
        gx_ref[...] = dx2_ref[...] + r * (dxh - xh * jnp.mean(dxh * xh, axis=-1, keepdims=True))

        @pl.when(i == steps - 1)
        def _():
            for cp in copies:
                cp.wait()

    row = lambda w: pl.BlockSpec((tm, w), lambda i: (i, 0))
    hbm = pl.BlockSpec(memory_space=pl.ANY)
    return pl.pallas_call(
        body, name="inproj_bwd", grid=(steps,),
        in_specs=[row(N_CAT), row(D_MODEL), row(D_MODEL), _const((1, D_MODEL)), _const((N_CAT, D_MODEL))] + [hbm] * n,
        out_specs=[row(D_MODEL), _const((1, D_MODEL))] + [hbm] * n,
        out_shape=[jax.ShapeDtypeStruct((T, D_MODEL), F32), jax.ShapeDtypeStruct((1, D_MODEL), F32)]
        + [jax.ShapeDtypeStruct(a.shape, a.dtype) for a in slabs],
        scratch_shapes=_chip_sems(n),
        compiler_params=_params(("arbitrary",), VMEM_LARGE),
    )(dproj, x2d, dx2, g1, wcat, *slabs)


ELEMENTWISE_COLS = 256


def _sum_slabs(recvs):
    n = len(recvs)
    cols = recvs[0].shape[2]
    cb = ELEMENTWISE_COLS

    def body(*refs):
        for r_ref, o_ref in zip(refs[:n], refs[n:]):
            part = [r_ref[s].astype(F32) for s in range(4)]
            o_ref[...] = ((part[0] + part[1]) + part[2]) + part[3]

    return pl.pallas_call(
        body, name="sum_slabs", grid=(cols // cb,),
        in_specs=[pl.BlockSpec((4, r.shape[1], cb), lambda i: (0, 0, i)) for r in recvs],
        out_specs=[pl.BlockSpec((r.shape[1], cb), lambda i: (0, i)) for r in recvs],
        out_shape=[jax.ShapeDtypeStruct(r.shape[1:], F32) for r in recvs],
        compiler_params=_params(("parallel",), VMEM_LARGE),
    )(*recvs)


def _adamw_math(w, g, m, v):
    m = ADAM_B1 * m + (1.0 - ADAM_B1) * g
    v = ADAM_B2 * v + (1.0 - ADAM_B2) * (g * g)
    m_hat = m / (1.0 - ADAM_B1 ** ADAM_STEP)
    v_hat = v / (1.0 - ADAM_B2 ** ADAM_STEP)
    delta = -ADAM_LR * (m_hat / (jnp.sqrt(v_hat) + ADAM_EPS) + ADAM_WD * w)
    return delta, m, v


def _adamw_pairs(groups):
    n = len(groups)
    cols = groups[0][0].shape[1]
    cb = ELEMENTWISE_COLS // 2

    def body(*refs):
        ins, outs = refs[:5 * n], refs[5 * n:]
        for w in range(n):
            a_ref, b_ref, w_ref, m_ref, v_ref = ins[5 * w:5 * w + 5]
            g_ref, d_ref, nm_ref, nv_ref = outs[4 * w:4 * w + 4]
            g = a_ref[...] + b_ref[...]
            g_ref[...] = g
            d_ref[...], nm_ref[...], nv_ref[...] = _adamw_math(w_ref[...], g, m_ref[...], v_ref[...])

    blk = lambda rows: pl.BlockSpec((rows, cb), lambda i: (0, i))
    res = pl.pallas_call(
        body, name="adamw_big", grid=(cols // cb,),
        in_specs=[blk(g[0].shape[0]) for g in groups for _ in range(5)],
        out_specs=[blk(g[0].shape[0]) for g in groups for _ in range(4)],
        out_shape=[jax.ShapeDtypeStruct(g[0].shape, F32) for g in groups for _ in range(4)],
        compiler_params=_params(("parallel",), VMEM_LARGE),
    )(*[a for g in groups for a in g])
    return [res[4 * w:4 * w + 4] for w in range(n)]


VEC_ROW = {"norm1_g": 0, "norm2_g": 1, "attn_out_g": 2, "lru_out_g": 3, "q_norm_g": 4, "k_norm_g": 5, "b_f": 6,
           "b_a": 8, "b_x": 9, "lam": 10, "conv_b": 11}
LOSS_ROW, CONV_W_ROW, PACK_ROWS = 7, 12, 16
SMALL = list(VEC_ROW) + ["conv_w", "w_a", "w_x"]


def _pack_small(dg1, dg2, dga, dgr, dgq, dgk, dbf, sq_err, lru_small):
    def body(dg1_ref, dg2_ref, dga_ref, dgr_ref, dgq_ref, dgk_ref, dbf_ref, err_ref, lru_ref, v_ref):
        v_ref[...] = jnp.zeros_like(v_ref)
        v_ref[0:1, :] = dg1_ref[...]
        v_ref[1:2, :] = dg2_ref[...]
        v_ref[2:3, 0:ATT_WIDTH] = dga_ref[...]
        v_ref[3:4, 0:LRU_WIDTH] = dgr_ref[...]
        for row, ref in ((4, dgq_ref), (5, dgk_ref)):
            g = ref[...]
            v_ref[row:row + 1, 0:PAIR] = g + pltpu.roll(g, HEAD_DIM, 1)
        v_ref[6:7, 0:F_PAD] = dbf_ref[...]
        v_ref[LOSS_ROW:LOSS_ROW + 1, 0:128] = err_ref[0:1, :] * (0.5 / D_MODEL)
        v_ref[8:16, 0:LRU_WIDTH] = lru_ref[...]

    ins = [dg1, dg2, dga, dgr, dgq, dgk, dbf, sq_err, lru_small]
    return pl.pallas_call(
        body, name="pack_small", grid=(1,),
        in_specs=[_const(a.shape) for a in ins], out_specs=_const((PACK_ROWS, D_MODEL)),
        out_shape=jax.ShapeDtypeStruct((PACK_ROWS, D_MODEL), F32),
        compiler_params=_params(("arbitrary",)),
    )(*ins)


def _diag_blocks(dwa_bd, dwx_bd):
    blk = LRU_WIDTH // LRU_BLOCKS

    def body(wa_ref, wx_ref, oa_ref, ox_ref):
        for src, dst in ((wa_ref, oa_ref), (wx_ref, ox_ref)):
            for nb in range(LRU_BLOCKS):
                tile = src[blk * nb:blk * (nb + 1), PAIR * (nb // 2):PAIR * (nb // 2 + 1)]
                if nb % 2:
                    tile = pltpu.roll(tile, blk, 1)
                dst[nb] = tile[:, 0:blk]

    out = jax.ShapeDtypeStruct((LRU_BLOCKS, blk, blk), F32)
    return pl.pallas_call(
        body, name="diag_blocks", grid=(1,),
        in_specs=[_const(dwa_bd.shape)] * 2, out_specs=[_const(out.shape)] * 2, out_shape=[out, out],
        compiler_params=_params(("arbitrary",)),
    )(dwa_bd, dwx_bd)


def _adamw_small(recv_v, recv_a, recv_x, params):
    names = list(params)
    flat = [a for n in names for a in params[n]]

    def body(rv_ref, ra_ref, rx_ref, *refs):
        ins, loss_ref, outs = refs[:len(flat)], refs[len(flat)], refs[len(flat) + 1:]
        x, y = lax.axis_index("x"), lax.axis_index("y")
        me = 2 * x + y

        def total(r):
            acc = r[0]
            for d in range(1, 8):
                acc = acc + r[d]
            return acc

        gv, ga, gx = total(rv_ref), total(ra_ref), total(rx_ref)
        loss_ref[...] = gv[LOSS_ROW:LOSS_ROW + 1, 0:128]
        for i, n in enumerate(names):
            w_ref, m_ref, v_ref = ins[3 * i:3 * i + 3]
            g_ref, d_ref, nm_ref, nv_ref = outs[4 * i:4 * i + 4]
            if n in VEC_ROW:
                g = gv[VEC_ROW[n]:VEC_ROW[n] + 1, 0:w_ref.shape[1]]
                w, m, v = w_ref[...], m_ref[...], v_ref[...]
            else:
                if n == "conv_w":
                    full = gv[CONV_W_ROW:CONV_W_ROW + CONV_WIDTH, 0:LRU_WIDTH]
                    width = LRU_WIDTH // 4
                    g = jnp.zeros((CONV_WIDTH, width), F32)
                    for s in range(4):
                        g = jnp.where(me == s, full[:, width * s:width * (s + 1)], g)
                else:
                    g = ga if n == "w_a" else gx
                w, m, v = w_ref[0], m_ref[0], v_ref[0]
            d, nm, nv = _adamw_math(w, g, m, v)
            for ref, val in ((g_ref, g), (d_ref, d), (nm_ref, nm), (nv_ref, nv)):
                if n in VEC_ROW:
                    ref[...] = val
                else:
                    ref[0] = val

    out_shape = [jax.ShapeDtypeStruct((1, 128), F32)] + [jax.ShapeDtypeStruct(params[n][0].shape, F32)
                                                          for n in names for _ in range(4)]
    res = pl.pallas_call(
        body, name="adamw_small", grid=(1,),
        in_specs=[_const(a.shape) for a in (recv_v, recv_a, recv_x, *flat)],
        out_specs=[_const(o.shape) for o in out_shape], out_shape=out_shape,
        compiler_params=_params(("arbitrary",)),
    )(recv_v, recv_a, recv_x, *flat)
    return res[0], {n: tuple(res[1 + 4 * i:5 + 4 * i]) for i, n in enumerate(names)}


def _cat_shards(g, pad_at=None, pad=0):
    _, rows, w = g.shape
    pieces = []
    for s in range(4):
        lo, hi = s * w, (s + 1) * w
        if pad_at is not None and lo < pad_at <= hi:
            pieces += [g[s][:, :pad_at - lo], jnp.zeros((rows, pad), g.dtype)]
            if pad_at < hi:
                pieces.append(g[s][:, pad_at - lo:])
        else:
            pieces.append(g[s])
    return jnp.concatenate(pieces, axis=1)


def _block_diag(w):
    eye = jnp.eye(LRU_BLOCKS, dtype=w.dtype)
    return (w[:, :, None, :] * eye[:, None, :, None]).reshape(LRU_WIDTH, LRU_WIDTH)


def kernel(x, norm1_g, w_in, q_norm_g, k_norm_g, b_f, conv_w, conv_b, w_a, b_a, w_x, b_x, lam, attn_out_g, lru_out_g, w_out, norm2_g, w_gate, w_up, w_down, loss_target, m_norm1_g, m_w_in, m_q_norm_g, m_k_norm_g, m_b_f, m_conv_w, m_conv_b, m_w_a, m_b_a, m_w_x, m_b_x, m_lam, m_attn_out_g, m_lru_out_g, m_w_out, m_norm2_g, m_w_gate, m_w_up, m_w_down, v_norm1_g, v_w_in, v_q_norm_g, v_k_norm_g, v_b_f, v_conv_w, v_conv_b, v_w_a, v_b_a, v_w_x, v_b_x, v_lam, v_attn_out_g, v_lru_out_g, v_w_out, v_norm2_g, v_w_gate, v_w_up, v_w_down):
    args = dict(locals())
    bl, seq, _ = x.shape
    T = bl * seq
    tq = min(ATT_TILE, seq)
    nq = seq // tq
    dff = w_gate.shape[2] * 4

    def transposed(name):
        return name.endswith(("w_in", "w_gate", "w_up"))

    def shard2d(name):
        return jnp.swapaxes(args[name], 1, 2)[0] if transposed(name) else args[name][0]

    x2d = x.reshape(T, D_MODEL)
    target2d = loss_target.reshape(T, D_MODEL)
    h1b, g_in, g_cw = _norm1_and_gather(x2d, norm1_g, shard2d("w_in").astype(MXU_DTYPE), conv_w[0])
    later_shards = [shard2d(n).astype(MXU_DTYPE) for n in ("w_out", "w_gate", "w_up", "w_down")]
    f0 = 3 * ATT_WIDTH
    w_in_t = g_in.reshape(-1, D_MODEL)
    wcat = jnp.concatenate([w_in_t[:f0 + HEADS], jnp.zeros((F_PAD - HEADS, D_MODEL), w_in_t.dtype),
                            w_in_t[f0 + HEADS:]], axis=0)
    cw_full = _cat_shards(g_cw)
    wa_bd = _block_diag(w_a[0]).astype(MXU_DTYPE)
    wx_bd = _block_diag(w_x[0]).astype(MXU_DTYPE)
    gq2 = jnp.tile(q_norm_g, (1, 2))
    gk2 = jnp.tile(k_norm_g, (1, 2))
    bf_pad = jnp.pad(b_f, ((0, 0), (0, F_PAD - HEADS)))

    qkv, qn, qn_t, kn, vb, f2d, lx, lg = _inproj(h1b, wcat, gq2, gk2)
    fcol = _forget_cumsum(f2d, bf_pad, bl, seq)
    frow = jnp.transpose(fcol.reshape(bl, seq, F_PAD)[:, :, :HEADS], (0, 2, 1)).reshape(bl, N_PAIR, 2, seq)
    fstart = frow[:, :, :, ::tq].reshape(-1)
    att, lse, g_out, g_gate, g_up, g_down = _attn_fwd(qn, kn, vb, frow, fstart, bl, seq, later_shards)
    wout_full = g_out.reshape(D_MODEL, D_MODEL)
    wg_full, wu_full = g_gate.reshape(dff, D_MODEL), g_up.reshape(dff, D_MODEL)
    wd_full = g_down.reshape(dff, D_MODEL)
    h, rec = _lru_fwd(lx, lg, cw_full, conv_b, wa_bd, b_a, wx_bd, b_x, lam, bl, seq)
    x2 = _outproj(x2d, att, rec, attn_out_g, lru_out_g, wout_full)
    gt, up, dy, sq_err = _mlp_fwd(x2, norm2_g, wg_full, wu_full, wd_full, target2d)

    dx2, dx2b, dgtb, dupb, actb, h2b, dyb, dg2 = _mlp_bwd(dy, x2, gt, up, norm2_g, wg_full, wu_full, wd_full)
    dw_down = _matmul_tn(actb, dyb, D_MODEL, "dw_down")
    dw_gate = _matmul_tn(dgtb, h2b, D_MODEL, "dw_gate")
    dw_up = _matmul_tn(dupb, h2b, D_MODEL, "dw_up")
    dattb, dattb_t, delta, drec, mixb, dga, dgr = _outproj_bwd(dx2b, att, rec, attn_out_g, lru_out_g, wout_full)
    dw_out = _matmul_tn(mixb, dx2b, D_MODEL, "dw_out")
    dlx, dlg, dwa_bd, dwx_bd, lru_small = _lru_bwd(drec, lg, h, lx, cw_full, conv_b, wa_bd, b_a, wx_bd, b_x, lam, bl, seq)
    early_slabs = [dw_out.reshape(4, D_MODEL // 4, D_MODEL), dw_gate.reshape(4, dff // 4, D_MODEL),
                   dw_up.reshape(4, dff // 4, D_MODEL), dw_down.reshape(4, dff // 4, D_MODEL)]
    pack_a, pack_x = _diag_blocks(dwa_bd, dwx_bd)
    dq, dk, dv, dfrow, *recv_early = _attn_bwd(qn, kn, vb, dattb, qn_t, dattb_t, lse, delta, frow, fstart, bl, seq,
                                               early_slabs, [pack_a, pack_x])
    recv_early, (recv_a, recv_x) = recv_early[:4], recv_early[4:]
    dfcol = jnp.pad(jnp.transpose(dfrow.reshape(bl, HEADS, seq), (0, 2, 1)), ((0, 0), (0, 0), (0, F_PAD - HEADS)))
    df, dbf = _forget_bwd(dfcol.reshape(T, F_PAD), f2d, bf_pad, bl, seq)
    dprojb, dgq, dgk = _dproj(dq, dk, dv, qkv, df, dlx, dlg, gq2, gk2)
    dwcat = _matmul_tn(dprojb, h1b, D_MODEL, "dw_in")
    dw_in_slabs = jnp.concatenate([dwcat[:f0 + HEADS], dwcat[f0 + F_PAD:]], axis=0).astype(jnp.bfloat16).reshape(
        4, -1, D_MODEL)
    grad_x, dg1, recv_in = _inproj_bwd(dprojb, x2d, dx2, norm1_g, wcat, [dw_in_slabs])

    pack_v = _pack_small(dg1, dg2, dga, dgr, dgq, dgk, dbf, sq_err, lru_small)
    recv = [recv_in] + recv_early
    big = ["w_in", "w_out", "w_gate", "w_up", "w_down"]
    part = _sum_slabs(recv)
    *theirs, recv_v = _swap_with_sibling(part, [pack_v])
    results = _adamw_pairs([(a, b_, shard2d(n), shard2d("m_" + n), shard2d("v_" + n))
                            for n, a, b_ in zip(big, part, theirs)])
    out = {}
    for n, res in zip(big, results):
        out[n] = tuple(jnp.swapaxes(r[None], 1, 2) if transposed(n) else r[None] for r in res)
    loss_row, small_out = _adamw_small(recv_v, recv_a, recv_x,
                                       {n: (args[n], args["m_" + n], args["v_" + n]) for n in SMALL})
    out.update(small_out)
    loss = loss_row[0, 0]

    order = ["norm1_g", "w_in", "q_norm_g", "k_norm_g", "b_f", "conv_w", "conv_b", "w_a", "b_a", "w_x", "b_x", "lam",
             "attn_out_g", "lru_out_g", "w_out", "norm2_g", "w_gate", "w_up", "w_down"]
    return (loss, grad_x.reshape(bl, seq, D_MODEL), *[out[n][0] for n in order], *[out[n][1] for n in order],
            *[out[n][2] for n in order], *[out[n][3] for n in order])
```

```python
import functools
import math

import jax
import jax.numpy as jnp
from jax import lax
from jax.experimental import pallas as pl
from jax.experimental.pallas import tpu as pltpu

F32 = jnp.float32
MXU_DTYPE = jnp.bfloat16
MESH = pl.DeviceIdType.MESH

D_MODEL = 1024
ATT_WIDTH = 512
LRU_WIDTH = 512
HEADS = 8
HEAD_DIM = 64
PAIR = 2 * HEAD_DIM
N_PAIR = HEADS // 2
LRU_BLOCKS = 8
CONV_WIDTH = 4
LRU_C = 8.0
NORM_EPS = 1e-6
QK_SCALE = 1.0 / math.sqrt(HEAD_DIM)
F_PAD = 128
N_CAT = 3 * ATT_WIDTH + F_PAD + 2 * LRU_WIDTH
NEG = -1e30

ADAM_LR, ADAM_B1, ADAM_B2, ADAM_EPS, ADAM_WD, ADAM_STEP = 0.001, 0.9, 0.999, 1e-08, 0.01, 10

TOKEN_TILE = 256
ATT_TILE = 512
LRU_TILE = 256
LRU_BWD_TILE = 512
VMEM_SMALL = 32 * 1024 * 1024
VMEM_LARGE = 56 * 1024 * 1024


def _params(sem, vmem=VMEM_SMALL):
    return pltpu.CompilerParams(dimension_semantics=sem, vmem_limit_bytes=vmem)


def _const(shape):
    nd = len(shape)
    return pl.BlockSpec(shape, lambda *_: (0,) * nd)


def _sigmoid(x):
    return 1.0 / (1.0 + jnp.exp(-x))


def _nt(a, b):
    return lax.dot_general(a, b, (((1,), (1,)), ((), ())), preferred_element_type=F32)


def _tn(a, b):
    return lax.dot_general(a, b, (((0,), (0,)), ((), ())), preferred_element_type=F32)


def _half_sums(t, lo):
    s_lo = jnp.sum(jnp.where(lo, t, 0.0), axis=-1, keepdims=True)
    s_hi = jnp.sum(jnp.where(lo, 0.0, t), axis=-1, keepdims=True)
    return jnp.where(lo, s_lo, s_hi)


def _lo_mask():
    return lax.broadcasted_iota(jnp.int32, (1, PAIR), 1) < HEAD_DIM


def _other_chips(x, y):
    return [(1 - x, y), (x, 1 - y), (1 - x, 1 - y)]


def _chip_copies(ins, outs, send_sems, recv_sems, loc_sems, scatter):
    x, y, c = lax.axis_index("x"), lax.axis_index("y"), lax.axis_index("c")
    me = 2 * x + y
    copies = []
    for w in range(len(ins)):
        copies.append(pltpu.make_async_copy(ins[w].at[me] if scatter else ins[w], outs[w].at[me], loc_sems.at[w]))
        for k, (cx, cy) in enumerate(_other_chips(x, y)):
            copies.append(pltpu.make_async_remote_copy(
                src_ref=ins[w].at[2 * cx + cy] if scatter else ins[w], dst_ref=outs[w].at[me],
                send_sem=send_sems.at[3 * w + k], recv_sem=recv_sems.at[3 * w + k],
                device_id=(cx, cy, c), device_id_type=MESH))
    return copies


def _chip_sems(n):
    return [pltpu.SemaphoreType.DMA((3 * n,)), pltpu.SemaphoreType.DMA((3 * n,)), pltpu.SemaphoreType.DMA((n,))]


def _norm1_and_gather(x2d, g1, shard, small):
    T = x2d.shape[0]
    tm = 2 * TOKEN_TILE
    steps = T // tm
    half = shard.shape[1] // 2

    def body(x_ref, g1_ref, w_ref, s_ref, h_ref, ow_ref, os_ref, ici_send, ici_recv, d2d_send, d2d_recv, sm_send,
             sm_recv, loc_sems):
        i = pl.program_id(0)
        x, y, c = lax.axis_index("x"), lax.axis_index("y"), lax.axis_index("c")
        me = 2 * x + y
        mine = pl.ds(pl.multiple_of(c * half, half), half)
        local = [pltpu.make_async_copy(w_ref, ow_ref.at[me], loc_sems.at[0]),
                 pltpu.make_async_copy(s_ref, os_ref.at[me], loc_sems.at[1])]
        fetch, little, forward = [], [], []
        for k, (cx, cy) in enumerate(_other_chips(x, y)):
            src_chip = 2 * cx + cy
            fetch.append(pltpu.make_async_remote_copy(
                src_ref=w_ref.at[:, mine], dst_ref=ow_ref.at[me, :, mine], send_sem=ici_send.at[k],
                recv_sem=ici_recv.at[k], device_id=(cx, cy, c), device_id_type=MESH))
            little.append(pltpu.make_async_remote_copy(
                src_ref=s_ref, dst_ref=os_ref.at[me], send_sem=sm_send.at[k], recv_sem=sm_recv.at[k],
                device_id=(cx, cy, c), device_id_type=MESH))
            forward.append(pltpu.make_async_remote_copy(
                src_ref=ow_ref.at[src_chip, :, mine], dst_ref=ow_ref.at[src_chip, :, mine], send_sem=d2d_send.at[k],
                recv_sem=d2d_recv.at[k], device_id=(x, y, 1 - c), device_id_type=MESH))

        @pl.when(i == 0)
        def _():
            for cp in local + fetch + little:
                cp.start()

        xv = x_ref[...]
        r = lax.rsqrt(jnp.mean(xv * xv, axis=-1, keepdims=True) + NORM_EPS)
        h_ref[...] = (xv * r * g1_ref[...]).astype(MXU_DTYPE)

        @pl.when(i == steps - 1)
        def _():
            for k in range(3):
                fetch[k].wait_recv()
                forward[k].start()
            for cp in fetch:
                cp.wait_send()
            for cp in little + forward + local:
                cp.wait()

    hbm = pl.BlockSpec(memory_space=pl.ANY)
    return pl.pallas_call(
        body, name="norm1_gather", grid=(steps,),
        out_shape=[jax.ShapeDtypeStruct((T, D_MODEL), MXU_DTYPE),
                   jax.ShapeDtypeStruct((4,) + shard.shape, shard.dtype),
                   jax.ShapeDtypeStruct((4,) + small.shape, small.dtype)],
        in_specs=[pl.BlockSpec((tm, D_MODEL), lambda i: (i, 0)), _const((1, D_MODEL)), hbm, hbm],
        out_specs=[pl.BlockSpec((tm, D_MODEL), lambda i: (i, 0)), hbm, hbm],
        scratch_shapes=[pltpu.SemaphoreType.DMA((3,))] * 6 + [pltpu.SemaphoreType.DMA((2,))],
        compiler_params=_params(("arbitrary",)),
    )(x2d, g1, shard, small)


def _device_copies(packs_in, packs_out, psend, precv, loc_sems, loc_base):
    x, y, c = lax.axis_index("x"), lax.axis_index("y"), lax.axis_index("c")
    dev = 4 * x + 2 * y + c
    copies = []
    for j in range(len(packs_in)):
        copies.append(pltpu.make_async_copy(packs_in[j], packs_out[j].at[dev], loc_sems.at[loc_base + j]))
        for k in range(1, 8):
            fx, fy, fc = (k >> 2) & 1, (k >> 1) & 1, k & 1
            tx = (1 - x) if fx else x
            ty = (1 - y) if fy else y
            tc = (1 - c) if fc else c
            copies.append(pltpu.make_async_remote_copy(
                src_ref=packs_in[j], dst_ref=packs_out[j].at[dev],
                send_sem=psend.at[7 * j + k - 1], recv_sem=precv.at[7 * j + k - 1],
                device_id=(tx, ty, tc), device_id_type=MESH))
    return copies


def _swap_with_sibling(arrs, packs):
    n, npk = len(arrs), len(packs)

    def body(*refs):
        ins, pack_in = refs[:n], refs[n:n + npk]
        outs, pack_out = refs[n + npk:2 * n + npk], refs[2 * n + npk:2 * (n + npk)]
        send_sems, recv_sems, loc_sems, psend, precv = refs[2 * (n + npk):]
        x, y, c = lax.axis_index("x"), lax.axis_index("y"), lax.axis_index("c")
        copies = [pltpu.make_async_remote_copy(
            src_ref=ins[w], dst_ref=outs[w], send_sem=send_sems.at[w], recv_sem=recv_sems.at[w],
            device_id=(x, y, 1 - c), device_id_type=MESH) for w in range(n)]
        copies += _device_copies(pack_in, pack_out, psend, precv, loc_sems, 0)
        for cp in copies:
            cp.start()
        for cp in copies:
            cp.wait()

    return pl.pallas_call(
        body, name="swap_sibling",
        out_shape=[jax.ShapeDtypeStruct(a.shape, a.dtype) for a in arrs]
        + [jax.ShapeDtypeStruct((8,) + p.shape, p.dtype) for p in packs],
        in_specs=[pl.BlockSpec(memory_space=pl.ANY)] * (n + npk),
        out_specs=[pl.BlockSpec(memory_space=pl.ANY)] * (n + npk),
        scratch_shapes=[pltpu.SemaphoreType.DMA((n,)), pltpu.SemaphoreType.DMA((n,)), pltpu.SemaphoreType.DMA((npk,)),
                        pltpu.SemaphoreType.DMA((7 * npk,)), pltpu.SemaphoreType.DMA((7 * npk,))],
    )(*arrs, *packs)


def _head_norm(t, g2, lo):
    rr = lax.rsqrt(_half_sums(t * t, lo) * (1.0 / HEAD_DIM) + NORM_EPS)
    return t * rr * g2


def _inproj(h1, wcat, gq2, gk2):
    T = h1.shape[0]
    tm = TOKEN_TILE

    def body(h_ref, w_ref, gq_ref, gk_ref, qkv_ref, qn_ref, qt_ref, kn_ref, vb_ref, f_ref, lx_ref, lg_ref):
        proj = _nt(h_ref[...], w_ref[...])
        qkv_ref[...] = proj[:, :2 * ATT_WIDTH]
        lo = _lo_mask()
        for p in range(N_PAIR):
            cols = slice(PAIR * p, PAIR * (p + 1))
            q = proj[:, PAIR * p:PAIR * (p + 1)]
            k = proj[:, ATT_WIDTH + PAIR * p:ATT_WIDTH + PAIR * (p + 1)]
            qs = _head_norm(q, gq_ref[...], lo) * QK_SCALE
            qn_ref[:, cols] = qs.astype(MXU_DTYPE)
            qt_ref[cols, :] = qs.T.astype(MXU_DTYPE)
            kn_ref[:, cols] = _head_norm(k, gk_ref[...], lo).astype(MXU_DTYPE)
        vb_ref[...] = proj[:, 2 * ATT_WIDTH:3 * ATT_WIDTH].astype(MXU_DTYPE)
        f0 = 3 * ATT_WIDTH
        f_ref[...] = proj[:, f0:f0 + F_PAD]
        lx_ref[...] = proj[:, f0 + F_PAD:f0 + F_PAD + LRU_WIDTH]
        lg_ref[...] = proj[:, f0 + F_PAD + LRU_WIDTH:]

    row = lambda w: pl.BlockSpec((tm, w), lambda i: (i, 0))
    return pl.pallas_call(
        body, name="inproj", grid=(T // tm,),
        in_specs=[row(D_MODEL), _const((N_CAT, D_MODEL)), _const((1, PAIR)), _const((1, PAIR))],
        out_specs=[row(2 * ATT_WIDTH), row(ATT_WIDTH), pl.BlockSpec((ATT_WIDTH, tm), lambda i: (0, i)), row(ATT_WIDTH),
                   row(ATT_WIDTH), row(F_PAD), row(LRU_WIDTH), row(LRU_WIDTH)],
        out_shape=[jax.ShapeDtypeStruct((T, 2 * ATT_WIDTH), F32),
                   jax.ShapeDtypeStruct((T, ATT_WIDTH), MXU_DTYPE), jax.ShapeDtypeStruct((ATT_WIDTH, T), MXU_DTYPE),
                   jax.ShapeDtypeStruct((T, ATT_WIDTH), MXU_DTYPE),
                   jax.ShapeDtypeStruct((T, ATT_WIDTH), MXU_DTYPE), jax.ShapeDtypeStruct((T, F_PAD), F32),
                   jax.ShapeDtypeStruct((T, LRU_WIDTH), F32), jax.ShapeDtypeStruct((T, LRU_WIDTH), F32)],
        compiler_params=_params(("parallel",), VMEM_LARGE),
    )(h1, wcat, gq2, gk2)


def _forget_cumsum(f2d, bf, bl, seq):
    def body(z_ref, b_ref, o_ref):
        z = z_ref[...] + b_ref[...]
        lf = jnp.minimum(z, 0.0) - jnp.log(1.0 + jnp.exp(-jnp.abs(z)))
        row = lax.broadcasted_iota(jnp.int32, (seq, F_PAD), 0)
        k = 1
        while k < seq:
            lf = lf + jnp.where(row >= k, pltpu.roll(lf, k, 0), 0.0)
            k *= 2
        o_ref[...] = lf

    return pl.pallas_call(
        body, name="forget_cumsum", grid=(bl,),
        in_specs=[pl.BlockSpec((seq, F_PAD), lambda b: (b, 0)), _const((1, F_PAD))],
        out_specs=pl.BlockSpec((seq, F_PAD), lambda b: (b, 0)),
        out_shape=jax.ShapeDtypeStruct(f2d.shape, F32),
        compiler_params=_params(("parallel",)),
    )(f2d, bf)


def _attn_fwd(qn, kn, vb, frow, fstart, bl, seq, shards):
    tq = min(ATT_TILE, seq)
    nq = seq // tq
    T = bl * seq
    n = len(shards)

    def body(fs_ref, q_ref, k_ref, v_ref, fr_ref, *rest):
        g_in, (o_ref, lse_ref), g_out, sems = rest[:n], rest[n:n + 2], rest[n + 2:2 * n + 2], rest[2 * n + 2:]
        b, p, i = pl.program_id(0), pl.program_id(1), pl.program_id(2)
        copies = _chip_copies(g_in, g_out, *sems, scatter=False)

        @pl.when((b == 0) & (p == 0) & (i == 0))
        def _():
            for cp in copies:
                cp.start()

        lane = lax.broadcasted_iota(jnp.int32, (1, PAIR), 1)
        rows = lax.broadcasted_iota(jnp.int32, (tq, tq), 0)
        cols = lax.broadcasted_iota(jnp.int32, (tq, tq), 1)
        causal = cols <= rows
        q = q_ref[...]
        hms = [(lane >= HEAD_DIM * hh) & (lane < HEAD_DIM * (hh + 1)) for hh in range(2)]
        qhs = [jnp.where(hm, q, jnp.zeros_like(q)) for hm in hms]
        shifts = [fs_ref[((b * N_PAIR + p) * 2 + hh) * nq + i] for hh in range(2)]
        sum_lane = [HEAD_DIM * (1 - hh) for hh in range(2)]

        def block(j, carry, masked):
            start = pl.multiple_of(j * tq, tq)
            k = k_ref[pl.ds(start, tq), :]
            v = v_ref[pl.ds(start, tq), :]
            new = []
            for hh in range(2):
                m, acc = carry[hh]
                s = lax.dot_general(qhs[hh], k, (((1,), (1,)), ((), ())), preferred_element_type=F32)
                s = s - (fr_ref[0, 0, hh:hh + 1, pl.ds(start, tq)] - shifts[hh])
                if masked:
                    s = jnp.where(causal, s, NEG)
                m_new = jnp.maximum(m, jnp.max(s, axis=-1, keepdims=True))
                alpha = jnp.exp(m - m_new)
                pe = jnp.exp(s - m_new)
                vh = jnp.where(hms[hh], v, jnp.where(lane == sum_lane[hh], 1.0, 0.0).astype(v.dtype))
                pb = pe.astype(MXU_DTYPE)
                p_lo = (pe - pb.astype(F32)).astype(MXU_DTYPE)
                acc = (alpha * acc + jnp.dot(pb, vh, preferred_element_type=F32)
                       + jnp.dot(p_lo, vh, preferred_element_type=F32))
                new.append((m_new, acc))
            return tuple(new)

        init = (jnp.full((tq, 1), NEG, F32), jnp.zeros((tq, PAIR), F32))
        carry = lax.fori_loop(0, i, functools.partial(block, masked=False), (init, init))
        carry = block(i, carry, True)
        out = jnp.zeros((tq, PAIR), F32)
        lse = jnp.zeros((tq, PAIR), F32)
        for hh in range(2):
            m, acc = carry[hh]
            l = acc[:, sum_lane[hh]:sum_lane[hh] + 1]
            out = jnp.where(hms[hh], acc * (1.0 / l), out)
            lse = jnp.where(hms[hh], m + jnp.log(l), lse)
        o_ref[...] = out
        lse_ref[...] = lse

        @pl.when((b == bl - 1) & (p == N_PAIR - 1) & (i == nq - 1))
        def _():
            for cp in copies:
                cp.wait()

    blk = pl.BlockSpec((tq, PAIR), lambda b, p, i: (b * nq + i, p))
    full = pl.BlockSpec((seq, PAIR), lambda b, p, i: (b, p))
    return pl.pallas_call(
        body, name="attn_fwd", grid=(bl, N_PAIR, nq),
        in_specs=[pl.BlockSpec(memory_space=pltpu.SMEM), blk, full, full,
                  pl.BlockSpec((1, 1, 2, seq), lambda b, p, i: (b, p, 0, 0))] + [pl.BlockSpec(memory_space=pl.ANY)] * n,
        out_specs=[blk, blk] + [pl.BlockSpec(memory_space=pl.ANY)] * n,
        out_shape=[jax.ShapeDtypeStruct((T, ATT_WIDTH), F32)] * 2
        + [jax.ShapeDtypeStruct((4,) + s.shape, s.dtype) for s in shards],
        scratch_shapes=_chip_sems(n),
        compiler_params=_params(("arbitrary", "arbitrary", "arbitrary")),
    )(fstart, qn, kn, vb, frow, *shards)


def _conv_taps(lx, prev8, cw, cb):
    xs = jnp.concatenate([prev8, lx], axis=0)
    shifted = [lx] + [pltpu.roll(xs, k, 0)[8:] for k in range(1, CONV_WIDTH)]
    xc = cb + cw[CONV_WIDTH - 1:CONV_WIDTH] * lx
    for k in range(1, CONV_WIDTH):
        xc = xc + cw[CONV_WIDTH - 1 - k:CONV_WIDTH - k] * shifted[k]
    return xc, shifted


def _lru_gates(xc, wa, ba, wx, bx, lam):
    xb = xc.astype(MXU_DTYPE)
    r = _sigmoid(jnp.dot(xb, wa, preferred_element_type=F32) + ba)
    ig = _sigmoid(jnp.dot(xb, wx, preferred_element_type=F32) + bx)
    sp = jnp.maximum(-lam, 0.0) + jnp.log(1.0 + jnp.exp(-jnp.abs(lam)))
    log_a = -LRU_C * r * sp
    a = jnp.exp(log_a)
    th = jnp.tanh(log_a)
    mult = jnp.sqrt(-2.0 * th / (1.0 - th))
    return r, ig, sp, a, mult


def _gelu_parts(x):
    c0 = math.sqrt(2.0 / math.pi)
    t = jnp.tanh(c0 * (x + 0.044715 * x * x * x))
    g = 0.5 * x * (1.0 + t)
    dg = 0.5 * (1.0 + t) + 0.5 * x * (1.0 - t * t) * c0 * (1.0 + 3.0 * 0.044715 * x * x)
    return g, dg


def _lru_fwd(lx, lg, cw, cb, wa, ba, wx, bx, lam, bl, seq):
    tc = min(LRU_TILE, seq)
    nc = seq // tc
    T = bl * seq

    def body(lx_ref, lxp_ref, lg_ref, cw_ref, cb_ref, wa_ref, ba_ref, wx_ref, bx_ref, lam_ref,
             h_ref, rec_ref, hc_ref):
        i = pl.program_id(1)

        @pl.when(i == 0)
        def _():
            hc_ref[...] = jnp.zeros_like(hc_ref)

        lxv = lx_ref[...]
        prev8 = jnp.where(i > 0, lxp_ref[...], 0.0)
        xc, _ = _conv_taps(lxv, prev8, cw_ref[...], cb_ref[...])
        _, ig, _, a, mult = _lru_gates(xc, wa_ref[...], ba_ref[...], wx_ref[...], bx_ref[...], lam_ref[...])
        u = mult * (ig * xc)
        sub = lax.broadcasted_iota(jnp.int32, (tc, LRU_WIDTH), 0) & 7
        A, B = a, u
        for k in (1, 2, 4):
            a_s = jnp.where(sub >= k, pltpu.roll(A, k, 0), 1.0)
            b_s = jnp.where(sub >= k, pltpu.roll(B, k, 0), 0.0)
            B = A * b_s + B
            A = A * a_s
        carry = hc_ref[0:1, :]
        groups = []
        for g in range(tc // 8):
            hg = A[8 * g:8 * (g + 1)] * carry + B[8 * g:8 * (g + 1)]
            groups.append(hg)
            carry = hg[7:8]
        h = jnp.concatenate(groups, axis=0)
        hc_ref[0:1, :] = carry
        h_ref[...] = h
        g, _ = _gelu_parts(lg_ref[...])
        rec_ref[...] = h * g

    tile = pl.BlockSpec((tc, LRU_WIDTH), lambda b, i: (b * nc + i, 0))
    prev = pl.BlockSpec((8, LRU_WIDTH), lambda b, i: (jnp.maximum((b * seq + i * tc) // 8 - 1, 0), 0))
    vec = _const((1, LRU_WIDTH))
    mat = _const((LRU_WIDTH, LRU_WIDTH))
    return pl.pallas_call(
        body, name="lru_fwd", grid=(bl, nc),
        in_specs=[tile, prev, tile, _const((CONV_WIDTH, LRU_WIDTH)), vec, mat, vec, mat, vec, vec],
        out_specs=[tile, tile],
        out_shape=[jax.ShapeDtypeStruct((T, LRU_WIDTH), F32), jax.ShapeDtypeStruct((T, LRU_WIDTH), F32)],
        scratch_shapes=[pltpu.VMEM((8, LRU_WIDTH), F32)],
        compiler_params=_params(("arbitrary", "arbitrary")),
    )(lx, lx, lg, cw, cb, wa, ba, wx, bx, lam)


def _outproj(x2d, att, rec, ga, gr, wout):
    T = x2d.shape[0]
    tm = TOKEN_TILE

    def body(x_ref, a_ref, r_ref, ga_ref, gr_ref, w_ref, o_ref):
        a = a_ref[...]
        rc = r_ref[...]
        na = a * lax.rsqrt(jnp.mean(a * a, axis=-1, keepdims=True) + NORM_EPS) * ga_ref[...]
        nr = rc * lax.rsqrt(jnp.mean(rc * rc, axis=-1, keepdims=True) + NORM_EPS) * gr_ref[...]
        o_ref[...] = (x_ref[...]
                      + jnp.dot(na.astype(MXU_DTYPE), w_ref[:ATT_WIDTH, :], preferred_element_type=F32)
                      + jnp.dot(nr.astype(MXU_DTYPE), w_ref[ATT_WIDTH:, :], preferred_element_type=F32))

    row = lambda w: pl.BlockSpec((tm, w), lambda i: (i, 0))
    return pl.pallas_call(
        body, name="outproj", grid=(T // tm,),
        in_specs=[row(D_MODEL), row(ATT_WIDTH), row(LRU_WIDTH), _const((1, ATT_WIDTH)), _const((1, LRU_WIDTH)),
                  _const((D_MODEL, D_MODEL))],
        out_specs=row(D_MODEL),
        out_shape=jax.ShapeDtypeStruct((T, D_MODEL), F32),
        compiler_params=_params(("parallel",)),
    )(x2d, att, rec, ga, gr, wout)


def _mlp_fwd(x2, g2, wg, wu, wd, target):
    T = x2.shape[0]
    tm = TOKEN_TILE
    dff = wg.shape[0]

    def body(x_ref, g_ref, wg_ref, wu_ref, wd_ref, t_ref, gt_ref, up_ref, dy_ref, loss_ref):
        @pl.when(pl.program_id(0) == 0)
        def _():
            loss_ref[...] = jnp.zeros_like(loss_ref)

        x = x_ref[...]
        r = lax.rsqrt(jnp.mean(x * x, axis=-1, keepdims=True) + NORM_EPS)
        h = (x * r * g_ref[...]).astype(MXU_DTYPE)
        gt = _nt(h, wg_ref[...])
        up = _nt(h, wu_ref[...])
        gt_ref[...] = gt
        up_ref[...] = up
        act = (gt * _sigmoid(gt) * up).astype(MXU_DTYPE)
        y = x + jnp.dot(act, wd_ref[...], preferred_element_type=F32)
        e = y - t_ref[...]
        dy_ref[...] = e * (1.0 / D_MODEL)
        loss_ref[...] += jnp.sum(e * e)

    row = lambda w: pl.BlockSpec((tm, w), lambda i: (i, 0))
    return pl.pallas_call(
        body, name="mlp_fwd", grid=(T // tm,),
        in_specs=[row(D_MODEL), _const((1, D_MODEL)), _const((dff, D_MODEL)), _const((dff, D_MODEL)),
                  _const((dff, D_MODEL)), row(D_MODEL)],
        out_specs=[row(dff), row(dff), row(D_MODEL), _const((8, 128))],
        out_shape=[jax.ShapeDtypeStruct((T, dff), F32), jax.ShapeDtypeStruct((T, dff), F32),
                   jax.ShapeDtypeStruct((T, D_MODEL), F32), jax.ShapeDtypeStruct((8, 128), F32)],
        compiler_params=_params(("arbitrary",), VMEM_LARGE),
    )(x2, g2, wg, wu, wd, target)


def _mlp_bwd(dy, x2, gt, up, g2, wg, wu, wd):
    T = x2.shape[0]
    tm = TOKEN_TILE
    dff = wg.shape[0]

    def body(dy_ref, x_ref, gt_ref, up_ref, g_ref, wg_ref, wu_ref, wd_ref,
             dx_ref, dxb_ref, dgt_ref, dup_ref, act_ref, h_ref, dyb_ref, dg_ref):
        @pl.when(pl.program_id(0) == 0)
        def _():
            dg_ref[...] = jnp.zeros_like(dg_ref)

        dy_v = dy_ref[...]
        dyb = dy_v.astype(MXU_DTYPE)
        dyb_ref[...] = dyb
        x = x_ref[...]
        r = lax.rsqrt(jnp.mean(x * x, axis=-1, keepdims=True) + NORM_EPS)
        xh = x * r
        h_ref[...] = (xh * g_ref[...]).astype(MXU_DTYPE)
        gt_v = gt_ref[...]
        up_v = up_ref[...]
        sg = _sigmoid(gt_v)
        silu = gt_v * sg
        act_ref[...] = (silu * up_v).astype(MXU_DTYPE)
        dact = _nt(dyb, wd_ref[...])
        dup = (dact * silu).astype(MXU_DTYPE)
        dgt = (dact * up_v * (sg * (1.0 + gt_v * (1.0 - sg)))).astype(MXU_DTYPE)
        dup_ref[...] = dup
        dgt_ref[...] = dgt
        dh = (jnp.dot(dgt, wg_ref[...], preferred_element_type=F32)
              + jnp.dot(dup, wu_ref[...], preferred_element_type=F32))
        dg_ref[...] += jnp.sum(dh * xh, axis=0, keepdims=True)
        dxh = dh * g_ref[...]
        dx = dy_v + r * (dxh - xh * jnp.mean(dxh * xh, axis=-1, keepdims=True))
        dx_ref[...] = dx
        dxb_ref[...] = dx.astype(MXU_DTYPE)

    row = lambda w: pl.BlockSpec((tm, w), lambda i: (i, 0))
    return pl.pallas_call(
        body, name="mlp_bwd", grid=(T // tm,),
        in_specs=[row(D_MODEL), row(D_MODEL), row(dff), row(dff), _const((1, D_MODEL)),
                  _const((dff, D_MODEL)), _const((dff, D_MODEL)), _const((dff, D_MODEL))],
        out_specs=[row(D_MODEL), row(D_MODEL), row(dff), row(dff), row(dff), row(D_MODEL), row(D_MODEL),
                   _const((1, D_MODEL))],
        out_shape=[jax.ShapeDtypeStruct((T, D_MODEL), F32), jax.ShapeDtypeStruct((T, D_MODEL), MXU_DTYPE),
                   jax.ShapeDtypeStruct((T, dff), MXU_DTYPE), jax.ShapeDtypeStruct((T, dff), MXU_DTYPE),
                   jax.ShapeDtypeStruct((T, dff), MXU_DTYPE), jax.ShapeDtypeStruct((T, D_MODEL), MXU_DTYPE),
                   jax.ShapeDtypeStruct((T, D_MODEL), MXU_DTYPE), jax.ShapeDtypeStruct((1, D_MODEL), F32)],
        compiler_params=_params(("arbitrary",), VMEM_LARGE),
    )(dy, x2, gt, up, g2, wg, wu, wd)


def _matmul_tn(a, b, tn, name):
    T, K = a.shape
    N = b.shape[1]
    tt = min(1024, T)

    def body(a_ref, b_ref, o_ref):
        @pl.when(pl.program_id(1) == 0)
        def _():
            o_ref[...] = jnp.zeros_like(o_ref)

        o_ref[...] += _tn(a_ref[...], b_ref[...])

    return pl.pallas_call(
        body, name=name, grid=(N // tn, T // tt),
        in_specs=[pl.BlockSpec((tt, K), lambda n, t: (t, 0)), pl.BlockSpec((tt, tn), lambda n, t: (t, n))],
        out_specs=pl.BlockSpec((K, tn), lambda n, t: (0, n)),
        out_shape=jax.ShapeDtypeStruct((K, N), F32),
        compiler_params=_params(("parallel", "arbitrary"), VMEM_LARGE),
    )(a, b)


def _outproj_bwd(dx2b, att, rec, ga, gr, wout):
    T = att.shape[0]
    tm = TOKEN_TILE

    def body(dx_ref, a_ref, r_ref, ga_ref, gr_ref, w_ref, datt_ref, dattt_ref, delta_ref, drec_ref, mix_ref, dga_ref,
             dgr_ref):
        @pl.when(pl.program_id(0) == 0)
        def _():
            dga_ref[...] = jnp.zeros_like(dga_ref)
            dgr_ref[...] = jnp.zeros_like(dgr_ref)

        dmix = _nt(dx_ref[...], w_ref[...])

        def norm_bwd(v, g, dn):
            rr = lax.rsqrt(jnp.mean(v * v, axis=-1, keepdims=True) + NORM_EPS)
            vh = v * rr
            dvh = dn * g
            dv = rr * (dvh - vh * jnp.mean(dvh * vh, axis=-1, keepdims=True))
            return vh, dv, jnp.sum(dn * vh, axis=0, keepdims=True)

        a = a_ref[...]
        ah, datt, dga = norm_bwd(a, ga_ref[...], dmix[:, :ATT_WIDTH])
        rh, drec, dgr = norm_bwd(r_ref[...], gr_ref[...], dmix[:, ATT_WIDTH:])
        dga_ref[...] += dga
        dgr_ref[...] += dgr
        mix_ref[:, :ATT_WIDTH] = (ah * ga_ref[...]).astype(MXU_DTYPE)
        mix_ref[:, ATT_WIDTH:] = (rh * gr_ref[...]).astype(MXU_DTYPE)
        dattb = datt.astype(MXU_DTYPE)
        datt_ref[...] = dattb
        dattt_ref[...] = datt.T.astype(MXU_DTYPE)
        drec_ref[...] = drec
        lo = _lo_mask()
        prod = dattb.astype(F32) * a
        for p in range(N_PAIR):
            delta_ref[:, PAIR * p:PAIR * (p + 1)] = _half_sums(prod[:, PAIR * p:PAIR * (p + 1)], lo)

    row = lambda w: pl.BlockSpec((tm, w), lambda i: (i, 0))
    return pl.pallas_call(
        body, name="outproj_bwd", grid=(T // tm,),
        in_specs=[row(D_MODEL), row(ATT_WIDTH), row(LRU_WIDTH), _const((1, ATT_WIDTH)), _const((1, LRU_WIDTH)),
                  _const((D_MODEL, D_MODEL))],
        out_specs=[row(ATT_WIDTH), pl.BlockSpec((ATT_WIDTH, tm), lambda i: (0, i)), row(ATT_WIDTH), row(LRU_WIDTH),
                   row(D_MODEL), _const((1, ATT_WIDTH)), _const((1, LRU_WIDTH))],
        out_shape=[jax.ShapeDtypeStruct((T, ATT_WIDTH), MXU_DTYPE), jax.ShapeDtypeStruct((ATT_WIDTH, T), MXU_DTYPE),
                   jax.ShapeDtypeStruct((T, ATT_WIDTH), F32),
                   jax.ShapeDtypeStruct((T, LRU_WIDTH), F32), jax.ShapeDtypeStruct((T, D_MODEL), MXU_DTYPE),
                   jax.ShapeDtypeStruct((1, ATT_WIDTH), F32), jax.ShapeDtypeStruct((1, LRU_WIDTH), F32)],
        compiler_params=_params(("arbitrary",)),
    )(dx2b, att, rec, ga, gr, wout)


def _lru_bwd(drec, lg, h, lx, cw, cb, wa, ba, wx, bx, lam, bl, seq):
    tc = min(LRU_BWD_TILE, seq)
    nc = seq // tc
    T = bl * seq
    n = tc

    def body(dr_ref, lg_ref, h_ref, hp_ref, lx_ref, lxp_ref, cw_ref, cb_ref, wa_ref, ba_ref, wx_ref, bx_ref, lam_ref,
             dlx_ref, dlg_ref, dwa_ref, dwx_ref, small_ref, gc_ref, dxn_ref):
        b, i = pl.program_id(0), pl.program_id(1)
        ir = nc - 1 - i

        @pl.when((b == 0) & (i == 0))
        def _():
            dwa_ref[...] = jnp.zeros_like(dwa_ref)
            dwx_ref[...] = jnp.zeros_like(dwx_ref)
            small_ref[...] = jnp.zeros_like(small_ref)

        @pl.when(i == 0)
        def _():
            gc_ref[...] = jnp.zeros_like(gc_ref)
            dxn_ref[...] = jnp.zeros_like(dxn_ref)

        cw = cw_ref[...]
        lam_v = lam_ref[...]
        lxv = lx_ref[...]
        prev8 = jnp.where(ir > 0, lxp_ref[...], 0.0)
        xc, shifted = _conv_taps(lxv, prev8, cw, cb_ref[...])
        r, ig, sp, a, mult = _lru_gates(xc, wa_ref[...], ba_ref[...], wx_ref[...], bx_ref[...], lam_v)
        hv = h_ref[...]
        drv = dr_ref[...]
        g, dg = _gelu_parts(lg_ref[...])
        dlg_ref[...] = drv * hv * dg
        dh = drv * g

        row = lax.broadcasted_iota(jnp.int32, (n, LRU_WIDTH), 0)
        sub = row & 7
        A = jnp.where(row < n - 1, pltpu.roll(a, n - 1, 0), 0.0)
        B = dh + jnp.where(row == n - 1, gc_ref[0:1, :], 0.0)
        for k in (1, 2, 4):
            a_s = jnp.where(sub < 8 - k, pltpu.roll(A, n - k, 0), 1.0)
            b_s = jnp.where(sub < 8 - k, pltpu.roll(B, n - k, 0), 0.0)
            B = B + A * b_s
            A = A * a_s
        carry = jnp.zeros((1, LRU_WIDTH), F32)
        groups = [None] * (n // 8)
        for g in reversed(range(n // 8)):
            gg = B[8 * g:8 * (g + 1)] + A[8 * g:8 * (g + 1)] * carry
            groups[g] = gg
            carry = gg[0:1]
        gs = jnp.concatenate(groups, axis=0)
        gc_ref[0:1, :] = a[0:1, :] * carry

        hprev8 = jnp.where(ir > 0, hp_ref[...], 0.0)
        h_prev = pltpu.roll(jnp.concatenate([hprev8, hv], axis=0), 1, 0)[8:]
        da = gs * h_prev
        ix = ig * xc
        dmult = gs * ix
        dig = gs * mult * xc
        dxc = gs * mult * ig
        dlog_a = da * a - dmult * (a * a) / mult
        dr_gate = dlog_a * (-LRU_C * sp)
        dza = dr_gate * r * (1.0 - r)
        dzx = dig * ig * (1.0 - ig)
        dzab = dza.astype(MXU_DTYPE)
        dzxb = dzx.astype(MXU_DTYPE)
        xcb = xc.astype(MXU_DTYPE)
        dwa_ref[...] += _tn(xcb, dzab)
        dwx_ref[...] += _tn(xcb, dzxb)
        dxc = dxc + _nt(dzab, wa_ref[...]) + _nt(dzxb, wx_ref[...])

        ds = jnp.concatenate([dxc, dxn_ref[...]], axis=0)
        dlx = cw[CONV_WIDTH - 1:CONV_WIDTH] * dxc
        for k in range(1, CONV_WIDTH):
            dlx = dlx + cw[CONV_WIDTH - 1 - k:CONV_WIDTH - k] * pltpu.roll(ds, n + 8 - k, 0)[:n]
        dlx_ref[...] = dlx
        dxn_ref[...] = dxc[0:8, :]

        colsum = lambda v: jnp.sum(v, axis=0, keepdims=True)
        small_ref[0:1, :] += colsum(dza)
        small_ref[1:2, :] += colsum(dzx)
        small_ref[2:3, :] += colsum(dlog_a * r) * (LRU_C * _sigmoid(-lam_v))
        small_ref[3:4, :] += colsum(dxc)
        for k in range(CONV_WIDTH):
            j = CONV_WIDTH - 1 - k
            small_ref[4 + j:5 + j, :] += colsum(dxc * shifted[k])

    tile = pl.BlockSpec((tc, LRU_WIDTH), lambda b, i: (b * nc + (nc - 1 - i), 0))
    prev = pl.BlockSpec((8, LRU_WIDTH), lambda b, i: (jnp.maximum((b * seq + (nc - 1 - i) * tc) // 8 - 1, 0), 0))
    vec = _const((1, LRU_WIDTH))
    mat = _const((LRU_WIDTH, LRU_WIDTH))
    return pl.pallas_call(
        body, name="lru_bwd", grid=(bl, nc),
        in_specs=[tile, tile, tile, prev, tile, prev, _const((CONV_WIDTH, LRU_WIDTH)), vec, mat, vec, mat, vec, vec],
        out_specs=[tile, tile, mat, mat, _const((8, LRU_WIDTH))],
        out_shape=[jax.ShapeDtypeStruct((T, LRU_WIDTH), F32), jax.ShapeDtypeStruct((T, LRU_WIDTH), F32),
                   jax.ShapeDtypeStruct((LRU_WIDTH, LRU_WIDTH), F32), jax.ShapeDtypeStruct((LRU_WIDTH, LRU_WIDTH), F32),
                   jax.ShapeDtypeStruct((8, LRU_WIDTH), F32)],
        scratch_shapes=[pltpu.VMEM((8, LRU_WIDTH), F32), pltpu.VMEM((8, LRU_WIDTH), F32)],
        compiler_params=_params(("arbitrary", "arbitrary")),
    )(drec, lg, h, h, lx, lx, cw, cb, wa, ba, wx, bx, lam)


def _attn_bwd(qn, kn, vb, dob, qt, dot_, lse, delta, frow, fstart, bl, seq, slabs, packs):
    tq = min(ATT_TILE, seq)
    nq = seq // tq
    T = bl * seq
    n, npk = len(slabs), len(packs)
    nx = n + npk

    def body(fs_ref, q_ref, k_ref, v_ref, do_ref, qt_ref, dot_ref, lse_ref, dl_ref, fr_ref, *rest):
        x_in, (dq_ref, dk_ref, dv_ref, df_ref), x_out = rest[:nx], rest[nx:nx + 4], rest[nx + 4:2 * nx + 4]
        send_sems, recv_sems, loc_sems, psend, precv = rest[2 * nx + 4:]
        b, p, j = pl.program_id(0), pl.program_id(1), pl.program_id(2)
        copies = _chip_copies(x_in[:n], x_out[:n], send_sems, recv_sems, loc_sems, scatter=True)
        copies += _device_copies(x_in[n:], x_out[n:], psend, precv, loc_sems, n)

        @pl.when((b == 0) & (p == 0) & (j == 0))
        def _():
            for cp in copies:
                cp.start()

        @pl.when(j == 0)
        def _():
            dq_ref[...] = jnp.zeros_like(dq_ref)

        lane = lax.broadcasted_iota(jnp.int32, (1, PAIR), 1)
        rows = lax.broadcasted_iota(jnp.int32, (tq, tq), 0)
        cols = lax.broadcasted_iota(jnp.int32, (tq, tq), 1)
        causal = cols <= rows
        kv = k_ref[...]
        vv = v_ref[...]
        hms = [(lane >= HEAD_DIM * hh) & (lane < HEAD_DIM * (hh + 1)) for hh in range(2)]
        srow = lax.broadcasted_iota(jnp.int32, (PAIR, 1), 0)
        hms_t = [(srow >= HEAD_DIM * hh) & (srow < HEAD_DIM * (hh + 1)) for hh in range(2)]
        khs = [jnp.where(hm, kv, jnp.zeros_like(kv)) for hm in hms]
        fks = [fr_ref[0, 0, hh:hh + 1, :] for hh in range(2)]
        bases = [((b * N_PAIR + p) * 2 + hh) * nq for hh in range(2)]

        def block(i, carry, masked):
            dk, dv, dfs = carry
            start = pl.multiple_of(i * tq, tq)
            qi = q_ref[pl.ds(start, tq), :]
            doi = do_ref[pl.ds(start, tq), :]
            qti = qt_ref[:, pl.ds(start, tq)]
            doti = dot_ref[:, pl.ds(start, tq)]
            dq = jnp.zeros((tq, PAIR), F32)
            new_dfs = []
            for hh in range(2):
                c0 = HEAD_DIM * hh
                qh = jnp.where(hms[hh], qi, jnp.zeros_like(qi))
                doh = jnp.where(hms[hh], doi, jnp.zeros_like(doi))
                s = _nt(qh, kv) - (fks[hh] - fs_ref[bases[hh] + i])
                if masked:
                    s = jnp.where(causal, s, NEG)
                pr = jnp.exp(s - lse_ref[pl.ds(start, tq), c0:c0 + 1])
                dp = _nt(doh, vv)
                ds = pr * (dp - dl_ref[pl.ds(start, tq), c0:c0 + 1])
                dsb = ds.astype(MXU_DTYPE)
                dv = dv + jnp.dot(jnp.where(hms_t[hh], doti, jnp.zeros_like(doti)), pr.astype(MXU_DTYPE),
                                  preferred_element_type=F32)
                dk = dk + jnp.dot(jnp.where(hms_t[hh], qti, jnp.zeros_like(qti)), dsb, preferred_element_type=F32)
                dq = dq + jnp.dot(dsb, khs[hh], preferred_element_type=F32)
                new_dfs.append(dfs[hh] - jnp.sum(ds, axis=0, keepdims=True))
            dq_ref[pl.ds(start, tq), :] += dq
            return dk, dv, tuple(new_dfs)

        zero = jnp.zeros((PAIR, tq), F32)
        carry = block(j, (zero, zero, (jnp.zeros((1, tq), F32), jnp.zeros((1, tq), F32))), True)
        dk, dv, dfs = lax.fori_loop(j + 1, nq, functools.partial(block, masked=False), carry)
        for hh in range(2):
            df_ref[0, 0, hh:hh + 1, :] = dfs[hh]
        dk_ref[...] = dk.T
        dv_ref[...] = dv.T

        @pl.when((b == bl - 1) & (p == N_PAIR - 1) & (j == nq - 1))
        def _():
            for cp in copies:
                cp.wait()

    blk = pl.BlockSpec((tq, PAIR), lambda b, p, j: (b * nq + j, p))
    full = pl.BlockSpec((seq, PAIR), lambda b, p, j: (b, p))
    fblk = pl.BlockSpec((1, 1, 2, tq), lambda b, p, j: (b, p, 0, j))
    full_t = pl.BlockSpec((PAIR, seq), lambda b, p, j: (p, b))
    hbm = pl.BlockSpec(memory_space=pl.ANY)
    return pl.pallas_call(
        body, name="attn_bwd", grid=(bl, N_PAIR, nq),
        in_specs=[pl.BlockSpec(memory_space=pltpu.SMEM), full, blk, blk, full, full_t, full_t, full, full, fblk]
        + [hbm] * nx,
        out_specs=[full, blk, blk, fblk] + [hbm] * nx,
        out_shape=[jax.ShapeDtypeStruct((T, ATT_WIDTH), F32), jax.ShapeDtypeStruct((T, ATT_WIDTH), F32),
                   jax.ShapeDtypeStruct((T, ATT_WIDTH), F32), jax.ShapeDtypeStruct((bl, N_PAIR, 2, seq), F32)]
        + [jax.ShapeDtypeStruct(s.shape, s.dtype) for s in slabs]
        + [jax.ShapeDtypeStruct((8,) + p.shape, p.dtype) for p in packs],
        scratch_shapes=[pltpu.SemaphoreType.DMA((3 * n,)), pltpu.SemaphoreType.DMA((3 * n,)),
                        pltpu.SemaphoreType.DMA((nx,)),
                        pltpu.SemaphoreType.DMA((7 * npk,)), pltpu.SemaphoreType.DMA((7 * npk,))],
        compiler_params=_params(("arbitrary", "arbitrary", "arbitrary"), VMEM_LARGE),
    )(fstart, qn, kn, vb, dob, qt, dot_, lse, delta, frow, *slabs, *packs)


def _forget_bwd(dfcol, f2d, bf, bl, seq):
    def body(d_ref, z_ref, b_ref, o_ref, db_ref):
        @pl.when(pl.program_id(0) == 0)
        def _():
            db_ref[...] = jnp.zeros_like(db_ref)

        d = d_ref[...]
        row = lax.broadcasted_iota(jnp.int32, (seq, F_PAD), 0)
        k = 1
        while k < seq:
            d = d + jnp.where(row < seq - k, pltpu.roll(d, seq - k, 0), 0.0)
            k *= 2
        dz = d * _sigmoid(-(z_ref[...] + b_ref[...]))
        o_ref[...] = dz
        db_ref[...] += jnp.sum(dz, axis=0, keepdims=True)

    blk = pl.BlockSpec((seq, F_PAD), lambda b: (b, 0))
    return pl.pallas_call(
        body, name="forget_bwd", grid=(bl,),
        in_specs=[blk, blk, _const((1, F_PAD))],
        out_specs=[blk, _const((1, F_PAD))],
        out_shape=[jax.ShapeDtypeStruct(f2d.shape, F32), jax.ShapeDtypeStruct((1, F_PAD), F32)],
        compiler_params=_params(("arbitrary",)),
    )(dfcol, f2d, bf)


def _dproj(dq, dk, dv, qkv, df, dlx, dlg, gq2, gk2):
    T = dq.shape[0]
    tm = TOKEN_TILE

    def body(dq_ref, dk_ref, dv_ref, qkv_ref, df_ref, dlx_ref, dlg_ref, gq_ref, gk_ref, dp_ref, dgq_ref, dgk_ref):
        @pl.when(pl.program_id(0) == 0)
        def _():
            dgq_ref[...] = jnp.zeros_like(dgq_ref)
            dgk_ref[...] = jnp.zeros_like(dgk_ref)

        lo = _lo_mask()

        def head_norm_bwd(t, g2, dy):
            rr = lax.rsqrt(_half_sums(t * t, lo) * (1.0 / HEAD_DIM) + NORM_EPS)
            th = t * rr
            dth = dy * g2
            mm = _half_sums(dth * th, lo) * (1.0 / HEAD_DIM)
            return rr * (dth - th * mm), jnp.sum(dy * th, axis=0, keepdims=True)

        dgq = jnp.zeros((1, PAIR), F32)
        dgk = jnp.zeros((1, PAIR), F32)
        for p in range(N_PAIR):
            cq = slice(PAIR * p, PAIR * (p + 1))
            ck = slice(ATT_WIDTH + PAIR * p, ATT_WIDTH + PAIR * (p + 1))
            dqp, g_ = head_norm_bwd(qkv_ref[:, cq], gq_ref[...], dq_ref[:, cq] * QK_SCALE)
            dgq = dgq + g_
            dp_ref[:, cq] = dqp.astype(MXU_DTYPE)
            dkp, g_ = head_norm_bwd(qkv_ref[:, ck], gk_ref[...], dk_ref[:, cq])
            dgk = dgk + g_
            dp_ref[:, ck] = dkp.astype(MXU_DTYPE)
        dgq_ref[...] += dgq
        dgk_ref[...] += dgk
        f0 = 3 * ATT_WIDTH
        dp_ref[:, 2 * ATT_WIDTH:f0] = dv_ref[...].astype(MXU_DTYPE)
        dp_ref[:, f0:f0 + F_PAD] = df_ref[...].astype(MXU_DTYPE)
        dp_ref[:, f0 + F_PAD:f0 + F_PAD + LRU_WIDTH] = dlx_ref[...].astype(MXU_DTYPE)
        dp_ref[:, f0 + F_PAD + LRU_WIDTH:] = dlg_ref[...].astype(MXU_DTYPE)

    row = lambda w: pl.BlockSpec((tm, w), lambda i: (i, 0))
    return pl.pallas_call(
        body, name="dproj", grid=(T // tm,),
        in_specs=[row(ATT_WIDTH), row(ATT_WIDTH), row(ATT_WIDTH), row(2 * ATT_WIDTH), row(F_PAD), row(LRU_WIDTH),
                  row(LRU_WIDTH), _const((1, PAIR)), _const((1, PAIR))],
        out_specs=[row(N_CAT), _const((1, PAIR)), _const((1, PAIR))],
        out_shape=[jax.ShapeDtypeStruct((T, N_CAT), MXU_DTYPE), jax.ShapeDtypeStruct((1, PAIR), F32),
                   jax.ShapeDtypeStruct((1, PAIR), F32)],
        compiler_params=_params(("arbitrary",)),
    )(dq, dk, dv, qkv, df, dlx, dlg, gq2, gk2)


def _inproj_bwd(dproj, x2d, dx2, g1, wcat, slabs):
    T = x2d.shape[0]
    tm = TOKEN_TILE
    n = len(slabs)
    steps = T // tm

    def body(dp_ref, x_ref, dx2_ref, g1_ref, w_ref, *rest):
        s_in, (gx_ref, dg1_ref), s_out, sems = rest[:n], rest[n:n + 2], rest[n + 2:2 * n + 2], rest[2 * n + 2:]
        i = pl.program_id(0)
        copies = _chip_copies(s_in, s_out, *sems, scatter=True)

        @pl.when(i == 0)
        def _():
            dg1_ref[...] = jnp.zeros_like(dg1_ref)
            for cp in copies:
                cp.start()

        dh = jnp.dot(dp_ref[...], w_ref[...], preferred_element_type=F32)
        x = x_ref[...]
        r = lax.rsqrt(jnp.mean(x * x, axis=-1, keepdims=True) + NORM_EPS)
        xh = x * r
        dg1_ref[...] += jnp.sum(dh * xh, axis=0, keepdims=True)
        dxh = dh * g1_ref[...]
        gx_ref[...] = dx2_ref[...] + r * (dxh - xh * jnp.mean(dxh * xh, axis=-1, keepdims=True))

        @pl.when(i == steps - 1)
        def _():
            for cp in copies:
                cp.wait()

    row = lambda w: pl.BlockSpec((tm, w), lambda i: (i, 0))
    hbm = pl.BlockSpec(memory_space=pl.ANY)
    return pl.pallas_call(
        body, name="inproj_bwd", grid=(steps,),
        in_specs=[row(N_CAT), row(D_MODEL), row(D_MODEL), _const((1, D_MODEL)), _const((N_CAT, D_MODEL))] + [hbm] * n,
        out_specs=[row(D_MODEL), _const((1, D_MODEL))] + [hbm] * n,
        out_shape=[jax.ShapeDtypeStruct((T, D_MODEL), F32), jax.ShapeDtypeStruct((1, D_MODEL), F32)]
        + [jax.ShapeDtypeStruct(a.shape, a.dtype) for a in slabs],
        scratch_shapes=_chip_sems(n),
        compiler_params=_params(("arbitrary",), VMEM_LARGE),
    )(dproj, x2d, dx2, g1, wcat, *slabs)


ELEMENTWISE_COLS = 256


def _sum_slabs(recvs):
    n = len(recvs)
    cols = recvs[0].shape[2]
    cb = ELEMENTWISE_COLS

    def body(*refs):
        for r_ref, o_ref in zip(refs[:n], refs[n:]):
            part = [r_ref[s].astype(F32) for s in range(4)]
            o_ref[...] = ((part[0] + part[1]) + part[2]) + part[3]

    return pl.pallas_call(
        body, name="sum_slabs", grid=(cols // cb,),
        in_specs=[pl.BlockSpec((4, r.shape[1], cb), lambda i: (0, 0, i)) for r in recvs],
        out_specs=[pl.BlockSpec((r.shape[1], cb), lambda i: (0, i)) for r in recvs],
        out_shape=[jax.ShapeDtypeStruct(r.shape[1:], F32) for r in recvs],
        compiler_params=_params(("parallel",), VMEM_LARGE),
    )(*recvs)


def _adamw_math(w, g, m, v):
    m = ADAM_B1 * m + (1.0 - ADAM_B1) * g
    v = ADAM_B2 * v + (1.0 - ADAM_B2) * (g * g)
    m_hat = m / (1.0 - ADAM_B1 ** ADAM_STEP)
    v_hat = v / (1.0 - ADAM_B2 ** ADAM_STEP)
    delta = -ADAM_LR * (m_hat / (jnp.sqrt(v_hat) + ADAM_EPS) + ADAM_WD * w)
    return delta, m, v


def _adamw_pairs(groups):
    n = len(groups)
    cols = groups[0][0].shape[1]
    cb = ELEMENTWISE_COLS // 2

    def body(*refs):
        ins, outs = refs[:5 * n], refs[5 * n:]
        for w in range(n):
            a_ref, b_ref, w_ref, m_ref, v_ref = ins[5 * w:5 * w + 5]
            g_ref, d_ref, nm_ref, nv_ref = outs[4 * w:4 * w + 4]
            g = a_ref[...] + b_ref[...]
            g_ref[...] = g
            d_ref[...], nm_ref[...], nv_ref[...] = _adamw_math(w_ref[...], g, m_ref[...], v_ref[...])

    blk = lambda rows: pl.BlockSpec((rows, cb), lambda i: (0, i))
    res = pl.pallas_call(
        body, name="adamw_big", grid=(cols // cb,),
        in_specs=[blk(g[0].shape[0]) for g in groups for _ in range(5)],
        out_specs=[blk(g[0].shape[0]) for g in groups for _ in range(4)],
        out_shape=[jax.ShapeDtypeStruct(g[0].shape, F32) for g in groups for _ in range(4)],
        compiler_params=_params(("parallel",), VMEM_LARGE),
    )(*[a for g in groups for a in g])
    return [res[4 * w:4 * w + 4] for w in range(n)]


VEC_ROW = {"norm1_g": 0, "norm2_g": 1, "attn_out_g": 2, "lru_out_g": 3, "q_norm_g": 4, "k_norm_g": 5, "b_f": 6,
           "b_a": 8, "b_x": 9, "lam": 10, "conv_b": 11}
LOSS_ROW, CONV_W_ROW, PACK_ROWS = 7, 12, 16
SMALL = list(VEC_ROW) + ["conv_w", "w_a", "w_x"]


def _pack_small(dg1, dg2, dga, dgr, dgq, dgk, dbf, sq_err, lru_small):
    def body(dg1_ref, dg2_ref, dga_ref, dgr_ref, dgq_ref, dgk_ref, dbf_ref, err_ref, lru_ref, v_ref):
        v_ref[...] = jnp.zeros_like(v_ref)
        v_ref[0:1, :] = dg1_ref[...]
        v_ref[1:2, :] = dg2_ref[...]
        v_ref[2:3, 0:ATT_WIDTH] = dga_ref[...]
        v_ref[3:4, 0:LRU_WIDTH] = dgr_ref[...]
        for row, ref in ((4, dgq_ref), (5, dgk_ref)):
            g = ref[...]
            v_ref[row:row + 1, 0:PAIR] = g + pltpu.roll(g, HEAD_DIM, 1)
        v_ref[6:7, 0:F_PAD] = dbf_ref[...]
        v_ref[LOSS_ROW:LOSS_ROW + 1, 0:128] = err_ref[0:1, :] * (0.5 / D_MODEL)
        v_ref[8:16, 0:LRU_WIDTH] = lru_ref[...]

    ins = [dg1, dg2, dga, dgr, dgq, dgk, dbf, sq_err, lru_small]
    return pl.pallas_call(
        body, name="pack_small", grid=(1,),
        in_specs=[_const(a.shape) for a in ins], out_specs=_const((PACK_ROWS, D_MODEL)),
        out_shape=jax.ShapeDtypeStruct((PACK_ROWS, D_MODEL), F32),
        compiler_params=_params(("arbitrary",)),
    )(*ins)


def _diag_blocks(dwa_bd, dwx_bd):
    blk = LRU_WIDTH // LRU_BLOCKS

    def body(wa_ref, wx_ref, oa_ref, ox_ref):
        for src, dst in ((wa_ref, oa_ref), (wx_ref, ox_ref)):
            for nb in range(LRU_BLOCKS):
                tile = src[blk * nb:blk * (nb + 1), PAIR * (nb // 2):PAIR * (nb // 2 + 1)]
                if nb % 2:
                    tile = pltpu.roll(tile, blk, 1)
                dst[nb] = tile[:, 0:blk]

    out = jax.ShapeDtypeStruct((LRU_BLOCKS, blk, blk), F32)
    return pl.pallas_call(
        body, name="diag_blocks", grid=(1,),
        in_specs=[_const(dwa_bd.shape)] * 2, out_specs=[_const(out.shape)] * 2, out_shape=[out, out],
        compiler_params=_params(("arbitrary",)),
    )(dwa_bd, dwx_bd)


def _adamw_small(recv_v, recv_a, recv_x, params):
    names = list(params)
    flat = [a for n in names for a in params[n]]

    def body(rv_ref, ra_ref, rx_ref, *refs):
        ins, loss_ref, outs = refs[:len(flat)], refs[len(flat)], refs[len(flat) + 1:]
        x, y = lax.axis_index("x"), lax.axis_index("y")
        me = 2 * x + y

        def total(r):
            acc = r[0]
            for d in range(1, 8):
                acc = acc + r[d]
            return acc

        gv, ga, gx = total(rv_ref), total(ra_ref), total(rx_ref)
        loss_ref[...] = gv[LOSS_ROW:LOSS_ROW + 1, 0:128]
        for i, n in enumerate(names):
            w_ref, m_ref, v_ref = ins[3 * i:3 * i + 3]
            g_ref, d_ref, nm_ref, nv_ref = outs[4 * i:4 * i + 4]
            if n in VEC_ROW:
                g = gv[VEC_ROW[n]:VEC_ROW[n] + 1, 0:w_ref.shape[1]]
                w, m, v = w_ref[...], m_ref[...], v_ref[...]
            else:
                if n == "conv_w":
                    full = gv[CONV_W_ROW:CONV_W_ROW + CONV_WIDTH, 0:LRU_WIDTH]
                    width = LRU_WIDTH // 4
                    g = jnp.zeros((CONV_WIDTH, width), F32)
                    for s in range(4):
                        g = jnp.where(me == s, full[:, width * s:width * (s + 1)], g)
                else:
                    g = ga if n == "w_a" else gx
                w, m, v = w_ref[0], m_ref[0], v_ref[0]
            d, nm, nv = _adamw_math(w, g, m, v)
            for ref, val in ((g_ref, g), (d_ref, d), (nm_ref, nm), (nv_ref, nv)):
                if n in VEC_ROW:
                    ref[...] = val
                else:
                    ref[0] = val

    out_shape = [jax.ShapeDtypeStruct((1, 128), F32)] + [jax.ShapeDtypeStruct(params[n][0].shape, F32)
                                                          for n in names for _ in range(4)]
    res = pl.pallas_call(
        body, name="adamw_small", grid=(1,),
        in_specs=[_const(a.shape) for a in (recv_v, recv_a, recv_x, *flat)],
        out_specs=[_const(o.shape) for o in out_shape], out_shape=out_shape,
        compiler_params=_params(("arbitrary",)),
    )(recv_v, recv_a, recv_x, *flat)
    return res[0], {n: tuple(res[1 + 4 * i:5 + 4 * i]) for i, n in enumerate(names)}


def _cat_shards(g, pad_at=None, pad=0):
    _, rows, w = g.shape
    pieces = []
    for s in range(4):
        lo, hi = s * w, (s + 1) * w
        if pad_at is not None and lo < pad_at <= hi:
            pieces += [g[s][:, :pad_at - lo], jnp.zeros((rows, pad), g.dtype)]
            if pad_at < hi:
                pieces.append(g[s][:, pad_at - lo:])
        else:
            pieces.append(g[s])
    return jnp.concatenate(pieces, axis=1)


def _block_diag(w):
    eye = jnp.eye(LRU_BLOCKS, dtype=w.dtype)
    return (w[:, :, None, :] * eye[:, None, :, None]).reshape(LRU_WIDTH, LRU_WIDTH)


def kernel(x, norm1_g, w_in, q_norm_g, k_norm_g, b_f, conv_w, conv_b, w_a, b_a, w_x, b_x, lam, attn_out_g, lru_out_g, w_out, norm2_g, w_gate, w_up, w_down, loss_target, m_norm1_g, m_w_in, m_q_norm_g, m_k_norm_g, m_b_f, m_conv_w, m_conv_b, m_w_a, m_b_a, m_w_x, m_b_x, m_lam, m_attn_out_g, m_lru_out_g, m_w_out, m_norm2_g, m_w_gate, m_w_up, m_w_down, v_norm1_g, v_w_in, v_q_norm_g, v_k_norm_g, v_b_f, v_conv_w, v_conv_b, v_w_a, v_b_a, v_w_x, v_b_x, v_lam, v_attn_out_g, v_lru_out_g, v_w_out, v_norm2_g, v_w_gate, v_w_up, v_w_down):
    args = dict(locals())
    bl, seq, _ = x.shape
    T = bl * seq
    tq = min(ATT_TILE, seq)
    nq = seq // tq
    dff = w_gate.shape[2] * 4

    def transposed(name):
        return name.endswith(("w_in", "w_gate", "w_up"))

    def shard2d(name):
        return jnp.swapaxes(args[name], 1, 2)[0] if transposed(name) else args[name][0]

    x2d = x.reshape(T, D_MODEL)
    target2d = loss_target.reshape(T, D_MODEL)
    h1b, g_in, g_cw = _norm1_and_gather(x2d, norm1_g, shard2d("w_in").astype(MXU_DTYPE), conv_w[0])
    later_shards = [shard2d(n).astype(MXU_DTYPE) for n in ("w_out", "w_gate", "w_up", "w_down")]
    f0 = 3 * ATT_WIDTH
    w_in_t = g_in.reshape(-1, D_MODEL)
    wcat = jnp.concatenate([w_in_t[:f0 + HEADS], jnp.zeros((F_PAD - HEADS, D_MODEL), w_in_t.dtype),
                            w_in_t[f0 + HEADS:]], axis=0)
    cw_full = _cat_shards(g_cw)
    wa_bd = _block_diag(w_a[0]).astype(MXU_DTYPE)
    wx_bd = _block_diag(w_x[0]).astype(MXU_DTYPE)
    gq2 = jnp.tile(q_norm_g, (1, 2))
    gk2 = jnp.tile(k_norm_g, (1, 2))
    bf_pad = jnp.pad(b_f, ((0, 0), (0, F_PAD - HEADS)))

    qkv, qn, qn_t, kn, vb, f2d, lx, lg = _inproj(h1b, wcat, gq2, gk2)
    fcol = _forget_cumsum(f2d, bf_pad, bl, seq)
    frow = jnp.transpose(fcol.reshape(bl, seq, F_PAD)[:, :, :HEADS], (0, 2, 1)).reshape(bl, N_PAIR, 2, seq)
    fstart = frow[:, :, :, ::tq].reshape(-1)
    att, lse, g_out, g_gate, g_up, g_down = _attn_fwd(qn, kn, vb, frow, fstart, bl, seq, later_shards)
    wout_full = g_out.reshape(D_MODEL, D_MODEL)
    wg_full, wu_full = g_gate.reshape(dff, D_MODEL), g_up.reshape(dff, D_MODEL)
    wd_full = g_down.reshape(dff, D_MODEL)
    h, rec = _lru_fwd(lx, lg, cw_full, conv_b, wa_bd, b_a, wx_bd, b_x, lam, bl, seq)
    x2 = _outproj(x2d, att, rec, attn_out_g, lru_out_g, wout_full)
    gt, up, dy, sq_err = _mlp_fwd(x2, norm2_g, wg_full, wu_full, wd_full, target2d)

    dx2, dx2b, dgtb, dupb, actb, h2b, dyb, dg2 = _mlp_bwd(dy, x2, gt, up, norm2_g, wg_full, wu_full, wd_full)
    dw_down = _matmul_tn(actb, dyb, D_MODEL, "dw_down")
    dw_gate = _matmul_tn(dgtb, h2b, D_MODEL, "dw_gate")
    dw_up = _matmul_tn(dupb, h2b, D_MODEL, "dw_up")
    dattb, dattb_t, delta, drec, mixb, dga, dgr = _outproj_bwd(dx2b, att, rec, attn_out_g, lru_out_g, wout_full)
    dw_out = _matmul_tn(mixb, dx2b, D_MODEL, "dw_out")
    dlx, dlg, dwa_bd, dwx_bd, lru_small = _lru_bwd(drec, lg, h, lx, cw_full, conv_b, wa_bd, b_a, wx_bd, b_x, lam, bl, seq)
    early_slabs = [dw_out.reshape(4, D_MODEL // 4, D_MODEL), dw_gate.reshape(4, dff // 4, D_MODEL),
                   dw_up.reshape(4, dff // 4, D_MODEL), dw_down.reshape(4, dff // 4, D_MODEL)]
    pack_a, pack_x = _diag_blocks(dwa_bd, dwx_bd)
    dq, dk, dv, dfrow, *recv_early = _attn_bwd(qn, kn, vb, dattb, qn_t, dattb_t, lse, delta, frow, fstart, bl, seq,
                                               early_slabs, [pack_a, pack_x])
    recv_early, (recv_a, recv_x) = recv_early[:4], recv_early[4:]
    dfcol = jnp.pad(jnp.transpose(dfrow.reshape(bl, HEADS, seq), (0, 2, 1)), ((0, 0), (0, 0), (0, F_PAD - HEADS)))
    df, dbf = _forget_bwd(dfcol.reshape(T, F_PAD), f2d, bf_pad, bl, seq)
    dprojb, dgq, dgk = _dproj(dq, dk, dv, qkv, df, dlx, dlg, gq2, gk2)
    dwcat = _matmul_tn(dprojb, h1b, D_MODEL, "dw_in")
    dw_in_slabs = jnp.concatenate([dwcat[:f0 + HEADS], dwcat[f0 + F_PAD:]], axis=0).astype(jnp.bfloat16).reshape(
        4, -1, D_MODEL)
    grad_x, dg1, recv_in = _inproj_bwd(dprojb, x2d, dx2, norm1_g, wcat, [dw_in_slabs])

    pack_v = _pack_small(dg1, dg2, dga, dgr, dgq, dgk, dbf, sq_err, lru_small)
    recv = [recv_in] + recv_early
    big = ["w_in", "w_out", "w_gate", "w_up", "w_down"]
    part = _sum_slabs(recv)
    *theirs, recv_v = _swap_with_sibling(part, [pack_v])
    results = _adamw_pairs([(a, b_, shard2d(n), shard2d("m_" + n), shard2d("v_" + n))
                            for n, a, b_ in zip(big, part, theirs)])
    out = {}
    for n, res in zip(big, results):
        out[n] = tuple(jnp.swapaxes(r[None], 1, 2) if transposed(n) else r[None] for r in res)
    loss_row, small_out = _adamw_small(recv_v, recv_a, recv_x,
                                       {n: (args[n], args["m_" + n], args["v_" + n]) for n in SMALL})
    out.update(small_out)
    loss = loss_row[0, 0]

    order = ["norm1_g", "w_in", "q_norm_g", "k_norm_g", "b_f", "conv_w", "conv_b", "w_a", "b_a", "w_x", "b_x", "lam",
             "attn_out_g", "lru_out_g", "w_out", "norm2_g", "w_gate", "w_up", "w_down"]
    return (loss, grad_x.reshape(bl, seq, D_MODEL), *[out[n][0] for n in order], *[out[n][1] for n in order],
            *[out[n][2] for n in order], *[out[n][3] for n in order])
```
